```python
import jax, jax.numpy as jnp
from jax import lax
import numpy as np

D_MODEL = 2048
BATCH = 16
SEQ = 2048
DEPTH = 2

D_CONV = D_MODEL // 2
CONV_WIDTH = 31
D_POOL = D_MODEL // 2
POOL_WINDOWS = (2, 4, 8, 16)
POOL_GROUP = D_POOL // len(POOL_WINDOWS)
D_SHORT = D_MODEL
SHORT_WIDTH = 3
D_FF = -(-8 * D_MODEL // (3 * 256)) * 256
N_EVEN = (DEPTH + 1) // 2
N_ODD = DEPTH // 2
EPS = 1e-6

kernel_name = 'hybrid_conformer_pool_shortconv_block'


def rms_norm(x, g):
    xf = x.astype(jnp.float32)
    y = xf * lax.rsqrt(jnp.mean(xf * xf, axis=-1, keepdims=True) + EPS)
    return (y * g.astype(jnp.float32)).astype(x.dtype)


def layer_norm(x, g, b):
    xf = x.astype(jnp.float32)
    mu = jnp.mean(xf, axis=-1, keepdims=True)
    xc = xf - mu
    var = jnp.mean(xc * xc, axis=-1, keepdims=True)
    y = xc * lax.rsqrt(var + EPS) * g.astype(jnp.float32) + b.astype(jnp.float32)
    return y.astype(x.dtype)


def causal_depthwise_conv(x, w):
    k, c = w.shape
    return lax.conv_general_dilated(
        x, w[:, None, :].astype(x.dtype), window_strides=(1,),
        padding=[(k - 1, 0)], dimension_numbers=('NWC', 'WIO', 'NWC'),
        feature_group_count=c)


def multiscale_pool(v, w_pool, scale):
    b, s, _ = v.shape
    vf = v.astype(jnp.float32)
    cnt_pos = jnp.arange(s) + 1
    outs = []
    for g, w in enumerate(POOL_WINDOWS):
        xg = vf[..., g * POOL_GROUP:(g + 1) * POOL_GROUP]
        cs = jnp.cumsum(xg, axis=1)
        lag = jnp.pad(cs, ((0, 0), (w, 0), (0, 0)))[:, :s]
        cnt = jnp.minimum(cnt_pos, w).astype(jnp.float32)[None, :, None]
        outs.append((cs - lag) / cnt - xg)
    p = jnp.stack(outs, axis=2).astype(v.dtype)
    p = jnp.einsum('bsgc,gcd->bsgd', p, w_pool).reshape(b, s, D_POOL)
    return p * scale


def conv_pool_mixer(x, norm_g, w_in, conv_w, conv_b, ln_g, ln_b, w_pool, pool_scale, w_out):
    h = rms_norm(x, norm_g)
    u = h @ w_in
    a_val = u[..., :D_CONV]
    a_gate = u[..., D_CONV:2 * D_CONV]
    b_in = u[..., 2 * D_CONV:]
    a = a_val * jax.nn.sigmoid(a_gate)
    a = causal_depthwise_conv(a, conv_w) + conv_b
    a = jax.nn.silu(layer_norm(a, ln_g, ln_b))
    p = multiscale_pool(b_in, w_pool, pool_scale)
    return jnp.concatenate([a, p], axis=-1) @ w_out


def short_conv_mixer(x, norm_g, w_in, conv_w, w_out):
    h = rms_norm(x, norm_g)
    u = h @ w_in
    gate_b = u[..., :D_SHORT]
    gate_c = u[..., D_SHORT:2 * D_SHORT]
    v = u[..., 2 * D_SHORT:]
    y = gate_b * causal_depthwise_conv(gate_c * v, conv_w)
    return y @ w_out


def swiglu(h, w_gate, w_up, w_down):
    return (jax.nn.silu(h @ w_gate) * (h @ w_up)) @ w_down


def _normal(k, shape, fan_in):
    return jax.random.normal(k, shape, jnp.float32) * (fan_in ** -0.5)


def _fwd_setup_inputs(seed: int = 0) -> dict:
    key = jax.random.key(seed)
    ks = jax.random.split(key, 20)
    d = D_MODEL
    x = jax.random.normal(ks[0], (BATCH, SEQ, d), jnp.float32)
    mix_norm_e = 1.0 + 0.02 * jax.random.normal(ks[1], (N_EVEN, d), jnp.float32)
    w_in_e = _normal(ks[2], (N_EVEN, d, 2 * D_CONV + D_POOL), d)
    conv_w_e = _normal(ks[3], (N_EVEN, CONV_WIDTH, D_CONV), CONV_WIDTH)
    conv_b_e = 0.02 * jax.random.normal(ks[4], (N_EVEN, D_CONV), jnp.float32)
    ln_g_e = 1.0 + 0.02 * jax.random.normal(ks[5], (N_EVEN, D_CONV), jnp.float32)
    ln_b_e = 0.02 * jax.random.normal(ks[6], (N_EVEN, D_CONV), jnp.float32)
    w_pool_e = _normal(ks[7], (N_EVEN, len(POOL_WINDOWS), POOL_GROUP, POOL_GROUP), POOL_GROUP)
    pool_scale_e = 1.0 + 0.02 * jax.random.normal(ks[8], (N_EVEN, D_POOL), jnp.float32)
    w_out_e = _normal(ks[9], (N_EVEN, D_CONV + D_POOL, d), D_CONV + D_POOL)
    mix_norm_o = 1.0 + 0.02 * jax.random.normal(ks[10], (N_ODD, d), jnp.float32)
    w_in_o = _normal(ks[11], (N_ODD, d, 3 * D_SHORT), d)
    conv_w_o = _normal(ks[12], (N_ODD, SHORT_WIDTH, D_SHORT), SHORT_WIDTH)
    w_out_o = _normal(ks[13], (N_ODD, D_SHORT, d), D_SHORT)
    ffn_norm = 1.0 + 0.02 * jax.random.normal(ks[14], (DEPTH, d), jnp.float32)
    w_gate = _normal(ks[15], (DEPTH, d, D_FF), d)
    w_up = _normal(ks[16], (DEPTH, d, D_FF), d)
    w_down = _normal(ks[17], (DEPTH, D_FF, d), D_FF)
    final_norm = 1.0 + 0.02 * jax.random.normal(ks[18], (d,), jnp.float32)
    return {'x': x, 'mix_norm_e': mix_norm_e, 'w_in_e': w_in_e, 'conv_w_e': conv_w_e,
            'conv_b_e': conv_b_e, 'ln_g_e': ln_g_e, 'ln_b_e': ln_b_e, 'w_pool_e': w_pool_e,
            'pool_scale_e': pool_scale_e, 'w_out_e': w_out_e, 'mix_norm_o': mix_norm_o,
            'w_in_o': w_in_o, 'conv_w_o': conv_w_o, 'w_out_o': w_out_o, 'ffn_norm': ffn_norm,
            'w_gate': w_gate, 'w_up': w_up, 'w_down': w_down, 'final_norm': final_norm}


def _fwd_reference(x, mix_norm_e, w_in_e, conv_w_e, conv_b_e, ln_g_e, ln_b_e, w_pool_e,
              pool_scale_e, w_out_e, mix_norm_o, w_in_o, conv_w_o, w_out_o, ffn_norm,
              w_gate, w_up, w_down, final_norm):
    h = x
    for i in range(DEPTH):
        j = i // 2
        if i % 2 == 0:
            h = h + conv_pool_mixer(h, mix_norm_e[j], w_in_e[j], conv_w_e[j], conv_b_e[j],
                                    ln_g_e[j], ln_b_e[j], w_pool_e[j], pool_scale_e[j],
                                    w_out_e[j])
        else:
            h = h + short_conv_mixer(h, mix_norm_o[j], w_in_o[j], conv_w_o[j], w_out_o[j])
        h = h + swiglu(rms_norm(h, ffn_norm[i]), w_gate[i], w_up[i], w_down[i])
    return rms_norm(h, final_norm)


import jax as _jax
import jax.numpy as _jnp

TWIN_FORMAT = 'train_step'
FWD_PARAMS = ['x', 'mix_norm_e', 'w_in_e', 'conv_w_e', 'conv_b_e', 'ln_g_e', 'ln_b_e', 'w_pool_e', 'pool_scale_e', 'w_out_e', 'mix_norm_o', 'w_in_o', 'conv_w_o', 'w_out_o', 'ffn_norm', 'w_gate', 'w_up', 'w_down', 'final_norm']
TWIN_WEIGHTS = ['mix_norm_e', 'w_in_e', 'conv_w_e', 'conv_b_e', 'ln_g_e', 'ln_b_e', 'w_pool_e', 'pool_scale_e', 'w_out_e', 'mix_norm_o', 'w_in_o', 'conv_w_o', 'w_out_o', 'ffn_norm', 'w_gate', 'w_up', 'w_down', 'final_norm']
TWIN_DIFF_INPUT = 'x'
TWIN_INPUTS = ['x', 'mix_norm_e', 'w_in_e', 'conv_w_e', 'conv_b_e', 'ln_g_e', 'ln_b_e', 'w_pool_e', 'pool_scale_e', 'w_out_e', 'mix_norm_o', 'w_in_o', 'conv_w_o', 'w_out_o', 'ffn_norm', 'w_gate', 'w_up', 'w_down', 'final_norm', 'loss_target', 'm_mix_norm_e', 'm_w_in_e', 'm_conv_w_e', 'm_conv_b_e', 'm_ln_g_e', 'm_ln_b_e', 'm_w_pool_e', 'm_pool_scale_e', 'm_w_out_e', 'm_mix_norm_o', 'm_w_in_o', 'm_conv_w_o', 'm_w_out_o', 'm_ffn_norm', 'm_w_gate', 'm_w_up', 'm_w_down', 'm_final_norm', 'v_mix_norm_e', 'v_w_in_e', 'v_conv_w_e', 'v_conv_b_e', 'v_ln_g_e', 'v_ln_b_e', 'v_w_pool_e', 'v_pool_scale_e', 'v_w_out_e', 'v_mix_norm_o', 'v_w_in_o', 'v_conv_w_o', 'v_w_out_o', 'v_ffn_norm', 'v_w_gate', 'v_w_up', 'v_w_down', 'v_final_norm']
TWIN_OUTPUTS = ['loss', 'grad_x', 'grad_mix_norm_e', 'grad_w_in_e', 'grad_conv_w_e', 'grad_conv_b_e', 'grad_ln_g_e', 'grad_ln_b_e', 'grad_w_pool_e', 'grad_pool_scale_e', 'grad_w_out_e', 'grad_mix_norm_o', 'grad_w_in_o', 'grad_conv_w_o', 'grad_w_out_o', 'grad_ffn_norm', 'grad_w_gate', 'grad_w_up', 'grad_w_down', 'grad_final_norm', 'delta_mix_norm_e', 'delta_w_in_e', 'delta_conv_w_e', 'delta_conv_b_e', 'delta_ln_g_e', 'delta_ln_b_e', 'delta_w_pool_e', 'delta_pool_scale_e', 'delta_w_out_e', 'delta_mix_norm_o', 'delta_w_in_o', 'delta_conv_w_o', 'delta_w_out_o', 'delta_ffn_norm', 'delta_w_gate', 'delta_w_up', 'delta_w_down', 'delta_final_norm', 'new_m_mix_norm_e', 'new_m_w_in_e', 'new_m_conv_w_e', 'new_m_conv_b_e', 'new_m_ln_g_e', 'new_m_ln_b_e', 'new_m_w_pool_e', 'new_m_pool_scale_e', 'new_m_w_out_e', 'new_m_mix_norm_o', 'new_m_w_in_o', 'new_m_conv_w_o', 'new_m_w_out_o', 'new_m_ffn_norm', 'new_m_w_gate', 'new_m_w_up', 'new_m_w_down', 'new_m_final_norm', 'new_v_mix_norm_e', 'new_v_w_in_e', 'new_v_conv_w_e', 'new_v_conv_b_e', 'new_v_ln_g_e', 'new_v_ln_b_e', 'new_v_w_pool_e', 'new_v_pool_scale_e', 'new_v_w_out_e', 'new_v_mix_norm_o', 'new_v_w_in_o', 'new_v_conv_w_o', 'new_v_w_out_o', 'new_v_ffn_norm', 'new_v_w_gate', 'new_v_w_up', 'new_v_w_down', 'new_v_final_norm']
TWIN_LEAF_KINDS = {'loss': 'loss', 'grad_x': 'grad_x', 'grad_mix_norm_e': 'grad_w', 'grad_w_in_e': 'grad_w', 'grad_conv_w_e': 'grad_w', 'grad_conv_b_e': 'grad_w', 'grad_ln_g_e': 'grad_w', 'grad_ln_b_e': 'grad_w', 'grad_w_pool_e': 'grad_w', 'grad_pool_scale_e': 'grad_w', 'grad_w_out_e': 'grad_w', 'grad_mix_norm_o': 'grad_w', 'grad_w_in_o': 'grad_w', 'grad_conv_w_o': 'grad_w', 'grad_w_out_o': 'grad_w', 'grad_ffn_norm': 'grad_w', 'grad_w_gate': 'grad_w', 'grad_w_up': 'grad_w', 'grad_w_down': 'grad_w', 'grad_final_norm': 'grad_w', 'delta_mix_norm_e': 'delta_w', 'delta_w_in_e': 'delta_w', 'delta_conv_w_e': 'delta_w', 'delta_conv_b_e': 'delta_w', 'delta_ln_g_e': 'delta_w', 'delta_ln_b_e': 'delta_w', 'delta_w_pool_e': 'delta_w', 'delta_pool_scale_e': 'delta_w', 'delta_w_out_e': 'delta_w', 'delta_mix_norm_o': 'delta_w', 'delta_w_in_o': 'delta_w', 'delta_conv_w_o': 'delta_w', 'delta_w_out_o': 'delta_w', 'delta_ffn_norm': 'delta_w', 'delta_w_gate': 'delta_w', 'delta_w_up': 'delta_w', 'delta_w_down': 'delta_w', 'delta_final_norm': 'delta_w', 'new_m_mix_norm_e': 'new_m', 'new_m_w_in_e': 'new_m', 'new_m_conv_w_e': 'new_m', 'new_m_conv_b_e': 'new_m', 'new_m_ln_g_e': 'new_m', 'new_m_ln_b_e': 'new_m', 'new_m_w_pool_e': 'new_m', 'new_m_pool_scale_e': 'new_m', 'new_m_w_out_e': 'new_m', 'new_m_mix_norm_o': 'new_m', 'new_m_w_in_o': 'new_m', 'new_m_conv_w_o': 'new_m', 'new_m_w_out_o': 'new_m', 'new_m_ffn_norm': 'new_m', 'new_m_w_gate': 'new_m', 'new_m_w_up': 'new_m', 'new_m_w_down': 'new_m', 'new_m_final_norm': 'new_m', 'new_v_mix_norm_e': 'new_v', 'new_v_w_in_e': 'new_v', 'new_v_conv_w_e': 'new_v', 'new_v_conv_b_e': 'new_v', 'new_v_ln_g_e': 'new_v', 'new_v_ln_b_e': 'new_v', 'new_v_w_pool_e': 'new_v', 'new_v_pool_scale_e': 'new_v', 'new_v_w_out_e': 'new_v', 'new_v_mix_norm_o': 'new_v', 'new_v_w_in_o': 'new_v', 'new_v_conv_w_o': 'new_v', 'new_v_w_out_o': 'new_v', 'new_v_ffn_norm': 'new_v', 'new_v_w_gate': 'new_v', 'new_v_w_up': 'new_v', 'new_v_w_down': 'new_v', 'new_v_final_norm': 'new_v'}


def _forward(args):
    return _fwd_reference(*[args[k] for k in FWD_PARAMS])


def _output_shape():
    out = _jax.eval_shape(lambda: _forward(_fwd_setup_inputs(0)))
    return out.shape, out.dtype

N_MICROBATCH = 1
ADAM_LR = 0.001
ADAM_B1 = 0.9
ADAM_B2 = 0.999
ADAM_EPS = 1e-08
ADAM_WD = 0.01
ADAM_STEP = 10
PER_EXAMPLE_BATCH_AXIS = {'x': 0, 'loss_target': 0}
SHARED_INPUTS = []
_WEIGHT_DTYPES = {'mix_norm_e': _jnp.float32, 'w_in_e': _jnp.float32, 'conv_w_e': _jnp.float32, 'conv_b_e': _jnp.float32, 'ln_g_e': _jnp.float32, 'ln_b_e': _jnp.float32, 'w_pool_e': _jnp.float32, 'pool_scale_e': _jnp.float32, 'w_out_e': _jnp.float32, 'mix_norm_o': _jnp.float32, 'w_in_o': _jnp.float32, 'conv_w_o': _jnp.float32, 'w_out_o': _jnp.float32, 'ffn_norm': _jnp.float32, 'w_gate': _jnp.float32, 'w_up': _jnp.float32, 'w_down': _jnp.float32, 'final_norm': _jnp.float32}
MOMENT_SCALE = {'mix_norm_e': 8.644336e-02, 'w_in_e': 6.766393e-02, 'conv_w_e': 6.580863e-02, 'conv_b_e': 1.466231e-01, 'ln_g_e': 8.110635e-02, 'ln_b_e': 6.602005e-02, 'w_pool_e': 9.342980e-02, 'pool_scale_e': 9.468483e-02, 'w_out_e': 7.977283e-02, 'mix_norm_o': 9.631109e-02, 'w_in_o': 5.449370e-02, 'conv_w_o': 5.483155e-02, 'w_out_o': 5.452458e-02, 'ffn_norm': 6.137619e-02, 'w_gate': 2.633920e-02, 'w_up': 2.553570e-02, 'w_down': 4.228887e-02, 'final_norm': 1.599322e+01}


def _to_microbatches(a, axis):
    t = _jnp.moveaxis(a, axis, 0)
    t = t.reshape((N_MICROBATCH, t.shape[0] // N_MICROBATCH) + t.shape[1:])
    return _jnp.moveaxis(t, 1, axis + 1)


def setup_inputs(seed: int = 0) -> dict:
    inp = _fwd_setup_inputs(seed)
    key = _jax.random.fold_in(_jax.random.key(seed), 7919)
    shape, _ = _output_shape()
    out = dict(inp)
    out["loss_target"] = _jax.random.normal(_jax.random.fold_in(key, 0), shape, _jnp.float32)
    for i, name in enumerate(TWIN_WEIGHTS):
        w = inp[name].astype(_jnp.float32)
        if MOMENT_SCALE is None:
            s = _jnp.sqrt(_jnp.mean(_jnp.square(w)) + 1e-30)
        else:
            s = MOMENT_SCALE[name]
        km, kv = _jax.random.split(_jax.random.fold_in(key, i + 1))
        out[name] = w
        out["m_" + name] = s * _jax.random.normal(km, w.shape, _jnp.float32)
        out["v_" + name] = (s * s) * _jax.random.uniform(kv, w.shape, _jnp.float32, 0.5, 1.5)
    if N_MICROBATCH > 1:
        for name, axis in PER_EXAMPLE_BATCH_AXIS.items():
            out[name] = _to_microbatches(out[name], axis)
    return {'x': out['x'], 'mix_norm_e': out['mix_norm_e'], 'w_in_e': out['w_in_e'], 'conv_w_e': out['conv_w_e'], 'conv_b_e': out['conv_b_e'], 'ln_g_e': out['ln_g_e'], 'ln_b_e': out['ln_b_e'], 'w_pool_e': out['w_pool_e'], 'pool_scale_e': out['pool_scale_e'], 'w_out_e': out['w_out_e'], 'mix_norm_o': out['mix_norm_o'], 'w_in_o': out['w_in_o'], 'conv_w_o': out['conv_w_o'], 'w_out_o': out['w_out_o'], 'ffn_norm': out['ffn_norm'], 'w_gate': out['w_gate'], 'w_up': out['w_up'], 'w_down': out['w_down'], 'final_norm': out['final_norm'], 'loss_target': out['loss_target'], 'm_mix_norm_e': out['m_mix_norm_e'], 'm_w_in_e': out['m_w_in_e'], 'm_conv_w_e': out['m_conv_w_e'], 'm_conv_b_e': out['m_conv_b_e'], 'm_ln_g_e': out['m_ln_g_e'], 'm_ln_b_e': out['m_ln_b_e'], 'm_w_pool_e': out['m_w_pool_e'], 'm_pool_scale_e': out['m_pool_scale_e'], 'm_w_out_e': out['m_w_out_e'], 'm_mix_norm_o': out['m_mix_norm_o'], 'm_w_in_o': out['m_w_in_o'], 'm_conv_w_o': out['m_conv_w_o'], 'm_w_out_o': out['m_w_out_o'], 'm_ffn_norm': out['m_ffn_norm'], 'm_w_gate': out['m_w_gate'], 'm_w_up': out['m_w_up'], 'm_w_down': out['m_w_down'], 'm_final_norm': out['m_final_norm'], 'v_mix_norm_e': out['v_mix_norm_e'], 'v_w_in_e': out['v_w_in_e'], 'v_conv_w_e': out['v_conv_w_e'], 'v_conv_b_e': out['v_conv_b_e'], 'v_ln_g_e': out['v_ln_g_e'], 'v_ln_b_e': out['v_ln_b_e'], 'v_w_pool_e': out['v_w_pool_e'], 'v_pool_scale_e': out['v_pool_scale_e'], 'v_w_out_e': out['v_w_out_e'], 'v_mix_norm_o': out['v_mix_norm_o'], 'v_w_in_o': out['v_w_in_o'], 'v_conv_w_o': out['v_conv_w_o'], 'v_w_out_o': out['v_w_out_o'], 'v_ffn_norm': out['v_ffn_norm'], 'v_w_gate': out['v_w_gate'], 'v_w_up': out['v_w_up'], 'v_w_down': out['v_w_down'], 'v_final_norm': out['v_final_norm']}


def _loss(weights, diff, rest, loss_target):
    with _jax.named_scope("forward"):
        args = {**rest, TWIN_DIFF_INPUT: diff, **{k: w.astype(_WEIGHT_DTYPES[k]) for k, w in weights.items()}}
        y = _forward(args)
    with _jax.named_scope("loss_head"):
        err = _jnp.square(y.astype(_jnp.float32) - loss_target)
        return 0.5 * _jnp.sum(_jnp.mean(err, axis=-1)) if err.ndim else 0.5 * err


def _adamw(w, g, m, v):
    m = ADAM_B1 * m + (1.0 - ADAM_B1) * g
    v = ADAM_B2 * v + (1.0 - ADAM_B2) * _jnp.square(g)
    m_hat = m / (1.0 - ADAM_B1 ** ADAM_STEP)
    v_hat = v / (1.0 - ADAM_B2 ** ADAM_STEP)
    delta = -ADAM_LR * (m_hat / (_jnp.sqrt(v_hat) + ADAM_EPS) + ADAM_WD * w)
    return delta, m, v


def reference(x, mix_norm_e, w_in_e, conv_w_e, conv_b_e, ln_g_e, ln_b_e, w_pool_e, pool_scale_e, w_out_e, mix_norm_o, w_in_o, conv_w_o, w_out_o, ffn_norm, w_gate, w_up, w_down, final_norm, loss_target, m_mix_norm_e, m_w_in_e, m_conv_w_e, m_conv_b_e, m_ln_g_e, m_ln_b_e, m_w_pool_e, m_pool_scale_e, m_w_out_e, m_mix_norm_o, m_w_in_o, m_conv_w_o, m_w_out_o, m_ffn_norm, m_w_gate, m_w_up, m_w_down, m_final_norm, v_mix_norm_e, v_w_in_e, v_conv_w_e, v_conv_b_e, v_ln_g_e, v_ln_b_e, v_w_pool_e, v_pool_scale_e, v_w_out_e, v_mix_norm_o, v_w_in_o, v_conv_w_o, v_w_out_o, v_ffn_norm, v_w_gate, v_w_up, v_w_down, v_final_norm):
    given = dict(x=x, mix_norm_e=mix_norm_e, w_in_e=w_in_e, conv_w_e=conv_w_e, conv_b_e=conv_b_e, ln_g_e=ln_g_e, ln_b_e=ln_b_e, w_pool_e=w_pool_e, pool_scale_e=pool_scale_e, w_out_e=w_out_e, mix_norm_o=mix_norm_o, w_in_o=w_in_o, conv_w_o=conv_w_o, w_out_o=w_out_o, ffn_norm=ffn_norm, w_gate=w_gate, w_up=w_up, w_down=w_down, final_norm=final_norm, loss_target=loss_target, m_mix_norm_e=m_mix_norm_e, m_w_in_e=m_w_in_e, m_conv_w_e=m_conv_w_e, m_conv_b_e=m_conv_b_e, m_ln_g_e=m_ln_g_e, m_ln_b_e=m_ln_b_e, m_w_pool_e=m_w_pool_e, m_pool_scale_e=m_pool_scale_e, m_w_out_e=m_w_out_e, m_mix_norm_o=m_mix_norm_o, m_w_in_o=m_w_in_o, m_conv_w_o=m_conv_w_o, m_w_out_o=m_w_out_o, m_ffn_norm=m_ffn_norm, m_w_gate=m_w_gate, m_w_up=m_w_up, m_w_down=m_w_down, m_final_norm=m_final_norm, v_mix_norm_e=v_mix_norm_e, v_w_in_e=v_w_in_e, v_conv_w_e=v_conv_w_e, v_conv_b_e=v_conv_b_e, v_ln_g_e=v_ln_g_e, v_ln_b_e=v_ln_b_e, v_w_pool_e=v_w_pool_e, v_pool_scale_e=v_pool_scale_e, v_w_out_e=v_w_out_e, v_mix_norm_o=v_mix_norm_o, v_w_in_o=v_w_in_o, v_conv_w_o=v_conv_w_o, v_w_out_o=v_w_out_o, v_ffn_norm=v_ffn_norm, v_w_gate=v_w_gate, v_w_up=v_w_up, v_w_down=v_w_down, v_final_norm=v_final_norm)
    weights = {n: given[n] for n in TWIN_WEIGHTS}
    shared = {n: given[n] for n in SHARED_INPUTS}
    per_example = {n: given[n] for n in ['x']}
    grad_fn = _jax.value_and_grad(_loss, argnums=(0, 1))

    def one_microbatch(ex, loss_target):
        ex = dict(ex)
        diff = ex.pop(TWIN_DIFF_INPUT)
        return grad_fn(weights, diff, {**shared, **ex}, loss_target)

    if N_MICROBATCH == 1:
        loss, (grad_w, grad_x) = one_microbatch(per_example, given["loss_target"])
    else:
        def body(carry, xs):
            loss_sum, grad_sum = carry
            l_k, (gw_k, gx_k) = one_microbatch(xs[0], xs[1])
            with _jax.named_scope("update"):
                return (loss_sum + l_k, _jax.tree.map(_jnp.add, grad_sum, gw_k)), gx_k

        init = (_jnp.zeros((), _jnp.float32), _jax.tree.map(_jnp.zeros_like, weights))
        (loss, grad_w), grad_x = _jax.lax.scan(body, init, (per_example, given["loss_target"]))
    with _jax.named_scope("update"):
        delta_w, new_m, new_v = {}, {}, {}
        for n in TWIN_WEIGHTS:
            delta_w[n], new_m[n], new_v[n] = _adamw(weights[n], grad_w[n], given["m_" + n], given["v_" + n])
    return (loss, grad_x, *[grad_w[n] for n in TWIN_WEIGHTS], *[delta_w[n] for n in TWIN_WEIGHTS],
            *[new_m[n] for n in TWIN_WEIGHTS], *[new_v[n] for n in TWIN_WEIGHTS])
```

```python
import jax
import jax.numpy as jnp
from jax import lax
from jax.experimental import pallas as pl
from jax.experimental.pallas import tpu as pltpu

F32 = jnp.float32
BF16 = jnp.bfloat16
MESH_ID = pl.DeviceIdType.MESH
AXES = ("x", "y", "c")
N_CHIPS = 4
N_DEV = 8

EPS = 1e-6
POOL_WINDOWS = (2, 4, 8, 16)
ADAM_LR, ADAM_B1, ADAM_B2, ADAM_EPS, ADAM_WD, ADAM_STEP = 0.001, 0.9, 0.999, 1e-08, 0.01, 10

LANES = 128
CONV_HALO = 32
POOL_HALO = 16
SHORT_HALO = 8
V7X_VMEM_LIMIT = 56 * 1024 * 1024


def _cparams(*sem):
    return pltpu.CompilerParams(dimension_semantics=sem if sem else None, vmem_limit_bytes=V7X_VMEM_LIMIT)


def _pick(dim, prefs):
    for p in prefs:
        if p <= dim and dim % p == 0:
            return p
    return dim


def _sigmoid(x):
    return jax.nn.sigmoid(x)


_DOT_DIMS = {
    "nn": (((1,), (0,)), ((), ())),
    "nt": (((1,), (1,)), ((), ())),
    "tn": (((0,), (0,)), ((), ())),
}


def _mm(name, pairs, mode, out_dtypes, epilogue, extras=(), acc_of=None, tm=512, tn=512, tk=2048):
    a0, b0 = pairs[0]
    if mode == "nn":
        (m, k), n = a0.shape, b0.shape[1]
    elif mode == "nt":
        (m, k), n = a0.shape, b0.shape[0]
    else:
        (k, m), n = a0.shape, b0.shape[1]
    tm = _pick(m, (tm, 512, 256, 128, 64, 32, 16, 8))
    tn = _pick(n, (tn, 512, 256, 128))
    tk = _pick(k, (tk, 2048, 1024, 512, 256, 128))
    nk = k // tk
    n_pairs = len(pairs)
    acc_of = tuple(acc_of) if acc_of is not None else (0,) * n_pairs
    n_acc = max(acc_of) + 1
    n_ex, n_out = len(extras), len(out_dtypes)
    dims = _DOT_DIMS[mode]

    def body(*refs):
        a_refs = refs[:n_pairs]
        b_refs = refs[n_pairs:2 * n_pairs]
        e_refs = refs[2 * n_pairs:2 * n_pairs + n_ex]
        o_refs = refs[2 * n_pairs + n_ex:2 * n_pairs + n_ex + n_out]
        acc_refs = refs[2 * n_pairs + n_ex + n_out:]

        def partial_sums():
            sums = [None] * n_acc
            for p in range(n_pairs):
                d = lax.dot_general(a_refs[p][...], b_refs[p][...], dims, preferred_element_type=F32)
                sums[acc_of[p]] = d if sums[acc_of[p]] is None else sums[acc_of[p]] + d
            return sums

        if nk == 1:
            epilogue(partial_sums(), e_refs, o_refs)
            return
        kk = pl.program_id(2)

        @pl.when(kk == 0)
        def _():
            for acc in acc_refs:
                acc[...] = jnp.zeros_like(acc)

        for acc, s in zip(acc_refs, partial_sums()):
            acc[...] += s

        @pl.when(kk == nk - 1)
        def _():
            epilogue([acc[...] for acc in acc_refs], e_refs, o_refs)

    if mode == "nn":
        a_spec = pl.BlockSpec((tm, tk), lambda i, j, kk: (i, kk))
        b_spec = pl.BlockSpec((tk, tn), lambda i, j, kk: (kk, j))
    elif mode == "nt":
        a_spec = pl.BlockSpec((tm, tk), lambda i, j, kk: (i, kk))
        b_spec = pl.BlockSpec((tn, tk), lambda i, j, kk: (j, kk))
    else:
        a_spec = pl.BlockSpec((tk, tm), lambda i, j, kk: (kk, i))
        b_spec = pl.BlockSpec((tk, tn), lambda i, j, kk: (kk, j))
    o_spec = pl.BlockSpec((tm, tn), lambda i, j, kk: (i, j))
    outs = pl.pallas_call(
        body,
        name=name,
        grid=(m // tm, n // tn, nk),
        in_specs=[a_spec] * n_pairs + [b_spec] * n_pairs + [o_spec] * n_ex,
        out_specs=[o_spec] * n_out,
        out_shape=[jax.ShapeDtypeStruct((m, n), dt) for dt in out_dtypes],
        scratch_shapes=[pltpu.VMEM((tm, tn), F32) for _ in range(n_acc)] if nk > 1 else [],
        compiler_params=_cparams("parallel", "parallel", "arbitrary"),
    )(*[p[0] for p in pairs], *[p[1] for p in pairs], *extras)
    return outs


def _ep_store(accs, ex, outs):
    outs[0][...] = accs[0].astype(outs[0].dtype)


def _ep_residual(accs, ex, outs):
    outs[0][...] = ex[0][...] + accs[0]


def _ep_swiglu(accs, ex, outs):
    g, u = accs
    outs[0][...] = g.astype(BF16)
    outs[1][...] = u.astype(BF16)
    outs[2][...] = (g * _sigmoid(g) * u).astype(BF16)


def _ep_swiglu_bwd(accs, ex, outs):
    d = accs[0]
    g = ex[0][...].astype(F32)
    u = ex[1][...].astype(F32)
    s = _sigmoid(g)
    outs[0][...] = (d * u * (s * (1.0 + g * (1.0 - s)))).astype(BF16)
    outs[1][...] = (d * (g * s)).astype(BF16)


def _rms_fwd(name, h, g):
    t, d = h.shape
    tr = _pick(t, (256, 128, 64, 32, 16, 8))

    def body(h_ref, g_ref, o_ref):
        x = h_ref[...]
        r = lax.rsqrt(jnp.mean(x * x, axis=-1, keepdims=True) + EPS)
        o_ref[...] = (x * r * g_ref[...]).astype(BF16)

    return pl.pallas_call(
        body, name=name, grid=(t // tr,),
        in_specs=[pl.BlockSpec((tr, d), lambda i: (i, 0)), pl.BlockSpec((1, d), lambda i: (0, 0))],
        out_specs=pl.BlockSpec((tr, d), lambda i: (i, 0)),
        out_shape=jax.ShapeDtypeStruct((t, d), BF16),
        compiler_params=_cparams("parallel"),
    )(h, g)


def _rms_bwd(name, dn, h, g, dres):
    t, d = h.shape
    tr = _pick(t, (256, 128, 64, 32, 16, 8))

    def body(dn_ref, h_ref, g_ref, dres_ref, dh_ref, dhb_ref, dg_ref):
        x = h_ref[...]
        r = lax.rsqrt(jnp.mean(x * x, axis=-1, keepdims=True) + EPS)
        xhat = x * r
        dnv = dn_ref[...]

        @pl.when(pl.program_id(0) == 0)
        def _():
            dg_ref[...] = jnp.zeros_like(dg_ref)

        dg_ref[...] += jnp.sum(dnv * xhat, axis=0, keepdims=True)
        dxh = dnv * g_ref[...]
        dh = dres_ref[...] + r * (dxh - xhat * jnp.mean(dxh * xhat, axis=-1, keepdims=True))
        dh_ref[...] = dh
        dhb_ref[...] = dh.astype(BF16)

    row = pl.BlockSpec((tr, d), lambda i: (i, 0))
    vec = pl.BlockSpec((1, d), lambda i: (0, 0))
    return pl.pallas_call(
        body, name=name, grid=(t // tr,),
        in_specs=[row, row, vec, row],
        out_specs=[row, row, vec],
        out_shape=[jax.ShapeDtypeStruct((t, d), F32), jax.ShapeDtypeStruct((t, d), BF16),
                   jax.ShapeDtypeStruct((1, d), F32)],
        compiler_params=_cparams("arbitrary"),
    )(dn, h, g, dres)


def _loss_head(h, g, target):
    t, d = h.shape
    tr = _pick(t, (256, 128, 64, 32, 16, 8))

    def body(h_ref, g_ref, t_ref, loss_ref, dh_ref, dhb_ref, dg_ref):
        x = h_ref[...]
        gv = g_ref[...]
        r = lax.rsqrt(jnp.mean(x * x, axis=-1, keepdims=True) + EPS)
        xhat = x * r
        err = xhat * gv - t_ref[...]

        @pl.when(pl.program_id(0) == 0)
        def _():
            dg_ref[...] = jnp.zeros_like(dg_ref)
            loss_ref[...] = jnp.zeros_like(loss_ref)

        loss_ref[...] += jnp.full(loss_ref.shape, 0.5 / d, F32) * jnp.sum(err * err)
        dy = err * (1.0 / d)
        dg_ref[...] += jnp.sum(dy * xhat, axis=0, keepdims=True)
        dxh = dy * gv
        dh = r * (dxh - xhat * jnp.mean(dxh * xhat, axis=-1, keepdims=True))
        dh_ref[...] = dh
        dhb_ref[...] = dh.astype(BF16)

    row = pl.BlockSpec((tr, d), lambda i: (i, 0))
    vec = pl.BlockSpec((1, d), lambda i: (0, 0))
    return pl.pallas_call(
        body, name="loss_head", grid=(t // tr,),
        in_specs=[row, vec, row],
        out_specs=[pl.BlockSpec((1, LANES), lambda i: (0, 0)), row, row, vec],
        out_shape=[jax.ShapeDtypeStruct((1, LANES), F32), jax.ShapeDtypeStruct((t, d), F32),
                   jax.ShapeDtypeStruct((t, d), BF16), jax.ShapeDtypeStruct((1, d), F32)],
        compiler_params=_cparams("arbitrary"),
    )(h, g, target)


def _cur(ts, width, col):
    return pl.BlockSpec((ts, width), lambda i: (i, col))


def _prev_halo(ts, halo, width, col):
    per = ts // halo
    return pl.BlockSpec((halo, width), lambda i: (jnp.maximum(i * per - 1, 0), col))


def _next_halo(ts, halo, width, col, n_rows):
    per = ts // halo
    last = n_rows // halo - 1
    return pl.BlockSpec((halo, width), lambda i: (jnp.minimum((i + 1) * per, last), col))


def _full(shape):
    nd = len(shape)
    return pl.BlockSpec(shape, lambda i: (0,) * nd)


def _shift_down(x, n):
    return x if n == 0 else pltpu.roll(x, n, 0)


def _shift_up(x, n):
    return x if n == 0 else pltpu.roll(x, x.shape[0] - n, 0)


def _pool_counts(i, ns, ts, w):
    pos = (i % ns) * ts + lax.broadcasted_iota(jnp.int32, (ts, 1), 0)
    return jnp.minimum(pos + 1, w).astype(F32)


def _pooled(cur, prev_tail, w, cnt):
    s = jnp.concatenate([prev_tail, cur], axis=0)
    d = 1
    while d < w:
        s = s + _shift_down(s, d)
        d *= 2
    return s[POOL_HALO:, :] / cnt - cur


def _mixer_e_fwd(u, conv_w, conv_b, ln_g, ln_b, w_pool, scale, seq, ts):
    t = u.shape[0]
    dc = conv_b.shape[1]
    ng, pg = w_pool.shape[0], w_pool.shape[1]
    taps = conv_w.shape[0]
    ns = seq // ts

    def body(val_ref, gate_ref, b_ref, pval_ref, pgate_ref, pb_ref, cw_ref, cb_ref, g_ref, be_ref, wp_ref, sc_ref,
             a2_ref, cat_ref):
        i = pl.program_id(0)
        keep_prev = jnp.where(i % ns == 0, 0.0, 1.0)
        a1 = val_ref[...] * _sigmoid(gate_ref[...])
        pa1 = pval_ref[...] * _sigmoid(pgate_ref[...]) * keep_prev
        ext = jnp.concatenate([pa1, a1], axis=0)
        acc = jnp.zeros_like(ext)
        for k in range(taps):
            acc = acc + cw_ref[k:k + 1, :] * _shift_down(ext, taps - 1 - k)
        a2 = acc[CONV_HALO:, :] + cb_ref[...]
        a2_ref[...] = a2
        mu = jnp.mean(a2, axis=-1, keepdims=True)
        xc = a2 - mu
        rstd = lax.rsqrt(jnp.mean(xc * xc, axis=-1, keepdims=True) + EPS)
        a3 = xc * rstd * g_ref[...] + be_ref[...]
        cat_ref[:, 0:dc] = (a3 * _sigmoid(a3)).astype(BF16)
        for g in range(ng):
            lo, hi = g * pg, (g + 1) * pg
            w = POOL_WINDOWS[g]
            p = _pooled(b_ref[:, lo:hi], pb_ref[:, lo:hi] * keep_prev, w, _pool_counts(i, ns, ts, w))
            q = jnp.dot(p.astype(BF16), wp_ref[g].astype(BF16), preferred_element_type=F32)
            cat_ref[:, dc + lo:dc + hi] = (q * sc_ref[:, lo:hi]).astype(BF16)

    return pl.pallas_call(
        body, name="mixer_e_fwd", grid=(t // ts,),
        in_specs=[_cur(ts, dc, 0), _cur(ts, dc, 1), _cur(ts, dc, 2),
                  _prev_halo(ts, CONV_HALO, dc, 0), _prev_halo(ts, CONV_HALO, dc, 1), _prev_halo(ts, POOL_HALO, dc, 2),
                  _full(conv_w.shape), _full(conv_b.shape), _full(ln_g.shape), _full(ln_b.shape),
                  _full(w_pool.shape), _full(scale.shape)],
        out_specs=[_cur(ts, dc, 0), _cur(ts, 2 * dc, 0)],
        out_shape=[jax.ShapeDtypeStruct((t, dc), F32), jax.ShapeDtypeStruct((t, 2 * dc), BF16)],
        compiler_params=_cparams("parallel"),
    )(u, u, u, u, u, u, conv_w, conv_b, ln_g, ln_b, w_pool, scale)


def _mixer_e_bwd_norm(dcat, a2, ln_g, ln_b, ts):
    t, dc = a2.shape

    def body(d_ref, a2_ref, g_ref, be_ref, da2_ref, dg_ref, db_ref, dcb_ref):
        x = a2_ref[...]
        gv = g_ref[...]
        mu = jnp.mean(x, axis=-1, keepdims=True)
        xc = x - mu
        rstd = lax.rsqrt(jnp.mean(xc * xc, axis=-1, keepdims=True) + EPS)
        xhat = xc * rstd
        a3 = xhat * gv + be_ref[...]
        sg = _sigmoid(a3)
        da3 = d_ref[...] * (sg * (1.0 + a3 * (1.0 - sg)))
        dxh = da3 * gv
        da2 = rstd * (dxh - jnp.mean(dxh, axis=-1, keepdims=True)
                      - xhat * jnp.mean(dxh * xhat, axis=-1, keepdims=True))
        da2_ref[...] = da2

        @pl.when(pl.program_id(0) == 0)
        def _():
            dg_ref[...] = jnp.zeros_like(dg_ref)
            db_ref[...] = jnp.zeros_like(db_ref)
            dcb_ref[...] = jnp.zeros_like(dcb_ref)

        dg_ref[...] += jnp.sum(da3 * xhat, axis=0, keepdims=True)
        db_ref[...] += jnp.sum(da3, axis=0, keepdims=True)
        dcb_ref[...] += jnp.sum(da2, axis=0, keepdims=True)

    vec = _full((1, dc))
    return pl.pallas_call(
        body, name="mixer_e_bwd_norm", grid=(t // ts,),
        in_specs=[_cur(ts, dc, 0), _cur(ts, dc, 0), vec, vec],
        out_specs=[_cur(ts, dc, 0), vec, vec, vec],
        out_shape=[jax.ShapeDtypeStruct((t, dc), F32)] + [jax.ShapeDtypeStruct((1, dc), F32)] * 3,
        compiler_params=_cparams("arbitrary"),
    )(dcat, a2, ln_g, ln_b)


def _mixer_e_bwd_mix(da2, dcat, u, conv_w, w_pool, scale, seq, ts):
    t, dc = da2.shape
    ng, pg = w_pool.shape[0], w_pool.shape[1]
    taps = conv_w.shape[0]
    ns = seq // ts

    def body(da2_ref, nda2_ref, dp_ref, ndp_ref, val_ref, gate_ref, b_ref, pval_ref, pgate_ref, pb_ref,
             cw_ref, wp_ref, sc_ref, du_ref, dcw_ref, dwp_ref, dsc_ref):
        i = pl.program_id(0)
        keep_prev = jnp.where(i % ns == 0, 0.0, 1.0)
        keep_next = jnp.where(i % ns == ns - 1, 0.0, 1.0)

        @pl.when(i == 0)
        def _():
            dcw_ref[...] = jnp.zeros_like(dcw_ref)
            dwp_ref[...] = jnp.zeros_like(dwp_ref)
            dsc_ref[...] = jnp.zeros_like(dsc_ref)

        val = val_ref[...]
        sg = _sigmoid(gate_ref[...])
        a1 = val * sg
        pa1 = pval_ref[...] * _sigmoid(pgate_ref[...]) * keep_prev
        ext_a = jnp.concatenate([pa1, a1], axis=0)
        da2v = da2_ref[...]
        ext_d = jnp.concatenate([da2v, nda2_ref[...] * keep_next], axis=0)
        da1 = jnp.zeros_like(ext_d)
        for k in range(taps):
            sh = taps - 1 - k
            dcw_ref[k:k + 1, :] += jnp.sum(da2v * _shift_down(ext_a, sh)[CONV_HALO:, :], axis=0, keepdims=True)
            da1 = da1 + cw_ref[k:k + 1, :] * _shift_up(ext_d, sh)
        da1 = da1[:ts, :]
        du_ref[:, 0:dc] = (da1 * sg).astype(BF16)
        du_ref[:, dc:2 * dc] = (da1 * a1 * (1.0 - sg)).astype(BF16)

        for g in range(ng):
            lo, hi = g * pg, (g + 1) * pg
            w = POOL_WINDOWS[g]
            cnt = _pool_counts(i, ns, ts, w)
            wpb = wp_ref[g].astype(BF16)
            sc = sc_ref[:, lo:hi]
            p = _pooled(b_ref[:, lo:hi], pb_ref[:, lo:hi] * keep_prev, w, cnt)
            pb16 = p.astype(BF16)
            q = jnp.dot(pb16, wpb, preferred_element_type=F32)
            dout = dp_ref[:, lo:hi]
            dsc_ref[:, lo:hi] += jnp.sum(dout * q, axis=0, keepdims=True)
            dq = (dout * sc).astype(BF16)
            dwp_ref[g] += lax.dot_general(pb16, dq, _DOT_DIMS["tn"], preferred_element_type=F32)
            dpool = lax.dot_general(dq, wpb, _DOT_DIMS["nt"], preferred_element_type=F32)
            ndq = (ndp_ref[:, lo:hi] * sc * keep_next).astype(BF16)
            ndpool = lax.dot_general(ndq, wpb, _DOT_DIMS["nt"], preferred_element_type=F32)
            s = jnp.concatenate([dpool / cnt, ndpool * (1.0 / w)], axis=0)
            d = 1
            while d < w:
                s = s + _shift_up(s, d)
                d *= 2
            du_ref[:, 2 * dc + lo:2 * dc + hi] = (s[:ts, :] - dpool).astype(BF16)

    return pl.pallas_call(
        body, name="mixer_e_bwd_mix", grid=(t // ts,),
        in_specs=[_cur(ts, dc, 0), _next_halo(ts, CONV_HALO, dc, 0, t),
                  _cur(ts, dc, 1), _next_halo(ts, POOL_HALO, dc, 1, t),
                  _cur(ts, dc, 0), _cur(ts, dc, 1), _cur(ts, dc, 2),
                  _prev_halo(ts, CONV_HALO, dc, 0), _prev_halo(ts, CONV_HALO, dc, 1), _prev_halo(ts, POOL_HALO, dc, 2),
                  _full(conv_w.shape), _full(w_pool.shape), _full(scale.shape)],
        out_specs=[_cur(ts, 3 * dc, 0), _full(conv_w.shape), _full(w_pool.shape), _full(scale.shape)],
        out_shape=[jax.ShapeDtypeStruct((t, 3 * dc), BF16), jax.ShapeDtypeStruct(conv_w.shape, F32),
                   jax.ShapeDtypeStruct(w_pool.shape, F32), jax.ShapeDtypeStruct(scale.shape, F32)],
        compiler_params=_cparams("arbitrary"),
    )(da2, da2, dcat, dcat, u, u, u, u, u, u, conv_w, w_pool, scale)


def _mixer_o_fwd(u, conv_w, seq, ts):
    t = u.shape[0]
    d = conv_w.shape[1]
    taps = conv_w.shape[0]
    ns = seq // ts

    def body(gb_ref, gc_ref, v_ref, pgc_ref, pv_ref, cw_ref, y_ref):
        keep_prev = jnp.where(pl.program_id(0) % ns == 0, 0.0, 1.0)
        ext = jnp.concatenate([pgc_ref[...] * pv_ref[...] * keep_prev, gc_ref[...] * v_ref[...]], axis=0)
        cc = jnp.zeros_like(ext)
        for k in range(taps):
            cc = cc + cw_ref[k:k + 1, :] * _shift_down(ext, taps - 1 - k)
        y_ref[...] = (gb_ref[...] * cc[SHORT_HALO:, :]).astype(BF16)

    return pl.pallas_call(
        body, name="mixer_o_fwd", grid=(t // ts,),
        in_specs=[_cur(ts, d, 0), _cur(ts, d, 1), _cur(ts, d, 2),
                  _prev_halo(ts, SHORT_HALO, d, 1), _prev_halo(ts, SHORT_HALO, d, 2), _full(conv_w.shape)],
        out_specs=_cur(ts, d, 0),
        out_shape=jax.ShapeDtypeStruct((t, d), BF16),
        compiler_params=_cparams("parallel"),
    )(u, u, u, u, u, conv_w)


def _mixer_o_bwd(dy, u, conv_w, seq, ts):
    t = u.shape[0]
    d = conv_w.shape[1]
    taps = conv_w.shape[0]
    ns = seq // ts

    def body(dy_ref, ndy_ref, gb_ref, gc_ref, v_ref, pgc_ref, pv_ref, ngb_ref, cw_ref, du_ref, dcw_ref):
        i = pl.program_id(0)
        keep_prev = jnp.where(i % ns == 0, 0.0, 1.0)
        keep_next = jnp.where(i % ns == ns - 1, 0.0, 1.0)

        @pl.when(i == 0)
        def _():
            dcw_ref[...] = jnp.zeros_like(dcw_ref)

        gb, gc, v, dyv = gb_ref[...], gc_ref[...], v_ref[...], dy_ref[...]
        ext = jnp.concatenate([pgc_ref[...] * pv_ref[...] * keep_prev, gc * v], axis=0)
        dcc = dyv * gb
        ext_d = jnp.concatenate([dcc, ndy_ref[...] * ngb_ref[...] * keep_next], axis=0)
        cc = jnp.zeros_like(ext)
        dcv = jnp.zeros_like(ext_d)
        for k in range(taps):
            sh = taps - 1 - k
            shifted = _shift_down(ext, sh)
            cc = cc + cw_ref[k:k + 1, :] * shifted
            dcw_ref[k:k + 1, :] += jnp.sum(dcc * shifted[SHORT_HALO:, :], axis=0, keepdims=True)
            dcv = dcv + cw_ref[k:k + 1, :] * _shift_up(ext_d, sh)
        dcv = dcv[:ts, :]
        du_ref[:, 0:d] = (dyv * cc[SHORT_HALO:, :]).astype(BF16)
        du_ref[:, d:2 * d] = (dcv * v).astype(BF16)
        du_ref[:, 2 * d:3 * d] = (dcv * gc).astype(BF16)

    return pl.pallas_call(
        body, name="mixer_o_bwd", grid=(t // ts,),
        in_specs=[_cur(ts, d, 0), _next_halo(ts, SHORT_HALO, d, 0, t),
                  _cur(ts, d, 0), _cur(ts, d, 1), _cur(ts, d, 2),
                  _prev_halo(ts, SHORT_HALO, d, 1), _prev_halo(ts, SHORT_HALO, d, 2),
                  _next_halo(ts, SHORT_HALO, d, 0, t), _full(conv_w.shape)],
        out_specs=[_cur(ts, 3 * d, 0), _full(conv_w.shape)],
        out_shape=[jax.ShapeDtypeStruct((t, 3 * d), BF16), jax.ShapeDtypeStruct(conv_w.shape, F32)],
        compiler_params=_cparams("arbitrary"),
    )(dy, dy, u, u, u, u, u, u, conv_w)


def _rows_for(cols, bytes_per_row_elem=4, target=1 << 20):
    return max(8, target // (cols * bytes_per_row_elem))


def _cast_bf16(name, w):
    r, c = w.shape
    tr = _pick(r, (512, 256, 128, 64, 32, 16))

    def body(w_ref, o_ref):
        o_ref[...] = w_ref[...].astype(BF16)

    spec = pl.BlockSpec((tr, c), lambda i: (i, 0))
    return pl.pallas_call(body, name=name, grid=(r // tr,), in_specs=[spec], out_specs=spec,
                          out_shape=jax.ShapeDtypeStruct((r, c), BF16), compiler_params=_cparams("parallel"))(w)


def _adamw(name, w, g, m, v):
    r, c = w.shape
    tr = _pick(r, (256, 128, 64, 32, 16, 8)) if c > 1024 else _pick(r, (512, 256, 128, 64, 32, 16, 8))
    bc1 = 1.0 - ADAM_B1 ** ADAM_STEP
    bc2 = 1.0 - ADAM_B2 ** ADAM_STEP

    def body(w_ref, g_ref, m_ref, v_ref, d_ref, mo_ref, vo_ref):
        gv = g_ref[...]
        mn = ADAM_B1 * m_ref[...] + (1.0 - ADAM_B1) * gv
        vn = ADAM_B2 * v_ref[...] + (1.0 - ADAM_B2) * (gv * gv)
        mo_ref[...] = mn
        vo_ref[...] = vn
        d_ref[...] = -ADAM_LR * ((mn / bc1) / (jnp.sqrt(vn / bc2) + ADAM_EPS) + ADAM_WD * w_ref[...])

    spec = pl.BlockSpec((tr, c), lambda i: (i, 0))
    return pl.pallas_call(body, name=name, grid=(r // tr,), in_specs=[spec] * 4, out_specs=[spec] * 3,
                          out_shape=[jax.ShapeDtypeStruct((r, c), F32)] * 3, compiler_params=_cparams("parallel"))(w, g, m, v)


def _aligned(offset, multiple):
    return offset if isinstance(offset, int) else pl.multiple_of(offset, multiple)


class _Mat:
    def __init__(self, kind, shard_shape):
        self.kind = kind
        self.sr, self.sc = shard_shape
        self.full_shape = (self.sr, self.sc * N_CHIPS) if kind == "col" else (self.sr * N_CHIPS, self.sc)
        self.pr, self.pc = self.sr // 2, self.sc

    def piece(self, ref, k, h):
        if self.kind == "col":
            return ref.at[pl.ds(_aligned(h * self.pr, 16), self.pr), pl.ds(_aligned(k * self.sc, LANES), self.sc)]
        return ref.at[pl.ds(_aligned(k * self.sr + h * self.pr, 16), self.pr), :]

    def shard(self, ref, k):
        if self.kind == "col":
            return ref.at[:, pl.ds(_aligned(k * self.sc, LANES), self.sc)]
        return ref.at[pl.ds(_aligned(k * self.sr, 16), self.sr), :]

    def half(self, ref, h):
        return ref.at[pl.ds(_aligned(h * self.pr, 16), self.pr), :]


def _place():
    x, y, c = lax.axis_index("x"), lax.axis_index("y"), lax.axis_index("c")
    others = [(1 - x, y), (x, 1 - y), (1 - x, 1 - y)]
    return x, y, c, others


_ANY = pl.BlockSpec(memory_space=pl.ANY)


def _gather_weights(mats, shards, packed_small):
    n = len(mats)

    def body(*refs):
        sh_refs, small_ref = refs[:n], refs[n]
        full_refs, small_all = refs[n + 1:2 * n + 1], refs[2 * n + 1]
        send_sems, recv_sems, local_sems = refs[2 * n + 2:]
        x, y, c, others = _place()
        me_k = 2 * x + y
        sibling = (x, y, 1 - c)

        def remote(src, dst, sem, to):
            return pltpu.make_async_remote_copy(src_ref=src, dst_ref=dst, send_sem=send_sems.at[sem],
                                                recv_sem=recv_sems.at[sem], device_id=to, device_id_type=MESH_ID)

        local = [pltpu.make_async_copy(sh_refs[m], mats[m].shard(full_refs[m], me_k), local_sems.at[m]) for m in range(n)]
        local.append(pltpu.make_async_copy(small_ref, small_all.at[me_k], local_sems.at[n]))
        for cp in local:
            cp.start()
        sent = []
        for j, (ox, oy) in enumerate(others):
            sent.append(remote(small_ref, small_all.at[me_k], 6 * n + j, (ox, oy, c)))
        for m in range(n):
            for j, (ox, oy) in enumerate(others):
                sent.append(remote(mats[m].half(sh_refs[m], c), mats[m].piece(full_refs[m], me_k, c), 6 * m + j, (ox, oy, c)))
        for cp in sent:
            cp.start()
        passed = []
        for m in range(n):
            for j, (ox, oy) in enumerate(others):
                got = mats[m].piece(full_refs[m], 2 * ox + oy, c)
                remote(got, got, 6 * m + j, sibling).wait_recv()
                fwd = remote(got, got, 6 * m + 3 + j, sibling)
                fwd.start()
                passed.append(fwd)
        for j, (ox, oy) in enumerate(others):
            got = small_all.at[2 * ox + oy]
            remote(got, got, 6 * n + j, sibling).wait_recv()
        for m in range(n):
            for j, (ox, oy) in enumerate(others):
                got = mats[m].piece(full_refs[m], 2 * ox + oy, 1 - c)
                remote(got, got, 6 * m + 3 + j, sibling).wait_recv()
        for cp in sent + passed:
            cp.wait_send()
        for cp in local:
            cp.wait()

    n_sems = 6 * n + 3
    outs = pl.pallas_call(
        body, name="gather_weights",
        in_specs=[_ANY] * (n + 1), out_specs=[_ANY] * (n + 1),
        out_shape=[jax.ShapeDtypeStruct(mt.full_shape, BF16) for mt in mats]
        + [jax.ShapeDtypeStruct((N_CHIPS,) + packed_small.shape, F32)],
        scratch_shapes=[pltpu.SemaphoreType.DMA((n_sems,)), pltpu.SemaphoreType.DMA((n_sems,)),
                        pltpu.SemaphoreType.DMA((n + 1,))],
    )(*shards, packed_small)
    return outs[:n], outs[n]


def _exchange_halves(mats, grads, packed_small):
    n = len(mats)

    def body(*refs):
        g_refs, small_ref = refs[:n], refs[n]
        land_refs, small_all = refs[n + 1:2 * n + 1], refs[2 * n + 1]
        send_sems, recv_sems, local_sem = refs[2 * n + 2:]
        x, y, c, _ = _place()
        me = 4 * x + 2 * y + c
        sibling = (x, y, 1 - c)

        def remote(src, dst, sem, to):
            return pltpu.make_async_remote_copy(src_ref=src, dst_ref=dst, send_sem=send_sems.at[sem],
                                                recv_sem=recv_sems.at[sem], device_id=to, device_id_type=MESH_ID)

        mine = pltpu.make_async_copy(small_ref, small_all.at[me], local_sem)
        mine.start()
        sent = []
        flips = [(fx, fy, fc) for fx in (0, 1) for fy in (0, 1) for fc in (0, 1) if (fx, fy, fc) != (0, 0, 0)]
        for j, (fx, fy, fc) in enumerate(flips):
            peer = (x ^ fx, y ^ fy, c ^ fc)
            sent.append(remote(small_ref, small_all.at[me], N_CHIPS * n + j, peer))
        for m in range(n):
            for k in range(N_CHIPS):
                sent.append(remote(mats[m].piece(g_refs[m], k, 1 - c), land_refs[m].at[k], N_CHIPS * m + k, sibling))
        for cp in sent:
            cp.start()
        for j, (fx, fy, fc) in enumerate(flips):
            got = small_all.at[4 * (x ^ fx) + 2 * (y ^ fy) + (c ^ fc)]
            remote(got, got, N_CHIPS * n + j, sibling).wait_recv()
        for m in range(n):
            for k in range(N_CHIPS):
                got = land_refs[m].at[k]
                remote(got, got, N_CHIPS * m + k, sibling).wait_recv()
        for cp in sent:
            cp.wait_send()
        mine.wait()

    n_sems = N_CHIPS * n + N_DEV - 1
    outs = pl.pallas_call(
        body, name="exchange_halves",
        in_specs=[_ANY] * (n + 1), out_specs=[_ANY] * (n + 1),
        out_shape=[jax.ShapeDtypeStruct((N_CHIPS, mt.pr, mt.pc), BF16) for mt in mats]
        + [jax.ShapeDtypeStruct((N_DEV,) + packed_small.shape, F32)],
        scratch_shapes=[pltpu.SemaphoreType.DMA((n_sems,)), pltpu.SemaphoreType.DMA((n_sems,)), pltpu.SemaphoreType.DMA],
    )(*grads, packed_small)
    return outs[:n], outs[n]


def _add_halves(name, mat, grad, landed, core):
    tr = _pick(mat.pr, (256, 128, 64, 32, 16))
    per = mat.pr // tr

    def body(core_ref, g_ref, l_ref, o_ref):
        o_ref[...] = (g_ref[...].astype(F32) + l_ref[...].astype(F32)).astype(BF16)

    if mat.kind == "col":
        g_spec = pl.BlockSpec((tr, mat.pc), lambda k, r, core_ref: (core_ref[0] * per + r, k))
    else:
        g_spec = pl.BlockSpec((tr, mat.pc), lambda k, r, core_ref: ((2 * k + core_ref[0]) * per + r, 0))
    p_spec = pl.BlockSpec((None, tr, mat.pc), lambda k, r, core_ref: (k, r, 0))
    return pl.pallas_call(
        body, name=name,
        grid_spec=pltpu.PrefetchScalarGridSpec(num_scalar_prefetch=1, grid=(N_CHIPS, per),
                                               in_specs=[g_spec, p_spec], out_specs=p_spec),
        out_shape=jax.ShapeDtypeStruct((N_CHIPS, mat.pr, mat.pc), BF16),
        compiler_params=_cparams("parallel", "parallel"),
    )(core, grad, landed)


def _scatter_pieces(mats, partials):
    n = len(mats)

    def body(*refs):
        p_refs = refs[:n]
        land_refs = refs[n:2 * n]
        send_sems, recv_sems, local_sems = refs[2 * n:]
        x, y, c, others = _place()
        me_k = 2 * x + y

        def remote(src, dst, sem, to):
            return pltpu.make_async_remote_copy(src_ref=src, dst_ref=dst, send_sem=send_sems.at[sem],
                                                recv_sem=recv_sems.at[sem], device_id=to, device_id_type=MESH_ID)

        local = [pltpu.make_async_copy(p_refs[m].at[me_k], land_refs[m].at[me_k], local_sems.at[m]) for m in range(n)]
        for cp in local:
            cp.start()
        sent = []
        for m in range(n):
            for j, (ox, oy) in enumerate(others):
                sent.append(remote(p_refs[m].at[2 * ox + oy], land_refs[m].at[me_k], 3 * m + j, (ox, oy, c)))
        for cp in sent:
            cp.start()
        for m in range(n):
            for j, (ox, oy) in enumerate(others):
                got = land_refs[m].at[2 * ox + oy]
                remote(got, got, 3 * m + j, (ox, oy, c)).wait_recv()
        for cp in sent:
            cp.wait_send()
        for cp in local:
            cp.wait()

    outs = pl.pallas_call(
        body, name="scatter_pieces",
        in_specs=[_ANY] * n, out_specs=[_ANY] * n,
        out_shape=[jax.ShapeDtypeStruct((N_CHIPS, mt.pr, mt.pc), BF16) for mt in mats],
        scratch_shapes=[pltpu.SemaphoreType.DMA((3 * n,)), pltpu.SemaphoreType.DMA((3 * n,)), pltpu.SemaphoreType.DMA((n,))],
    )(*partials)
    return outs


def _sum_chips(name, mat, landed):
    tr = _pick(mat.pr, (256, 128, 64, 32, 16))

    def body(l_ref, o_ref):
        s = l_ref[0].astype(F32)
        for k in range(1, N_CHIPS):
            s = s + l_ref[k].astype(F32)
        o_ref[...] = s

    return pl.pallas_call(
        body, name=name, grid=(mat.pr // tr,),
        in_specs=[pl.BlockSpec((N_CHIPS, tr, mat.pc), lambda r: (0, r, 0))],
        out_specs=pl.BlockSpec((tr, mat.pc), lambda r: (r, 0)),
        out_shape=jax.ShapeDtypeStruct((mat.pr, mat.pc), F32),
        compiler_params=_cparams("parallel"),
    )(landed)


def _share_pieces(mats, pieces, groups):
    n = len(mats)

    def body(*refs):
        p_refs = refs[:n]
        out_refs = refs[n:n + len(groups)]
        send_sems, recv_sems, local_sems = refs[n + len(groups):]
        x, y, c, _ = _place()
        sibling = (x, y, 1 - c)
        local, sent, waits = [], [], []
        for o, members in enumerate(groups):
            for l, m in enumerate(members):
                dst = out_refs[o].at[l] if len(members) > 1 else out_refs[o]
                mine = mats[m].half(dst, c)
                local.append(pltpu.make_async_copy(p_refs[m], mine, local_sems.at[m]))
                sent.append(pltpu.make_async_remote_copy(src_ref=p_refs[m], dst_ref=mine, send_sem=send_sems.at[m],
                                                         recv_sem=recv_sems.at[m], device_id=sibling, device_id_type=MESH_ID))
                theirs = mats[m].half(dst, 1 - c)
                waits.append(pltpu.make_async_remote_copy(src_ref=theirs, dst_ref=theirs, send_sem=send_sems.at[m],
                                                          recv_sem=recv_sems.at[m], device_id=sibling, device_id_type=MESH_ID))
        for cp in local + sent:
            cp.start()
        for cp in waits:
            cp.wait_recv()
        for cp in sent:
            cp.wait_send()
        for cp in local:
            cp.wait()

    out_shape = []
    for members in groups:
        mt = mats[members[0]]
        shape = (mt.sr, mt.sc) if len(members) == 1 else (len(members), mt.sr, mt.sc)
        out_shape.append(jax.ShapeDtypeStruct(shape, F32))
    return pl.pallas_call(
        body, name="share_pieces",
        in_specs=[_ANY] * n, out_specs=[_ANY] * len(groups), out_shape=out_shape,
        scratch_shapes=[pltpu.SemaphoreType.DMA((n,)), pltpu.SemaphoreType.DMA((n,)), pltpu.SemaphoreType.DMA((n,))],
    )(*pieces)


def _sum_devices(stacked):
    nd, r, c = stacked.shape

    def body(s_ref, o_ref):
        s = s_ref[0]
        for k in range(1, nd):
            s = s + s_ref[k]
        o_ref[...] = s

    return pl.pallas_call(
        body, name="sum_small_grads", grid=(1,),
        in_specs=[pl.BlockSpec((nd, r, c), lambda i: (0, 0, 0))],
        out_specs=pl.BlockSpec((r, c), lambda i: (0, 0)),
        out_shape=jax.ShapeDtypeStruct((r, c), F32),
        compiler_params=_cparams("arbitrary"),
    )(stacked)


def _pack(arrs):
    flat = jnp.concatenate([a.reshape(-1) for a in arrs])
    rows = -(-flat.shape[0] // (8 * LANES)) * 8
    return jnp.pad(flat, (0, rows * LANES - flat.shape[0])).reshape(rows, LANES)


def _unpack(packed, shapes):
    flat = packed.reshape(-1)
    out, at = [], 0
    for s in shapes:
        size = 1
        for dim in s:
            size *= dim
        out.append(flat[at:at + size].reshape(s))
        at += size
    return out


def kernel(x, mix_norm_e, w_in_e, conv_w_e, conv_b_e, ln_g_e, ln_b_e, w_pool_e, pool_scale_e, w_out_e, mix_norm_o, w_in_o, conv_w_o, w_out_o, ffn_norm, w_gate, w_up, w_down, final_norm, loss_target, m_mix_norm_e, m_w_in_e, m_conv_w_e, m_conv_b_e, m_ln_g_e, m_ln_b_e, m_w_pool_e, m_pool_scale_e, m_w_out_e, m_mix_norm_o, m_w_in_o, m_conv_w_o, m_w_out_o, m_ffn_norm, m_w_gate, m_w_up, m_w_down, m_final_norm, v_mix_norm_e, v_w_in_e, v_conv_w_e, v_conv_b_e, v_ln_g_e, v_ln_b_e, v_w_pool_e, v_pool_scale_e, v_w_out_e, v_mix_norm_o, v_w_in_o, v_conv_w_o, v_w_out_o, v_ffn_norm, v_w_gate, v_w_up, v_w_down, v_final_norm):
    bsz, seq, d = x.shape
    t = bsz * seq
    depth = ffn_norm.shape[0]
    assert depth == 2 and conv_b_e.shape[1] == pool_scale_e.shape[1]
    ts = _pick(seq, (256, 128, 64, 32))
    me_k = 2 * lax.axis_index("x") + lax.axis_index("y")
    core = lax.axis_index("c").astype(jnp.int32).reshape(1)

    mat_src = [("col", w_in_e[0]), ("row", w_out_e[0]), ("col", w_gate[0]), ("col", w_up[0]), ("row", w_down[0]),
               ("col", w_in_o[0]), ("row", w_out_o[0]), ("col", w_gate[1]), ("col", w_up[1]), ("row", w_down[1])]
    mats = [_Mat(kind, w.shape) for kind, w in mat_src]
    shards16 = [_cast_bf16("cast_w%d" % i, w) for i, (_, w) in enumerate(mat_src)]
    small_shards = [conv_w_e[0], w_pool_e[0], mix_norm_o, conv_w_o[0]]
    fulls, small_all = _gather_weights(mats, shards16, _pack(small_shards))
    (W_in_e, W_out_e, W_gate0, W_up0, W_down0, W_in_o, W_out_o, W_gate1, W_up1, W_down1) = fulls
    per_chip = [_unpack(small_all[k], [s.shape for s in small_shards]) for k in range(N_CHIPS)]
    conv_w_e_f = jnp.concatenate([p[0] for p in per_chip], axis=1)
    w_pool_f = jnp.concatenate([p[1] for p in per_chip], axis=1)
    mix_norm_o_f = jnp.concatenate([p[2] for p in per_chip], axis=1)
    conv_w_o_f = jnp.concatenate([p[3] for p in per_chip], axis=1)
    W_gate, W_up, W_down = (W_gate0, W_gate1), (W_up0, W_up1), (W_down0, W_down1)

    h0 = x.reshape(t, d)
    target = loss_target.reshape(t, d)

    def ffn_fwd(l, h):
        n = _rms_fwd("ffn%d_norm" % l, h, ffn_norm[l:l + 1])
        gt, up, act = _mm("ffn%d_gate_up" % l, [(n, W_gate[l]), (n, W_up[l])], "nn", [BF16] * 3, _ep_swiglu,
                          acc_of=(0, 1), tm=1024, tn=512, tk=2048)
        (h_out,) = _mm("ffn%d_down" % l, [(act, W_down[l])], "nn", [F32], _ep_residual, extras=(h,),
                       tm=512, tn=1024, tk=2816)
        return n, gt, up, act, h_out

    n1 = _rms_fwd("mix0_norm", h0, mix_norm_e)
    (u_e,) = _mm("mix0_in", [(n1, W_in_e)], "nn", [F32], _ep_store, tm=1024, tn=1024, tk=2048)
    a2, cat = _mixer_e_fwd(u_e, conv_w_e_f, conv_b_e, ln_g_e, ln_b_e, w_pool_f, pool_scale_e, seq, ts)
    (h1,) = _mm("mix0_out", [(cat, W_out_e)], "nn", [F32], _ep_residual, extras=(h0,), tm=1024, tn=1024, tk=2048)
    n2, gt0, up0, act0, h2 = ffn_fwd(0, h1)
    n3 = _rms_fwd("mix1_norm", h2, mix_norm_o_f)
    (u_o,) = _mm("mix1_in", [(n3, W_in_o)], "nn", [F32], _ep_store, tm=1024, tn=1024, tk=2048)
    y_o = _mixer_o_fwd(u_o, conv_w_o_f, seq, ts)
    (h3,) = _mm("mix1_out", [(y_o, W_out_o)], "nn", [F32], _ep_residual, extras=(h2,), tm=1024, tn=1024, tk=2048)
    n4, gt1, up1, act1, h4 = ffn_fwd(1, h3)
    loss_part, dh4, dh4b, d_final_norm = _loss_head(h4, final_norm.reshape(1, d), target)
    loss = lax.psum(loss_part[0, 0], AXES)

    def ffn_bwd(l, dh, dhb, h_in, n, gt, up, act):
        dgt, dup = _mm("ffn%d_dact" % l, [(dhb, W_down[l])], "nt", [BF16, BF16], _ep_swiglu_bwd, extras=(gt, up),
                       tm=1024, tn=512, tk=2048)
        (dW_down,) = _mm("ffn%d_dw_down" % l, [(act, dhb)], "tn", [BF16], _ep_store, tm=512, tn=1024, tk=2048)
        (dn,) = _mm("ffn%d_dn" % l, [(dgt, W_gate[l]), (dup, W_up[l])], "nt", [F32], _ep_store,
                    tm=1024, tn=1024, tk=1408)
        (dW_gate,) = _mm("ffn%d_dw_gate" % l, [(n, dgt)], "tn", [BF16], _ep_store, tm=1024, tn=512, tk=2048)
        (dW_up,) = _mm("ffn%d_dw_up" % l, [(n, dup)], "tn", [BF16], _ep_store, tm=1024, tn=512, tk=2048)
        dh_in, dh_in_b, dg = _rms_bwd("ffn%d_norm_bwd" % l, dn, h_in, ffn_norm[l:l + 1], dh)
        return dh_in, dh_in_b, dg, dW_gate, dW_up, dW_down

    dh3, dh3b, d_ffn_norm1, dW_gate1, dW_up1, dW_down1 = ffn_bwd(1, dh4, dh4b, h3, n4, gt1, up1, act1)

    (dy_o,) = _mm("mix1_dy", [(dh3b, W_out_o)], "nt", [F32], _ep_store, tm=1024, tn=1024, tk=2048)
    (dW_out_o,) = _mm("mix1_dw_out", [(y_o, dh3b)], "tn", [BF16], _ep_store, tm=1024, tn=1024, tk=2048)
    du_o, d_conv_w_o = _mixer_o_bwd(dy_o, u_o, conv_w_o_f, seq, ts)
    (dn3,) = _mm("mix1_dn", [(du_o, W_in_o)], "nt", [F32], _ep_store, tm=1024, tn=1024, tk=2048)
    (dW_in_o,) = _mm("mix1_dw_in", [(n3, du_o)], "tn", [BF16], _ep_store, tm=1024, tn=1024, tk=2048)
    dh2, dh2b, d_mix_norm_o = _rms_bwd("mix1_norm_bwd", dn3, h2, mix_norm_o_f, dh3)

    dh1, dh1b, d_ffn_norm0, dW_gate0, dW_up0, dW_down0 = ffn_bwd(0, dh2, dh2b, h1, n2, gt0, up0, act0)

    (dcat,) = _mm("mix0_dcat", [(dh1b, W_out_e)], "nt", [F32], _ep_store, tm=1024, tn=1024, tk=2048)
    (dW_out_e,) = _mm("mix0_dw_out", [(cat, dh1b)], "tn", [BF16], _ep_store, tm=1024, tn=1024, tk=2048)
    da2, d_ln_g, d_ln_b, d_conv_b = _mixer_e_bwd_norm(dcat, a2, ln_g_e, ln_b_e, ts)
    du_e, d_conv_w_e, d_w_pool, d_pool_scale = _mixer_e_bwd_mix(da2, dcat, u_e, conv_w_e_f, w_pool_f, pool_scale_e, seq, ts)
    (dn1,) = _mm("mix0_dn", [(du_e, W_in_e)], "nt", [F32], _ep_store, tm=1024, tn=1024, tk=2048)
    (dW_in_e,) = _mm("mix0_dw_in", [(n1, du_e)], "tn", [BF16], _ep_store, tm=1024, tn=1024, tk=2048)
    dx, _, d_mix_norm_e = _rms_bwd("mix0_norm_bwd", dn1, h0, mix_norm_e, dh1)

    grads16 = [dW_in_e, dW_out_e, dW_gate0, dW_up0, dW_down0, dW_in_o, dW_out_o, dW_gate1, dW_up1, dW_down1]
    d_ffn_norm = jnp.concatenate([d_ffn_norm0, d_ffn_norm1], axis=0)
    small_partials = [d_mix_norm_e, d_conv_w_e, d_conv_b, d_ln_g, d_ln_b, d_w_pool, d_pool_scale, d_mix_norm_o,
                      d_conv_w_o, d_ffn_norm, d_final_norm]
    landed, small_stack = _exchange_halves(mats, grads16, _pack(small_partials))
    partials = [_add_halves("add_halves%d" % m, mats[m], grads16[m], landed[m], core) for m in range(len(mats))]
    scattered = _scatter_pieces(mats, partials)
    pieces = [_sum_chips("sum_chips%d" % m, mats[m], scattered[m]) for m in range(len(mats))]
    g_w_in_e, g_w_out_e, g_w_in_o, g_w_out_o, g_w_gate, g_w_up, g_w_down = _share_pieces(
        mats, pieces, [(0,), (1,), (5,), (6,), (2, 7), (3, 8), (4, 9)])
    small_sum = _unpack(_sum_devices(small_stack), [s.shape for s in small_partials])
    (g_mix_norm_e, g_conv_w_e_f, g_conv_b, g_ln_g, g_ln_b, g_w_pool_f, g_pool_scale, g_mix_norm_o_f, g_conv_w_o_f,
     g_ffn_norm, g_final_norm) = small_sum

    def my_shard(full, axis):
        size = full.shape[axis] // N_CHIPS
        return lax.dynamic_slice_in_dim(full, me_k * size, size, axis)

    g_conv_w_e = my_shard(g_conv_w_e_f, 1)
    g_w_pool = my_shard(g_w_pool_f, 1)
    g_mix_norm_o = my_shard(g_mix_norm_o_f, 1)
    g_conv_w_o = my_shard(g_conv_w_o_f, 1)

    grad = {
        "mix_norm_e": g_mix_norm_e, "w_in_e": g_w_in_e[None], "conv_w_e": g_conv_w_e[None], "conv_b_e": g_conv_b,
        "ln_g_e": g_ln_g, "ln_b_e": g_ln_b, "w_pool_e": g_w_pool[None], "pool_scale_e": g_pool_scale,
        "w_out_e": g_w_out_e[None], "mix_norm_o": g_mix_norm_o, "w_in_o": g_w_in_o[None], "conv_w_o": g_conv_w_o[None],
        "w_out_o": g_w_out_o[None], "ffn_norm": g_ffn_norm, "w_gate": g_w_gate, "w_up": g_w_up, "w_down": g_w_down,
        "final_norm": g_final_norm.reshape(final_norm.shape),
    }
    weights = dict(mix_norm_e=mix_norm_e, w_in_e=w_in_e, conv_w_e=conv_w_e, conv_b_e=conv_b_e, ln_g_e=ln_g_e, ln_b_e=ln_b_e,
                   w_pool_e=w_pool_e, pool_scale_e=pool_scale_e, w_out_e=w_out_e, mix_norm_o=mix_norm_o, w_in_o=w_in_o,
                   conv_w_o=conv_w_o, w_out_o=w_out_o, ffn_norm=ffn_norm, w_gate=w_gate, w_up=w_up, w_down=w_down,
                   final_norm=final_norm)
    mom1 = dict(mix_norm_e=m_mix_norm_e, w_in_e=m_w_in_e, conv_w_e=m_conv_w_e, conv_b_e=m_conv_b_e, ln_g_e=m_ln_g_e,
                ln_b_e=m_ln_b_e, w_pool_e=m_w_pool_e, pool_scale_e=m_pool_scale_e, w_out_e=m_w_out_e, mix_norm_o=m_mix_norm_o,
                w_in_o=m_w_in_o, conv_w_o=m_conv_w_o, w_out_o=m_w_out_o, ffn_norm=m_ffn_norm, w_gate=m_w_gate, w_up=m_w_up,
                w_down=m_w_down, final_norm=m_final_norm)
    mom2 = dict(mix_norm_e=v_mix_norm_e, w_in_e=v_w_in_e, conv_w_e=v_conv_w_e, conv_b_e=v_conv_b_e, ln_g_e=v_ln_g_e,
                ln_b_e=v_ln_b_e, w_pool_e=v_w_pool_e, pool_scale_e=v_pool_scale_e, w_out_e=v_w_out_e, mix_norm_o=v_mix_norm_o,
                w_in_o=v_w_in_o, conv_w_o=v_conv_w_o, w_out_o=v_w_out_o, ffn_norm=v_ffn_norm, w_gate=v_w_gate, w_up=v_w_up,
                w_down=v_w_down, final_norm=v_final_norm)
    names = list(weights)

    big = ("w_in_e", "w_out_e", "w_in_o", "w_out_o", "w_gate", "w_up", "w_down")
    delta, new_m, new_v = {}, {}, {}
    for nm in big:
        shape = weights[nm].shape
        as2d = lambda a: a.reshape(shape[0] * shape[1], shape[2])
        dl, mn, vn = _adamw("adamw_" + nm, as2d(weights[nm]), as2d(grad[nm]), as2d(mom1[nm]), as2d(mom2[nm]))
        delta[nm], new_m[nm], new_v[nm] = dl.reshape(shape), mn.reshape(shape), vn.reshape(shape)
    small = [nm for nm in names if nm not in big]
    shapes = [weights[nm].shape for nm in small]
    dl, mn, vn = _adamw("adamw_small", _pack([weights[nm] for nm in small]), _pack([grad[nm] for nm in small]),
                        _pack([mom1[nm] for nm in small]), _pack([mom2[nm] for nm in small]))
    for nm, a, b, c_ in zip(small, _unpack(dl, shapes), _unpack(mn, shapes), _unpack(vn, shapes)):
        delta[nm], new_m[nm], new_v[nm] = a, b, c_

    grad_x = dx.reshape(bsz, seq, d)
    return (loss, grad_x, *[grad[nm] for nm in names], *[delta[nm] for nm in names],
            *[new_m[nm] for nm in names], *[new_v[nm] for nm in names])
```

```python
import jax
import jax.numpy as jnp
from jax import lax
from jax.experimental import pallas as pl
from jax.experimental.pallas import tpu as pltpu

F32 = jnp.float32
BF16 = jnp.bfloat16
MESH_ID = pl.DeviceIdType.MESH
AXES = ("x", "y", "c")
N_CHIPS = 4
N_DEV = 8

EPS = 1e-6
POOL_WINDOWS = (2, 4, 8, 16)
ADAM_LR, ADAM_B1, ADAM_B2, ADAM_EPS, ADAM_WD, ADAM_STEP = 0.001, 0.9, 0.999, 1e-08, 0.01, 10

LANES = 128
CONV_HALO = 32
POOL_HALO = 16
SHORT_HALO = 8
V7X_VMEM_LIMIT = 56 * 1024 * 1024


def _cparams(*sem):
    return pltpu.CompilerParams(dimension_semantics=sem if sem else None, vmem_limit_bytes=V7X_VMEM_LIMIT)


def _pick(dim, prefs):
    for p in prefs:
        if p <= dim and dim % p == 0:
            return p
    return dim


def _sigmoid(x):
    return jax.nn.sigmoid(x)


_DOT_DIMS = {
    "nn": (((1,), (0,)), ((), ())),
    "nt": (((1,), (1,)), ((), ())),
    "tn": (((0,), (0,)), ((), ())),
}


def _mm(name, pairs, mode, out_dtypes, epilogue, extras=(), acc_of=None, tm=512, tn=512, tk=2048):
    a0, b0 = pairs[0]
    if mode == "nn":
        (m, k), n = a0.shape, b0.shape[1]
    elif mode == "nt":
        (m, k), n = a0.shape, b0.shape[0]
    else:
        (k, m), n = a0.shape, b0.shape[1]
    tm = _pick(m, (tm, 512, 256, 128, 64, 32, 16, 8))
    tn = _pick(n, (tn, 512, 256, 128))
    tk = _pick(k, (tk, 2048, 1024, 512, 256, 128))
    nk = k // tk
    n_pairs = len(pairs)
    acc_of = tuple(acc_of) if acc_of is not None else (0,) * n_pairs
    n_acc = max(acc_of) + 1
    n_ex, n_out = len(extras), len(out_dtypes)
    dims = _DOT_DIMS[mode]

    def body(*refs):
        a_refs = refs[:n_pairs]
        b_refs = refs[n_pairs:2 * n_pairs]
        e_refs = refs[2 * n_pairs:2 * n_pairs + n_ex]
        o_refs = refs[2 * n_pairs + n_ex:2 * n_pairs + n_ex + n_out]
        acc_refs = refs[2 * n_pairs + n_ex + n_out:]

        def partial_sums():
            sums = [None] * n_acc
            for p in range(n_pairs):
                d = lax.dot_general(a_refs[p][...], b_refs[p][...], dims, preferred_element_type=F32)
                sums[acc_of[p]] = d if sums[acc_of[p]] is None else sums[acc_of[p]] + d
            return sums

        if nk == 1:
            epilogue(partial_sums(), e_refs, o_refs)
            return
        kk = pl.program_id(2)

        @pl.when(kk == 0)
        def _():
            for acc in acc_refs:
                acc[...] = jnp.zeros_like(acc)

        for acc, s in zip(acc_refs, partial_sums()):
            acc[...] += s

        @pl.when(kk == nk - 1)
        def _():
            epilogue([acc[...] for acc in acc_refs], e_refs, o_refs)

    if mode == "nn":
        a_spec = pl.BlockSpec((tm, tk), lambda i, j, kk: (i, kk))
        b_spec = pl.BlockSpec((tk, tn), lambda i, j, kk: (kk, j))
    elif mode == "nt":
        a_spec = pl.BlockSpec((tm, tk), lambda i, j, kk: (i, kk))
        b_spec = pl.BlockSpec((tn, tk), lambda i, j, kk: (j, kk))
    else:
        a_spec = pl.BlockSpec((tk, tm), lambda i, j, kk: (kk, i))
        b_spec = pl.BlockSpec((tk, tn), lambda i, j, kk: (kk, j))
    o_spec = pl.BlockSpec((tm, tn), lambda i, j, kk: (i, j))
    outs = pl.pallas_call(
        body,
        name=name,
        grid=(m // tm, n // tn, nk),
        in_specs=[a_spec] * n_pairs + [b_spec] * n_pairs + [o_spec] * n_ex,
        out_specs=[o_spec] * n_out,
        out_shape=[jax.ShapeDtypeStruct((m, n), dt) for dt in out_dtypes],
        scratch_shapes=[pltpu.VMEM((tm, tn), F32) for _ in range(n_acc)] if nk > 1 else [],
        compiler_params=_cparams("parallel", "parallel", "arbitrary"),
    )(*[p[0] for p in pairs], *[p[1] for p in pairs], *extras)
    return outs


def _ep_store(accs, ex, outs):
    outs[0][...] = accs[0].astype(outs[0].dtype)


def _ep_residual(accs, ex, outs):
    outs[0][...] = ex[0][...] + accs[0]


def _ep_swiglu(accs, ex, outs):
    g, u = accs
    outs[0][...] = g.astype(BF16)
    outs[1][...] = u.astype(BF16)
    outs[2][...] = (g * _sigmoid(g) * u).astype(BF16)


def _ep_swiglu_bwd(accs, ex, outs):
    d = accs[0]
    g = ex[0][...].astype(F32)
    u = ex[1][...].astype(F32)
    s = _sigmoid(g)
    outs[0][...] = (d * u * (s * (1.0 + g * (1.0 - s)))).astype(BF16)
    outs[1][...] = (d * (g * s)).astype(BF16)


def _rms_fwd(name, h, g):
    t, d = h.shape
    tr = _pick(t, (256, 128, 64, 32, 16, 8))

    def body(h_ref, g_ref, o_ref):
        x = h_ref[...]
        r = lax.rsqrt(jnp.mean(x * x, axis=-1, keepdims=True) + EPS)
        o_ref[...] = (x * r * g_ref[...]).astype(BF16)

    return pl.pallas_call(
        body, name=name, grid=(t // tr,),
        in_specs=[pl.BlockSpec((tr, d), lambda i: (i, 0)), pl.BlockSpec((1, d), lambda i: (0, 0))],
        out_specs=pl.BlockSpec((tr, d), lambda i: (i, 0)),
        out_shape=jax.ShapeDtypeStruct((t, d), BF16),
        compiler_params=_cparams("parallel"),
    )(h, g)


def _rms_bwd(name, dn, h, g, dres):
    t, d = h.shape
    tr = _pick(t, (256, 128, 64, 32, 16, 8))

    def body(dn_ref, h_ref, g_ref, dres_ref, dh_ref, dhb_ref, dg_ref):
        x = h_ref[...]
        r = lax.rsqrt(jnp.mean(x * x, axis=-1, keepdims=True) + EPS)
        xhat = x * r
        dnv = dn_ref[...]

        @pl.when(pl.program_id(0) == 0)
        def _():
            dg_ref[...] = jnp.zeros_like(dg_ref)

        dg_ref[...] += jnp.sum(dnv * xhat, axis=0, keepdims=True)
        dxh = dnv * g_ref[...]
        dh = dres_ref[...] + r * (dxh - xhat * jnp.mean(dxh * xhat, axis=-1, keepdims=True))
        dh_ref[...] = dh
        dhb_ref[...] = dh.astype(BF16)

    row = pl.BlockSpec((tr, d), lambda i: (i, 0))
    vec = pl.BlockSpec((1, d), lambda i: (0, 0))
    return pl.pallas_call(
        body, name=name, grid=(t // tr,),
        in_specs=[row, row, vec, row],
        out_specs=[row, row, vec],
        out_shape=[jax.ShapeDtypeStruct((t, d), F32), jax.ShapeDtypeStruct((t, d), BF16),
                   jax.ShapeDtypeStruct((1, d), F32)],
        compiler_params=_cparams("arbitrary"),
    )(dn, h, g, dres)


def _loss_head(h, g, target):
    t, d = h.shape
    tr = _pick(t, (256, 128, 64, 32, 16, 8))

    def body(h_ref, g_ref, t_ref, loss_ref, dh_ref, dhb_ref, dg_ref):
        x = h_ref[...]
        gv = g_ref[...]
        r = lax.rsqrt(jnp.mean(x * x, axis=-1, keepdims=True) + EPS)
        xhat = x * r
        err = xhat * gv - t_ref[...]

        @pl.when(pl.program_id(0) == 0)
        def _():
            dg_ref[...] = jnp.zeros_like(dg_ref)
            loss_ref[...] = jnp.zeros_like(loss_ref)

        loss_ref[...] += jnp.full(loss_ref.shape, 0.5 / d, F32) * jnp.sum(err * err)
        dy = err * (1.0 / d)
        dg_ref[...] += jnp.sum(dy * xhat, axis=0, keepdims=True)
        dxh = dy * gv
        dh = r * (dxh - xhat * jnp.mean(dxh * xhat, axis=-1, keepdims=True))
        dh_ref[...] = dh
        dhb_ref[...] = dh.astype(BF16)

    row = pl.BlockSpec((tr, d), lambda i: (i, 0))
    vec = pl.BlockSpec((1, d), lambda i: (0, 0))
    return pl.pallas_call(
        body, name="loss_head", grid=(t // tr,),
        in_specs=[row, vec, row],
        out_specs=[pl.BlockSpec((1, LANES), lambda i: (0, 0)), row, row, vec],
        out_shape=[jax.ShapeDtypeStruct((1, LANES), F32), jax.ShapeDtypeStruct((t, d), F32),
                   jax.ShapeDtypeStruct((t, d), BF16), jax.ShapeDtypeStruct((1, d), F32)],
        compiler_params=_cparams("arbitrary"),
    )(h, g, target)


def _cur(ts, width, col):
    return pl.BlockSpec((ts, width), lambda i: (i, col))


def _prev_halo(ts, halo, width, col):
    per = ts // halo
    return pl.BlockSpec((halo, width), lambda i: (jnp.maximum(i * per - 1, 0), col))


def _next_halo(ts, halo, width, col, n_rows):
    per = ts // halo
    last = n_rows // halo - 1
    return pl.BlockSpec((halo, width), lambda i: (jnp.minimum((i + 1) * per, last), col))


def _full(shape):
    nd = len(shape)
    return pl.BlockSpec(shape, lambda i: (0,) * nd)


def _shift_down(x, n):
    return x if n == 0 else pltpu.roll(x, n, 0)


def _shift_up(x, n):
    return x if n == 0 else pltpu.roll(x, x.shape[0] - n, 0)


def _pool_counts(i, ns, ts, w):
    pos = (i % ns) * ts + lax.broadcasted_iota(jnp.int32, (ts, 1), 0)
    return jnp.minimum(pos + 1, w).astype(F32)


def _pooled(cur, prev_tail, w, cnt):
    s = jnp.concatenate([prev_tail, cur], axis=0)
    d = 1
    while d < w:
        s = s + _shift_down(s, d)
        d *= 2
    return s[POOL_HALO:, :] / cnt - cur


def _mixer_e_fwd(u, conv_w, conv_b, ln_g, ln_b, w_pool, scale, seq, ts):
    t = u.shape[0]
    dc = conv_b.shape[1]
    ng, pg = w_pool.shape[0], w_pool.shape[1]
    taps = conv_w.shape[0]
    ns = seq // ts

    def body(val_ref, gate_ref, b_ref, pval_ref, pgate_ref, pb_ref, cw_ref, cb_ref, g_ref, be_ref, wp_ref, sc_ref,
             a2_ref, cat_ref):
        i = pl.program_id(0)
        keep_prev = jnp.where(i % ns == 0, 0.0, 1.0)
        a1 = val_ref[...] * _sigmoid(gate_ref[...])
        pa1 = pval_ref[...] * _sigmoid(pgate_ref[...]) * keep_prev
        ext = jnp.concatenate([pa1, a1], axis=0)
        acc = jnp.zeros_like(ext)
        for k in range(taps):
            acc = acc + cw_ref[k:k + 1, :] * _shift_down(ext, taps - 1 - k)
        a2 = acc[CONV_HALO:, :] + cb_ref[...]
        a2_ref[...] = a2
        mu = jnp.mean(a2, axis=-1, keepdims=True)
        xc = a2 - mu
        rstd = lax.rsqrt(jnp.mean(xc * xc, axis=-1, keepdims=True) + EPS)
        a3 = xc * rstd * g_ref[...] + be_ref[...]
        cat_ref[:, 0:dc] = (a3 * _sigmoid(a3)).astype(BF16)
        for g in range(ng):
            lo, hi = g * pg, (g + 1) * pg
            w = POOL_WINDOWS[g]
            p = _pooled(b_ref[:, lo:hi], pb_ref[:, lo:hi] * keep_prev, w, _pool_counts(i, ns, ts, w))
            q = jnp.dot(p.astype(BF16), wp_ref[g].astype(BF16), preferred_element_type=F32)
            cat_ref[:, dc + lo:dc + hi] = (q * sc_ref[:, lo:hi]).astype(BF16)

    return pl.pallas_call(
        body, name="mixer_e_fwd", grid=(t // ts,),
        in_specs=[_cur(ts, dc, 0), _cur(ts, dc, 1), _cur(ts, dc, 2),
                  _prev_halo(ts, CONV_HALO, dc, 0), _prev_halo(ts, CONV_HALO, dc, 1), _prev_halo(ts, POOL_HALO, dc, 2),
                  _full(conv_w.shape), _full(conv_b.shape), _full(ln_g.shape), _full(ln_b.shape),
                  _full(w_pool.shape), _full(scale.shape)],
        out_specs=[_cur(ts, dc, 0), _cur(ts, 2 * dc, 0)],
        out_shape=[jax.ShapeDtypeStruct((t, dc), F32), jax.ShapeDtypeStruct((t, 2 * dc), BF16)],
        compiler_params=_cparams("parallel"),
    )(u, u, u, u, u, u, conv_w, conv_b, ln_g, ln_b, w_pool, scale)


def _mixer_e_bwd_norm(dcat, a2, ln_g, ln_b, ts):
    t, dc = a2.shape

    def body(d_ref, a2_ref, g_ref, be_ref, da2_ref, dg_ref, db_ref, dcb_ref):
        x = a2_ref[...]
        gv = g_ref[...]
        mu = jnp.mean(x, axis=-1, keepdims=True)
        xc = x - mu
        rstd = lax.rsqrt(jnp.mean(xc * xc, axis=-1, keepdims=True) + EPS)
        xhat = xc * rstd
        a3 = xhat * gv + be_ref[...]
        sg = _sigmoid(a3)
        da3 = d_ref[...] * (sg * (1.0 + a3 * (1.0 - sg)))
        dxh = da3 * gv
        da2 = rstd * (dxh - jnp.mean(dxh, axis=-1, keepdims=True)
                      - xhat * jnp.mean(dxh * xhat, axis=-1, keepdims=True))
        da2_ref[...] = da2

        @pl.when(pl.program_id(0) == 0)
        def _():
            dg_ref[...] = jnp.zeros_like(dg_ref)
            db_ref[...] = jnp.zeros_like(db_ref)
            dcb_ref[...] = jnp.zeros_like(dcb_ref)

        dg_ref[...] += jnp.sum(da3 * xhat, axis=0, keepdims=True)
        db_ref[...] += jnp.sum(da3, axis=0, keepdims=True)
        dcb_ref[...] += jnp.sum(da2, axis=0, keepdims=True)

    vec = _full((1, dc))
    return pl.pallas_call(
        body, name="mixer_e_bwd_norm", grid=(t // ts,),
        in_specs=[_cur(ts, dc, 0), _cur(ts, dc, 0), vec, vec],
        out_specs=[_cur(ts, dc, 0), vec, vec, vec],
        out_shape=[jax.ShapeDtypeStruct((t, dc), F32)] + [jax.ShapeDtypeStruct((1, dc), F32)] * 3,
        compiler_params=_cparams("arbitrary"),
    )(dcat, a2, ln_g, ln_b)


def _mixer_e_bwd_mix(da2, dcat, u, conv_w, w_pool, scale, seq, ts):
    t, dc = da2.shape
    ng, pg = w_pool.shape[0], w_pool.shape[1]
    taps = conv_w.shape[0]
    ns = seq // ts

    def body(da2_ref, nda2_ref, dp_ref, ndp_ref, val_ref, gate_ref, b_ref, pval_ref, pgate_ref, pb_ref,
             cw_ref, wp_ref, sc_ref, du_ref, dcw_ref, dwp_ref, dsc_ref):
        i = pl.program_id(0)
        keep_prev = jnp.where(i % ns == 0, 0.0, 1.0)
        keep_next = jnp.where(i % ns == ns - 1, 0.0, 1.0)

        @pl.when(i == 0)
        def _():
            dcw_ref[...] = jnp.zeros_like(dcw_ref)
            dwp_ref[...] = jnp.zeros_like(dwp_ref)
            dsc_ref[...] = jnp.zeros_like(dsc_ref)

        val = val_ref[...]
        sg = _sigmoid(gate_ref[...])
        a1 = val * sg
        pa1 = pval_ref[...] * _sigmoid(pgate_ref[...]) * keep_prev
        ext_a = jnp.concatenate([pa1, a1], axis=0)
        da2v = da2_ref[...]
        ext_d = jnp.concatenate([da2v, nda2_ref[...] * keep_next], axis=0)
        da1 = jnp.zeros_like(ext_d)
        for k in range(taps):
            sh = taps - 1 - k
            dcw_ref[k:k + 1, :] += jnp.sum(da2v * _shift_down(ext_a, sh)[CONV_HALO:, :], axis=0, keepdims=True)
            da1 = da1 + cw_ref[k:k + 1, :] * _shift_up(ext_d, sh)
        da1 = da1[:ts, :]
        du_ref[:, 0:dc] = (da1 * sg).astype(BF16)
        du_ref[:, dc:2 * dc] = (da1 * a1 * (1.0 - sg)).astype(BF16)

        for g in range(ng):
            lo, hi = g * pg, (g + 1) * pg
            w = POOL_WINDOWS[g]
            cnt = _pool_counts(i, ns, ts, w)
            wpb = wp_ref[g].astype(BF16)
            sc = sc_ref[:, lo:hi]
            p = _pooled(b_ref[:, lo:hi], pb_ref[:, lo:hi] * keep_prev, w, cnt)
            pb16 = p.astype(BF16)
            q = jnp.dot(pb16, wpb, preferred_element_type=F32)
            dout = dp_ref[:, lo:hi]
            dsc_ref[:, lo:hi] += jnp.sum(dout * q, axis=0, keepdims=True)
            dq = (dout * sc).astype(BF16)
            dwp_ref[g] += lax.dot_general(pb16, dq, _DOT_DIMS["tn"], preferred_element_type=F32)
            dpool = lax.dot_general(dq, wpb, _DOT_DIMS["nt"], preferred_element_type=F32)
            ndq = (ndp_ref[:, lo:hi] * sc * keep_next).astype(BF16)
            ndpool = lax.dot_general(ndq, wpb, _DOT_DIMS["nt"], preferred_element_type=F32)
            s = jnp.concatenate([dpool / cnt, ndpool * (1.0 / w)], axis=0)
            d = 1
            while d < w:
                s = s + _shift_up(s, d)
                d *= 2
            du_ref[:, 2 * dc + lo:2 * dc + hi] = (s[:ts, :] - dpool).astype(BF16)

    return pl.pallas_call(
        body, name="mixer_e_bwd_mix", grid=(t // ts,),
        in_specs=[_cur(ts, dc, 0), _next_halo(ts, CONV_HALO, dc, 0, t),
                  _cur(ts, dc, 1), _next_halo(ts, POOL_HALO, dc, 1, t),
                  _cur(ts, dc, 0), _cur(ts, dc, 1), _cur(ts, dc, 2),
                  _prev_halo(ts, CONV_HALO, dc, 0), _prev_halo(ts, CONV_HALO, dc, 1), _prev_halo(ts, POOL_HALO, dc, 2),
                  _full(conv_w.shape), _full(w_pool.shape), _full(scale.shape)],
        out_specs=[_cur(ts, 3 * dc, 0), _full(conv_w.shape), _full(w_pool.shape), _full(scale.shape)],
        out_shape=[jax.ShapeDtypeStruct((t, 3 * dc), BF16), jax.ShapeDtypeStruct(conv_w.shape, F32),
                   jax.ShapeDtypeStruct(w_pool.shape, F32), jax.ShapeDtypeStruct(scale.shape, F32)],
        compiler_params=_cparams("arbitrary"),
    )(da2, da2, dcat, dcat, u, u, u, u, u, u, conv_w, w_pool, scale)


def _mixer_o_fwd(u, conv_w, seq, ts):
    t = u.shape[0]
    d = conv_w.shape[1]
    taps = conv_w.shape[0]
    ns = seq // ts

    def body(gb_ref, gc_ref, v_ref, pgc_ref, pv_ref, cw_ref, y_ref):
        keep_prev = jnp.where(pl.program_id(0) % ns == 0, 0.0, 1.0)
        ext = jnp.concatenate([pgc_ref[...] * pv_ref[...] * keep_prev, gc_ref[...] * v_ref[...]], axis=0)
        cc = jnp.zeros_like(ext)
        for k in range(taps):
            cc = cc + cw_ref[k:k + 1, :] * _shift_down(ext, taps - 1 - k)
        y_ref[...] = (gb_ref[...] * cc[SHORT_HALO:, :]).astype(BF16)

    return pl.pallas_call(
        body, name="mixer_o_fwd", grid=(t // ts,),
        in_specs=[_cur(ts, d, 0), _cur(ts, d, 1), _cur(ts, d, 2),
                  _prev_halo(ts, SHORT_HALO, d, 1), _prev_halo(ts, SHORT_HALO, d, 2), _full(conv_w.shape)],
        out_specs=_cur(ts, d, 0),
        out_shape=jax.ShapeDtypeStruct((t, d), BF16),
        compiler_params=_cparams("parallel"),
    )(u, u, u, u, u, conv_w)


def _mixer_o_bwd(dy, u, conv_w, seq, ts):
    t = u.shape[0]
    d = conv_w.shape[1]
    taps = conv_w.shape[0]
    ns = seq // ts

    def body(dy_ref, ndy_ref, gb_ref, gc_ref, v_ref, pgc_ref, pv_ref, ngb_ref, cw_ref, du_ref, dcw_ref):
        i = pl.program_id(0)
        keep_prev = jnp.where(i % ns == 0, 0.0, 1.0)
        keep_next = jnp.where(i % ns == ns - 1, 0.0, 1.0)

        @pl.when(i == 0)
        def _():
            dcw_ref[...] = jnp.zeros_like(dcw_ref)

        gb, gc, v, dyv = gb_ref[...], gc_ref[...], v_ref[...], dy_ref[...]
        ext = jnp.concatenate([pgc_ref[...] * pv_ref[...] * keep_prev, gc * v], axis=0)
        dcc = dyv * gb
        ext_d = jnp.concatenate([dcc, ndy_ref[...] * ngb_ref[...] * keep_next], axis=0)
        cc = jnp.zeros_like(ext)
        dcv = jnp.zeros_like(ext_d)
        for k in range(taps):
            sh = taps - 1 - k
            shifted = _shift_down(ext, sh)
            cc = cc + cw_ref[k:k + 1, :] * shifted
            dcw_ref[k:k + 1, :] += jnp.sum(dcc * shifted[SHORT_HALO:, :], axis=0, keepdims=True)
            dcv = dcv + cw_ref[k:k + 1, :] * _shift_up(ext_d, sh)
        dcv = dcv[:ts, :]
        du_ref[:, 0:d] = (dyv * cc[SHORT_HALO:, :]).astype(BF16)
        du_ref[:, d:2 * d] = (dcv * v).astype(BF16)
        du_ref[:, 2 * d:3 * d] = (dcv * gc).astype(BF16)

    return pl.pallas_call(
        body, name="mixer_o_bwd", grid=(t // ts,),
        in_specs=[_cur(ts, d, 0), _next_halo(ts, SHORT_HALO, d, 0, t),
                  _cur(ts, d, 0), _cur(ts, d, 1), _cur(ts, d, 2),
                  _prev_halo(ts, SHORT_HALO, d, 1), _prev_halo(ts, SHORT_HALO, d, 2),
                  _next_halo(ts, SHORT_HALO, d, 0, t), _full(conv_w.shape)],
        out_specs=[_cur(ts, 3 * d, 0), _full(conv_w.shape)],
        out_shape=[jax.ShapeDtypeStruct((t, 3 * d), BF16), jax.ShapeDtypeStruct(conv_w.shape, F32)],
        compiler_params=_cparams("arbitrary"),
    )(dy, dy, u, u, u, u, u, u, conv_w)


def _cast_into_full(name, mat, w, chip):
    tr = _pick(mat.sr, (512, 256, 128, 64, 32, 16))
    per = mat.sr // tr

    def body(chip_ref, w_ref, o_ref):
        o_ref[...] = w_ref[...].astype(BF16)

    if mat.kind == "col":
        o_spec = pl.BlockSpec((tr, mat.sc), lambda i, chip_ref: (i, chip_ref[0]))
    else:
        o_spec = pl.BlockSpec((tr, mat.sc), lambda i, chip_ref: (chip_ref[0] * per + i, 0))
    return pl.pallas_call(
        body, name=name,
        grid_spec=pltpu.PrefetchScalarGridSpec(
            num_scalar_prefetch=1, grid=(per,),
            in_specs=[pl.BlockSpec((tr, mat.sc), lambda i, chip_ref: (i, 0))], out_specs=o_spec),
        out_shape=jax.ShapeDtypeStruct(mat.full_shape, BF16),
        compiler_params=_cparams("parallel"),
    )(chip, w)


def _adamw(name, w, g, m, v):
    r, c = w.shape
    tr = _pick(r, (256, 128, 64, 32, 16, 8)) if c > 1024 else _pick(r, (512, 256, 128, 64, 32, 16, 8))
    bc1 = 1.0 - ADAM_B1 ** ADAM_STEP
    bc2 = 1.0 - ADAM_B2 ** ADAM_STEP

    def body(w_ref, g_ref, m_ref, v_ref, d_ref, mo_ref, vo_ref):
        gv = g_ref[...]
        mn = ADAM_B1 * m_ref[...] + (1.0 - ADAM_B1) * gv
        vn = ADAM_B2 * v_ref[...] + (1.0 - ADAM_B2) * (gv * gv)
        mo_ref[...] = mn
        vo_ref[...] = vn
        d_ref[...] = -ADAM_LR * ((mn / bc1) / (jnp.sqrt(vn / bc2) + ADAM_EPS) + ADAM_WD * w_ref[...])

    spec = pl.BlockSpec((tr, c), lambda i: (i, 0))
    return pl.pallas_call(body, name=name, grid=(r // tr,), in_specs=[spec] * 4, out_specs=[spec] * 3,
                          out_shape=[jax.ShapeDtypeStruct((r, c), F32)] * 3, compiler_params=_cparams("parallel"))(w, g, m, v)


def _aligned(offset, multiple):
    return offset if isinstance(offset, int) else pl.multiple_of(offset, multiple)


class _Mat:
    def __init__(self, kind, shard_shape):
        self.kind = kind
        self.sr, self.sc = shard_shape
        self.full_shape = (self.sr, self.sc * N_CHIPS) if kind == "col" else (self.sr * N_CHIPS, self.sc)
        self.pr, self.pc = self.sr // 2, self.sc

    def piece(self, ref, k, h):
        if self.kind == "col":
            return ref.at[pl.ds(_aligned(h * self.pr, 16), self.pr), pl.ds(_aligned(k * self.sc, LANES), self.sc)]
        return ref.at[pl.ds(_aligned(k * self.sr + h * self.pr, 16), self.pr), :]

    def shard(self, ref, k):
        if self.kind == "col":
            return ref.at[:, pl.ds(_aligned(k * self.sc, LANES), self.sc)]
        return ref.at[pl.ds(_aligned(k * self.sr, 16), self.sr), :]

    def half(self, ref, h):
        return ref.at[pl.ds(_aligned(h * self.pr, 16), self.pr), :]


def _place():
    x, y, c = lax.axis_index("x"), lax.axis_index("y"), lax.axis_index("c")
    others = [(1 - x, y), (x, 1 - y), (1 - x, 1 - y)]
    return x, y, c, others


_ANY = pl.BlockSpec(memory_space=pl.ANY)


def _gather_weights(mats, fulls, packed_small):
    n = len(mats)

    def body(*refs):
        small_ref = refs[n]
        full_refs, small_all = refs[n + 1:2 * n + 1], refs[2 * n + 1]
        send_sems, recv_sems, local_sem = refs[2 * n + 2:]
        x, y, c, others = _place()
        me_k = 2 * x + y
        sibling = (x, y, 1 - c)

        def remote(src, dst, sem, to):
            return pltpu.make_async_remote_copy(src_ref=src, dst_ref=dst, send_sem=send_sems.at[sem],
                                                recv_sem=recv_sems.at[sem], device_id=to, device_id_type=MESH_ID)

        local = [pltpu.make_async_copy(small_ref, small_all.at[me_k], local_sem)]
        for cp in local:
            cp.start()
        sent = []
        for j, (ox, oy) in enumerate(others):
            sent.append(remote(small_ref, small_all.at[me_k], 6 * n + j, (ox, oy, c)))
        for m in range(n):
            mine = mats[m].piece(full_refs[m], me_k, c)
            for j, (ox, oy) in enumerate(others):
                sent.append(remote(mine, mine, 6 * m + j, (ox, oy, c)))
        for cp in sent:
            cp.start()
        passed = []
        for m in range(n):
            for j, (ox, oy) in enumerate(others):
                got = mats[m].piece(full_refs[m], 2 * ox + oy, c)
                remote(got, got, 6 * m + j, sibling).wait_recv()
                fwd = remote(got, got, 6 * m + 3 + j, sibling)
                fwd.start()
                passed.append(fwd)
        for j, (ox, oy) in enumerate(others):
            got = small_all.at[2 * ox + oy]
            remote(got, got, 6 * n + j, sibling).wait_recv()
        for m in range(n):
            for j, (ox, oy) in enumerate(others):
                got = mats[m].piece(full_refs[m], 2 * ox + oy, 1 - c)
                remote(got, got, 6 * m + 3 + j, sibling).wait_recv()
        for cp in sent + passed:
            cp.wait_send()
        for cp in local:
            cp.wait()

    n_sems = 6 * n + 3
    outs = pl.pallas_call(
        body, name="gather_weights",
        in_specs=[_ANY] * (n + 1), out_specs=[_ANY] * (n + 1),
        out_shape=[jax.ShapeDtypeStruct(mt.full_shape, BF16) for mt in mats]
        + [jax.ShapeDtypeStruct((N_CHIPS,) + packed_small.shape, F32)],
        input_output_aliases={m: m for m in range(n)},
        scratch_shapes=[pltpu.SemaphoreType.DMA((n_sems,)), pltpu.SemaphoreType.DMA((n_sems,)), pltpu.SemaphoreType.DMA],
    )(*fulls, packed_small)
    return outs[:n], outs[n]


def _exchange_halves(mats, grads, packed_small):
    n = len(mats)

    def body(*refs):
        g_refs, small_ref = refs[:n], refs[n]
        land_refs, small_all = refs[n + 1:2 * n + 1], refs[2 * n + 1]
        send_sems, recv_sems, local_sem = refs[2 * n + 2:]
        x, y, c, _ = _place()
        me = 4 * x + 2 * y + c
        sibling = (x, y, 1 - c)

        def remote(src, dst, sem, to):
            return pltpu.make_async_remote_copy(src_ref=src, dst_ref=dst, send_sem=send_sems.at[sem],
                                                recv_sem=recv_sems.at[sem], device_id=to, device_id_type=MESH_ID)

        mine = pltpu.make_async_copy(small_ref, small_all.at[me], local_sem)
        mine.start()
        sent = []
        flips = [(fx, fy, fc) for fx in (0, 1) for fy in (0, 1) for fc in (0, 1) if (fx, fy, fc) != (0, 0, 0)]
        for j, (fx, fy, fc) in enumerate(flips):
            peer = (x ^ fx, y ^ fy, c ^ fc)
            sent.append(remote(small_ref, small_all.at[me], N_CHIPS * n + j, peer))
        for m in range(n):
            for k in range(N_CHIPS):
                sent.append(remote(mats[m].piece(g_refs[m], k, 1 - c), land_refs[m].at[k], N_CHIPS * m + k, sibling))
        for cp in sent:
            cp.start()
        for j, (fx, fy, fc) in enumerate(flips):
            got = small_all.at[4 * (x ^ fx) + 2 * (y ^ fy) + (c ^ fc)]
            remote(got, got, N_CHIPS * n + j, sibling).wait_recv()
        for m in range(n):
            for k in range(N_CHIPS):
                got = land_refs[m].at[k]
                remote(got, got, N_CHIPS * m + k, sibling).wait_recv()
        for cp in sent:
            cp.wait_send()
        mine.wait()

    n_sems = N_CHIPS * n + N_DEV - 1
    outs = pl.pallas_call(
        body, name="exchange_halves",
        in_specs=[_ANY] * (n + 1), out_specs=[_ANY] * (n + 1),
        out_shape=[jax.ShapeDtypeStruct((N_CHIPS, mt.pr, mt.pc), BF16) for mt in mats]
        + [jax.ShapeDtypeStruct((N_DEV,) + packed_small.shape, F32)],
        scratch_shapes=[pltpu.SemaphoreType.DMA((n_sems,)), pltpu.SemaphoreType.DMA((n_sems,)), pltpu.SemaphoreType.DMA],
    )(*grads, packed_small)
    return outs[:n], outs[n]


def _add_halves(name, mat, grad, landed, core):
    tr = _pick(mat.pr, (256, 128, 64, 32, 16))
    per = mat.pr // tr

    def body(core_ref, g_ref, l_ref, o_ref):
        o_ref[...] = (g_ref[...].astype(F32) + l_ref[...].astype(F32)).astype(BF16)

    if mat.kind == "col":
        g_spec = pl.BlockSpec((tr, mat.pc), lambda k, r, core_ref: (core_ref[0] * per + r, k))
    else:
        g_spec = pl.BlockSpec((tr, mat.pc), lambda k, r, core_ref: ((2 * k + core_ref[0]) * per + r, 0))
    p_spec = pl.BlockSpec((None, tr, mat.pc), lambda k, r, core_ref: (k, r, 0))
    return pl.pallas_call(
        body, name=name,
        grid_spec=pltpu.PrefetchScalarGridSpec(num_scalar_prefetch=1, grid=(N_CHIPS, per),
                                               in_specs=[g_spec, p_spec], out_specs=p_spec),
        out_shape=jax.ShapeDtypeStruct((N_CHIPS, mat.pr, mat.pc), BF16),
        compiler_params=_cparams("parallel", "parallel"),
    )(core, grad, landed)


def _scatter_pieces(mats, partials):
    n = len(mats)

    def body(*refs):
        p_refs = refs[:n]
        land_refs = refs[n:2 * n]
        send_sems, recv_sems = refs[2 * n:]
        x, y, c, others = _place()
        me_k = 2 * x + y

        def remote(src, dst, sem, to):
            return pltpu.make_async_remote_copy(src_ref=src, dst_ref=dst, send_sem=send_sems.at[sem],
                                                recv_sem=recv_sems.at[sem], device_id=to, device_id_type=MESH_ID)

        sent = []
        for m in range(n):
            for j, (ox, oy) in enumerate(others):
                sent.append(remote(p_refs[m].at[2 * ox + oy], land_refs[m].at[me_k], 3 * m + j, (ox, oy, c)))
        for cp in sent:
            cp.start()
        for m in range(n):
            for j, (ox, oy) in enumerate(others):
                got = land_refs[m].at[2 * ox + oy]
                remote(got, got, 3 * m + j, (ox, oy, c)).wait_recv()
        for cp in sent:
            cp.wait_send()

    outs = pl.pallas_call(
        body, name="scatter_pieces",
        in_specs=[_ANY] * n, out_specs=[_ANY] * n,
        out_shape=[jax.ShapeDtypeStruct((N_CHIPS, mt.pr, mt.pc), BF16) for mt in mats],
        scratch_shapes=[pltpu.SemaphoreType.DMA((3 * n,)), pltpu.SemaphoreType.DMA((3 * n,))],
    )(*partials)
    return outs


def _sum_chips(name, mat, partial, landed, slots, layer=None, stack=None, n_layers=1):
    tr = _pick(mat.pr, (256, 128, 64, 32, 16))
    per = mat.pr // tr

    def body(slots_ref, own_ref, a_ref, b_ref, c_ref, *rest):
        o_ref = rest[-1]
        o_ref[...] = ((own_ref[...].astype(F32) + a_ref[...].astype(F32)) + b_ref[...].astype(F32)) + c_ref[...].astype(F32)

    def slot_spec(which):
        return pl.BlockSpec((None, tr, mat.pc), lambda r, slots_ref: (slots_ref[which], r, 0))

    in_specs = [slot_spec(0), slot_spec(1), slot_spec(2), slot_spec(3)]
    operands = [slots, partial, landed, landed, landed]
    aliases = {}
    if layer is None:
        o_spec = pl.BlockSpec((tr, mat.pc), lambda r, slots_ref: (slots_ref[4] * per + r, 0))
        out_shape = jax.ShapeDtypeStruct((mat.sr, mat.sc), F32)
    else:
        o_spec = pl.BlockSpec((None, tr, mat.pc), lambda r, slots_ref: (layer, slots_ref[4] * per + r, 0))
        out_shape = jax.ShapeDtypeStruct((n_layers, mat.sr, mat.sc), F32)
        if stack is not None:
            in_specs.append(_ANY)
            operands.append(stack)
            aliases = {len(operands) - 1: 0}
    return pl.pallas_call(
        body, name=name,
        grid_spec=pltpu.PrefetchScalarGridSpec(num_scalar_prefetch=1, grid=(per,), in_specs=in_specs, out_specs=o_spec),
        out_shape=out_shape, input_output_aliases=aliases,
        compiler_params=_cparams("parallel"),
    )(*operands)


def _share_pieces(mats, shards, groups):
    n = len(mats)
    n_out = len(groups)

    def body(*refs):
        out_refs = refs[n_out:2 * n_out]
        send_sems, recv_sems = refs[2 * n_out:]
        x, y, c, _ = _place()
        sibling = (x, y, 1 - c)
        sent, waits = [], []
        for o, members in enumerate(groups):
            for l, m in enumerate(members):
                dst = out_refs[o].at[l] if len(members) > 1 else out_refs[o]
                mine = mats[m].half(dst, c)
                sent.append(pltpu.make_async_remote_copy(src_ref=mine, dst_ref=mine, send_sem=send_sems.at[m],
                                                         recv_sem=recv_sems.at[m], device_id=sibling, device_id_type=MESH_ID))
                theirs = mats[m].half(dst, 1 - c)
                waits.append(pltpu.make_async_remote_copy(src_ref=theirs, dst_ref=theirs, send_sem=send_sems.at[m],
                                                          recv_sem=recv_sems.at[m], device_id=sibling, device_id_type=MESH_ID))
        for cp in sent:
            cp.start()
        for cp in waits:
            cp.wait_recv()
        for cp in sent:
            cp.wait_send()

    return pl.pallas_call(
        body, name="share_pieces",
        in_specs=[_ANY] * n_out, out_specs=[_ANY] * n_out,
        out_shape=[jax.ShapeDtypeStruct(s.shape, F32) for s in shards],
        input_output_aliases={o: o for o in range(n_out)},
        scratch_shapes=[pltpu.SemaphoreType.DMA((n,)), pltpu.SemaphoreType.DMA((n,))],
    )(*shards)


def _sum_devices(stacked):
    nd, r, c = stacked.shape

    def body(s_ref, o_ref):
        s = s_ref[0]
        for k in range(1, nd):
            s = s + s_ref[k]
        o_ref[...] = s

    return pl.pallas_call(
        body, name="sum_small_grads", grid=(1,),
        in_specs=[pl.BlockSpec((nd, r, c), lambda i: (0, 0, 0))],
        out_specs=pl.BlockSpec((r, c), lambda i: (0, 0)),
        out_shape=jax.ShapeDtypeStruct((r, c), F32),
        compiler_params=_cparams("arbitrary"),
    )(stacked)


def _pack(arrs):
    flat = jnp.concatenate([a.reshape(-1) for a in arrs])
    rows = -(-flat.shape[0] // (8 * LANES)) * 8
    return jnp.pad(flat, (0, rows * LANES - flat.shape[0])).reshape(rows, LANES)


def _unpack(packed, shapes):
    flat = packed.reshape(-1)
    out, at = [], 0
    for s in shapes:
        size = 1
        for dim in s:
            size *= dim
        out.append(flat[at:at + size].reshape(s))
        at += size
    return out


def kernel(x, mix_norm_e, w_in_e, conv_w_e, conv_b_e, ln_g_e, ln_b_e, w_pool_e, pool_scale_e, w_out_e, mix_norm_o, w_in_o, conv_w_o, w_out_o, ffn_norm, w_gate, w_up, w_down, final_norm, loss_target, m_mix_norm_e, m_w_in_e, m_conv_w_e, m_conv_b_e, m_ln_g_e, m_ln_b_e, m_w_pool_e, m_pool_scale_e, m_w_out_e, m_mix_norm_o, m_w_in_o, m_conv_w_o, m_w_out_o, m_ffn_norm, m_w_gate, m_w_up, m_w_down, m_final_norm, v_mix_norm_e, v_w_in_e, v_conv_w_e, v_conv_b_e, v_ln_g_e, v_ln_b_e, v_w_pool_e, v_pool_scale_e, v_w_out_e, v_mix_norm_o, v_w_in_o, v_conv_w_o, v_w_out_o, v_ffn_norm, v_w_gate, v_w_up, v_w_down, v_final_norm):
    bsz, seq, d = x.shape
    t = bsz * seq
    depth = ffn_norm.shape[0]
    assert depth == 2 and conv_b_e.shape[1] == pool_scale_e.shape[1]
    ts = _pick(seq, (256, 128, 64, 32))
    me_k = 2 * lax.axis_index("x") + lax.axis_index("y")
    core = lax.axis_index("c").astype(jnp.int32).reshape(1)

    mat_src = [("col", w_in_e[0]), ("row", w_out_e[0]), ("col", w_gate[0]), ("col", w_up[0]), ("row", w_down[0]),
               ("col", w_in_o[0]), ("row", w_out_o[0]), ("col", w_gate[1]), ("col", w_up[1]), ("row", w_down[1])]
    mats = [_Mat(kind, w.shape) for kind, w in mat_src]
    chip = me_k.astype(jnp.int32).reshape(1)
    own16 = [_cast_into_full("cast_w%d" % i, mats[i], w, chip) for i, (_, w) in enumerate(mat_src)]
    small_shards = [conv_w_e[0], w_pool_e[0], mix_norm_o, conv_w_o[0]]
    fulls, small_all = _gather_weights(mats, own16, _pack(small_shards))
    (W_in_e, W_out_e, W_gate0, W_up0, W_down0, W_in_o, W_out_o, W_gate1, W_up1, W_down1) = fulls
    per_chip = [_unpack(small_all[k], [s.shape for s in small_shards]) for k in range(N_CHIPS)]
    conv_w_e_f = jnp.concatenate([p[0] for p in per_chip], axis=1)
    w_pool_f = jnp.concatenate([p[1] for p in per_chip], axis=1)
    mix_norm_o_f = jnp.concatenate([p[2] for p in per_chip], axis=1)
    conv_w_o_f = jnp.concatenate([p[3] for p in per_chip], axis=1)
    W_gate, W_up, W_down = (W_gate0, W_gate1), (W_up0, W_up1), (W_down0, W_down1)

    h0 = x.reshape(t, d)
    target = loss_target.reshape(t, d)

    def ffn_fwd(l, h):
        n = _rms_fwd("ffn%d_norm" % l, h, ffn_norm[l:l + 1])
        gt, up, act = _mm("ffn%d_gate_up" % l, [(n, W_gate[l]), (n, W_up[l])], "nn", [BF16] * 3, _ep_swiglu,
                          acc_of=(0, 1), tm=1024, tn=512, tk=2048)
        (h_out,) = _mm("ffn%d_down" % l, [(act, W_down[l])], "nn", [F32], _ep_residual, extras=(h,),
                       tm=512, tn=1024, tk=2816)
        return n, gt, up, act, h_out

    n1 = _rms_fwd("mix0_norm", h0, mix_norm_e)
    (u_e,) = _mm("mix0_in", [(n1, W_in_e)], "nn", [F32], _ep_store, tm=1024, tn=1024, tk=2048)
    a2, cat = _mixer_e_fwd(u_e, conv_w_e_f, conv_b_e, ln_g_e, ln_b_e, w_pool_f, pool_scale_e, seq, ts)
    (h1,) = _mm("mix0_out", [(cat, W_out_e)], "nn", [F32], _ep_residual, extras=(h0,), tm=1024, tn=1024, tk=2048)
    n2, gt0, up0, act0, h2 = ffn_fwd(0, h1)
    n3 = _rms_fwd("mix1_norm", h2, mix_norm_o_f)
    (u_o,) = _mm("mix1_in", [(n3, W_in_o)], "nn", [F32], _ep_store, tm=1024, tn=1024, tk=2048)
    y_o = _mixer_o_fwd(u_o, conv_w_o_f, seq, ts)
    (h3,) = _mm("mix1_out", [(y_o, W_out_o)], "nn", [F32], _ep_residual, extras=(h2,), tm=1024, tn=1024, tk=2048)
    n4, gt1, up1, act1, h4 = ffn_fwd(1, h3)
    loss_part, dh4, dh4b, d_final_norm = _loss_head(h4, final_norm.reshape(1, d), target)
    loss = lax.psum(loss_part[0, 0], AXES)

    def ffn_bwd(l, dh, dhb, h_in, n, gt, up, act):
        dgt, dup = _mm("ffn%d_dact" % l, [(dhb, W_down[l])], "nt", [BF16, BF16], _ep_swiglu_bwd, extras=(gt, up),
                       tm=1024, tn=512, tk=2048)
        (dW_down,) = _mm("ffn%d_dw_down" % l, [(act, dhb)], "tn", [BF16], _ep_store, tm=512, tn=1024, tk=2048)
        (dn,) = _mm("ffn%d_dn" % l, [(dgt, W_gate[l]), (dup, W_up[l])], "nt", [F32], _ep_store,
                    tm=1024, tn=1024, tk=1408)
        (dW_gate,) = _mm("ffn%d_dw_gate" % l, [(n, dgt)], "tn", [BF16], _ep_store, tm=1024, tn=512, tk=2048)
        (dW_up,) = _mm("ffn%d_dw_up" % l, [(n, dup)], "tn", [BF16], _ep_store, tm=1024, tn=512, tk=2048)
        dh_in, dh_in_b, dg = _rms_bwd("ffn%d_norm_bwd" % l, dn, h_in, ffn_norm[l:l + 1], dh)
        return dh_in, dh_in_b, dg, dW_gate, dW_up, dW_down

    dh3, dh3b, d_ffn_norm1, dW_gate1, dW_up1, dW_down1 = ffn_bwd(1, dh4, dh4b, h3, n4, gt1, up1, act1)

    (dy_o,) = _mm("mix1_dy", [(dh3b, W_out_o)], "nt", [F32], _ep_store, tm=1024, tn=1024, tk=2048)
    (dW_out_o,) = _mm("mix1_dw_out", [(y_o, dh3b)], "tn", [BF16], _ep_store, tm=1024, tn=1024, tk=2048)
    du_o, d_conv_w_o = _mixer_o_bwd(dy_o, u_o, conv_w_o_f, seq, ts)
    (dn3,) = _mm("mix1_dn", [(du_o, W_in_o)], "nt", [F32], _ep_store, tm=1024, tn=1024, tk=2048)
    (dW_in_o,) = _mm("mix1_dw_in", [(n3, du_o)], "tn", [BF16], _ep_store, tm=1024, tn=1024, tk=2048)
    dh2, dh2b, d_mix_norm_o = _rms_bwd("mix1_norm_bwd", dn3, h2, mix_norm_o_f, dh3)

    dh1, dh1b, d_ffn_norm0, dW_gate0, dW_up0, dW_down0 = ffn_bwd(0, dh2, dh2b, h1, n2, gt0, up0, act0)

    (dcat,) = _mm("mix0_dcat", [(dh1b, W_out_e)], "nt", [F32], _ep_store, tm=1024, tn=1024, tk=2048)
    (dW_out_e,) = _mm("mix0_dw_out", [(cat, dh1b)], "tn", [BF16], _ep_store, tm=1024, tn=1024, tk=2048)
    da2, d_ln_g, d_ln_b, d_conv_b = _mixer_e_bwd_norm(dcat, a2, ln_g_e, ln_b_e, ts)
    du_e, d_conv_w_e, d_w_pool, d_pool_scale = _mixer_e_bwd_mix(da2, dcat, u_e, conv_w_e_f, w_pool_f, pool_scale_e, seq, ts)
    (dn1,) = _mm("mix0_dn", [(du_e, W_in_e)], "nt", [F32], _ep_store, tm=1024, tn=1024, tk=2048)
    (dW_in_e,) = _mm("mix0_dw_in", [(n1, du_e)], "tn", [BF16], _ep_store, tm=1024, tn=1024, tk=2048)
    dx, _, d_mix_norm_e = _rms_bwd("mix0_norm_bwd", dn1, h0, mix_norm_e, dh1)

    grads16 = [dW_in_e, dW_out_e, dW_gate0, dW_up0, dW_down0, dW_in_o, dW_out_o, dW_gate1, dW_up1, dW_down1]
    d_ffn_norm = jnp.concatenate([d_ffn_norm0, d_ffn_norm1], axis=0)
    small_partials = [d_mix_norm_e, d_conv_w_e, d_conv_b, d_ln_g, d_ln_b, d_w_pool, d_pool_scale, d_mix_norm_o,
                      d_conv_w_o, d_ffn_norm, d_final_norm]
    landed, small_stack = _exchange_halves(mats, grads16, _pack(small_partials))
    partials = [_add_halves("add_halves%d" % m, mats[m], grads16[m], landed[m], core) for m in range(len(mats))]
    scattered = _scatter_pieces(mats, partials)
    xi, yi, ci = lax.axis_index("x"), lax.axis_index("y"), lax.axis_index("c")
    slots = jnp.stack([me_k, 2 * (1 - xi) + yi, 2 * xi + (1 - yi), 2 * (1 - xi) + (1 - yi), ci]).astype(jnp.int32)
    groups = [(0,), (1,), (5,), (6,), (2, 7), (3, 8), (4, 9)]
    halves = []
    for members in groups:
        if len(members) == 1:
            m = members[0]
            halves.append(_sum_chips("sum_chips%d" % m, mats[m], partials[m], scattered[m], slots))
        else:
            stack = None
            for l, m in enumerate(members):
                stack = _sum_chips("sum_chips%d" % m, mats[m], partials[m], scattered[m], slots, layer=l, stack=stack,
                                   n_layers=len(members))
            halves.append(stack)
    g_w_in_e, g_w_out_e, g_w_in_o, g_w_out_o, g_w_gate, g_w_up, g_w_down = _share_pieces(mats, halves, groups)
    small_sum = _unpack(_sum_devices(small_stack), [s.shape for s in small_partials])
    (g_mix_norm_e, g_conv_w_e_f, g_conv_b, g_ln_g, g_ln_b, g_w_pool_f, g_pool_scale, g_mix_norm_o_f, g_conv_w_o_f,
     g_ffn_norm, g_final_norm) = small_sum

    def my_shard(full, axis):
        size = full.shape[axis] // N_CHIPS
        return lax.dynamic_slice_in_dim(full, me_k * size, size, axis)

    g_conv_w_e = my_shard(g_conv_w_e_f, 1)
    g_w_pool = my_shard(g_w_pool_f, 1)
    g_mix_norm_o = my_shard(g_mix_norm_o_f, 1)
    g_conv_w_o = my_shard(g_conv_w_o_f, 1)

    grad = {
        "mix_norm_e": g_mix_norm_e, "w_in_e": g_w_in_e[None], "conv_w_e": g_conv_w_e[None], "conv_b_e": g_conv_b,
        "ln_g_e": g_ln_g, "ln_b_e": g_ln_b, "w_pool_e": g_w_pool[None], "pool_scale_e": g_pool_scale,
        "w_out_e": g_w_out_e[None], "mix_norm_o": g_mix_norm_o, "w_in_o": g_w_in_o[None], "conv_w_o": g_conv_w_o[None],
        "w_out_o": g_w_out_o[None], "ffn_norm": g_ffn_norm, "w_gate": g_w_gate, "w_up": g_w_up, "w_down": g_w_down,
        "final_norm": g_final_norm.reshape(final_norm.shape),
    }
    weights = dict(mix_norm_e=mix_norm_e, w_in_e=w_in_e, conv_w_e=conv_w_e, conv_b_e=conv_b_e, ln_g_e=ln_g_e, ln_b_e=ln_b_e,
                   w_pool_e=w_pool_e, pool_scale_e=pool_scale_e, w_out_e=w_out_e, mix_norm_o=mix_norm_o, w_in_o=w_in_o,
                   conv_w_o=conv_w_o, w_out_o=w_out_o, ffn_norm=ffn_norm, w_gate=w_gate, w_up=w_up, w_down=w_down,
                   final_norm=final_norm)
    mom1 = dict(mix_norm_e=m_mix_norm_e, w_in_e=m_w_in_e, conv_w_e=m_conv_w_e, conv_b_e=m_conv_b_e, ln_g_e=m_ln_g_e,
                ln_b_e=m_ln_b_e, w_pool_e=m_w_pool_e, pool_scale_e=m_pool_scale_e, w_out_e=m_w_out_e, mix_norm_o=m_mix_norm_o,
                w_in_o=m_w_in_o, conv_w_o=m_conv_w_o, w_out_o=m_w_out_o, ffn_norm=m_ffn_norm, w_gate=m_w_gate, w_up=m_w_up,
                w_down=m_w_down, final_norm=m_final_norm)
    mom2 = dict(mix_norm_e=v_mix_norm_e, w_in_e=v_w_in_e, conv_w_e=v_conv_w_e, conv_b_e=v_conv_b_e, ln_g_e=v_ln_g_e,
                ln_b_e=v_ln_b_e, w_pool_e=v_w_pool_e, pool_scale_e=v_pool_scale_e, w_out_e=v_w_out_e, mix_norm_o=v_mix_norm_o,
                w_in_o=v_w_in_o, conv_w_o=v_conv_w_o, w_out_o=v_w_out_o, ffn_norm=v_ffn_norm, w_gate=v_w_gate, w_up=v_w_up,
                w_down=v_w_down, final_norm=v_final_norm)
    names = list(weights)

    big = ("w_in_e", "w_out_e", "w_in_o", "w_out_o", "w_gate", "w_up", "w_down")
    delta, new_m, new_v = {}, {}, {}
    for nm in big:
        shape = weights[nm].shape
        as2d = lambda a: a.reshape(shape[0] * shape[1], shape[2])
        dl, mn, vn = _adamw("adamw_" + nm, as2d(weights[nm]), as2d(grad[nm]), as2d(mom1[nm]), as2d(mom2[nm]))
        delta[nm], new_m[nm], new_v[nm] = dl.reshape(shape), mn.reshape(shape), vn.reshape(shape)
    small = [nm for nm in names if nm not in big]
    shapes = [weights[nm].shape for nm in small]
    dl, mn, vn = _adamw("adamw_small", _pack([weights[nm] for nm in small]), _pack([grad[nm] for nm in small]),
                        _pack([mom1[nm] for nm in small]), _pack([mom2[nm] for nm in small]))
    for nm, a, b, c_ in zip(small, _unpack(dl, shapes), _unpack(mn, shapes), _unpack(vn, shapes)):
        delta[nm], new_m[nm], new_v[nm] = a, b, c_

    grad_x = dx.reshape(bsz, seq, d)
    return (loss, grad_x, *[grad[nm] for nm in names], *[delta[nm] for nm in names],
            *[new_m[nm] for nm in names], *[new_v[nm] for nm in names])
```

```python
import jax
import jax.numpy as jnp
from jax import lax
from jax.experimental import pallas as pl
from jax.experimental.pallas import tpu as pltpu

F32 = jnp.float32
BF16 = jnp.bfloat16
MESH_ID = pl.DeviceIdType.MESH
AXES = ("x", "y", "c")
N_CHIPS = 4
N_DEV = 8

EPS = 1e-6
POOL_WINDOWS = (2, 4, 8, 16)
ADAM_LR, ADAM_B1, ADAM_B2, ADAM_EPS, ADAM_WD, ADAM_STEP = 0.001, 0.9, 0.999, 1e-08, 0.01, 10

LANES = 128
CONV_HALO = 32
POOL_HALO = 16
SHORT_HALO = 8
V7X_VMEM_LIMIT = 56 * 1024 * 1024


def _cparams(*sem):
    return pltpu.CompilerParams(dimension_semantics=sem if sem else None, vmem_limit_bytes=V7X_VMEM_LIMIT)


def _pick(dim, prefs):
    for p in prefs:
        if p <= dim and dim % p == 0:
            return p
    return dim


def _sigmoid(x):
    return jax.nn.sigmoid(x)


_DOT_DIMS = {
    "nn": (((1,), (0,)), ((), ())),
    "nt": (((1,), (1,)), ((), ())),
    "tn": (((0,), (0,)), ((), ())),
}


def _mm(name, pairs, mode, out_dtypes, epilogue, extras=(), acc_of=None, tm=512, tn=512, tk=2048, after=()):
    a0, b0 = pairs[0]
    if mode == "nn":
        (m, k), n = a0.shape, b0.shape[1]
    elif mode == "nt":
        (m, k), n = a0.shape, b0.shape[0]
    else:
        (k, m), n = a0.shape, b0.shape[1]
    tm = _pick(m, (tm, 512, 256, 128, 64, 32, 16, 8))
    tn = _pick(n, (tn, 512, 256, 128))
    tk = _pick(k, (tk, 2048, 1024, 512, 256, 128))
    nk = k // tk
    n_pairs = len(pairs)
    acc_of = tuple(acc_of) if acc_of is not None else (0,) * n_pairs
    n_acc = max(acc_of) + 1
    n_ex, n_out = len(extras), len(out_dtypes)
    dims = _DOT_DIMS[mode]

    def body(*refs):
        a_refs = refs[:n_pairs]
        b_refs = refs[n_pairs:2 * n_pairs]
        e_refs = refs[2 * n_pairs:2 * n_pairs + n_ex]
        first_out = 2 * n_pairs + n_ex + len(after)
        o_refs = refs[first_out:first_out + n_out]
        acc_refs = refs[first_out + n_out:]

        def partial_sums():
            sums = [None] * n_acc
            for p in range(n_pairs):
                d = lax.dot_general(a_refs[p][...], b_refs[p][...], dims, preferred_element_type=F32)
                sums[acc_of[p]] = d if sums[acc_of[p]] is None else sums[acc_of[p]] + d
            return sums

        if nk == 1:
            epilogue(partial_sums(), e_refs, o_refs)
            return
        kk = pl.program_id(2)

        @pl.when(kk == 0)
        def _():
            for acc in acc_refs:
                acc[...] = jnp.zeros_like(acc)

        for acc, s in zip(acc_refs, partial_sums()):
            acc[...] += s

        @pl.when(kk == nk - 1)
        def _():
            epilogue([acc[...] for acc in acc_refs], e_refs, o_refs)

    if mode == "nn":
        a_spec = pl.BlockSpec((tm, tk), lambda i, j, kk: (i, kk))
        b_spec = pl.BlockSpec((tk, tn), lambda i, j, kk: (kk, j))
    elif mode == "nt":
        a_spec = pl.BlockSpec((tm, tk), lambda i, j, kk: (i, kk))
        b_spec = pl.BlockSpec((tn, tk), lambda i, j, kk: (j, kk))
    else:
        a_spec = pl.BlockSpec((tk, tm), lambda i, j, kk: (kk, i))
        b_spec = pl.BlockSpec((tk, tn), lambda i, j, kk: (kk, j))
    o_spec = pl.BlockSpec((tm, tn), lambda i, j, kk: (i, j))
    outs = pl.pallas_call(
        body,
        name=name,
        grid=(m // tm, n // tn, nk),
        in_specs=[a_spec] * n_pairs + [b_spec] * n_pairs + [o_spec] * n_ex
        + [pl.BlockSpec(memory_space=pl.ANY)] * len(after),
        out_specs=[o_spec] * n_out,
        out_shape=[jax.ShapeDtypeStruct((m, n), dt) for dt in out_dtypes],
        scratch_shapes=[pltpu.VMEM((tm, tn), F32) for _ in range(n_acc)] if nk > 1 else [],
        compiler_params=_cparams("parallel", "parallel", "arbitrary"),
    )(*[p[0] for p in pairs], *[p[1] for p in pairs], *extras, *after)
    return outs


def _ep_store(accs, ex, outs):
    outs[0][...] = accs[0].astype(outs[0].dtype)


def _ep_residual(accs, ex, outs):
    outs[0][...] = ex[0][...] + accs[0]


def _ep_swiglu(accs, ex, outs):
    g, u = accs
    outs[0][...] = g.astype(BF16)
    outs[1][...] = u.astype(BF16)
    outs[2][...] = (g * _sigmoid(g) * u).astype(BF16)


def _ep_swiglu_bwd(accs, ex, outs):
    d = accs[0]
    g = ex[0][...].astype(F32)
    u = ex[1][...].astype(F32)
    s = _sigmoid(g)
    outs[0][...] = (d * u * (s * (1.0 + g * (1.0 - s)))).astype(BF16)
    outs[1][...] = (d * (g * s)).astype(BF16)


def _rms_fwd(name, h, g):
    t, d = h.shape
    tr = _pick(t, (256, 128, 64, 32, 16, 8))

    def body(h_ref, g_ref, o_ref):
        x = h_ref[...]
        r = lax.rsqrt(jnp.mean(x * x, axis=-1, keepdims=True) + EPS)
        o_ref[...] = (x * r * g_ref[...]).astype(BF16)

    return pl.pallas_call(
        body, name=name, grid=(t // tr,),
        in_specs=[pl.BlockSpec((tr, d), lambda i: (i, 0)), pl.BlockSpec((1, d), lambda i: (0, 0))],
        out_specs=pl.BlockSpec((tr, d), lambda i: (i, 0)),
        out_shape=jax.ShapeDtypeStruct((t, d), BF16),
        compiler_params=_cparams("parallel"),
    )(h, g)


def _rms_bwd(name, dn, h, g, dres):
    t, d = h.shape
    tr = _pick(t, (256, 128, 64, 32, 16, 8))

    def body(dn_ref, h_ref, g_ref, dres_ref, dh_ref, dhb_ref, dg_ref):
        x = h_ref[...]
        r = lax.rsqrt(jnp.mean(x * x, axis=-1, keepdims=True) + EPS)
        xhat = x * r
        dnv = dn_ref[...]

        @pl.when(pl.program_id(0) == 0)
        def _():
            dg_ref[...] = jnp.zeros_like(dg_ref)

        dg_ref[...] += jnp.sum(dnv * xhat, axis=0, keepdims=True)
        dxh = dnv * g_ref[...]
        dh = dres_ref[...] + r * (dxh - xhat * jnp.mean(dxh * xhat, axis=-1, keepdims=True))
        dh_ref[...] = dh
        dhb_ref[...] = dh.astype(BF16)

    row = pl.BlockSpec((tr, d), lambda i: (i, 0))
    vec = pl.BlockSpec((1, d), lambda i: (0, 0))
    return pl.pallas_call(
        body, name=name, grid=(t // tr,),
        in_specs=[row, row, vec, row],
        out_specs=[row, row, vec],
        out_shape=[jax.ShapeDtypeStruct((t, d), F32), jax.ShapeDtypeStruct((t, d), BF16),
                   jax.ShapeDtypeStruct((1, d), F32)],
        compiler_params=_cparams("arbitrary"),
    )(dn, h, g, dres)


def _loss_head(h, g, target):
    t, d = h.shape
    tr = _pick(t, (256, 128, 64, 32, 16, 8))

    def body(h_ref, g_ref, t_ref, loss_ref, dh_ref, dhb_ref, dg_ref):
        x = h_ref[...]
        gv = g_ref[...]
        r = lax.rsqrt(jnp.mean(x * x, axis=-1, keepdims=True) + EPS)
        xhat = x * r
        err = xhat * gv - t_ref[...]

        @pl.when(pl.program_id(0) == 0)
        def _():
            dg_ref[...] = jnp.zeros_like(dg_ref)
            loss_ref[...] = jnp.zeros_like(loss_ref)

        loss_ref[...] += jnp.full(loss_ref.shape, 0.5 / d, F32) * jnp.sum(err * err)
        dy = err * (1.0 / d)
        dg_ref[...] += jnp.sum(dy * xhat, axis=0, keepdims=True)
        dxh = dy * gv
        dh = r * (dxh - xhat * jnp.mean(dxh * xhat, axis=-1, keepdims=True))
        dh_ref[...] = dh
        dhb_ref[...] = dh.astype(BF16)

    row = pl.BlockSpec((tr, d), lambda i: (i, 0))
    vec = pl.BlockSpec((1, d), lambda i: (0, 0))
    return pl.pallas_call(
        body, name="loss_head", grid=(t // tr,),
        in_specs=[row, vec, row],
        out_specs=[pl.BlockSpec((1, LANES), lambda i: (0, 0)), row, row, vec],
        out_shape=[jax.ShapeDtypeStruct((1, LANES), F32), jax.ShapeDtypeStruct((t, d), F32),
                   jax.ShapeDtypeStruct((t, d), BF16), jax.ShapeDtypeStruct((1, d), F32)],
        compiler_params=_cparams("arbitrary"),
    )(h, g, target)


def _cur(ts, width, col):
    return pl.BlockSpec((ts, width), lambda i: (i, col))


def _prev_halo(ts, halo, width, col):
    per = ts // halo
    return pl.BlockSpec((halo, width), lambda i: (jnp.maximum(i * per - 1, 0), col))


def _next_halo(ts, halo, width, col, n_rows):
    per = ts // halo
    last = n_rows // halo - 1
    return pl.BlockSpec((halo, width), lambda i: (jnp.minimum((i + 1) * per, last), col))


def _full(shape):
    nd = len(shape)
    return pl.BlockSpec(shape, lambda i: (0,) * nd)


def _shift_down(x, n):
    return x if n == 0 else pltpu.roll(x, n, 0)


def _shift_up(x, n):
    return x if n == 0 else pltpu.roll(x, x.shape[0] - n, 0)


def _pool_counts(i, ns, ts, w):
    pos = (i % ns) * ts + lax.broadcasted_iota(jnp.int32, (ts, 1), 0)
    return jnp.minimum(pos + 1, w).astype(F32)


def _pooled(cur, prev_tail, w, cnt):
    s = jnp.concatenate([prev_tail, cur], axis=0)
    d = 1
    while d < w:
        s = s + _shift_down(s, d)
        d *= 2
    return s[POOL_HALO:, :] / cnt - cur


def _mixer_e_fwd(u, conv_w, conv_b, ln_g, ln_b, w_pool, scale, seq, ts):
    t = u.shape[0]
    dc = conv_b.shape[1]
    ng, pg = w_pool.shape[0], w_pool.shape[1]
    taps = conv_w.shape[0]
    ns = seq // ts

    def body(val_ref, gate_ref, b_ref, pval_ref, pgate_ref, pb_ref, cw_ref, cb_ref, g_ref, be_ref, wp_ref, sc_ref,
             a2_ref, cat_ref):
        i = pl.program_id(0)
        keep_prev = jnp.where(i % ns == 0, 0.0, 1.0)
        a1 = val_ref[...] * _sigmoid(gate_ref[...])
        pa1 = pval_ref[...] * _sigmoid(pgate_ref[...]) * keep_prev
        ext = jnp.concatenate([pa1, a1], axis=0)
        acc = jnp.zeros_like(ext)
        for k in range(taps):
            acc = acc + cw_ref[k:k + 1, :] * _shift_down(ext, taps - 1 - k)
        a2 = acc[CONV_HALO:, :] + cb_ref[...]
        a2_ref[...] = a2
        mu = jnp.mean(a2, axis=-1, keepdims=True)
        xc = a2 - mu
        rstd = lax.rsqrt(jnp.mean(xc * xc, axis=-1, keepdims=True) + EPS)
        a3 = xc * rstd * g_ref[...] + be_ref[...]
        cat_ref[:, 0:dc] = (a3 * _sigmoid(a3)).astype(BF16)
        for g in range(ng):
            lo, hi = g * pg, (g + 1) * pg
            w = POOL_WINDOWS[g]
            p = _pooled(b_ref[:, lo:hi], pb_ref[:, lo:hi] * keep_prev, w, _pool_counts(i, ns, ts, w))
            q = jnp.dot(p.astype(BF16), wp_ref[g].astype(BF16), preferred_element_type=F32)
            cat_ref[:, dc + lo:dc + hi] = (q * sc_ref[:, lo:hi]).astype(BF16)

    return pl.pallas_call(
        body, name="mixer_e_fwd", grid=(t // ts,),
        in_specs=[_cur(ts, dc, 0), _cur(ts, dc, 1), _cur(ts, dc, 2),
                  _prev_halo(ts, CONV_HALO, dc, 0), _prev_halo(ts, CONV_HALO, dc, 1), _prev_halo(ts, POOL_HALO, dc, 2),
                  _full(conv_w.shape), _full(conv_b.shape), _full(ln_g.shape), _full(ln_b.shape),
                  _full(w_pool.shape), _full(scale.shape)],
        out_specs=[_cur(ts, dc, 0), _cur(ts, 2 * dc, 0)],
        out_shape=[jax.ShapeDtypeStruct((t, dc), F32), jax.ShapeDtypeStruct((t, 2 * dc), BF16)],
        compiler_params=_cparams("parallel"),
    )(u, u, u, u, u, u, conv_w, conv_b, ln_g, ln_b, w_pool, scale)


def _mixer_e_bwd_norm(dcat, a2, ln_g, ln_b, ts):
    t, dc = a2.shape

    def body(d_ref, a2_ref, g_ref, be_ref, da2_ref, dg_ref, db_ref, dcb_ref):
        x = a2_ref[...]
        gv = g_ref[...]
        mu = jnp.mean(x, axis=-1, keepdims=True)
        xc = x - mu
        rstd = lax.rsqrt(jnp.mean(xc * xc, axis=-1, keepdims=True) + EPS)
        xhat = xc * rstd
        a3 = xhat * gv + be_ref[...]
        sg = _sigmoid(a3)
        da3 = d_ref[...] * (sg * (1.0 + a3 * (1.0 - sg)))
        dxh = da3 * gv
        da2 = rstd * (dxh - jnp.mean(dxh, axis=-1, keepdims=True)
                      - xhat * jnp.mean(dxh * xhat, axis=-1, keepdims=True))
        da2_ref[...] = da2

        @pl.when(pl.program_id(0) == 0)
        def _():
            dg_ref[...] = jnp.zeros_like(dg_ref)
            db_ref[...] = jnp.zeros_like(db_ref)
            dcb_ref[...] = jnp.zeros_like(dcb_ref)

        dg_ref[...] += jnp.sum(da3 * xhat, axis=0, keepdims=True)
        db_ref[...] += jnp.sum(da3, axis=0, keepdims=True)
        dcb_ref[...] += jnp.sum(da2, axis=0, keepdims=True)

    vec = _full((1, dc))
    return pl.pallas_call(
        body, name="mixer_e_bwd_norm", grid=(t // ts,),
        in_specs=[_cur(ts, dc, 0), _cur(ts, dc, 0), vec, vec],
        out_specs=[_cur(ts, dc, 0), vec, vec, vec],
        out_shape=[jax.ShapeDtypeStruct((t, dc), F32)] + [jax.ShapeDtypeStruct((1, dc), F32)] * 3,
        compiler_params=_cparams("arbitrary"),
    )(dcat, a2, ln_g, ln_b)


def _mixer_e_bwd_mix(da2, dcat, u, conv_w, w_pool, scale, seq, ts):
    t, dc = da2.shape
    ng, pg = w_pool.shape[0], w_pool.shape[1]
    taps = conv_w.shape[0]
    ns = seq // ts

    def body(da2_ref, nda2_ref, dp_ref, ndp_ref, val_ref, gate_ref, b_ref, pval_ref, pgate_ref, pb_ref,
             cw_ref, wp_ref, sc_ref, du_ref, dcw_ref, dwp_ref, dsc_ref):
        i = pl.program_id(0)
        keep_prev = jnp.where(i % ns == 0, 0.0, 1.0)
        keep_next = jnp.where(i % ns == ns - 1, 0.0, 1.0)

        @pl.when(i == 0)
        def _():
            dcw_ref[...] = jnp.zeros_like(dcw_ref)
            dwp_ref[...] = jnp.zeros_like(dwp_ref)
            dsc_ref[...] = jnp.zeros_like(dsc_ref)

        val = val_ref[...]
        sg = _sigmoid(gate_ref[...])
        a1 = val * sg
        pa1 = pval_ref[...] * _sigmoid(pgate_ref[...]) * keep_prev
        ext_a = jnp.concatenate([pa1, a1], axis=0)
        da2v = da2_ref[...]
        ext_d = jnp.concatenate([da2v, nda2_ref[...] * keep_next], axis=0)
        da1 = jnp.zeros_like(ext_d)
        for k in range(taps):
            sh = taps - 1 - k
            dcw_ref[k:k + 1, :] += jnp.sum(da2v * _shift_down(ext_a, sh)[CONV_HALO:, :], axis=0, keepdims=True)
            da1 = da1 + cw_ref[k:k + 1, :] * _shift_up(ext_d, sh)
        da1 = da1[:ts, :]
        du_ref[:, 0:dc] = (da1 * sg).astype(BF16)
        du_ref[:, dc:2 * dc] = (da1 * a1 * (1.0 - sg)).astype(BF16)

        for g in range(ng):
            lo, hi = g * pg, (g + 1) * pg
            w = POOL_WINDOWS[g]
            cnt = _pool_counts(i, ns, ts, w)
            wpb = wp_ref[g].astype(BF16)
            sc = sc_ref[:, lo:hi]
            p = _pooled(b_ref[:, lo:hi], pb_ref[:, lo:hi] * keep_prev, w, cnt)
            pb16 = p.astype(BF16)
            q = jnp.dot(pb16, wpb, preferred_element_type=F32)
            dout = dp_ref[:, lo:hi]
            dsc_ref[:, lo:hi] += jnp.sum(dout * q, axis=0, keepdims=True)
            dq = (dout * sc).astype(BF16)
            dwp_ref[g] += lax.dot_general(pb16, dq, _DOT_DIMS["tn"], preferred_element_type=F32)
            dpool = lax.dot_general(dq, wpb, _DOT_DIMS["nt"], preferred_element_type=F32)
            ndq = (ndp_ref[:, lo:hi] * sc * keep_next).astype(BF16)
            ndpool = lax.dot_general(ndq, wpb, _DOT_DIMS["nt"], preferred_element_type=F32)
            s = jnp.concatenate([dpool / cnt, ndpool * (1.0 / w)], axis=0)
            d = 1
            while d < w:
                s = s + _shift_up(s, d)
                d *= 2
            du_ref[:, 2 * dc + lo:2 * dc + hi] = (s[:ts, :] - dpool).astype(BF16)

    return pl.pallas_call(
        body, name="mixer_e_bwd_mix", grid=(t // ts,),
        in_specs=[_cur(ts, dc, 0), _next_halo(ts, CONV_HALO, dc, 0, t),
                  _cur(ts, dc, 1), _next_halo(ts, POOL_HALO, dc, 1, t),
                  _cur(ts, dc, 0), _cur(ts, dc, 1), _cur(ts, dc, 2),
                  _prev_halo(ts, CONV_HALO, dc, 0), _prev_halo(ts, CONV_HALO, dc, 1), _prev_halo(ts, POOL_HALO, dc, 2),
                  _full(conv_w.shape), _full(w_pool.shape), _full(scale.shape)],
        out_specs=[_cur(ts, 3 * dc, 0), _full(conv_w.shape), _full(w_pool.shape), _full(scale.shape)],
        out_shape=[jax.ShapeDtypeStruct((t, 3 * dc), BF16), jax.ShapeDtypeStruct(conv_w.shape, F32),
                   jax.ShapeDtypeStruct(w_pool.shape, F32), jax.ShapeDtypeStruct(scale.shape, F32)],
        compiler_params=_cparams("arbitrary"),
    )(da2, da2, dcat, dcat, u, u, u, u, u, u, conv_w, w_pool, scale)


def _mixer_o_fwd(u, conv_w, seq, ts):
    t = u.shape[0]
    d = conv_w.shape[1]
    taps = conv_w.shape[0]
    ns = seq // ts

    def body(gb_ref, gc_ref, v_ref, pgc_ref, pv_ref, cw_ref, y_ref):
        keep_prev = jnp.where(pl.program_id(0) % ns == 0, 0.0, 1.0)
        ext = jnp.concatenate([pgc_ref[...] * pv_ref[...] * keep_prev, gc_ref[...] * v_ref[...]], axis=0)
        cc = jnp.zeros_like(ext)
        for k in range(taps):
            cc = cc + cw_ref[k:k + 1, :] * _shift_down(ext, taps - 1 - k)
        y_ref[...] = (gb_ref[...] * cc[SHORT_HALO:, :]).astype(BF16)

    return pl.pallas_call(
        body, name="mixer_o_fwd", grid=(t // ts,),
        in_specs=[_cur(ts, d, 0), _cur(ts, d, 1), _cur(ts, d, 2),
                  _prev_halo(ts, SHORT_HALO, d, 1), _prev_halo(ts, SHORT_HALO, d, 2), _full(conv_w.shape)],
        out_specs=_cur(ts, d, 0),
        out_shape=jax.ShapeDtypeStruct((t, d), BF16),
        compiler_params=_cparams("parallel"),
    )(u, u, u, u, u, conv_w)


def _mixer_o_bwd(dy, u, conv_w, seq, ts):
    t = u.shape[0]
    d = conv_w.shape[1]
    taps = conv_w.shape[0]
    ns = seq // ts

    def body(dy_ref, ndy_ref, gb_ref, gc_ref, v_ref, pgc_ref, pv_ref, ngb_ref, cw_ref, du_ref, dcw_ref):
        i = pl.program_id(0)
        keep_prev = jnp.where(i % ns == 0, 0.0, 1.0)
        keep_next = jnp.where(i % ns == ns - 1, 0.0, 1.0)

        @pl.when(i == 0)
        def _():
            dcw_ref[...] = jnp.zeros_like(dcw_ref)

        gb, gc, v, dyv = gb_ref[...], gc_ref[...], v_ref[...], dy_ref[...]
        ext = jnp.concatenate([pgc_ref[...] * pv_ref[...] * keep_prev, gc * v], axis=0)
        dcc = dyv * gb
        ext_d = jnp.concatenate([dcc, ndy_ref[...] * ngb_ref[...] * keep_next], axis=0)
        cc = jnp.zeros_like(ext)
        dcv = jnp.zeros_like(ext_d)
        for k in range(taps):
            sh = taps - 1 - k
            shifted = _shift_down(ext, sh)
            cc = cc + cw_ref[k:k + 1, :] * shifted
            dcw_ref[k:k + 1, :] += jnp.sum(dcc * shifted[SHORT_HALO:, :], axis=0, keepdims=True)
            dcv = dcv + cw_ref[k:k + 1, :] * _shift_up(ext_d, sh)
        dcv = dcv[:ts, :]
        du_ref[:, 0:d] = (dyv * cc[SHORT_HALO:, :]).astype(BF16)
        du_ref[:, d:2 * d] = (dcv * v).astype(BF16)
        du_ref[:, 2 * d:3 * d] = (dcv * gc).astype(BF16)

    return pl.pallas_call(
        body, name="mixer_o_bwd", grid=(t // ts,),
        in_specs=[_cur(ts, d, 0), _next_halo(ts, SHORT_HALO, d, 0, t),
                  _cur(ts, d, 0), _cur(ts, d, 1), _cur(ts, d, 2),
                  _prev_halo(ts, SHORT_HALO, d, 1), _prev_halo(ts, SHORT_HALO, d, 2),
                  _next_halo(ts, SHORT_HALO, d, 0, t), _full(conv_w.shape)],
        out_specs=[_cur(ts, 3 * d, 0), _full(conv_w.shape)],
        out_shape=[jax.ShapeDtypeStruct((t, 3 * d), BF16), jax.ShapeDtypeStruct(conv_w.shape, F32)],
        compiler_params=_cparams("arbitrary"),
    )(dy, dy, u, u, u, u, u, u, conv_w)


def _cast_into_full(name, mat, w, chip):
    tr = _pick(mat.sr, (512, 256, 128, 64, 32, 16))
    per = mat.sr // tr

    def body(chip_ref, w_ref, o_ref):
        o_ref[...] = w_ref[...].astype(BF16)

    if mat.kind == "col":
        o_spec = pl.BlockSpec((tr, mat.sc), lambda i, chip_ref: (i, chip_ref[0]))
    else:
        o_spec = pl.BlockSpec((tr, mat.sc), lambda i, chip_ref: (chip_ref[0] * per + i, 0))
    return pl.pallas_call(
        body, name=name,
        grid_spec=pltpu.PrefetchScalarGridSpec(
            num_scalar_prefetch=1, grid=(per,),
            in_specs=[pl.BlockSpec((tr, mat.sc), lambda i, chip_ref: (i, 0))], out_specs=o_spec),
        out_shape=jax.ShapeDtypeStruct(mat.full_shape, BF16),
        compiler_params=_cparams("parallel"),
    )(chip, w)


def _adamw(name, w, g, m, v):
    r, c = w.shape
    tr = _pick(r, (256, 128, 64, 32, 16, 8)) if c > 1024 else _pick(r, (512, 256, 128, 64, 32, 16, 8))
    bc1 = 1.0 - ADAM_B1 ** ADAM_STEP
    bc2 = 1.0 - ADAM_B2 ** ADAM_STEP

    def body(w_ref, g_ref, m_ref, v_ref, d_ref, mo_ref, vo_ref):
        gv = g_ref[...]
        mn = ADAM_B1 * m_ref[...] + (1.0 - ADAM_B1) * gv
        vn = ADAM_B2 * v_ref[...] + (1.0 - ADAM_B2) * (gv * gv)
        mo_ref[...] = mn
        vo_ref[...] = vn
        d_ref[...] = -ADAM_LR * ((mn / bc1) / (jnp.sqrt(vn / bc2) + ADAM_EPS) + ADAM_WD * w_ref[...])

    spec = pl.BlockSpec((tr, c), lambda i: (i, 0))
    return pl.pallas_call(body, name=name, grid=(r // tr,), in_specs=[spec] * 4, out_specs=[spec] * 3,
                          out_shape=[jax.ShapeDtypeStruct((r, c), F32)] * 3, compiler_params=_cparams("parallel"))(w, g, m, v)


def _aligned(offset, multiple):
    return offset if isinstance(offset, int) else pl.multiple_of(offset, multiple)


class _Mat:
    def __init__(self, kind, shard_shape):
        self.kind = kind
        self.sr, self.sc = shard_shape
        self.full_shape = (self.sr, self.sc * N_CHIPS) if kind == "col" else (self.sr * N_CHIPS, self.sc)
        self.pr, self.pc = self.sr // 2, self.sc

    def piece(self, ref, k, h):
        if self.kind == "col":
            return ref.at[pl.ds(_aligned(h * self.pr, 16), self.pr), pl.ds(_aligned(k * self.sc, LANES), self.sc)]
        return ref.at[pl.ds(_aligned(k * self.sr + h * self.pr, 16), self.pr), :]

    def shard(self, ref, k):
        if self.kind == "col":
            return ref.at[:, pl.ds(_aligned(k * self.sc, LANES), self.sc)]
        return ref.at[pl.ds(_aligned(k * self.sr, 16), self.sr), :]

    def half(self, ref, h):
        return ref.at[pl.ds(_aligned(h * self.pr, 16), self.pr), :]


def _place():
    x, y, c = lax.axis_index("x"), lax.axis_index("y"), lax.axis_index("c")
    others = [(1 - x, y), (x, 1 - y), (1 - x, 1 - y)]
    return x, y, c, others


_ANY = pl.BlockSpec(memory_space=pl.ANY)


_HBM = pl.BlockSpec(memory_space=pltpu.HBM)
_SEM = pl.BlockSpec(memory_space=pltpu.SEMAPHORE)
_TOKEN = jax.ShapeDtypeStruct((8, LANES), F32)
_TOKEN_SPEC = pl.BlockSpec(memory_space=pltpu.VMEM)


def _split_params():
    return pltpu.CompilerParams(has_side_effects=pltpu.SideEffectType.DATAFLOW_SIDE_EFFECTING)


def _in_hbm(a):
    return pltpu.with_memory_space_constraint(a, pltpu.HBM)


def _copy_to(src, dst, send_sem, recv_sem, to):
    return pltpu.make_async_remote_copy(src_ref=src, dst_ref=dst, send_sem=send_sem, recv_sem=recv_sem,
                                        device_id=to, device_id_type=MESH_ID)


def _gather_start(mats, groups, fulls, packed_small):
    n, ng = len(mats), len(groups)

    def body(*refs):
        full_refs, small_ref = refs[:n], refs[n]
        outs = refs[n + 1:]
        small_all = outs[n]
        sems = outs[n + 1:n + 1 + 2 * ng]
        token = outs[n + 1 + 2 * ng]
        x, y, c, others = _place()
        me_k = 2 * x + y
        for g, members in enumerate(groups):
            send_sem, recv_sem = sems[2 * g], sems[2 * g + 1]
            if g == 0:
                for ox, oy in others:
                    _copy_to(small_ref, small_all.at[me_k], send_sem, recv_sem, (ox, oy, c)).start()
            for m in members:
                mine = mats[m].piece(full_refs[m], me_k, c)
                for ox, oy in others:
                    _copy_to(mine, mine, send_sem, recv_sem, (ox, oy, c)).start()
        token[...] = jnp.zeros_like(token)

    outs = pl.pallas_call(
        body, name="gather_start",
        in_specs=[_HBM] * (n + 1),
        out_specs=[_HBM] * (n + 1) + [_SEM] * (2 * ng) + [_TOKEN_SPEC],
        out_shape=[pltpu.HBM(mt.full_shape, BF16) for mt in mats] + [pltpu.HBM((N_CHIPS,) + packed_small.shape, F32)]
        + [pltpu.SemaphoreType.DMA(())] * (2 * ng) + [_TOKEN],
        input_output_aliases={m: m for m in range(n)},
        compiler_params=_split_params(),
    )(*[_in_hbm(f) for f in fulls], _in_hbm(packed_small))
    fulls, small_all = list(outs[:n]), outs[n]
    sems = [(outs[n + 1 + 2 * g], outs[n + 2 + 2 * g]) for g in range(ng)]
    return fulls, small_all, sems, outs[n + 1 + 2 * ng]


def _gather_pass(name, gmats, gfulls, small_all, sems, after):
    k = len(gmats)
    n_buf = k + (1 if small_all is not None else 0)

    def body(*refs):
        bufs = refs[:n_buf]
        send_sem, recv_sem = refs[n_buf], refs[n_buf + 1]
        outs = refs[n_buf + 3:]
        fsend, frecv, token = outs[n_buf], outs[n_buf + 1], outs[n_buf + 2]
        x, y, c, others = _place()
        me_k = 2 * x + y
        sibling = (x, y, 1 - c)
        for m in range(k):
            for ox, oy in others:
                got = gmats[m].piece(bufs[m], 2 * ox + oy, c)
                _copy_to(got, got, send_sem, recv_sem, sibling).wait_recv()
        if small_all is not None:
            for ox, oy in others:
                got = bufs[k].at[2 * ox + oy]
                _copy_to(got, got, send_sem, recv_sem, sibling).wait_recv()
        for m in range(k):
            mine = gmats[m].piece(bufs[m], me_k, c)
            for _ in others:
                _copy_to(mine, mine, send_sem, recv_sem, sibling).wait_send()
        if small_all is not None:
            for _ in others:
                _copy_to(bufs[k].at[me_k], bufs[k].at[me_k], send_sem, recv_sem, sibling).wait_send()
        for m in range(k):
            for ox, oy in others:
                got = gmats[m].piece(bufs[m], 2 * ox + oy, c)
                _copy_to(got, got, fsend, frecv, sibling).start()
        token[...] = jnp.zeros_like(token)

    operands = [_in_hbm(f) for f in gfulls] + ([_in_hbm(small_all)] if small_all is not None else [])
    outs = pl.pallas_call(
        body, name=name,
        in_specs=[_HBM] * n_buf + [_SEM, _SEM, _ANY],
        out_specs=[_HBM] * n_buf + [_SEM, _SEM, _TOKEN_SPEC],
        out_shape=[pltpu.HBM(a.shape, a.dtype) for a in operands] + [pltpu.SemaphoreType.DMA(())] * 2 + [_TOKEN],
        input_output_aliases={i: i for i in range(n_buf)},
        compiler_params=_split_params(),
    )(*operands, sems[0], sems[1], after)
    return list(outs[:n_buf]), (outs[n_buf], outs[n_buf + 1]), outs[n_buf + 2]


def _gather_done(name, gmats, gfulls, sems):
    k = len(gmats)

    def body(*refs):
        bufs = refs[:k]
        send_sem, recv_sem = refs[k], refs[k + 1]
        x, y, c, others = _place()
        sibling = (x, y, 1 - c)
        for m in range(k):
            for ox, oy in others:
                got = gmats[m].piece(bufs[m], 2 * ox + oy, 1 - c)
                _copy_to(got, got, send_sem, recv_sem, sibling).wait_recv()
        for m in range(k):
            for ox, oy in others:
                sent = gmats[m].piece(bufs[m], 2 * ox + oy, c)
                _copy_to(sent, sent, send_sem, recv_sem, sibling).wait_send()

    outs = pl.pallas_call(
        body, name=name,
        in_specs=[_HBM] * k + [_SEM, _SEM], out_specs=[_HBM] * k,
        out_shape=[pltpu.HBM(a.shape, a.dtype) for a in gfulls],
        input_output_aliases={i: i for i in range(k)},
        compiler_params=_split_params(),
    )(*[_in_hbm(f) for f in gfulls], sems[0], sems[1])
    return list(outs)


def _exchange_halves(name, mats, grads, packed_small=None):
    n = len(mats)
    n_in = n + (1 if packed_small is not None else 0)

    def body(*refs):
        g_refs = refs[:n]
        land_refs = refs[n_in:n_in + n]
        send_sems, recv_sems, local_sem = refs[2 * n_in:]
        x, y, c, _ = _place()
        me = 4 * x + 2 * y + c
        sibling = (x, y, 1 - c)

        def remote(src, dst, sem, to):
            return pltpu.make_async_remote_copy(src_ref=src, dst_ref=dst, send_sem=send_sems.at[sem],
                                                recv_sem=recv_sems.at[sem], device_id=to, device_id_type=MESH_ID)

        sent = []
        flips = [(fx, fy, fc) for fx in (0, 1) for fy in (0, 1) for fc in (0, 1) if (fx, fy, fc) != (0, 0, 0)]
        if packed_small is not None:
            small_ref, small_all = refs[n], refs[n_in + n]
            mine = pltpu.make_async_copy(small_ref, small_all.at[me], local_sem)
            mine.start()
            for j, (fx, fy, fc) in enumerate(flips):
                peer = (x ^ fx, y ^ fy, c ^ fc)
                sent.append(remote(small_ref, small_all.at[me], N_CHIPS * n + j, peer))
        for m in range(n):
            for k in range(N_CHIPS):
                sent.append(remote(mats[m].piece(g_refs[m], k, 1 - c), land_refs[m].at[k], N_CHIPS * m + k, sibling))
        for cp in sent:
            cp.start()
        if packed_small is not None:
            for j, (fx, fy, fc) in enumerate(flips):
                got = small_all.at[4 * (x ^ fx) + 2 * (y ^ fy) + (c ^ fc)]
                remote(got, got, N_CHIPS * n + j, sibling).wait_recv()
        for m in range(n):
            for k in range(N_CHIPS):
                got = land_refs[m].at[k]
                remote(got, got, N_CHIPS * m + k, sibling).wait_recv()
        for cp in sent:
            cp.wait_send()
        if packed_small is not None:
            mine.wait()

    n_sems = N_CHIPS * n + N_DEV - 1
    out_shape = [jax.ShapeDtypeStruct((N_CHIPS, mt.pr, mt.pc), BF16) for mt in mats]
    operands = list(grads)
    if packed_small is not None:
        out_shape.append(jax.ShapeDtypeStruct((N_DEV,) + packed_small.shape, F32))
        operands.append(packed_small)
    outs = pl.pallas_call(
        body, name=name,
        in_specs=[_ANY] * n_in, out_specs=[_ANY] * n_in, out_shape=out_shape,
        scratch_shapes=[pltpu.SemaphoreType.DMA((n_sems,)), pltpu.SemaphoreType.DMA((n_sems,)), pltpu.SemaphoreType.DMA],
    )(*operands)
    return outs[:n], (outs[n] if packed_small is not None else None)


def _add_halves(name, mat, grad, landed, core):
    tr = _pick(mat.pr, (256, 128, 64, 32, 16))
    per = mat.pr // tr

    def body(core_ref, g_ref, l_ref, o_ref):
        o_ref[...] = (g_ref[...].astype(F32) + l_ref[...].astype(F32)).astype(BF16)

    if mat.kind == "col":
        g_spec = pl.BlockSpec((tr, mat.pc), lambda k, r, core_ref: (core_ref[0] * per + r, k))
    else:
        g_spec = pl.BlockSpec((tr, mat.pc), lambda k, r, core_ref: ((2 * k + core_ref[0]) * per + r, 0))
    p_spec = pl.BlockSpec((None, tr, mat.pc), lambda k, r, core_ref: (k, r, 0))
    return pl.pallas_call(
        body, name=name,
        grid_spec=pltpu.PrefetchScalarGridSpec(num_scalar_prefetch=1, grid=(N_CHIPS, per),
                                               in_specs=[g_spec, p_spec], out_specs=p_spec),
        out_shape=jax.ShapeDtypeStruct((N_CHIPS, mat.pr, mat.pc), BF16),
        compiler_params=_cparams("parallel", "parallel"),
    )(core, grad, landed)


def _scatter_start(name, mats, partials):
    n = len(mats)

    def body(*refs):
        p_refs = refs[:n]
        outs = refs[n:]
        land_refs = outs[n:2 * n]
        send_sem, recv_sem, token = outs[2 * n], outs[2 * n + 1], outs[2 * n + 2]
        x, y, c, others = _place()
        me_k = 2 * x + y
        for m in range(n):
            for ox, oy in others:
                _copy_to(p_refs[m].at[2 * ox + oy], land_refs[m].at[me_k], send_sem, recv_sem, (ox, oy, c)).start()
        token[...] = jnp.zeros_like(token)

    piece_shapes = [pltpu.HBM((N_CHIPS, mt.pr, mt.pc), BF16) for mt in mats]
    outs = pl.pallas_call(
        body, name=name,
        in_specs=[_HBM] * n,
        out_specs=[_HBM] * (2 * n) + [_SEM, _SEM, _TOKEN_SPEC],
        out_shape=piece_shapes + piece_shapes + [pltpu.SemaphoreType.DMA(())] * 2 + [_TOKEN],
        input_output_aliases={m: m for m in range(n)},
        compiler_params=_split_params(),
    )(*[_in_hbm(p) for p in partials])
    return list(outs[:n]), list(outs[n:2 * n]), (outs[2 * n], outs[2 * n + 1]), outs[2 * n + 2]


def _scatter_wait(name, mats, partials, landed, sems, after):
    n = len(mats)

    def body(*refs):
        p_refs, land_refs = refs[:n], refs[n:2 * n]
        send_sem, recv_sem = refs[2 * n], refs[2 * n + 1]
        x, y, c, others = _place()
        for m in range(n):
            for ox, oy in others:
                got = land_refs[m].at[2 * ox + oy]
                _copy_to(got, got, send_sem, recv_sem, (ox, oy, c)).wait_recv()
        for m in range(n):
            for ox, oy in others:
                sent = p_refs[m].at[2 * ox + oy]
                _copy_to(sent, sent, send_sem, recv_sem, (ox, oy, c)).wait_send()

    outs = pl.pallas_call(
        body, name=name,
        in_specs=[_HBM] * (2 * n) + [_SEM, _SEM, _ANY], out_specs=[_HBM] * (2 * n),
        out_shape=[pltpu.HBM(a.shape, a.dtype) for a in list(partials) + list(landed)],
        input_output_aliases={i: i for i in range(2 * n)},
        compiler_params=_split_params(),
    )(*[_in_hbm(a) for a in list(partials) + list(landed)], sems[0], sems[1], after)
    return list(outs[:n]), list(outs[n:])


def _sum_chips(name, mat, partial, landed, slots, layer=None, stack=None, n_layers=1):
    tr = _pick(mat.pr, (256, 128, 64, 32, 16))
    per = mat.pr // tr

    def body(slots_ref, own_ref, a_ref, b_ref, c_ref, *rest):
        o_ref = rest[-1]
        o_ref[...] = ((own_ref[...].astype(F32) + a_ref[...].astype(F32)) + b_ref[...].astype(F32)) + c_ref[...].astype(F32)

    def slot_spec(which):
        return pl.BlockSpec((None, tr, mat.pc), lambda r, slots_ref: (slots_ref[which], r, 0))

    in_specs = [slot_spec(0), slot_spec(1), slot_spec(2), slot_spec(3)]
    operands = [slots, partial, landed, landed, landed]
    aliases = {}
    if layer is None:
        o_spec = pl.BlockSpec((tr, mat.pc), lambda r, slots_ref: (slots_ref[4] * per + r, 0))
        out_shape = jax.ShapeDtypeStruct((mat.sr, mat.sc), F32)
    else:
        o_spec = pl.BlockSpec((None, tr, mat.pc), lambda r, slots_ref: (layer, slots_ref[4] * per + r, 0))
        out_shape = jax.ShapeDtypeStruct((n_layers, mat.sr, mat.sc), F32)
        if stack is not None:
            in_specs.append(_ANY)
            operands.append(stack)
            aliases = {len(operands) - 1: 0}
    return pl.pallas_call(
        body, name=name,
        grid_spec=pltpu.PrefetchScalarGridSpec(num_scalar_prefetch=1, grid=(per,), in_specs=in_specs, out_specs=o_spec),
        out_shape=out_shape, input_output_aliases=aliases,
        compiler_params=_cparams("parallel"),
    )(*operands)


def _share_pieces(mats, shards, groups):
    n = len(mats)
    n_out = len(groups)

    def body(*refs):
        out_refs = refs[n_out:2 * n_out]
        send_sems, recv_sems = refs[2 * n_out:]
        x, y, c, _ = _place()
        sibling = (x, y, 1 - c)
        sent, waits = [], []
        for o, members in enumerate(groups):
            for l, m in enumerate(members):
                dst = out_refs[o].at[l] if len(members) > 1 else out_refs[o]
                mine = mats[m].half(dst, c)
                sent.append(pltpu.make_async_remote_copy(src_ref=mine, dst_ref=mine, send_sem=send_sems.at[m],
                                                         recv_sem=recv_sems.at[m], device_id=sibling, device_id_type=MESH_ID))
                theirs = mats[m].half(dst, 1 - c)
                waits.append(pltpu.make_async_remote_copy(src_ref=theirs, dst_ref=theirs, send_sem=send_sems.at[m],
                                                          recv_sem=recv_sems.at[m], device_id=sibling, device_id_type=MESH_ID))
        for cp in sent:
            cp.start()
        for cp in waits:
            cp.wait_recv()
        for cp in sent:
            cp.wait_send()

    return pl.pallas_call(
        body, name="share_pieces",
        in_specs=[_ANY] * n_out, out_specs=[_ANY] * n_out,
        out_shape=[jax.ShapeDtypeStruct(s.shape, F32) for s in shards],
        input_output_aliases={o: o for o in range(n_out)},
        scratch_shapes=[pltpu.SemaphoreType.DMA((n,)), pltpu.SemaphoreType.DMA((n,))],
    )(*shards)


def _sum_devices(stacked):
    nd, r, c = stacked.shape

    def body(s_ref, o_ref):
        s = s_ref[0]
        for k in range(1, nd):
            s = s + s_ref[k]
        o_ref[...] = s

    return pl.pallas_call(
        body, name="sum_small_grads", grid=(1,),
        in_specs=[pl.BlockSpec((nd, r, c), lambda i: (0, 0, 0))],
        out_specs=pl.BlockSpec((r, c), lambda i: (0, 0)),
        out_shape=jax.ShapeDtypeStruct((r, c), F32),
        compiler_params=_cparams("arbitrary"),
    )(stacked)


def _pack(arrs):
    flat = jnp.concatenate([a.reshape(-1) for a in arrs])
    rows = -(-flat.shape[0] // (8 * LANES)) * 8
    return jnp.pad(flat, (0, rows * LANES - flat.shape[0])).reshape(rows, LANES)


def _unpack(packed, shapes):
    flat = packed.reshape(-1)
    out, at = [], 0
    for s in shapes:
        size = 1
        for dim in s:
            size *= dim
        out.append(flat[at:at + size].reshape(s))
        at += size
    return out


def kernel(x, mix_norm_e, w_in_e, conv_w_e, conv_b_e, ln_g_e, ln_b_e, w_pool_e, pool_scale_e, w_out_e, mix_norm_o, w_in_o, conv_w_o, w_out_o, ffn_norm, w_gate, w_up, w_down, final_norm, loss_target, m_mix_norm_e, m_w_in_e, m_conv_w_e, m_conv_b_e, m_ln_g_e, m_ln_b_e, m_w_pool_e, m_pool_scale_e, m_w_out_e, m_mix_norm_o, m_w_in_o, m_conv_w_o, m_w_out_o, m_ffn_norm, m_w_gate, m_w_up, m_w_down, m_final_norm, v_mix_norm_e, v_w_in_e, v_conv_w_e, v_conv_b_e, v_ln_g_e, v_ln_b_e, v_w_pool_e, v_pool_scale_e, v_w_out_e, v_mix_norm_o, v_w_in_o, v_conv_w_o, v_w_out_o, v_ffn_norm, v_w_gate, v_w_up, v_w_down, v_final_norm):
    bsz, seq, d = x.shape
    t = bsz * seq
    depth = ffn_norm.shape[0]
    assert depth == 2 and conv_b_e.shape[1] == pool_scale_e.shape[1]
    ts = _pick(seq, (256, 128, 64, 32))
    me_k = 2 * lax.axis_index("x") + lax.axis_index("y")
    core = lax.axis_index("c").astype(jnp.int32).reshape(1)

    mat_src = [("col", w_in_e[0]), ("row", w_out_e[0]), ("col", w_gate[0]), ("col", w_up[0]), ("row", w_down[0]),
               ("col", w_in_o[0]), ("row", w_out_o[0]), ("col", w_gate[1]), ("col", w_up[1]), ("row", w_down[1])]
    mats = [_Mat(kind, w.shape) for kind, w in mat_src]
    chip = me_k.astype(jnp.int32).reshape(1)
    own16 = [_cast_into_full("cast_w%d" % i, mats[i], w, chip) for i, (_, w) in enumerate(mat_src)]
    small_shards = [conv_w_e[0], w_pool_e[0], mix_norm_o, conv_w_o[0]]
    packed_small = _pack(small_shards)
    gather_groups = [(0,), (1,), (2, 3), (4,), (5, 6), (7, 8), (9,)]
    fulls, small_all, gather_sems, gather_token = _gather_start(mats, gather_groups, own16, packed_small)

    def arrived(g, after):
        ms = gather_groups[g]
        gm = [mats[m] for m in ms]
        bufs, pass_sems, _ = _gather_pass("gather_pass%d" % g, gm, [fulls[m] for m in ms],
                                          small_all if g == 0 else None, gather_sems[g], after)
        done = _gather_done("gather_done%d" % g, gm, bufs[:len(ms)], pass_sems)
        return done + bufs[len(ms):]

    W_in_e, small_all = arrived(0, gather_token)
    per_chip = [_unpack(jnp.where(me_k == k, packed_small, small_all[k]), [s.shape for s in small_shards])
                for k in range(N_CHIPS)]
    conv_w_e_f = jnp.concatenate([p[0] for p in per_chip], axis=1)
    w_pool_f = jnp.concatenate([p[1] for p in per_chip], axis=1)
    mix_norm_o_f = jnp.concatenate([p[2] for p in per_chip], axis=1)
    conv_w_o_f = jnp.concatenate([p[3] for p in per_chip], axis=1)
    W_gate, W_up, W_down = [None, None], [None, None], [None, None]

    h0 = x.reshape(t, d)
    target = loss_target.reshape(t, d)

    def ffn_fwd(l, h, g_gate_up, g_down):
        n = _rms_fwd("ffn%d_norm" % l, h, ffn_norm[l:l + 1])
        W_gate[l], W_up[l] = arrived(g_gate_up, n)
        gt, up, act = _mm("ffn%d_gate_up" % l, [(n, W_gate[l]), (n, W_up[l])], "nn", [BF16] * 3, _ep_swiglu,
                          acc_of=(0, 1), tm=1024, tn=512, tk=2048)
        (W_down[l],) = arrived(g_down, act)
        (h_out,) = _mm("ffn%d_down" % l, [(act, W_down[l])], "nn", [F32], _ep_residual, extras=(h,),
                       tm=512, tn=1024, tk=2816)
        return n, gt, up, act, h_out

    n1 = _rms_fwd("mix0_norm", h0, mix_norm_e)
    (u_e,) = _mm("mix0_in", [(n1, W_in_e)], "nn", [F32], _ep_store, tm=1024, tn=1024, tk=2048)
    a2, cat = _mixer_e_fwd(u_e, conv_w_e_f, conv_b_e, ln_g_e, ln_b_e, w_pool_f, pool_scale_e, seq, ts)
    (W_out_e,) = arrived(1, cat)
    (h1,) = _mm("mix0_out", [(cat, W_out_e)], "nn", [F32], _ep_residual, extras=(h0,), tm=1024, tn=1024, tk=2048)
    n2, gt0, up0, act0, h2 = ffn_fwd(0, h1, 2, 3)
    n3 = _rms_fwd("mix1_norm", h2, mix_norm_o_f)
    W_in_o, W_out_o = arrived(4, n3)
    (u_o,) = _mm("mix1_in", [(n3, W_in_o)], "nn", [F32], _ep_store, tm=1024, tn=1024, tk=2048)
    y_o = _mixer_o_fwd(u_o, conv_w_o_f, seq, ts)
    (h3,) = _mm("mix1_out", [(y_o, W_out_o)], "nn", [F32], _ep_residual, extras=(h2,), tm=1024, tn=1024, tk=2048)
    n4, gt1, up1, act1, h4 = ffn_fwd(1, h3, 5, 6)
    loss_part, dh4, dh4b, d_final_norm = _loss_head(h4, final_norm.reshape(1, d), target)
    loss = lax.psum(loss_part[0, 0], AXES)

    in_flight = []

    def reduce_start(tag, ms, grads, small=None):
        gm = [mats[m] for m in ms]
        landed, small_stack = _exchange_halves("exchange_" + tag, gm, grads, small)
        parts = [_add_halves("add_halves%d" % m, mats[m], g, l, core) for m, g, l in zip(ms, grads, landed)]
        parts, lands, sems, token = _scatter_start("scatter_start_" + tag, gm, parts)
        in_flight.append((tag, ms, parts, lands, sems))
        return token, small_stack

    def ffn_bwd(l, dh, dhb, h_in, n, gt, up, act, after):
        dgt, dup = _mm("ffn%d_dact" % l, [(dhb, W_down[l])], "nt", [BF16, BF16], _ep_swiglu_bwd, extras=(gt, up),
                       tm=1024, tn=512, tk=2048, after=after)
        (dW_down,) = _mm("ffn%d_dw_down" % l, [(act, dhb)], "tn", [BF16], _ep_store, tm=512, tn=1024, tk=2048)
        (dn,) = _mm("ffn%d_dn" % l, [(dgt, W_gate[l]), (dup, W_up[l])], "nt", [F32], _ep_store,
                    tm=1024, tn=1024, tk=1408)
        (dW_gate,) = _mm("ffn%d_dw_gate" % l, [(n, dgt)], "tn", [BF16], _ep_store, tm=1024, tn=512, tk=2048)
        (dW_up,) = _mm("ffn%d_dw_up" % l, [(n, dup)], "tn", [BF16], _ep_store, tm=1024, tn=512, tk=2048)
        dh_in, dh_in_b, dg = _rms_bwd("ffn%d_norm_bwd" % l, dn, h_in, ffn_norm[l:l + 1], dh)
        return dh_in, dh_in_b, dg, dW_gate, dW_up, dW_down

    dh3, dh3b, d_ffn_norm1, dW_gate1, dW_up1, dW_down1 = ffn_bwd(1, dh4, dh4b, h3, n4, gt1, up1, act1, ())
    token, _ = reduce_start("ffn1", (7, 8, 9), [dW_gate1, dW_up1, dW_down1])

    (dy_o,) = _mm("mix1_dy", [(dh3b, W_out_o)], "nt", [F32], _ep_store, tm=1024, tn=1024, tk=2048, after=(token,))
    (dW_out_o,) = _mm("mix1_dw_out", [(y_o, dh3b)], "tn", [BF16], _ep_store, tm=1024, tn=1024, tk=2048)
    du_o, d_conv_w_o = _mixer_o_bwd(dy_o, u_o, conv_w_o_f, seq, ts)
    (dn3,) = _mm("mix1_dn", [(du_o, W_in_o)], "nt", [F32], _ep_store, tm=1024, tn=1024, tk=2048)
    (dW_in_o,) = _mm("mix1_dw_in", [(n3, du_o)], "tn", [BF16], _ep_store, tm=1024, tn=1024, tk=2048)
    dh2, dh2b, d_mix_norm_o = _rms_bwd("mix1_norm_bwd", dn3, h2, mix_norm_o_f, dh3)
    token, _ = reduce_start("mix1", (5, 6), [dW_in_o, dW_out_o])

    dh1, dh1b, d_ffn_norm0, dW_gate0, dW_up0, dW_down0 = ffn_bwd(0, dh2, dh2b, h1, n2, gt0, up0, act0, (token,))
    token, _ = reduce_start("ffn0", (2, 3, 4), [dW_gate0, dW_up0, dW_down0])

    (dcat,) = _mm("mix0_dcat", [(dh1b, W_out_e)], "nt", [F32], _ep_store, tm=1024, tn=1024, tk=2048, after=(token,))
    (dW_out_e,) = _mm("mix0_dw_out", [(cat, dh1b)], "tn", [BF16], _ep_store, tm=1024, tn=1024, tk=2048)
    da2, d_ln_g, d_ln_b, d_conv_b = _mixer_e_bwd_norm(dcat, a2, ln_g_e, ln_b_e, ts)
    du_e, d_conv_w_e, d_w_pool, d_pool_scale = _mixer_e_bwd_mix(da2, dcat, u_e, conv_w_e_f, w_pool_f, pool_scale_e, seq, ts)
    (dn1,) = _mm("mix0_dn", [(du_e, W_in_e)], "nt", [F32], _ep_store, tm=1024, tn=1024, tk=2048)
    (dW_in_e,) = _mm("mix0_dw_in", [(n1, du_e)], "tn", [BF16], _ep_store, tm=1024, tn=1024, tk=2048)
    dx, _, d_mix_norm_e = _rms_bwd("mix0_norm_bwd", dn1, h0, mix_norm_e, dh1)

    d_ffn_norm = jnp.concatenate([d_ffn_norm0, d_ffn_norm1], axis=0)
    small_partials = [d_mix_norm_e, d_conv_w_e, d_conv_b, d_ln_g, d_ln_b, d_w_pool, d_pool_scale, d_mix_norm_o,
                      d_conv_w_o, d_ffn_norm, d_final_norm]
    token, small_stack = reduce_start("mix0", (0, 1), [dW_in_e, dW_out_e], _pack(small_partials))
    partials, scattered = [None] * len(mats), [None] * len(mats)
    for tag, ms, parts, lands, sems in in_flight:
        parts, lands = _scatter_wait("scatter_wait_" + tag, [mats[m] for m in ms], parts, lands, sems, token)
        for m, p, l in zip(ms, parts, lands):
            partials[m], scattered[m] = p, l
    xi, yi, ci = lax.axis_index("x"), lax.axis_index("y"), lax.axis_index("c")
    slots = jnp.stack([me_k, 2 * (1 - xi) + yi, 2 * xi + (1 - yi), 2 * (1 - xi) + (1 - yi), ci]).astype(jnp.int32)
    groups = [(0,), (1,), (5,), (6,), (2, 7), (3, 8), (4, 9)]
    halves = []
    for members in groups:
        if len(members) == 1:
            m = members[0]
            halves.append(_sum_chips("sum_chips%d" % m, mats[m], partials[m], scattered[m], slots))
        else:
            stack = None
            for l, m in enumerate(members):
                stack = _sum_chips("sum_chips%d" % m, mats[m], partials[m], scattered[m], slots, layer=l, stack=stack,
                                   n_layers=len(members))
            halves.append(stack)
    g_w_in_e, g_w_out_e, g_w_in_o, g_w_out_o, g_w_gate, g_w_up, g_w_down = _share_pieces(mats, halves, groups)
    small_sum = _unpack(_sum_devices(small_stack), [s.shape for s in small_partials])
    (g_mix_norm_e, g_conv_w_e_f, g_conv_b, g_ln_g, g_ln_b, g_w_pool_f, g_pool_scale, g_mix_norm_o_f, g_conv_w_o_f,
     g_ffn_norm, g_final_norm) = small_sum

    def my_shard(full, axis):
        size = full.shape[axis] // N_CHIPS
        return lax.dynamic_slice_in_dim(full, me_k * size, size, axis)

    g_conv_w_e = my_shard(g_conv_w_e_f, 1)
    g_w_pool = my_shard(g_w_pool_f, 1)
    g_mix_norm_o = my_shard(g_mix_norm_o_f, 1)
    g_conv_w_o = my_shard(g_conv_w_o_f, 1)

    grad = {
        "mix_norm_e": g_mix_norm_e, "w_in_e": g_w_in_e[None], "conv_w_e": g_conv_w_e[None], "conv_b_e": g_conv_b,
        "ln_g_e": g_ln_g, "ln_b_e": g_ln_b, "w_pool_e": g_w_pool[None], "pool_scale_e": g_pool_scale,
        "w_out_e": g_w_out_e[None], "mix_norm_o": g_mix_norm_o, "w_in_o": g_w_in_o[None], "conv_w_o": g_conv_w_o[None],
        "w_out_o": g_w_out_o[None], "ffn_norm": g_ffn_norm, "w_gate": g_w_gate, "w_up": g_w_up, "w_down": g_w_down,
        "final_norm": g_final_norm.reshape(final_norm.shape),
    }
    weights = dict(mix_norm_e=mix_norm_e, w_in_e=w_in_e, conv_w_e=conv_w_e, conv_b_e=conv_b_e, ln_g_e=ln_g_e, ln_b_e=ln_b_e,
                   w_pool_e=w_pool_e, pool_scale_e=pool_scale_e, w_out_e=w_out_e, mix_norm_o=mix_norm_o, w_in_o=w_in_o,
                   conv_w_o=conv_w_o, w_out_o=w_out_o, ffn_norm=ffn_norm, w_gate=w_gate, w_up=w_up, w_down=w_down,
                   final_norm=final_norm)
    mom1 = dict(mix_norm_e=m_mix_norm_e, w_in_e=m_w_in_e, conv_w_e=m_conv_w_e, conv_b_e=m_conv_b_e, ln_g_e=m_ln_g_e,
                ln_b_e=m_ln_b_e, w_pool_e=m_w_pool_e, pool_scale_e=m_pool_scale_e, w_out_e=m_w_out_e, mix_norm_o=m_mix_norm_o,
                w_in_o=m_w_in_o, conv_w_o=m_conv_w_o, w_out_o=m_w_out_o, ffn_norm=m_ffn_norm, w_gate=m_w_gate, w_up=m_w_up,
                w_down=m_w_down, final_norm=m_final_norm)
    mom2 = dict(mix_norm_e=v_mix_norm_e, w_in_e=v_w_in_e, conv_w_e=v_conv_w_e, conv_b_e=v_conv_b_e, ln_g_e=v_ln_g_e,
                ln_b_e=v_ln_b_e, w_pool_e=v_w_pool_e, pool_scale_e=v_pool_scale_e, w_out_e=v_w_out_e, mix_norm_o=v_mix_norm_o,
                w_in_o=v_w_in_o, conv_w_o=v_conv_w_o, w_out_o=v_w_out_o, ffn_norm=v_ffn_norm, w_gate=v_w_gate, w_up=v_w_up,
                w_down=v_w_down, final_norm=v_final_norm)
    names = list(weights)

    big = ("w_in_e", "w_out_e", "w_in_o", "w_out_o", "w_gate", "w_up", "w_down")
    delta, new_m, new_v = {}, {}, {}
    for nm in big:
        shape = weights[nm].shape
        as2d = lambda a: a.reshape(shape[0] * shape[1], shape[2])
        dl, mn, vn = _adamw("adamw_" + nm, as2d(weights[nm]), as2d(grad[nm]), as2d(mom1[nm]), as2d(mom2[nm]))
        delta[nm], new_m[nm], new_v[nm] = dl.reshape(shape), mn.reshape(shape), vn.reshape(shape)
    small = [nm for nm in names if nm not in big]
    shapes = [weights[nm].shape for nm in small]
    dl, mn, vn = _adamw("adamw_small", _pack([weights[nm] for nm in small]), _pack([grad[nm] for nm in small]),
                        _pack([mom1[nm] for nm in small]), _pack([mom2[nm] for nm in small]))
    for nm, a, b, c_ in zip(small, _unpack(dl, shapes), _unpack(mn, shapes), _unpack(vn, shapes)):
        delta[nm], new_m[nm], new_v[nm] = a, b, c_

    grad_x = dx.reshape(bsz, seq, d)
    return (loss, grad_x, *[grad[nm] for nm in names], *[delta[nm] for nm in names],
            *[new_m[nm] for nm in names], *[new_v[nm] for nm in names])
```

```python
import jax
import jax.numpy as jnp
from jax import lax
from jax.experimental import pallas as pl
from jax.experimental.pallas import tpu as pltpu

F32 = jnp.float32
BF16 = jnp.bfloat16
MESH_ID = pl.DeviceIdType.MESH
AXES = ("x", "y", "c")
N_CHIPS = 4
N_DEV = 8

EPS = 1e-6
POOL_WINDOWS = (2, 4, 8, 16)
ADAM_LR, ADAM_B1, ADAM_B2, ADAM_EPS, ADAM_WD, ADAM_STEP = 0.001, 0.9, 0.999, 1e-08, 0.01, 10

LANES = 128
CONV_HALO = 32
POOL_HALO = 16
SHORT_HALO = 8
V7X_VMEM_LIMIT = 56 * 1024 * 1024


def _cparams(*sem):
    return pltpu.CompilerParams(dimension_semantics=sem if sem else None, vmem_limit_bytes=V7X_VMEM_LIMIT)


def _pick(dim, prefs):
    for p in prefs:
        if p <= dim and dim % p == 0:
            return p
    return dim


def _sigmoid(x):
    return jax.nn.sigmoid(x)


_DOT_DIMS = {
    "nn": (((1,), (0,)), ((), ())),
    "nt": (((1,), (1,)), ((), ())),
    "tn": (((0,), (0,)), ((), ())),
}


def _mm(name, pairs, mode, out_dtypes, epilogue, extras=(), acc_of=None, tm=512, tn=512, tk=2048, after=()):
    a0, b0 = pairs[0]
    if mode == "nn":
        (m, k), n = a0.shape, b0.shape[1]
    elif mode == "nt":
        (m, k), n = a0.shape, b0.shape[0]
    else:
        (k, m), n = a0.shape, b0.shape[1]
    tm = _pick(m, (tm, 512, 256, 128, 64, 32, 16, 8))
    tn = _pick(n, (tn, 512, 256, 128))
    tk = _pick(k, (tk, 2048, 1024, 512, 256, 128))
    nk = k // tk
    n_pairs = len(pairs)
    acc_of = tuple(acc_of) if acc_of is not None else (0,) * n_pairs
    n_acc = max(acc_of) + 1
    n_ex, n_out = len(extras), len(out_dtypes)
    dims = _DOT_DIMS[mode]

    def body(*refs):
        a_refs = refs[:n_pairs]
        b_refs = refs[n_pairs:2 * n_pairs]
        e_refs = refs[2 * n_pairs:2 * n_pairs + n_ex]
        first_out = 2 * n_pairs + n_ex + len(after)
        o_refs = refs[first_out:first_out + n_out]
        acc_refs = refs[first_out + n_out:]

        def partial_sums():
            sums = [None] * n_acc
            for p in range(n_pairs):
                d = lax.dot_general(a_refs[p][...], b_refs[p][...], dims, preferred_element_type=F32)
                sums[acc_of[p]] = d if sums[acc_of[p]] is None else sums[acc_of[p]] + d
            return sums

        if nk == 1:
            epilogue(partial_sums(), e_refs, o_refs)
            return
        kk = pl.program_id(2)

        @pl.when(kk == 0)
        def _():
            for acc in acc_refs:
                acc[...] = jnp.zeros_like(acc)

        for acc, s in zip(acc_refs, partial_sums()):
            acc[...] += s

        @pl.when(kk == nk - 1)
        def _():
            epilogue([acc[...] for acc in acc_refs], e_refs, o_refs)

    if mode == "nn":
        a_spec = pl.BlockSpec((tm, tk), lambda i, j, kk: (i, kk))
        b_spec = pl.BlockSpec((tk, tn), lambda i, j, kk: (kk, j))
    elif mode == "nt":
        a_spec = pl.BlockSpec((tm, tk), lambda i, j, kk: (i, kk))
        b_spec = pl.BlockSpec((tn, tk), lambda i, j, kk: (j, kk))
    else:
        a_spec = pl.BlockSpec((tk, tm), lambda i, j, kk: (kk, i))
        b_spec = pl.BlockSpec((tk, tn), lambda i, j, kk: (kk, j))
    o_spec = pl.BlockSpec((tm, tn), lambda i, j, kk: (i, j))
    outs = pl.pallas_call(
        body,
        name=name,
        grid=(m // tm, n // tn, nk),
        in_specs=[a_spec] * n_pairs + [b_spec] * n_pairs + [o_spec] * n_ex
        + [pl.BlockSpec(memory_space=pl.ANY)] * len(after),
        out_specs=[o_spec] * n_out,
        out_shape=[jax.ShapeDtypeStruct((m, n), dt) for dt in out_dtypes],
        scratch_shapes=[pltpu.VMEM((tm, tn), F32) for _ in range(n_acc)] if nk > 1 else [],
        compiler_params=_cparams("parallel", "parallel", "arbitrary"),
    )(*[p[0] for p in pairs], *[p[1] for p in pairs], *extras, *after)
    return outs


def _ep_store(accs, ex, outs):
    outs[0][...] = accs[0].astype(outs[0].dtype)


def _ep_residual(accs, ex, outs):
    outs[0][...] = ex[0][...] + accs[0]


def _ep_swiglu(accs, ex, outs):
    g, u = accs
    outs[0][...] = g.astype(BF16)
    outs[1][...] = u.astype(BF16)
    outs[2][...] = (g * _sigmoid(g) * u).astype(BF16)


def _ep_swiglu_bwd(accs, ex, outs):
    d = accs[0]
    g = ex[0][...].astype(F32)
    u = ex[1][...].astype(F32)
    s = _sigmoid(g)
    outs[0][...] = (d * u * (s * (1.0 + g * (1.0 - s)))).astype(BF16)
    outs[1][...] = (d * (g * s)).astype(BF16)


def _rms_fwd(name, h, g, after=()):
    t, d = h.shape
    tr = _pick(t, (256, 128, 64, 32, 16, 8))

    def body(h_ref, g_ref, *rest):
        o_ref = rest[-1]
        x = h_ref[...]
        r = lax.rsqrt(jnp.mean(x * x, axis=-1, keepdims=True) + EPS)
        o_ref[...] = (x * r * g_ref[...]).astype(BF16)

    return pl.pallas_call(
        body, name=name, grid=(t // tr,),
        in_specs=[pl.BlockSpec((tr, d), lambda i: (i, 0)), pl.BlockSpec((1, d), lambda i: (0, 0))]
        + [pl.BlockSpec(memory_space=pl.ANY)] * len(after),
        out_specs=pl.BlockSpec((tr, d), lambda i: (i, 0)),
        out_shape=jax.ShapeDtypeStruct((t, d), BF16),
        compiler_params=_cparams("parallel"),
    )(h, g, *after)


def _rms_bwd(name, dn, h, g, dres):
    t, d = h.shape
    tr = _pick(t, (256, 128, 64, 32, 16, 8))

    def body(dn_ref, h_ref, g_ref, dres_ref, dh_ref, dhb_ref, dg_ref):
        x = h_ref[...]
        r = lax.rsqrt(jnp.mean(x * x, axis=-1, keepdims=True) + EPS)
        xhat = x * r
        dnv = dn_ref[...]

        @pl.when(pl.program_id(0) == 0)
        def _():
            dg_ref[...] = jnp.zeros_like(dg_ref)

        dg_ref[...] += jnp.sum(dnv * xhat, axis=0, keepdims=True)
        dxh = dnv * g_ref[...]
        dh = dres_ref[...] + r * (dxh - xhat * jnp.mean(dxh * xhat, axis=-1, keepdims=True))
        dh_ref[...] = dh
        dhb_ref[...] = dh.astype(BF16)

    row = pl.BlockSpec((tr, d), lambda i: (i, 0))
    vec = pl.BlockSpec((1, d), lambda i: (0, 0))
    return pl.pallas_call(
        body, name=name, grid=(t // tr,),
        in_specs=[row, row, vec, row],
        out_specs=[row, row, vec],
        out_shape=[jax.ShapeDtypeStruct((t, d), F32), jax.ShapeDtypeStruct((t, d), BF16),
                   jax.ShapeDtypeStruct((1, d), F32)],
        compiler_params=_cparams("arbitrary"),
    )(dn, h, g, dres)


def _loss_head(h, g, target):
    t, d = h.shape
    tr = _pick(t, (256, 128, 64, 32, 16, 8))

    def body(h_ref, g_ref, t_ref, loss_ref, dh_ref, dhb_ref, dg_ref):
        x = h_ref[...]
        gv = g_ref[...]
        r = lax.rsqrt(jnp.mean(x * x, axis=-1, keepdims=True) + EPS)
        xhat = x * r
        err = xhat * gv - t_ref[...]

        @pl.when(pl.program_id(0) == 0)
        def _():
            dg_ref[...] = jnp.zeros_like(dg_ref)
            loss_ref[...] = jnp.zeros_like(loss_ref)

        loss_ref[...] += jnp.full(loss_ref.shape, 0.5 / d, F32) * jnp.sum(err * err)
        dy = err * (1.0 / d)
        dg_ref[...] += jnp.sum(dy * xhat, axis=0, keepdims=True)
        dxh = dy * gv
        dh = r * (dxh - xhat * jnp.mean(dxh * xhat, axis=-1, keepdims=True))
        dh_ref[...] = dh
        dhb_ref[...] = dh.astype(BF16)

    row = pl.BlockSpec((tr, d), lambda i: (i, 0))
    vec = pl.BlockSpec((1, d), lambda i: (0, 0))
    return pl.pallas_call(
        body, name="loss_head", grid=(t // tr,),
        in_specs=[row, vec, row],
        out_specs=[pl.BlockSpec((1, LANES), lambda i: (0, 0)), row, row, vec],
        out_shape=[jax.ShapeDtypeStruct((1, LANES), F32), jax.ShapeDtypeStruct((t, d), F32),
                   jax.ShapeDtypeStruct((t, d), BF16), jax.ShapeDtypeStruct((1, d), F32)],
        compiler_params=_cparams("arbitrary"),
    )(h, g, target)


def _cur(ts, width, col):
    return pl.BlockSpec((ts, width), lambda i: (i, col))


def _prev_halo(ts, halo, width, col):
    per = ts // halo
    return pl.BlockSpec((halo, width), lambda i: (jnp.maximum(i * per - 1, 0), col))


def _next_halo(ts, halo, width, col, n_rows):
    per = ts // halo
    last = n_rows // halo - 1
    return pl.BlockSpec((halo, width), lambda i: (jnp.minimum((i + 1) * per, last), col))


def _full(shape):
    nd = len(shape)
    return pl.BlockSpec(shape, lambda i: (0,) * nd)


def _shift_down(x, n):
    return x if n == 0 else pltpu.roll(x, n, 0)


def _shift_up(x, n):
    return x if n == 0 else pltpu.roll(x, x.shape[0] - n, 0)


def _pool_counts(i, ns, ts, w):
    pos = (i % ns) * ts + lax.broadcasted_iota(jnp.int32, (ts, 1), 0)
    return jnp.minimum(pos + 1, w).astype(F32)


def _pooled(cur, prev_tail, w, cnt):
    s = jnp.concatenate([prev_tail, cur], axis=0)
    d = 1
    while d < w:
        s = s + _shift_down(s, d)
        d *= 2
    return s[POOL_HALO:, :] / cnt - cur


def _mixer_e_fwd(u, conv_w, conv_b, ln_g, ln_b, w_pool, scale, seq, ts):
    t = u.shape[0]
    dc = conv_b.shape[1]
    ng, pg = w_pool.shape[0], w_pool.shape[1]
    taps = conv_w.shape[0]
    ns = seq // ts

    def body(val_ref, gate_ref, b_ref, pval_ref, pgate_ref, pb_ref, cw_ref, cb_ref, g_ref, be_ref, wp_ref, sc_ref,
             a2_ref, cat_ref):
        i = pl.program_id(0)
        keep_prev = jnp.where(i % ns == 0, 0.0, 1.0)
        a1 = val_ref[...] * _sigmoid(gate_ref[...])
        pa1 = pval_ref[...] * _sigmoid(pgate_ref[...]) * keep_prev
        ext = jnp.concatenate([pa1, a1], axis=0)
        acc = jnp.zeros_like(ext)
        for k in range(taps):
            acc = acc + cw_ref[k:k + 1, :] * _shift_down(ext, taps - 1 - k)
        a2 = acc[CONV_HALO:, :] + cb_ref[...]
        a2_ref[...] = a2
        mu = jnp.mean(a2, axis=-1, keepdims=True)
        xc = a2 - mu
        rstd = lax.rsqrt(jnp.mean(xc * xc, axis=-1, keepdims=True) + EPS)
        a3 = xc * rstd * g_ref[...] + be_ref[...]
        cat_ref[:, 0:dc] = (a3 * _sigmoid(a3)).astype(BF16)
        for g in range(ng):
            lo, hi = g * pg, (g + 1) * pg
            w = POOL_WINDOWS[g]
            p = _pooled(b_ref[:, lo:hi], pb_ref[:, lo:hi] * keep_prev, w, _pool_counts(i, ns, ts, w))
            q = jnp.dot(p.astype(BF16), wp_ref[g].astype(BF16), preferred_element_type=F32)
            cat_ref[:, dc + lo:dc + hi] = (q * sc_ref[:, lo:hi]).astype(BF16)

    return pl.pallas_call(
        body, name="mixer_e_fwd", grid=(t // ts,),
        in_specs=[_cur(ts, dc, 0), _cur(ts, dc, 1), _cur(ts, dc, 2),
                  _prev_halo(ts, CONV_HALO, dc, 0), _prev_halo(ts, CONV_HALO, dc, 1), _prev_halo(ts, POOL_HALO, dc, 2),
                  _full(conv_w.shape), _full(conv_b.shape), _full(ln_g.shape), _full(ln_b.shape),
                  _full(w_pool.shape), _full(scale.shape)],
        out_specs=[_cur(ts, dc, 0), _cur(ts, 2 * dc, 0)],
        out_shape=[jax.ShapeDtypeStruct((t, dc), F32), jax.ShapeDtypeStruct((t, 2 * dc), BF16)],
        compiler_params=_cparams("parallel"),
    )(u, u, u, u, u, u, conv_w, conv_b, ln_g, ln_b, w_pool, scale)


def _mixer_e_bwd_norm(dcat, a2, ln_g, ln_b, ts):
    t, dc = a2.shape

    def body(d_ref, a2_ref, g_ref, be_ref, da2_ref, dg_ref, db_ref, dcb_ref):
        x = a2_ref[...]
        gv = g_ref[...]
        mu = jnp.mean(x, axis=-1, keepdims=True)
        xc = x - mu
        rstd = lax.rsqrt(jnp.mean(xc * xc, axis=-1, keepdims=True) + EPS)
        xhat = xc * rstd
        a3 = xhat * gv + be_ref[...]
        sg = _sigmoid(a3)
        da3 = d_ref[...] * (sg * (1.0 + a3 * (1.0 - sg)))
        dxh = da3 * gv
        da2 = rstd * (dxh - jnp.mean(dxh, axis=-1, keepdims=True)
                      - xhat * jnp.mean(dxh * xhat, axis=-1, keepdims=True))
        da2_ref[...] = da2

        @pl.when(pl.program_id(0) == 0)
        def _():
            dg_ref[...] = jnp.zeros_like(dg_ref)
            db_ref[...] = jnp.zeros_like(db_ref)
            dcb_ref[...] = jnp.zeros_like(dcb_ref)

        dg_ref[...] += jnp.sum(da3 * xhat, axis=0, keepdims=True)
        db_ref[...] += jnp.sum(da3, axis=0, keepdims=True)
        dcb_ref[...] += jnp.sum(da2, axis=0, keepdims=True)

    vec = _full((1, dc))
    return pl.pallas_call(
        body, name="mixer_e_bwd_norm", grid=(t // ts,),
        in_specs=[_cur(ts, dc, 0), _cur(ts, dc, 0), vec, vec],
        out_specs=[_cur(ts, dc, 0), vec, vec, vec],
        out_shape=[jax.ShapeDtypeStruct((t, dc), F32)] + [jax.ShapeDtypeStruct((1, dc), F32)] * 3,
        compiler_params=_cparams("arbitrary"),
    )(dcat, a2, ln_g, ln_b)


def _mixer_e_bwd_mix(da2, dcat, u, conv_w, w_pool, scale, seq, ts):
    t, dc = da2.shape
    ng, pg = w_pool.shape[0], w_pool.shape[1]
    taps = conv_w.shape[0]
    ns = seq // ts

    def body(da2_ref, nda2_ref, dp_ref, ndp_ref, val_ref, gate_ref, b_ref, pval_ref, pgate_ref, pb_ref,
             cw_ref, wp_ref, sc_ref, du_ref, dcw_ref, dwp_ref, dsc_ref):
        i = pl.program_id(0)
        keep_prev = jnp.where(i % ns == 0, 0.0, 1.0)
        keep_next = jnp.where(i % ns == ns - 1, 0.0, 1.0)

        @pl.when(i == 0)
        def _():
            dcw_ref[...] = jnp.zeros_like(dcw_ref)
            dwp_ref[...] = jnp.zeros_like(dwp_ref)
            dsc_ref[...] = jnp.zeros_like(dsc_ref)

        val = val_ref[...]
        sg = _sigmoid(gate_ref[...])
        a1 = val * sg
        pa1 = pval_ref[...] * _sigmoid(pgate_ref[...]) * keep_prev
        ext_a = jnp.concatenate([pa1, a1], axis=0)
        da2v = da2_ref[...]
        ext_d = jnp.concatenate([da2v, nda2_ref[...] * keep_next], axis=0)
        da1 = jnp.zeros_like(ext_d)
        for k in range(taps):
            sh = taps - 1 - k
            dcw_ref[k:k + 1, :] += jnp.sum(da2v * _shift_down(ext_a, sh)[CONV_HALO:, :], axis=0, keepdims=True)
            da1 = da1 + cw_ref[k:k + 1, :] * _shift_up(ext_d, sh)
        da1 = da1[:ts, :]
        du_ref[:, 0:dc] = (da1 * sg).astype(BF16)
        du_ref[:, dc:2 * dc] = (da1 * a1 * (1.0 - sg)).astype(BF16)

        for g in range(ng):
            lo, hi = g * pg, (g + 1) * pg
            w = POOL_WINDOWS[g]
            cnt = _pool_counts(i, ns, ts, w)
            wpb = wp_ref[g].astype(BF16)
            sc = sc_ref[:, lo:hi]
            p = _pooled(b_ref[:, lo:hi], pb_ref[:, lo:hi] * keep_prev, w, cnt)
            pb16 = p.astype(BF16)
            q = jnp.dot(pb16, wpb, preferred_element_type=F32)
            dout = dp_ref[:, lo:hi]
            dsc_ref[:, lo:hi] += jnp.sum(dout * q, axis=0, keepdims=True)
            dq = (dout * sc).astype(BF16)
            dwp_ref[g] += lax.dot_general(pb16, dq, _DOT_DIMS["tn"], preferred_element_type=F32)
            dpool = lax.dot_general(dq, wpb, _DOT_DIMS["nt"], preferred_element_type=F32)
            ndq = (ndp_ref[:, lo:hi] * sc * keep_next).astype(BF16)
            ndpool = lax.dot_general(ndq, wpb, _DOT_DIMS["nt"], preferred_element_type=F32)
            s = jnp.concatenate([dpool / cnt, ndpool * (1.0 / w)], axis=0)
            d = 1
            while d < w:
                s = s + _shift_up(s, d)
                d *= 2
            du_ref[:, 2 * dc + lo:2 * dc + hi] = (s[:ts, :] - dpool).astype(BF16)

    return pl.pallas_call(
        body, name="mixer_e_bwd_mix", grid=(t // ts,),
        in_specs=[_cur(ts, dc, 0), _next_halo(ts, CONV_HALO, dc, 0, t),
                  _cur(ts, dc, 1), _next_halo(ts, POOL_HALO, dc, 1, t),
                  _cur(ts, dc, 0), _cur(ts, dc, 1), _cur(ts, dc, 2),
                  _prev_halo(ts, CONV_HALO, dc, 0), _prev_halo(ts, CONV_HALO, dc, 1), _prev_halo(ts, POOL_HALO, dc, 2),
                  _full(conv_w.shape), _full(w_pool.shape), _full(scale.shape)],
        out_specs=[_cur(ts, 3 * dc, 0), _full(conv_w.shape), _full(w_pool.shape), _full(scale.shape)],
        out_shape=[jax.ShapeDtypeStruct((t, 3 * dc), BF16), jax.ShapeDtypeStruct(conv_w.shape, F32),
                   jax.ShapeDtypeStruct(w_pool.shape, F32), jax.ShapeDtypeStruct(scale.shape, F32)],
        compiler_params=_cparams("arbitrary"),
    )(da2, da2, dcat, dcat, u, u, u, u, u, u, conv_w, w_pool, scale)


def _mixer_o_fwd(u, conv_w, seq, ts):
    t = u.shape[0]
    d = conv_w.shape[1]
    taps = conv_w.shape[0]
    ns = seq // ts

    def body(gb_ref, gc_ref, v_ref, pgc_ref, pv_ref, cw_ref, y_ref):
        keep_prev = jnp.where(pl.program_id(0) % ns == 0, 0.0, 1.0)
        ext = jnp.concatenate([pgc_ref[...] * pv_ref[...] * keep_prev, gc_ref[...] * v_ref[...]], axis=0)
        cc = jnp.zeros_like(ext)
        for k in range(taps):
            cc = cc + cw_ref[k:k + 1, :] * _shift_down(ext, taps - 1 - k)
        y_ref[...] = (gb_ref[...] * cc[SHORT_HALO:, :]).astype(BF16)

    return pl.pallas_call(
        body, name="mixer_o_fwd", grid=(t // ts,),
        in_specs=[_cur(ts, d, 0), _cur(ts, d, 1), _cur(ts, d, 2),
                  _prev_halo(ts, SHORT_HALO, d, 1), _prev_halo(ts, SHORT_HALO, d, 2), _full(conv_w.shape)],
        out_specs=_cur(ts, d, 0),
        out_shape=jax.ShapeDtypeStruct((t, d), BF16),
        compiler_params=_cparams("parallel"),
    )(u, u, u, u, u, conv_w)


def _mixer_o_bwd(dy, u, conv_w, seq, ts):
    t = u.shape[0]
    d = conv_w.shape[1]
    taps = conv_w.shape[0]
    ns = seq // ts

    def body(dy_ref, ndy_ref, gb_ref, gc_ref, v_ref, pgc_ref, pv_ref, ngb_ref, cw_ref, du_ref, dcw_ref):
        i = pl.program_id(0)
        keep_prev = jnp.where(i % ns == 0, 0.0, 1.0)
        keep_next = jnp.where(i % ns == ns - 1, 0.0, 1.0)

        @pl.when(i == 0)
        def _():
            dcw_ref[...] = jnp.zeros_like(dcw_ref)

        gb, gc, v, dyv = gb_ref[...], gc_ref[...], v_ref[...], dy_ref[...]
        ext = jnp.concatenate([pgc_ref[...] * pv_ref[...] * keep_prev, gc * v], axis=0)
        dcc = dyv * gb
        ext_d = jnp.concatenate([dcc, ndy_ref[...] * ngb_ref[...] * keep_next], axis=0)
        cc = jnp.zeros_like(ext)
        dcv = jnp.zeros_like(ext_d)
        for k in range(taps):
            sh = taps - 1 - k
            shifted = _shift_down(ext, sh)
            cc = cc + cw_ref[k:k + 1, :] * shifted
            dcw_ref[k:k + 1, :] += jnp.sum(dcc * shifted[SHORT_HALO:, :], axis=0, keepdims=True)
            dcv = dcv + cw_ref[k:k + 1, :] * _shift_up(ext_d, sh)
        dcv = dcv[:ts, :]
        du_ref[:, 0:d] = (dyv * cc[SHORT_HALO:, :]).astype(BF16)
        du_ref[:, d:2 * d] = (dcv * v).astype(BF16)
        du_ref[:, 2 * d:3 * d] = (dcv * gc).astype(BF16)

    return pl.pallas_call(
        body, name="mixer_o_bwd", grid=(t // ts,),
        in_specs=[_cur(ts, d, 0), _next_halo(ts, SHORT_HALO, d, 0, t),
                  _cur(ts, d, 0), _cur(ts, d, 1), _cur(ts, d, 2),
                  _prev_halo(ts, SHORT_HALO, d, 1), _prev_halo(ts, SHORT_HALO, d, 2),
                  _next_halo(ts, SHORT_HALO, d, 0, t), _full(conv_w.shape)],
        out_specs=[_cur(ts, 3 * d, 0), _full(conv_w.shape)],
        out_shape=[jax.ShapeDtypeStruct((t, 3 * d), BF16), jax.ShapeDtypeStruct(conv_w.shape, F32)],
        compiler_params=_cparams("arbitrary"),
    )(dy, dy, u, u, u, u, u, u, conv_w)


def _cast_into_full(name, mat, w, chip, after=()):
    tr = _pick(mat.sr, (512, 256, 128, 64, 32, 16))
    per = mat.sr // tr

    def body(chip_ref, w_ref, *rest):
        o_ref = rest[-1]
        o_ref[...] = w_ref[...].astype(BF16)

    if mat.kind == "col":
        o_spec = pl.BlockSpec((tr, mat.sc), lambda i, chip_ref: (i, chip_ref[0]))
    else:
        o_spec = pl.BlockSpec((tr, mat.sc), lambda i, chip_ref: (chip_ref[0] * per + i, 0))
    return pl.pallas_call(
        body, name=name,
        grid_spec=pltpu.PrefetchScalarGridSpec(
            num_scalar_prefetch=1, grid=(per,),
            in_specs=[pl.BlockSpec((tr, mat.sc), lambda i, chip_ref: (i, 0))]
            + [pl.BlockSpec(memory_space=pl.ANY)] * len(after), out_specs=o_spec),
        out_shape=jax.ShapeDtypeStruct(mat.full_shape, BF16),
        compiler_params=_cparams("parallel"),
    )(chip, w, *after)


def _adamw(name, w, g, m, v):
    r, c = w.shape
    tr = _pick(r, (256, 128, 64, 32, 16, 8)) if c > 1024 else _pick(r, (512, 256, 128, 64, 32, 16, 8))
    bc1 = 1.0 - ADAM_B1 ** ADAM_STEP
    bc2 = 1.0 - ADAM_B2 ** ADAM_STEP

    def body(w_ref, g_ref, m_ref, v_ref, d_ref, mo_ref, vo_ref):
        gv = g_ref[...]
        mn = ADAM_B1 * m_ref[...] + (1.0 - ADAM_B1) * gv
        vn = ADAM_B2 * v_ref[...] + (1.0 - ADAM_B2) * (gv * gv)
        mo_ref[...] = mn
        vo_ref[...] = vn
        d_ref[...] = -ADAM_LR * ((mn / bc1) / (jnp.sqrt(vn / bc2) + ADAM_EPS) + ADAM_WD * w_ref[...])

    spec = pl.BlockSpec((tr, c), lambda i: (i, 0))
    return pl.pallas_call(body, name=name, grid=(r // tr,), in_specs=[spec] * 4, out_specs=[spec] * 3,
                          out_shape=[jax.ShapeDtypeStruct((r, c), F32)] * 3, compiler_params=_cparams("parallel"))(w, g, m, v)


def _aligned(offset, multiple):
    return offset if isinstance(offset, int) else pl.multiple_of(offset, multiple)


class _Mat:
    def __init__(self, kind, shard_shape):
        self.kind = kind
        self.sr, self.sc = shard_shape
        self.full_shape = (self.sr, self.sc * N_CHIPS) if kind == "col" else (self.sr * N_CHIPS, self.sc)
        self.pr, self.pc = self.sr // 2, self.sc

    def piece(self, ref, k, h):
        if self.kind == "col":
            return ref.at[pl.ds(_aligned(h * self.pr, 16), self.pr), pl.ds(_aligned(k * self.sc, LANES), self.sc)]
        return ref.at[pl.ds(_aligned(k * self.sr + h * self.pr, 16), self.pr), :]

    def shard(self, ref, k):
        if self.kind == "col":
            return ref.at[:, pl.ds(_aligned(k * self.sc, LANES), self.sc)]
        return ref.at[pl.ds(_aligned(k * self.sr, 16), self.sr), :]

    def half(self, ref, h):
        return ref.at[pl.ds(_aligned(h * self.pr, 16), self.pr), :]


def _place():
    x, y, c = lax.axis_index("x"), lax.axis_index("y"), lax.axis_index("c")
    others = [(1 - x, y), (x, 1 - y), (1 - x, 1 - y)]
    return x, y, c, others


_ANY = pl.BlockSpec(memory_space=pl.ANY)


_HBM = pl.BlockSpec(memory_space=pltpu.HBM)
_SEM = pl.BlockSpec(memory_space=pltpu.SEMAPHORE)
_TOKEN = jax.ShapeDtypeStruct((8, LANES), F32)
_TOKEN_SPEC = pl.BlockSpec(memory_space=pltpu.VMEM)


def _split_params():
    return pltpu.CompilerParams(has_side_effects=pltpu.SideEffectType.DATAFLOW_SIDE_EFFECTING)


def _in_hbm(a):
    return pltpu.with_memory_space_constraint(a, pltpu.HBM)


def _copy_to(src, dst, send_sem, recv_sem, to):
    return pltpu.make_async_remote_copy(src_ref=src, dst_ref=dst, send_sem=send_sem, recv_sem=recv_sem,
                                        device_id=to, device_id_type=MESH_ID)


def _gather_start(name, gmats, gfulls, packed_small=None):
    n = len(gmats)
    n_in = n + (1 if packed_small is not None else 0)

    def body(*refs):
        full_refs = refs[:n]
        outs = refs[n_in:]
        send_sem, recv_sem, token = outs[n_in], outs[n_in + 1], outs[n_in + 2]
        x, y, c, others = _place()
        me_k = 2 * x + y
        if packed_small is not None:
            for ox, oy in others:
                _copy_to(refs[n], outs[n].at[me_k], send_sem, recv_sem, (ox, oy, c)).start()
        for m in range(n):
            mine = gmats[m].piece(full_refs[m], me_k, c)
            for ox, oy in others:
                _copy_to(mine, mine, send_sem, recv_sem, (ox, oy, c)).start()
        token[...] = jnp.zeros_like(token)

    operands = [_in_hbm(f) for f in gfulls]
    out_shape = [pltpu.HBM(mt.full_shape, BF16) for mt in gmats]
    if packed_small is not None:
        operands.append(_in_hbm(packed_small))
        out_shape.append(pltpu.HBM((N_CHIPS,) + packed_small.shape, F32))
    outs = pl.pallas_call(
        body, name=name,
        in_specs=[_HBM] * n_in,
        out_specs=[_HBM] * n_in + [_SEM, _SEM, _TOKEN_SPEC],
        out_shape=out_shape + [pltpu.SemaphoreType.DMA(())] * 2 + [_TOKEN],
        input_output_aliases={m: m for m in range(n)},
        compiler_params=_split_params(),
    )(*operands)
    small_all = outs[n] if packed_small is not None else None
    return list(outs[:n]), small_all, (outs[n_in], outs[n_in + 1]), outs[n_in + 2]


def _gather_pass(name, gmats, gfulls, small_all, sems, after):
    k = len(gmats)
    n_buf = k + (1 if small_all is not None else 0)

    def body(*refs):
        bufs = refs[:n_buf]
        send_sem, recv_sem = refs[n_buf], refs[n_buf + 1]
        outs = refs[n_buf + 3:]
        fsend, frecv, token = outs[n_buf], outs[n_buf + 1], outs[n_buf + 2]
        x, y, c, others = _place()
        me_k = 2 * x + y
        sibling = (x, y, 1 - c)
        for m in range(k):
            for ox, oy in others:
                got = gmats[m].piece(bufs[m], 2 * ox + oy, c)
                _copy_to(got, got, send_sem, recv_sem, sibling).wait_recv()
        if small_all is not None:
            for ox, oy in others:
                got = bufs[k].at[2 * ox + oy]
                _copy_to(got, got, send_sem, recv_sem, sibling).wait_recv()
        for m in range(k):
            mine = gmats[m].piece(bufs[m], me_k, c)
            for _ in others:
                _copy_to(mine, mine, send_sem, recv_sem, sibling).wait_send()
        if small_all is not None:
            for _ in others:
                _copy_to(bufs[k].at[me_k], bufs[k].at[me_k], send_sem, recv_sem, sibling).wait_send()
        for m in range(k):
            for ox, oy in others:
                got = gmats[m].piece(bufs[m], 2 * ox + oy, c)
                _copy_to(got, got, fsend, frecv, sibling).start()
        token[...] = jnp.zeros_like(token)

    operands = [_in_hbm(f) for f in gfulls] + ([_in_hbm(small_all)] if small_all is not None else [])
    outs = pl.pallas_call(
        body, name=name,
        in_specs=[_HBM] * n_buf + [_SEM, _SEM, _ANY],
        out_specs=[_HBM] * n_buf + [_SEM, _SEM, _TOKEN_SPEC],
        out_shape=[pltpu.HBM(a.shape, a.dtype) for a in operands] + [pltpu.SemaphoreType.DMA(())] * 2 + [_TOKEN],
        input_output_aliases={i: i for i in range(n_buf)},
        compiler_params=_split_params(),
    )(*operands, sems[0], sems[1], after)
    return list(outs[:n_buf]), (outs[n_buf], outs[n_buf + 1]), outs[n_buf + 2]


def _gather_done(name, gmats, gfulls, sems):
    k = len(gmats)

    def body(*refs):
        bufs = refs[:k]
        send_sem, recv_sem = refs[k], refs[k + 1]
        x, y, c, others = _place()
        sibling = (x, y, 1 - c)
        for m in range(k):
            for ox, oy in others:
                got = gmats[m].piece(bufs[m], 2 * ox + oy, 1 - c)
                _copy_to(got, got, send_sem, recv_sem, sibling).wait_recv()
        for m in range(k):
            for ox, oy in others:
                sent = gmats[m].piece(bufs[m], 2 * ox + oy, c)
                _copy_to(sent, sent, send_sem, recv_sem, sibling).wait_send()

    outs = pl.pallas_call(
        body, name=name,
        in_specs=[_HBM] * k + [_SEM, _SEM], out_specs=[_HBM] * k,
        out_shape=[pltpu.HBM(a.shape, a.dtype) for a in gfulls],
        input_output_aliases={i: i for i in range(k)},
        compiler_params=_split_params(),
    )(*[_in_hbm(f) for f in gfulls], sems[0], sems[1])
    return list(outs)


def _exchange_halves(name, mats, grads, packed_small=None):
    n = len(mats)
    n_in = n + (1 if packed_small is not None else 0)

    def body(*refs):
        g_refs = refs[:n]
        land_refs = refs[n_in:n_in + n]
        send_sems, recv_sems, local_sem = refs[2 * n_in:]
        x, y, c, _ = _place()
        me = 4 * x + 2 * y + c
        sibling = (x, y, 1 - c)

        def remote(src, dst, sem, to):
            return pltpu.make_async_remote_copy(src_ref=src, dst_ref=dst, send_sem=send_sems.at[sem],
                                                recv_sem=recv_sems.at[sem], device_id=to, device_id_type=MESH_ID)

        sent = []
        flips = [(fx, fy, fc) for fx in (0, 1) for fy in (0, 1) for fc in (0, 1) if (fx, fy, fc) != (0, 0, 0)]
        if packed_small is not None:
            small_ref, small_all = refs[n], refs[n_in + n]
            mine = pltpu.make_async_copy(small_ref, small_all.at[me], local_sem)
            mine.start()
            for j, (fx, fy, fc) in enumerate(flips):
                peer = (x ^ fx, y ^ fy, c ^ fc)
                sent.append(remote(small_ref, small_all.at[me], N_CHIPS * n + j, peer))
        for m in range(n):
            for k in range(N_CHIPS):
                sent.append(remote(mats[m].piece(g_refs[m], k, 1 - c), land_refs[m].at[k], N_CHIPS * m + k, sibling))
        for cp in sent:
            cp.start()
        if packed_small is not None:
            for j, (fx, fy, fc) in enumerate(flips):
                got = small_all.at[4 * (x ^ fx) + 2 * (y ^ fy) + (c ^ fc)]
                remote(got, got, N_CHIPS * n + j, sibling).wait_recv()
        for m in range(n):
            for k in range(N_CHIPS):
                got = land_refs[m].at[k]
                remote(got, got, N_CHIPS * m + k, sibling).wait_recv()
        for cp in sent:
            cp.wait_send()
        if packed_small is not None:
            mine.wait()

    n_sems = N_CHIPS * n + N_DEV - 1
    out_shape = [jax.ShapeDtypeStruct((N_CHIPS, mt.pr, mt.pc), BF16) for mt in mats]
    operands = list(grads)
    if packed_small is not None:
        out_shape.append(jax.ShapeDtypeStruct((N_DEV,) + packed_small.shape, F32))
        operands.append(packed_small)
    outs = pl.pallas_call(
        body, name=name,
        in_specs=[_ANY] * n_in, out_specs=[_ANY] * n_in, out_shape=out_shape,
        scratch_shapes=[pltpu.SemaphoreType.DMA((n_sems,)), pltpu.SemaphoreType.DMA((n_sems,)), pltpu.SemaphoreType.DMA],
    )(*operands)
    return outs[:n], (outs[n] if packed_small is not None else None)


def _exchange_start(name, mats, grads):
    n = len(mats)

    def body(*refs):
        g_refs = refs[:n]
        outs = refs[n:]
        land_refs = outs[n:2 * n]
        send_sem, recv_sem, token = outs[2 * n], outs[2 * n + 1], outs[2 * n + 2]
        x, y, c, _ = _place()
        for m in range(n):
            for k in range(N_CHIPS):
                _copy_to(mats[m].piece(g_refs[m], k, 1 - c), land_refs[m].at[k], send_sem, recv_sem, (x, y, 1 - c)).start()
        token[...] = jnp.zeros_like(token)

    outs = pl.pallas_call(
        body, name=name,
        in_specs=[_HBM] * n,
        out_specs=[_HBM] * (2 * n) + [_SEM, _SEM, _TOKEN_SPEC],
        out_shape=[pltpu.HBM(mt.full_shape, BF16) for mt in mats]
        + [pltpu.HBM((N_CHIPS, mt.pr, mt.pc), BF16) for mt in mats] + [pltpu.SemaphoreType.DMA(())] * 2 + [_TOKEN],
        input_output_aliases={m: m for m in range(n)},
        compiler_params=_split_params(),
    )(*[_in_hbm(g) for g in grads])
    return list(outs[:n]), list(outs[n:2 * n]), (outs[2 * n], outs[2 * n + 1]), outs[2 * n + 2]


def _exchange_wait(name, mats, grads, landed, sems, after):
    n = len(mats)

    def body(*refs):
        g_refs, land_refs = refs[:n], refs[n:2 * n]
        send_sem, recv_sem = refs[2 * n], refs[2 * n + 1]
        x, y, c, _ = _place()
        for m in range(n):
            for k in range(N_CHIPS):
                got = land_refs[m].at[k]
                _copy_to(got, got, send_sem, recv_sem, (x, y, 1 - c)).wait_recv()
        for m in range(n):
            for k in range(N_CHIPS):
                sent = mats[m].piece(g_refs[m], k, 1 - c)
                _copy_to(sent, sent, send_sem, recv_sem, (x, y, 1 - c)).wait_send()

    outs = pl.pallas_call(
        body, name=name,
        in_specs=[_HBM] * (2 * n) + [_SEM, _SEM, _ANY], out_specs=[_HBM] * (2 * n),
        out_shape=[pltpu.HBM(a.shape, a.dtype) for a in list(grads) + list(landed)],
        input_output_aliases={i: i for i in range(2 * n)},
        compiler_params=_split_params(),
    )(*[_in_hbm(a) for a in list(grads) + list(landed)], sems[0], sems[1], after)
    return list(outs[:n]), list(outs[n:])


def _add_halves(name, mat, grad, landed, core):
    tr = _pick(mat.pr, (256, 128, 64, 32, 16))
    per = mat.pr // tr

    def body(core_ref, g_ref, l_ref, o_ref):
        o_ref[...] = (g_ref[...].astype(F32) + l_ref[...].astype(F32)).astype(BF16)

    if mat.kind == "col":
        g_spec = pl.BlockSpec((tr, mat.pc), lambda k, r, core_ref: (core_ref[0] * per + r, k))
    else:
        g_spec = pl.BlockSpec((tr, mat.pc), lambda k, r, core_ref: ((2 * k + core_ref[0]) * per + r, 0))
    p_spec = pl.BlockSpec((None, tr, mat.pc), lambda k, r, core_ref: (k, r, 0))
    return pl.pallas_call(
        body, name=name,
        grid_spec=pltpu.PrefetchScalarGridSpec(num_scalar_prefetch=1, grid=(N_CHIPS, per),
                                               in_specs=[g_spec, p_spec], out_specs=p_spec),
        out_shape=jax.ShapeDtypeStruct((N_CHIPS, mat.pr, mat.pc), BF16),
        compiler_params=_cparams("parallel", "parallel"),
    )(core, grad, landed)


def _scatter_start(name, mats, partials):
    n = len(mats)

    def body(*refs):
        p_refs = refs[:n]
        outs = refs[n:]
        land_refs = outs[n:2 * n]
        send_sem, recv_sem, token = outs[2 * n], outs[2 * n + 1], outs[2 * n + 2]
        x, y, c, others = _place()
        me_k = 2 * x + y
        for m in range(n):
            for ox, oy in others:
                _copy_to(p_refs[m].at[2 * ox + oy], land_refs[m].at[me_k], send_sem, recv_sem, (ox, oy, c)).start()
        token[...] = jnp.zeros_like(token)

    piece_shapes = [pltpu.HBM((N_CHIPS, mt.pr, mt.pc), BF16) for mt in mats]
    outs = pl.pallas_call(
        body, name=name,
        in_specs=[_HBM] * n,
        out_specs=[_HBM] * (2 * n) + [_SEM, _SEM, _TOKEN_SPEC],
        out_shape=piece_shapes + piece_shapes + [pltpu.SemaphoreType.DMA(())] * 2 + [_TOKEN],
        input_output_aliases={m: m for m in range(n)},
        compiler_params=_split_params(),
    )(*[_in_hbm(p) for p in partials])
    return list(outs[:n]), list(outs[n:2 * n]), (outs[2 * n], outs[2 * n + 1]), outs[2 * n + 2]


def _scatter_wait(name, mats, partials, landed, sems, after):
    n = len(mats)

    def body(*refs):
        p_refs, land_refs = refs[:n], refs[n:2 * n]
        send_sem, recv_sem = refs[2 * n], refs[2 * n + 1]
        x, y, c, others = _place()
        for m in range(n):
            for ox, oy in others:
                got = land_refs[m].at[2 * ox + oy]
                _copy_to(got, got, send_sem, recv_sem, (ox, oy, c)).wait_recv()
        for m in range(n):
            for ox, oy in others:
                sent = p_refs[m].at[2 * ox + oy]
                _copy_to(sent, sent, send_sem, recv_sem, (ox, oy, c)).wait_send()

    outs = pl.pallas_call(
        body, name=name,
        in_specs=[_HBM] * (2 * n) + [_SEM, _SEM, _ANY], out_specs=[_HBM] * (2 * n),
        out_shape=[pltpu.HBM(a.shape, a.dtype) for a in list(partials) + list(landed)],
        input_output_aliases={i: i for i in range(2 * n)},
        compiler_params=_split_params(),
    )(*[_in_hbm(a) for a in list(partials) + list(landed)], sems[0], sems[1], after)
    return list(outs[:n]), list(outs[n:])


def _sum_chips(name, mat, partial, landed, slots, layer=None, stack=None, n_layers=1):
    tr = _pick(mat.pr, (256, 128, 64, 32, 16))
    per = mat.pr // tr

    def body(slots_ref, own_ref, a_ref, b_ref, c_ref, *rest):
        o_ref = rest[-1]
        o_ref[...] = ((own_ref[...].astype(F32) + a_ref[...].astype(F32)) + b_ref[...].astype(F32)) + c_ref[...].astype(F32)

    def slot_spec(which):
        return pl.BlockSpec((None, tr, mat.pc), lambda r, slots_ref: (slots_ref[which], r, 0))

    in_specs = [slot_spec(0), slot_spec(1), slot_spec(2), slot_spec(3)]
    operands = [slots, partial, landed, landed, landed]
    aliases = {}
    if layer is None:
        o_spec = pl.BlockSpec((tr, mat.pc), lambda r, slots_ref: (slots_ref[4] * per + r, 0))
        out_shape = jax.ShapeDtypeStruct((mat.sr, mat.sc), F32)
    else:
        o_spec = pl.BlockSpec((None, tr, mat.pc), lambda r, slots_ref: (layer, slots_ref[4] * per + r, 0))
        out_shape = jax.ShapeDtypeStruct((n_layers, mat.sr, mat.sc), F32)
        if stack is not None:
            in_specs.append(_ANY)
            operands.append(stack)
            aliases = {len(operands) - 1: 0}
    return pl.pallas_call(
        body, name=name,
        grid_spec=pltpu.PrefetchScalarGridSpec(num_scalar_prefetch=1, grid=(per,), in_specs=in_specs, out_specs=o_spec),
        out_shape=out_shape, input_output_aliases=aliases,
        compiler_params=_cparams("parallel"),
    )(*operands)


def _share_pieces(name, mats, shards, groups):
    n = len(mats)
    n_out = len(groups)

    def body(*refs):
        out_refs = refs[n_out:2 * n_out]
        send_sems, recv_sems = refs[2 * n_out:]
        x, y, c, _ = _place()
        sibling = (x, y, 1 - c)
        sent, waits = [], []
        for o, members in enumerate(groups):
            for l, m in enumerate(members):
                dst = out_refs[o].at[l] if len(members) > 1 else out_refs[o]
                mine = mats[m].half(dst, c)
                sent.append(pltpu.make_async_remote_copy(src_ref=mine, dst_ref=mine, send_sem=send_sems.at[m],
                                                         recv_sem=recv_sems.at[m], device_id=sibling, device_id_type=MESH_ID))
                theirs = mats[m].half(dst, 1 - c)
                waits.append(pltpu.make_async_remote_copy(src_ref=theirs, dst_ref=theirs, send_sem=send_sems.at[m],
                                                          recv_sem=recv_sems.at[m], device_id=sibling, device_id_type=MESH_ID))
        for cp in sent:
            cp.start()
        for cp in waits:
            cp.wait_recv()
        for cp in sent:
            cp.wait_send()

    return pl.pallas_call(
        body, name=name,
        in_specs=[_ANY] * n_out, out_specs=[_ANY] * n_out,
        out_shape=[jax.ShapeDtypeStruct(s.shape, F32) for s in shards],
        input_output_aliases={o: o for o in range(n_out)},
        scratch_shapes=[pltpu.SemaphoreType.DMA((n,)), pltpu.SemaphoreType.DMA((n,))],
    )(*shards)


def _sum_devices(stacked):
    nd, r, c = stacked.shape

    def body(s_ref, o_ref):
        s = s_ref[0]
        for k in range(1, nd):
            s = s + s_ref[k]
        o_ref[...] = s

    return pl.pallas_call(
        body, name="sum_small_grads", grid=(1,),
        in_specs=[pl.BlockSpec((nd, r, c), lambda i: (0, 0, 0))],
        out_specs=pl.BlockSpec((r, c), lambda i: (0, 0)),
        out_shape=jax.ShapeDtypeStruct((r, c), F32),
        compiler_params=_cparams("arbitrary"),
    )(stacked)


def _pack(arrs):
    flat = jnp.concatenate([a.reshape(-1) for a in arrs])
    rows = -(-flat.shape[0] // (8 * LANES)) * 8
    return jnp.pad(flat, (0, rows * LANES - flat.shape[0])).reshape(rows, LANES)


def _unpack(packed, shapes):
    flat = packed.reshape(-1)
    out, at = [], 0
    for s in shapes:
        size = 1
        for dim in s:
            size *= dim
        out.append(flat[at:at + size].reshape(s))
        at += size
    return out


def kernel(x, mix_norm_e, w_in_e, conv_w_e, conv_b_e, ln_g_e, ln_b_e, w_pool_e, pool_scale_e, w_out_e, mix_norm_o, w_in_o, conv_w_o, w_out_o, ffn_norm, w_gate, w_up, w_down, final_norm, loss_target, m_mix_norm_e, m_w_in_e, m_conv_w_e, m_conv_b_e, m_ln_g_e, m_ln_b_e, m_w_pool_e, m_pool_scale_e, m_w_out_e, m_mix_norm_o, m_w_in_o, m_conv_w_o, m_w_out_o, m_ffn_norm, m_w_gate, m_w_up, m_w_down, m_final_norm, v_mix_norm_e, v_w_in_e, v_conv_w_e, v_conv_b_e, v_ln_g_e, v_ln_b_e, v_w_pool_e, v_pool_scale_e, v_w_out_e, v_mix_norm_o, v_w_in_o, v_conv_w_o, v_w_out_o, v_ffn_norm, v_w_gate, v_w_up, v_w_down, v_final_norm):
    bsz, seq, d = x.shape
    t = bsz * seq
    depth = ffn_norm.shape[0]
    assert depth == 2 and conv_b_e.shape[1] == pool_scale_e.shape[1]
    ts = _pick(seq, (256, 128, 64, 32))
    me_k = 2 * lax.axis_index("x") + lax.axis_index("y")
    core = lax.axis_index("c").astype(jnp.int32).reshape(1)

    mat_src = [("col", w_in_e[0]), ("row", w_out_e[0]), ("col", w_gate[0]), ("col", w_up[0]), ("row", w_down[0]),
               ("col", w_in_o[0]), ("row", w_out_o[0]), ("col", w_gate[1]), ("col", w_up[1]), ("row", w_down[1])]
    mats = [_Mat(kind, w.shape) for kind, w in mat_src]
    chip = me_k.astype(jnp.int32).reshape(1)
    small_shards = [conv_w_e[0], w_pool_e[0], mix_norm_o, conv_w_o[0]]
    packed_small = _pack(small_shards)
    gather_groups = [(0,), (1,), (2, 3), (4,), (5, 6), (7, 8), (9,)]
    fulls, gather_sems, small_all, gather_token = [None] * len(mats), [], None, ()
    for g, ms in enumerate(gather_groups):
        own16 = [_cast_into_full("cast_w%d" % m, mats[m], mat_src[m][1], chip, after=gather_token) for m in ms]
        sent, landing, sems, token = _gather_start("gather_start%d" % g, [mats[m] for m in ms], own16,
                                                   packed_small if g == 0 else None)
        for m, f in zip(ms, sent):
            fulls[m] = f
        gather_sems.append(sems)
        gather_token = (token,)
        if g == 0:
            small_all = landing

    def arrived(g, after):
        ms = gather_groups[g]
        gm = [mats[m] for m in ms]
        bufs, pass_sems, _ = _gather_pass("gather_pass%d" % g, gm, [fulls[m] for m in ms],
                                          small_all if g == 0 else None, gather_sems[g], after)
        done = _gather_done("gather_done%d" % g, gm, bufs[:len(ms)], pass_sems)
        return done + bufs[len(ms):]

    h0 = x.reshape(t, d)
    target = loss_target.reshape(t, d)
    n1 = _rms_fwd("mix0_norm", h0, mix_norm_e, after=gather_token)
    W_in_e, small_all = arrived(0, n1)
    per_chip = [_unpack(jnp.where(me_k == k, packed_small, small_all[k]), [s.shape for s in small_shards])
                for k in range(N_CHIPS)]
    conv_w_e_f = jnp.concatenate([p[0] for p in per_chip], axis=1)
    w_pool_f = jnp.concatenate([p[1] for p in per_chip], axis=1)
    mix_norm_o_f = jnp.concatenate([p[2] for p in per_chip], axis=1)
    conv_w_o_f = jnp.concatenate([p[3] for p in per_chip], axis=1)
    W_gate, W_up, W_down = [None, None], [None, None], [None, None]

    def ffn_fwd(l, h, g_gate_up, g_down):
        n = _rms_fwd("ffn%d_norm" % l, h, ffn_norm[l:l + 1])
        W_gate[l], W_up[l] = arrived(g_gate_up, n)
        gt, up, act = _mm("ffn%d_gate_up" % l, [(n, W_gate[l]), (n, W_up[l])], "nn", [BF16] * 3, _ep_swiglu,
                          acc_of=(0, 1), tm=1024, tn=512, tk=2048)
        (W_down[l],) = arrived(g_down, act)
        (h_out,) = _mm("ffn%d_down" % l, [(act, W_down[l])], "nn", [F32], _ep_residual, extras=(h,),
                       tm=512, tn=1024, tk=2816)
        return n, gt, up, act, h_out

    (u_e,) = _mm("mix0_in", [(n1, W_in_e)], "nn", [F32], _ep_store, tm=1024, tn=1024, tk=2048)
    a2, cat = _mixer_e_fwd(u_e, conv_w_e_f, conv_b_e, ln_g_e, ln_b_e, w_pool_f, pool_scale_e, seq, ts)
    (W_out_e,) = arrived(1, cat)
    (h1,) = _mm("mix0_out", [(cat, W_out_e)], "nn", [F32], _ep_residual, extras=(h0,), tm=1024, tn=1024, tk=2048)
    n2, gt0, up0, act0, h2 = ffn_fwd(0, h1, 2, 3)
    n3 = _rms_fwd("mix1_norm", h2, mix_norm_o_f)
    W_in_o, W_out_o = arrived(4, n3)
    (u_o,) = _mm("mix1_in", [(n3, W_in_o)], "nn", [F32], _ep_store, tm=1024, tn=1024, tk=2048)
    y_o = _mixer_o_fwd(u_o, conv_w_o_f, seq, ts)
    (h3,) = _mm("mix1_out", [(y_o, W_out_o)], "nn", [F32], _ep_residual, extras=(h2,), tm=1024, tn=1024, tk=2048)
    n4, gt1, up1, act1, h4 = ffn_fwd(1, h3, 5, 6)
    loss_part, dh4, dh4b, d_final_norm = _loss_head(h4, final_norm.reshape(1, d), target)
    loss = lax.psum(loss_part[0, 0], AXES)

    in_flight = {}
    partials, scattered = [None] * len(mats), [None] * len(mats)

    def reduce_begin(tag, ms, grads):
        gm = [mats[m] for m in ms]
        grads, landed, sems, token = _exchange_start("exchange_start_" + tag, gm, grads)
        in_flight[tag] = (ms, gm, grads, landed, sems)
        return (token,)

    def reduce_advance(tag, after):
        ms, gm, grads, landed, sems = in_flight[tag]
        grads, landed = _exchange_wait("exchange_wait_" + tag, gm, grads, landed, sems, after)
        parts = [_add_halves("add_halves%d" % m, mats[m], g, l, core) for m, g, l in zip(ms, grads, landed)]
        parts, lands, sems, token = _scatter_start("scatter_start_" + tag, gm, parts)
        in_flight[tag] = (ms, gm, parts, lands, sems)
        return (token,)

    def reduce_finish(tag, after):
        ms, gm, parts, lands, sems = in_flight[tag]
        parts, lands = _scatter_wait("scatter_wait_" + tag, gm, parts, lands, sems, after)
        for m, p, l in zip(ms, parts, lands):
            partials[m], scattered[m] = p, l

    def ffn_bwd(l, dh, dhb, h_in, n, gt, up, act, after, advance=None):
        dgt, dup = _mm("ffn%d_dact" % l, [(dhb, W_down[l])], "nt", [BF16, BF16], _ep_swiglu_bwd, extras=(gt, up),
                       tm=1024, tn=512, tk=2048, after=after)
        (dW_down,) = _mm("ffn%d_dw_down" % l, [(act, dhb)], "tn", [BF16], _ep_store, tm=512, tn=1024, tk=2048)
        (dn,) = _mm("ffn%d_dn" % l, [(dgt, W_gate[l]), (dup, W_up[l])], "nt", [F32], _ep_store,
                    tm=1024, tn=1024, tk=1408, after=advance(dW_down) if advance else ())
        (dW_gate,) = _mm("ffn%d_dw_gate" % l, [(n, dgt)], "tn", [BF16], _ep_store, tm=1024, tn=512, tk=2048)
        (dW_up,) = _mm("ffn%d_dw_up" % l, [(n, dup)], "tn", [BF16], _ep_store, tm=1024, tn=512, tk=2048)
        dh_in, dh_in_b, dg = _rms_bwd("ffn%d_norm_bwd" % l, dn, h_in, ffn_norm[l:l + 1], dh)
        return dh_in, dh_in_b, dg, dW_gate, dW_up, dW_down

    dh3, dh3b, d_ffn_norm1, dW_gate1, dW_up1, dW_down1 = ffn_bwd(1, dh4, dh4b, h3, n4, gt1, up1, act1, ())
    token = reduce_begin("ffn1", (7, 8, 9), [dW_gate1, dW_up1, dW_down1])

    (dy_o,) = _mm("mix1_dy", [(dh3b, W_out_o)], "nt", [F32], _ep_store, tm=1024, tn=1024, tk=2048, after=token)
    (dW_out_o,) = _mm("mix1_dw_out", [(y_o, dh3b)], "tn", [BF16], _ep_store, tm=1024, tn=1024, tk=2048)
    token = reduce_advance("ffn1", dW_out_o)
    du_o, d_conv_w_o = _mixer_o_bwd(dy_o, u_o, conv_w_o_f, seq, ts)
    (dn3,) = _mm("mix1_dn", [(du_o, W_in_o)], "nt", [F32], _ep_store, tm=1024, tn=1024, tk=2048, after=token)
    (dW_in_o,) = _mm("mix1_dw_in", [(n3, du_o)], "tn", [BF16], _ep_store, tm=1024, tn=1024, tk=2048)
    dh2, dh2b, d_mix_norm_o = _rms_bwd("mix1_norm_bwd", dn3, h2, mix_norm_o_f, dh3)
    token = reduce_begin("mix1", (5, 6), [dW_in_o, dW_out_o])

    dh1, dh1b, d_ffn_norm0, dW_gate0, dW_up0, dW_down0 = ffn_bwd(
        0, dh2, dh2b, h1, n2, gt0, up0, act0, token, advance=lambda a: reduce_advance("mix1", a))
    token = reduce_begin("ffn0", (2, 3, 4), [dW_gate0, dW_up0, dW_down0])

    (dcat,) = _mm("mix0_dcat", [(dh1b, W_out_e)], "nt", [F32], _ep_store, tm=1024, tn=1024, tk=2048, after=token)
    (dW_out_e,) = _mm("mix0_dw_out", [(cat, dh1b)], "tn", [BF16], _ep_store, tm=1024, tn=1024, tk=2048)
    token = reduce_advance("ffn0", dW_out_e)
    da2, d_ln_g, d_ln_b, d_conv_b = _mixer_e_bwd_norm(dcat, a2, ln_g_e, ln_b_e, ts)
    du_e, d_conv_w_e, d_w_pool, d_pool_scale = _mixer_e_bwd_mix(da2, dcat, u_e, conv_w_e_f, w_pool_f, pool_scale_e, seq, ts)
    (dW_in_e,) = _mm("mix0_dw_in", [(n1, du_e)], "tn", [BF16], _ep_store, tm=1024, tn=1024, tk=2048, after=token)
    token = reduce_begin("mix0", (0, 1), [dW_in_e, dW_out_e])
    (dn1,) = _mm("mix0_dn", [(du_e, W_in_e)], "nt", [F32], _ep_store, tm=1024, tn=1024, tk=2048, after=token)
    dx, _, d_mix_norm_e = _rms_bwd("mix0_norm_bwd", dn1, h0, mix_norm_e, dh1)
    token = reduce_advance("mix0", dx)

    d_ffn_norm = jnp.concatenate([d_ffn_norm0, d_ffn_norm1], axis=0)
    small_partials = [d_mix_norm_e, d_conv_w_e, d_conv_b, d_ln_g, d_ln_b, d_w_pool, d_pool_scale, d_mix_norm_o,
                      d_conv_w_o, d_ffn_norm, d_final_norm]
    _, small_stack = _exchange_halves("allgather_small_grads", [], [], _pack(small_partials))
    xi, yi, ci = lax.axis_index("x"), lax.axis_index("y"), lax.axis_index("c")
    slots = jnp.stack([me_k, 2 * (1 - xi) + yi, 2 * xi + (1 - yi), 2 * (1 - xi) + (1 - yi), ci]).astype(jnp.int32)

    def reduced_shards(name, groups):
        halves = []
        for members in groups:
            if len(members) == 1:
                m = members[0]
                halves.append(_sum_chips("sum_chips%d" % m, mats[m], partials[m], scattered[m], slots))
            else:
                stack = None
                for l, m in enumerate(members):
                    stack = _sum_chips("sum_chips%d" % m, mats[m], partials[m], scattered[m], slots, layer=l,
                                       stack=stack, n_layers=len(members))
                halves.append(stack)
        return _share_pieces(name, mats, halves, groups)

    for tag in ("ffn1", "mix1", "ffn0"):
        reduce_finish(tag, token[0])
    g_w_in_o, g_w_out_o, g_w_gate, g_w_up, g_w_down = reduced_shards(
        "share_pieces_late", [(5,), (6,), (2, 7), (3, 8), (4, 9)])
    small_sum = _unpack(_sum_devices(small_stack), [s.shape for s in small_partials])
    (g_mix_norm_e, g_conv_w_e_f, g_conv_b, g_ln_g, g_ln_b, g_w_pool_f, g_pool_scale, g_mix_norm_o_f, g_conv_w_o_f,
     g_ffn_norm, g_final_norm) = small_sum

    def my_shard(full, axis):
        size = full.shape[axis] // N_CHIPS
        return lax.dynamic_slice_in_dim(full, me_k * size, size, axis)

    g_conv_w_e = my_shard(g_conv_w_e_f, 1)
    g_w_pool = my_shard(g_w_pool_f, 1)
    g_mix_norm_o = my_shard(g_mix_norm_o_f, 1)
    g_conv_w_o = my_shard(g_conv_w_o_f, 1)

    grad = {
        "mix_norm_e": g_mix_norm_e, "conv_w_e": g_conv_w_e[None], "conv_b_e": g_conv_b,
        "ln_g_e": g_ln_g, "ln_b_e": g_ln_b, "w_pool_e": g_w_pool[None], "pool_scale_e": g_pool_scale,
        "mix_norm_o": g_mix_norm_o, "w_in_o": g_w_in_o[None], "conv_w_o": g_conv_w_o[None],
        "w_out_o": g_w_out_o[None], "ffn_norm": g_ffn_norm, "w_gate": g_w_gate, "w_up": g_w_up, "w_down": g_w_down,
        "final_norm": g_final_norm.reshape(final_norm.shape),
    }
    weights = dict(mix_norm_e=mix_norm_e, w_in_e=w_in_e, conv_w_e=conv_w_e, conv_b_e=conv_b_e, ln_g_e=ln_g_e, ln_b_e=ln_b_e,
                   w_pool_e=w_pool_e, pool_scale_e=pool_scale_e, w_out_e=w_out_e, mix_norm_o=mix_norm_o, w_in_o=w_in_o,
                   conv_w_o=conv_w_o, w_out_o=w_out_o, ffn_norm=ffn_norm, w_gate=w_gate, w_up=w_up, w_down=w_down,
                   final_norm=final_norm)
    mom1 = dict(mix_norm_e=m_mix_norm_e, w_in_e=m_w_in_e, conv_w_e=m_conv_w_e, conv_b_e=m_conv_b_e, ln_g_e=m_ln_g_e,
                ln_b_e=m_ln_b_e, w_pool_e=m_w_pool_e, pool_scale_e=m_pool_scale_e, w_out_e=m_w_out_e, mix_norm_o=m_mix_norm_o,
                w_in_o=m_w_in_o, conv_w_o=m_conv_w_o, w_out_o=m_w_out_o, ffn_norm=m_ffn_norm, w_gate=m_w_gate, w_up=m_w_up,
                w_down=m_w_down, final_norm=m_final_norm)
    mom2 = dict(mix_norm_e=v_mix_norm_e, w_in_e=v_w_in_e, conv_w_e=v_conv_w_e, conv_b_e=v_conv_b_e, ln_g_e=v_ln_g_e,
                ln_b_e=v_ln_b_e, w_pool_e=v_w_pool_e, pool_scale_e=v_pool_scale_e, w_out_e=v_w_out_e, mix_norm_o=v_mix_norm_o,
                w_in_o=v_w_in_o, conv_w_o=v_conv_w_o, w_out_o=v_w_out_o, ffn_norm=v_ffn_norm, w_gate=v_w_gate, w_up=v_w_up,
                w_down=v_w_down, final_norm=v_final_norm)
    names = list(weights)

    big = ("w_in_o", "w_out_o", "w_gate", "w_up", "w_down", "w_in_e", "w_out_e")
    delta, new_m, new_v = {}, {}, {}

    def update(nm):
        shape = weights[nm].shape
        as2d = lambda a: a.reshape(shape[0] * shape[1], shape[2])
        dl, mn, vn = _adamw("adamw_" + nm, as2d(weights[nm]), as2d(grad[nm]), as2d(mom1[nm]), as2d(mom2[nm]))
        delta[nm], new_m[nm], new_v[nm] = dl.reshape(shape), mn.reshape(shape), vn.reshape(shape)

    for nm in big[:5]:
        update(nm)
    reduce_finish("mix0", new_v["w_down"])
    g_w_in_e, g_w_out_e = reduced_shards("share_pieces_first", [(0,), (1,)])
    grad["w_in_e"], grad["w_out_e"] = g_w_in_e[None], g_w_out_e[None]
    for nm in big[5:]:
        update(nm)
    small = [nm for nm in names if nm not in big]
    shapes = [weights[nm].shape for nm in small]
    dl, mn, vn = _adamw("adamw_small", _pack([weights[nm] for nm in small]), _pack([grad[nm] for nm in small]),
                        _pack([mom1[nm] for nm in small]), _pack([mom2[nm] for nm in small]))
    for nm, a, b, c_ in zip(small, _unpack(dl, shapes), _unpack(mn, shapes), _unpack(vn, shapes)):
        delta[nm], new_m[nm], new_v[nm] = a, b, c_

    grad_x = dx.reshape(bsz, seq, d)
    return (loss, grad_x, *[grad[nm] for nm in names], *[delta[nm] for nm in names],
            *[new_m[nm] for nm in names], *[new_v[nm] for nm in names])
```

```python
import jax
import jax.numpy as jnp
from jax import lax
from jax.experimental import pallas as pl
from jax.experimental.pallas import tpu as pltpu

F32 = jnp.float32
BF16 = jnp.bfloat16
MESH_ID = pl.DeviceIdType.MESH
AXES = ("x", "y", "c")
N_CHIPS = 4
N_DEV = 8

EPS = 1e-6
POOL_WINDOWS = (2, 4, 8, 16)
ADAM_LR, ADAM_B1, ADAM_B2, ADAM_EPS, ADAM_WD, ADAM_STEP = 0.001, 0.9, 0.999, 1e-08, 0.01, 10

LANES = 128
CONV_HALO = 32
POOL_HALO = 16
SHORT_HALO = 8
V7X_VMEM_LIMIT = 56 * 1024 * 1024


def _cparams(*sem):
    return pltpu.CompilerParams(dimension_semantics=sem if sem else None, vmem_limit_bytes=V7X_VMEM_LIMIT)


def _pick(dim, prefs):
    for p in prefs:
        if p <= dim and dim % p == 0:
            return p
    return dim


def _sigmoid(x):
    return jax.nn.sigmoid(x)


_ANY = pl.BlockSpec(memory_space=pl.ANY)


def _behind(after, body, n_in):
    if not after:
        return body
    skip = len(after)

    def body_behind(*refs):
        return body(*refs[:n_in], *refs[n_in + skip:])

    return body_behind


_DOT_DIMS = {
    "nn": (((1,), (0,)), ((), ())),
    "nt": (((1,), (1,)), ((), ())),
    "tn": (((0,), (0,)), ((), ())),
}


def _mm(name, pairs, mode, out_dtypes, epilogue, extras=(), acc_of=None, tm=512, tn=512, tk=2048, after=()):
    a0, b0 = pairs[0]
    if mode == "nn":
        (m, k), n = a0.shape, b0.shape[1]
    elif mode == "nt":
        (m, k), n = a0.shape, b0.shape[0]
    else:
        (k, m), n = a0.shape, b0.shape[1]
    tm = _pick(m, (tm, 512, 256, 128, 64, 32, 16, 8))
    tn = _pick(n, (tn, 512, 256, 128))
    tk = _pick(k, (tk, 2048, 1024, 512, 256, 128))
    nk = k // tk
    n_pairs = len(pairs)
    acc_of = tuple(acc_of) if acc_of is not None else (0,) * n_pairs
    n_acc = max(acc_of) + 1
    n_ex, n_out = len(extras), len(out_dtypes)
    dims = _DOT_DIMS[mode]

    def body(*refs):
        a_refs = refs[:n_pairs]
        b_refs = refs[n_pairs:2 * n_pairs]
        e_refs = refs[2 * n_pairs:2 * n_pairs + n_ex]
        first_out = 2 * n_pairs + n_ex + len(after)
        o_refs = refs[first_out:first_out + n_out]
        acc_refs = refs[first_out + n_out:]

        def partial_sums():
            sums = [None] * n_acc
            for p in range(n_pairs):
                d = lax.dot_general(a_refs[p][...], b_refs[p][...], dims, preferred_element_type=F32)
                sums[acc_of[p]] = d if sums[acc_of[p]] is None else sums[acc_of[p]] + d
            return sums

        if nk == 1:
            epilogue(partial_sums(), e_refs, o_refs)
            return
        kk = pl.program_id(2)

        @pl.when(kk == 0)
        def _():
            for acc in acc_refs:
                acc[...] = jnp.zeros_like(acc)

        for acc, s in zip(acc_refs, partial_sums()):
            acc[...] += s

        @pl.when(kk == nk - 1)
        def _():
            epilogue([acc[...] for acc in acc_refs], e_refs, o_refs)

    if mode == "nn":
        a_spec = pl.BlockSpec((tm, tk), lambda i, j, kk: (i, kk))
        b_spec = pl.BlockSpec((tk, tn), lambda i, j, kk: (kk, j))
    elif mode == "nt":
        a_spec = pl.BlockSpec((tm, tk), lambda i, j, kk: (i, kk))
        b_spec = pl.BlockSpec((tn, tk), lambda i, j, kk: (j, kk))
    else:
        a_spec = pl.BlockSpec((tk, tm), lambda i, j, kk: (kk, i))
        b_spec = pl.BlockSpec((tk, tn), lambda i, j, kk: (kk, j))
    o_spec = pl.BlockSpec((tm, tn), lambda i, j, kk: (i, j))
    outs = pl.pallas_call(
        body,
        name=name,
        grid=(m // tm, n // tn, nk),
        in_specs=[a_spec] * n_pairs + [b_spec] * n_pairs + [o_spec] * n_ex
        + [pl.BlockSpec(memory_space=pl.ANY)] * len(after),
        out_specs=[o_spec] * n_out,
        out_shape=[jax.ShapeDtypeStruct((m, n), dt) for dt in out_dtypes],
        scratch_shapes=[pltpu.VMEM((tm, tn), F32) for _ in range(n_acc)] if nk > 1 else [],
        compiler_params=_cparams("parallel", "parallel", "arbitrary"),
    )(*[p[0] for p in pairs], *[p[1] for p in pairs], *extras, *after)
    return outs


def _ep_store(accs, ex, outs):
    outs[0][...] = accs[0].astype(outs[0].dtype)


def _ep_residual(accs, ex, outs):
    outs[0][...] = ex[0][...] + accs[0]


def _ep_swiglu(accs, ex, outs):
    g, u = accs
    outs[0][...] = g.astype(BF16)
    outs[1][...] = u.astype(BF16)
    outs[2][...] = (g * _sigmoid(g) * u).astype(BF16)


def _ep_swiglu_bwd(accs, ex, outs):
    d = accs[0]
    g = ex[0][...].astype(F32)
    u = ex[1][...].astype(F32)
    s = _sigmoid(g)
    outs[0][...] = (d * u * (s * (1.0 + g * (1.0 - s)))).astype(BF16)
    outs[1][...] = (d * (g * s)).astype(BF16)


def _rms_fwd(name, h, g, after=()):
    t, d = h.shape
    tr = _pick(t, (256, 128, 64, 32, 16, 8))

    def body(h_ref, g_ref, *rest):
        o_ref = rest[-1]
        x = h_ref[...]
        r = lax.rsqrt(jnp.mean(x * x, axis=-1, keepdims=True) + EPS)
        o_ref[...] = (x * r * g_ref[...]).astype(BF16)

    return pl.pallas_call(
        body, name=name, grid=(t // tr,),
        in_specs=[pl.BlockSpec((tr, d), lambda i: (i, 0)), pl.BlockSpec((1, d), lambda i: (0, 0))]
        + [pl.BlockSpec(memory_space=pl.ANY)] * len(after),
        out_specs=pl.BlockSpec((tr, d), lambda i: (i, 0)),
        out_shape=jax.ShapeDtypeStruct((t, d), BF16),
        compiler_params=_cparams("parallel"),
    )(h, g, *after)


def _rms_bwd(name, dn, h, g, dres, after=()):
    t, d = h.shape
    tr = _pick(t, (256, 128, 64, 32, 16, 8))

    def body(dn_ref, h_ref, g_ref, dres_ref, dh_ref, dhb_ref, dg_ref):
        x = h_ref[...]
        r = lax.rsqrt(jnp.mean(x * x, axis=-1, keepdims=True) + EPS)
        xhat = x * r
        dnv = dn_ref[...]

        @pl.when(pl.program_id(0) == 0)
        def _():
            dg_ref[...] = jnp.zeros_like(dg_ref)

        dg_ref[...] += jnp.sum(dnv * xhat, axis=0, keepdims=True)
        dxh = dnv * g_ref[...]
        dh = dres_ref[...] + r * (dxh - xhat * jnp.mean(dxh * xhat, axis=-1, keepdims=True))
        dh_ref[...] = dh
        dhb_ref[...] = dh.astype(BF16)

    row = pl.BlockSpec((tr, d), lambda i: (i, 0))
    vec = pl.BlockSpec((1, d), lambda i: (0, 0))
    return pl.pallas_call(
        _behind(after, body, 4), name=name, grid=(t // tr,),
        in_specs=[row, row, vec, row] + [_ANY] * len(after),
        out_specs=[row, row, vec],
        out_shape=[jax.ShapeDtypeStruct((t, d), F32), jax.ShapeDtypeStruct((t, d), BF16),
                   jax.ShapeDtypeStruct((1, d), F32)],
        compiler_params=_cparams("arbitrary"),
    )(dn, h, g, dres, *after)


def _loss_head(h, g, target, after=()):
    t, d = h.shape
    tr = _pick(t, (256, 128, 64, 32, 16, 8))

    def body(h_ref, g_ref, t_ref, loss_ref, dh_ref, dhb_ref, dg_ref):
        x = h_ref[...]
        gv = g_ref[...]
        r = lax.rsqrt(jnp.mean(x * x, axis=-1, keepdims=True) + EPS)
        xhat = x * r
        err = xhat * gv - t_ref[...]

        @pl.when(pl.program_id(0) == 0)
        def _():
            dg_ref[...] = jnp.zeros_like(dg_ref)
            loss_ref[...] = jnp.zeros_like(loss_ref)

        loss_ref[...] += jnp.full(loss_ref.shape, 0.5 / d, F32) * jnp.sum(err * err)
        dy = err * (1.0 / d)
        dg_ref[...] += jnp.sum(dy * xhat, axis=0, keepdims=True)
        dxh = dy * gv
        dh = r * (dxh - xhat * jnp.mean(dxh * xhat, axis=-1, keepdims=True))
        dh_ref[...] = dh
        dhb_ref[...] = dh.astype(BF16)

    row = pl.BlockSpec((tr, d), lambda i: (i, 0))
    vec = pl.BlockSpec((1, d), lambda i: (0, 0))
    return pl.pallas_call(
        _behind(after, body, 3), name="loss_head", grid=(t // tr,),
        in_specs=[row, vec, row] + [_ANY] * len(after),
        out_specs=[pl.BlockSpec((1, LANES), lambda i: (0, 0)), row, row, vec],
        out_shape=[jax.ShapeDtypeStruct((1, LANES), F32), jax.ShapeDtypeStruct((t, d), F32),
                   jax.ShapeDtypeStruct((t, d), BF16), jax.ShapeDtypeStruct((1, d), F32)],
        compiler_params=_cparams("arbitrary"),
    )(h, g, target, *after)


def _cur(ts, width, col):
    return pl.BlockSpec((ts, width), lambda i: (i, col))


def _prev_halo(ts, halo, width, col):
    per = ts // halo
    return pl.BlockSpec((halo, width), lambda i: (jnp.maximum(i * per - 1, 0), col))


def _next_halo(ts, halo, width, col, n_rows):
    per = ts // halo
    last = n_rows // halo - 1
    return pl.BlockSpec((halo, width), lambda i: (jnp.minimum((i + 1) * per, last), col))


def _full(shape):
    nd = len(shape)
    return pl.BlockSpec(shape, lambda i: (0,) * nd)


def _shift_down(x, n):
    return x if n == 0 else pltpu.roll(x, n, 0)


def _shift_up(x, n):
    return x if n == 0 else pltpu.roll(x, x.shape[0] - n, 0)


def _pool_counts(i, ns, ts, w):
    pos = (i % ns) * ts + lax.broadcasted_iota(jnp.int32, (ts, 1), 0)
    return jnp.minimum(pos + 1, w).astype(F32)


def _pooled(cur, prev_tail, w, cnt):
    s = jnp.concatenate([prev_tail, cur], axis=0)
    d = 1
    while d < w:
        s = s + _shift_down(s, d)
        d *= 2
    return s[POOL_HALO:, :] / cnt - cur


def _mixer_e_fwd(u, conv_w, conv_b, ln_g, ln_b, w_pool, scale, seq, ts, after=()):
    t = u.shape[0]
    dc = conv_b.shape[1]
    ng, pg = w_pool.shape[0], w_pool.shape[1]
    taps = conv_w.shape[0]
    ns = seq // ts

    def body(val_ref, gate_ref, b_ref, pval_ref, pgate_ref, pb_ref, cw_ref, cb_ref, g_ref, be_ref, wp_ref, sc_ref,
             a2_ref, cat_ref):
        i = pl.program_id(0)
        keep_prev = jnp.where(i % ns == 0, 0.0, 1.0)
        a1 = val_ref[...] * _sigmoid(gate_ref[...])
        pa1 = pval_ref[...] * _sigmoid(pgate_ref[...]) * keep_prev
        ext = jnp.concatenate([pa1, a1], axis=0)
        acc = jnp.zeros_like(ext)
        for k in range(taps):
            acc = acc + cw_ref[k:k + 1, :] * _shift_down(ext, taps - 1 - k)
        a2 = acc[CONV_HALO:, :] + cb_ref[...]
        a2_ref[...] = a2
        mu = jnp.mean(a2, axis=-1, keepdims=True)
        xc = a2 - mu
        rstd = lax.rsqrt(jnp.mean(xc * xc, axis=-1, keepdims=True) + EPS)
        a3 = xc * rstd * g_ref[...] + be_ref[...]
        cat_ref[:, 0:dc] = (a3 * _sigmoid(a3)).astype(BF16)
        for g in range(ng):
            lo, hi = g * pg, (g + 1) * pg
            w = POOL_WINDOWS[g]
            p = _pooled(b_ref[:, lo:hi], pb_ref[:, lo:hi] * keep_prev, w, _pool_counts(i, ns, ts, w))
            q = jnp.dot(p.astype(BF16), wp_ref[g].astype(BF16), preferred_element_type=F32)
            cat_ref[:, dc + lo:dc + hi] = (q * sc_ref[:, lo:hi]).astype(BF16)

    return pl.pallas_call(
        _behind(after, body, 12), name="mixer_e_fwd", grid=(t // ts,),
        in_specs=[_cur(ts, dc, 0), _cur(ts, dc, 1), _cur(ts, dc, 2),
                  _prev_halo(ts, CONV_HALO, dc, 0), _prev_halo(ts, CONV_HALO, dc, 1), _prev_halo(ts, POOL_HALO, dc, 2),
                  _full(conv_w.shape), _full(conv_b.shape), _full(ln_g.shape), _full(ln_b.shape),
                  _full(w_pool.shape), _full(scale.shape)] + [_ANY] * len(after),
        out_specs=[_cur(ts, dc, 0), _cur(ts, 2 * dc, 0)],
        out_shape=[jax.ShapeDtypeStruct((t, dc), F32), jax.ShapeDtypeStruct((t, 2 * dc), BF16)],
        compiler_params=_cparams("parallel"),
    )(u, u, u, u, u, u, conv_w, conv_b, ln_g, ln_b, w_pool, scale, *after)


def _mixer_e_bwd_norm(dcat, a2, ln_g, ln_b, ts, after=()):
    t, dc = a2.shape

    def body(d_ref, a2_ref, g_ref, be_ref, da2_ref, dg_ref, db_ref, dcb_ref):
        x = a2_ref[...]
        gv = g_ref[...]
        mu = jnp.mean(x, axis=-1, keepdims=True)
        xc = x - mu
        rstd = lax.rsqrt(jnp.mean(xc * xc, axis=-1, keepdims=True) + EPS)
        xhat = xc * rstd
        a3 = xhat * gv + be_ref[...]
        sg = _sigmoid(a3)
        da3 = d_ref[...] * (sg * (1.0 + a3 * (1.0 - sg)))
        dxh = da3 * gv
        da2 = rstd * (dxh - jnp.mean(dxh, axis=-1, keepdims=True)
                      - xhat * jnp.mean(dxh * xhat, axis=-1, keepdims=True))
        da2_ref[...] = da2

        @pl.when(pl.program_id(0) == 0)
        def _():
            dg_ref[...] = jnp.zeros_like(dg_ref)
            db_ref[...] = jnp.zeros_like(db_ref)
            dcb_ref[...] = jnp.zeros_like(dcb_ref)

        dg_ref[...] += jnp.sum(da3 * xhat, axis=0, keepdims=True)
        db_ref[...] += jnp.sum(da3, axis=0, keepdims=True)
        dcb_ref[...] += jnp.sum(da2, axis=0, keepdims=True)

    vec = _full((1, dc))
    return pl.pallas_call(
        _behind(after, body, 4), name="mixer_e_bwd_norm", grid=(t // ts,),
        in_specs=[_cur(ts, dc, 0), _cur(ts, dc, 0), vec, vec] + [_ANY] * len(after),
        out_specs=[_cur(ts, dc, 0), vec, vec, vec],
        out_shape=[jax.ShapeDtypeStruct((t, dc), F32)] + [jax.ShapeDtypeStruct((1, dc), F32)] * 3,
        compiler_params=_cparams("arbitrary"),
    )(dcat, a2, ln_g, ln_b, *after)


def _mixer_e_bwd_mix(da2, dcat, u, conv_w, w_pool, scale, seq, ts, after=()):
    t, dc = da2.shape
    ng, pg = w_pool.shape[0], w_pool.shape[1]
    taps = conv_w.shape[0]
    ns = seq // ts

    def body(da2_ref, nda2_ref, dp_ref, ndp_ref, val_ref, gate_ref, b_ref, pval_ref, pgate_ref, pb_ref,
             cw_ref, wp_ref, sc_ref, du_ref, dcw_ref, dwp_ref, dsc_ref):
        i = pl.program_id(0)
        keep_prev = jnp.where(i % ns == 0, 0.0, 1.0)
        keep_next = jnp.where(i % ns == ns - 1, 0.0, 1.0)

        @pl.when(i == 0)
        def _():
            dcw_ref[...] = jnp.zeros_like(dcw_ref)
            dwp_ref[...] = jnp.zeros_like(dwp_ref)
            dsc_ref[...] = jnp.zeros_like(dsc_ref)

        val = val_ref[...]
        sg = _sigmoid(gate_ref[...])
        a1 = val * sg
        pa1 = pval_ref[...] * _sigmoid(pgate_ref[...]) * keep_prev
        ext_a = jnp.concatenate([pa1, a1], axis=0)
        da2v = da2_ref[...]
        ext_d = jnp.concatenate([da2v, nda2_ref[...] * keep_next], axis=0)
        da1 = jnp.zeros_like(ext_d)
        for k in range(taps):
            sh = taps - 1 - k
            dcw_ref[k:k + 1, :] += jnp.sum(da2v * _shift_down(ext_a, sh)[CONV_HALO:, :], axis=0, keepdims=True)
            da1 = da1 + cw_ref[k:k + 1, :] * _shift_up(ext_d, sh)
        da1 = da1[:ts, :]
        du_ref[:, 0:dc] = (da1 * sg).astype(BF16)
        du_ref[:, dc:2 * dc] = (da1 * a1 * (1.0 - sg)).astype(BF16)

        for g in range(ng):
            lo, hi = g * pg, (g + 1) * pg
            w = POOL_WINDOWS[g]
            cnt = _pool_counts(i, ns, ts, w)
            wpb = wp_ref[g].astype(BF16)
            sc = sc_ref[:, lo:hi]
            p = _pooled(b_ref[:, lo:hi], pb_ref[:, lo:hi] * keep_prev, w, cnt)
            pb16 = p.astype(BF16)
            q = jnp.dot(pb16, wpb, preferred_element_type=F32)
            dout = dp_ref[:, lo:hi]
            dsc_ref[:, lo:hi] += jnp.sum(dout * q, axis=0, keepdims=True)
            dq = (dout * sc).astype(BF16)
            dwp_ref[g] += lax.dot_general(pb16, dq, _DOT_DIMS["tn"], preferred_element_type=F32)
            dpool = lax.dot_general(dq, wpb, _DOT_DIMS["nt"], preferred_element_type=F32)
            ndq = (ndp_ref[:, lo:hi] * sc * keep_next).astype(BF16)
            ndpool = lax.dot_general(ndq, wpb, _DOT_DIMS["nt"], preferred_element_type=F32)
            s = jnp.concatenate([dpool / cnt, ndpool * (1.0 / w)], axis=0)
            d = 1
            while d < w:
                s = s + _shift_up(s, d)
                d *= 2
            du_ref[:, 2 * dc + lo:2 * dc + hi] = (s[:ts, :] - dpool).astype(BF16)

    return pl.pallas_call(
        _behind(after, body, 13), name="mixer_e_bwd_mix", grid=(t // ts,),
        in_specs=[_cur(ts, dc, 0), _next_halo(ts, CONV_HALO, dc, 0, t),
                  _cur(ts, dc, 1), _next_halo(ts, POOL_HALO, dc, 1, t),
                  _cur(ts, dc, 0), _cur(ts, dc, 1), _cur(ts, dc, 2),
                  _prev_halo(ts, CONV_HALO, dc, 0), _prev_halo(ts, CONV_HALO, dc, 1), _prev_halo(ts, POOL_HALO, dc, 2),
                  _full(conv_w.shape), _full(w_pool.shape), _full(scale.shape)] + [_ANY] * len(after),
        out_specs=[_cur(ts, 3 * dc, 0), _full(conv_w.shape), _full(w_pool.shape), _full(scale.shape)],
        out_shape=[jax.ShapeDtypeStruct((t, 3 * dc), BF16), jax.ShapeDtypeStruct(conv_w.shape, F32),
                   jax.ShapeDtypeStruct(w_pool.shape, F32), jax.ShapeDtypeStruct(scale.shape, F32)],
        compiler_params=_cparams("arbitrary"),
    )(da2, da2, dcat, dcat, u, u, u, u, u, u, conv_w, w_pool, scale, *after)


def _mixer_o_fwd(u, conv_w, seq, ts, after=()):
    t = u.shape[0]
    d = conv_w.shape[1]
    taps = conv_w.shape[0]
    ns = seq // ts

    def body(gb_ref, gc_ref, v_ref, pgc_ref, pv_ref, cw_ref, y_ref):
        keep_prev = jnp.where(pl.program_id(0) % ns == 0, 0.0, 1.0)
        ext = jnp.concatenate([pgc_ref[...] * pv_ref[...] * keep_prev, gc_ref[...] * v_ref[...]], axis=0)
        cc = jnp.zeros_like(ext)
        for k in range(taps):
            cc = cc + cw_ref[k:k + 1, :] * _shift_down(ext, taps - 1 - k)
        y_ref[...] = (gb_ref[...] * cc[SHORT_HALO:, :]).astype(BF16)

    return pl.pallas_call(
        _behind(after, body, 6), name="mixer_o_fwd", grid=(t // ts,),
        in_specs=[_cur(ts, d, 0), _cur(ts, d, 1), _cur(ts, d, 2),
                  _prev_halo(ts, SHORT_HALO, d, 1), _prev_halo(ts, SHORT_HALO, d, 2), _full(conv_w.shape)]
        + [_ANY] * len(after),
        out_specs=_cur(ts, d, 0),
        out_shape=jax.ShapeDtypeStruct((t, d), BF16),
        compiler_params=_cparams("parallel"),
    )(u, u, u, u, u, conv_w, *after)


def _mixer_o_bwd(dy, u, conv_w, seq, ts, after=()):
    t = u.shape[0]
    d = conv_w.shape[1]
    taps = conv_w.shape[0]
    ns = seq // ts

    def body(dy_ref, ndy_ref, gb_ref, gc_ref, v_ref, pgc_ref, pv_ref, ngb_ref, cw_ref, du_ref, dcw_ref):
        i = pl.program_id(0)
        keep_prev = jnp.where(i % ns == 0, 0.0, 1.0)
        keep_next = jnp.where(i % ns == ns - 1, 0.0, 1.0)

        @pl.when(i == 0)
        def _():
            dcw_ref[...] = jnp.zeros_like(dcw_ref)

        gb, gc, v, dyv = gb_ref[...], gc_ref[...], v_ref[...], dy_ref[...]
        ext = jnp.concatenate([pgc_ref[...] * pv_ref[...] * keep_prev, gc * v], axis=0)
        dcc = dyv * gb
        ext_d = jnp.concatenate([dcc, ndy_ref[...] * ngb_ref[...] * keep_next], axis=0)
        cc = jnp.zeros_like(ext)
        dcv = jnp.zeros_like(ext_d)
        for k in range(taps):
            sh = taps - 1 - k
            shifted = _shift_down(ext, sh)
            cc = cc + cw_ref[k:k + 1, :] * shifted
            dcw_ref[k:k + 1, :] += jnp.sum(dcc * shifted[SHORT_HALO:, :], axis=0, keepdims=True)
            dcv = dcv + cw_ref[k:k + 1, :] * _shift_up(ext_d, sh)
        dcv = dcv[:ts, :]
        du_ref[:, 0:d] = (dyv * cc[SHORT_HALO:, :]).astype(BF16)
        du_ref[:, d:2 * d] = (dcv * v).astype(BF16)
        du_ref[:, 2 * d:3 * d] = (dcv * gc).astype(BF16)

    return pl.pallas_call(
        _behind(after, body, 9), name="mixer_o_bwd", grid=(t // ts,),
        in_specs=[_cur(ts, d, 0), _next_halo(ts, SHORT_HALO, d, 0, t),
                  _cur(ts, d, 0), _cur(ts, d, 1), _cur(ts, d, 2),
                  _prev_halo(ts, SHORT_HALO, d, 1), _prev_halo(ts, SHORT_HALO, d, 2),
                  _next_halo(ts, SHORT_HALO, d, 0, t), _full(conv_w.shape)] + [_ANY] * len(after),
        out_specs=[_cur(ts, 3 * d, 0), _full(conv_w.shape)],
        out_shape=[jax.ShapeDtypeStruct((t, 3 * d), BF16), jax.ShapeDtypeStruct(conv_w.shape, F32)],
        compiler_params=_cparams("arbitrary"),
    )(dy, dy, u, u, u, u, u, u, conv_w, *after)


def _cast_into_full(name, mat, w, layer, chip, after=()):
    tr = _pick(mat.sr, (512, 256, 128, 64, 32, 16))
    per = mat.sr // tr

    def body(chip_ref, w_ref, *rest):
        o_ref = rest[-1]
        o_ref[...] = w_ref[...].astype(BF16)

    if mat.kind == "col":
        o_spec = pl.BlockSpec((tr, mat.sc), lambda i, chip_ref: (i, chip_ref[0]))
    else:
        o_spec = pl.BlockSpec((tr, mat.sc), lambda i, chip_ref: (chip_ref[0] * per + i, 0))
    return pl.pallas_call(
        body, name=name,
        grid_spec=pltpu.PrefetchScalarGridSpec(
            num_scalar_prefetch=1, grid=(per,),
            in_specs=[pl.BlockSpec((None, tr, mat.sc), lambda i, chip_ref: (layer, i, 0))] + [_ANY] * len(after),
            out_specs=o_spec),
        out_shape=jax.ShapeDtypeStruct(mat.full_shape, BF16),
        compiler_params=_cparams("parallel"),
    )(chip, w, *after)


def _adamw(name, w, g, m, v, after=()):
    r, c = w.shape
    tr = _pick(r, (256, 128, 64, 32, 16, 8)) if c > 1024 else _pick(r, (512, 256, 128, 64, 32, 16, 8))
    bc1 = 1.0 - ADAM_B1 ** ADAM_STEP
    bc2 = 1.0 - ADAM_B2 ** ADAM_STEP

    def body(w_ref, g_ref, m_ref, v_ref, d_ref, mo_ref, vo_ref):
        gv = g_ref[...]
        mn = ADAM_B1 * m_ref[...] + (1.0 - ADAM_B1) * gv
        vn = ADAM_B2 * v_ref[...] + (1.0 - ADAM_B2) * (gv * gv)
        mo_ref[...] = mn
        vo_ref[...] = vn
        d_ref[...] = -ADAM_LR * ((mn / bc1) / (jnp.sqrt(vn / bc2) + ADAM_EPS) + ADAM_WD * w_ref[...])

    spec = pl.BlockSpec((tr, c), lambda i: (i, 0))
    return pl.pallas_call(_behind(after, body, 4), name=name, grid=(r // tr,),
                          in_specs=[spec] * 4 + [_ANY] * len(after), out_specs=[spec] * 3,
                          out_shape=[jax.ShapeDtypeStruct((r, c), F32)] * 3,
                          compiler_params=_cparams("parallel"))(w, g, m, v, *after)


def _aligned(offset, multiple):
    return offset if isinstance(offset, int) else pl.multiple_of(offset, multiple)


class _Mat:
    def __init__(self, kind, shard_shape):
        self.kind = kind
        self.sr, self.sc = shard_shape
        self.full_shape = (self.sr, self.sc * N_CHIPS) if kind == "col" else (self.sr * N_CHIPS, self.sc)
        self.pr, self.pc = self.sr // 2, self.sc

    def piece(self, ref, k, h):
        if self.kind == "col":
            return ref.at[pl.ds(_aligned(h * self.pr, 16), self.pr), pl.ds(_aligned(k * self.sc, LANES), self.sc)]
        return ref.at[pl.ds(_aligned(k * self.sr + h * self.pr, 16), self.pr), :]

    def shard(self, ref, k):
        if self.kind == "col":
            return ref.at[:, pl.ds(_aligned(k * self.sc, LANES), self.sc)]
        return ref.at[pl.ds(_aligned(k * self.sr, 16), self.sr), :]

    def half(self, ref, h):
        return ref.at[pl.ds(_aligned(h * self.pr, 16), self.pr), :]


def _place():
    x, y, c = lax.axis_index("x"), lax.axis_index("y"), lax.axis_index("c")
    others = [(1 - x, y), (x, 1 - y), (1 - x, 1 - y)]
    return x, y, c, others


_HBM = pl.BlockSpec(memory_space=pltpu.HBM)
_SEM = pl.BlockSpec(memory_space=pltpu.SEMAPHORE)
_TOKEN = jax.ShapeDtypeStruct((8, LANES), F32)
_TOKEN_SPEC = pl.BlockSpec(memory_space=pltpu.VMEM)


def _split_params():
    return pltpu.CompilerParams(has_side_effects=pltpu.SideEffectType.DATAFLOW_SIDE_EFFECTING)


def _in_hbm(a):
    return pltpu.with_memory_space_constraint(a, pltpu.HBM)


def _copy_to(src, dst, send_sem, recv_sem, to):
    return pltpu.make_async_remote_copy(src_ref=src, dst_ref=dst, send_sem=send_sem, recv_sem=recv_sem,
                                        device_id=to, device_id_type=MESH_ID)


def _place_small(packed, chip):
    rows, cols = packed.shape

    def body(chip_ref, p_ref, o_ref):
        o_ref[...] = p_ref[...]

    return pl.pallas_call(
        body, name="place_small",
        grid_spec=pltpu.PrefetchScalarGridSpec(
            num_scalar_prefetch=1, grid=(1,),
            in_specs=[pl.BlockSpec((rows, cols), lambda i, chip_ref: (0, 0))],
            out_specs=pl.BlockSpec((None, rows, cols), lambda i, chip_ref: (chip_ref[0], 0, 0))),
        out_shape=jax.ShapeDtypeStruct((N_CHIPS, rows, cols), F32),
        compiler_params=_cparams("arbitrary"),
    )(chip, packed)


def _gather_start(name, gmats, gfulls, small_all=None):
    n = len(gmats)
    n_in = n + (1 if small_all is not None else 0)

    def body(*refs):
        full_refs = refs[:n]
        outs = refs[n_in:]
        send_sem, recv_sem, token = outs[n_in], outs[n_in + 1], outs[n_in + 2]
        x, y, c, others = _place()
        me_k = 2 * x + y
        if small_all is not None:
            mine = refs[n].at[me_k]
            for ox, oy in others:
                _copy_to(mine, mine, send_sem, recv_sem, (ox, oy, c)).start()
        for m in range(n):
            mine = gmats[m].piece(full_refs[m], me_k, c)
            for ox, oy in others:
                _copy_to(mine, mine, send_sem, recv_sem, (ox, oy, c)).start()
        token[...] = jnp.zeros_like(token)

    operands = [_in_hbm(f) for f in gfulls] + ([_in_hbm(small_all)] if small_all is not None else [])
    outs = pl.pallas_call(
        body, name=name,
        in_specs=[_HBM] * n_in,
        out_specs=[_HBM] * n_in + [_SEM, _SEM, _TOKEN_SPEC],
        out_shape=[pltpu.HBM(a.shape, a.dtype) for a in operands] + [pltpu.SemaphoreType.DMA(())] * 2 + [_TOKEN],
        input_output_aliases={m: m for m in range(n_in)},
        compiler_params=_split_params(),
    )(*operands)
    return list(outs[:n]), (outs[n] if small_all is not None else None), (outs[n_in], outs[n_in + 1]), outs[n_in + 2]


def _gather_pass(name, gmats, gfulls, small_all, sems, after):
    k = len(gmats)
    n_buf = k + (1 if small_all is not None else 0)

    def body(*refs):
        bufs = refs[:n_buf]
        send_sem, recv_sem = refs[n_buf], refs[n_buf + 1]
        outs = refs[n_buf + 3:]
        fsend, frecv, token = outs[n_buf], outs[n_buf + 1], outs[n_buf + 2]
        x, y, c, others = _place()
        me_k = 2 * x + y
        sibling = (x, y, 1 - c)
        for m in range(k):
            for ox, oy in others:
                got = gmats[m].piece(bufs[m], 2 * ox + oy, c)
                _copy_to(got, got, send_sem, recv_sem, sibling).wait_recv()
        if small_all is not None:
            for ox, oy in others:
                got = bufs[k].at[2 * ox + oy]
                _copy_to(got, got, send_sem, recv_sem, sibling).wait_recv()
        for m in range(k):
            mine = gmats[m].piece(bufs[m], me_k, c)
            for _ in others:
                _copy_to(mine, mine, send_sem, recv_sem, sibling).wait_send()
        if small_all is not None:
            for _ in others:
                _copy_to(bufs[k].at[me_k], bufs[k].at[me_k], send_sem, recv_sem, sibling).wait_send()
        for m in range(k):
            for ox, oy in others:
                got = gmats[m].piece(bufs[m], 2 * ox + oy, c)
                _copy_to(got, got, fsend, frecv, sibling).start()
        token[...] = jnp.zeros_like(token)

    operands = [_in_hbm(f) for f in gfulls] + ([_in_hbm(small_all)] if small_all is not None else [])
    outs = pl.pallas_call(
        body, name=name,
        in_specs=[_HBM] * n_buf + [_SEM, _SEM, _ANY],
        out_specs=[_HBM] * n_buf + [_SEM, _SEM, _TOKEN_SPEC],
        out_shape=[pltpu.HBM(a.shape, a.dtype) for a in operands] + [pltpu.SemaphoreType.DMA(())] * 2 + [_TOKEN],
        input_output_aliases={i: i for i in range(n_buf)},
        compiler_params=_split_params(),
    )(*operands, sems[0], sems[1], after)
    return list(outs[:n_buf]), (outs[n_buf], outs[n_buf + 1]), outs[n_buf + 2]


def _gather_done(name, gmats, gfulls, sems, after):
    k = len(gmats)

    def body(*refs):
        bufs = refs[:k]
        send_sem, recv_sem = refs[k], refs[k + 1]
        x, y, c, others = _place()
        sibling = (x, y, 1 - c)
        for m in range(k):
            for ox, oy in others:
                got = gmats[m].piece(bufs[m], 2 * ox + oy, 1 - c)
                _copy_to(got, got, send_sem, recv_sem, sibling).wait_recv()
        for m in range(k):
            for ox, oy in others:
                sent = gmats[m].piece(bufs[m], 2 * ox + oy, c)
                _copy_to(sent, sent, send_sem, recv_sem, sibling).wait_send()

    outs = pl.pallas_call(
        body, name=name,
        in_specs=[_HBM] * k + [_SEM, _SEM, _ANY], out_specs=[_HBM] * k,
        out_shape=[pltpu.HBM(a.shape, a.dtype) for a in gfulls],
        input_output_aliases={i: i for i in range(k)},
        compiler_params=_split_params(),
    )(*[_in_hbm(f) for f in gfulls], sems[0], sems[1], after)
    return list(outs)


_FLIPS = [(fx, fy, fc) for fx in (0, 1) for fy in (0, 1) for fc in (0, 1) if (fx, fy, fc) != (0, 0, 0)]


def _small_start(packed):
    def body(small_ref, small_thru, land_ref, send_sem, recv_sem, token):
        x, y, c, _ = _place()
        me = 4 * x + 2 * y + c
        for fx, fy, fc in _FLIPS:
            _copy_to(small_ref, land_ref.at[me], send_sem, recv_sem, (x ^ fx, y ^ fy, c ^ fc)).start()
        token[...] = jnp.zeros_like(token)

    outs = pl.pallas_call(
        body, name="small_grads_start",
        in_specs=[_HBM], out_specs=[_HBM, _HBM, _SEM, _SEM, _TOKEN_SPEC],
        out_shape=[pltpu.HBM(packed.shape, F32), pltpu.HBM((N_DEV,) + packed.shape, F32)]
        + [pltpu.SemaphoreType.DMA(())] * 2 + [_TOKEN],
        input_output_aliases={0: 0},
        compiler_params=_split_params(),
    )(_in_hbm(packed))
    return outs[0], outs[1], (outs[2], outs[3]), outs[4]


def _small_wait(packed, landed, sems, after):
    def body(small_ref, land_ref, send_sem, recv_sem, after_ref, small_thru, land_thru):
        x, y, c, _ = _place()
        for fx, fy, fc in _FLIPS:
            got = land_ref.at[4 * (x ^ fx) + 2 * (y ^ fy) + (c ^ fc)]
            _copy_to(got, got, send_sem, recv_sem, (x, y, 1 - c)).wait_recv()
        for _ in _FLIPS:
            _copy_to(small_ref, small_ref, send_sem, recv_sem, (x, y, 1 - c)).wait_send()

    outs = pl.pallas_call(
        body, name="small_grads_wait",
        in_specs=[_HBM, _HBM, _SEM, _SEM, _ANY], out_specs=[_HBM, _HBM],
        out_shape=[pltpu.HBM(packed.shape, F32), pltpu.HBM(landed.shape, F32)],
        input_output_aliases={0: 0, 1: 1},
        compiler_params=_split_params(),
    )(_in_hbm(packed), _in_hbm(landed), sems[0], sems[1], after)
    return outs[0], outs[1]


def _exchange_start(name, mats, grads):
    n = len(mats)

    def body(*refs):
        g_refs = refs[:n]
        outs = refs[n:]
        land_refs = outs[n:2 * n]
        send_sem, recv_sem, token = outs[2 * n], outs[2 * n + 1], outs[2 * n + 2]
        x, y, c, _ = _place()
        for m in range(n):
            for k in range(N_CHIPS):
                _copy_to(mats[m].piece(g_refs[m], k, 1 - c), land_refs[m].at[k], send_sem, recv_sem, (x, y, 1 - c)).start()
        token[...] = jnp.zeros_like(token)

    outs = pl.pallas_call(
        body, name=name,
        in_specs=[_HBM] * n,
        out_specs=[_HBM] * (2 * n) + [_SEM, _SEM, _TOKEN_SPEC],
        out_shape=[pltpu.HBM(mt.full_shape, BF16) for mt in mats]
        + [pltpu.HBM((N_CHIPS, mt.pr, mt.pc), BF16) for mt in mats] + [pltpu.SemaphoreType.DMA(())] * 2 + [_TOKEN],
        input_output_aliases={m: m for m in range(n)},
        compiler_params=_split_params(),
    )(*[_in_hbm(g) for g in grads])
    return list(outs[:n]), list(outs[n:2 * n]), (outs[2 * n], outs[2 * n + 1]), outs[2 * n + 2]


def _exchange_wait(name, mats, grads, landed, sems, after):
    n = len(mats)

    def body(*refs):
        g_refs, land_refs = refs[:n], refs[n:2 * n]
        send_sem, recv_sem = refs[2 * n], refs[2 * n + 1]
        x, y, c, _ = _place()
        for m in range(n):
            for k in range(N_CHIPS):
                got = land_refs[m].at[k]
                _copy_to(got, got, send_sem, recv_sem, (x, y, 1 - c)).wait_recv()
        for m in range(n):
            for k in range(N_CHIPS):
                sent = mats[m].piece(g_refs[m], k, 1 - c)
                _copy_to(sent, sent, send_sem, recv_sem, (x, y, 1 - c)).wait_send()

    outs = pl.pallas_call(
        body, name=name,
        in_specs=[_HBM] * (2 * n) + [_SEM, _SEM, _ANY], out_specs=[_HBM] * (2 * n),
        out_shape=[pltpu.HBM(a.shape, a.dtype) for a in list(grads) + list(landed)],
        input_output_aliases={i: i for i in range(2 * n)},
        compiler_params=_split_params(),
    )(*[_in_hbm(a) for a in list(grads) + list(landed)], sems[0], sems[1], after)
    return list(outs[:n]), list(outs[n:])


def _add_halves(name, mat, grad, landed, core):
    tr = _pick(mat.pr, (256, 128, 64, 32, 16))
    per = mat.pr // tr

    def body(core_ref, g_ref, l_ref, o_ref):
        o_ref[...] = (g_ref[...].astype(F32) + l_ref[...].astype(F32)).astype(BF16)

    if mat.kind == "col":
        g_spec = pl.BlockSpec((tr, mat.pc), lambda k, r, core_ref: (core_ref[0] * per + r, k))
    else:
        g_spec = pl.BlockSpec((tr, mat.pc), lambda k, r, core_ref: ((2 * k + core_ref[0]) * per + r, 0))
    p_spec = pl.BlockSpec((None, tr, mat.pc), lambda k, r, core_ref: (k, r, 0))
    return pl.pallas_call(
        body, name=name,
        grid_spec=pltpu.PrefetchScalarGridSpec(num_scalar_prefetch=1, grid=(N_CHIPS, per),
                                               in_specs=[g_spec, p_spec], out_specs=p_spec),
        out_shape=jax.ShapeDtypeStruct((N_CHIPS, mat.pr, mat.pc), BF16),
        compiler_params=_cparams("parallel", "parallel"),
    )(core, grad, landed)


def _scatter_start(name, mats, partials):
    n = len(mats)

    def body(*refs):
        p_refs = refs[:n]
        outs = refs[n:]
        land_refs = outs[n:2 * n]
        send_sem, recv_sem, token = outs[2 * n], outs[2 * n + 1], outs[2 * n + 2]
        x, y, c, others = _place()
        me_k = 2 * x + y
        for m in range(n):
            for ox, oy in others:
                _copy_to(p_refs[m].at[2 * ox + oy], land_refs[m].at[me_k], send_sem, recv_sem, (ox, oy, c)).start()
        token[...] = jnp.zeros_like(token)

    piece_shapes = [pltpu.HBM((N_CHIPS, mt.pr, mt.pc), BF16) for mt in mats]
    outs = pl.pallas_call(
        body, name=name,
        in_specs=[_HBM] * n,
        out_specs=[_HBM] * (2 * n) + [_SEM, _SEM, _TOKEN_SPEC],
        out_shape=piece_shapes + piece_shapes + [pltpu.SemaphoreType.DMA(())] * 2 + [_TOKEN],
        input_output_aliases={m: m for m in range(n)},
        compiler_params=_split_params(),
    )(*[_in_hbm(p) for p in partials])
    return list(outs[:n]), list(outs[n:2 * n]), (outs[2 * n], outs[2 * n + 1]), outs[2 * n + 2]


def _scatter_wait(name, mats, partials, landed, sems, after):
    n = len(mats)

    def body(*refs):
        p_refs, land_refs = refs[:n], refs[n:2 * n]
        send_sem, recv_sem = refs[2 * n], refs[2 * n + 1]
        x, y, c, others = _place()
        for m in range(n):
            for ox, oy in others:
                got = land_refs[m].at[2 * ox + oy]
                _copy_to(got, got, send_sem, recv_sem, (ox, oy, c)).wait_recv()
        for m in range(n):
            for ox, oy in others:
                sent = p_refs[m].at[2 * ox + oy]
                _copy_to(sent, sent, send_sem, recv_sem, (ox, oy, c)).wait_send()

    outs = pl.pallas_call(
        body, name=name,
        in_specs=[_HBM] * (2 * n) + [_SEM, _SEM, _ANY], out_specs=[_HBM] * (2 * n),
        out_shape=[pltpu.HBM(a.shape, a.dtype) for a in list(partials) + list(landed)],
        input_output_aliases={i: i for i in range(2 * n)},
        compiler_params=_split_params(),
    )(*[_in_hbm(a) for a in list(partials) + list(landed)], sems[0], sems[1], after)
    return list(outs[:n]), list(outs[n:])


def _sum_chips(name, mat, partial, landed, slots, layer=None, stack=None, n_layers=1):
    tr = _pick(mat.pr, (256, 128, 64, 32, 16))
    per = mat.pr // tr

    def body(slots_ref, own_ref, a_ref, b_ref, c_ref, *rest):
        o_ref = rest[-1]
        o_ref[...] = ((own_ref[...].astype(F32) + a_ref[...].astype(F32)) + b_ref[...].astype(F32)) + c_ref[...].astype(F32)

    def slot_spec(which):
        return pl.BlockSpec((None, tr, mat.pc), lambda r, slots_ref: (slots_ref[which], r, 0))

    in_specs = [slot_spec(0), slot_spec(1), slot_spec(2), slot_spec(3)]
    operands = [slots, partial, landed, landed, landed]
    aliases = {}
    if layer is None:
        o_spec = pl.BlockSpec((tr, mat.pc), lambda r, slots_ref: (slots_ref[4] * per + r, 0))
        out_shape = jax.ShapeDtypeStruct((mat.sr, mat.sc), F32)
    else:
        o_spec = pl.BlockSpec((None, tr, mat.pc), lambda r, slots_ref: (layer, slots_ref[4] * per + r, 0))
        out_shape = jax.ShapeDtypeStruct((n_layers, mat.sr, mat.sc), F32)
        if stack is not None:
            in_specs.append(_ANY)
            operands.append(stack)
            aliases = {len(operands) - 1: 0}
    return pl.pallas_call(
        body, name=name,
        grid_spec=pltpu.PrefetchScalarGridSpec(num_scalar_prefetch=1, grid=(per,), in_specs=in_specs, out_specs=o_spec),
        out_shape=out_shape, input_output_aliases=aliases,
        compiler_params=_cparams("parallel"),
    )(*operands)


def _share_pieces(name, mats, shards, groups):
    n = len(mats)
    n_out = len(groups)

    def body(*refs):
        out_refs = refs[n_out:2 * n_out]
        send_sems, recv_sems = refs[2 * n_out:]
        x, y, c, _ = _place()
        sibling = (x, y, 1 - c)
        sent, waits = [], []
        for o, members in enumerate(groups):
            for l, m in enumerate(members):
                dst = out_refs[o].at[l] if len(members) > 1 else out_refs[o]
                mine = mats[m].half(dst, c)
                sent.append(pltpu.make_async_remote_copy(src_ref=mine, dst_ref=mine, send_sem=send_sems.at[m],
                                                         recv_sem=recv_sems.at[m], device_id=sibling, device_id_type=MESH_ID))
                theirs = mats[m].half(dst, 1 - c)
                waits.append(pltpu.make_async_remote_copy(src_ref=theirs, dst_ref=theirs, send_sem=send_sems.at[m],
                                                          recv_sem=recv_sems.at[m], device_id=sibling, device_id_type=MESH_ID))
        for cp in sent:
            cp.start()
        for cp in waits:
            cp.wait_recv()
        for cp in sent:
            cp.wait_send()

    return pl.pallas_call(
        body, name=name,
        in_specs=[_ANY] * n_out, out_specs=[_ANY] * n_out,
        out_shape=[jax.ShapeDtypeStruct(s.shape, F32) for s in shards],
        input_output_aliases={o: o for o in range(n_out)},
        scratch_shapes=[pltpu.SemaphoreType.DMA((n,)), pltpu.SemaphoreType.DMA((n,))],
    )(*shards)


def _sum_devices(stacked):
    nd, r, c = stacked.shape

    def body(s_ref, o_ref):
        s = s_ref[0]
        for k in range(1, nd):
            s = s + s_ref[k]
        o_ref[...] = s

    return pl.pallas_call(
        body, name="sum_small_grads", grid=(1,),
        in_specs=[pl.BlockSpec((nd, r, c), lambda i: (0, 0, 0))],
        out_specs=pl.BlockSpec((r, c), lambda i: (0, 0)),
        out_shape=jax.ShapeDtypeStruct((r, c), F32),
        compiler_params=_cparams("arbitrary"),
    )(stacked)


def _pack(arrs):
    flat = jnp.concatenate([a.reshape(-1) for a in arrs])
    rows = -(-flat.shape[0] // (8 * LANES)) * 8
    return jnp.pad(flat, (0, rows * LANES - flat.shape[0])).reshape(rows, LANES)


def _unpack(packed, shapes):
    flat = packed.reshape(-1)
    out, at = [], 0
    for s in shapes:
        size = 1
        for dim in s:
            size *= dim
        out.append(flat[at:at + size].reshape(s))
        at += size
    return out


def kernel(x, mix_norm_e, w_in_e, conv_w_e, conv_b_e, ln_g_e, ln_b_e, w_pool_e, pool_scale_e, w_out_e, mix_norm_o, w_in_o, conv_w_o, w_out_o, ffn_norm, w_gate, w_up, w_down, final_norm, loss_target, m_mix_norm_e, m_w_in_e, m_conv_w_e, m_conv_b_e, m_ln_g_e, m_ln_b_e, m_w_pool_e, m_pool_scale_e, m_w_out_e, m_mix_norm_o, m_w_in_o, m_conv_w_o, m_w_out_o, m_ffn_norm, m_w_gate, m_w_up, m_w_down, m_final_norm, v_mix_norm_e, v_w_in_e, v_conv_w_e, v_conv_b_e, v_ln_g_e, v_ln_b_e, v_w_pool_e, v_pool_scale_e, v_w_out_e, v_mix_norm_o, v_w_in_o, v_conv_w_o, v_w_out_o, v_ffn_norm, v_w_gate, v_w_up, v_w_down, v_final_norm):
    bsz, seq_len, d = x.shape
    t = bsz * seq_len
    depth = ffn_norm.shape[0]
    assert depth == 2 and conv_b_e.shape[1] == pool_scale_e.shape[1]
    ts = _pick(seq_len, (256, 128, 64, 32))
    me_k = 2 * lax.axis_index("x") + lax.axis_index("y")
    core = lax.axis_index("c").astype(jnp.int32).reshape(1)

    mat_src = [("col", w_in_e, 0), ("row", w_out_e, 0), ("col", w_gate, 0), ("col", w_up, 0), ("row", w_down, 0),
               ("col", w_in_o, 0), ("row", w_out_o, 0), ("col", w_gate, 1), ("col", w_up, 1), ("row", w_down, 1)]
    mats = [_Mat(kind, w.shape[1:]) for kind, w, _ in mat_src]
    n_pool = w_pool_e.shape[1]
    pool_mats = tuple(range(len(mats), len(mats) + n_pool))
    mats = mats + [_Mat("row", w_pool_e.shape[2:])] * n_pool
    chip = me_k.astype(jnp.int32).reshape(1)
    small_shards = [conv_w_e[0], w_pool_e[0], mix_norm_o, conv_w_o[0]]
    packed_small = _pack(small_shards)

    chain = [()]

    def seq(fn, *args, **kw):
        out = fn(*args, after=chain[0], **kw)
        chain[0] = (out[0] if isinstance(out, (list, tuple)) else out,)
        return out

    def mm(*args, **kw):
        return seq(_mm, *args, **kw)

    gather_groups = [(0,), (1,), (2, 3), (4,), (5, 6), (7, 8), (9,)]
    fulls, gather_sems, small_all = [None] * len(mats), [], None
    for g, ms in enumerate(gather_groups):
        own16 = [seq(_cast_into_full, "cast_w%d" % m, mats[m], mat_src[m][1], mat_src[m][2], chip) for m in ms]
        sent, landing, sems, token = _gather_start("gather_start%d" % g, [mats[m] for m in ms], own16,
                                                   _place_small(packed_small, chip) if g == 0 else None)
        chain[0] = (token,)
        for m, f in zip(ms, sent):
            fulls[m] = f
        gather_sems.append(sems)
        if g == 0:
            small_all = landing

    passed = {}

    def gather_pass(g):
        ms = gather_groups[g]
        bufs, pass_sems, token = _gather_pass("gather_pass%d" % g, [mats[m] for m in ms], [fulls[m] for m in ms],
                                              small_all if g == 0 else None, gather_sems[g], chain[0][0])
        chain[0] = (token,)
        passed[g] = (bufs, pass_sems)

    def gather_done(g):
        ms = gather_groups[g]
        bufs, pass_sems = passed[g]
        done = _gather_done("gather_done%d" % g, [mats[m] for m in ms], bufs[:len(ms)], pass_sems, chain[0][0])
        chain[0] = (done[0],)
        return done + bufs[len(ms):]

    h0 = x.reshape(t, d)
    target = loss_target.reshape(t, d)
    gather_pass(0)
    n1 = seq(_rms_fwd, "mix0_norm", h0, mix_norm_e)
    W_in_e, small_all = gather_done(0)
    per_chip = [_unpack(small_all[k], [s.shape for s in small_shards]) for k in range(N_CHIPS)]
    conv_w_e_f = jnp.concatenate([p[0] for p in per_chip], axis=1)
    w_pool_f = jnp.concatenate([p[1] for p in per_chip], axis=1)
    mix_norm_o_f = jnp.concatenate([p[2] for p in per_chip], axis=1)
    conv_w_o_f = jnp.concatenate([p[3] for p in per_chip], axis=1)
    W_gate, W_up, W_down = [None, None], [None, None], [None, None]

    (u_e,) = mm("mix0_in", [(n1, W_in_e)], "nn", [F32], _ep_store, tm=1024, tn=1024, tk=2048)
    gather_pass(1)
    a2, cat = seq(_mixer_e_fwd, u_e, conv_w_e_f, conv_b_e, ln_g_e, ln_b_e, w_pool_f, pool_scale_e, seq_len, ts)
    (W_out_e,) = gather_done(1)
    (h1,) = mm("mix0_out", [(cat, W_out_e)], "nn", [F32], _ep_residual, extras=(h0,), tm=1024, tn=1024, tk=2048)
    gather_pass(2)
    n2 = seq(_rms_fwd, "ffn0_norm", h1, ffn_norm[0:1])
    W_gate[0], W_up[0] = gather_done(2)
    gt0, up0, act0 = mm("ffn0_gate_up", [(n2, W_gate[0]), (n2, W_up[0])], "nn", [BF16] * 3, _ep_swiglu,
                        acc_of=(0, 1), tm=1024, tn=512, tk=2048)
    gather_pass(3)
    (W_down[0],) = gather_done(3)
    gather_pass(4)
    (h2,) = mm("ffn0_down", [(act0, W_down[0])], "nn", [F32], _ep_residual, extras=(h1,), tm=512, tn=1024, tk=2816)
    n3 = seq(_rms_fwd, "mix1_norm", h2, mix_norm_o_f)
    W_in_o, W_out_o = gather_done(4)
    (u_o,) = mm("mix1_in", [(n3, W_in_o)], "nn", [F32], _ep_store, tm=1024, tn=1024, tk=2048)
    gather_pass(5)
    y_o = seq(_mixer_o_fwd, u_o, conv_w_o_f, seq_len, ts)
    (h3,) = mm("mix1_out", [(y_o, W_out_o)], "nn", [F32], _ep_residual, extras=(h2,), tm=1024, tn=1024, tk=2048)
    gather_pass(6)
    n4 = seq(_rms_fwd, "ffn1_norm", h3, ffn_norm[1:2])
    W_gate[1], W_up[1] = gather_done(5)
    gt1, up1, act1 = mm("ffn1_gate_up", [(n4, W_gate[1]), (n4, W_up[1])], "nn", [BF16] * 3, _ep_swiglu,
                        acc_of=(0, 1), tm=1024, tn=512, tk=2048)
    (W_down[1],) = gather_done(6)
    (h4,) = mm("ffn1_down", [(act1, W_down[1])], "nn", [F32], _ep_residual, extras=(h3,), tm=512, tn=1024, tk=2816)
    loss_part, dh4, dh4b, d_final_norm = seq(_loss_head, h4, final_norm.reshape(1, d), target)
    loss = lax.psum(loss_part[0, 0], AXES)

    in_flight = {}
    partials, scattered = [None] * len(mats), [None] * len(mats)

    def reduce_begin(tag, ms, grads):
        gm = [mats[m] for m in ms]
        grads, landed, sems, token = _exchange_start("exchange_start_" + tag, gm, grads)
        chain[0] = (token,)
        in_flight[tag] = (ms, gm, grads, landed, sems)

    def reduce_advance(tag):
        ms, gm, grads, landed, sems = in_flight[tag]
        grads, landed = _exchange_wait("exchange_wait_" + tag, gm, grads, landed, sems, chain[0][0])
        parts = [_add_halves("add_halves%d" % m, mats[m], g, l, core) for m, g, l in zip(ms, grads, landed)]
        parts, lands, sems, token = _scatter_start("scatter_start_" + tag, gm, parts)
        chain[0] = (token,)
        in_flight[tag] = (ms, gm, parts, lands, sems)

    def reduce_finish(tag):
        ms, gm, parts, lands, sems = in_flight[tag]
        parts, lands = _scatter_wait("scatter_wait_" + tag, gm, parts, lands, sems, chain[0][0])
        chain[0] = (lands[0],)
        for m, p, l in zip(ms, parts, lands):
            partials[m], scattered[m] = p, l

    def ffn_bwd(l, dhb, n, gt, up, act):
        dgt, dup = mm("ffn%d_dact" % l, [(dhb, W_down[l])], "nt", [BF16, BF16], _ep_swiglu_bwd, extras=(gt, up),
                      tm=1024, tn=512, tk=2048)
        (dW_down,) = mm("ffn%d_dw_down" % l, [(act, dhb)], "tn", [BF16], _ep_store, tm=512, tn=1024, tk=2048)
        (dn,) = mm("ffn%d_dn" % l, [(dgt, W_gate[l]), (dup, W_up[l])], "nt", [F32], _ep_store,
                   tm=1024, tn=1024, tk=1408)
        (dW_gate,) = mm("ffn%d_dw_gate" % l, [(n, dgt)], "tn", [BF16], _ep_store, tm=1024, tn=512, tk=2048)
        (dW_up,) = mm("ffn%d_dw_up" % l, [(n, dup)], "tn", [BF16], _ep_store, tm=1024, tn=512, tk=2048)
        return dn, dW_gate, dW_up, dW_down

    dn4, dW_gate1, dW_up1, dW_down1 = ffn_bwd(1, dh4b, n4, gt1, up1, act1)
    reduce_begin("ffn1", (7, 8, 9), [dW_gate1, dW_up1, dW_down1])
    dh3, dh3b, d_ffn_norm1 = seq(_rms_bwd, "ffn1_norm_bwd", dn4, h3, ffn_norm[1:2], dh4)
    (dy_o,) = mm("mix1_dy", [(dh3b, W_out_o)], "nt", [F32], _ep_store, tm=1024, tn=1024, tk=2048)
    reduce_advance("ffn1")
    (dW_out_o,) = mm("mix1_dw_out", [(y_o, dh3b)], "tn", [BF16], _ep_store, tm=1024, tn=1024, tk=2048)
    du_o, d_conv_w_o = seq(_mixer_o_bwd, dy_o, u_o, conv_w_o_f, seq_len, ts)
    (dW_in_o,) = mm("mix1_dw_in", [(n3, du_o)], "tn", [BF16], _ep_store, tm=1024, tn=1024, tk=2048)
    reduce_begin("mix1", (5, 6), [dW_in_o, dW_out_o])
    (dn3,) = mm("mix1_dn", [(du_o, W_in_o)], "nt", [F32], _ep_store, tm=1024, tn=1024, tk=2048)
    reduce_advance("mix1")
    dh2, dh2b, d_mix_norm_o = seq(_rms_bwd, "mix1_norm_bwd", dn3, h2, mix_norm_o_f, dh3)

    dn2, dW_gate0, dW_up0, dW_down0 = ffn_bwd(0, dh2b, n2, gt0, up0, act0)
    reduce_begin("ffn0", (2, 3, 4), [dW_gate0, dW_up0, dW_down0])
    dh1, dh1b, d_ffn_norm0 = seq(_rms_bwd, "ffn0_norm_bwd", dn2, h1, ffn_norm[0:1], dh2)
    (dcat,) = mm("mix0_dcat", [(dh1b, W_out_e)], "nt", [F32], _ep_store, tm=1024, tn=1024, tk=2048)
    reduce_advance("ffn0")
    (dW_out_e,) = mm("mix0_dw_out", [(cat, dh1b)], "tn", [BF16], _ep_store, tm=1024, tn=1024, tk=2048)
    da2, d_ln_g, d_ln_b, d_conv_b = seq(_mixer_e_bwd_norm, dcat, a2, ln_g_e, ln_b_e, ts)
    du_e, d_conv_w_e, d_w_pool, d_pool_scale = seq(_mixer_e_bwd_mix, da2, dcat, u_e, conv_w_e_f, w_pool_f, pool_scale_e,
                                                   seq_len, ts)
    (dW_in_e,) = mm("mix0_dw_in", [(n1, du_e)], "tn", [BF16], _ep_store, tm=1024, tn=1024, tk=2048)
    reduce_begin("mix0", (0, 1) + pool_mats, [dW_in_e, dW_out_e] + [d_w_pool[g].astype(BF16) for g in range(n_pool)])
    (dn1,) = mm("mix0_dn", [(du_e, W_in_e)], "nt", [F32], _ep_store, tm=1024, tn=1024, tk=2048)
    dx, _, d_mix_norm_e = seq(_rms_bwd, "mix0_norm_bwd", dn1, h0, mix_norm_e, dh1)
    reduce_advance("mix0")

    d_ffn_norm = jnp.concatenate([d_ffn_norm0, d_ffn_norm1], axis=0)
    small_partials = [d_mix_norm_e, d_conv_w_e, d_conv_b, d_ln_g, d_ln_b, d_pool_scale, d_mix_norm_o, d_conv_w_o,
                      d_ffn_norm, d_final_norm]
    packed_grads, small_stack, small_sems, token = _small_start(_pack(small_partials))
    chain[0] = (token,)
    xi, yi, ci = lax.axis_index("x"), lax.axis_index("y"), lax.axis_index("c")
    slots = jnp.stack([me_k, 2 * (1 - xi) + yi, 2 * xi + (1 - yi), 2 * (1 - xi) + (1 - yi), ci]).astype(jnp.int32)

    def reduced_shards(name, groups):
        halves = []
        for members in groups:
            if len(members) == 1:
                m = members[0]
                halves.append(_sum_chips("sum_chips%d" % m, mats[m], partials[m], scattered[m], slots))
            else:
                stack = None
                for l, m in enumerate(members):
                    stack = _sum_chips("sum_chips%d" % m, mats[m], partials[m], scattered[m], slots, layer=l,
                                       stack=stack, n_layers=len(members))
                halves.append(stack)
        return _share_pieces(name, mats, halves, groups)

    for tag in ("ffn1", "mix1", "ffn0"):
        reduce_finish(tag)
    g_w_in_o, g_w_out_o, g_w_gate, g_w_up, g_w_down = reduced_shards(
        "share_pieces_late", [(5,), (6,), (2, 7), (3, 8), (4, 9)])
    grad = {"w_in_o": g_w_in_o[None], "w_out_o": g_w_out_o[None], "w_gate": g_w_gate, "w_up": g_w_up, "w_down": g_w_down}
    weights = dict(mix_norm_e=mix_norm_e, w_in_e=w_in_e, conv_w_e=conv_w_e, conv_b_e=conv_b_e, ln_g_e=ln_g_e, ln_b_e=ln_b_e,
                   w_pool_e=w_pool_e, pool_scale_e=pool_scale_e, w_out_e=w_out_e, mix_norm_o=mix_norm_o, w_in_o=w_in_o,
                   conv_w_o=conv_w_o, w_out_o=w_out_o, ffn_norm=ffn_norm, w_gate=w_gate, w_up=w_up, w_down=w_down,
                   final_norm=final_norm)
    mom1 = dict(mix_norm_e=m_mix_norm_e, w_in_e=m_w_in_e, conv_w_e=m_conv_w_e, conv_b_e=m_conv_b_e, ln_g_e=m_ln_g_e,
                ln_b_e=m_ln_b_e, w_pool_e=m_w_pool_e, pool_scale_e=m_pool_scale_e, w_out_e=m_w_out_e, mix_norm_o=m_mix_norm_o,
                w_in_o=m_w_in_o, conv_w_o=m_conv_w_o, w_out_o=m_w_out_o, ffn_norm=m_ffn_norm, w_gate=m_w_gate, w_up=m_w_up,
                w_down=m_w_down, final_norm=m_final_norm)
    mom2 = dict(mix_norm_e=v_mix_norm_e, w_in_e=v_w_in_e, conv_w_e=v_conv_w_e, conv_b_e=v_conv_b_e, ln_g_e=v_ln_g_e,
                ln_b_e=v_ln_b_e, w_pool_e=v_w_pool_e, pool_scale_e=v_pool_scale_e, w_out_e=v_w_out_e, mix_norm_o=v_mix_norm_o,
                w_in_o=v_w_in_o, conv_w_o=v_conv_w_o, w_out_o=v_w_out_o, ffn_norm=v_ffn_norm, w_gate=v_w_gate, w_up=v_w_up,
                w_down=v_w_down, final_norm=v_final_norm)
    names = list(weights)

    big = ("w_in_o", "w_out_o", "w_gate", "w_up", "w_down", "w_in_e", "w_out_e")
    delta, new_m, new_v = {}, {}, {}

    def update(nm):
        shape = weights[nm].shape
        rows = 1
        for dim in shape[:-1]:
            rows *= dim
        as2d = lambda a: a.reshape(rows, shape[-1])
        dl, mn, vn = seq(_adamw, "adamw_" + nm, as2d(weights[nm]), as2d(grad[nm]), as2d(mom1[nm]), as2d(mom2[nm]))
        delta[nm], new_m[nm], new_v[nm] = dl.reshape(shape), mn.reshape(shape), vn.reshape(shape)

    for nm in big[:5]:
        update(nm)
    reduce_finish("mix0")
    g_w_in_e, g_w_out_e, g_w_pool = reduced_shards("share_pieces_first", [(0,), (1,), pool_mats])
    grad["w_in_e"], grad["w_out_e"], grad["w_pool_e"] = g_w_in_e[None], g_w_out_e[None], g_w_pool[None]
    for nm in big[5:]:
        update(nm)

    packed_grads, small_stack = _small_wait(packed_grads, small_stack, small_sems, chain[0][0])
    me_dev = 4 * xi + 2 * yi + ci
    small_stack = jnp.where(lax.broadcasted_iota(jnp.int32, (N_DEV, 1, 1), 0) == me_dev, packed_grads[None], small_stack)
    small_sum = _unpack(_sum_devices(small_stack), [s.shape for s in small_partials])
    (g_mix_norm_e, g_conv_w_e_f, g_conv_b, g_ln_g, g_ln_b, g_pool_scale, g_mix_norm_o_f, g_conv_w_o_f,
     g_ffn_norm, g_final_norm) = small_sum

    def my_shard(full, axis):
        size = full.shape[axis] // N_CHIPS
        return lax.dynamic_slice_in_dim(full, me_k * size, size, axis)

    grad.update({
        "mix_norm_e": g_mix_norm_e, "conv_w_e": my_shard(g_conv_w_e_f, 1)[None], "conv_b_e": g_conv_b,
        "ln_g_e": g_ln_g, "ln_b_e": g_ln_b, "pool_scale_e": g_pool_scale,
        "mix_norm_o": my_shard(g_mix_norm_o_f, 1), "conv_w_o": my_shard(g_conv_w_o_f, 1)[None],
        "ffn_norm": g_ffn_norm, "final_norm": g_final_norm.reshape(final_norm.shape),
    })
    small = [nm for nm in names if nm not in big]
    shapes = [weights[nm].shape for nm in small]
    dl, mn, vn = _adamw("adamw_small", _pack([weights[nm] for nm in small]), _pack([grad[nm] for nm in small]),
                        _pack([mom1[nm] for nm in small]), _pack([mom2[nm] for nm in small]))
    for nm, a, b, c_ in zip(small, _unpack(dl, shapes), _unpack(mn, shapes), _unpack(vn, shapes)):
        delta[nm], new_m[nm], new_v[nm] = a, b, c_

    grad_x = dx.reshape(bsz, seq_len, d)
    return (loss, grad_x, *[grad[nm] for nm in names], *[delta[nm] for nm in names],
            *[new_m[nm] for nm in names], *[new_v[nm] for nm in names])
```

```python
import jax
import jax.numpy as jnp
from jax import lax
from jax.experimental import pallas as pl
from jax.experimental.pallas import tpu as pltpu

F32 = jnp.float32
BF16 = jnp.bfloat16
MESH_ID = pl.DeviceIdType.MESH
AXES = ("x", "y", "c")
N_CHIPS = 4
N_DEV = 8

EPS = 1e-6
POOL_WINDOWS = (2, 4, 8, 16)
ADAM_LR, ADAM_B1, ADAM_B2, ADAM_EPS, ADAM_WD, ADAM_STEP = 0.001, 0.9, 0.999, 1e-08, 0.01, 10

LANES = 128
CONV_HALO = 32
POOL_HALO = 16
SHORT_HALO = 8
V7X_VMEM_LIMIT = 56 * 1024 * 1024


def _cparams(*sem):
    return pltpu.CompilerParams(dimension_semantics=sem if sem else None, vmem_limit_bytes=V7X_VMEM_LIMIT)


def _pick(dim, prefs):
    for p in prefs:
        if p <= dim and dim % p == 0:
            return p
    return dim


def _sigmoid(x):
    return jax.nn.sigmoid(x)


_ANY = pl.BlockSpec(memory_space=pl.ANY)


def _behind(after, body, n_in):
    if not after:
        return body
    skip = len(after)

    def body_behind(*refs):
        return body(*refs[:n_in], *refs[n_in + skip:])

    return body_behind


_DOT_DIMS = {
    "nn": (((1,), (0,)), ((), ())),
    "nt": (((1,), (1,)), ((), ())),
    "tn": (((0,), (0,)), ((), ())),
}


def _mm(name, pairs, mode, out_dtypes, epilogue, extras=(), acc_of=None, tm=512, tn=512, tk=2048, row_chunk=0, after=()):
    a0, b0 = pairs[0]
    if mode == "nn":
        (m, k), n = a0.shape, b0.shape[1]
    elif mode == "nt":
        (m, k), n = a0.shape, b0.shape[0]
    else:
        (k, m), n = a0.shape, b0.shape[1]
    tm = _pick(m, (tm, 512, 256, 128, 64, 32, 16, 8))
    tn = _pick(n, (tn, 512, 256, 128))
    tk = _pick(k, (tk, 2048, 1024, 512, 256, 128))
    nk = k // tk
    n_pairs = len(pairs)
    acc_of = tuple(acc_of) if acc_of is not None else (0,) * n_pairs
    n_acc = max(acc_of) + 1
    n_ex, n_out = len(extras), len(out_dtypes)
    dims = _DOT_DIMS[mode]

    def body(*refs):
        a_refs = refs[:n_pairs]
        b_refs = refs[n_pairs:2 * n_pairs]
        e_refs = refs[2 * n_pairs:2 * n_pairs + n_ex]
        first_out = 2 * n_pairs + n_ex + len(after)
        o_refs = refs[first_out:first_out + n_out]
        acc_refs = refs[first_out + n_out:]

        def partial_sums(rows=None):
            sums = [None] * n_acc
            for p in range(n_pairs):
                a = a_refs[p][...] if rows is None else (a_refs[p][:, rows] if mode == "tn" else a_refs[p][rows, :])
                d = lax.dot_general(a, b_refs[p][...], dims, preferred_element_type=F32)
                sums[acc_of[p]] = d if sums[acc_of[p]] is None else sums[acc_of[p]] + d
            return sums

        if nk == 1 and row_chunk:
            for r0 in range(0, tm, row_chunk):
                rows = pl.ds(r0, row_chunk)
                epilogue(partial_sums(rows), [e.at[rows, :] for e in e_refs], [o.at[rows, :] for o in o_refs])
            return
        if nk == 1:
            epilogue(partial_sums(), e_refs, o_refs)
            return
        kk = pl.program_id(2)

        @pl.when(kk == 0)
        def _():
            for acc in acc_refs:
                acc[...] = jnp.zeros_like(acc)

        for acc, s in zip(acc_refs, partial_sums()):
            acc[...] += s

        @pl.when(kk == nk - 1)
        def _():
            epilogue([acc[...] for acc in acc_refs], e_refs, o_refs)

    if mode == "nn":
        a_spec = pl.BlockSpec((tm, tk), lambda i, j, kk: (i, kk))
        b_spec = pl.BlockSpec((tk, tn), lambda i, j, kk: (kk, j))
    elif mode == "nt":
        a_spec = pl.BlockSpec((tm, tk), lambda i, j, kk: (i, kk))
        b_spec = pl.BlockSpec((tn, tk), lambda i, j, kk: (j, kk))
    else:
        a_spec = pl.BlockSpec((tk, tm), lambda i, j, kk: (kk, i))
        b_spec = pl.BlockSpec((tk, tn), lambda i, j, kk: (kk, j))
    o_spec = pl.BlockSpec((tm, tn), lambda i, j, kk: (i, j))
    outs = pl.pallas_call(
        body,
        name=name,
        grid=(m // tm, n // tn, nk),
        in_specs=[a_spec] * n_pairs + [b_spec] * n_pairs + [o_spec] * n_ex
        + [pl.BlockSpec(memory_space=pl.ANY)] * len(after),
        out_specs=[o_spec] * n_out,
        out_shape=[jax.ShapeDtypeStruct((m, n), dt) for dt in out_dtypes],
        scratch_shapes=[pltpu.VMEM((tm, tn), F32) for _ in range(n_acc)] if nk > 1 else [],
        compiler_params=_cparams("parallel", "parallel", "arbitrary"),
    )(*[p[0] for p in pairs], *[p[1] for p in pairs], *extras, *after)
    return outs


def _ep_store(accs, ex, outs):
    outs[0][...] = accs[0].astype(outs[0].dtype)


def _ep_residual(accs, ex, outs):
    outs[0][...] = ex[0][...] + accs[0]


def _ep_swiglu(accs, ex, outs):
    g, u = accs
    outs[0][...] = g.astype(BF16)
    outs[1][...] = u.astype(BF16)
    outs[2][...] = (g * _sigmoid(g) * u).astype(BF16)


def _ep_swiglu_bwd(accs, ex, outs):
    d = accs[0]
    g = ex[0][...].astype(F32)
    u = ex[1][...].astype(F32)
    s = _sigmoid(g)
    outs[0][...] = (d * u * (s * (1.0 + g * (1.0 - s)))).astype(BF16)
    outs[1][...] = (d * (g * s)).astype(BF16)


def _rms_fwd(name, h, g, after=()):
    t, d = h.shape
    tr = _pick(t, (256, 128, 64, 32, 16, 8))

    def body(h_ref, g_ref, *rest):
        o_ref = rest[-1]
        x = h_ref[...]
        r = lax.rsqrt(jnp.mean(x * x, axis=-1, keepdims=True) + EPS)
        o_ref[...] = (x * r * g_ref[...]).astype(BF16)

    return pl.pallas_call(
        body, name=name, grid=(t // tr,),
        in_specs=[pl.BlockSpec((tr, d), lambda i: (i, 0)), pl.BlockSpec((1, d), lambda i: (0, 0))]
        + [pl.BlockSpec(memory_space=pl.ANY)] * len(after),
        out_specs=pl.BlockSpec((tr, d), lambda i: (i, 0)),
        out_shape=jax.ShapeDtypeStruct((t, d), BF16),
        compiler_params=_cparams("parallel"),
    )(h, g, *after)


def _rms_bwd(name, dn, h, g, dres, after=()):
    t, d = h.shape
    tr = _pick(t, (256, 128, 64, 32, 16, 8))

    def body(dn_ref, h_ref, g_ref, dres_ref, dh_ref, dhb_ref, dg_ref):
        x = h_ref[...]
        r = lax.rsqrt(jnp.mean(x * x, axis=-1, keepdims=True) + EPS)
        xhat = x * r
        dnv = dn_ref[...]

        @pl.when(pl.program_id(0) == 0)
        def _():
            dg_ref[...] = jnp.zeros_like(dg_ref)

        dg_ref[...] += jnp.sum(dnv * xhat, axis=0, keepdims=True)
        dxh = dnv * g_ref[...]
        dh = dres_ref[...] + r * (dxh - xhat * jnp.mean(dxh * xhat, axis=-1, keepdims=True))
        dh_ref[...] = dh
        dhb_ref[...] = dh.astype(BF16)

    row = pl.BlockSpec((tr, d), lambda i: (i, 0))
    vec = pl.BlockSpec((1, d), lambda i: (0, 0))
    return pl.pallas_call(
        _behind(after, body, 4), name=name, grid=(t // tr,),
        in_specs=[row, row, vec, row] + [_ANY] * len(after),
        out_specs=[row, row, vec],
        out_shape=[jax.ShapeDtypeStruct((t, d), F32), jax.ShapeDtypeStruct((t, d), BF16),
                   jax.ShapeDtypeStruct((1, d), F32)],
        compiler_params=_cparams("arbitrary"),
    )(dn, h, g, dres, *after)


def _loss_head(h, g, target, after=()):
    t, d = h.shape
    tr = _pick(t, (256, 128, 64, 32, 16, 8))

    def body(h_ref, g_ref, t_ref, loss_ref, dh_ref, dhb_ref, dg_ref):
        x = h_ref[...]
        gv = g_ref[...]
        r = lax.rsqrt(jnp.mean(x * x, axis=-1, keepdims=True) + EPS)
        xhat = x * r
        err = xhat * gv - t_ref[...]

        @pl.when(pl.program_id(0) == 0)
        def _():
            dg_ref[...] = jnp.zeros_like(dg_ref)
            loss_ref[...] = jnp.zeros_like(loss_ref)

        loss_ref[...] += jnp.full(loss_ref.shape, 0.5 / d, F32) * jnp.sum(err * err)
        dy = err * (1.0 / d)
        dg_ref[...] += jnp.sum(dy * xhat, axis=0, keepdims=True)
        dxh = dy * gv
        dh = r * (dxh - xhat * jnp.mean(dxh * xhat, axis=-1, keepdims=True))
        dh_ref[...] = dh
        dhb_ref[...] = dh.astype(BF16)

    row = pl.BlockSpec((tr, d), lambda i: (i, 0))
    vec = pl.BlockSpec((1, d), lambda i: (0, 0))
    return pl.pallas_call(
        _behind(after, body, 3), name="loss_head", grid=(t // tr,),
        in_specs=[row, vec, row] + [_ANY] * len(after),
        out_specs=[pl.BlockSpec((1, LANES), lambda i: (0, 0)), row, row, vec],
        out_shape=[jax.ShapeDtypeStruct((1, LANES), F32), jax.ShapeDtypeStruct((t, d), F32),
                   jax.ShapeDtypeStruct((t, d), BF16), jax.ShapeDtypeStruct((1, d), F32)],
        compiler_params=_cparams("arbitrary"),
    )(h, g, target, *after)


def _cur(ts, width, col):
    return pl.BlockSpec((ts, width), lambda i: (i, col))


def _prev_halo(ts, halo, width, col):
    per = ts // halo
    return pl.BlockSpec((halo, width), lambda i: (jnp.maximum(i * per - 1, 0), col))


def _next_halo(ts, halo, width, col, n_rows):
    per = ts // halo
    last = n_rows // halo - 1
    return pl.BlockSpec((halo, width), lambda i: (jnp.minimum((i + 1) * per, last), col))


def _full(shape):
    nd = len(shape)
    return pl.BlockSpec(shape, lambda i: (0,) * nd)


def _shift_down(x, n):
    return x if n == 0 else pltpu.roll(x, n, 0)


def _shift_up(x, n):
    return x if n == 0 else pltpu.roll(x, x.shape[0] - n, 0)


def _pool_counts(i, ns, ts, w):
    pos = (i % ns) * ts + lax.broadcasted_iota(jnp.int32, (ts, 1), 0)
    return jnp.minimum(pos + 1, w).astype(F32)


def _pooled(cur, prev_tail, w, cnt):
    s = jnp.concatenate([prev_tail, cur], axis=0)
    d = 1
    while d < w:
        s = s + _shift_down(s, d)
        d *= 2
    return s[POOL_HALO:, :] / cnt - cur


def _mixer_e_fwd(u, conv_w, conv_b, ln_g, ln_b, w_pool, scale, seq, ts, after=()):
    t = u.shape[0]
    dc = conv_b.shape[1]
    ng, pg = w_pool.shape[0], w_pool.shape[1]
    taps = conv_w.shape[0]
    ns = seq // ts

    def body(val_ref, gate_ref, b_ref, pval_ref, pgate_ref, pb_ref, cw_ref, cb_ref, g_ref, be_ref, wp_ref, sc_ref,
             a2_ref, cat_ref):
        i = pl.program_id(0)
        keep_prev = jnp.where(i % ns == 0, 0.0, 1.0)
        a1 = val_ref[...] * _sigmoid(gate_ref[...])
        pa1 = pval_ref[...] * _sigmoid(pgate_ref[...]) * keep_prev
        ext = jnp.concatenate([pa1, a1], axis=0)
        acc = jnp.zeros_like(ext)
        for k in range(taps):
            acc = acc + cw_ref[k:k + 1, :] * _shift_down(ext, taps - 1 - k)
        a2 = acc[CONV_HALO:, :] + cb_ref[...]
        a2_ref[...] = a2
        mu = jnp.mean(a2, axis=-1, keepdims=True)
        xc = a2 - mu
        rstd = lax.rsqrt(jnp.mean(xc * xc, axis=-1, keepdims=True) + EPS)
        a3 = xc * rstd * g_ref[...] + be_ref[...]
        cat_ref[:, 0:dc] = (a3 * _sigmoid(a3)).astype(BF16)
        for g in range(ng):
            lo, hi = g * pg, (g + 1) * pg
            w = POOL_WINDOWS[g]
            p = _pooled(b_ref[:, lo:hi], pb_ref[:, lo:hi] * keep_prev, w, _pool_counts(i, ns, ts, w))
            q = jnp.dot(p.astype(BF16), wp_ref[g].astype(BF16), preferred_element_type=F32)
            cat_ref[:, dc + lo:dc + hi] = (q * sc_ref[:, lo:hi]).astype(BF16)

    return pl.pallas_call(
        _behind(after, body, 12), name="mixer_e_fwd", grid=(t // ts,),
        in_specs=[_cur(ts, dc, 0), _cur(ts, dc, 1), _cur(ts, dc, 2),
                  _prev_halo(ts, CONV_HALO, dc, 0), _prev_halo(ts, CONV_HALO, dc, 1), _prev_halo(ts, POOL_HALO, dc, 2),
                  _full(conv_w.shape), _full(conv_b.shape), _full(ln_g.shape), _full(ln_b.shape),
                  _full(w_pool.shape), _full(scale.shape)] + [_ANY] * len(after),
        out_specs=[_cur(ts, dc, 0), _cur(ts, 2 * dc, 0)],
        out_shape=[jax.ShapeDtypeStruct((t, dc), F32), jax.ShapeDtypeStruct((t, 2 * dc), BF16)],
        compiler_params=_cparams("parallel"),
    )(u, u, u, u, u, u, conv_w, conv_b, ln_g, ln_b, w_pool, scale, *after)


def _mixer_e_bwd_norm(dcat, a2, ln_g, ln_b, ts, after=()):
    t, dc = a2.shape

    def body(d_ref, a2_ref, g_ref, be_ref, da2_ref, dg_ref, db_ref, dcb_ref):
        x = a2_ref[...]
        gv = g_ref[...]
        mu = jnp.mean(x, axis=-1, keepdims=True)
        xc = x - mu
        rstd = lax.rsqrt(jnp.mean(xc * xc, axis=-1, keepdims=True) + EPS)
        xhat = xc * rstd
        a3 = xhat * gv + be_ref[...]
        sg = _sigmoid(a3)
        da3 = d_ref[...] * (sg * (1.0 + a3 * (1.0 - sg)))
        dxh = da3 * gv
        da2 = rstd * (dxh - jnp.mean(dxh, axis=-1, keepdims=True)
                      - xhat * jnp.mean(dxh * xhat, axis=-1, keepdims=True))
        da2_ref[...] = da2

        @pl.when(pl.program_id(0) == 0)
        def _():
            dg_ref[...] = jnp.zeros_like(dg_ref)
            db_ref[...] = jnp.zeros_like(db_ref)
            dcb_ref[...] = jnp.zeros_like(dcb_ref)

        dg_ref[...] += jnp.sum(da3 * xhat, axis=0, keepdims=True)
        db_ref[...] += jnp.sum(da3, axis=0, keepdims=True)
        dcb_ref[...] += jnp.sum(da2, axis=0, keepdims=True)

    vec = _full((1, dc))
    return pl.pallas_call(
        _behind(after, body, 4), name="mixer_e_bwd_norm", grid=(t // ts,),
        in_specs=[_cur(ts, dc, 0), _cur(ts, dc, 0), vec, vec] + [_ANY] * len(after),
        out_specs=[_cur(ts, dc, 0), vec, vec, vec],
        out_shape=[jax.ShapeDtypeStruct((t, dc), F32)] + [jax.ShapeDtypeStruct((1, dc), F32)] * 3,
        compiler_params=_cparams("arbitrary"),
    )(dcat, a2, ln_g, ln_b, *after)


def _mixer_e_bwd_mix(da2, dcat, u, conv_w, w_pool, scale, seq, ts, after=()):
    t, dc = da2.shape
    ng, pg = w_pool.shape[0], w_pool.shape[1]
    taps = conv_w.shape[0]
    ns = seq // ts

    def body(da2_ref, nda2_ref, dp_ref, ndp_ref, val_ref, gate_ref, b_ref, pval_ref, pgate_ref, pb_ref,
             cw_ref, wp_ref, sc_ref, du_ref, dcw_ref, dwp_ref, dsc_ref):
        i = pl.program_id(0)
        keep_prev = jnp.where(i % ns == 0, 0.0, 1.0)
        keep_next = jnp.where(i % ns == ns - 1, 0.0, 1.0)

        @pl.when(i == 0)
        def _():
            dcw_ref[...] = jnp.zeros_like(dcw_ref)
            dwp_ref[...] = jnp.zeros_like(dwp_ref)
            dsc_ref[...] = jnp.zeros_like(dsc_ref)

        val = val_ref[...]
        sg = _sigmoid(gate_ref[...])
        a1 = val * sg
        pa1 = pval_ref[...] * _sigmoid(pgate_ref[...]) * keep_prev
        ext_a = jnp.concatenate([pa1, a1], axis=0)
        da2v = da2_ref[...]
        ext_d = jnp.concatenate([da2v, nda2_ref[...] * keep_next], axis=0)
        da1 = jnp.zeros_like(ext_d)
        for k in range(taps):
            sh = taps - 1 - k
            dcw_ref[k:k + 1, :] += jnp.sum(da2v * _shift_down(ext_a, sh)[CONV_HALO:, :], axis=0, keepdims=True)
            da1 = da1 + cw_ref[k:k + 1, :] * _shift_up(ext_d, sh)
        da1 = da1[:ts, :]
        du_ref[:, 0:dc] = (da1 * sg).astype(BF16)
        du_ref[:, dc:2 * dc] = (da1 * a1 * (1.0 - sg)).astype(BF16)

        for g in range(ng):
            lo, hi = g * pg, (g + 1) * pg
            w = POOL_WINDOWS[g]
            cnt = _pool_counts(i, ns, ts, w)
            wpb = wp_ref[g].astype(BF16)
            sc = sc_ref[:, lo:hi]
            p = _pooled(b_ref[:, lo:hi], pb_ref[:, lo:hi] * keep_prev, w, cnt)
            pb16 = p.astype(BF16)
            q = jnp.dot(pb16, wpb, preferred_element_type=F32)
            dout = dp_ref[:, lo:hi]
            dsc_ref[:, lo:hi] += jnp.sum(dout * q, axis=0, keepdims=True)
            dq = (dout * sc).astype(BF16)
            dwp_ref[g] += lax.dot_general(pb16, dq, _DOT_DIMS["tn"], preferred_element_type=F32)
            dpool = lax.dot_general(dq, wpb, _DOT_DIMS["nt"], preferred_element_type=F32)
            ndq = (ndp_ref[:, lo:hi] * sc * keep_next).astype(BF16)
            ndpool = lax.dot_general(ndq, wpb, _DOT_DIMS["nt"], preferred_element_type=F32)
            s = jnp.concatenate([dpool / cnt, ndpool * (1.0 / w)], axis=0)
            d = 1
            while d < w:
                s = s + _shift_up(s, d)
                d *= 2
            du_ref[:, 2 * dc + lo:2 * dc + hi] = (s[:ts, :] - dpool).astype(BF16)

    return pl.pallas_call(
        _behind(after, body, 13), name="mixer_e_bwd_mix", grid=(t // ts,),
        in_specs=[_cur(ts, dc, 0), _next_halo(ts, CONV_HALO, dc, 0, t),
                  _cur(ts, dc, 1), _next_halo(ts, POOL_HALO, dc, 1, t),
                  _cur(ts, dc, 0), _cur(ts, dc, 1), _cur(ts, dc, 2),
                  _prev_halo(ts, CONV_HALO, dc, 0), _prev_halo(ts, CONV_HALO, dc, 1), _prev_halo(ts, POOL_HALO, dc, 2),
                  _full(conv_w.shape), _full(w_pool.shape), _full(scale.shape)] + [_ANY] * len(after),
        out_specs=[_cur(ts, 3 * dc, 0), _full(conv_w.shape), _full(w_pool.shape), _full(scale.shape)],
        out_shape=[jax.ShapeDtypeStruct((t, 3 * dc), BF16), jax.ShapeDtypeStruct(conv_w.shape, F32),
                   jax.ShapeDtypeStruct(w_pool.shape, F32), jax.ShapeDtypeStruct(scale.shape, F32)],
        compiler_params=_cparams("arbitrary"),
    )(da2, da2, dcat, dcat, u, u, u, u, u, u, conv_w, w_pool, scale, *after)


def _mixer_o_fwd(u, conv_w, seq, ts, after=()):
    t = u.shape[0]
    d = conv_w.shape[1]
    taps = conv_w.shape[0]
    ns = seq // ts

    def body(gb_ref, gc_ref, v_ref, pgc_ref, pv_ref, cw_ref, y_ref):
        keep_prev = jnp.where(pl.program_id(0) % ns == 0, 0.0, 1.0)
        ext = jnp.concatenate([pgc_ref[...] * pv_ref[...] * keep_prev, gc_ref[...] * v_ref[...]], axis=0)
        cc = jnp.zeros_like(ext)
        for k in range(taps):
            cc = cc + cw_ref[k:k + 1, :] * _shift_down(ext, taps - 1 - k)
        y_ref[...] = (gb_ref[...] * cc[SHORT_HALO:, :]).astype(BF16)

    return pl.pallas_call(
        _behind(after, body, 6), name="mixer_o_fwd", grid=(t // ts,),
        in_specs=[_cur(ts, d, 0), _cur(ts, d, 1), _cur(ts, d, 2),
                  _prev_halo(ts, SHORT_HALO, d, 1), _prev_halo(ts, SHORT_HALO, d, 2), _full(conv_w.shape)]
        + [_ANY] * len(after),
        out_specs=_cur(ts, d, 0),
        out_shape=jax.ShapeDtypeStruct((t, d), BF16),
        compiler_params=_cparams("parallel"),
    )(u, u, u, u, u, conv_w, *after)


def _mixer_o_bwd(dy, u, conv_w, seq, ts, after=()):
    t = u.shape[0]
    d = conv_w.shape[1]
    taps = conv_w.shape[0]
    ns = seq // ts

    def body(dy_ref, ndy_ref, gb_ref, gc_ref, v_ref, pgc_ref, pv_ref, ngb_ref, cw_ref, du_ref, dcw_ref):
        i = pl.program_id(0)
        keep_prev = jnp.where(i % ns == 0, 0.0, 1.0)
        keep_next = jnp.where(i % ns == ns - 1, 0.0, 1.0)

        @pl.when(i == 0)
        def _():
            dcw_ref[...] = jnp.zeros_like(dcw_ref)

        gb, gc, v, dyv = gb_ref[...], gc_ref[...], v_ref[...], dy_ref[...]
        ext = jnp.concatenate([pgc_ref[...] * pv_ref[...] * keep_prev, gc * v], axis=0)
        dcc = dyv * gb
        ext_d = jnp.concatenate([dcc, ndy_ref[...] * ngb_ref[...] * keep_next], axis=0)
        cc = jnp.zeros_like(ext)
        dcv = jnp.zeros_like(ext_d)
        for k in range(taps):
            sh = taps - 1 - k
            shifted = _shift_down(ext, sh)
            cc = cc + cw_ref[k:k + 1, :] * shifted
            dcw_ref[k:k + 1, :] += jnp.sum(dcc * shifted[SHORT_HALO:, :], axis=0, keepdims=True)
            dcv = dcv + cw_ref[k:k + 1, :] * _shift_up(ext_d, sh)
        dcv = dcv[:ts, :]
        du_ref[:, 0:d] = (dyv * cc[SHORT_HALO:, :]).astype(BF16)
        du_ref[:, d:2 * d] = (dcv * v).astype(BF16)
        du_ref[:, 2 * d:3 * d] = (dcv * gc).astype(BF16)

    return pl.pallas_call(
        _behind(after, body, 9), name="mixer_o_bwd", grid=(t // ts,),
        in_specs=[_cur(ts, d, 0), _next_halo(ts, SHORT_HALO, d, 0, t),
                  _cur(ts, d, 0), _cur(ts, d, 1), _cur(ts, d, 2),
                  _prev_halo(ts, SHORT_HALO, d, 1), _prev_halo(ts, SHORT_HALO, d, 2),
                  _next_halo(ts, SHORT_HALO, d, 0, t), _full(conv_w.shape)] + [_ANY] * len(after),
        out_specs=[_cur(ts, 3 * d, 0), _full(conv_w.shape)],
        out_shape=[jax.ShapeDtypeStruct((t, 3 * d), BF16), jax.ShapeDtypeStruct(conv_w.shape, F32)],
        compiler_params=_cparams("arbitrary"),
    )(dy, dy, u, u, u, u, u, u, conv_w, *after)


def _cast_into_full(name, mat, w, layer, chip, after=()):
    tr = _pick(mat.sr, (512, 256, 128, 64, 32, 16))
    per = mat.sr // tr

    def body(chip_ref, w_ref, *rest):
        o_ref = rest[-1]
        o_ref[...] = w_ref[...].astype(BF16)

    if mat.kind == "col":
        o_spec = pl.BlockSpec((tr, mat.sc), lambda i, chip_ref: (i, chip_ref[0]))
    else:
        o_spec = pl.BlockSpec((tr, mat.sc), lambda i, chip_ref: (chip_ref[0] * per + i, 0))
    return pl.pallas_call(
        body, name=name,
        grid_spec=pltpu.PrefetchScalarGridSpec(
            num_scalar_prefetch=1, grid=(per,),
            in_specs=[pl.BlockSpec((None, tr, mat.sc), lambda i, chip_ref: (layer, i, 0))] + [_ANY] * len(after),
            out_specs=o_spec),
        out_shape=jax.ShapeDtypeStruct(mat.full_shape, BF16),
        compiler_params=_cparams("parallel"),
    )(chip, w, *after)


def _adamw(name, w, g, m, v, after=()):
    r, c = w.shape
    tr = _pick(r, (256, 128, 64, 32, 16, 8)) if c > 1024 else _pick(r, (512, 256, 128, 64, 32, 16, 8))
    bc1 = 1.0 - ADAM_B1 ** ADAM_STEP
    bc2 = 1.0 - ADAM_B2 ** ADAM_STEP

    def body(w_ref, g_ref, m_ref, v_ref, d_ref, mo_ref, vo_ref):
        gv = g_ref[...]
        mn = ADAM_B1 * m_ref[...] + (1.0 - ADAM_B1) * gv
        vn = ADAM_B2 * v_ref[...] + (1.0 - ADAM_B2) * (gv * gv)
        mo_ref[...] = mn
        vo_ref[...] = vn
        d_ref[...] = -ADAM_LR * ((mn / bc1) / (jnp.sqrt(vn / bc2) + ADAM_EPS) + ADAM_WD * w_ref[...])

    spec = pl.BlockSpec((tr, c), lambda i: (i, 0))
    return pl.pallas_call(_behind(after, body, 4), name=name, grid=(r // tr,),
                          in_specs=[spec] * 4 + [_ANY] * len(after), out_specs=[spec] * 3,
                          out_shape=[jax.ShapeDtypeStruct((r, c), F32)] * 3,
                          compiler_params=_cparams("parallel"))(w, g, m, v, *after)


def _aligned(offset, multiple):
    return offset if isinstance(offset, int) else pl.multiple_of(offset, multiple)


class _Mat:
    def __init__(self, kind, shard_shape):
        self.kind = kind
        self.sr, self.sc = shard_shape
        self.full_shape = (self.sr, self.sc * N_CHIPS) if kind == "col" else (self.sr * N_CHIPS, self.sc)
        self.pr, self.pc = self.sr // 2, self.sc

    def piece(self, ref, k, h):
        if self.kind == "col":
            return ref.at[pl.ds(_aligned(h * self.pr, 16), self.pr), pl.ds(_aligned(k * self.sc, LANES), self.sc)]
        return ref.at[pl.ds(_aligned(k * self.sr + h * self.pr, 16), self.pr), :]

    def shard(self, ref, k):
        if self.kind == "col":
            return ref.at[:, pl.ds(_aligned(k * self.sc, LANES), self.sc)]
        return ref.at[pl.ds(_aligned(k * self.sr, 16), self.sr), :]

    def half(self, ref, h):
        return ref.at[pl.ds(_aligned(h * self.pr, 16), self.pr), :]


def _place():
    x, y, c = lax.axis_index("x"), lax.axis_index("y"), lax.axis_index("c")
    others = [(1 - x, y), (x, 1 - y), (1 - x, 1 - y)]
    return x, y, c, others


_HBM = pl.BlockSpec(memory_space=pltpu.HBM)
_SEM = pl.BlockSpec(memory_space=pltpu.SEMAPHORE)
_TOKEN = jax.ShapeDtypeStruct((8, LANES), F32)
_TOKEN_SPEC = pl.BlockSpec(memory_space=pltpu.VMEM)


def _split_params():
    return pltpu.CompilerParams(has_side_effects=pltpu.SideEffectType.DATAFLOW_SIDE_EFFECTING)


def _in_hbm(a):
    return pltpu.with_memory_space_constraint(a, pltpu.HBM)


def _copy_to(src, dst, send_sem, recv_sem, to):
    return pltpu.make_async_remote_copy(src_ref=src, dst_ref=dst, send_sem=send_sem, recv_sem=recv_sem,
                                        device_id=to, device_id_type=MESH_ID)


def _place_small(packed, chip):
    rows, cols = packed.shape

    def body(chip_ref, p_ref, o_ref):
        o_ref[...] = p_ref[...]

    return pl.pallas_call(
        body, name="place_small",
        grid_spec=pltpu.PrefetchScalarGridSpec(
            num_scalar_prefetch=1, grid=(1,),
            in_specs=[pl.BlockSpec((rows, cols), lambda i, chip_ref: (0, 0))],
            out_specs=pl.BlockSpec((None, rows, cols), lambda i, chip_ref: (chip_ref[0], 0, 0))),
        out_shape=jax.ShapeDtypeStruct((N_CHIPS, rows, cols), F32),
        compiler_params=_cparams("arbitrary"),
    )(chip, packed)


def _gather_start(name, gmats, gfulls, small_all=None):
    n = len(gmats)
    n_in = n + (1 if small_all is not None else 0)

    def body(*refs):
        full_refs = refs[:n]
        outs = refs[n_in:]
        send_sem, recv_sem, token = outs[n_in], outs[n_in + 1], outs[n_in + 2]
        x, y, c, others = _place()
        me_k = 2 * x + y
        if small_all is not None:
            mine = refs[n].at[me_k]
            for ox, oy in others:
                _copy_to(mine, mine, send_sem, recv_sem, (ox, oy, c)).start()
        for m in range(n):
            mine = gmats[m].piece(full_refs[m], me_k, c)
            for ox, oy in others:
                _copy_to(mine, mine, send_sem, recv_sem, (ox, oy, c)).start()
        token[...] = jnp.zeros_like(token)

    operands = [_in_hbm(f) for f in gfulls] + ([_in_hbm(small_all)] if small_all is not None else [])
    outs = pl.pallas_call(
        body, name=name,
        in_specs=[_HBM] * n_in,
        out_specs=[_HBM] * n_in + [_SEM, _SEM, _TOKEN_SPEC],
        out_shape=[pltpu.HBM(a.shape, a.dtype) for a in operands] + [pltpu.SemaphoreType.DMA(())] * 2 + [_TOKEN],
        input_output_aliases={m: m for m in range(n_in)},
        compiler_params=_split_params(),
    )(*operands)
    return list(outs[:n]), (outs[n] if small_all is not None else None), (outs[n_in], outs[n_in + 1]), outs[n_in + 2]


def _gather_pass(name, gmats, gfulls, small_all, sems, after):
    k = len(gmats)
    n_buf = k + (1 if small_all is not None else 0)

    def body(*refs):
        bufs = refs[:n_buf]
        send_sem, recv_sem = refs[n_buf], refs[n_buf + 1]
        outs = refs[n_buf + 3:]
        fsend, frecv, token = outs[n_buf], outs[n_buf + 1], outs[n_buf + 2]
        x, y, c, others = _place()
        me_k = 2 * x + y
        sibling = (x, y, 1 - c)
        for m in range(k):
            for ox, oy in others:
                got = gmats[m].piece(bufs[m], 2 * ox + oy, c)
                _copy_to(got, got, send_sem, recv_sem, sibling).wait_recv()
        if small_all is not None:
            for ox, oy in others:
                got = bufs[k].at[2 * ox + oy]
                _copy_to(got, got, send_sem, recv_sem, sibling).wait_recv()
        for m in range(k):
            mine = gmats[m].piece(bufs[m], me_k, c)
            for _ in others:
                _copy_to(mine, mine, send_sem, recv_sem, sibling).wait_send()
        if small_all is not None:
            for _ in others:
                _copy_to(bufs[k].at[me_k], bufs[k].at[me_k], send_sem, recv_sem, sibling).wait_send()
        for m in range(k):
            for ox, oy in others:
                got = gmats[m].piece(bufs[m], 2 * ox + oy, c)
                _copy_to(got, got, fsend, frecv, sibling).start()
        token[...] = jnp.zeros_like(token)

    operands = [_in_hbm(f) for f in gfulls] + ([_in_hbm(small_all)] if small_all is not None else [])
    outs = pl.pallas_call(
        body, name=name,
        in_specs=[_HBM] * n_buf + [_SEM, _SEM, _ANY],
        out_specs=[_HBM] * n_buf + [_SEM, _SEM, _TOKEN_SPEC],
        out_shape=[pltpu.HBM(a.shape, a.dtype) for a in operands] + [pltpu.SemaphoreType.DMA(())] * 2 + [_TOKEN],
        input_output_aliases={i: i for i in range(n_buf)},
        compiler_params=_split_params(),
    )(*operands, sems[0], sems[1], after)
    return list(outs[:n_buf]), (outs[n_buf], outs[n_buf + 1]), outs[n_buf + 2]


def _gather_done(name, gmats, gfulls, sems, after):
    k = len(gmats)

    def body(*refs):
        bufs = refs[:k]
        send_sem, recv_sem = refs[k], refs[k + 1]
        x, y, c, others = _place()
        sibling = (x, y, 1 - c)
        for m in range(k):
            for ox, oy in others:
                got = gmats[m].piece(bufs[m], 2 * ox + oy, 1 - c)
                _copy_to(got, got, send_sem, recv_sem, sibling).wait_recv()
        for m in range(k):
            for ox, oy in others:
                sent = gmats[m].piece(bufs[m], 2 * ox + oy, c)
                _copy_to(sent, sent, send_sem, recv_sem, sibling).wait_send()

    outs = pl.pallas_call(
        body, name=name,
        in_specs=[_HBM] * k + [_SEM, _SEM, _ANY], out_specs=[_HBM] * k,
        out_shape=[pltpu.HBM(a.shape, a.dtype) for a in gfulls],
        input_output_aliases={i: i for i in range(k)},
        compiler_params=_split_params(),
    )(*[_in_hbm(f) for f in gfulls], sems[0], sems[1], after)
    return list(outs)


_FLIPS = [(fx, fy, fc) for fx in (0, 1) for fy in (0, 1) for fc in (0, 1) if (fx, fy, fc) != (0, 0, 0)]


def _small_start(packed, after):
    def body(small_ref, after_ref, small_thru, land_ref, send_sem, recv_sem, token):
        x, y, c, _ = _place()
        me = 4 * x + 2 * y + c
        for fx, fy, fc in _FLIPS:
            _copy_to(small_ref, land_ref.at[me], send_sem, recv_sem, (x ^ fx, y ^ fy, c ^ fc)).start()
        token[...] = jnp.zeros_like(token)

    outs = pl.pallas_call(
        body, name="small_grads_start",
        in_specs=[_HBM, _ANY], out_specs=[_HBM, _HBM, _SEM, _SEM, _TOKEN_SPEC],
        out_shape=[pltpu.HBM(packed.shape, F32), pltpu.HBM((N_DEV,) + packed.shape, F32)]
        + [pltpu.SemaphoreType.DMA(())] * 2 + [_TOKEN],
        input_output_aliases={0: 0},
        compiler_params=_split_params(),
    )(_in_hbm(packed), after)
    return outs[0], outs[1], (outs[2], outs[3]), outs[4]


def _small_wait(packed, landed, sems, after):
    def body(small_ref, land_ref, send_sem, recv_sem, after_ref, small_thru, land_thru):
        x, y, c, _ = _place()
        for fx, fy, fc in _FLIPS:
            got = land_ref.at[4 * (x ^ fx) + 2 * (y ^ fy) + (c ^ fc)]
            _copy_to(got, got, send_sem, recv_sem, (x, y, 1 - c)).wait_recv()
        for _ in _FLIPS:
            _copy_to(small_ref, small_ref, send_sem, recv_sem, (x, y, 1 - c)).wait_send()

    outs = pl.pallas_call(
        body, name="small_grads_wait",
        in_specs=[_HBM, _HBM, _SEM, _SEM, _ANY], out_specs=[_HBM, _HBM],
        out_shape=[pltpu.HBM(packed.shape, F32), pltpu.HBM(landed.shape, F32)],
        input_output_aliases={0: 0, 1: 1},
        compiler_params=_split_params(),
    )(_in_hbm(packed), _in_hbm(landed), sems[0], sems[1], after)
    return outs[0], outs[1]


def _exchange_start(name, mats, grads):
    n = len(mats)

    def body(*refs):
        g_refs = refs[:n]
        outs = refs[n:]
        land_refs = outs[n:2 * n]
        send_sem, recv_sem, token = outs[2 * n], outs[2 * n + 1], outs[2 * n + 2]
        x, y, c, _ = _place()
        for m in range(n):
            for k in range(N_CHIPS):
                _copy_to(mats[m].piece(g_refs[m], k, 1 - c), land_refs[m].at[k], send_sem, recv_sem, (x, y, 1 - c)).start()
        token[...] = jnp.zeros_like(token)

    outs = pl.pallas_call(
        body, name=name,
        in_specs=[_HBM] * n,
        out_specs=[_HBM] * (2 * n) + [_SEM, _SEM, _TOKEN_SPEC],
        out_shape=[pltpu.HBM(mt.full_shape, BF16) for mt in mats]
        + [pltpu.HBM((N_CHIPS, mt.pr, mt.pc), BF16) for mt in mats] + [pltpu.SemaphoreType.DMA(())] * 2 + [_TOKEN],
        input_output_aliases={m: m for m in range(n)},
        compiler_params=_split_params(),
    )(*[_in_hbm(g) for g in grads])
    return list(outs[:n]), list(outs[n:2 * n]), (outs[2 * n], outs[2 * n + 1]), outs[2 * n + 2]


def _exchange_wait(name, mats, grads, landed, sems, after):
    n = len(mats)

    def body(*refs):
        g_refs, land_refs = refs[:n], refs[n:2 * n]
        send_sem, recv_sem = refs[2 * n], refs[2 * n + 1]
        x, y, c, _ = _place()
        for m in range(n):
            for k in range(N_CHIPS):
                got = land_refs[m].at[k]
                _copy_to(got, got, send_sem, recv_sem, (x, y, 1 - c)).wait_recv()
        for m in range(n):
            for k in range(N_CHIPS):
                sent = mats[m].piece(g_refs[m], k, 1 - c)
                _copy_to(sent, sent, send_sem, recv_sem, (x, y, 1 - c)).wait_send()

    outs = pl.pallas_call(
        body, name=name,
        in_specs=[_HBM] * (2 * n) + [_SEM, _SEM, _ANY], out_specs=[_HBM] * (2 * n),
        out_shape=[pltpu.HBM(a.shape, a.dtype) for a in list(grads) + list(landed)],
        input_output_aliases={i: i for i in range(2 * n)},
        compiler_params=_split_params(),
    )(*[_in_hbm(a) for a in list(grads) + list(landed)], sems[0], sems[1], after)
    return list(outs[:n]), list(outs[n:])


def _add_halves(name, mat, grad, landed, core):
    tr = _pick(mat.pr, (256, 128, 64, 32, 16))
    per = mat.pr // tr

    def body(core_ref, g_ref, l_ref, o_ref):
        o_ref[...] = (g_ref[...].astype(F32) + l_ref[...].astype(F32)).astype(BF16)

    if mat.kind == "col":
        g_spec = pl.BlockSpec((tr, mat.pc), lambda k, r, core_ref: (core_ref[0] * per + r, k))
    else:
        g_spec = pl.BlockSpec((tr, mat.pc), lambda k, r, core_ref: ((2 * k + core_ref[0]) * per + r, 0))
    p_spec = pl.BlockSpec((None, tr, mat.pc), lambda k, r, core_ref: (k, r, 0))
    return pl.pallas_call(
        body, name=name,
        grid_spec=pltpu.PrefetchScalarGridSpec(num_scalar_prefetch=1, grid=(N_CHIPS, per),
                                               in_specs=[g_spec, p_spec], out_specs=p_spec),
        out_shape=jax.ShapeDtypeStruct((N_CHIPS, mat.pr, mat.pc), BF16),
        compiler_params=_cparams("parallel", "parallel"),
    )(core, grad, landed)


def _scatter_start(name, mats, partials):
    n = len(mats)

    def body(*refs):
        p_refs = refs[:n]
        outs = refs[n:]
        land_refs = outs[n:2 * n]
        send_sem, recv_sem, token = outs[2 * n], outs[2 * n + 1], outs[2 * n + 2]
        x, y, c, others = _place()
        me_k = 2 * x + y
        for m in range(n):
            for ox, oy in others:
                _copy_to(p_refs[m].at[2 * ox + oy], land_refs[m].at[me_k], send_sem, recv_sem, (ox, oy, c)).start()
        token[...] = jnp.zeros_like(token)

    piece_shapes = [pltpu.HBM((N_CHIPS, mt.pr, mt.pc), BF16) for mt in mats]
    outs = pl.pallas_call(
        body, name=name,
        in_specs=[_HBM] * n,
        out_specs=[_HBM] * (2 * n) + [_SEM, _SEM, _TOKEN_SPEC],
        out_shape=piece_shapes + piece_shapes + [pltpu.SemaphoreType.DMA(())] * 2 + [_TOKEN],
        input_output_aliases={m: m for m in range(n)},
        compiler_params=_split_params(),
    )(*[_in_hbm(p) for p in partials])
    return list(outs[:n]), list(outs[n:2 * n]), (outs[2 * n], outs[2 * n + 1]), outs[2 * n + 2]


def _scatter_wait(name, mats, partials, landed, sems, after):
    n = len(mats)

    def body(*refs):
        p_refs, land_refs = refs[:n], refs[n:2 * n]
        send_sem, recv_sem = refs[2 * n], refs[2 * n + 1]
        x, y, c, others = _place()
        for m in range(n):
            for ox, oy in others:
                got = land_refs[m].at[2 * ox + oy]
                _copy_to(got, got, send_sem, recv_sem, (ox, oy, c)).wait_recv()
        for m in range(n):
            for ox, oy in others:
                sent = p_refs[m].at[2 * ox + oy]
                _copy_to(sent, sent, send_sem, recv_sem, (ox, oy, c)).wait_send()

    outs = pl.pallas_call(
        body, name=name,
        in_specs=[_HBM] * (2 * n) + [_SEM, _SEM, _ANY], out_specs=[_HBM] * (2 * n),
        out_shape=[pltpu.HBM(a.shape, a.dtype) for a in list(partials) + list(landed)],
        input_output_aliases={i: i for i in range(2 * n)},
        compiler_params=_split_params(),
    )(*[_in_hbm(a) for a in list(partials) + list(landed)], sems[0], sems[1], after)
    return list(outs[:n]), list(outs[n:])


def _sum_chips(name, mat, partial, landed, slots, layer=None, stack=None, n_layers=1):
    tr = _pick(mat.pr, (256, 128, 64, 32, 16))
    per = mat.pr // tr

    def body(slots_ref, own_ref, a_ref, b_ref, c_ref, *rest):
        o_ref = rest[-1]
        o_ref[...] = ((own_ref[...].astype(F32) + a_ref[...].astype(F32)) + b_ref[...].astype(F32)) + c_ref[...].astype(F32)

    def slot_spec(which):
        return pl.BlockSpec((None, tr, mat.pc), lambda r, slots_ref: (slots_ref[which], r, 0))

    in_specs = [slot_spec(0), slot_spec(1), slot_spec(2), slot_spec(3)]
    operands = [slots, partial, landed, landed, landed]
    aliases = {}
    if layer is None:
        o_spec = pl.BlockSpec((tr, mat.pc), lambda r, slots_ref: (slots_ref[4] * per + r, 0))
        out_shape = jax.ShapeDtypeStruct((mat.sr, mat.sc), F32)
    else:
        o_spec = pl.BlockSpec((None, tr, mat.pc), lambda r, slots_ref: (layer, slots_ref[4] * per + r, 0))
        out_shape = jax.ShapeDtypeStruct((n_layers, mat.sr, mat.sc), F32)
        if stack is not None:
            in_specs.append(_ANY)
            operands.append(stack)
            aliases = {len(operands) - 1: 0}
    return pl.pallas_call(
        body, name=name,
        grid_spec=pltpu.PrefetchScalarGridSpec(num_scalar_prefetch=1, grid=(per,), in_specs=in_specs, out_specs=o_spec),
        out_shape=out_shape, input_output_aliases=aliases,
        compiler_params=_cparams("parallel"),
    )(*operands)


def _share_pieces(name, mats, shards, groups):
    n = len(mats)
    n_out = len(groups)

    def body(*refs):
        out_refs = refs[n_out:2 * n_out]
        send_sems, recv_sems = refs[2 * n_out:]
        x, y, c, _ = _place()
        sibling = (x, y, 1 - c)
        sent, waits = [], []
        for o, members in enumerate(groups):
            for l, m in enumerate(members):
                dst = out_refs[o].at[l] if len(members) > 1 else out_refs[o]
                mine = mats[m].half(dst, c)
                sent.append(pltpu.make_async_remote_copy(src_ref=mine, dst_ref=mine, send_sem=send_sems.at[m],
                                                         recv_sem=recv_sems.at[m], device_id=sibling, device_id_type=MESH_ID))
                theirs = mats[m].half(dst, 1 - c)
                waits.append(pltpu.make_async_remote_copy(src_ref=theirs, dst_ref=theirs, send_sem=send_sems.at[m],
                                                          recv_sem=recv_sems.at[m], device_id=sibling, device_id_type=MESH_ID))
        for cp in sent:
            cp.start()
        for cp in waits:
            cp.wait_recv()
        for cp in sent:
            cp.wait_send()

    return pl.pallas_call(
        body, name=name,
        in_specs=[_ANY] * n_out, out_specs=[_ANY] * n_out,
        out_shape=[jax.ShapeDtypeStruct(s.shape, F32) for s in shards],
        input_output_aliases={o: o for o in range(n_out)},
        scratch_shapes=[pltpu.SemaphoreType.DMA((n,)), pltpu.SemaphoreType.DMA((n,))],
    )(*shards)


def _sum_devices(stacked):
    nd, r, c = stacked.shape

    def body(s_ref, o_ref):
        s = s_ref[0]
        for k in range(1, nd):
            s = s + s_ref[k]
        o_ref[...] = s

    return pl.pallas_call(
        body, name="sum_small_grads", grid=(1,),
        in_specs=[pl.BlockSpec((nd, r, c), lambda i: (0, 0, 0))],
        out_specs=pl.BlockSpec((r, c), lambda i: (0, 0)),
        out_shape=jax.ShapeDtypeStruct((r, c), F32),
        compiler_params=_cparams("arbitrary"),
    )(stacked)


def _pack(arrs):
    flat = jnp.concatenate([a.reshape(-1) for a in arrs])
    rows = -(-flat.shape[0] // (8 * LANES)) * 8
    return jnp.pad(flat, (0, rows * LANES - flat.shape[0])).reshape(rows, LANES)


def _unpack(packed, shapes):
    flat = packed.reshape(-1)
    out, at = [], 0
    for s in shapes:
        size = 1
        for dim in s:
            size *= dim
        out.append(flat[at:at + size].reshape(s))
        at += size
    return out


def kernel(x, mix_norm_e, w_in_e, conv_w_e, conv_b_e, ln_g_e, ln_b_e, w_pool_e, pool_scale_e, w_out_e, mix_norm_o, w_in_o, conv_w_o, w_out_o, ffn_norm, w_gate, w_up, w_down, final_norm, loss_target, m_mix_norm_e, m_w_in_e, m_conv_w_e, m_conv_b_e, m_ln_g_e, m_ln_b_e, m_w_pool_e, m_pool_scale_e, m_w_out_e, m_mix_norm_o, m_w_in_o, m_conv_w_o, m_w_out_o, m_ffn_norm, m_w_gate, m_w_up, m_w_down, m_final_norm, v_mix_norm_e, v_w_in_e, v_conv_w_e, v_conv_b_e, v_ln_g_e, v_ln_b_e, v_w_pool_e, v_pool_scale_e, v_w_out_e, v_mix_norm_o, v_w_in_o, v_conv_w_o, v_w_out_o, v_ffn_norm, v_w_gate, v_w_up, v_w_down, v_final_norm):
    bsz, seq_len, d = x.shape
    t = bsz * seq_len
    depth = ffn_norm.shape[0]
    assert depth == 2 and conv_b_e.shape[1] == pool_scale_e.shape[1]
    ts = _pick(seq_len, (256, 128, 64, 32))
    me_k = 2 * lax.axis_index("x") + lax.axis_index("y")
    core = lax.axis_index("c").astype(jnp.int32).reshape(1)

    mat_src = [("col", w_in_e, 0), ("row", w_out_e, 0), ("col", w_gate, 0), ("col", w_up, 0), ("row", w_down, 0),
               ("col", w_in_o, 0), ("row", w_out_o, 0), ("col", w_gate, 1), ("col", w_up, 1), ("row", w_down, 1)]
    mats = [_Mat(kind, w.shape[1:]) for kind, w, _ in mat_src]
    n_pool = w_pool_e.shape[1]
    pool_mats = tuple(range(len(mats), len(mats) + n_pool))
    mats = mats + [_Mat("row", w_pool_e.shape[2:])] * n_pool
    chip = me_k.astype(jnp.int32).reshape(1)
    small_shards = [conv_w_e[0], w_pool_e[0], mix_norm_o, conv_w_o[0]]
    packed_small = _pack(small_shards)

    chain = [()]

    def seq(fn, *args, **kw):
        out = fn(*args, after=chain[0], **kw)
        chain[0] = (out[0] if isinstance(out, (list, tuple)) else out,)
        return out

    def mm(*args, **kw):
        return seq(_mm, *args, **kw)

    gather_groups = [(0,), (1,), (2, 3), (4,), (5, 6), (7, 8), (9,)]
    fulls, gather_sems, small_all = [None] * len(mats), [], None
    for g, ms in enumerate(gather_groups):
        own16 = [seq(_cast_into_full, "cast_w%d" % m, mats[m], mat_src[m][1], mat_src[m][2], chip) for m in ms]
        sent, landing, sems, token = _gather_start("gather_start%d" % g, [mats[m] for m in ms], own16,
                                                   _place_small(packed_small, chip) if g == 0 else None)
        chain[0] = (token,)
        for m, f in zip(ms, sent):
            fulls[m] = f
        gather_sems.append(sems)
        if g == 0:
            small_all = landing

    passed = {}

    def gather_pass(g):
        ms = gather_groups[g]
        bufs, pass_sems, token = _gather_pass("gather_pass%d" % g, [mats[m] for m in ms], [fulls[m] for m in ms],
                                              small_all if g == 0 else None, gather_sems[g], chain[0][0])
        chain[0] = (token,)
        passed[g] = (bufs, pass_sems)

    def gather_done(g):
        ms = gather_groups[g]
        bufs, pass_sems = passed[g]
        done = _gather_done("gather_done%d" % g, [mats[m] for m in ms], bufs[:len(ms)], pass_sems, chain[0][0])
        chain[0] = (done[0],)
        return done + bufs[len(ms):]

    h0 = x.reshape(t, d)
    target = loss_target.reshape(t, d)
    gather_pass(0)
    n1 = seq(_rms_fwd, "mix0_norm", h0, mix_norm_e)
    W_in_e, small_all = gather_done(0)
    per_chip = [_unpack(small_all[k], [s.shape for s in small_shards]) for k in range(N_CHIPS)]
    conv_w_e_f = jnp.concatenate([p[0] for p in per_chip], axis=1)
    w_pool_f = jnp.concatenate([p[1] for p in per_chip], axis=1)
    mix_norm_o_f = jnp.concatenate([p[2] for p in per_chip], axis=1)
    conv_w_o_f = jnp.concatenate([p[3] for p in per_chip], axis=1)
    W_gate, W_up, W_down = [None, None], [None, None], [None, None]

    (u_e,) = mm("mix0_in", [(n1, W_in_e)], "nn", [F32], _ep_store, tm=1024, tn=1024, tk=2048)
    gather_pass(1)
    a2, cat = seq(_mixer_e_fwd, u_e, conv_w_e_f, conv_b_e, ln_g_e, ln_b_e, w_pool_f, pool_scale_e, seq_len, ts)
    (W_out_e,) = gather_done(1)
    (h1,) = mm("mix0_out", [(cat, W_out_e)], "nn", [F32], _ep_residual, extras=(h0,), tm=1024, tn=1024, tk=2048)
    gather_pass(2)
    n2 = seq(_rms_fwd, "ffn0_norm", h1, ffn_norm[0:1])
    W_gate[0], W_up[0] = gather_done(2)
    gt0, up0, act0 = mm("ffn0_gate_up", [(n2, W_gate[0]), (n2, W_up[0])], "nn", [BF16] * 3, _ep_swiglu,
                        acc_of=(0, 1), tm=1024, tn=512, tk=2048)
    gather_pass(3)
    (W_down[0],) = gather_done(3)
    gather_pass(4)
    (h2,) = mm("ffn0_down", [(act0, W_down[0])], "nn", [F32], _ep_residual, extras=(h1,), tm=512, tn=1024, tk=5632)
    n3 = seq(_rms_fwd, "mix1_norm", h2, mix_norm_o_f)
    W_in_o, W_out_o = gather_done(4)
    (u_o,) = mm("mix1_in", [(n3, W_in_o)], "nn", [F32], _ep_store, tm=1024, tn=1024, tk=2048)
    gather_pass(5)
    y_o = seq(_mixer_o_fwd, u_o, conv_w_o_f, seq_len, ts)
    (h3,) = mm("mix1_out", [(y_o, W_out_o)], "nn", [F32], _ep_residual, extras=(h2,), tm=1024, tn=1024, tk=2048)
    gather_pass(6)
    n4 = seq(_rms_fwd, "ffn1_norm", h3, ffn_norm[1:2])
    W_gate[1], W_up[1] = gather_done(5)
    gt1, up1, act1 = mm("ffn1_gate_up", [(n4, W_gate[1]), (n4, W_up[1])], "nn", [BF16] * 3, _ep_swiglu,
                        acc_of=(0, 1), tm=1024, tn=512, tk=2048)
    (W_down[1],) = gather_done(6)
    (h4,) = mm("ffn1_down", [(act1, W_down[1])], "nn", [F32], _ep_residual, extras=(h3,), tm=512, tn=1024, tk=5632)
    loss_part, dh4, dh4b, d_final_norm = seq(_loss_head, h4, final_norm.reshape(1, d), target)
    loss = lax.psum(loss_part[0, 0], AXES)

    in_flight = {}
    partials, scattered = [None] * len(mats), [None] * len(mats)

    def reduce_begin(tag, ms, grads):
        gm = [mats[m] for m in ms]
        grads, landed, sems, token = _exchange_start("exchange_start_" + tag, gm, grads)
        chain[0] = (token,)
        in_flight[tag] = (ms, gm, grads, landed, sems)

    def reduce_advance(tag):
        ms, gm, grads, landed, sems = in_flight[tag]
        grads, landed = _exchange_wait("exchange_wait_" + tag, gm, grads, landed, sems, chain[0][0])
        parts = [_add_halves("add_halves%d" % m, mats[m], g, l, core) for m, g, l in zip(ms, grads, landed)]
        parts, lands, sems, token = _scatter_start("scatter_start_" + tag, gm, parts)
        chain[0] = (token,)
        in_flight[tag] = (ms, gm, parts, lands, sems)

    def reduce_finish(tag):
        ms, gm, parts, lands, sems = in_flight[tag]
        parts, lands = _scatter_wait("scatter_wait_" + tag, gm, parts, lands, sems, chain[0][0])
        chain[0] = (lands[0],)
        for m, p, l in zip(ms, parts, lands):
            partials[m], scattered[m] = p, l

    def ffn_bwd(l, dhb, n, gt, up, act):
        dgt, dup = mm("ffn%d_dact" % l, [(dhb, W_down[l])], "nt", [BF16, BF16], _ep_swiglu_bwd, extras=(gt, up),
                      tm=1024, tn=512, tk=2048, row_chunk=256)
        (dW_down,) = mm("ffn%d_dw_down" % l, [(act, dhb)], "tn", [BF16], _ep_store, tm=512, tn=1024, tk=4096)
        (dn,) = mm("ffn%d_dn" % l, [(dgt, W_gate[l]), (dup, W_up[l])], "nt", [F32], _ep_store,
                   tm=1024, tn=1024, tk=1408)
        (dW_gate,) = mm("ffn%d_dw_gate" % l, [(n, dgt)], "tn", [BF16], _ep_store, tm=1024, tn=512, tk=4096)
        (dW_up,) = mm("ffn%d_dw_up" % l, [(n, dup)], "tn", [BF16], _ep_store, tm=1024, tn=512, tk=4096)
        return dn, dW_gate, dW_up, dW_down

    dn4, dW_gate1, dW_up1, dW_down1 = ffn_bwd(1, dh4b, n4, gt1, up1, act1)
    reduce_begin("ffn1", (7, 8, 9), [dW_gate1, dW_up1, dW_down1])
    dh3, dh3b, d_ffn_norm1 = seq(_rms_bwd, "ffn1_norm_bwd", dn4, h3, ffn_norm[1:2], dh4)
    (dy_o,) = mm("mix1_dy", [(dh3b, W_out_o)], "nt", [F32], _ep_store, tm=1024, tn=1024, tk=2048)
    reduce_advance("ffn1")
    (dW_out_o,) = mm("mix1_dw_out", [(y_o, dh3b)], "tn", [BF16], _ep_store, tm=1024, tn=1024, tk=4096)
    du_o, d_conv_w_o = seq(_mixer_o_bwd, dy_o, u_o, conv_w_o_f, seq_len, ts)
    (dW_in_o,) = mm("mix1_dw_in", [(n3, du_o)], "tn", [BF16], _ep_store, tm=1024, tn=1024, tk=4096)
    reduce_begin("mix1", (5, 6), [dW_in_o, dW_out_o])
    (dn3,) = mm("mix1_dn", [(du_o, W_in_o)], "nt", [F32], _ep_store, tm=1024, tn=1024, tk=2048)
    reduce_advance("mix1")
    dh2, dh2b, d_mix_norm_o = seq(_rms_bwd, "mix1_norm_bwd", dn3, h2, mix_norm_o_f, dh3)

    dn2, dW_gate0, dW_up0, dW_down0 = ffn_bwd(0, dh2b, n2, gt0, up0, act0)
    reduce_begin("ffn0", (2, 3, 4), [dW_gate0, dW_up0, dW_down0])
    dh1, dh1b, d_ffn_norm0 = seq(_rms_bwd, "ffn0_norm_bwd", dn2, h1, ffn_norm[0:1], dh2)
    (dcat,) = mm("mix0_dcat", [(dh1b, W_out_e)], "nt", [F32], _ep_store, tm=1024, tn=1024, tk=2048)
    reduce_advance("ffn0")
    (dW_out_e,) = mm("mix0_dw_out", [(cat, dh1b)], "tn", [BF16], _ep_store, tm=1024, tn=1024, tk=4096)
    da2, d_ln_g, d_ln_b, d_conv_b = seq(_mixer_e_bwd_norm, dcat, a2, ln_g_e, ln_b_e, ts)
    du_e, d_conv_w_e, d_w_pool, d_pool_scale = seq(_mixer_e_bwd_mix, da2, dcat, u_e, conv_w_e_f, w_pool_f, pool_scale_e,
                                                   seq_len, ts)
    (dW_in_e,) = mm("mix0_dw_in", [(n1, du_e)], "tn", [BF16], _ep_store, tm=1024, tn=1024, tk=4096)
    reduce_begin("mix0", (0, 1) + pool_mats, [dW_in_e, dW_out_e] + [d_w_pool[g].astype(BF16) for g in range(n_pool)])
    (dn1,) = mm("mix0_dn", [(du_e, W_in_e)], "nt", [F32], _ep_store, tm=1024, tn=1024, tk=2048)
    dx, _, d_mix_norm_e = seq(_rms_bwd, "mix0_norm_bwd", dn1, h0, mix_norm_e, dh1)
    reduce_advance("mix0")

    d_ffn_norm = jnp.concatenate([d_ffn_norm0, d_ffn_norm1], axis=0)
    small_partials = [d_mix_norm_e, d_conv_w_e, d_conv_b, d_ln_g, d_ln_b, d_pool_scale, d_mix_norm_o, d_conv_w_o,
                      d_ffn_norm, d_final_norm]
    packed_grads, small_stack, small_sems, token = _small_start(_pack(small_partials), chain[0][0])
    chain[0] = (token,)
    xi, yi, ci = lax.axis_index("x"), lax.axis_index("y"), lax.axis_index("c")
    slots = jnp.stack([me_k, 2 * (1 - xi) + yi, 2 * xi + (1 - yi), 2 * (1 - xi) + (1 - yi), ci]).astype(jnp.int32)

    def reduced_shards(name, groups):
        halves = []
        for members in groups:
            if len(members) == 1:
                m = members[0]
                halves.append(_sum_chips("sum_chips%d" % m, mats[m], partials[m], scattered[m], slots))
            else:
                stack = None
                for l, m in enumerate(members):
                    stack = _sum_chips("sum_chips%d" % m, mats[m], partials[m], scattered[m], slots, layer=l,
                                       stack=stack, n_layers=len(members))
                halves.append(stack)
        return _share_pieces(name, mats, halves, groups)

    for tag in ("ffn1", "mix1", "ffn0"):
        reduce_finish(tag)
    g_w_in_o, g_w_out_o, g_w_gate, g_w_up, g_w_down = reduced_shards(
        "share_pieces_late", [(5,), (6,), (2, 7), (3, 8), (4, 9)])
    grad = {"w_in_o": g_w_in_o[None], "w_out_o": g_w_out_o[None], "w_gate": g_w_gate, "w_up": g_w_up, "w_down": g_w_down}
    weights = dict(mix_norm_e=mix_norm_e, w_in_e=w_in_e, conv_w_e=conv_w_e, conv_b_e=conv_b_e, ln_g_e=ln_g_e, ln_b_e=ln_b_e,
                   w_pool_e=w_pool_e, pool_scale_e=pool_scale_e, w_out_e=w_out_e, mix_norm_o=mix_norm_o, w_in_o=w_in_o,
                   conv_w_o=conv_w_o, w_out_o=w_out_o, ffn_norm=ffn_norm, w_gate=w_gate, w_up=w_up, w_down=w_down,
                   final_norm=final_norm)
    mom1 = dict(mix_norm_e=m_mix_norm_e, w_in_e=m_w_in_e, conv_w_e=m_conv_w_e, conv_b_e=m_conv_b_e, ln_g_e=m_ln_g_e,
                ln_b_e=m_ln_b_e, w_pool_e=m_w_pool_e, pool_scale_e=m_pool_scale_e, w_out_e=m_w_out_e, mix_norm_o=m_mix_norm_o,
                w_in_o=m_w_in_o, conv_w_o=m_conv_w_o, w_out_o=m_w_out_o, ffn_norm=m_ffn_norm, w_gate=m_w_gate, w_up=m_w_up,
                w_down=m_w_down, final_norm=m_final_norm)
    mom2 = dict(mix_norm_e=v_mix_norm_e, w_in_e=v_w_in_e, conv_w_e=v_conv_w_e, conv_b_e=v_conv_b_e, ln_g_e=v_ln_g_e,
                ln_b_e=v_ln_b_e, w_pool_e=v_w_pool_e, pool_scale_e=v_pool_scale_e, w_out_e=v_w_out_e, mix_norm_o=v_mix_norm_o,
                w_in_o=v_w_in_o, conv_w_o=v_conv_w_o, w_out_o=v_w_out_o, ffn_norm=v_ffn_norm, w_gate=v_w_gate, w_up=v_w_up,
                w_down=v_w_down, final_norm=v_final_norm)
    names = list(weights)

    big = ("w_in_o", "w_out_o", "w_gate", "w_up", "w_down", "w_in_e", "w_out_e")
    delta, new_m, new_v = {}, {}, {}

    def update(nm):
        shape = weights[nm].shape
        rows = 1
        for dim in shape[:-1]:
            rows *= dim
        as2d = lambda a: a.reshape(rows, shape[-1])
        dl, mn, vn = seq(_adamw, "adamw_" + nm, as2d(weights[nm]), as2d(grad[nm]), as2d(mom1[nm]), as2d(mom2[nm]))
        delta[nm], new_m[nm], new_v[nm] = dl.reshape(shape), mn.reshape(shape), vn.reshape(shape)

    for nm in big[:5]:
        update(nm)
    reduce_finish("mix0")
    g_w_in_e, g_w_out_e, g_w_pool = reduced_shards("share_pieces_first", [(0,), (1,), pool_mats])
    grad["w_in_e"], grad["w_out_e"], grad["w_pool_e"] = g_w_in_e[None], g_w_out_e[None], g_w_pool[None]
    for nm in big[5:]:
        update(nm)

    packed_grads, small_stack = _small_wait(packed_grads, small_stack, small_sems, chain[0][0])
    me_dev = 4 * xi + 2 * yi + ci
    small_stack = jnp.where(lax.broadcasted_iota(jnp.int32, (N_DEV, 1, 1), 0) == me_dev, packed_grads[None], small_stack)
    small_sum = _unpack(_sum_devices(small_stack), [s.shape for s in small_partials])
    (g_mix_norm_e, g_conv_w_e_f, g_conv_b, g_ln_g, g_ln_b, g_pool_scale, g_mix_norm_o_f, g_conv_w_o_f,
     g_ffn_norm, g_final_norm) = small_sum

    def my_shard(full, axis):
        size = full.shape[axis] // N_CHIPS
        return lax.dynamic_slice_in_dim(full, me_k * size, size, axis)

    grad.update({
        "mix_norm_e": g_mix_norm_e, "conv_w_e": my_shard(g_conv_w_e_f, 1)[None], "conv_b_e": g_conv_b,
        "ln_g_e": g_ln_g, "ln_b_e": g_ln_b, "pool_scale_e": g_pool_scale,
        "mix_norm_o": my_shard(g_mix_norm_o_f, 1), "conv_w_o": my_shard(g_conv_w_o_f, 1)[None],
        "ffn_norm": g_ffn_norm, "final_norm": g_final_norm.reshape(final_norm.shape),
    })
    small = [nm for nm in names if nm not in big]
    shapes = [weights[nm].shape for nm in small]
    dl, mn, vn = _adamw("adamw_small", _pack([weights[nm] for nm in small]), _pack([grad[nm] for nm in small]),
                        _pack([mom1[nm] for nm in small]), _pack([mom2[nm] for nm in small]))
    for nm, a, b, c_ in zip(small, _unpack(dl, shapes), _unpack(mn, shapes), _unpack(vn, shapes)):
        delta[nm], new_m[nm], new_v[nm] = a, b, c_

    grad_x = dx.reshape(bsz, seq_len, d)
    return (loss, grad_x, *[grad[nm] for nm in names], *[delta[nm] for nm in names],
            *[new_m[nm] for nm in names], *[new_v[nm] for nm in names])
```

```python
import jax
import jax.numpy as jnp
from jax import lax
from jax.experimental import pallas as pl
from jax.experimental.pallas import tpu as pltpu

F32 = jnp.float32
BF16 = jnp.bfloat16
MESH_ID = pl.DeviceIdType.MESH
AXES = ("x", "y", "c")
N_CHIPS = 4
N_DEV = 8

EPS = 1e-6
POOL_WINDOWS = (2, 4, 8, 16)
ADAM_LR, ADAM_B1, ADAM_B2, ADAM_EPS, ADAM_WD, ADAM_STEP = 0.001, 0.9, 0.999, 1e-08, 0.01, 10

LANES = 128
CONV_HALO = 32
POOL_HALO = 16
SHORT_HALO = 8
V7X_VMEM_LIMIT = 56 * 1024 * 1024


def _cparams(*sem):
    return pltpu.CompilerParams(dimension_semantics=sem if sem else None, vmem_limit_bytes=V7X_VMEM_LIMIT)


def _pick(dim, prefs):
    for p in prefs:
        if p <= dim and dim % p == 0:
            return p
    return dim


def _sigmoid(x):
    return jax.nn.sigmoid(x)


_ANY = pl.BlockSpec(memory_space=pl.ANY)


def _behind(after, body, n_in):
    if not after:
        return body
    skip = len(after)

    def body_behind(*refs):
        return body(*refs[:n_in], *refs[n_in + skip:])

    return body_behind


_DOT_DIMS = {
    "nn": (((1,), (0,)), ((), ())),
    "nt": (((1,), (1,)), ((), ())),
    "tn": (((0,), (0,)), ((), ())),
}


def _mm(name, pairs, mode, out_dtypes, epilogue, extras=(), acc_of=None, tm=512, tn=512, tk=2048, row_chunk=0, after=()):
    a0, b0 = pairs[0]
    if mode == "nn":
        (m, k), n = a0.shape, b0.shape[1]
    elif mode == "nt":
        (m, k), n = a0.shape, b0.shape[0]
    else:
        (k, m), n = a0.shape, b0.shape[1]
    tm = _pick(m, (tm, 512, 256, 128, 64, 32, 16, 8))
    tn = _pick(n, (tn, 512, 256, 128))
    tk = _pick(k, (tk, 2048, 1024, 512, 256, 128))
    nk = k // tk
    n_pairs = len(pairs)
    acc_of = tuple(acc_of) if acc_of is not None else (0,) * n_pairs
    n_acc = max(acc_of) + 1
    n_ex, n_out = len(extras), len(out_dtypes)
    dims = _DOT_DIMS[mode]

    def body(*refs):
        a_refs = refs[:n_pairs]
        b_refs = refs[n_pairs:2 * n_pairs]
        e_refs = refs[2 * n_pairs:2 * n_pairs + n_ex]
        first_out = 2 * n_pairs + n_ex + len(after)
        o_refs = refs[first_out:first_out + n_out]
        acc_refs = refs[first_out + n_out:]

        def partial_sums(rows=None):
            sums = [None] * n_acc
            for p in range(n_pairs):
                a = a_refs[p][...] if rows is None else (a_refs[p][:, rows] if mode == "tn" else a_refs[p][rows, :])
                d = lax.dot_general(a, b_refs[p][...], dims, preferred_element_type=F32)
                sums[acc_of[p]] = d if sums[acc_of[p]] is None else sums[acc_of[p]] + d
            return sums

        if nk == 1 and row_chunk:
            for r0 in range(0, tm, row_chunk):
                rows = pl.ds(r0, row_chunk)
                epilogue(partial_sums(rows), [e.at[rows, :] for e in e_refs], [o.at[rows, :] for o in o_refs])
            return
        if nk == 1:
            epilogue(partial_sums(), e_refs, o_refs)
            return
        kk = pl.program_id(2)

        @pl.when(kk == 0)
        def _():
            for acc in acc_refs:
                acc[...] = jnp.zeros_like(acc)

        for acc, s in zip(acc_refs, partial_sums()):
            acc[...] += s

        @pl.when(kk == nk - 1)
        def _():
            epilogue([acc[...] for acc in acc_refs], e_refs, o_refs)

    if mode == "nn":
        a_spec = pl.BlockSpec((tm, tk), lambda i, j, kk: (i, kk))
        b_spec = pl.BlockSpec((tk, tn), lambda i, j, kk: (kk, j))
    elif mode == "nt":
        a_spec = pl.BlockSpec((tm, tk), lambda i, j, kk: (i, kk))
        b_spec = pl.BlockSpec((tn, tk), lambda i, j, kk: (j, kk))
    else:
        a_spec = pl.BlockSpec((tk, tm), lambda i, j, kk: (kk, i))
        b_spec = pl.BlockSpec((tk, tn), lambda i, j, kk: (kk, j))
    o_spec = pl.BlockSpec((tm, tn), lambda i, j, kk: (i, j))
    outs = pl.pallas_call(
        body,
        name=name,
        grid=(m // tm, n // tn, nk),
        in_specs=[a_spec] * n_pairs + [b_spec] * n_pairs + [o_spec] * n_ex
        + [pl.BlockSpec(memory_space=pl.ANY)] * len(after),
        out_specs=[o_spec] * n_out,
        out_shape=[jax.ShapeDtypeStruct((m, n), dt) for dt in out_dtypes],
        scratch_shapes=[pltpu.VMEM((tm, tn), F32) for _ in range(n_acc)] if nk > 1 else [],
        compiler_params=_cparams("parallel", "parallel", "arbitrary"),
    )(*[p[0] for p in pairs], *[p[1] for p in pairs], *extras, *after)
    return outs


def _ep_store(accs, ex, outs):
    outs[0][...] = accs[0].astype(outs[0].dtype)


def _ep_residual(accs, ex, outs):
    outs[0][...] = ex[0][...] + accs[0]


def _ep_swiglu(accs, ex, outs):
    g, u = accs
    outs[0][...] = g.astype(BF16)
    outs[1][...] = u.astype(BF16)
    outs[2][...] = (g * _sigmoid(g) * u).astype(BF16)


def _ep_swiglu_bwd(accs, ex, outs):
    d = accs[0]
    g = ex[0][...].astype(F32)
    u = ex[1][...].astype(F32)
    s = _sigmoid(g)
    outs[0][...] = (d * u * (s * (1.0 + g * (1.0 - s)))).astype(BF16)
    outs[1][...] = (d * (g * s)).astype(BF16)


def _rms_fwd(name, h, g, after=()):
    t, d = h.shape
    tr = _pick(t, (256, 128, 64, 32, 16, 8))

    def body(h_ref, g_ref, *rest):
        o_ref = rest[-1]
        x = h_ref[...]
        r = lax.rsqrt(jnp.mean(x * x, axis=-1, keepdims=True) + EPS)
        o_ref[...] = (x * r * g_ref[...]).astype(BF16)

    return pl.pallas_call(
        body, name=name, grid=(t // tr,),
        in_specs=[pl.BlockSpec((tr, d), lambda i: (i, 0)), pl.BlockSpec((1, d), lambda i: (0, 0))]
        + [pl.BlockSpec(memory_space=pl.ANY)] * len(after),
        out_specs=pl.BlockSpec((tr, d), lambda i: (i, 0)),
        out_shape=jax.ShapeDtypeStruct((t, d), BF16),
        compiler_params=_cparams("parallel"),
    )(h, g, *after)


def _rms_bwd(name, dn, h, g, dres, after=()):
    t, d = h.shape
    tr = _pick(t, (256, 128, 64, 32, 16, 8))

    def body(dn_ref, h_ref, g_ref, dres_ref, dh_ref, dhb_ref, dg_ref):
        x = h_ref[...]
        r = lax.rsqrt(jnp.mean(x * x, axis=-1, keepdims=True) + EPS)
        xhat = x * r
        dnv = dn_ref[...]

        @pl.when(pl.program_id(0) == 0)
        def _():
            dg_ref[...] = jnp.zeros_like(dg_ref)

        dg_ref[...] += jnp.sum(dnv * xhat, axis=0, keepdims=True)
        dxh = dnv * g_ref[...]
        dh = dres_ref[...] + r * (dxh - xhat * jnp.mean(dxh * xhat, axis=-1, keepdims=True))
        dh_ref[...] = dh
        dhb_ref[...] = dh.astype(BF16)

    row = pl.BlockSpec((tr, d), lambda i: (i, 0))
    vec = pl.BlockSpec((1, d), lambda i: (0, 0))
    return pl.pallas_call(
        _behind(after, body, 4), name=name, grid=(t // tr,),
        in_specs=[row, row, vec, row] + [_ANY] * len(after),
        out_specs=[row, row, vec],
        out_shape=[jax.ShapeDtypeStruct((t, d), F32), jax.ShapeDtypeStruct((t, d), BF16),
                   jax.ShapeDtypeStruct((1, d), F32)],
        compiler_params=_cparams("arbitrary"),
    )(dn, h, g, dres, *after)


def _loss_head(h, g, target, after=()):
    t, d = h.shape
    tr = _pick(t, (256, 128, 64, 32, 16, 8))

    def body(h_ref, g_ref, t_ref, loss_ref, dh_ref, dhb_ref, dg_ref):
        x = h_ref[...]
        gv = g_ref[...]
        r = lax.rsqrt(jnp.mean(x * x, axis=-1, keepdims=True) + EPS)
        xhat = x * r
        err = xhat * gv - t_ref[...]

        @pl.when(pl.program_id(0) == 0)
        def _():
            dg_ref[...] = jnp.zeros_like(dg_ref)
            loss_ref[...] = jnp.zeros_like(loss_ref)

        loss_ref[...] += jnp.full(loss_ref.shape, 0.5 / d, F32) * jnp.sum(err * err)
        dy = err * (1.0 / d)
        dg_ref[...] += jnp.sum(dy * xhat, axis=0, keepdims=True)
        dxh = dy * gv
        dh = r * (dxh - xhat * jnp.mean(dxh * xhat, axis=-1, keepdims=True))
        dh_ref[...] = dh
        dhb_ref[...] = dh.astype(BF16)

    row = pl.BlockSpec((tr, d), lambda i: (i, 0))
    vec = pl.BlockSpec((1, d), lambda i: (0, 0))
    return pl.pallas_call(
        _behind(after, body, 3), name="loss_head", grid=(t // tr,),
        in_specs=[row, vec, row] + [_ANY] * len(after),
        out_specs=[pl.BlockSpec((1, LANES), lambda i: (0, 0)), row, row, vec],
        out_shape=[jax.ShapeDtypeStruct((1, LANES), F32), jax.ShapeDtypeStruct((t, d), F32),
                   jax.ShapeDtypeStruct((t, d), BF16), jax.ShapeDtypeStruct((1, d), F32)],
        compiler_params=_cparams("arbitrary"),
    )(h, g, target, *after)


def _cur(ts, width, col):
    return pl.BlockSpec((ts, width), lambda i: (i, col))


def _prev_halo(ts, halo, width, col):
    per = ts // halo
    return pl.BlockSpec((halo, width), lambda i: (jnp.maximum(i * per - 1, 0), col))


def _next_halo(ts, halo, width, col, n_rows):
    per = ts // halo
    last = n_rows // halo - 1
    return pl.BlockSpec((halo, width), lambda i: (jnp.minimum((i + 1) * per, last), col))


def _full(shape):
    nd = len(shape)
    return pl.BlockSpec(shape, lambda i: (0,) * nd)


def _shift_down(x, n):
    return x if n == 0 else pltpu.roll(x, n, 0)


def _shift_up(x, n):
    return x if n == 0 else pltpu.roll(x, x.shape[0] - n, 0)


CONV_ROWS = 32


def _conv_block_shape(channels, ts):
    return min(CONV_ROWS, ts), min(LANES, channels)


SUBLANES = 8


def _fill_shifted(rot_ref, ext):
    rot_ref[0] = ext
    for r in range(1, SUBLANES):
        rot_ref[r] = _shift_up(ext, r)


def _window(rot_ref, first, rows, c0, cw):
    r = first % SUBLANES
    return rot_ref[r, first - r:first - r + rows, c0:c0 + cw]


def _causal_taps(rot_ref, w_ref, halo, taps, r0, c0, rows, cw):
    acc = jnp.zeros((rows, cw), F32)
    for k in range(taps):
        acc = acc + w_ref[k:k + 1, c0:c0 + cw] * _window(rot_ref, halo + r0 - (taps - 1 - k), rows, c0, cw)
    return acc


def _pool_counts(i, ns, ts, w):
    pos = (i % ns) * ts + lax.broadcasted_iota(jnp.int32, (ts, 1), 0)
    return jnp.minimum(pos + 1, w).astype(F32)


def _pooled(cur, prev_tail, w, cnt):
    s = jnp.concatenate([prev_tail, cur], axis=0)
    d = 1
    while d < w:
        s = s + _shift_down(s, d)
        d *= 2
    return s[POOL_HALO:, :] / cnt - cur


def _mixer_e_fwd(u, conv_w, conv_b, ln_g, ln_b, w_pool, scale, seq, ts, after=()):
    t = u.shape[0]
    dc = conv_b.shape[1]
    ng, pg = w_pool.shape[0], w_pool.shape[1]
    taps = conv_w.shape[0]
    ns = seq // ts

    def body(val_ref, gate_ref, b_ref, pval_ref, pgate_ref, pb_ref, cw_ref, cb_ref, g_ref, be_ref, wp_ref, sc_ref,
             a2_ref, cat_ref, rot_ref):
        i = pl.program_id(0)
        keep_prev = jnp.where(i % ns == 0, 0.0, 1.0)
        a1 = val_ref[...] * _sigmoid(gate_ref[...])
        pa1 = pval_ref[...] * _sigmoid(pgate_ref[...]) * keep_prev
        _fill_shifted(rot_ref, jnp.concatenate([pa1, a1], axis=0))
        rows, cw = _conv_block_shape(dc, ts)
        for c0 in range(0, dc, cw):
            for r0 in range(0, ts, rows):
                acc = _causal_taps(rot_ref, cw_ref, CONV_HALO, taps, r0, c0, rows, cw)
                a2_ref[r0:r0 + rows, c0:c0 + cw] = acc + cb_ref[:, c0:c0 + cw]
        a2 = a2_ref[...]
        mu = jnp.mean(a2, axis=-1, keepdims=True)
        xc = a2 - mu
        rstd = lax.rsqrt(jnp.mean(xc * xc, axis=-1, keepdims=True) + EPS)
        a3 = xc * rstd * g_ref[...] + be_ref[...]
        cat_ref[:, 0:dc] = (a3 * _sigmoid(a3)).astype(BF16)
        for g in range(ng):
            lo, hi = g * pg, (g + 1) * pg
            w = POOL_WINDOWS[g]
            p = _pooled(b_ref[:, lo:hi], pb_ref[:, lo:hi] * keep_prev, w, _pool_counts(i, ns, ts, w))
            q = jnp.dot(p.astype(BF16), wp_ref[g].astype(BF16), preferred_element_type=F32)
            cat_ref[:, dc + lo:dc + hi] = (q * sc_ref[:, lo:hi]).astype(BF16)

    return pl.pallas_call(
        _behind(after, body, 12), name="mixer_e_fwd", grid=(t // ts,),
        in_specs=[_cur(ts, dc, 0), _cur(ts, dc, 1), _cur(ts, dc, 2),
                  _prev_halo(ts, CONV_HALO, dc, 0), _prev_halo(ts, CONV_HALO, dc, 1), _prev_halo(ts, POOL_HALO, dc, 2),
                  _full(conv_w.shape), _full(conv_b.shape), _full(ln_g.shape), _full(ln_b.shape),
                  _full(w_pool.shape), _full(scale.shape)] + [_ANY] * len(after),
        out_specs=[_cur(ts, dc, 0), _cur(ts, 2 * dc, 0)],
        out_shape=[jax.ShapeDtypeStruct((t, dc), F32), jax.ShapeDtypeStruct((t, 2 * dc), BF16)],
        scratch_shapes=[pltpu.VMEM((SUBLANES, CONV_HALO + ts, dc), F32)],
        compiler_params=_cparams("parallel"),
    )(u, u, u, u, u, u, conv_w, conv_b, ln_g, ln_b, w_pool, scale, *after)


def _mixer_e_bwd_norm(dcat, a2, ln_g, ln_b, ts, after=()):
    t, dc = a2.shape

    def body(d_ref, a2_ref, g_ref, be_ref, da2_ref, dg_ref, db_ref, dcb_ref):
        x = a2_ref[...]
        gv = g_ref[...]
        mu = jnp.mean(x, axis=-1, keepdims=True)
        xc = x - mu
        rstd = lax.rsqrt(jnp.mean(xc * xc, axis=-1, keepdims=True) + EPS)
        xhat = xc * rstd
        a3 = xhat * gv + be_ref[...]
        sg = _sigmoid(a3)
        da3 = d_ref[...] * (sg * (1.0 + a3 * (1.0 - sg)))
        dxh = da3 * gv
        da2 = rstd * (dxh - jnp.mean(dxh, axis=-1, keepdims=True)
                      - xhat * jnp.mean(dxh * xhat, axis=-1, keepdims=True))
        da2_ref[...] = da2

        @pl.when(pl.program_id(0) == 0)
        def _():
            dg_ref[...] = jnp.zeros_like(dg_ref)
            db_ref[...] = jnp.zeros_like(db_ref)
            dcb_ref[...] = jnp.zeros_like(dcb_ref)

        dg_ref[...] += jnp.sum(da3 * xhat, axis=0, keepdims=True)
        db_ref[...] += jnp.sum(da3, axis=0, keepdims=True)
        dcb_ref[...] += jnp.sum(da2, axis=0, keepdims=True)

    vec = _full((1, dc))
    return pl.pallas_call(
        _behind(after, body, 4), name="mixer_e_bwd_norm", grid=(t // ts,),
        in_specs=[_cur(ts, dc, 0), _cur(ts, dc, 0), vec, vec] + [_ANY] * len(after),
        out_specs=[_cur(ts, dc, 0), vec, vec, vec],
        out_shape=[jax.ShapeDtypeStruct((t, dc), F32)] + [jax.ShapeDtypeStruct((1, dc), F32)] * 3,
        compiler_params=_cparams("arbitrary"),
    )(dcat, a2, ln_g, ln_b, *after)


def _mixer_e_bwd_mix(da2, dcat, u, conv_w, w_pool, scale, seq, ts, after=()):
    t, dc = da2.shape
    ng, pg = w_pool.shape[0], w_pool.shape[1]
    taps = conv_w.shape[0]
    ns = seq // ts

    def body(da2_ref, nda2_ref, dp_ref, ndp_ref, val_ref, gate_ref, b_ref, pval_ref, pgate_ref, pb_ref,
             cw_ref, wp_ref, sc_ref, du_ref, dcw_ref, dwp_ref, dsc_ref, rota_ref, rotd_ref):
        i = pl.program_id(0)
        keep_prev = jnp.where(i % ns == 0, 0.0, 1.0)
        keep_next = jnp.where(i % ns == ns - 1, 0.0, 1.0)

        @pl.when(i == 0)
        def _():
            dcw_ref[...] = jnp.zeros_like(dcw_ref)
            dwp_ref[...] = jnp.zeros_like(dwp_ref)
            dsc_ref[...] = jnp.zeros_like(dsc_ref)

        val = val_ref[...]
        sg = _sigmoid(gate_ref[...])
        a1 = val * sg
        pa1 = pval_ref[...] * _sigmoid(pgate_ref[...]) * keep_prev
        _fill_shifted(rota_ref, jnp.concatenate([pa1, a1], axis=0))
        _fill_shifted(rotd_ref, jnp.concatenate([da2_ref[...], nda2_ref[...] * keep_next], axis=0))
        rows, cw = _conv_block_shape(dc, ts)
        for c0 in range(0, dc, cw):
            lanes = slice(c0, c0 + cw)
            dw = [jnp.zeros((SUBLANES, cw), F32)] * taps
            for r0 in range(0, ts, rows):
                blk = slice(r0, r0 + rows)
                d_blk = da2_ref[blk, lanes]
                da1 = jnp.zeros((rows, cw), F32)
                for k in range(taps):
                    sh = taps - 1 - k
                    prod = d_blk * _window(rota_ref, CONV_HALO + r0 - sh, rows, c0, cw)
                    for f in range(0, rows, SUBLANES):
                        dw[k] = dw[k] + prod[f:f + SUBLANES, :]
                    da1 = da1 + cw_ref[k:k + 1, lanes] * _window(rotd_ref, r0 + sh, rows, c0, cw)
                sg_b = sg[blk, lanes]
                du_ref[blk, lanes] = (da1 * sg_b).astype(BF16)
                du_ref[blk, dc + c0:dc + c0 + cw] = (da1 * a1[blk, lanes] * (1.0 - sg_b)).astype(BF16)
            for k in range(taps):
                dcw_ref[k:k + 1, lanes] += jnp.sum(dw[k], axis=0, keepdims=True)

        for g in range(ng):
            lo, hi = g * pg, (g + 1) * pg
            w = POOL_WINDOWS[g]
            cnt = _pool_counts(i, ns, ts, w)
            wpb = wp_ref[g].astype(BF16)
            sc = sc_ref[:, lo:hi]
            p = _pooled(b_ref[:, lo:hi], pb_ref[:, lo:hi] * keep_prev, w, cnt)
            pb16 = p.astype(BF16)
            q = jnp.dot(pb16, wpb, preferred_element_type=F32)
            dout = dp_ref[:, lo:hi]
            dsc_ref[:, lo:hi] += jnp.sum(dout * q, axis=0, keepdims=True)
            dq = (dout * sc).astype(BF16)
            dwp_ref[g] += lax.dot_general(pb16, dq, _DOT_DIMS["tn"], preferred_element_type=F32)
            dpool = lax.dot_general(dq, wpb, _DOT_DIMS["nt"], preferred_element_type=F32)
            ndq = (ndp_ref[:, lo:hi] * sc * keep_next).astype(BF16)
            ndpool = lax.dot_general(ndq, wpb, _DOT_DIMS["nt"], preferred_element_type=F32)
            s = jnp.concatenate([dpool / cnt, ndpool * (1.0 / w)], axis=0)
            d = 1
            while d < w:
                s = s + _shift_up(s, d)
                d *= 2
            du_ref[:, 2 * dc + lo:2 * dc + hi] = (s[:ts, :] - dpool).astype(BF16)

    return pl.pallas_call(
        _behind(after, body, 13), name="mixer_e_bwd_mix", grid=(t // ts,),
        in_specs=[_cur(ts, dc, 0), _next_halo(ts, CONV_HALO, dc, 0, t),
                  _cur(ts, dc, 1), _next_halo(ts, POOL_HALO, dc, 1, t),
                  _cur(ts, dc, 0), _cur(ts, dc, 1), _cur(ts, dc, 2),
                  _prev_halo(ts, CONV_HALO, dc, 0), _prev_halo(ts, CONV_HALO, dc, 1), _prev_halo(ts, POOL_HALO, dc, 2),
                  _full(conv_w.shape), _full(w_pool.shape), _full(scale.shape)] + [_ANY] * len(after),
        out_specs=[_cur(ts, 3 * dc, 0), _full(conv_w.shape), _full(w_pool.shape), _full(scale.shape)],
        out_shape=[jax.ShapeDtypeStruct((t, 3 * dc), BF16), jax.ShapeDtypeStruct(conv_w.shape, F32),
                   jax.ShapeDtypeStruct(w_pool.shape, F32), jax.ShapeDtypeStruct(scale.shape, F32)],
        scratch_shapes=[pltpu.VMEM((SUBLANES, ts + CONV_HALO, dc), F32)] * 2,
        compiler_params=_cparams("arbitrary"),
    )(da2, da2, dcat, dcat, u, u, u, u, u, u, conv_w, w_pool, scale, *after)


def _mixer_o_fwd(u, conv_w, seq, ts, after=()):
    t = u.shape[0]
    d = conv_w.shape[1]
    taps = conv_w.shape[0]
    ns = seq // ts

    def body(gb_ref, gc_ref, v_ref, pgc_ref, pv_ref, cw_ref, y_ref):
        keep_prev = jnp.where(pl.program_id(0) % ns == 0, 0.0, 1.0)
        ext = jnp.concatenate([pgc_ref[...] * pv_ref[...] * keep_prev, gc_ref[...] * v_ref[...]], axis=0)
        cc = jnp.zeros_like(ext)
        for k in range(taps):
            cc = cc + cw_ref[k:k + 1, :] * _shift_down(ext, taps - 1 - k)
        y_ref[...] = (gb_ref[...] * cc[SHORT_HALO:, :]).astype(BF16)

    return pl.pallas_call(
        _behind(after, body, 6), name="mixer_o_fwd", grid=(t // ts,),
        in_specs=[_cur(ts, d, 0), _cur(ts, d, 1), _cur(ts, d, 2),
                  _prev_halo(ts, SHORT_HALO, d, 1), _prev_halo(ts, SHORT_HALO, d, 2), _full(conv_w.shape)]
        + [_ANY] * len(after),
        out_specs=_cur(ts, d, 0),
        out_shape=jax.ShapeDtypeStruct((t, d), BF16),
        compiler_params=_cparams("parallel"),
    )(u, u, u, u, u, conv_w, *after)


def _mixer_o_bwd(dy, u, conv_w, seq, ts, after=()):
    t = u.shape[0]
    d = conv_w.shape[1]
    taps = conv_w.shape[0]
    ns = seq // ts

    def body(dy_ref, ndy_ref, gb_ref, gc_ref, v_ref, pgc_ref, pv_ref, ngb_ref, cw_ref, du_ref, dcw_ref):
        i = pl.program_id(0)
        keep_prev = jnp.where(i % ns == 0, 0.0, 1.0)
        keep_next = jnp.where(i % ns == ns - 1, 0.0, 1.0)

        @pl.when(i == 0)
        def _():
            dcw_ref[...] = jnp.zeros_like(dcw_ref)

        gb, gc, v, dyv = gb_ref[...], gc_ref[...], v_ref[...], dy_ref[...]
        ext = jnp.concatenate([pgc_ref[...] * pv_ref[...] * keep_prev, gc * v], axis=0)
        dcc = dyv * gb
        ext_d = jnp.concatenate([dcc, ndy_ref[...] * ngb_ref[...] * keep_next], axis=0)
        cc = jnp.zeros_like(ext)
        dcv = jnp.zeros_like(ext_d)
        for k in range(taps):
            sh = taps - 1 - k
            shifted = _shift_down(ext, sh)
            cc = cc + cw_ref[k:k + 1, :] * shifted
            dcw_ref[k:k + 1, :] += jnp.sum(dcc * shifted[SHORT_HALO:, :], axis=0, keepdims=True)
            dcv = dcv + cw_ref[k:k + 1, :] * _shift_up(ext_d, sh)
        dcv = dcv[:ts, :]
        du_ref[:, 0:d] = (dyv * cc[SHORT_HALO:, :]).astype(BF16)
        du_ref[:, d:2 * d] = (dcv * v).astype(BF16)
        du_ref[:, 2 * d:3 * d] = (dcv * gc).astype(BF16)

    return pl.pallas_call(
        _behind(after, body, 9), name="mixer_o_bwd", grid=(t // ts,),
        in_specs=[_cur(ts, d, 0), _next_halo(ts, SHORT_HALO, d, 0, t),
                  _cur(ts, d, 0), _cur(ts, d, 1), _cur(ts, d, 2),
                  _prev_halo(ts, SHORT_HALO, d, 1), _prev_halo(ts, SHORT_HALO, d, 2),
                  _next_halo(ts, SHORT_HALO, d, 0, t), _full(conv_w.shape)] + [_ANY] * len(after),
        out_specs=[_cur(ts, 3 * d, 0), _full(conv_w.shape)],
        out_shape=[jax.ShapeDtypeStruct((t, 3 * d), BF16), jax.ShapeDtypeStruct(conv_w.shape, F32)],
        compiler_params=_cparams("arbitrary"),
    )(dy, dy, u, u, u, u, u, u, conv_w, *after)


def _cast_into_full(name, mat, w, layer, chip, after=()):
    tr = _pick(mat.sr, (512, 256, 128, 64, 32, 16))
    per = mat.sr // tr

    def body(chip_ref, w_ref, *rest):
        o_ref = rest[-1]
        o_ref[...] = w_ref[...].astype(BF16)

    if mat.kind == "col":
        o_spec = pl.BlockSpec((tr, mat.sc), lambda i, chip_ref: (i, chip_ref[0]))
    else:
        o_spec = pl.BlockSpec((tr, mat.sc), lambda i, chip_ref: (chip_ref[0] * per + i, 0))
    return pl.pallas_call(
        body, name=name,
        grid_spec=pltpu.PrefetchScalarGridSpec(
            num_scalar_prefetch=1, grid=(per,),
            in_specs=[pl.BlockSpec((None, tr, mat.sc), lambda i, chip_ref: (layer, i, 0))] + [_ANY] * len(after),
            out_specs=o_spec),
        out_shape=jax.ShapeDtypeStruct(mat.full_shape, BF16),
        compiler_params=_cparams("parallel"),
    )(chip, w, *after)


def _adamw(name, w, g, m, v, after=()):
    r, c = w.shape
    tr = _pick(r, (256, 128, 64, 32, 16, 8)) if c > 1024 else _pick(r, (512, 256, 128, 64, 32, 16, 8))
    bc1 = 1.0 - ADAM_B1 ** ADAM_STEP
    bc2 = 1.0 - ADAM_B2 ** ADAM_STEP

    def body(w_ref, g_ref, m_ref, v_ref, d_ref, mo_ref, vo_ref):
        gv = g_ref[...]
        mn = ADAM_B1 * m_ref[...] + (1.0 - ADAM_B1) * gv
        vn = ADAM_B2 * v_ref[...] + (1.0 - ADAM_B2) * (gv * gv)
        mo_ref[...] = mn
        vo_ref[...] = vn
        d_ref[...] = -ADAM_LR * ((mn / bc1) / (jnp.sqrt(vn / bc2) + ADAM_EPS) + ADAM_WD * w_ref[...])

    spec = pl.BlockSpec((tr, c), lambda i: (i, 0))
    return pl.pallas_call(_behind(after, body, 4), name=name, grid=(r // tr,),
                          in_specs=[spec] * 4 + [_ANY] * len(after), out_specs=[spec] * 3,
                          out_shape=[jax.ShapeDtypeStruct((r, c), F32)] * 3,
                          compiler_params=_cparams("parallel"))(w, g, m, v, *after)


def _aligned(offset, multiple):
    return offset if isinstance(offset, int) else pl.multiple_of(offset, multiple)


class _Mat:
    def __init__(self, kind, shard_shape):
        self.kind = kind
        self.sr, self.sc = shard_shape
        self.full_shape = (self.sr, self.sc * N_CHIPS) if kind == "col" else (self.sr * N_CHIPS, self.sc)
        self.pr, self.pc = self.sr // 2, self.sc

    def piece(self, ref, k, h):
        if self.kind == "col":
            return ref.at[pl.ds(_aligned(h * self.pr, 16), self.pr), pl.ds(_aligned(k * self.sc, LANES), self.sc)]
        return ref.at[pl.ds(_aligned(k * self.sr + h * self.pr, 16), self.pr), :]

    def shard(self, ref, k):
        if self.kind == "col":
            return ref.at[:, pl.ds(_aligned(k * self.sc, LANES), self.sc)]
        return ref.at[pl.ds(_aligned(k * self.sr, 16), self.sr), :]

    def half(self, ref, h):
        return ref.at[pl.ds(_aligned(h * self.pr, 16), self.pr), :]


def _place():
    x, y, c = lax.axis_index("x"), lax.axis_index("y"), lax.axis_index("c")
    others = [(1 - x, y), (x, 1 - y), (1 - x, 1 - y)]
    return x, y, c, others


_HBM = pl.BlockSpec(memory_space=pltpu.HBM)
_SEM = pl.BlockSpec(memory_space=pltpu.SEMAPHORE)
_TOKEN = jax.ShapeDtypeStruct((8, LANES), F32)
_TOKEN_SPEC = pl.BlockSpec(memory_space=pltpu.VMEM)


def _split_params():
    return pltpu.CompilerParams(has_side_effects=pltpu.SideEffectType.DATAFLOW_SIDE_EFFECTING)


def _in_hbm(a):
    return pltpu.with_memory_space_constraint(a, pltpu.HBM)


def _copy_to(src, dst, send_sem, recv_sem, to):
    return pltpu.make_async_remote_copy(src_ref=src, dst_ref=dst, send_sem=send_sem, recv_sem=recv_sem,
                                        device_id=to, device_id_type=MESH_ID)


def _place_small(packed, chip):
    rows, cols = packed.shape

    def body(chip_ref, p_ref, o_ref):
        o_ref[...] = p_ref[...]

    return pl.pallas_call(
        body, name="place_small",
        grid_spec=pltpu.PrefetchScalarGridSpec(
            num_scalar_prefetch=1, grid=(1,),
            in_specs=[pl.BlockSpec((rows, cols), lambda i, chip_ref: (0, 0))],
            out_specs=pl.BlockSpec((None, rows, cols), lambda i, chip_ref: (chip_ref[0], 0, 0))),
        out_shape=jax.ShapeDtypeStruct((N_CHIPS, rows, cols), F32),
        compiler_params=_cparams("arbitrary"),
    )(chip, packed)


def _gather_start(name, gmats, gfulls, small_all=None):
    n = len(gmats)
    n_in = n + (1 if small_all is not None else 0)

    def body(*refs):
        full_refs = refs[:n]
        outs = refs[n_in:]
        send_sem, recv_sem, token = outs[n_in], outs[n_in + 1], outs[n_in + 2]
        x, y, c, others = _place()
        me_k = 2 * x + y
        if small_all is not None:
            mine = refs[n].at[me_k]
            for ox, oy in others:
                _copy_to(mine, mine, send_sem, recv_sem, (ox, oy, c)).start()
        for m in range(n):
            mine = gmats[m].piece(full_refs[m], me_k, c)
            for ox, oy in others:
                _copy_to(mine, mine, send_sem, recv_sem, (ox, oy, c)).start()
        token[...] = jnp.zeros_like(token)

    operands = [_in_hbm(f) for f in gfulls] + ([_in_hbm(small_all)] if small_all is not None else [])
    outs = pl.pallas_call(
        body, name=name,
        in_specs=[_HBM] * n_in,
        out_specs=[_HBM] * n_in + [_SEM, _SEM, _TOKEN_SPEC],
        out_shape=[pltpu.HBM(a.shape, a.dtype) for a in operands] + [pltpu.SemaphoreType.DMA(())] * 2 + [_TOKEN],
        input_output_aliases={m: m for m in range(n_in)},
        compiler_params=_split_params(),
    )(*operands)
    return list(outs[:n]), (outs[n] if small_all is not None else None), (outs[n_in], outs[n_in + 1]), outs[n_in + 2]


def _gather_pass(name, gmats, gfulls, small_all, sems, after):
    k = len(gmats)
    n_buf = k + (1 if small_all is not None else 0)

    def body(*refs):
        bufs = refs[:n_buf]
        send_sem, recv_sem = refs[n_buf], refs[n_buf + 1]
        outs = refs[n_buf + 3:]
        fsend, frecv, token = outs[n_buf], outs[n_buf + 1], outs[n_buf + 2]
        x, y, c, others = _place()
        me_k = 2 * x + y
        sibling = (x, y, 1 - c)
        for m in range(k):
            for ox, oy in others:
                got = gmats[m].piece(bufs[m], 2 * ox + oy, c)
                _copy_to(got, got, send_sem, recv_sem, sibling).wait_recv()
        if small_all is not None:
            for ox, oy in others:
                got = bufs[k].at[2 * ox + oy]
                _copy_to(got, got, send_sem, recv_sem, sibling).wait_recv()
        for m in range(k):
            mine = gmats[m].piece(bufs[m], me_k, c)
            for _ in others:
                _copy_to(mine, mine, send_sem, recv_sem, sibling).wait_send()
        if small_all is not None:
            for _ in others:
                _copy_to(bufs[k].at[me_k], bufs[k].at[me_k], send_sem, recv_sem, sibling).wait_send()
        for m in range(k):
            for ox, oy in others:
                got = gmats[m].piece(bufs[m], 2 * ox + oy, c)
                _copy_to(got, got, fsend, frecv, sibling).start()
        token[...] = jnp.zeros_like(token)

    operands = [_in_hbm(f) for f in gfulls] + ([_in_hbm(small_all)] if small_all is not None else [])
    outs = pl.pallas_call(
        body, name=name,
        in_specs=[_HBM] * n_buf + [_SEM, _SEM, _ANY],
        out_specs=[_HBM] * n_buf + [_SEM, _SEM, _TOKEN_SPEC],
        out_shape=[pltpu.HBM(a.shape, a.dtype) for a in operands] + [pltpu.SemaphoreType.DMA(())] * 2 + [_TOKEN],
        input_output_aliases={i: i for i in range(n_buf)},
        compiler_params=_split_params(),
    )(*operands, sems[0], sems[1], after)
    return list(outs[:n_buf]), (outs[n_buf], outs[n_buf + 1]), outs[n_buf + 2]


def _gather_done(name, gmats, gfulls, sems, after):
    k = len(gmats)

    def body(*refs):
        bufs = refs[:k]
        send_sem, recv_sem = refs[k], refs[k + 1]
        x, y, c, others = _place()
        sibling = (x, y, 1 - c)
        for m in range(k):
            for ox, oy in others:
                got = gmats[m].piece(bufs[m], 2 * ox + oy, 1 - c)
                _copy_to(got, got, send_sem, recv_sem, sibling).wait_recv()
        for m in range(k):
            for ox, oy in others:
                sent = gmats[m].piece(bufs[m], 2 * ox + oy, c)
                _copy_to(sent, sent, send_sem, recv_sem, sibling).wait_send()

    outs = pl.pallas_call(
        body, name=name,
        in_specs=[_HBM] * k + [_SEM, _SEM, _ANY], out_specs=[_HBM] * k,
        out_shape=[pltpu.HBM(a.shape, a.dtype) for a in gfulls],
        input_output_aliases={i: i for i in range(k)},
        compiler_params=_split_params(),
    )(*[_in_hbm(f) for f in gfulls], sems[0], sems[1], after)
    return list(outs)


_FLIPS = [(fx, fy, fc) for fx in (0, 1) for fy in (0, 1) for fc in (0, 1) if (fx, fy, fc) != (0, 0, 0)]


def _small_start(packed, after):
    def body(small_ref, after_ref, small_thru, land_ref, send_sem, recv_sem, token):
        x, y, c, _ = _place()
        me = 4 * x + 2 * y + c
        for fx, fy, fc in _FLIPS:
            _copy_to(small_ref, land_ref.at[me], send_sem, recv_sem, (x ^ fx, y ^ fy, c ^ fc)).start()
        token[...] = jnp.zeros_like(token)

    outs = pl.pallas_call(
        body, name="small_grads_start",
        in_specs=[_HBM, _ANY], out_specs=[_HBM, _HBM, _SEM, _SEM, _TOKEN_SPEC],
        out_shape=[pltpu.HBM(packed.shape, F32), pltpu.HBM((N_DEV,) + packed.shape, F32)]
        + [pltpu.SemaphoreType.DMA(())] * 2 + [_TOKEN],
        input_output_aliases={0: 0},
        compiler_params=_split_params(),
    )(_in_hbm(packed), after)
    return outs[0], outs[1], (outs[2], outs[3]), outs[4]


def _small_wait(packed, landed, sems, after):
    def body(small_ref, land_ref, send_sem, recv_sem, after_ref, small_thru, land_thru):
        x, y, c, _ = _place()
        for fx, fy, fc in _FLIPS:
            got = land_ref.at[4 * (x ^ fx) + 2 * (y ^ fy) + (c ^ fc)]
            _copy_to(got, got, send_sem, recv_sem, (x, y, 1 - c)).wait_recv()
        for _ in _FLIPS:
            _copy_to(small_ref, small_ref, send_sem, recv_sem, (x, y, 1 - c)).wait_send()

    outs = pl.pallas_call(
        body, name="small_grads_wait",
        in_specs=[_HBM, _HBM, _SEM, _SEM, _ANY], out_specs=[_HBM, _HBM],
        out_shape=[pltpu.HBM(packed.shape, F32), pltpu.HBM(landed.shape, F32)],
        input_output_aliases={0: 0, 1: 1},
        compiler_params=_split_params(),
    )(_in_hbm(packed), _in_hbm(landed), sems[0], sems[1], after)
    return outs[0], outs[1]


def _exchange_start(name, mats, grads):
    n = len(mats)

    def body(*refs):
        g_refs = refs[:n]
        outs = refs[n:]
        land_refs = outs[n:2 * n]
        send_sem, recv_sem, token = outs[2 * n], outs[2 * n + 1], outs[2 * n + 2]
        x, y, c, _ = _place()
        for m in range(n):
            for k in range(N_CHIPS):
                _copy_to(mats[m].piece(g_refs[m], k, 1 - c), land_refs[m].at[k], send_sem, recv_sem, (x, y, 1 - c)).start()
        token[...] = jnp.zeros_like(token)

    outs = pl.pallas_call(
        body, name=name,
        in_specs=[_HBM] * n,
        out_specs=[_HBM] * (2 * n) + [_SEM, _SEM, _TOKEN_SPEC],
        out_shape=[pltpu.HBM(mt.full_shape, BF16) for mt in mats]
        + [pltpu.HBM((N_CHIPS, mt.pr, mt.pc), BF16) for mt in mats] + [pltpu.SemaphoreType.DMA(())] * 2 + [_TOKEN],
        input_output_aliases={m: m for m in range(n)},
        compiler_params=_split_params(),
    )(*[_in_hbm(g) for g in grads])
    return list(outs[:n]), list(outs[n:2 * n]), (outs[2 * n], outs[2 * n + 1]), outs[2 * n + 2]


def _exchange_wait(name, mats, grads, landed, sems, after):
    n = len(mats)

    def body(*refs):
        g_refs, land_refs = refs[:n], refs[n:2 * n]
        send_sem, recv_sem = refs[2 * n], refs[2 * n + 1]
        x, y, c, _ = _place()
        for m in range(n):
            for k in range(N_CHIPS):
                got = land_refs[m].at[k]
                _copy_to(got, got, send_sem, recv_sem, (x, y, 1 - c)).wait_recv()
        for m in range(n):
            for k in range(N_CHIPS):
                sent = mats[m].piece(g_refs[m], k, 1 - c)
                _copy_to(sent, sent, send_sem, recv_sem, (x, y, 1 - c)).wait_send()

    outs = pl.pallas_call(
        body, name=name,
        in_specs=[_HBM] * (2 * n) + [_SEM, _SEM, _ANY], out_specs=[_HBM] * (2 * n),
        out_shape=[pltpu.HBM(a.shape, a.dtype) for a in list(grads) + list(landed)],
        input_output_aliases={i: i for i in range(2 * n)},
        compiler_params=_split_params(),
    )(*[_in_hbm(a) for a in list(grads) + list(landed)], sems[0], sems[1], after)
    return list(outs[:n]), list(outs[n:])


def _add_halves(name, mat, grad, landed, core):
    tr = _pick(mat.pr, (1024, 704, 512, 352, 256, 128, 64, 32, 16))
    per = mat.pr // tr

    def body(core_ref, g_ref, l_ref, o_ref):
        o_ref[...] = (g_ref[...].astype(F32) + l_ref[...].astype(F32)).astype(BF16)

    if mat.kind == "col":
        g_spec = pl.BlockSpec((tr, mat.pc), lambda k, r, core_ref: (core_ref[0] * per + r, k))
    else:
        g_spec = pl.BlockSpec((tr, mat.pc), lambda k, r, core_ref: ((2 * k + core_ref[0]) * per + r, 0))
    p_spec = pl.BlockSpec((None, tr, mat.pc), lambda k, r, core_ref: (k, r, 0))
    return pl.pallas_call(
        body, name=name,
        grid_spec=pltpu.PrefetchScalarGridSpec(num_scalar_prefetch=1, grid=(N_CHIPS, per),
                                               in_specs=[g_spec, p_spec], out_specs=p_spec),
        out_shape=jax.ShapeDtypeStruct((N_CHIPS, mat.pr, mat.pc), BF16),
        compiler_params=_cparams("parallel", "parallel"),
    )(core, grad, landed)


def _scatter_start(name, mats, partials):
    n = len(mats)

    def body(*refs):
        p_refs = refs[:n]
        outs = refs[n:]
        land_refs = outs[n:2 * n]
        send_sem, recv_sem, token = outs[2 * n], outs[2 * n + 1], outs[2 * n + 2]
        x, y, c, others = _place()
        me_k = 2 * x + y
        for m in range(n):
            for ox, oy in others:
                _copy_to(p_refs[m].at[2 * ox + oy], land_refs[m].at[me_k], send_sem, recv_sem, (ox, oy, c)).start()
        token[...] = jnp.zeros_like(token)

    piece_shapes = [pltpu.HBM((N_CHIPS, mt.pr, mt.pc), BF16) for mt in mats]
    outs = pl.pallas_call(
        body, name=name,
        in_specs=[_HBM] * n,
        out_specs=[_HBM] * (2 * n) + [_SEM, _SEM, _TOKEN_SPEC],
        out_shape=piece_shapes + piece_shapes + [pltpu.SemaphoreType.DMA(())] * 2 + [_TOKEN],
        input_output_aliases={m: m for m in range(n)},
        compiler_params=_split_params(),
    )(*[_in_hbm(p) for p in partials])
    return list(outs[:n]), list(outs[n:2 * n]), (outs[2 * n], outs[2 * n + 1]), outs[2 * n + 2]


def _scatter_wait(name, mats, partials, landed, sems, after):
    n = len(mats)

    def body(*refs):
        p_refs, land_refs = refs[:n], refs[n:2 * n]
        send_sem, recv_sem = refs[2 * n], refs[2 * n + 1]
        x, y, c, others = _place()
        for m in range(n):
            for ox, oy in others:
                got = land_refs[m].at[2 * ox + oy]
                _copy_to(got, got, send_sem, recv_sem, (ox, oy, c)).wait_recv()
        for m in range(n):
            for ox, oy in others:
                sent = p_refs[m].at[2 * ox + oy]
                _copy_to(sent, sent, send_sem, recv_sem, (ox, oy, c)).wait_send()

    outs = pl.pallas_call(
        body, name=name,
        in_specs=[_HBM] * (2 * n) + [_SEM, _SEM, _ANY], out_specs=[_HBM] * (2 * n),
        out_shape=[pltpu.HBM(a.shape, a.dtype) for a in list(partials) + list(landed)],
        input_output_aliases={i: i for i in range(2 * n)},
        compiler_params=_split_params(),
    )(*[_in_hbm(a) for a in list(partials) + list(landed)], sems[0], sems[1], after)
    return list(outs[:n]), list(outs[n:])


def _sum_chips(name, mat, partial, landed, slots, layer=None, stack=None, n_layers=1):
    tr = _pick(mat.pr, (1024, 704, 512, 352, 256, 128, 64, 32, 16))
    per = mat.pr // tr

    def body(slots_ref, own_ref, a_ref, b_ref, c_ref, *rest):
        o_ref = rest[-1]
        o_ref[...] = ((own_ref[...].astype(F32) + a_ref[...].astype(F32)) + b_ref[...].astype(F32)) + c_ref[...].astype(F32)

    def slot_spec(which):
        return pl.BlockSpec((None, tr, mat.pc), lambda r, slots_ref: (slots_ref[which], r, 0))

    in_specs = [slot_spec(0), slot_spec(1), slot_spec(2), slot_spec(3)]
    operands = [slots, partial, landed, landed, landed]
    aliases = {}
    if layer is None:
        o_spec = pl.BlockSpec((tr, mat.pc), lambda r, slots_ref: (slots_ref[4] * per + r, 0))
        out_shape = jax.ShapeDtypeStruct((mat.sr, mat.sc), F32)
    else:
        o_spec = pl.BlockSpec((None, tr, mat.pc), lambda r, slots_ref: (layer, slots_ref[4] * per + r, 0))
        out_shape = jax.ShapeDtypeStruct((n_layers, mat.sr, mat.sc), F32)
        if stack is not None:
            in_specs.append(_ANY)
            operands.append(stack)
            aliases = {len(operands) - 1: 0}
    return pl.pallas_call(
        body, name=name,
        grid_spec=pltpu.PrefetchScalarGridSpec(num_scalar_prefetch=1, grid=(per,), in_specs=in_specs, out_specs=o_spec),
        out_shape=out_shape, input_output_aliases=aliases,
        compiler_params=_cparams("parallel"),
    )(*operands)


def _share_pieces(name, mats, shards, groups):
    n = len(mats)
    n_out = len(groups)

    def body(*refs):
        out_refs = refs[n_out:2 * n_out]
        send_sems, recv_sems = refs[2 * n_out:]
        x, y, c, _ = _place()
        sibling = (x, y, 1 - c)
        sent, waits = [], []
        for o, members in enumerate(groups):
            for l, m in enumerate(members):
                dst = out_refs[o].at[l] if len(members) > 1 else out_refs[o]
                mine = mats[m].half(dst, c)
                sent.append(pltpu.make_async_remote_copy(src_ref=mine, dst_ref=mine, send_sem=send_sems.at[m],
                                                         recv_sem=recv_sems.at[m], device_id=sibling, device_id_type=MESH_ID))
                theirs = mats[m].half(dst, 1 - c)
                waits.append(pltpu.make_async_remote_copy(src_ref=theirs, dst_ref=theirs, send_sem=send_sems.at[m],
                                                          recv_sem=recv_sems.at[m], device_id=sibling, device_id_type=MESH_ID))
        for cp in sent:
            cp.start()
        for cp in waits:
            cp.wait_recv()
        for cp in sent:
            cp.wait_send()

    return pl.pallas_call(
        body, name=name,
        in_specs=[_ANY] * n_out, out_specs=[_ANY] * n_out,
        out_shape=[jax.ShapeDtypeStruct(s.shape, F32) for s in shards],
        input_output_aliases={o: o for o in range(n_out)},
        scratch_shapes=[pltpu.SemaphoreType.DMA((n,)), pltpu.SemaphoreType.DMA((n,))],
    )(*shards)


def _sum_devices(stacked):
    nd, r, c = stacked.shape

    def body(s_ref, o_ref):
        s = s_ref[0]
        for k in range(1, nd):
            s = s + s_ref[k]
        o_ref[...] = s

    return pl.pallas_call(
        body, name="sum_small_grads", grid=(1,),
        in_specs=[pl.BlockSpec((nd, r, c), lambda i: (0, 0, 0))],
        out_specs=pl.BlockSpec((r, c), lambda i: (0, 0)),
        out_shape=jax.ShapeDtypeStruct((r, c), F32),
        compiler_params=_cparams("arbitrary"),
    )(stacked)


def _pack(arrs):
    flat = jnp.concatenate([a.reshape(-1) for a in arrs])
    rows = -(-flat.shape[0] // (8 * LANES)) * 8
    return jnp.pad(flat, (0, rows * LANES - flat.shape[0])).reshape(rows, LANES)


def _unpack(packed, shapes):
    flat = packed.reshape(-1)
    out, at = [], 0
    for s in shapes:
        size = 1
        for dim in s:
            size *= dim
        out.append(flat[at:at + size].reshape(s))
        at += size
    return out


def kernel(x, mix_norm_e, w_in_e, conv_w_e, conv_b_e, ln_g_e, ln_b_e, w_pool_e, pool_scale_e, w_out_e, mix_norm_o, w_in_o, conv_w_o, w_out_o, ffn_norm, w_gate, w_up, w_down, final_norm, loss_target, m_mix_norm_e, m_w_in_e, m_conv_w_e, m_conv_b_e, m_ln_g_e, m_ln_b_e, m_w_pool_e, m_pool_scale_e, m_w_out_e, m_mix_norm_o, m_w_in_o, m_conv_w_o, m_w_out_o, m_ffn_norm, m_w_gate, m_w_up, m_w_down, m_final_norm, v_mix_norm_e, v_w_in_e, v_conv_w_e, v_conv_b_e, v_ln_g_e, v_ln_b_e, v_w_pool_e, v_pool_scale_e, v_w_out_e, v_mix_norm_o, v_w_in_o, v_conv_w_o, v_w_out_o, v_ffn_norm, v_w_gate, v_w_up, v_w_down, v_final_norm):
    bsz, seq_len, d = x.shape
    t = bsz * seq_len
    depth = ffn_norm.shape[0]
    assert depth == 2 and conv_b_e.shape[1] == pool_scale_e.shape[1]
    ts = _pick(seq_len, (256, 128, 64, 32))
    me_k = 2 * lax.axis_index("x") + lax.axis_index("y")
    core = lax.axis_index("c").astype(jnp.int32).reshape(1)

    mat_src = [("col", w_in_e, 0), ("row", w_out_e, 0), ("col", w_gate, 0), ("col", w_up, 0), ("row", w_down, 0),
               ("col", w_in_o, 0), ("row", w_out_o, 0), ("col", w_gate, 1), ("col", w_up, 1), ("row", w_down, 1)]
    mats = [_Mat(kind, w.shape[1:]) for kind, w, _ in mat_src]
    n_pool = w_pool_e.shape[1]
    pool_mats = tuple(range(len(mats), len(mats) + n_pool))
    mats = mats + [_Mat("row", w_pool_e.shape[2:])] * n_pool
    chip = me_k.astype(jnp.int32).reshape(1)
    small_shards = [conv_w_e[0], w_pool_e[0], mix_norm_o, conv_w_o[0]]
    packed_small = _pack(small_shards)

    chain = [()]

    def seq(fn, *args, **kw):
        out = fn(*args, after=chain[0], **kw)
        chain[0] = (out[0] if isinstance(out, (list, tuple)) else out,)
        return out

    def mm(*args, **kw):
        return seq(_mm, *args, **kw)

    gather_groups = [(0,), (1,), (2, 3), (4,), (5, 6), (7, 8), (9,)]
    fulls, gather_sems, small_all = [None] * len(mats), [], None
    for g, ms in enumerate(gather_groups):
        own16 = [seq(_cast_into_full, "cast_w%d" % m, mats[m], mat_src[m][1], mat_src[m][2], chip) for m in ms]
        sent, landing, sems, token = _gather_start("gather_start%d" % g, [mats[m] for m in ms], own16,
                                                   _place_small(packed_small, chip) if g == 0 else None)
        chain[0] = (token,)
        for m, f in zip(ms, sent):
            fulls[m] = f
        gather_sems.append(sems)
        if g == 0:
            small_all = landing

    passed = {}

    def gather_pass(g):
        ms = gather_groups[g]
        bufs, pass_sems, token = _gather_pass("gather_pass%d" % g, [mats[m] for m in ms], [fulls[m] for m in ms],
                                              small_all if g == 0 else None, gather_sems[g], chain[0][0])
        chain[0] = (token,)
        passed[g] = (bufs, pass_sems)

    def gather_done(g):
        ms = gather_groups[g]
        bufs, pass_sems = passed[g]
        done = _gather_done("gather_done%d" % g, [mats[m] for m in ms], bufs[:len(ms)], pass_sems, chain[0][0])
        chain[0] = (done[0],)
        return done + bufs[len(ms):]

    h0 = x.reshape(t, d)
    target = loss_target.reshape(t, d)
    gather_pass(0)
    n1 = seq(_rms_fwd, "mix0_norm", h0, mix_norm_e)
    W_in_e, small_all = gather_done(0)
    per_chip = [_unpack(small_all[k], [s.shape for s in small_shards]) for k in range(N_CHIPS)]
    conv_w_e_f = jnp.concatenate([p[0] for p in per_chip], axis=1)
    w_pool_f = jnp.concatenate([p[1] for p in per_chip], axis=1)
    mix_norm_o_f = jnp.concatenate([p[2] for p in per_chip], axis=1)
    conv_w_o_f = jnp.concatenate([p[3] for p in per_chip], axis=1)
    W_gate, W_up, W_down = [None, None], [None, None], [None, None]

    (u_e,) = mm("mix0_in", [(n1, W_in_e)], "nn", [F32], _ep_store, tm=1024, tn=1024, tk=2048)
    gather_pass(1)
    a2, cat = seq(_mixer_e_fwd, u_e, conv_w_e_f, conv_b_e, ln_g_e, ln_b_e, w_pool_f, pool_scale_e, seq_len, ts)
    (W_out_e,) = gather_done(1)
    (h1,) = mm("mix0_out", [(cat, W_out_e)], "nn", [F32], _ep_residual, extras=(h0,), tm=1024, tn=1024, tk=2048)
    gather_pass(2)
    n2 = seq(_rms_fwd, "ffn0_norm", h1, ffn_norm[0:1])
    W_gate[0], W_up[0] = gather_done(2)
    gt0, up0, act0 = mm("ffn0_gate_up", [(n2, W_gate[0]), (n2, W_up[0])], "nn", [BF16] * 3, _ep_swiglu,
                        acc_of=(0, 1), tm=1024, tn=512, tk=2048)
    gather_pass(3)
    (W_down[0],) = gather_done(3)
    gather_pass(4)
    (h2,) = mm("ffn0_down", [(act0, W_down[0])], "nn", [F32], _ep_residual, extras=(h1,), tm=512, tn=1024, tk=5632)
    n3 = seq(_rms_fwd, "mix1_norm", h2, mix_norm_o_f)
    W_in_o, W_out_o = gather_done(4)
    (u_o,) = mm("mix1_in", [(n3, W_in_o)], "nn", [F32], _ep_store, tm=1024, tn=1024, tk=2048)
    gather_pass(5)
    y_o = seq(_mixer_o_fwd, u_o, conv_w_o_f, seq_len, ts)
    (h3,) = mm("mix1_out", [(y_o, W_out_o)], "nn", [F32], _ep_residual, extras=(h2,), tm=1024, tn=1024, tk=2048)
    gather_pass(6)
    n4 = seq(_rms_fwd, "ffn1_norm", h3, ffn_norm[1:2])
    W_gate[1], W_up[1] = gather_done(5)
    gt1, up1, act1 = mm("ffn1_gate_up", [(n4, W_gate[1]), (n4, W_up[1])], "nn", [BF16] * 3, _ep_swiglu,
                        acc_of=(0, 1), tm=1024, tn=512, tk=2048)
    (W_down[1],) = gather_done(6)
    (h4,) = mm("ffn1_down", [(act1, W_down[1])], "nn", [F32], _ep_residual, extras=(h3,), tm=512, tn=1024, tk=5632)
    loss_part, dh4, dh4b, d_final_norm = seq(_loss_head, h4, final_norm.reshape(1, d), target)
    loss = lax.psum(loss_part[0, 0], AXES)

    in_flight = {}
    partials, scattered = [None] * len(mats), [None] * len(mats)

    def reduce_begin(tag, ms, grads):
        gm = [mats[m] for m in ms]
        grads, landed, sems, token = _exchange_start("exchange_start_" + tag, gm, grads)
        chain[0] = (token,)
        in_flight[tag] = (ms, gm, grads, landed, sems)

    def reduce_advance(tag):
        ms, gm, grads, landed, sems = in_flight[tag]
        grads, landed = _exchange_wait("exchange_wait_" + tag, gm, grads, landed, sems, chain[0][0])
        parts = [_add_halves("add_halves%d" % m, mats[m], g, l, core) for m, g, l in zip(ms, grads, landed)]
        parts, lands, sems, token = _scatter_start("scatter_start_" + tag, gm, parts)
        chain[0] = (token,)
        in_flight[tag] = (ms, gm, parts, lands, sems)

    def reduce_finish(tag):
        ms, gm, parts, lands, sems = in_flight[tag]
        parts, lands = _scatter_wait("scatter_wait_" + tag, gm, parts, lands, sems, chain[0][0])
        chain[0] = (lands[0],)
        for m, p, l in zip(ms, parts, lands):
            partials[m], scattered[m] = p, l

    def ffn_bwd(l, dhb, n, gt, up, act):
        dgt, dup = mm("ffn%d_dact" % l, [(dhb, W_down[l])], "nt", [BF16, BF16], _ep_swiglu_bwd, extras=(gt, up),
                      tm=1024, tn=512, tk=2048, row_chunk=256)
        (dW_down,) = mm("ffn%d_dw_down" % l, [(act, dhb)], "tn", [BF16], _ep_store, tm=512, tn=1024, tk=4096)
        (dn,) = mm("ffn%d_dn" % l, [(dgt, W_gate[l]), (dup, W_up[l])], "nt", [F32], _ep_store,
                   tm=1024, tn=1024, tk=1408)
        (dW_gate,) = mm("ffn%d_dw_gate" % l, [(n, dgt)], "tn", [BF16], _ep_store, tm=1024, tn=512, tk=4096)
        (dW_up,) = mm("ffn%d_dw_up" % l, [(n, dup)], "tn", [BF16], _ep_store, tm=1024, tn=512, tk=4096)
        return dn, dW_gate, dW_up, dW_down

    dn4, dW_gate1, dW_up1, dW_down1 = ffn_bwd(1, dh4b, n4, gt1, up1, act1)
    reduce_begin("ffn1", (7, 8, 9), [dW_gate1, dW_up1, dW_down1])
    dh3, dh3b, d_ffn_norm1 = seq(_rms_bwd, "ffn1_norm_bwd", dn4, h3, ffn_norm[1:2], dh4)
    (dy_o,) = mm("mix1_dy", [(dh3b, W_out_o)], "nt", [F32], _ep_store, tm=1024, tn=1024, tk=2048)
    reduce_advance("ffn1")
    (dW_out_o,) = mm("mix1_dw_out", [(y_o, dh3b)], "tn", [BF16], _ep_store, tm=1024, tn=1024, tk=4096)
    du_o, d_conv_w_o = seq(_mixer_o_bwd, dy_o, u_o, conv_w_o_f, seq_len, ts)
    (dW_in_o,) = mm("mix1_dw_in", [(n3, du_o)], "tn", [BF16], _ep_store, tm=1024, tn=1024, tk=4096)
    reduce_begin("mix1", (5, 6), [dW_in_o, dW_out_o])
    (dn3,) = mm("mix1_dn", [(du_o, W_in_o)], "nt", [F32], _ep_store, tm=1024, tn=1024, tk=3072)
    reduce_advance("mix1")
    dh2, dh2b, d_mix_norm_o = seq(_rms_bwd, "mix1_norm_bwd", dn3, h2, mix_norm_o_f, dh3)

    dn2, dW_gate0, dW_up0, dW_down0 = ffn_bwd(0, dh2b, n2, gt0, up0, act0)
    reduce_begin("ffn0", (2, 3, 4), [dW_gate0, dW_up0, dW_down0])
    dh1, dh1b, d_ffn_norm0 = seq(_rms_bwd, "ffn0_norm_bwd", dn2, h1, ffn_norm[0:1], dh2)
    (dcat,) = mm("mix0_dcat", [(dh1b, W_out_e)], "nt", [F32], _ep_store, tm=1024, tn=1024, tk=2048)
    reduce_advance("ffn0")
    (dW_out_e,) = mm("mix0_dw_out", [(cat, dh1b)], "tn", [BF16], _ep_store, tm=1024, tn=1024, tk=4096)
    da2, d_ln_g, d_ln_b, d_conv_b = seq(_mixer_e_bwd_norm, dcat, a2, ln_g_e, ln_b_e, ts)
    du_e, d_conv_w_e, d_w_pool, d_pool_scale = seq(_mixer_e_bwd_mix, da2, dcat, u_e, conv_w_e_f, w_pool_f, pool_scale_e,
                                                   seq_len, ts)
    (dW_in_e,) = mm("mix0_dw_in", [(n1, du_e)], "tn", [BF16], _ep_store, tm=1024, tn=1024, tk=4096)
    reduce_begin("mix0", (0, 1) + pool_mats, [dW_in_e, dW_out_e] + [d_w_pool[g].astype(BF16) for g in range(n_pool)])
    (dn1,) = mm("mix0_dn", [(du_e, W_in_e)], "nt", [F32], _ep_store, tm=1024, tn=1024, tk=3072)
    dx, _, d_mix_norm_e = seq(_rms_bwd, "mix0_norm_bwd", dn1, h0, mix_norm_e, dh1)
    reduce_advance("mix0")

    d_ffn_norm = jnp.concatenate([d_ffn_norm0, d_ffn_norm1], axis=0)
    small_partials = [d_mix_norm_e, d_conv_w_e, d_conv_b, d_ln_g, d_ln_b, d_pool_scale, d_mix_norm_o, d_conv_w_o,
                      d_ffn_norm, d_final_norm]
    packed_grads, small_stack, small_sems, token = _small_start(_pack(small_partials), chain[0][0])
    chain[0] = (token,)
    xi, yi, ci = lax.axis_index("x"), lax.axis_index("y"), lax.axis_index("c")
    slots = jnp.stack([me_k, 2 * (1 - xi) + yi, 2 * xi + (1 - yi), 2 * (1 - xi) + (1 - yi), ci]).astype(jnp.int32)

    def reduced_shards(name, groups):
        halves = []
        for members in groups:
            if len(members) == 1:
                m = members[0]
                halves.append(_sum_chips("sum_chips%d" % m, mats[m], partials[m], scattered[m], slots))
            else:
                stack = None
                for l, m in enumerate(members):
                    stack = _sum_chips("sum_chips%d" % m, mats[m], partials[m], scattered[m], slots, layer=l,
                                       stack=stack, n_layers=len(members))
                halves.append(stack)
        return _share_pieces(name, mats, halves, groups)

    for tag in ("ffn1", "mix1", "ffn0"):
        reduce_finish(tag)
    g_w_in_o, g_w_out_o, g_w_gate, g_w_up, g_w_down = reduced_shards(
        "share_pieces_late", [(5,), (6,), (2, 7), (3, 8), (4, 9)])
    grad = {"w_in_o": g_w_in_o[None], "w_out_o": g_w_out_o[None], "w_gate": g_w_gate, "w_up": g_w_up, "w_down": g_w_down}
    weights = dict(mix_norm_e=mix_norm_e, w_in_e=w_in_e, conv_w_e=conv_w_e, conv_b_e=conv_b_e, ln_g_e=ln_g_e, ln_b_e=ln_b_e,
                   w_pool_e=w_pool_e, pool_scale_e=pool_scale_e, w_out_e=w_out_e, mix_norm_o=mix_norm_o, w_in_o=w_in_o,
                   conv_w_o=conv_w_o, w_out_o=w_out_o, ffn_norm=ffn_norm, w_gate=w_gate, w_up=w_up, w_down=w_down,
                   final_norm=final_norm)
    mom1 = dict(mix_norm_e=m_mix_norm_e, w_in_e=m_w_in_e, conv_w_e=m_conv_w_e, conv_b_e=m_conv_b_e, ln_g_e=m_ln_g_e,
                ln_b_e=m_ln_b_e, w_pool_e=m_w_pool_e, pool_scale_e=m_pool_scale_e, w_out_e=m_w_out_e, mix_norm_o=m_mix_norm_o,
                w_in_o=m_w_in_o, conv_w_o=m_conv_w_o, w_out_o=m_w_out_o, ffn_norm=m_ffn_norm, w_gate=m_w_gate, w_up=m_w_up,
                w_down=m_w_down, final_norm=m_final_norm)
    mom2 = dict(mix_norm_e=v_mix_norm_e, w_in_e=v_w_in_e, conv_w_e=v_conv_w_e, conv_b_e=v_conv_b_e, ln_g_e=v_ln_g_e,
                ln_b_e=v_ln_b_e, w_pool_e=v_w_pool_e, pool_scale_e=v_pool_scale_e, w_out_e=v_w_out_e, mix_norm_o=v_mix_norm_o,
                w_in_o=v_w_in_o, conv_w_o=v_conv_w_o, w_out_o=v_w_out_o, ffn_norm=v_ffn_norm, w_gate=v_w_gate, w_up=v_w_up,
                w_down=v_w_down, final_norm=v_final_norm)
    names = list(weights)

    big = ("w_in_o", "w_out_o", "w_gate", "w_up", "w_down", "w_in_e", "w_out_e")
    delta, new_m, new_v = {}, {}, {}

    def update(nm):
        shape = weights[nm].shape
        rows = 1
        for dim in shape[:-1]:
            rows *= dim
        as2d = lambda a: a.reshape(rows, shape[-1])
        dl, mn, vn = seq(_adamw, "adamw_" + nm, as2d(weights[nm]), as2d(grad[nm]), as2d(mom1[nm]), as2d(mom2[nm]))
        delta[nm], new_m[nm], new_v[nm] = dl.reshape(shape), mn.reshape(shape), vn.reshape(shape)

    for nm in big[:5]:
        update(nm)
    reduce_finish("mix0")
    g_w_in_e, g_w_out_e, g_w_pool = reduced_shards("share_pieces_first", [(0,), (1,), pool_mats])
    grad["w_in_e"], grad["w_out_e"], grad["w_pool_e"] = g_w_in_e[None], g_w_out_e[None], g_w_pool[None]
    for nm in big[5:]:
        update(nm)

    packed_grads, small_stack = _small_wait(packed_grads, small_stack, small_sems, chain[0][0])
    me_dev = 4 * xi + 2 * yi + ci
    small_stack = jnp.where(lax.broadcasted_iota(jnp.int32, (N_DEV, 1, 1), 0) == me_dev, packed_grads[None], small_stack)
    small_sum = _unpack(_sum_devices(small_stack), [s.shape for s in small_partials])
    (g_mix_norm_e, g_conv_w_e_f, g_conv_b, g_ln_g, g_ln_b, g_pool_scale, g_mix_norm_o_f, g_conv_w_o_f,
     g_ffn_norm, g_final_norm) = small_sum

    def my_shard(full, axis):
        size = full.shape[axis] // N_CHIPS
        return lax.dynamic_slice_in_dim(full, me_k * size, size, axis)

    grad.update({
        "mix_norm_e": g_mix_norm_e, "conv_w_e": my_shard(g_conv_w_e_f, 1)[None], "conv_b_e": g_conv_b,
        "ln_g_e": g_ln_g, "ln_b_e": g_ln_b, "pool_scale_e": g_pool_scale,
        "mix_norm_o": my_shard(g_mix_norm_o_f, 1), "conv_w_o": my_shard(g_conv_w_o_f, 1)[None],
        "ffn_norm": g_ffn_norm, "final_norm": g_final_norm.reshape(final_norm.shape),
    })
    small = [nm for nm in names if nm not in big]
    shapes = [weights[nm].shape for nm in small]
    dl, mn, vn = _adamw("adamw_small", _pack([weights[nm] for nm in small]), _pack([grad[nm] for nm in small]),
                        _pack([mom1[nm] for nm in small]), _pack([mom2[nm] for nm in small]))
    for nm, a, b, c_ in zip(small, _unpack(dl, shapes), _unpack(mn, shapes), _unpack(vn, shapes)):
        delta[nm], new_m[nm], new_v[nm] = a, b, c_

    grad_x = dx.reshape(bsz, seq_len, d)
    return (loss, grad_x, *[grad[nm] for nm in names], *[delta[nm] for nm in names],
            *[new_m[nm] for nm in names], *[new_v[nm] for nm in names])
```

```python
import jax
import jax.numpy as jnp
from jax import lax
from jax.experimental import pallas as pl
from jax.experimental.pallas import tpu as pltpu

F32 = jnp.float32
BF16 = jnp.bfloat16
MESH_ID = pl.DeviceIdType.MESH
AXES = ("x", "y", "c")
N_CHIPS = 4
N_DEV = 8

EPS = 1e-6
POOL_WINDOWS = (2, 4, 8, 16)
ADAM_LR, ADAM_B1, ADAM_B2, ADAM_EPS, ADAM_WD, ADAM_STEP = 0.001, 0.9, 0.999, 1e-08, 0.01, 10

LANES = 128
CONV_HALO = 32
POOL_HALO = 16
SHORT_HALO = 8
V7X_VMEM_LIMIT = 56 * 1024 * 1024


def _cparams(*sem):
    return pltpu.CompilerParams(dimension_semantics=sem if sem else None, vmem_limit_bytes=V7X_VMEM_LIMIT)


def _pick(dim, prefs):
    for p in prefs:
        if p <= dim and dim % p == 0:
            return p
    return dim


def _sigmoid(x):
    return jax.nn.sigmoid(x)


_ANY = pl.BlockSpec(memory_space=pl.ANY)


def _behind(after, body, n_in):
    if not after:
        return body
    skip = len(after)

    def body_behind(*refs):
        return body(*refs[:n_in], *refs[n_in + skip:])

    return body_behind


_DOT_DIMS = {
    "nn": (((1,), (0,)), ((), ())),
    "nt": (((1,), (1,)), ((), ())),
    "tn": (((0,), (0,)), ((), ())),
}


def _mm(name, pairs, mode, out_dtypes, epilogue, extras=(), acc_of=None, tm=512, tn=512, tk=2048, row_chunk=0, after=()):
    a0, b0 = pairs[0]
    if mode == "nn":
        (m, k), n = a0.shape, b0.shape[1]
    elif mode == "nt":
        (m, k), n = a0.shape, b0.shape[0]
    else:
        (k, m), n = a0.shape, b0.shape[1]
    tm = _pick(m, (tm, 512, 256, 128, 64, 32, 16, 8))
    tn = _pick(n, (tn, 512, 256, 128))
    tk = _pick(k, (tk, 2048, 1024, 512, 256, 128))
    nk = k // tk
    n_pairs = len(pairs)
    acc_of = tuple(acc_of) if acc_of is not None else (0,) * n_pairs
    n_acc = max(acc_of) + 1
    n_ex, n_out = len(extras), len(out_dtypes)
    dims = _DOT_DIMS[mode]

    def body(*refs):
        a_refs = refs[:n_pairs]
        b_refs = refs[n_pairs:2 * n_pairs]
        e_refs = refs[2 * n_pairs:2 * n_pairs + n_ex]
        first_out = 2 * n_pairs + n_ex + len(after)
        o_refs = refs[first_out:first_out + n_out]
        acc_refs = refs[first_out + n_out:]

        def partial_sums(rows=None):
            sums = [None] * n_acc
            for p in range(n_pairs):
                a = a_refs[p][...] if rows is None else (a_refs[p][:, rows] if mode == "tn" else a_refs[p][rows, :])
                d = lax.dot_general(a, b_refs[p][...], dims, preferred_element_type=F32)
                sums[acc_of[p]] = d if sums[acc_of[p]] is None else sums[acc_of[p]] + d
            return sums

        if nk == 1 and row_chunk:
            for r0 in range(0, tm, row_chunk):
                rows = pl.ds(r0, row_chunk)
                epilogue(partial_sums(rows), [e.at[rows, :] for e in e_refs], [o.at[rows, :] for o in o_refs])
            return
        if nk == 1:
            epilogue(partial_sums(), e_refs, o_refs)
            return
        kk = pl.program_id(2)

        @pl.when(kk == 0)
        def _():
            for acc in acc_refs:
                acc[...] = jnp.zeros_like(acc)

        for acc, s in zip(acc_refs, partial_sums()):
            acc[...] += s

        @pl.when(kk == nk - 1)
        def _():
            epilogue([acc[...] for acc in acc_refs], e_refs, o_refs)

    if mode == "nn":
        a_spec = pl.BlockSpec((tm, tk), lambda i, j, kk: (i, kk))
        b_spec = pl.BlockSpec((tk, tn), lambda i, j, kk: (kk, j))
    elif mode == "nt":
        a_spec = pl.BlockSpec((tm, tk), lambda i, j, kk: (i, kk))
        b_spec = pl.BlockSpec((tn, tk), lambda i, j, kk: (j, kk))
    else:
        a_spec = pl.BlockSpec((tk, tm), lambda i, j, kk: (kk, i))
        b_spec = pl.BlockSpec((tk, tn), lambda i, j, kk: (kk, j))
    o_spec = pl.BlockSpec((tm, tn), lambda i, j, kk: (i, j))
    outs = pl.pallas_call(
        body,
        name=name,
        grid=(m // tm, n // tn, nk),
        in_specs=[a_spec] * n_pairs + [b_spec] * n_pairs + [o_spec] * n_ex
        + [pl.BlockSpec(memory_space=pl.ANY)] * len(after),
        out_specs=[o_spec] * n_out,
        out_shape=[jax.ShapeDtypeStruct((m, n), dt) for dt in out_dtypes],
        scratch_shapes=[pltpu.VMEM((tm, tn), F32) for _ in range(n_acc)] if nk > 1 else [],
        compiler_params=_cparams("parallel", "parallel", "arbitrary"),
    )(*[p[0] for p in pairs], *[p[1] for p in pairs], *extras, *after)
    return outs


def _ep_store(accs, ex, outs):
    outs[0][...] = accs[0].astype(outs[0].dtype)


def _ep_residual(accs, ex, outs):
    outs[0][...] = ex[0][...] + accs[0]


def _ep_swiglu(accs, ex, outs):
    g, u = accs
    outs[0][...] = g.astype(BF16)
    outs[1][...] = u.astype(BF16)
    outs[2][...] = (g * _sigmoid(g) * u).astype(BF16)


def _ep_swiglu_bwd(accs, ex, outs):
    d = accs[0]
    g = ex[0][...].astype(F32)
    u = ex[1][...].astype(F32)
    s = _sigmoid(g)
    outs[0][...] = (d * u * (s * (1.0 + g * (1.0 - s)))).astype(BF16)
    outs[1][...] = (d * (g * s)).astype(BF16)


def _rms_fwd(name, h, g, after=()):
    t, d = h.shape
    tr = _pick(t, (256, 128, 64, 32, 16, 8))

    def body(h_ref, g_ref, *rest):
        o_ref = rest[-1]
        x = h_ref[...]
        r = lax.rsqrt(jnp.mean(x * x, axis=-1, keepdims=True) + EPS)
        o_ref[...] = (x * r * g_ref[...]).astype(BF16)

    return pl.pallas_call(
        body, name=name, grid=(t // tr,),
        in_specs=[pl.BlockSpec((tr, d), lambda i: (i, 0)), pl.BlockSpec((1, d), lambda i: (0, 0))]
        + [pl.BlockSpec(memory_space=pl.ANY)] * len(after),
        out_specs=pl.BlockSpec((tr, d), lambda i: (i, 0)),
        out_shape=jax.ShapeDtypeStruct((t, d), BF16),
        compiler_params=_cparams("parallel"),
    )(h, g, *after)


def _rms_bwd(name, dn, h, g, dres, after=()):
    t, d = h.shape
    tr = _pick(t, (256, 128, 64, 32, 16, 8))

    def body(dn_ref, h_ref, g_ref, dres_ref, dh_ref, dhb_ref, dg_ref):
        x = h_ref[...]
        r = lax.rsqrt(jnp.mean(x * x, axis=-1, keepdims=True) + EPS)
        xhat = x * r
        dnv = dn_ref[...]

        @pl.when(pl.program_id(0) == 0)
        def _():
            dg_ref[...] = jnp.zeros_like(dg_ref)

        dg_ref[...] += jnp.sum(dnv * xhat, axis=0, keepdims=True)
        dxh = dnv * g_ref[...]
        dh = dres_ref[...] + r * (dxh - xhat * jnp.mean(dxh * xhat, axis=-1, keepdims=True))
        dh_ref[...] = dh
        dhb_ref[...] = dh.astype(BF16)

    row = pl.BlockSpec((tr, d), lambda i: (i, 0))
    vec = pl.BlockSpec((1, d), lambda i: (0, 0))
    return pl.pallas_call(
        _behind(after, body, 4), name=name, grid=(t // tr,),
        in_specs=[row, row, vec, row] + [_ANY] * len(after),
        out_specs=[row, row, vec],
        out_shape=[jax.ShapeDtypeStruct((t, d), F32), jax.ShapeDtypeStruct((t, d), BF16),
                   jax.ShapeDtypeStruct((1, d), F32)],
        compiler_params=_cparams("arbitrary"),
    )(dn, h, g, dres, *after)


def _loss_head(h, g, target, after=()):
    t, d = h.shape
    tr = _pick(t, (256, 128, 64, 32, 16, 8))

    def body(h_ref, g_ref, t_ref, loss_ref, dh_ref, dhb_ref, dg_ref):
        x = h_ref[...]
        gv = g_ref[...]
        r = lax.rsqrt(jnp.mean(x * x, axis=-1, keepdims=True) + EPS)
        xhat = x * r
        err = xhat * gv - t_ref[...]

        @pl.when(pl.program_id(0) == 0)
        def _():
            dg_ref[...] = jnp.zeros_like(dg_ref)
            loss_ref[...] = jnp.zeros_like(loss_ref)

        loss_ref[...] += jnp.full(loss_ref.shape, 0.5 / d, F32) * jnp.sum(err * err)
        dy = err * (1.0 / d)
        dg_ref[...] += jnp.sum(dy * xhat, axis=0, keepdims=True)
        dxh = dy * gv
        dh = r * (dxh - xhat * jnp.mean(dxh * xhat, axis=-1, keepdims=True))
        dh_ref[...] = dh
        dhb_ref[...] = dh.astype(BF16)

    row = pl.BlockSpec((tr, d), lambda i: (i, 0))
    vec = pl.BlockSpec((1, d), lambda i: (0, 0))
    return pl.pallas_call(
        _behind(after, body, 3), name="loss_head", grid=(t // tr,),
        in_specs=[row, vec, row] + [_ANY] * len(after),
        out_specs=[pl.BlockSpec((1, LANES), lambda i: (0, 0)), row, row, vec],
        out_shape=[jax.ShapeDtypeStruct((1, LANES), F32), jax.ShapeDtypeStruct((t, d), F32),
                   jax.ShapeDtypeStruct((t, d), BF16), jax.ShapeDtypeStruct((1, d), F32)],
        compiler_params=_cparams("arbitrary"),
    )(h, g, target, *after)


def _cur(ts, width, col):
    return pl.BlockSpec((ts, width), lambda i: (i, col))


def _prev_halo(ts, halo, width, col):
    per = ts // halo
    return pl.BlockSpec((halo, width), lambda i: (jnp.maximum(i * per - 1, 0), col))


def _next_halo(ts, halo, width, col, n_rows):
    per = ts // halo
    last = n_rows // halo - 1
    return pl.BlockSpec((halo, width), lambda i: (jnp.minimum((i + 1) * per, last), col))


def _full(shape):
    nd = len(shape)
    return pl.BlockSpec(shape, lambda i: (0,) * nd)


def _shift_down(x, n):
    return x if n == 0 else pltpu.roll(x, n, 0)


def _shift_up(x, n):
    return x if n == 0 else pltpu.roll(x, x.shape[0] - n, 0)


CONV_ROWS = 32


def _conv_block_shape(channels, ts):
    return min(CONV_ROWS, ts), min(LANES, channels)


SUBLANES = 8


def _fill_shifted(rot_ref, ext):
    rot_ref[0] = ext
    for r in range(1, SUBLANES):
        rot_ref[r] = _shift_up(ext, r)


def _window(rot_ref, first, rows, c0, cw):
    r = first % SUBLANES
    return rot_ref[r, first - r:first - r + rows, c0:c0 + cw]


def _causal_taps(rot_ref, w_ref, halo, taps, r0, c0, rows, cw):
    acc = jnp.zeros((rows, cw), F32)
    for k in range(taps):
        acc = acc + w_ref[k:k + 1, c0:c0 + cw] * _window(rot_ref, halo + r0 - (taps - 1 - k), rows, c0, cw)
    return acc


def _pool_counts(i, ns, ts, w):
    pos = (i % ns) * ts + lax.broadcasted_iota(jnp.int32, (ts, 1), 0)
    return jnp.minimum(pos + 1, w).astype(F32)


def _pooled(cur, prev_tail, w, cnt):
    s = jnp.concatenate([prev_tail, cur], axis=0)
    d = 1
    while d < w:
        s = s + _shift_down(s, d)
        d *= 2
    return s[POOL_HALO:, :] / cnt - cur


def _mixer_e_fwd(u, conv_w, conv_b, ln_g, ln_b, w_pool, scale, seq, ts, after=()):
    t = u.shape[0]
    dc = conv_b.shape[1]
    ng, pg = w_pool.shape[0], w_pool.shape[1]
    taps = conv_w.shape[0]
    ns = seq // ts

    def body(val_ref, gate_ref, b_ref, pval_ref, pgate_ref, pb_ref, cw_ref, cb_ref, g_ref, be_ref, wp_ref, sc_ref,
             a2_ref, cat_ref, rot_ref):
        i = pl.program_id(0)
        keep_prev = jnp.where(i % ns == 0, 0.0, 1.0)
        a1 = val_ref[...] * _sigmoid(gate_ref[...])
        pa1 = pval_ref[...] * _sigmoid(pgate_ref[...]) * keep_prev
        _fill_shifted(rot_ref, jnp.concatenate([pa1, a1], axis=0))
        rows, cw = _conv_block_shape(dc, ts)
        for c0 in range(0, dc, cw):
            for r0 in range(0, ts, rows):
                acc = _causal_taps(rot_ref, cw_ref, CONV_HALO, taps, r0, c0, rows, cw)
                a2_ref[r0:r0 + rows, c0:c0 + cw] = acc + cb_ref[:, c0:c0 + cw]
        a2 = a2_ref[...]
        mu = jnp.mean(a2, axis=-1, keepdims=True)
        xc = a2 - mu
        rstd = lax.rsqrt(jnp.mean(xc * xc, axis=-1, keepdims=True) + EPS)
        a3 = xc * rstd * g_ref[...] + be_ref[...]
        cat_ref[:, 0:dc] = (a3 * _sigmoid(a3)).astype(BF16)
        for g in range(ng):
            lo, hi = g * pg, (g + 1) * pg
            w = POOL_WINDOWS[g]
            p = _pooled(b_ref[:, lo:hi], pb_ref[:, lo:hi] * keep_prev, w, _pool_counts(i, ns, ts, w))
            q = jnp.dot(p.astype(BF16), wp_ref[g].astype(BF16), preferred_element_type=F32)
            cat_ref[:, dc + lo:dc + hi] = (q * sc_ref[:, lo:hi]).astype(BF16)

    return pl.pallas_call(
        _behind(after, body, 12), name="mixer_e_fwd", grid=(t // ts,),
        in_specs=[_cur(ts, dc, 0), _cur(ts, dc, 1), _cur(ts, dc, 2),
                  _prev_halo(ts, CONV_HALO, dc, 0), _prev_halo(ts, CONV_HALO, dc, 1), _prev_halo(ts, POOL_HALO, dc, 2),
                  _full(conv_w.shape), _full(conv_b.shape), _full(ln_g.shape), _full(ln_b.shape),
                  _full(w_pool.shape), _full(scale.shape)] + [_ANY] * len(after),
        out_specs=[_cur(ts, dc, 0), _cur(ts, 2 * dc, 0)],
        out_shape=[jax.ShapeDtypeStruct((t, dc), F32), jax.ShapeDtypeStruct((t, 2 * dc), BF16)],
        scratch_shapes=[pltpu.VMEM((SUBLANES, CONV_HALO + ts, dc), F32)],
        compiler_params=_cparams("parallel"),
    )(u, u, u, u, u, u, conv_w, conv_b, ln_g, ln_b, w_pool, scale, *after)


def _mixer_e_bwd_norm(dcat, a2, ln_g, ln_b, ts, after=()):
    t, dc = a2.shape

    def body(d_ref, a2_ref, g_ref, be_ref, da2_ref, dg_ref, db_ref, dcb_ref):
        x = a2_ref[...]
        gv = g_ref[...]
        mu = jnp.mean(x, axis=-1, keepdims=True)
        xc = x - mu
        rstd = lax.rsqrt(jnp.mean(xc * xc, axis=-1, keepdims=True) + EPS)
        xhat = xc * rstd
        a3 = xhat * gv + be_ref[...]
        sg = _sigmoid(a3)
        da3 = d_ref[...] * (sg * (1.0 + a3 * (1.0 - sg)))
        dxh = da3 * gv
        da2 = rstd * (dxh - jnp.mean(dxh, axis=-1, keepdims=True)
                      - xhat * jnp.mean(dxh * xhat, axis=-1, keepdims=True))
        da2_ref[...] = da2

        @pl.when(pl.program_id(0) == 0)
        def _():
            dg_ref[...] = jnp.zeros_like(dg_ref)
            db_ref[...] = jnp.zeros_like(db_ref)
            dcb_ref[...] = jnp.zeros_like(dcb_ref)

        dg_ref[...] += jnp.sum(da3 * xhat, axis=0, keepdims=True)
        db_ref[...] += jnp.sum(da3, axis=0, keepdims=True)
        dcb_ref[...] += jnp.sum(da2, axis=0, keepdims=True)

    vec = _full((1, dc))
    return pl.pallas_call(
        _behind(after, body, 4), name="mixer_e_bwd_norm", grid=(t // ts,),
        in_specs=[_cur(ts, dc, 0), _cur(ts, dc, 0), vec, vec] + [_ANY] * len(after),
        out_specs=[_cur(ts, dc, 0), vec, vec, vec],
        out_shape=[jax.ShapeDtypeStruct((t, dc), F32)] + [jax.ShapeDtypeStruct((1, dc), F32)] * 3,
        compiler_params=_cparams("arbitrary"),
    )(dcat, a2, ln_g, ln_b, *after)


def _mixer_e_bwd_mix(da2, dcat, u, conv_w, w_pool, scale, seq, ts, after=()):
    t, dc = da2.shape
    ng, pg = w_pool.shape[0], w_pool.shape[1]
    taps = conv_w.shape[0]
    ns = seq // ts

    def body(da2_ref, nda2_ref, dp_ref, ndp_ref, val_ref, gate_ref, b_ref, pval_ref, pgate_ref, pb_ref,
             cw_ref, wp_ref, sc_ref, du_ref, dcw_ref, dwp_ref, dsc_ref, rota_ref, rotd_ref):
        i = pl.program_id(0)
        keep_prev = jnp.where(i % ns == 0, 0.0, 1.0)
        keep_next = jnp.where(i % ns == ns - 1, 0.0, 1.0)

        @pl.when(i == 0)
        def _():
            dcw_ref[...] = jnp.zeros_like(dcw_ref)
            dwp_ref[...] = jnp.zeros_like(dwp_ref)
            dsc_ref[...] = jnp.zeros_like(dsc_ref)

        val = val_ref[...]
        sg = _sigmoid(gate_ref[...])
        a1 = val * sg
        pa1 = pval_ref[...] * _sigmoid(pgate_ref[...]) * keep_prev
        _fill_shifted(rota_ref, jnp.concatenate([pa1, a1], axis=0))
        _fill_shifted(rotd_ref, jnp.concatenate([da2_ref[...], nda2_ref[...] * keep_next], axis=0))
        rows, cw = _conv_block_shape(dc, ts)
        for c0 in range(0, dc, cw):
            lanes = slice(c0, c0 + cw)
            dw = [jnp.zeros((SUBLANES, cw), F32)] * taps
            for r0 in range(0, ts, rows):
                blk = slice(r0, r0 + rows)
                d_blk = da2_ref[blk, lanes]
                da1 = jnp.zeros((rows, cw), F32)
                for k in range(taps):
                    sh = taps - 1 - k
                    prod = d_blk * _window(rota_ref, CONV_HALO + r0 - sh, rows, c0, cw)
                    for f in range(0, rows, SUBLANES):
                        dw[k] = dw[k] + prod[f:f + SUBLANES, :]
                    da1 = da1 + cw_ref[k:k + 1, lanes] * _window(rotd_ref, r0 + sh, rows, c0, cw)
                sg_b = sg[blk, lanes]
                du_ref[blk, lanes] = (da1 * sg_b).astype(BF16)
                du_ref[blk, dc + c0:dc + c0 + cw] = (da1 * a1[blk, lanes] * (1.0 - sg_b)).astype(BF16)
            for k in range(taps):
                dcw_ref[k:k + 1, lanes] += jnp.sum(dw[k], axis=0, keepdims=True)

        for g in range(ng):
            lo, hi = g * pg, (g + 1) * pg
            w = POOL_WINDOWS[g]
            cnt = _pool_counts(i, ns, ts, w)
            wpb = wp_ref[g].astype(BF16)
            sc = sc_ref[:, lo:hi]
            p = _pooled(b_ref[:, lo:hi], pb_ref[:, lo:hi] * keep_prev, w, cnt)
            pb16 = p.astype(BF16)
            q = jnp.dot(pb16, wpb, preferred_element_type=F32)
            dout = dp_ref[:, lo:hi]
            dsc_ref[:, lo:hi] += jnp.sum(dout * q, axis=0, keepdims=True)
            dq = (dout * sc).astype(BF16)
            dwp_ref[g] += lax.dot_general(pb16, dq, _DOT_DIMS["tn"], preferred_element_type=F32)
            dpool = lax.dot_general(dq, wpb, _DOT_DIMS["nt"], preferred_element_type=F32)
            ndq = (ndp_ref[:, lo:hi] * sc * keep_next).astype(BF16)
            ndpool = lax.dot_general(ndq, wpb, _DOT_DIMS["nt"], preferred_element_type=F32)
            s = jnp.concatenate([dpool / cnt, ndpool * (1.0 / w)], axis=0)
            d = 1
            while d < w:
                s = s + _shift_up(s, d)
                d *= 2
            du_ref[:, 2 * dc + lo:2 * dc + hi] = (s[:ts, :] - dpool).astype(BF16)

    return pl.pallas_call(
        _behind(after, body, 13), name="mixer_e_bwd_mix", grid=(t // ts,),
        in_specs=[_cur(ts, dc, 0), _next_halo(ts, CONV_HALO, dc, 0, t),
                  _cur(ts, dc, 1), _next_halo(ts, POOL_HALO, dc, 1, t),
                  _cur(ts, dc, 0), _cur(ts, dc, 1), _cur(ts, dc, 2),
                  _prev_halo(ts, CONV_HALO, dc, 0), _prev_halo(ts, CONV_HALO, dc, 1), _prev_halo(ts, POOL_HALO, dc, 2),
                  _full(conv_w.shape), _full(w_pool.shape), _full(scale.shape)] + [_ANY] * len(after),
        out_specs=[_cur(ts, 3 * dc, 0), _full(conv_w.shape), _full(w_pool.shape), _full(scale.shape)],
        out_shape=[jax.ShapeDtypeStruct((t, 3 * dc), BF16), jax.ShapeDtypeStruct(conv_w.shape, F32),
                   jax.ShapeDtypeStruct(w_pool.shape, F32), jax.ShapeDtypeStruct(scale.shape, F32)],
        scratch_shapes=[pltpu.VMEM((SUBLANES, ts + CONV_HALO, dc), F32)] * 2,
        compiler_params=_cparams("arbitrary"),
    )(da2, da2, dcat, dcat, u, u, u, u, u, u, conv_w, w_pool, scale, *after)


def _mixer_o_fwd(u, conv_w, seq, ts, after=()):
    t = u.shape[0]
    d = conv_w.shape[1]
    taps = conv_w.shape[0]
    ns = seq // ts

    def body(gb_ref, gc_ref, v_ref, pgc_ref, pv_ref, cw_ref, y_ref):
        keep_prev = jnp.where(pl.program_id(0) % ns == 0, 0.0, 1.0)
        ext = jnp.concatenate([pgc_ref[...] * pv_ref[...] * keep_prev, gc_ref[...] * v_ref[...]], axis=0)
        cc = jnp.zeros_like(ext)
        for k in range(taps):
            cc = cc + cw_ref[k:k + 1, :] * _shift_down(ext, taps - 1 - k)
        y_ref[...] = (gb_ref[...] * cc[SHORT_HALO:, :]).astype(BF16)

    return pl.pallas_call(
        _behind(after, body, 6), name="mixer_o_fwd", grid=(t // ts,),
        in_specs=[_cur(ts, d, 0), _cur(ts, d, 1), _cur(ts, d, 2),
                  _prev_halo(ts, SHORT_HALO, d, 1), _prev_halo(ts, SHORT_HALO, d, 2), _full(conv_w.shape)]
        + [_ANY] * len(after),
        out_specs=_cur(ts, d, 0),
        out_shape=jax.ShapeDtypeStruct((t, d), BF16),
        compiler_params=_cparams("parallel"),
    )(u, u, u, u, u, conv_w, *after)


def _mixer_o_bwd(dy, u, conv_w, seq, ts, after=()):
    t = u.shape[0]
    d = conv_w.shape[1]
    taps = conv_w.shape[0]
    ns = seq // ts

    def body(dy_ref, ndy_ref, gb_ref, gc_ref, v_ref, pgc_ref, pv_ref, ngb_ref, cw_ref, du_ref, dcw_ref):
        i = pl.program_id(0)
        keep_prev = jnp.where(i % ns == 0, 0.0, 1.0)
        keep_next = jnp.where(i % ns == ns - 1, 0.0, 1.0)

        @pl.when(i == 0)
        def _():
            dcw_ref[...] = jnp.zeros_like(dcw_ref)

        gb, gc, v, dyv = gb_ref[...], gc_ref[...], v_ref[...], dy_ref[...]
        ext = jnp.concatenate([pgc_ref[...] * pv_ref[...] * keep_prev, gc * v], axis=0)
        dcc = dyv * gb
        ext_d = jnp.concatenate([dcc, ndy_ref[...] * ngb_ref[...] * keep_next], axis=0)
        cc = jnp.zeros_like(ext)
        dcv = jnp.zeros_like(ext_d)
        for k in range(taps):
            sh = taps - 1 - k
            shifted = _shift_down(ext, sh)
            cc = cc + cw_ref[k:k + 1, :] * shifted
            dcw_ref[k:k + 1, :] += jnp.sum(dcc * shifted[SHORT_HALO:, :], axis=0, keepdims=True)
            dcv = dcv + cw_ref[k:k + 1, :] * _shift_up(ext_d, sh)
        dcv = dcv[:ts, :]
        du_ref[:, 0:d] = (dyv * cc[SHORT_HALO:, :]).astype(BF16)
        du_ref[:, d:2 * d] = (dcv * v).astype(BF16)
        du_ref[:, 2 * d:3 * d] = (dcv * gc).astype(BF16)

    return pl.pallas_call(
        _behind(after, body, 9), name="mixer_o_bwd", grid=(t // ts,),
        in_specs=[_cur(ts, d, 0), _next_halo(ts, SHORT_HALO, d, 0, t),
                  _cur(ts, d, 0), _cur(ts, d, 1), _cur(ts, d, 2),
                  _prev_halo(ts, SHORT_HALO, d, 1), _prev_halo(ts, SHORT_HALO, d, 2),
                  _next_halo(ts, SHORT_HALO, d, 0, t), _full(conv_w.shape)] + [_ANY] * len(after),
        out_specs=[_cur(ts, 3 * d, 0), _full(conv_w.shape)],
        out_shape=[jax.ShapeDtypeStruct((t, 3 * d), BF16), jax.ShapeDtypeStruct(conv_w.shape, F32)],
        compiler_params=_cparams("arbitrary"),
    )(dy, dy, u, u, u, u, u, u, conv_w, *after)


def _cast_into_full(name, mat, w, layer, chip, after=()):
    tr = _pick(mat.sr, (512, 256, 128, 64, 32, 16))
    per = mat.sr // tr

    def body(chip_ref, w_ref, *rest):
        o_ref = rest[-1]
        o_ref[...] = w_ref[...].astype(BF16)

    if mat.kind == "col":
        o_spec = pl.BlockSpec((tr, mat.sc), lambda i, chip_ref: (i, chip_ref[0]))
    else:
        o_spec = pl.BlockSpec((tr, mat.sc), lambda i, chip_ref: (chip_ref[0] * per + i, 0))
    return pl.pallas_call(
        body, name=name,
        grid_spec=pltpu.PrefetchScalarGridSpec(
            num_scalar_prefetch=1, grid=(per,),
            in_specs=[pl.BlockSpec((None, tr, mat.sc), lambda i, chip_ref: (layer, i, 0))] + [_ANY] * len(after),
            out_specs=o_spec),
        out_shape=jax.ShapeDtypeStruct(mat.full_shape, BF16),
        compiler_params=_cparams("parallel"),
    )(chip, w, *after)


def _adamw(name, w, g, m, v, after=()):
    r, c = w.shape
    tr = _pick(r, (256, 128, 64, 32, 16, 8)) if c > 1024 else _pick(r, (512, 256, 128, 64, 32, 16, 8))
    bc1 = 1.0 - ADAM_B1 ** ADAM_STEP
    bc2 = 1.0 - ADAM_B2 ** ADAM_STEP

    def body(w_ref, g_ref, m_ref, v_ref, d_ref, mo_ref, vo_ref):
        gv = g_ref[...]
        mn = ADAM_B1 * m_ref[...] + (1.0 - ADAM_B1) * gv
        vn = ADAM_B2 * v_ref[...] + (1.0 - ADAM_B2) * (gv * gv)
        mo_ref[...] = mn
        vo_ref[...] = vn
        d_ref[...] = -ADAM_LR * ((mn / bc1) / (jnp.sqrt(vn / bc2) + ADAM_EPS) + ADAM_WD * w_ref[...])

    spec = pl.BlockSpec((tr, c), lambda i: (i, 0))
    return pl.pallas_call(_behind(after, body, 4), name=name, grid=(r // tr,),
                          in_specs=[spec] * 4 + [_ANY] * len(after), out_specs=[spec] * 3,
                          out_shape=[jax.ShapeDtypeStruct((r, c), F32)] * 3,
                          compiler_params=_cparams("parallel"))(w, g, m, v, *after)


def _aligned(offset, multiple):
    return offset if isinstance(offset, int) else pl.multiple_of(offset, multiple)


class _Mat:
    def __init__(self, kind, shard_shape):
        self.kind = kind
        self.sr, self.sc = shard_shape
        self.full_shape = (self.sr, self.sc * N_CHIPS) if kind == "col" else (self.sr * N_CHIPS, self.sc)
        self.pr, self.pc = self.sr // 2, self.sc

    def piece(self, ref, k, h):
        if self.kind == "col":
            return ref.at[pl.ds(_aligned(h * self.pr, 16), self.pr), pl.ds(_aligned(k * self.sc, LANES), self.sc)]
        return ref.at[pl.ds(_aligned(k * self.sr + h * self.pr, 16), self.pr), :]

    def shard(self, ref, k):
        if self.kind == "col":
            return ref.at[:, pl.ds(_aligned(k * self.sc, LANES), self.sc)]
        return ref.at[pl.ds(_aligned(k * self.sr, 16), self.sr), :]

    def half(self, ref, h):
        return ref.at[pl.ds(_aligned(h * self.pr, 16), self.pr), :]


def _place():
    x, y, c = lax.axis_index("x"), lax.axis_index("y"), lax.axis_index("c")
    others = [(1 - x, y), (x, 1 - y), (1 - x, 1 - y)]
    return x, y, c, others


_HBM = pl.BlockSpec(memory_space=pltpu.HBM)
_SEM = pl.BlockSpec(memory_space=pltpu.SEMAPHORE)
_TOKEN = jax.ShapeDtypeStruct((8, LANES), F32)
_TOKEN_SPEC = pl.BlockSpec(memory_space=pltpu.VMEM)


def _split_params():
    return pltpu.CompilerParams(has_side_effects=pltpu.SideEffectType.DATAFLOW_SIDE_EFFECTING)


def _in_hbm(a):
    return pltpu.with_memory_space_constraint(a, pltpu.HBM)


def _copy_to(src, dst, send_sem, recv_sem, to):
    return pltpu.make_async_remote_copy(src_ref=src, dst_ref=dst, send_sem=send_sem, recv_sem=recv_sem,
                                        device_id=to, device_id_type=MESH_ID)


def _place_small(packed, chip):
    rows, cols = packed.shape

    def body(chip_ref, p_ref, o_ref):
        o_ref[...] = p_ref[...]

    return pl.pallas_call(
        body, name="place_small",
        grid_spec=pltpu.PrefetchScalarGridSpec(
            num_scalar_prefetch=1, grid=(1,),
            in_specs=[pl.BlockSpec((rows, cols), lambda i, chip_ref: (0, 0))],
            out_specs=pl.BlockSpec((None, rows, cols), lambda i, chip_ref: (chip_ref[0], 0, 0))),
        out_shape=jax.ShapeDtypeStruct((N_CHIPS, rows, cols), F32),
        compiler_params=_cparams("arbitrary"),
    )(chip, packed)


def _gather_start(name, gmats, gfulls, small_all=None):
    n = len(gmats)
    n_in = n + (1 if small_all is not None else 0)

    def body(*refs):
        full_refs = refs[:n]
        outs = refs[n_in:]
        send_sem, recv_sem, token = outs[n_in], outs[n_in + 1], outs[n_in + 2]
        x, y, c, others = _place()
        me_k = 2 * x + y
        if small_all is not None:
            mine = refs[n].at[me_k]
            for ox, oy in others:
                _copy_to(mine, mine, send_sem, recv_sem, (ox, oy, c)).start()
        for m in range(n):
            mine = gmats[m].piece(full_refs[m], me_k, c)
            for ox, oy in others:
                _copy_to(mine, mine, send_sem, recv_sem, (ox, oy, c)).start()
        token[...] = jnp.zeros_like(token)

    operands = [_in_hbm(f) for f in gfulls] + ([_in_hbm(small_all)] if small_all is not None else [])
    outs = pl.pallas_call(
        body, name=name,
        in_specs=[_HBM] * n_in,
        out_specs=[_HBM] * n_in + [_SEM, _SEM, _TOKEN_SPEC],
        out_shape=[pltpu.HBM(a.shape, a.dtype) for a in operands] + [pltpu.SemaphoreType.DMA(())] * 2 + [_TOKEN],
        input_output_aliases={m: m for m in range(n_in)},
        compiler_params=_split_params(),
    )(*operands)
    return list(outs[:n]), (outs[n] if small_all is not None else None), (outs[n_in], outs[n_in + 1]), outs[n_in + 2]


def _gather_pass(name, gmats, gfulls, small_all, sems, after):
    k = len(gmats)
    n_buf = k + (1 if small_all is not None else 0)

    def body(*refs):
        bufs = refs[:n_buf]
        send_sem, recv_sem = refs[n_buf], refs[n_buf + 1]
        outs = refs[n_buf + 3:]
        fsend, frecv, token = outs[n_buf], outs[n_buf + 1], outs[n_buf + 2]
        x, y, c, others = _place()
        me_k = 2 * x + y
        sibling = (x, y, 1 - c)
        for m in range(k):
            for ox, oy in others:
                got = gmats[m].piece(bufs[m], 2 * ox + oy, c)
                _copy_to(got, got, send_sem, recv_sem, sibling).wait_recv()
        if small_all is not None:
            for ox, oy in others:
                got = bufs[k].at[2 * ox + oy]
                _copy_to(got, got, send_sem, recv_sem, sibling).wait_recv()
        for m in range(k):
            mine = gmats[m].piece(bufs[m], me_k, c)
            for _ in others:
                _copy_to(mine, mine, send_sem, recv_sem, sibling).wait_send()
        if small_all is not None:
            for _ in others:
                _copy_to(bufs[k].at[me_k], bufs[k].at[me_k], send_sem, recv_sem, sibling).wait_send()
        for m in range(k):
            for ox, oy in others:
                got = gmats[m].piece(bufs[m], 2 * ox + oy, c)
                _copy_to(got, got, fsend, frecv, sibling).start()
        token[...] = jnp.zeros_like(token)

    operands = [_in_hbm(f) for f in gfulls] + ([_in_hbm(small_all)] if small_all is not None else [])
    outs = pl.pallas_call(
        body, name=name,
        in_specs=[_HBM] * n_buf + [_SEM, _SEM, _ANY],
        out_specs=[_HBM] * n_buf + [_SEM, _SEM, _TOKEN_SPEC],
        out_shape=[pltpu.HBM(a.shape, a.dtype) for a in operands] + [pltpu.SemaphoreType.DMA(())] * 2 + [_TOKEN],
        input_output_aliases={i: i for i in range(n_buf)},
        compiler_params=_split_params(),
    )(*operands, sems[0], sems[1], after)
    return list(outs[:n_buf]), (outs[n_buf], outs[n_buf + 1]), outs[n_buf + 2]


def _gather_done(name, gmats, gfulls, sems, after):
    k = len(gmats)

    def body(*refs):
        bufs = refs[:k]
        send_sem, recv_sem = refs[k], refs[k + 1]
        x, y, c, others = _place()
        sibling = (x, y, 1 - c)
        for m in range(k):
            for ox, oy in others:
                got = gmats[m].piece(bufs[m], 2 * ox + oy, 1 - c)
                _copy_to(got, got, send_sem, recv_sem, sibling).wait_recv()
        for m in range(k):
            for ox, oy in others:
                sent = gmats[m].piece(bufs[m], 2 * ox + oy, c)
                _copy_to(sent, sent, send_sem, recv_sem, sibling).wait_send()

    outs = pl.pallas_call(
        body, name=name,
        in_specs=[_HBM] * k + [_SEM, _SEM, _ANY], out_specs=[_HBM] * k,
        out_shape=[pltpu.HBM(a.shape, a.dtype) for a in gfulls],
        input_output_aliases={i: i for i in range(k)},
        compiler_params=_split_params(),
    )(*[_in_hbm(f) for f in gfulls], sems[0], sems[1], after)
    return list(outs)


_FLIPS = [(fx, fy, fc) for fx in (0, 1) for fy in (0, 1) for fc in (0, 1) if (fx, fy, fc) != (0, 0, 0)]


def _small_start(packed, after):
    def body(small_ref, after_ref, small_thru, land_ref, send_sem, recv_sem, token):
        x, y, c, _ = _place()
        me = 4 * x + 2 * y + c
        for fx, fy, fc in _FLIPS:
            _copy_to(small_ref, land_ref.at[me], send_sem, recv_sem, (x ^ fx, y ^ fy, c ^ fc)).start()
        token[...] = jnp.zeros_like(token)

    outs = pl.pallas_call(
        body, name="small_grads_start",
        in_specs=[_HBM, _ANY], out_specs=[_HBM, _HBM, _SEM, _SEM, _TOKEN_SPEC],
        out_shape=[pltpu.HBM(packed.shape, F32), pltpu.HBM((N_DEV,) + packed.shape, F32)]
        + [pltpu.SemaphoreType.DMA(())] * 2 + [_TOKEN],
        input_output_aliases={0: 0},
        compiler_params=_split_params(),
    )(_in_hbm(packed), after)
    return outs[0], outs[1], (outs[2], outs[3]), outs[4]


def _small_wait(packed, landed, sems, after):
    def body(small_ref, land_ref, send_sem, recv_sem, after_ref, small_thru, land_thru):
        x, y, c, _ = _place()
        for fx, fy, fc in _FLIPS:
            got = land_ref.at[4 * (x ^ fx) + 2 * (y ^ fy) + (c ^ fc)]
            _copy_to(got, got, send_sem, recv_sem, (x, y, 1 - c)).wait_recv()
        for _ in _FLIPS:
            _copy_to(small_ref, small_ref, send_sem, recv_sem, (x, y, 1 - c)).wait_send()

    outs = pl.pallas_call(
        body, name="small_grads_wait",
        in_specs=[_HBM, _HBM, _SEM, _SEM, _ANY], out_specs=[_HBM, _HBM],
        out_shape=[pltpu.HBM(packed.shape, F32), pltpu.HBM(landed.shape, F32)],
        input_output_aliases={0: 0, 1: 1},
        compiler_params=_split_params(),
    )(_in_hbm(packed), _in_hbm(landed), sems[0], sems[1], after)
    return outs[0], outs[1]


def _exchange_start(name, mats, grads):
    n = len(mats)

    def body(*refs):
        g_refs = refs[:n]
        outs = refs[n:]
        land_refs = outs[n:2 * n]
        send_sem, recv_sem, token = outs[2 * n], outs[2 * n + 1], outs[2 * n + 2]
        x, y, c, _ = _place()
        for m in range(n):
            for k in range(N_CHIPS):
                _copy_to(mats[m].piece(g_refs[m], k, 1 - c), land_refs[m].at[k], send_sem, recv_sem, (x, y, 1 - c)).start()
        token[...] = jnp.zeros_like(token)

    outs = pl.pallas_call(
        body, name=name,
        in_specs=[_HBM] * n,
        out_specs=[_HBM] * (2 * n) + [_SEM, _SEM, _TOKEN_SPEC],
        out_shape=[pltpu.HBM(mt.full_shape, BF16) for mt in mats]
        + [pltpu.HBM((N_CHIPS, mt.pr, mt.pc), BF16) for mt in mats] + [pltpu.SemaphoreType.DMA(())] * 2 + [_TOKEN],
        input_output_aliases={m: m for m in range(n)},
        compiler_params=_split_params(),
    )(*[_in_hbm(g) for g in grads])
    return list(outs[:n]), list(outs[n:2 * n]), (outs[2 * n], outs[2 * n + 1]), outs[2 * n + 2]


def _exchange_wait(name, mats, grads, landed, sems, after):
    n = len(mats)

    def body(*refs):
        g_refs, land_refs = refs[:n], refs[n:2 * n]
        send_sem, recv_sem = refs[2 * n], refs[2 * n + 1]
        x, y, c, _ = _place()
        for m in range(n):
            for k in range(N_CHIPS):
                got = land_refs[m].at[k]
                _copy_to(got, got, send_sem, recv_sem, (x, y, 1 - c)).wait_recv()
        for m in range(n):
            for k in range(N_CHIPS):
                sent = mats[m].piece(g_refs[m], k, 1 - c)
                _copy_to(sent, sent, send_sem, recv_sem, (x, y, 1 - c)).wait_send()

    outs = pl.pallas_call(
        body, name=name,
        in_specs=[_HBM] * (2 * n) + [_SEM, _SEM, _ANY], out_specs=[_HBM] * (2 * n),
        out_shape=[pltpu.HBM(a.shape, a.dtype) for a in list(grads) + list(landed)],
        input_output_aliases={i: i for i in range(2 * n)},
        compiler_params=_split_params(),
    )(*[_in_hbm(a) for a in list(grads) + list(landed)], sems[0], sems[1], after)
    return list(outs[:n]), list(outs[n:])


def _add_halves(name, mat, grad, landed, core):
    tr = _pick(mat.pr, (1024, 704, 512, 352, 256, 128, 64, 32, 16))
    per = mat.pr // tr

    def body(core_ref, g_ref, l_ref, o_ref):
        o_ref[...] = (g_ref[...].astype(F32) + l_ref[...].astype(F32)).astype(BF16)

    if mat.kind == "col":
        g_spec = pl.BlockSpec((tr, mat.pc), lambda k, r, core_ref: (core_ref[0] * per + r, k))
    else:
        g_spec = pl.BlockSpec((tr, mat.pc), lambda k, r, core_ref: ((2 * k + core_ref[0]) * per + r, 0))
    p_spec = pl.BlockSpec((None, tr, mat.pc), lambda k, r, core_ref: (k, r, 0))
    return pl.pallas_call(
        body, name=name,
        grid_spec=pltpu.PrefetchScalarGridSpec(num_scalar_prefetch=1, grid=(N_CHIPS, per),
                                               in_specs=[g_spec, p_spec], out_specs=p_spec),
        out_shape=jax.ShapeDtypeStruct((N_CHIPS, mat.pr, mat.pc), BF16),
        compiler_params=_cparams("parallel", "parallel"),
    )(core, grad, landed)


def _scatter_start(name, mats, partials):
    n = len(mats)

    def body(*refs):
        p_refs = refs[:n]
        outs = refs[n:]
        land_refs = outs[n:2 * n]
        send_sem, recv_sem, token = outs[2 * n], outs[2 * n + 1], outs[2 * n + 2]
        x, y, c, others = _place()
        me_k = 2 * x + y
        for m in range(n):
            for ox, oy in others:
                _copy_to(p_refs[m].at[2 * ox + oy], land_refs[m].at[me_k], send_sem, recv_sem, (ox, oy, c)).start()
        token[...] = jnp.zeros_like(token)

    piece_shapes = [pltpu.HBM((N_CHIPS, mt.pr, mt.pc), BF16) for mt in mats]
    outs = pl.pallas_call(
        body, name=name,
        in_specs=[_HBM] * n,
        out_specs=[_HBM] * (2 * n) + [_SEM, _SEM, _TOKEN_SPEC],
        out_shape=piece_shapes + piece_shapes + [pltpu.SemaphoreType.DMA(())] * 2 + [_TOKEN],
        input_output_aliases={m: m for m in range(n)},
        compiler_params=_split_params(),
    )(*[_in_hbm(p) for p in partials])
    return list(outs[:n]), list(outs[n:2 * n]), (outs[2 * n], outs[2 * n + 1]), outs[2 * n + 2]


def _scatter_wait(name, mats, partials, landed, sems, after):
    n = len(mats)

    def body(*refs):
        p_refs, land_refs = refs[:n], refs[n:2 * n]
        send_sem, recv_sem = refs[2 * n], refs[2 * n + 1]
        x, y, c, others = _place()
        for m in range(n):
            for ox, oy in others:
                got = land_refs[m].at[2 * ox + oy]
                _copy_to(got, got, send_sem, recv_sem, (ox, oy, c)).wait_recv()
        for m in range(n):
            for ox, oy in others:
                sent = p_refs[m].at[2 * ox + oy]
                _copy_to(sent, sent, send_sem, recv_sem, (ox, oy, c)).wait_send()

    outs = pl.pallas_call(
        body, name=name,
        in_specs=[_HBM] * (2 * n) + [_SEM, _SEM, _ANY], out_specs=[_HBM] * (2 * n),
        out_shape=[pltpu.HBM(a.shape, a.dtype) for a in list(partials) + list(landed)],
        input_output_aliases={i: i for i in range(2 * n)},
        compiler_params=_split_params(),
    )(*[_in_hbm(a) for a in list(partials) + list(landed)], sems[0], sems[1], after)
    return list(outs[:n]), list(outs[n:])


def _sum_chips(name, mat, partial, landed, slots, layer=None, stack=None, n_layers=1):
    tr = _pick(mat.pr, (1024, 704, 512, 352, 256, 128, 64, 32, 16))
    per = mat.pr // tr

    def body(slots_ref, own_ref, a_ref, b_ref, c_ref, *rest):
        o_ref = rest[-1]
        o_ref[...] = ((own_ref[...].astype(F32) + a_ref[...].astype(F32)) + b_ref[...].astype(F32)) + c_ref[...].astype(F32)

    def slot_spec(which):
        return pl.BlockSpec((None, tr, mat.pc), lambda r, slots_ref: (slots_ref[which], r, 0))

    in_specs = [slot_spec(0), slot_spec(1), slot_spec(2), slot_spec(3)]
    operands = [slots, partial, landed, landed, landed]
    aliases = {}
    if layer is None:
        o_spec = pl.BlockSpec((tr, mat.pc), lambda r, slots_ref: (slots_ref[4] * per + r, 0))
        out_shape = jax.ShapeDtypeStruct((mat.sr, mat.sc), F32)
    else:
        o_spec = pl.BlockSpec((None, tr, mat.pc), lambda r, slots_ref: (layer, slots_ref[4] * per + r, 0))
        out_shape = jax.ShapeDtypeStruct((n_layers, mat.sr, mat.sc), F32)
        if stack is not None:
            in_specs.append(_ANY)
            operands.append(stack)
            aliases = {len(operands) - 1: 0}
    return pl.pallas_call(
        body, name=name,
        grid_spec=pltpu.PrefetchScalarGridSpec(num_scalar_prefetch=1, grid=(per,), in_specs=in_specs, out_specs=o_spec),
        out_shape=out_shape, input_output_aliases=aliases,
        compiler_params=_cparams("parallel"),
    )(*operands)


def _share_pieces(name, mats, shards, groups):
    n = len(mats)
    n_out = len(groups)

    def body(*refs):
        out_refs = refs[n_out:2 * n_out]
        send_sems, recv_sems = refs[2 * n_out:]
        x, y, c, _ = _place()
        sibling = (x, y, 1 - c)
        sent, waits = [], []
        for o, members in enumerate(groups):
            for l, m in enumerate(members):
                dst = out_refs[o].at[l] if len(members) > 1 else out_refs[o]
                mine = mats[m].half(dst, c)
                sent.append(pltpu.make_async_remote_copy(src_ref=mine, dst_ref=mine, send_sem=send_sems.at[m],
                                                         recv_sem=recv_sems.at[m], device_id=sibling, device_id_type=MESH_ID))
                theirs = mats[m].half(dst, 1 - c)
                waits.append(pltpu.make_async_remote_copy(src_ref=theirs, dst_ref=theirs, send_sem=send_sems.at[m],
                                                          recv_sem=recv_sems.at[m], device_id=sibling, device_id_type=MESH_ID))
        for cp in sent:
            cp.start()
        for cp in waits:
            cp.wait_recv()
        for cp in sent:
            cp.wait_send()

    return pl.pallas_call(
        body, name=name,
        in_specs=[_ANY] * n_out, out_specs=[_ANY] * n_out,
        out_shape=[jax.ShapeDtypeStruct(s.shape, F32) for s in shards],
        input_output_aliases={o: o for o in range(n_out)},
        scratch_shapes=[pltpu.SemaphoreType.DMA((n,)), pltpu.SemaphoreType.DMA((n,))],
    )(*shards)


def _share_start(name, mats, shards, items):
    n_out = len(shards)

    def body(*refs):
        out_refs = refs[n_out:2 * n_out]
        send_sem, recv_sem, token = refs[2 * n_out], refs[2 * n_out + 1], refs[2 * n_out + 2]
        x, y, c, _ = _place()
        for o, members in enumerate(items):
            for layer, m in members:
                dst = out_refs[o] if layer is None else out_refs[o].at[layer]
                mine = mats[m].half(dst, c)
                _copy_to(mine, mine, send_sem, recv_sem, (x, y, 1 - c)).start()
        token[...] = jnp.zeros_like(token)

    outs = pl.pallas_call(
        body, name=name,
        in_specs=[_HBM] * n_out, out_specs=[_HBM] * n_out + [_SEM, _SEM, _TOKEN_SPEC],
        out_shape=[pltpu.HBM(s.shape, F32) for s in shards] + [pltpu.SemaphoreType.DMA(())] * 2 + [_TOKEN],
        input_output_aliases={o: o for o in range(n_out)},
        compiler_params=_split_params(),
    )(*[_in_hbm(s) for s in shards])
    return list(outs[:n_out]), (outs[n_out], outs[n_out + 1]), outs[n_out + 2]


def _share_wait(name, mats, shards, items, sems, after):
    n_out = len(shards)

    def body(*refs):
        bufs = refs[:n_out]
        send_sem, recv_sem = refs[n_out], refs[n_out + 1]
        x, y, c, _ = _place()
        for o, members in enumerate(items):
            for layer, m in members:
                dst = bufs[o] if layer is None else bufs[o].at[layer]
                theirs = mats[m].half(dst, 1 - c)
                _copy_to(theirs, theirs, send_sem, recv_sem, (x, y, 1 - c)).wait_recv()
        for o, members in enumerate(items):
            for layer, m in members:
                dst = bufs[o] if layer is None else bufs[o].at[layer]
                mine = mats[m].half(dst, c)
                _copy_to(mine, mine, send_sem, recv_sem, (x, y, 1 - c)).wait_send()

    outs = pl.pallas_call(
        body, name=name,
        in_specs=[_HBM] * n_out + [_SEM, _SEM, _ANY], out_specs=[_HBM] * n_out,
        out_shape=[pltpu.HBM(s.shape, F32) for s in shards],
        input_output_aliases={o: o for o in range(n_out)},
        compiler_params=_split_params(),
    )(*[_in_hbm(s) for s in shards], sems[0], sems[1], after)
    return list(outs)


def _sum_devices(stacked):
    nd, r, c = stacked.shape

    def body(s_ref, o_ref):
        s = s_ref[0]
        for k in range(1, nd):
            s = s + s_ref[k]
        o_ref[...] = s

    return pl.pallas_call(
        body, name="sum_small_grads", grid=(1,),
        in_specs=[pl.BlockSpec((nd, r, c), lambda i: (0, 0, 0))],
        out_specs=pl.BlockSpec((r, c), lambda i: (0, 0)),
        out_shape=jax.ShapeDtypeStruct((r, c), F32),
        compiler_params=_cparams("arbitrary"),
    )(stacked)


def _pack(arrs):
    flat = jnp.concatenate([a.reshape(-1) for a in arrs])
    rows = -(-flat.shape[0] // (8 * LANES)) * 8
    return jnp.pad(flat, (0, rows * LANES - flat.shape[0])).reshape(rows, LANES)


def _unpack(packed, shapes):
    flat = packed.reshape(-1)
    out, at = [], 0
    for s in shapes:
        size = 1
        for dim in s:
            size *= dim
        out.append(flat[at:at + size].reshape(s))
        at += size
    return out


def kernel(x, mix_norm_e, w_in_e, conv_w_e, conv_b_e, ln_g_e, ln_b_e, w_pool_e, pool_scale_e, w_out_e, mix_norm_o, w_in_o, conv_w_o, w_out_o, ffn_norm, w_gate, w_up, w_down, final_norm, loss_target, m_mix_norm_e, m_w_in_e, m_conv_w_e, m_conv_b_e, m_ln_g_e, m_ln_b_e, m_w_pool_e, m_pool_scale_e, m_w_out_e, m_mix_norm_o, m_w_in_o, m_conv_w_o, m_w_out_o, m_ffn_norm, m_w_gate, m_w_up, m_w_down, m_final_norm, v_mix_norm_e, v_w_in_e, v_conv_w_e, v_conv_b_e, v_ln_g_e, v_ln_b_e, v_w_pool_e, v_pool_scale_e, v_w_out_e, v_mix_norm_o, v_w_in_o, v_conv_w_o, v_w_out_o, v_ffn_norm, v_w_gate, v_w_up, v_w_down, v_final_norm):
    bsz, seq_len, d = x.shape
    t = bsz * seq_len
    depth = ffn_norm.shape[0]
    assert depth == 2 and conv_b_e.shape[1] == pool_scale_e.shape[1]
    ts = _pick(seq_len, (256, 128, 64, 32))
    me_k = 2 * lax.axis_index("x") + lax.axis_index("y")
    core = lax.axis_index("c").astype(jnp.int32).reshape(1)

    mat_src = [("col", w_in_e, 0), ("row", w_out_e, 0), ("col", w_gate, 0), ("col", w_up, 0), ("row", w_down, 0),
               ("col", w_in_o, 0), ("row", w_out_o, 0), ("col", w_gate, 1), ("col", w_up, 1), ("row", w_down, 1)]
    mats = [_Mat(kind, w.shape[1:]) for kind, w, _ in mat_src]
    n_pool = w_pool_e.shape[1]
    pool_mats = tuple(range(len(mats), len(mats) + n_pool))
    mats = mats + [_Mat("row", w_pool_e.shape[2:])] * n_pool
    chip = me_k.astype(jnp.int32).reshape(1)
    small_shards = [conv_w_e[0], w_pool_e[0], mix_norm_o, conv_w_o[0]]
    packed_small = _pack(small_shards)

    chain = [()]

    def seq(fn, *args, **kw):
        out = fn(*args, after=chain[0], **kw)
        chain[0] = (out[0] if isinstance(out, (list, tuple)) else out,)
        return out

    def mm(*args, **kw):
        return seq(_mm, *args, **kw)

    gather_groups = [(0,), (1,), (2, 3), (4,), (5, 6), (7, 8), (9,)]
    fulls, gather_sems, small_all = [None] * len(mats), [], None
    for g, ms in enumerate(gather_groups):
        own16 = [seq(_cast_into_full, "cast_w%d" % m, mats[m], mat_src[m][1], mat_src[m][2], chip) for m in ms]
        sent, landing, sems, token = _gather_start("gather_start%d" % g, [mats[m] for m in ms], own16,
                                                   _place_small(packed_small, chip) if g == 0 else None)
        chain[0] = (token,)
        for m, f in zip(ms, sent):
            fulls[m] = f
        gather_sems.append(sems)
        if g == 0:
            small_all = landing

    passed = {}

    def gather_pass(g):
        ms = gather_groups[g]
        bufs, pass_sems, token = _gather_pass("gather_pass%d" % g, [mats[m] for m in ms], [fulls[m] for m in ms],
                                              small_all if g == 0 else None, gather_sems[g], chain[0][0])
        chain[0] = (token,)
        passed[g] = (bufs, pass_sems)

    def gather_done(g):
        ms = gather_groups[g]
        bufs, pass_sems = passed[g]
        done = _gather_done("gather_done%d" % g, [mats[m] for m in ms], bufs[:len(ms)], pass_sems, chain[0][0])
        chain[0] = (done[0],)
        return done + bufs[len(ms):]

    h0 = x.reshape(t, d)
    target = loss_target.reshape(t, d)
    gather_pass(0)
    n1 = seq(_rms_fwd, "mix0_norm", h0, mix_norm_e)
    W_in_e, small_all = gather_done(0)
    per_chip = [_unpack(small_all[k], [s.shape for s in small_shards]) for k in range(N_CHIPS)]
    conv_w_e_f = jnp.concatenate([p[0] for p in per_chip], axis=1)
    w_pool_f = jnp.concatenate([p[1] for p in per_chip], axis=1)
    mix_norm_o_f = jnp.concatenate([p[2] for p in per_chip], axis=1)
    conv_w_o_f = jnp.concatenate([p[3] for p in per_chip], axis=1)
    W_gate, W_up, W_down = [None, None], [None, None], [None, None]

    (u_e,) = mm("mix0_in", [(n1, W_in_e)], "nn", [F32], _ep_store, tm=1024, tn=1024, tk=2048)
    gather_pass(1)
    a2, cat = seq(_mixer_e_fwd, u_e, conv_w_e_f, conv_b_e, ln_g_e, ln_b_e, w_pool_f, pool_scale_e, seq_len, ts)
    (W_out_e,) = gather_done(1)
    (h1,) = mm("mix0_out", [(cat, W_out_e)], "nn", [F32], _ep_residual, extras=(h0,), tm=1024, tn=1024, tk=2048)
    gather_pass(2)
    n2 = seq(_rms_fwd, "ffn0_norm", h1, ffn_norm[0:1])
    W_gate[0], W_up[0] = gather_done(2)
    gt0, up0, act0 = mm("ffn0_gate_up", [(n2, W_gate[0]), (n2, W_up[0])], "nn", [BF16] * 3, _ep_swiglu,
                        acc_of=(0, 1), tm=1024, tn=512, tk=2048)
    gather_pass(3)
    (W_down[0],) = gather_done(3)
    gather_pass(4)
    (h2,) = mm("ffn0_down", [(act0, W_down[0])], "nn", [F32], _ep_residual, extras=(h1,), tm=512, tn=1024, tk=5632)
    n3 = seq(_rms_fwd, "mix1_norm", h2, mix_norm_o_f)
    W_in_o, W_out_o = gather_done(4)
    (u_o,) = mm("mix1_in", [(n3, W_in_o)], "nn", [F32], _ep_store, tm=1024, tn=1024, tk=2048)
    gather_pass(5)
    y_o = seq(_mixer_o_fwd, u_o, conv_w_o_f, seq_len, ts)
    (h3,) = mm("mix1_out", [(y_o, W_out_o)], "nn", [F32], _ep_residual, extras=(h2,), tm=1024, tn=1024, tk=2048)
    gather_pass(6)
    n4 = seq(_rms_fwd, "ffn1_norm", h3, ffn_norm[1:2])
    W_gate[1], W_up[1] = gather_done(5)
    gt1, up1, act1 = mm("ffn1_gate_up", [(n4, W_gate[1]), (n4, W_up[1])], "nn", [BF16] * 3, _ep_swiglu,
                        acc_of=(0, 1), tm=1024, tn=512, tk=2048)
    (W_down[1],) = gather_done(6)
    (h4,) = mm("ffn1_down", [(act1, W_down[1])], "nn", [F32], _ep_residual, extras=(h3,), tm=512, tn=1024, tk=5632)
    loss_part, dh4, dh4b, d_final_norm = seq(_loss_head, h4, final_norm.reshape(1, d), target)
    loss = lax.psum(loss_part[0, 0], AXES)

    in_flight = {}
    partials, scattered = [None] * len(mats), [None] * len(mats)

    def reduce_begin(tag, ms, grads):
        gm = [mats[m] for m in ms]
        grads, landed, sems, token = _exchange_start("exchange_start_" + tag, gm, grads)
        chain[0] = (token,)
        in_flight[tag] = (ms, gm, grads, landed, sems)

    def reduce_advance(tag):
        ms, gm, grads, landed, sems = in_flight[tag]
        grads, landed = _exchange_wait("exchange_wait_" + tag, gm, grads, landed, sems, chain[0][0])
        parts = [_add_halves("add_halves%d" % m, mats[m], g, l, core) for m, g, l in zip(ms, grads, landed)]
        parts, lands, sems, token = _scatter_start("scatter_start_" + tag, gm, parts)
        chain[0] = (token,)
        in_flight[tag] = (ms, gm, parts, lands, sems)

    def reduce_finish(tag):
        ms, gm, parts, lands, sems = in_flight[tag]
        parts, lands = _scatter_wait("scatter_wait_" + tag, gm, parts, lands, sems, chain[0][0])
        chain[0] = (lands[0],)
        for m, p, l in zip(ms, parts, lands):
            partials[m], scattered[m] = p, l

    xi, yi, ci = lax.axis_index("x"), lax.axis_index("y"), lax.axis_index("c")
    slots = jnp.stack([me_k, 2 * (1 - xi) + yi, 2 * xi + (1 - yi), 2 * (1 - xi) + (1 - yi), ci]).astype(jnp.int32)
    shards = {}

    def sum_into(name, m, layer=None, n_layers=1):
        if layer is None:
            shards[name] = _sum_chips("sum_chips%d" % m, mats[m], partials[m], scattered[m], slots)
        else:
            shards[name] = _sum_chips("sum_chips%d" % m, mats[m], partials[m], scattered[m], slots, layer=layer,
                                      stack=shards.get(name), n_layers=n_layers)

    sharing = {}

    def share_begin(tag, names, items):
        arrays, sems, token = _share_start("share_start_" + tag, mats, [shards[nm] for nm in names], items)
        chain[0] = (token,)
        sharing[tag] = (names, items, arrays, sems)

    def share_end(tag):
        names, items, arrays, sems = sharing[tag]
        arrays = _share_wait("share_wait_" + tag, mats, arrays, items, sems, chain[0][0])
        chain[0] = (arrays[0],)
        for nm, a in zip(names, arrays):
            shards[nm] = a

    def ffn_bwd(l, dhb, n, gt, up, act, mid=None):
        dgt, dup = mm("ffn%d_dact" % l, [(dhb, W_down[l])], "nt", [BF16, BF16], _ep_swiglu_bwd, extras=(gt, up),
                      tm=1024, tn=512, tk=2048, row_chunk=256)
        (dW_down,) = mm("ffn%d_dw_down" % l, [(act, dhb)], "tn", [BF16], _ep_store, tm=512, tn=1024, tk=4096)
        if mid is not None:
            mid()
        (dn,) = mm("ffn%d_dn" % l, [(dgt, W_gate[l]), (dup, W_up[l])], "nt", [F32], _ep_store,
                   tm=1024, tn=1024, tk=1408)
        (dW_gate,) = mm("ffn%d_dw_gate" % l, [(n, dgt)], "tn", [BF16], _ep_store, tm=1024, tn=512, tk=4096)
        (dW_up,) = mm("ffn%d_dw_up" % l, [(n, dup)], "tn", [BF16], _ep_store, tm=1024, tn=512, tk=4096)
        return dn, dW_gate, dW_up, dW_down

    dn4, dW_gate1, dW_up1, dW_down1 = ffn_bwd(1, dh4b, n4, gt1, up1, act1)
    reduce_begin("ffn1", (7, 8, 9), [dW_gate1, dW_up1, dW_down1])
    dh3, dh3b, d_ffn_norm1 = seq(_rms_bwd, "ffn1_norm_bwd", dn4, h3, ffn_norm[1:2], dh4)
    (dy_o,) = mm("mix1_dy", [(dh3b, W_out_o)], "nt", [F32], _ep_store, tm=1024, tn=1024, tk=2048)
    reduce_advance("ffn1")
    (dW_out_o,) = mm("mix1_dw_out", [(y_o, dh3b)], "tn", [BF16], _ep_store, tm=1024, tn=1024, tk=4096)
    du_o, d_conv_w_o = seq(_mixer_o_bwd, dy_o, u_o, conv_w_o_f, seq_len, ts)
    (dW_in_o,) = mm("mix1_dw_in", [(n3, du_o)], "tn", [BF16], _ep_store, tm=1024, tn=1024, tk=4096)
    reduce_begin("mix1", (5, 6), [dW_in_o, dW_out_o])
    (dn3,) = mm("mix1_dn", [(du_o, W_in_o)], "nt", [F32], _ep_store, tm=1024, tn=1024, tk=3072)
    reduce_advance("mix1")
    dh2, dh2b, d_mix_norm_o = seq(_rms_bwd, "mix1_norm_bwd", dn3, h2, mix_norm_o_f, dh3)

    def finish_layer1():
        reduce_finish("ffn1")
        reduce_finish("mix1")
        sum_into("w_in_o", 5)
        sum_into("w_out_o", 6)
        for nm, m in (("w_gate", 7), ("w_up", 8), ("w_down", 9)):
            sum_into(nm, m, layer=1, n_layers=2)
        share_begin("layer1", ["w_in_o", "w_out_o", "w_gate", "w_up", "w_down"],
                    [[(None, 5)], [(None, 6)], [(1, 7)], [(1, 8)], [(1, 9)]])

    dn2, dW_gate0, dW_up0, dW_down0 = ffn_bwd(0, dh2b, n2, gt0, up0, act0, mid=finish_layer1)
    reduce_begin("ffn0", (2, 3, 4), [dW_gate0, dW_up0, dW_down0])
    dh1, dh1b, d_ffn_norm0 = seq(_rms_bwd, "ffn0_norm_bwd", dn2, h1, ffn_norm[0:1], dh2)
    (dcat,) = mm("mix0_dcat", [(dh1b, W_out_e)], "nt", [F32], _ep_store, tm=1024, tn=1024, tk=2048)
    reduce_advance("ffn0")
    (dW_out_e,) = mm("mix0_dw_out", [(cat, dh1b)], "tn", [BF16], _ep_store, tm=1024, tn=1024, tk=4096)
    da2, d_ln_g, d_ln_b, d_conv_b = seq(_mixer_e_bwd_norm, dcat, a2, ln_g_e, ln_b_e, ts)
    du_e, d_conv_w_e, d_w_pool, d_pool_scale = seq(_mixer_e_bwd_mix, da2, dcat, u_e, conv_w_e_f, w_pool_f, pool_scale_e,
                                                   seq_len, ts)
    (dW_in_e,) = mm("mix0_dw_in", [(n1, du_e)], "tn", [BF16], _ep_store, tm=1024, tn=1024, tk=4096)
    reduce_begin("mix0", (0, 1) + pool_mats, [dW_in_e, dW_out_e] + [d_w_pool[g].astype(BF16) for g in range(n_pool)])
    (dn1,) = mm("mix0_dn", [(du_e, W_in_e)], "nt", [F32], _ep_store, tm=1024, tn=1024, tk=3072)
    dx, _, d_mix_norm_e = seq(_rms_bwd, "mix0_norm_bwd", dn1, h0, mix_norm_e, dh1)
    reduce_advance("mix0")

    d_ffn_norm = jnp.concatenate([d_ffn_norm0, d_ffn_norm1], axis=0)
    small_partials = [d_mix_norm_e, d_conv_w_e, d_conv_b, d_ln_g, d_ln_b, d_pool_scale, d_mix_norm_o, d_conv_w_o,
                      d_ffn_norm, d_final_norm]
    packed_grads, small_stack, small_sems, token = _small_start(_pack(small_partials), chain[0][0])
    chain[0] = (token,)
    share_end("layer1")
    reduce_finish("ffn0")
    for nm, m in (("w_gate", 2), ("w_up", 3), ("w_down", 4)):
        sum_into(nm, m, layer=0, n_layers=2)
    share_begin("layer0", ["w_gate", "w_up", "w_down"], [[(0, 2)], [(0, 3)], [(0, 4)]])
    grad = {"w_in_o": shards["w_in_o"][None], "w_out_o": shards["w_out_o"][None]}
    weights = dict(mix_norm_e=mix_norm_e, w_in_e=w_in_e, conv_w_e=conv_w_e, conv_b_e=conv_b_e, ln_g_e=ln_g_e, ln_b_e=ln_b_e,
                   w_pool_e=w_pool_e, pool_scale_e=pool_scale_e, w_out_e=w_out_e, mix_norm_o=mix_norm_o, w_in_o=w_in_o,
                   conv_w_o=conv_w_o, w_out_o=w_out_o, ffn_norm=ffn_norm, w_gate=w_gate, w_up=w_up, w_down=w_down,
                   final_norm=final_norm)
    mom1 = dict(mix_norm_e=m_mix_norm_e, w_in_e=m_w_in_e, conv_w_e=m_conv_w_e, conv_b_e=m_conv_b_e, ln_g_e=m_ln_g_e,
                ln_b_e=m_ln_b_e, w_pool_e=m_w_pool_e, pool_scale_e=m_pool_scale_e, w_out_e=m_w_out_e, mix_norm_o=m_mix_norm_o,
                w_in_o=m_w_in_o, conv_w_o=m_conv_w_o, w_out_o=m_w_out_o, ffn_norm=m_ffn_norm, w_gate=m_w_gate, w_up=m_w_up,
                w_down=m_w_down, final_norm=m_final_norm)
    mom2 = dict(mix_norm_e=v_mix_norm_e, w_in_e=v_w_in_e, conv_w_e=v_conv_w_e, conv_b_e=v_conv_b_e, ln_g_e=v_ln_g_e,
                ln_b_e=v_ln_b_e, w_pool_e=v_w_pool_e, pool_scale_e=v_pool_scale_e, w_out_e=v_w_out_e, mix_norm_o=v_mix_norm_o,
                w_in_o=v_w_in_o, conv_w_o=v_conv_w_o, w_out_o=v_w_out_o, ffn_norm=v_ffn_norm, w_gate=v_w_gate, w_up=v_w_up,
                w_down=v_w_down, final_norm=v_final_norm)
    names = list(weights)

    big = ("w_in_o", "w_out_o", "w_gate", "w_up", "w_down", "w_in_e", "w_out_e")
    delta, new_m, new_v = {}, {}, {}

    def update(nm):
        shape = weights[nm].shape
        rows = 1
        for dim in shape[:-1]:
            rows *= dim
        as2d = lambda a: a.reshape(rows, shape[-1])
        dl, mn, vn = seq(_adamw, "adamw_" + nm, as2d(weights[nm]), as2d(grad[nm]), as2d(mom1[nm]), as2d(mom2[nm]))
        delta[nm], new_m[nm], new_v[nm] = dl.reshape(shape), mn.reshape(shape), vn.reshape(shape)

    for nm in big[:2]:
        update(nm)
    share_end("layer0")
    for nm in big[2:5]:
        grad[nm] = shards[nm]
        update(nm)
    reduce_finish("mix0")
    sum_into("w_in_e", 0)
    sum_into("w_out_e", 1)
    for layer, m in enumerate(pool_mats):
        sum_into("w_pool_e", m, layer=layer, n_layers=n_pool)
    g_w_in_e, g_w_out_e, g_w_pool = _share_pieces("share_pieces_first", mats,
                                                  [shards["w_in_e"], shards["w_out_e"], shards["w_pool_e"]],
                                                  [(0,), (1,), pool_mats])
    grad["w_in_e"], grad["w_out_e"], grad["w_pool_e"] = g_w_in_e[None], g_w_out_e[None], g_w_pool[None]
    for nm in big[5:]:
        update(nm)

    packed_grads, small_stack = _small_wait(packed_grads, small_stack, small_sems, chain[0][0])
    me_dev = 4 * xi + 2 * yi + ci
    small_stack = jnp.where(lax.broadcasted_iota(jnp.int32, (N_DEV, 1, 1), 0) == me_dev, packed_grads[None], small_stack)
    small_sum = _unpack(_sum_devices(small_stack), [s.shape for s in small_partials])
    (g_mix_norm_e, g_conv_w_e_f, g_conv_b, g_ln_g, g_ln_b, g_pool_scale, g_mix_norm_o_f, g_conv_w_o_f,
     g_ffn_norm, g_final_norm) = small_sum

    def my_shard(full, axis):
        size = full.shape[axis] // N_CHIPS
        return lax.dynamic_slice_in_dim(full, me_k * size, size, axis)

    grad.update({
        "mix_norm_e": g_mix_norm_e, "conv_w_e": my_shard(g_conv_w_e_f, 1)[None], "conv_b_e": g_conv_b,
        "ln_g_e": g_ln_g, "ln_b_e": g_ln_b, "pool_scale_e": g_pool_scale,
        "mix_norm_o": my_shard(g_mix_norm_o_f, 1), "conv_w_o": my_shard(g_conv_w_o_f, 1)[None],
        "ffn_norm": g_ffn_norm, "final_norm": g_final_norm.reshape(final_norm.shape),
    })
    small = [nm for nm in names if nm not in big]
    shapes = [weights[nm].shape for nm in small]
    dl, mn, vn = _adamw("adamw_small", _pack([weights[nm] for nm in small]), _pack([grad[nm] for nm in small]),
                        _pack([mom1[nm] for nm in small]), _pack([mom2[nm] for nm in small]))
    for nm, a, b, c_ in zip(small, _unpack(dl, shapes), _unpack(mn, shapes), _unpack(vn, shapes)):
        delta[nm], new_m[nm], new_v[nm] = a, b, c_

    grad_x = dx.reshape(bsz, seq_len, d)
    return (loss, grad_x, *[grad[nm] for nm in names], *[delta[nm] for nm in names],
            *[new_m[nm] for nm in names], *[new_v[nm] for nm in names])
```

```python
import jax
import jax.numpy as jnp
from jax import lax
from jax.experimental import pallas as pl
from jax.experimental.pallas import tpu as pltpu

F32 = jnp.float32
BF16 = jnp.bfloat16
MESH_ID = pl.DeviceIdType.MESH
AXES = ("x", "y", "c")
N_CHIPS = 4
N_DEV = 8

EPS = 1e-6
POOL_WINDOWS = (2, 4, 8, 16)
ADAM_LR, ADAM_B1, ADAM_B2, ADAM_EPS, ADAM_WD, ADAM_STEP = 0.001, 0.9, 0.999, 1e-08, 0.01, 10

LANES = 128
CONV_HALO = 32
POOL_HALO = 16
SHORT_HALO = 16
V7X_VMEM_LIMIT = 56 * 1024 * 1024


def _cparams(*sem):
    return pltpu.CompilerParams(dimension_semantics=sem if sem else None, vmem_limit_bytes=V7X_VMEM_LIMIT)


def _pick(dim, prefs):
    for p in prefs:
        if p <= dim and dim % p == 0:
            return p
    return dim


def _sigmoid(x):
    return jax.nn.sigmoid(x)


_ANY = pl.BlockSpec(memory_space=pl.ANY)


def _behind(after, body, n_in):
    if not after:
        return body
    skip = len(after)

    def body_behind(*refs):
        return body(*refs[:n_in], *refs[n_in + skip:])

    return body_behind


_DOT_DIMS = {
    "nn": (((1,), (0,)), ((), ())),
    "nt": (((1,), (1,)), ((), ())),
    "tn": (((0,), (0,)), ((), ())),
}


def _mm(name, pairs, mode, out_dtypes, epilogue, extras=(), acc_of=None, tm=512, tn=512, tk=2048, row_chunk=0, after=()):
    a0, b0 = pairs[0]
    if mode == "nn":
        (m, k), n = a0.shape, b0.shape[1]
    elif mode == "nt":
        (m, k), n = a0.shape, b0.shape[0]
    else:
        (k, m), n = a0.shape, b0.shape[1]
    tm = _pick(m, (tm, 512, 256, 128, 64, 32, 16, 8))
    tn = _pick(n, (tn, 512, 256, 128))
    tk = _pick(k, (tk, 2048, 1024, 512, 256, 128))
    nk = k // tk
    n_pairs = len(pairs)
    acc_of = tuple(acc_of) if acc_of is not None else (0,) * n_pairs
    n_acc = max(acc_of) + 1
    n_ex, n_out = len(extras), len(out_dtypes)
    dims = _DOT_DIMS[mode]

    def body(*refs):
        a_refs = refs[:n_pairs]
        b_refs = refs[n_pairs:2 * n_pairs]
        e_refs = refs[2 * n_pairs:2 * n_pairs + n_ex]
        first_out = 2 * n_pairs + n_ex + len(after)
        o_refs = refs[first_out:first_out + n_out]
        acc_refs = refs[first_out + n_out:]

        def partial_sums(rows=None):
            sums = [None] * n_acc
            for p in range(n_pairs):
                a = a_refs[p][...] if rows is None else (a_refs[p][:, rows] if mode == "tn" else a_refs[p][rows, :])
                d = lax.dot_general(a, b_refs[p][...], dims, preferred_element_type=F32)
                sums[acc_of[p]] = d if sums[acc_of[p]] is None else sums[acc_of[p]] + d
            return sums

        if nk == 1 and row_chunk:
            for r0 in range(0, tm, row_chunk):
                rows = pl.ds(r0, row_chunk)
                epilogue(partial_sums(rows), [e.at[rows, :] for e in e_refs], [o.at[rows, :] for o in o_refs])
            return
        if nk == 1:
            epilogue(partial_sums(), e_refs, o_refs)
            return
        kk = pl.program_id(2)

        @pl.when(kk == 0)
        def _():
            for acc in acc_refs:
                acc[...] = jnp.zeros_like(acc)

        for acc, s in zip(acc_refs, partial_sums()):
            acc[...] += s

        @pl.when(kk == nk - 1)
        def _():
            epilogue([acc[...] for acc in acc_refs], e_refs, o_refs)

    if mode == "nn":
        a_spec = pl.BlockSpec((tm, tk), lambda i, j, kk: (i, kk))
        b_spec = pl.BlockSpec((tk, tn), lambda i, j, kk: (kk, j))
    elif mode == "nt":
        a_spec = pl.BlockSpec((tm, tk), lambda i, j, kk: (i, kk))
        b_spec = pl.BlockSpec((tn, tk), lambda i, j, kk: (j, kk))
    else:
        a_spec = pl.BlockSpec((tk, tm), lambda i, j, kk: (kk, i))
        b_spec = pl.BlockSpec((tk, tn), lambda i, j, kk: (kk, j))
    o_spec = pl.BlockSpec((tm, tn), lambda i, j, kk: (i, j))
    outs = pl.pallas_call(
        body,
        name=name,
        grid=(m // tm, n // tn, nk),
        in_specs=[a_spec] * n_pairs + [b_spec] * n_pairs + [o_spec] * n_ex
        + [pl.BlockSpec(memory_space=pl.ANY)] * len(after),
        out_specs=[o_spec] * n_out,
        out_shape=[jax.ShapeDtypeStruct((m, n), dt) for dt in out_dtypes],
        scratch_shapes=[pltpu.VMEM((tm, tn), F32) for _ in range(n_acc)] if nk > 1 else [],
        compiler_params=_cparams("parallel", "parallel", "arbitrary"),
    )(*[p[0] for p in pairs], *[p[1] for p in pairs], *extras, *after)
    return outs


def _ep_store(accs, ex, outs):
    outs[0][...] = accs[0].astype(outs[0].dtype)


def _ep_residual(accs, ex, outs):
    outs[0][...] = ex[0][...] + accs[0]


def _ep_swiglu(accs, ex, outs):
    g, u = accs
    s = _sigmoid(g)
    gs = g * s
    outs[0][...] = (u * (s * (1.0 + g * (1.0 - s)))).astype(BF16)
    outs[1][...] = gs.astype(BF16)
    outs[2][...] = (gs * u).astype(BF16)


def _ep_swiglu_bwd(accs, ex, outs):
    d = accs[0]
    outs[0][...] = (d * ex[0][...].astype(F32)).astype(BF16)
    outs[1][...] = (d * ex[1][...].astype(F32)).astype(BF16)


def _rms_fwd(name, h, g, after=()):
    t, d = h.shape
    tr = _pick(t, (256, 128, 64, 32, 16, 8))

    def body(h_ref, g_ref, *rest):
        o_ref = rest[-1]
        x = h_ref[...]
        r = lax.rsqrt(jnp.mean(x * x, axis=-1, keepdims=True) + EPS)
        o_ref[...] = (x * r * g_ref[...]).astype(BF16)

    return pl.pallas_call(
        body, name=name, grid=(t // tr,),
        in_specs=[pl.BlockSpec((tr, d), lambda i: (i, 0)), pl.BlockSpec((1, d), lambda i: (0, 0))]
        + [pl.BlockSpec(memory_space=pl.ANY)] * len(after),
        out_specs=pl.BlockSpec((tr, d), lambda i: (i, 0)),
        out_shape=jax.ShapeDtypeStruct((t, d), BF16),
        compiler_params=_cparams("parallel"),
    )(h, g, *after)


def _rms_bwd(name, dn, h, g, dres, after=()):
    t, d = h.shape
    tr = _pick(t, (256, 128, 64, 32, 16, 8))

    def body(dn_ref, h_ref, g_ref, dres_ref, dh_ref, dhb_ref, dg_ref):
        x = h_ref[...]
        r = lax.rsqrt(jnp.mean(x * x, axis=-1, keepdims=True) + EPS)
        xhat = x * r
        dnv = dn_ref[...].astype(F32)

        @pl.when(pl.program_id(0) == 0)
        def _():
            dg_ref[...] = jnp.zeros_like(dg_ref)

        dg_ref[...] += jnp.sum(dnv * xhat, axis=0, keepdims=True)
        dxh = dnv * g_ref[...]
        dh = dres_ref[...] + r * (dxh - xhat * jnp.mean(dxh * xhat, axis=-1, keepdims=True))
        dh_ref[...] = dh
        dhb_ref[...] = dh.astype(BF16)

    row = pl.BlockSpec((tr, d), lambda i: (i, 0))
    vec = pl.BlockSpec((1, d), lambda i: (0, 0))
    return pl.pallas_call(
        _behind(after, body, 4), name=name, grid=(t // tr,),
        in_specs=[row, row, vec, row] + [_ANY] * len(after),
        out_specs=[row, row, vec],
        out_shape=[jax.ShapeDtypeStruct((t, d), F32), jax.ShapeDtypeStruct((t, d), BF16),
                   jax.ShapeDtypeStruct((1, d), F32)],
        compiler_params=_cparams("arbitrary"),
    )(dn, h, g, dres, *after)


def _loss_head(h, g, target, after=()):
    t, d = h.shape
    tr = _pick(t, (256, 128, 64, 32, 16, 8))

    def body(h_ref, g_ref, t_ref, loss_ref, dh_ref, dhb_ref, dg_ref):
        x = h_ref[...]
        gv = g_ref[...]
        r = lax.rsqrt(jnp.mean(x * x, axis=-1, keepdims=True) + EPS)
        xhat = x * r
        err = xhat * gv - t_ref[...]

        @pl.when(pl.program_id(0) == 0)
        def _():
            dg_ref[...] = jnp.zeros_like(dg_ref)
            loss_ref[...] = jnp.zeros_like(loss_ref)

        loss_ref[...] += jnp.full(loss_ref.shape, 0.5 / d, F32) * jnp.sum(err * err)
        dy = err * (1.0 / d)
        dg_ref[...] += jnp.sum(dy * xhat, axis=0, keepdims=True)
        dxh = dy * gv
        dh = r * (dxh - xhat * jnp.mean(dxh * xhat, axis=-1, keepdims=True))
        dh_ref[...] = dh
        dhb_ref[...] = dh.astype(BF16)

    row = pl.BlockSpec((tr, d), lambda i: (i, 0))
    vec = pl.BlockSpec((1, d), lambda i: (0, 0))
    return pl.pallas_call(
        _behind(after, body, 3), name="loss_head", grid=(t // tr,),
        in_specs=[row, vec, row] + [_ANY] * len(after),
        out_specs=[pl.BlockSpec((1, LANES), lambda i: (0, 0)), row, row, vec],
        out_shape=[jax.ShapeDtypeStruct((1, LANES), F32), jax.ShapeDtypeStruct((t, d), F32),
                   jax.ShapeDtypeStruct((t, d), BF16), jax.ShapeDtypeStruct((1, d), F32)],
        compiler_params=_cparams("arbitrary"),
    )(h, g, target, *after)


def _cur(ts, width, col):
    return pl.BlockSpec((ts, width), lambda i: (i, col))


def _prev_halo(ts, halo, width, col):
    per = ts // halo
    return pl.BlockSpec((halo, width), lambda i: (jnp.maximum(i * per - 1, 0), col))


def _next_halo(ts, halo, width, col, n_rows):
    per = ts // halo
    last = n_rows // halo - 1
    return pl.BlockSpec((halo, width), lambda i: (jnp.minimum((i + 1) * per, last), col))


def _full(shape):
    nd = len(shape)
    return pl.BlockSpec(shape, lambda i: (0,) * nd)


def _shift_down(x, n):
    return x if n == 0 else pltpu.roll(x, n, 0)


def _shift_up(x, n):
    return x if n == 0 else pltpu.roll(x, x.shape[0] - n, 0)


CONV_ROWS = 32


def _conv_block_shape(channels, ts):
    return min(CONV_ROWS, ts), min(LANES, channels)


SUBLANES = 8


def _fill_shifted(rot_ref, ext):
    rot_ref[0] = ext
    for r in range(1, SUBLANES):
        rot_ref[r] = _shift_up(ext, r)


def _window(rot_ref, first, rows, c0, cw):
    r = first % SUBLANES
    return rot_ref[r, first - r:first - r + rows, c0:c0 + cw]


def _causal_taps(rot_ref, w_ref, halo, taps, r0, c0, rows, cw):
    acc = jnp.zeros((rows, cw), F32)
    for k in range(taps):
        acc = acc + w_ref[k:k + 1, c0:c0 + cw] * _window(rot_ref, halo + r0 - (taps - 1 - k), rows, c0, cw)
    return acc


def _pool_counts(i, ns, ts, w):
    pos = (i % ns) * ts + lax.broadcasted_iota(jnp.int32, (ts, 1), 0)
    return jnp.minimum(pos + 1, w).astype(F32)


def _pooled(cur, prev_tail, w, cnt):
    s = jnp.concatenate([prev_tail, cur], axis=0)
    d = 1
    while d < w:
        s = s + _shift_down(s, d)
        d *= 2
    return s[POOL_HALO:, :] / cnt - cur


def _mixer_e_fwd(u, conv_w, conv_b, ln_g, ln_b, w_pool, scale, seq, ts, after=()):
    t = u.shape[0]
    dc = conv_b.shape[1]
    ng, pg = w_pool.shape[0], w_pool.shape[1]
    taps = conv_w.shape[0]
    ns = seq // ts

    def body(val_ref, gate_ref, b_ref, pval_ref, pgate_ref, pb_ref, cw_ref, cb_ref, g_ref, be_ref, wp_ref, sc_ref,
             a2_ref, cat_ref, rot_ref):
        i = pl.program_id(0)
        keep_prev = jnp.where(i % ns == 0, 0.0, 1.0)
        a1 = val_ref[...] * _sigmoid(gate_ref[...])
        pa1 = pval_ref[...] * _sigmoid(pgate_ref[...]) * keep_prev
        _fill_shifted(rot_ref, jnp.concatenate([pa1, a1], axis=0))
        rows, cw = _conv_block_shape(dc, ts)
        for c0 in range(0, dc, cw):
            for r0 in range(0, ts, rows):
                acc = _causal_taps(rot_ref, cw_ref, CONV_HALO, taps, r0, c0, rows, cw)
                a2_ref[r0:r0 + rows, c0:c0 + cw] = acc + cb_ref[:, c0:c0 + cw]
        a2 = a2_ref[...]
        mu = jnp.mean(a2, axis=-1, keepdims=True)
        xc = a2 - mu
        rstd = lax.rsqrt(jnp.mean(xc * xc, axis=-1, keepdims=True) + EPS)
        a3 = xc * rstd * g_ref[...] + be_ref[...]
        cat_ref[:, 0:dc] = (a3 * _sigmoid(a3)).astype(BF16)
        for g in range(ng):
            lo, hi = g * pg, (g + 1) * pg
            w = POOL_WINDOWS[g]
            p = _pooled(b_ref[:, lo:hi], pb_ref[:, lo:hi] * keep_prev, w, _pool_counts(i, ns, ts, w))
            q = jnp.dot(p.astype(BF16), wp_ref[g].astype(BF16), preferred_element_type=F32)
            cat_ref[:, dc + lo:dc + hi] = (q * sc_ref[:, lo:hi]).astype(BF16)

    return pl.pallas_call(
        _behind(after, body, 12), name="mixer_e_fwd", grid=(t // ts,),
        in_specs=[_cur(ts, dc, 0), _cur(ts, dc, 1), _cur(ts, dc, 2),
                  _prev_halo(ts, CONV_HALO, dc, 0), _prev_halo(ts, CONV_HALO, dc, 1), _prev_halo(ts, POOL_HALO, dc, 2),
                  _full(conv_w.shape), _full(conv_b.shape), _full(ln_g.shape), _full(ln_b.shape),
                  _full(w_pool.shape), _full(scale.shape)] + [_ANY] * len(after),
        out_specs=[_cur(ts, dc, 0), _cur(ts, 2 * dc, 0)],
        out_shape=[jax.ShapeDtypeStruct((t, dc), F32), jax.ShapeDtypeStruct((t, 2 * dc), BF16)],
        scratch_shapes=[pltpu.VMEM((SUBLANES, CONV_HALO + ts, dc), F32)],
        compiler_params=_cparams("parallel"),
    )(u, u, u, u, u, u, conv_w, conv_b, ln_g, ln_b, w_pool, scale, *after)


def _mixer_e_bwd_norm(dcat, a2, ln_g, ln_b, ts, after=()):
    t, dc = a2.shape

    def body(d_ref, a2_ref, g_ref, be_ref, da2_ref, dg_ref, db_ref, dcb_ref):
        x = a2_ref[...]
        gv = g_ref[...]
        mu = jnp.mean(x, axis=-1, keepdims=True)
        xc = x - mu
        rstd = lax.rsqrt(jnp.mean(xc * xc, axis=-1, keepdims=True) + EPS)
        xhat = xc * rstd
        a3 = xhat * gv + be_ref[...]
        sg = _sigmoid(a3)
        da3 = d_ref[...] * (sg * (1.0 + a3 * (1.0 - sg)))
        dxh = da3 * gv
        da2 = rstd * (dxh - jnp.mean(dxh, axis=-1, keepdims=True)
                      - xhat * jnp.mean(dxh * xhat, axis=-1, keepdims=True))
        da2_ref[...] = da2

        @pl.when(pl.program_id(0) == 0)
        def _():
            dg_ref[...] = jnp.zeros_like(dg_ref)
            db_ref[...] = jnp.zeros_like(db_ref)
            dcb_ref[...] = jnp.zeros_like(dcb_ref)

        dg_ref[...] += jnp.sum(da3 * xhat, axis=0, keepdims=True)
        db_ref[...] += jnp.sum(da3, axis=0, keepdims=True)
        dcb_ref[...] += jnp.sum(da2, axis=0, keepdims=True)

    vec = _full((1, dc))
    return pl.pallas_call(
        _behind(after, body, 4), name="mixer_e_bwd_norm", grid=(t // ts,),
        in_specs=[_cur(ts, dc, 0), _cur(ts, dc, 0), vec, vec] + [_ANY] * len(after),
        out_specs=[_cur(ts, dc, 0), vec, vec, vec],
        out_shape=[jax.ShapeDtypeStruct((t, dc), F32)] + [jax.ShapeDtypeStruct((1, dc), F32)] * 3,
        compiler_params=_cparams("arbitrary"),
    )(dcat, a2, ln_g, ln_b, *after)


def _mixer_e_bwd_mix(da2, dcat, u, conv_w, w_pool, scale, seq, ts, after=()):
    t, dc = da2.shape
    ng, pg = w_pool.shape[0], w_pool.shape[1]
    taps = conv_w.shape[0]
    ns = seq // ts

    def body(da2_ref, nda2_ref, dp_ref, ndp_ref, val_ref, gate_ref, b_ref, pval_ref, pgate_ref, pb_ref,
             cw_ref, wp_ref, sc_ref, du_ref, dcw_ref, dwp_ref, dsc_ref, rota_ref, rotd_ref):
        i = pl.program_id(0)
        keep_prev = jnp.where(i % ns == 0, 0.0, 1.0)
        keep_next = jnp.where(i % ns == ns - 1, 0.0, 1.0)

        @pl.when(i == 0)
        def _():
            dcw_ref[...] = jnp.zeros_like(dcw_ref)
            dwp_ref[...] = jnp.zeros_like(dwp_ref)
            dsc_ref[...] = jnp.zeros_like(dsc_ref)

        val = val_ref[...]
        sg = _sigmoid(gate_ref[...])
        a1 = val * sg
        pa1 = pval_ref[...] * _sigmoid(pgate_ref[...]) * keep_prev
        _fill_shifted(rota_ref, jnp.concatenate([pa1, a1], axis=0))
        _fill_shifted(rotd_ref, jnp.concatenate([da2_ref[...], nda2_ref[...] * keep_next], axis=0))
        rows, cw = _conv_block_shape(dc, ts)
        for c0 in range(0, dc, cw):
            lanes = slice(c0, c0 + cw)
            dw = [jnp.zeros((SUBLANES, cw), F32)] * taps
            for r0 in range(0, ts, rows):
                blk = slice(r0, r0 + rows)
                d_blk = da2_ref[blk, lanes]
                da1 = jnp.zeros((rows, cw), F32)
                for k in range(taps):
                    sh = taps - 1 - k
                    prod = d_blk * _window(rota_ref, CONV_HALO + r0 - sh, rows, c0, cw)
                    for f in range(0, rows, SUBLANES):
                        dw[k] = dw[k] + prod[f:f + SUBLANES, :]
                    da1 = da1 + cw_ref[k:k + 1, lanes] * _window(rotd_ref, r0 + sh, rows, c0, cw)
                sg_b = sg[blk, lanes]
                du_ref[blk, lanes] = (da1 * sg_b).astype(BF16)
                du_ref[blk, dc + c0:dc + c0 + cw] = (da1 * a1[blk, lanes] * (1.0 - sg_b)).astype(BF16)
            for k in range(taps):
                dcw_ref[k:k + 1, lanes] += jnp.sum(dw[k], axis=0, keepdims=True)

        for g in range(ng):
            lo, hi = g * pg, (g + 1) * pg
            w = POOL_WINDOWS[g]
            cnt = _pool_counts(i, ns, ts, w)
            wpb = wp_ref[g].astype(BF16)
            sc = sc_ref[:, lo:hi]
            p = _pooled(b_ref[:, lo:hi], pb_ref[:, lo:hi] * keep_prev, w, cnt)
            pb16 = p.astype(BF16)
            q = jnp.dot(pb16, wpb, preferred_element_type=F32)
            dout = dp_ref[:, lo:hi]
            dsc_ref[:, lo:hi] += jnp.sum(dout * q, axis=0, keepdims=True)
            dq = (dout * sc).astype(BF16)
            dwp_ref[g] += lax.dot_general(pb16, dq, _DOT_DIMS["tn"], preferred_element_type=F32)
            dpool = lax.dot_general(dq, wpb, _DOT_DIMS["nt"], preferred_element_type=F32)
            ndq = (ndp_ref[:, lo:hi] * sc * keep_next).astype(BF16)
            ndpool = lax.dot_general(ndq, wpb, _DOT_DIMS["nt"], preferred_element_type=F32)
            s = jnp.concatenate([dpool / cnt, ndpool * (1.0 / w)], axis=0)
            d = 1
            while d < w:
                s = s + _shift_up(s, d)
                d *= 2
            du_ref[:, 2 * dc + lo:2 * dc + hi] = (s[:ts, :] - dpool).astype(BF16)

    return pl.pallas_call(
        _behind(after, body, 13), name="mixer_e_bwd_mix", grid=(t // ts,),
        in_specs=[_cur(ts, dc, 0), _next_halo(ts, CONV_HALO, dc, 0, t),
                  _cur(ts, dc, 1), _next_halo(ts, POOL_HALO, dc, 1, t),
                  _cur(ts, dc, 0), _cur(ts, dc, 1), _cur(ts, dc, 2),
                  _prev_halo(ts, CONV_HALO, dc, 0), _prev_halo(ts, CONV_HALO, dc, 1), _prev_halo(ts, POOL_HALO, dc, 2),
                  _full(conv_w.shape), _full(w_pool.shape), _full(scale.shape)] + [_ANY] * len(after),
        out_specs=[_cur(ts, 3 * dc, 0), _full(conv_w.shape), _full(w_pool.shape), _full(scale.shape)],
        out_shape=[jax.ShapeDtypeStruct((t, 3 * dc), BF16), jax.ShapeDtypeStruct(conv_w.shape, F32),
                   jax.ShapeDtypeStruct(w_pool.shape, F32), jax.ShapeDtypeStruct(scale.shape, F32)],
        scratch_shapes=[pltpu.VMEM((SUBLANES, ts + CONV_HALO, dc), F32)] * 2,
        compiler_params=_cparams("arbitrary"),
    )(da2, da2, dcat, dcat, u, u, u, u, u, u, conv_w, w_pool, scale, *after)


def _mixer_o_fwd(u, conv_w, seq, ts, after=()):
    t = u.shape[0]
    d = conv_w.shape[1]
    taps = conv_w.shape[0]
    ns = seq // ts

    def body(gb_ref, gc_ref, v_ref, pgc_ref, pv_ref, cw_ref, y_ref):
        keep_prev = jnp.where(pl.program_id(0) % ns == 0, 0.0, 1.0)
        f32 = lambda ref: ref[...].astype(F32)
        ext = jnp.concatenate([f32(pgc_ref) * f32(pv_ref) * keep_prev, f32(gc_ref) * f32(v_ref)], axis=0)
        cc = jnp.zeros_like(ext)
        for k in range(taps):
            cc = cc + cw_ref[k:k + 1, :] * _shift_down(ext, taps - 1 - k)
        y_ref[...] = (f32(gb_ref) * cc[SHORT_HALO:, :]).astype(BF16)

    return pl.pallas_call(
        _behind(after, body, 6), name="mixer_o_fwd", grid=(t // ts,),
        in_specs=[_cur(ts, d, 0), _cur(ts, d, 1), _cur(ts, d, 2),
                  _prev_halo(ts, SHORT_HALO, d, 1), _prev_halo(ts, SHORT_HALO, d, 2), _full(conv_w.shape)]
        + [_ANY] * len(after),
        out_specs=_cur(ts, d, 0),
        out_shape=jax.ShapeDtypeStruct((t, d), BF16),
        compiler_params=_cparams("parallel"),
    )(u, u, u, u, u, conv_w, *after)


def _mixer_o_bwd(dy, u, conv_w, seq, ts, after=()):
    t = u.shape[0]
    d = conv_w.shape[1]
    taps = conv_w.shape[0]
    ns = seq // ts

    def body(dy_ref, ndy_ref, gb_ref, gc_ref, v_ref, pgc_ref, pv_ref, ngb_ref, cw_ref, du_ref, dcw_ref):
        i = pl.program_id(0)
        keep_prev = jnp.where(i % ns == 0, 0.0, 1.0)
        keep_next = jnp.where(i % ns == ns - 1, 0.0, 1.0)

        @pl.when(i == 0)
        def _():
            dcw_ref[...] = jnp.zeros_like(dcw_ref)

        f32 = lambda ref: ref[...].astype(F32)
        gb, gc, v, dyv = f32(gb_ref), f32(gc_ref), f32(v_ref), f32(dy_ref)
        ext = jnp.concatenate([f32(pgc_ref) * f32(pv_ref) * keep_prev, gc * v], axis=0)
        dcc = dyv * gb
        ext_d = jnp.concatenate([dcc, f32(ndy_ref) * f32(ngb_ref) * keep_next], axis=0)
        cc = jnp.zeros_like(ext)
        dcv = jnp.zeros_like(ext_d)
        for k in range(taps):
            sh = taps - 1 - k
            shifted = _shift_down(ext, sh)
            cc = cc + cw_ref[k:k + 1, :] * shifted
            dcw_ref[k:k + 1, :] += jnp.sum(dcc * shifted[SHORT_HALO:, :], axis=0, keepdims=True)
            dcv = dcv + cw_ref[k:k + 1, :] * _shift_up(ext_d, sh)
        dcv = dcv[:ts, :]
        du_ref[:, 0:d] = (dyv * cc[SHORT_HALO:, :]).astype(BF16)
        du_ref[:, d:2 * d] = (dcv * v).astype(BF16)
        du_ref[:, 2 * d:3 * d] = (dcv * gc).astype(BF16)

    return pl.pallas_call(
        _behind(after, body, 9), name="mixer_o_bwd", grid=(t // ts,),
        in_specs=[_cur(ts, d, 0), _next_halo(ts, SHORT_HALO, d, 0, t),
                  _cur(ts, d, 0), _cur(ts, d, 1), _cur(ts, d, 2),
                  _prev_halo(ts, SHORT_HALO, d, 1), _prev_halo(ts, SHORT_HALO, d, 2),
                  _next_halo(ts, SHORT_HALO, d, 0, t), _full(conv_w.shape)] + [_ANY] * len(after),
        out_specs=[_cur(ts, 3 * d, 0), _full(conv_w.shape)],
        out_shape=[jax.ShapeDtypeStruct((t, 3 * d), BF16), jax.ShapeDtypeStruct(conv_w.shape, F32)],
        compiler_params=_cparams("arbitrary"),
    )(dy, dy, u, u, u, u, u, u, conv_w, *after)


def _cast_into_full(name, mat, w, layer, chip, after=()):
    tr = _pick(mat.sr, (512, 256, 128, 64, 32, 16))
    per = mat.sr // tr

    def body(chip_ref, w_ref, *rest):
        o_ref = rest[-1]
        o_ref[...] = w_ref[...].astype(BF16)

    if mat.kind == "col":
        o_spec = pl.BlockSpec((tr, mat.sc), lambda i, chip_ref: (i, chip_ref[0]))
    else:
        o_spec = pl.BlockSpec((tr, mat.sc), lambda i, chip_ref: (chip_ref[0] * per + i, 0))
    return pl.pallas_call(
        body, name=name,
        grid_spec=pltpu.PrefetchScalarGridSpec(
            num_scalar_prefetch=1, grid=(per,),
            in_specs=[pl.BlockSpec((None, tr, mat.sc), lambda i, chip_ref: (layer, i, 0))] + [_ANY] * len(after),
            out_specs=o_spec),
        out_shape=jax.ShapeDtypeStruct(mat.full_shape, BF16),
        compiler_params=_cparams("parallel"),
    )(chip, w, *after)


def _adamw(name, w, g, m, v, after=()):
    r, c = w.shape
    tr = _pick(r, (256, 128, 64, 32, 16, 8)) if c > 1024 else _pick(r, (512, 256, 128, 64, 32, 16, 8))
    bc1 = 1.0 - ADAM_B1 ** ADAM_STEP
    bc2 = 1.0 - ADAM_B2 ** ADAM_STEP

    def body(w_ref, g_ref, m_ref, v_ref, d_ref, mo_ref, vo_ref):
        gv = g_ref[...]
        mn = ADAM_B1 * m_ref[...] + (1.0 - ADAM_B1) * gv
        vn = ADAM_B2 * v_ref[...] + (1.0 - ADAM_B2) * (gv * gv)
        mo_ref[...] = mn
        vo_ref[...] = vn
        d_ref[...] = -ADAM_LR * ((mn / bc1) / (jnp.sqrt(vn / bc2) + ADAM_EPS) + ADAM_WD * w_ref[...])

    spec = pl.BlockSpec((tr, c), lambda i: (i, 0))
    return pl.pallas_call(_behind(after, body, 4), name=name, grid=(r // tr,),
                          in_specs=[spec] * 4 + [_ANY] * len(after), out_specs=[spec] * 3,
                          out_shape=[jax.ShapeDtypeStruct((r, c), F32)] * 3,
                          compiler_params=_cparams("parallel"))(w, g, m, v, *after)


def _aligned(offset, multiple):
    return offset if isinstance(offset, int) else pl.multiple_of(offset, multiple)


class _Mat:
    def __init__(self, kind, shard_shape):
        self.kind = kind
        self.sr, self.sc = shard_shape
        self.full_shape = (self.sr, self.sc * N_CHIPS) if kind == "col" else (self.sr * N_CHIPS, self.sc)
        self.pr, self.pc = self.sr // 2, self.sc

    def piece(self, ref, k, h):
        if self.kind == "col":
            return ref.at[pl.ds(_aligned(h * self.pr, 16), self.pr), pl.ds(_aligned(k * self.sc, LANES), self.sc)]
        return ref.at[pl.ds(_aligned(k * self.sr + h * self.pr, 16), self.pr), :]

    def shard(self, ref, k):
        if self.kind == "col":
            return ref.at[:, pl.ds(_aligned(k * self.sc, LANES), self.sc)]
        return ref.at[pl.ds(_aligned(k * self.sr, 16), self.sr), :]

    def half(self, ref, h):
        return ref.at[pl.ds(_aligned(h * self.pr, 16), self.pr), :]


def _place():
    x, y, c = lax.axis_index("x"), lax.axis_index("y"), lax.axis_index("c")
    others = [(1 - x, y), (x, 1 - y), (1 - x, 1 - y)]
    return x, y, c, others


_HBM = pl.BlockSpec(memory_space=pltpu.HBM)
_SEM = pl.BlockSpec(memory_space=pltpu.SEMAPHORE)
_TOKEN = jax.ShapeDtypeStruct((8, LANES), F32)
_TOKEN_SPEC = pl.BlockSpec(memory_space=pltpu.VMEM)


def _split_params():
    return pltpu.CompilerParams(has_side_effects=pltpu.SideEffectType.DATAFLOW_SIDE_EFFECTING)


def _in_hbm(a):
    return pltpu.with_memory_space_constraint(a, pltpu.HBM)


def _copy_to(src, dst, send_sem, recv_sem, to):
    return pltpu.make_async_remote_copy(src_ref=src, dst_ref=dst, send_sem=send_sem, recv_sem=recv_sem,
                                        device_id=to, device_id_type=MESH_ID)


def _place_small(packed, chip):
    rows, cols = packed.shape

    def body(chip_ref, p_ref, o_ref):
        o_ref[...] = p_ref[...]

    return pl.pallas_call(
        body, name="place_small",
        grid_spec=pltpu.PrefetchScalarGridSpec(
            num_scalar_prefetch=1, grid=(1,),
            in_specs=[pl.BlockSpec((rows, cols), lambda i, chip_ref: (0, 0))],
            out_specs=pl.BlockSpec((None, rows, cols), lambda i, chip_ref: (chip_ref[0], 0, 0))),
        out_shape=jax.ShapeDtypeStruct((N_CHIPS, rows, cols), F32),
        compiler_params=_cparams("arbitrary"),
    )(chip, packed)


def _gather_start(name, gmats, gfulls, small_all=None):
    n = len(gmats)
    n_in = n + (1 if small_all is not None else 0)

    def body(*refs):
        full_refs = refs[:n]
        outs = refs[n_in:]
        send_sem, recv_sem, token = outs[n_in], outs[n_in + 1], outs[n_in + 2]
        x, y, c, others = _place()
        me_k = 2 * x + y
        if small_all is not None:
            mine = refs[n].at[me_k]
            for ox, oy in others:
                _copy_to(mine, mine, send_sem, recv_sem, (ox, oy, c)).start()
        for m in range(n):
            mine = gmats[m].piece(full_refs[m], me_k, c)
            for ox, oy in others:
                _copy_to(mine, mine, send_sem, recv_sem, (ox, oy, c)).start()
        token[...] = jnp.zeros_like(token)

    operands = [_in_hbm(f) for f in gfulls] + ([_in_hbm(small_all)] if small_all is not None else [])
    outs = pl.pallas_call(
        body, name=name,
        in_specs=[_HBM] * n_in,
        out_specs=[_HBM] * n_in + [_SEM, _SEM, _TOKEN_SPEC],
        out_shape=[pltpu.HBM(a.shape, a.dtype) for a in operands] + [pltpu.SemaphoreType.DMA(())] * 2 + [_TOKEN],
        input_output_aliases={m: m for m in range(n_in)},
        compiler_params=_split_params(),
    )(*operands)
    return list(outs[:n]), (outs[n] if small_all is not None else None), (outs[n_in], outs[n_in + 1]), outs[n_in + 2]


def _gather_pass(name, gmats, gfulls, small_all, sems, after):
    k = len(gmats)
    n_buf = k + (1 if small_all is not None else 0)

    def body(*refs):
        bufs = refs[:n_buf]
        send_sem, recv_sem = refs[n_buf], refs[n_buf + 1]
        outs = refs[n_buf + 3:]
        fsend, frecv, token = outs[n_buf], outs[n_buf + 1], outs[n_buf + 2]
        x, y, c, others = _place()
        me_k = 2 * x + y
        sibling = (x, y, 1 - c)
        for m in range(k):
            for ox, oy in others:
                got = gmats[m].piece(bufs[m], 2 * ox + oy, c)
                _copy_to(got, got, send_sem, recv_sem, sibling).wait_recv()
        if small_all is not None:
            for ox, oy in others:
                got = bufs[k].at[2 * ox + oy]
                _copy_to(got, got, send_sem, recv_sem, sibling).wait_recv()
        for m in range(k):
            mine = gmats[m].piece(bufs[m], me_k, c)
            for _ in others:
                _copy_to(mine, mine, send_sem, recv_sem, sibling).wait_send()
        if small_all is not None:
            for _ in others:
                _copy_to(bufs[k].at[me_k], bufs[k].at[me_k], send_sem, recv_sem, sibling).wait_send()
        for m in range(k):
            for ox, oy in others:
                got = gmats[m].piece(bufs[m], 2 * ox + oy, c)
                _copy_to(got, got, fsend, frecv, sibling).start()
        token[...] = jnp.zeros_like(token)

    operands = [_in_hbm(f) for f in gfulls] + ([_in_hbm(small_all)] if small_all is not None else [])
    outs = pl.pallas_call(
        body, name=name,
        in_specs=[_HBM] * n_buf + [_SEM, _SEM, _ANY],
        out_specs=[_HBM] * n_buf + [_SEM, _SEM, _TOKEN_SPEC],
        out_shape=[pltpu.HBM(a.shape, a.dtype) for a in operands] + [pltpu.SemaphoreType.DMA(())] * 2 + [_TOKEN],
        input_output_aliases={i: i for i in range(n_buf)},
        compiler_params=_split_params(),
    )(*operands, sems[0], sems[1], after)
    return list(outs[:n_buf]), (outs[n_buf], outs[n_buf + 1]), outs[n_buf + 2]


def _gather_done(name, gmats, gfulls, sems, after):
    k = len(gmats)

    def body(*refs):
        bufs = refs[:k]
        send_sem, recv_sem = refs[k], refs[k + 1]
        x, y, c, others = _place()
        sibling = (x, y, 1 - c)
        for m in range(k):
            for ox, oy in others:
                got = gmats[m].piece(bufs[m], 2 * ox + oy, 1 - c)
                _copy_to(got, got, send_sem, recv_sem, sibling).wait_recv()
        for m in range(k):
            for ox, oy in others:
                sent = gmats[m].piece(bufs[m], 2 * ox + oy, c)
                _copy_to(sent, sent, send_sem, recv_sem, sibling).wait_send()

    outs = pl.pallas_call(
        body, name=name,
        in_specs=[_HBM] * k + [_SEM, _SEM, _ANY], out_specs=[_HBM] * k,
        out_shape=[pltpu.HBM(a.shape, a.dtype) for a in gfulls],
        input_output_aliases={i: i for i in range(k)},
        compiler_params=_split_params(),
    )(*[_in_hbm(f) for f in gfulls], sems[0], sems[1], after)
    return list(outs)


_FLIPS = [(fx, fy, fc) for fx in (0, 1) for fy in (0, 1) for fc in (0, 1) if (fx, fy, fc) != (0, 0, 0)]


def _small_start(packed, after):
    def body(small_ref, after_ref, small_thru, land_ref, send_sem, recv_sem, token):
        x, y, c, _ = _place()
        me = 4 * x + 2 * y + c
        for fx, fy, fc in _FLIPS:
            _copy_to(small_ref, land_ref.at[me], send_sem, recv_sem, (x ^ fx, y ^ fy, c ^ fc)).start()
        token[...] = jnp.zeros_like(token)

    outs = pl.pallas_call(
        body, name="small_grads_start",
        in_specs=[_HBM, _ANY], out_specs=[_HBM, _HBM, _SEM, _SEM, _TOKEN_SPEC],
        out_shape=[pltpu.HBM(packed.shape, F32), pltpu.HBM((N_DEV,) + packed.shape, F32)]
        + [pltpu.SemaphoreType.DMA(())] * 2 + [_TOKEN],
        input_output_aliases={0: 0},
        compiler_params=_split_params(),
    )(_in_hbm(packed), after)
    return outs[0], outs[1], (outs[2], outs[3]), outs[4]


def _small_wait(packed, landed, sems, after):
    def body(small_ref, land_ref, send_sem, recv_sem, after_ref, small_thru, land_thru):
        x, y, c, _ = _place()
        for fx, fy, fc in _FLIPS:
            got = land_ref.at[4 * (x ^ fx) + 2 * (y ^ fy) + (c ^ fc)]
            _copy_to(got, got, send_sem, recv_sem, (x, y, 1 - c)).wait_recv()
        for _ in _FLIPS:
            _copy_to(small_ref, small_ref, send_sem, recv_sem, (x, y, 1 - c)).wait_send()

    outs = pl.pallas_call(
        body, name="small_grads_wait",
        in_specs=[_HBM, _HBM, _SEM, _SEM, _ANY], out_specs=[_HBM, _HBM],
        out_shape=[pltpu.HBM(packed.shape, F32), pltpu.HBM(landed.shape, F32)],
        input_output_aliases={0: 0, 1: 1},
        compiler_params=_split_params(),
    )(_in_hbm(packed), _in_hbm(landed), sems[0], sems[1], after)
    return outs[0], outs[1]


def _exchange_start(name, mats, grads):
    n = len(mats)

    def body(*refs):
        g_refs = refs[:n]
        outs = refs[n:]
        land_refs = outs[n:2 * n]
        send_sem, recv_sem, token = outs[2 * n], outs[2 * n + 1], outs[2 * n + 2]
        x, y, c, _ = _place()
        for m in range(n):
            for k in range(N_CHIPS):
                _copy_to(mats[m].piece(g_refs[m], k, 1 - c), land_refs[m].at[k], send_sem, recv_sem, (x, y, 1 - c)).start()
        token[...] = jnp.zeros_like(token)

    outs = pl.pallas_call(
        body, name=name,
        in_specs=[_HBM] * n,
        out_specs=[_HBM] * (2 * n) + [_SEM, _SEM, _TOKEN_SPEC],
        out_shape=[pltpu.HBM(mt.full_shape, BF16) for mt in mats]
        + [pltpu.HBM((N_CHIPS, mt.pr, mt.pc), BF16) for mt in mats] + [pltpu.SemaphoreType.DMA(())] * 2 + [_TOKEN],
        input_output_aliases={m: m for m in range(n)},
        compiler_params=_split_params(),
    )(*[_in_hbm(g) for g in grads])
    return list(outs[:n]), list(outs[n:2 * n]), (outs[2 * n], outs[2 * n + 1]), outs[2 * n + 2]


def _exchange_wait(name, mats, grads, landed, sems, after):
    n = len(mats)

    def body(*refs):
        g_refs, land_refs = refs[:n], refs[n:2 * n]
        send_sem, recv_sem = refs[2 * n], refs[2 * n + 1]
        x, y, c, _ = _place()
        for m in range(n):
            for k in range(N_CHIPS):
                got = land_refs[m].at[k]
                _copy_to(got, got, send_sem, recv_sem, (x, y, 1 - c)).wait_recv()
        for m in range(n):
            for k in range(N_CHIPS):
                sent = mats[m].piece(g_refs[m], k, 1 - c)
                _copy_to(sent, sent, send_sem, recv_sem, (x, y, 1 - c)).wait_send()

    outs = pl.pallas_call(
        body, name=name,
        in_specs=[_HBM] * (2 * n) + [_SEM, _SEM, _ANY], out_specs=[_HBM] * (2 * n),
        out_shape=[pltpu.HBM(a.shape, a.dtype) for a in list(grads) + list(landed)],
        input_output_aliases={i: i for i in range(2 * n)},
        compiler_params=_split_params(),
    )(*[_in_hbm(a) for a in list(grads) + list(landed)], sems[0], sems[1], after)
    return list(outs[:n]), list(outs[n:])


def _add_halves(name, mat, grad, landed, core):
    tr = _pick(mat.pr, (1024, 704, 512, 352, 256, 128, 64, 32, 16))
    per = mat.pr // tr

    def body(core_ref, g_ref, l_ref, o_ref):
        o_ref[...] = (g_ref[...].astype(F32) + l_ref[...].astype(F32)).astype(BF16)

    if mat.kind == "col":
        g_spec = pl.BlockSpec((tr, mat.pc), lambda k, r, core_ref: (core_ref[0] * per + r, k))
    else:
        g_spec = pl.BlockSpec((tr, mat.pc), lambda k, r, core_ref: ((2 * k + core_ref[0]) * per + r, 0))
    p_spec = pl.BlockSpec((None, tr, mat.pc), lambda k, r, core_ref: (k, r, 0))
    return pl.pallas_call(
        body, name=name,
        grid_spec=pltpu.PrefetchScalarGridSpec(num_scalar_prefetch=1, grid=(N_CHIPS, per),
                                               in_specs=[g_spec, p_spec], out_specs=p_spec),
        out_shape=jax.ShapeDtypeStruct((N_CHIPS, mat.pr, mat.pc), BF16),
        compiler_params=_cparams("parallel", "parallel"),
    )(core, grad, landed)


def _scatter_start(name, mats, partials):
    n = len(mats)

    def body(*refs):
        p_refs = refs[:n]
        outs = refs[n:]
        land_refs = outs[n:2 * n]
        send_sem, recv_sem, token = outs[2 * n], outs[2 * n + 1], outs[2 * n + 2]
        x, y, c, others = _place()
        me_k = 2 * x + y
        for m in range(n):
            for ox, oy in others:
                _copy_to(p_refs[m].at[2 * ox + oy], land_refs[m].at[me_k], send_sem, recv_sem, (ox, oy, c)).start()
        token[...] = jnp.zeros_like(token)

    piece_shapes = [pltpu.HBM((N_CHIPS, mt.pr, mt.pc), BF16) for mt in mats]
    outs = pl.pallas_call(
        body, name=name,
        in_specs=[_HBM] * n,
        out_specs=[_HBM] * (2 * n) + [_SEM, _SEM, _TOKEN_SPEC],
        out_shape=piece_shapes + piece_shapes + [pltpu.SemaphoreType.DMA(())] * 2 + [_TOKEN],
        input_output_aliases={m: m for m in range(n)},
        compiler_params=_split_params(),
    )(*[_in_hbm(p) for p in partials])
    return list(outs[:n]), list(outs[n:2 * n]), (outs[2 * n], outs[2 * n + 1]), outs[2 * n + 2]


def _scatter_wait(name, mats, partials, landed, sems, after):
    n = len(mats)

    def body(*refs):
        p_refs, land_refs = refs[:n], refs[n:2 * n]
        send_sem, recv_sem = refs[2 * n], refs[2 * n + 1]
        x, y, c, others = _place()
        for m in range(n):
            for ox, oy in others:
                got = land_refs[m].at[2 * ox + oy]
                _copy_to(got, got, send_sem, recv_sem, (ox, oy, c)).wait_recv()
        for m in range(n):
            for ox, oy in others:
                sent = p_refs[m].at[2 * ox + oy]
                _copy_to(sent, sent, send_sem, recv_sem, (ox, oy, c)).wait_send()

    outs = pl.pallas_call(
        body, name=name,
        in_specs=[_HBM] * (2 * n) + [_SEM, _SEM, _ANY], out_specs=[_HBM] * (2 * n),
        out_shape=[pltpu.HBM(a.shape, a.dtype) for a in list(partials) + list(landed)],
        input_output_aliases={i: i for i in range(2 * n)},
        compiler_params=_split_params(),
    )(*[_in_hbm(a) for a in list(partials) + list(landed)], sems[0], sems[1], after)
    return list(outs[:n]), list(outs[n:])


def _sum_chips(name, mat, partial, landed, slots, layer=None, stack=None, n_layers=1):
    tr = _pick(mat.pr, (1024, 704, 512, 352, 256, 128, 64, 32, 16))
    per = mat.pr // tr

    def body(slots_ref, own_ref, a_ref, b_ref, c_ref, *rest):
        o_ref = rest[-1]
        o_ref[...] = ((own_ref[...].astype(F32) + a_ref[...].astype(F32)) + b_ref[...].astype(F32)) + c_ref[...].astype(F32)

    def slot_spec(which):
        return pl.BlockSpec((None, tr, mat.pc), lambda r, slots_ref: (slots_ref[which], r, 0))

    in_specs = [slot_spec(0), slot_spec(1), slot_spec(2), slot_spec(3)]
    operands = [slots, partial, landed, landed, landed]
    aliases = {}
    if layer is None:
        o_spec = pl.BlockSpec((tr, mat.pc), lambda r, slots_ref: (slots_ref[4] * per + r, 0))
        out_shape = jax.ShapeDtypeStruct((mat.sr, mat.sc), F32)
    else:
        o_spec = pl.BlockSpec((None, tr, mat.pc), lambda r, slots_ref: (layer, slots_ref[4] * per + r, 0))
        out_shape = jax.ShapeDtypeStruct((n_layers, mat.sr, mat.sc), F32)
        if stack is not None:
            in_specs.append(_ANY)
            operands.append(stack)
            aliases = {len(operands) - 1: 0}
    return pl.pallas_call(
        body, name=name,
        grid_spec=pltpu.PrefetchScalarGridSpec(num_scalar_prefetch=1, grid=(per,), in_specs=in_specs, out_specs=o_spec),
        out_shape=out_shape, input_output_aliases=aliases,
        compiler_params=_cparams("parallel"),
    )(*operands)


def _share_pieces(name, mats, shards, groups):
    n = len(mats)
    n_out = len(groups)

    def body(*refs):
        out_refs = refs[n_out:2 * n_out]
        send_sems, recv_sems = refs[2 * n_out:]
        x, y, c, _ = _place()
        sibling = (x, y, 1 - c)
        sent, waits = [], []
        for o, members in enumerate(groups):
            for l, m in enumerate(members):
                dst = out_refs[o].at[l] if len(members) > 1 else out_refs[o]
                mine = mats[m].half(dst, c)
                sent.append(pltpu.make_async_remote_copy(src_ref=mine, dst_ref=mine, send_sem=send_sems.at[m],
                                                         recv_sem=recv_sems.at[m], device_id=sibling, device_id_type=MESH_ID))
                theirs = mats[m].half(dst, 1 - c)
                waits.append(pltpu.make_async_remote_copy(src_ref=theirs, dst_ref=theirs, send_sem=send_sems.at[m],
                                                          recv_sem=recv_sems.at[m], device_id=sibling, device_id_type=MESH_ID))
        for cp in sent:
            cp.start()
        for cp in waits:
            cp.wait_recv()
        for cp in sent:
            cp.wait_send()

    return pl.pallas_call(
        body, name=name,
        in_specs=[_ANY] * n_out, out_specs=[_ANY] * n_out,
        out_shape=[jax.ShapeDtypeStruct(s.shape, F32) for s in shards],
        input_output_aliases={o: o for o in range(n_out)},
        scratch_shapes=[pltpu.SemaphoreType.DMA((n,)), pltpu.SemaphoreType.DMA((n,))],
    )(*shards)


def _share_start(name, mats, shards, items):
    n_out = len(shards)

    def body(*refs):
        out_refs = refs[n_out:2 * n_out]
        send_sem, recv_sem, token = refs[2 * n_out], refs[2 * n_out + 1], refs[2 * n_out + 2]
        x, y, c, _ = _place()
        for o, members in enumerate(items):
            for layer, m in members:
                dst = out_refs[o] if layer is None else out_refs[o].at[layer]
                mine = mats[m].half(dst, c)
                _copy_to(mine, mine, send_sem, recv_sem, (x, y, 1 - c)).start()
        token[...] = jnp.zeros_like(token)

    outs = pl.pallas_call(
        body, name=name,
        in_specs=[_HBM] * n_out, out_specs=[_HBM] * n_out + [_SEM, _SEM, _TOKEN_SPEC],
        out_shape=[pltpu.HBM(s.shape, F32) for s in shards] + [pltpu.SemaphoreType.DMA(())] * 2 + [_TOKEN],
        input_output_aliases={o: o for o in range(n_out)},
        compiler_params=_split_params(),
    )(*[_in_hbm(s) for s in shards])
    return list(outs[:n_out]), (outs[n_out], outs[n_out + 1]), outs[n_out + 2]


def _share_wait(name, mats, shards, items, sems, after):
    n_out = len(shards)

    def body(*refs):
        bufs = refs[:n_out]
        send_sem, recv_sem = refs[n_out], refs[n_out + 1]
        x, y, c, _ = _place()
        for o, members in enumerate(items):
            for layer, m in members:
                dst = bufs[o] if layer is None else bufs[o].at[layer]
                theirs = mats[m].half(dst, 1 - c)
                _copy_to(theirs, theirs, send_sem, recv_sem, (x, y, 1 - c)).wait_recv()
        for o, members in enumerate(items):
            for layer, m in members:
                dst = bufs[o] if layer is None else bufs[o].at[layer]
                mine = mats[m].half(dst, c)
                _copy_to(mine, mine, send_sem, recv_sem, (x, y, 1 - c)).wait_send()

    outs = pl.pallas_call(
        body, name=name,
        in_specs=[_HBM] * n_out + [_SEM, _SEM, _ANY], out_specs=[_HBM] * n_out,
        out_shape=[pltpu.HBM(s.shape, F32) for s in shards],
        input_output_aliases={o: o for o in range(n_out)},
        compiler_params=_split_params(),
    )(*[_in_hbm(s) for s in shards], sems[0], sems[1], after)
    return list(outs)


def _sum_devices(stacked):
    nd, r, c = stacked.shape

    def body(s_ref, o_ref):
        s = s_ref[0]
        for k in range(1, nd):
            s = s + s_ref[k]
        o_ref[...] = s

    return pl.pallas_call(
        body, name="sum_small_grads", grid=(1,),
        in_specs=[pl.BlockSpec((nd, r, c), lambda i: (0, 0, 0))],
        out_specs=pl.BlockSpec((r, c), lambda i: (0, 0)),
        out_shape=jax.ShapeDtypeStruct((r, c), F32),
        compiler_params=_cparams("arbitrary"),
    )(stacked)


def _pack(arrs):
    flat = jnp.concatenate([a.reshape(-1) for a in arrs])
    rows = -(-flat.shape[0] // (8 * LANES)) * 8
    return jnp.pad(flat, (0, rows * LANES - flat.shape[0])).reshape(rows, LANES)


def _unpack(packed, shapes):
    flat = packed.reshape(-1)
    out, at = [], 0
    for s in shapes:
        size = 1
        for dim in s:
            size *= dim
        out.append(flat[at:at + size].reshape(s))
        at += size
    return out


def kernel(x, mix_norm_e, w_in_e, conv_w_e, conv_b_e, ln_g_e, ln_b_e, w_pool_e, pool_scale_e, w_out_e, mix_norm_o, w_in_o, conv_w_o, w_out_o, ffn_norm, w_gate, w_up, w_down, final_norm, loss_target, m_mix_norm_e, m_w_in_e, m_conv_w_e, m_conv_b_e, m_ln_g_e, m_ln_b_e, m_w_pool_e, m_pool_scale_e, m_w_out_e, m_mix_norm_o, m_w_in_o, m_conv_w_o, m_w_out_o, m_ffn_norm, m_w_gate, m_w_up, m_w_down, m_final_norm, v_mix_norm_e, v_w_in_e, v_conv_w_e, v_conv_b_e, v_ln_g_e, v_ln_b_e, v_w_pool_e, v_pool_scale_e, v_w_out_e, v_mix_norm_o, v_w_in_o, v_conv_w_o, v_w_out_o, v_ffn_norm, v_w_gate, v_w_up, v_w_down, v_final_norm):
    bsz, seq_len, d = x.shape
    t = bsz * seq_len
    depth = ffn_norm.shape[0]
    assert depth == 2 and conv_b_e.shape[1] == pool_scale_e.shape[1]
    ts = _pick(seq_len, (256, 128, 64, 32))
    me_k = 2 * lax.axis_index("x") + lax.axis_index("y")
    core = lax.axis_index("c").astype(jnp.int32).reshape(1)

    mat_src = [("col", w_in_e, 0), ("row", w_out_e, 0), ("col", w_gate, 0), ("col", w_up, 0), ("row", w_down, 0),
               ("col", w_in_o, 0), ("row", w_out_o, 0), ("col", w_gate, 1), ("col", w_up, 1), ("row", w_down, 1)]
    mats = [_Mat(kind, w.shape[1:]) for kind, w, _ in mat_src]
    n_pool = w_pool_e.shape[1]
    pool_mats = tuple(range(len(mats), len(mats) + n_pool))
    mats = mats + [_Mat("row", w_pool_e.shape[2:])] * n_pool
    chip = me_k.astype(jnp.int32).reshape(1)
    small_shards = [conv_w_e[0], w_pool_e[0], mix_norm_o, conv_w_o[0]]
    packed_small = _pack(small_shards)

    chain = [()]

    def seq(fn, *args, **kw):
        out = fn(*args, after=chain[0], **kw)
        chain[0] = (out[0] if isinstance(out, (list, tuple)) else out,)
        return out

    def mm(*args, **kw):
        return seq(_mm, *args, **kw)

    gather_groups = [(0,), (1,), (2, 3), (4,), (5, 6), (7, 8), (9,)]
    fulls, gather_sems, small_all = [None] * len(mats), [], None
    for g, ms in enumerate(gather_groups):
        own16 = [seq(_cast_into_full, "cast_w%d" % m, mats[m], mat_src[m][1], mat_src[m][2], chip) for m in ms]
        sent, landing, sems, token = _gather_start("gather_start%d" % g, [mats[m] for m in ms], own16,
                                                   _place_small(packed_small, chip) if g == 0 else None)
        chain[0] = (token,)
        for m, f in zip(ms, sent):
            fulls[m] = f
        gather_sems.append(sems)
        if g == 0:
            small_all = landing

    passed = {}

    def gather_pass(g):
        ms = gather_groups[g]
        bufs, pass_sems, token = _gather_pass("gather_pass%d" % g, [mats[m] for m in ms], [fulls[m] for m in ms],
                                              small_all if g == 0 else None, gather_sems[g], chain[0][0])
        chain[0] = (token,)
        passed[g] = (bufs, pass_sems)

    def gather_done(g):
        ms = gather_groups[g]
        bufs, pass_sems = passed[g]
        done = _gather_done("gather_done%d" % g, [mats[m] for m in ms], bufs[:len(ms)], pass_sems, chain[0][0])
        chain[0] = (done[0],)
        return done + bufs[len(ms):]

    h0 = x.reshape(t, d)
    target = loss_target.reshape(t, d)
    gather_pass(0)
    n1 = seq(_rms_fwd, "mix0_norm", h0, mix_norm_e)
    W_in_e, small_all = gather_done(0)
    per_chip = [_unpack(small_all[k], [s.shape for s in small_shards]) for k in range(N_CHIPS)]
    conv_w_e_f = jnp.concatenate([p[0] for p in per_chip], axis=1)
    w_pool_f = jnp.concatenate([p[1] for p in per_chip], axis=1)
    mix_norm_o_f = jnp.concatenate([p[2] for p in per_chip], axis=1)
    conv_w_o_f = jnp.concatenate([p[3] for p in per_chip], axis=1)
    W_gate, W_up, W_down = [None, None], [None, None], [None, None]

    (u_e,) = mm("mix0_in", [(n1, W_in_e)], "nn", [F32], _ep_store, tm=1024, tn=1024, tk=2048)
    gather_pass(1)
    a2, cat = seq(_mixer_e_fwd, u_e, conv_w_e_f, conv_b_e, ln_g_e, ln_b_e, w_pool_f, pool_scale_e, seq_len, ts)
    (W_out_e,) = gather_done(1)
    (h1,) = mm("mix0_out", [(cat, W_out_e)], "nn", [F32], _ep_residual, extras=(h0,), tm=1024, tn=1024, tk=2048)
    gather_pass(2)
    n2 = seq(_rms_fwd, "ffn0_norm", h1, ffn_norm[0:1])
    W_gate[0], W_up[0] = gather_done(2)
    gt0, up0, act0 = mm("ffn0_gate_up", [(n2, W_gate[0]), (n2, W_up[0])], "nn", [BF16] * 3, _ep_swiglu,
                        acc_of=(0, 1), tm=1024, tn=512, tk=2048)
    gather_pass(3)
    (W_down[0],) = gather_done(3)
    gather_pass(4)
    (h2,) = mm("ffn0_down", [(act0, W_down[0])], "nn", [F32], _ep_residual, extras=(h1,), tm=512, tn=1024, tk=5632)
    n3 = seq(_rms_fwd, "mix1_norm", h2, mix_norm_o_f)
    W_in_o, W_out_o = gather_done(4)
    (u_o,) = mm("mix1_in", [(n3, W_in_o)], "nn", [BF16], _ep_store, tm=1024, tn=1024, tk=2048)
    gather_pass(5)
    y_o = seq(_mixer_o_fwd, u_o, conv_w_o_f, seq_len, ts)
    (h3,) = mm("mix1_out", [(y_o, W_out_o)], "nn", [F32], _ep_residual, extras=(h2,), tm=1024, tn=1024, tk=2048)
    gather_pass(6)
    n4 = seq(_rms_fwd, "ffn1_norm", h3, ffn_norm[1:2])
    W_gate[1], W_up[1] = gather_done(5)
    gt1, up1, act1 = mm("ffn1_gate_up", [(n4, W_gate[1]), (n4, W_up[1])], "nn", [BF16] * 3, _ep_swiglu,
                        acc_of=(0, 1), tm=1024, tn=512, tk=2048)
    (W_down[1],) = gather_done(6)
    (h4,) = mm("ffn1_down", [(act1, W_down[1])], "nn", [F32], _ep_residual, extras=(h3,), tm=512, tn=1024, tk=5632)
    loss_part, dh4, dh4b, d_final_norm = seq(_loss_head, h4, final_norm.reshape(1, d), target)
    loss = lax.psum(loss_part[0, 0], AXES)

    in_flight = {}
    partials, scattered = [None] * len(mats), [None] * len(mats)

    def reduce_begin(tag, ms, grads):
        gm = [mats[m] for m in ms]
        grads, landed, sems, token = _exchange_start("exchange_start_" + tag, gm, grads)
        chain[0] = (token,)
        in_flight[tag] = (ms, gm, grads, landed, sems)

    def reduce_advance(tag):
        ms, gm, grads, landed, sems = in_flight[tag]
        grads, landed = _exchange_wait("exchange_wait_" + tag, gm, grads, landed, sems, chain[0][0])
        parts = [_add_halves("add_halves%d" % m, mats[m], g, l, core) for m, g, l in zip(ms, grads, landed)]
        parts, lands, sems, token = _scatter_start("scatter_start_" + tag, gm, parts)
        chain[0] = (token,)
        in_flight[tag] = (ms, gm, parts, lands, sems)

    def reduce_finish(tag):
        ms, gm, parts, lands, sems = in_flight[tag]
        parts, lands = _scatter_wait("scatter_wait_" + tag, gm, parts, lands, sems, chain[0][0])
        chain[0] = (lands[0],)
        for m, p, l in zip(ms, parts, lands):
            partials[m], scattered[m] = p, l

    xi, yi, ci = lax.axis_index("x"), lax.axis_index("y"), lax.axis_index("c")
    slots = jnp.stack([me_k, 2 * (1 - xi) + yi, 2 * xi + (1 - yi), 2 * (1 - xi) + (1 - yi), ci]).astype(jnp.int32)
    shards = {}

    def sum_into(name, m, layer=None, n_layers=1):
        if layer is None:
            shards[name] = _sum_chips("sum_chips%d" % m, mats[m], partials[m], scattered[m], slots)
        else:
            shards[name] = _sum_chips("sum_chips%d" % m, mats[m], partials[m], scattered[m], slots, layer=layer,
                                      stack=shards.get(name), n_layers=n_layers)

    sharing = {}

    def share_begin(tag, names, items):
        arrays, sems, token = _share_start("share_start_" + tag, mats, [shards[nm] for nm in names], items)
        chain[0] = (token,)
        sharing[tag] = (names, items, arrays, sems)

    def share_end(tag):
        names, items, arrays, sems = sharing[tag]
        arrays = _share_wait("share_wait_" + tag, mats, arrays, items, sems, chain[0][0])
        chain[0] = (arrays[0],)
        for nm, a in zip(names, arrays):
            shards[nm] = a

    def ffn_bwd(l, dhb, n, gt, up, act, mid=None):
        dgt, dup = mm("ffn%d_dact" % l, [(dhb, W_down[l])], "nt", [BF16, BF16], _ep_swiglu_bwd, extras=(gt, up),
                      tm=1024, tn=512, tk=2048, row_chunk=256)
        (dW_down,) = mm("ffn%d_dw_down" % l, [(act, dhb)], "tn", [BF16], _ep_store, tm=512, tn=1024, tk=4096)
        if mid is not None:
            mid()
        (dn,) = mm("ffn%d_dn" % l, [(dgt, W_gate[l]), (dup, W_up[l])], "nt", [BF16], _ep_store,
                   tm=1024, tn=1024, tk=1408)
        (dW_gate,) = mm("ffn%d_dw_gate" % l, [(n, dgt)], "tn", [BF16], _ep_store, tm=1024, tn=512, tk=4096)
        (dW_up,) = mm("ffn%d_dw_up" % l, [(n, dup)], "tn", [BF16], _ep_store, tm=1024, tn=512, tk=4096)
        return dn, dW_gate, dW_up, dW_down

    dn4, dW_gate1, dW_up1, dW_down1 = ffn_bwd(1, dh4b, n4, gt1, up1, act1)
    reduce_begin("ffn1", (7, 8, 9), [dW_gate1, dW_up1, dW_down1])
    dh3, dh3b, d_ffn_norm1 = seq(_rms_bwd, "ffn1_norm_bwd", dn4, h3, ffn_norm[1:2], dh4)
    (dy_o,) = mm("mix1_dy", [(dh3b, W_out_o)], "nt", [BF16], _ep_store, tm=1024, tn=1024, tk=2048)
    reduce_advance("ffn1")
    (dW_out_o,) = mm("mix1_dw_out", [(y_o, dh3b)], "tn", [BF16], _ep_store, tm=1024, tn=1024, tk=4096)
    du_o, d_conv_w_o = seq(_mixer_o_bwd, dy_o, u_o, conv_w_o_f, seq_len, ts)
    (dW_in_o,) = mm("mix1_dw_in", [(n3, du_o)], "tn", [BF16], _ep_store, tm=1024, tn=1024, tk=4096)
    reduce_begin("mix1", (5, 6), [dW_in_o, dW_out_o])
    (dn3,) = mm("mix1_dn", [(du_o, W_in_o)], "nt", [BF16], _ep_store, tm=1024, tn=1024, tk=3072)
    reduce_advance("mix1")
    dh2, dh2b, d_mix_norm_o = seq(_rms_bwd, "mix1_norm_bwd", dn3, h2, mix_norm_o_f, dh3)

    def finish_layer1():
        reduce_finish("ffn1")
        reduce_finish("mix1")
        sum_into("w_in_o", 5)
        sum_into("w_out_o", 6)
        for nm, m in (("w_gate", 7), ("w_up", 8), ("w_down", 9)):
            sum_into(nm, m, layer=1, n_layers=2)
        share_begin("layer1", ["w_in_o", "w_out_o", "w_gate", "w_up", "w_down"],
                    [[(None, 5)], [(None, 6)], [(1, 7)], [(1, 8)], [(1, 9)]])

    dn2, dW_gate0, dW_up0, dW_down0 = ffn_bwd(0, dh2b, n2, gt0, up0, act0, mid=finish_layer1)
    reduce_begin("ffn0", (2, 3, 4), [dW_gate0, dW_up0, dW_down0])
    dh1, dh1b, d_ffn_norm0 = seq(_rms_bwd, "ffn0_norm_bwd", dn2, h1, ffn_norm[0:1], dh2)
    (dcat,) = mm("mix0_dcat", [(dh1b, W_out_e)], "nt", [F32], _ep_store, tm=1024, tn=1024, tk=2048)
    reduce_advance("ffn0")
    (dW_out_e,) = mm("mix0_dw_out", [(cat, dh1b)], "tn", [BF16], _ep_store, tm=1024, tn=1024, tk=4096)
    da2, d_ln_g, d_ln_b, d_conv_b = seq(_mixer_e_bwd_norm, dcat, a2, ln_g_e, ln_b_e, ts)
    du_e, d_conv_w_e, d_w_pool, d_pool_scale = seq(_mixer_e_bwd_mix, da2, dcat, u_e, conv_w_e_f, w_pool_f, pool_scale_e,
                                                   seq_len, ts)
    (dW_in_e,) = mm("mix0_dw_in", [(n1, du_e)], "tn", [BF16], _ep_store, tm=1024, tn=1024, tk=4096)
    reduce_begin("mix0", (0, 1) + pool_mats, [dW_in_e, dW_out_e] + [d_w_pool[g].astype(BF16) for g in range(n_pool)])
    (dn1,) = mm("mix0_dn", [(du_e, W_in_e)], "nt", [BF16], _ep_store, tm=1024, tn=1024, tk=3072)
    dx, _, d_mix_norm_e = seq(_rms_bwd, "mix0_norm_bwd", dn1, h0, mix_norm_e, dh1)
    reduce_advance("mix0")

    d_ffn_norm = jnp.concatenate([d_ffn_norm0, d_ffn_norm1], axis=0)
    small_partials = [d_mix_norm_e, d_conv_w_e, d_conv_b, d_ln_g, d_ln_b, d_pool_scale, d_mix_norm_o, d_conv_w_o,
                      d_ffn_norm, d_final_norm]
    packed_grads, small_stack, small_sems, token = _small_start(_pack(small_partials), chain[0][0])
    chain[0] = (token,)
    share_end("layer1")
    reduce_finish("ffn0")
    for nm, m in (("w_gate", 2), ("w_up", 3), ("w_down", 4)):
        sum_into(nm, m, layer=0, n_layers=2)
    share_begin("layer0", ["w_gate", "w_up", "w_down"], [[(0, 2)], [(0, 3)], [(0, 4)]])
    grad = {"w_in_o": shards["w_in_o"][None], "w_out_o": shards["w_out_o"][None]}
    weights = dict(mix_norm_e=mix_norm_e, w_in_e=w_in_e, conv_w_e=conv_w_e, conv_b_e=conv_b_e, ln_g_e=ln_g_e, ln_b_e=ln_b_e,
                   w_pool_e=w_pool_e, pool_scale_e=pool_scale_e, w_out_e=w_out_e, mix_norm_o=mix_norm_o, w_in_o=w_in_o,
                   conv_w_o=conv_w_o, w_out_o=w_out_o, ffn_norm=ffn_norm, w_gate=w_gate, w_up=w_up, w_down=w_down,
                   final_norm=final_norm)
    mom1 = dict(mix_norm_e=m_mix_norm_e, w_in_e=m_w_in_e, conv_w_e=m_conv_w_e, conv_b_e=m_conv_b_e, ln_g_e=m_ln_g_e,
                ln_b_e=m_ln_b_e, w_pool_e=m_w_pool_e, pool_scale_e=m_pool_scale_e, w_out_e=m_w_out_e, mix_norm_o=m_mix_norm_o,
                w_in_o=m_w_in_o, conv_w_o=m_conv_w_o, w_out_o=m_w_out_o, ffn_norm=m_ffn_norm, w_gate=m_w_gate, w_up=m_w_up,
                w_down=m_w_down, final_norm=m_final_norm)
    mom2 = dict(mix_norm_e=v_mix_norm_e, w_in_e=v_w_in_e, conv_w_e=v_conv_w_e, conv_b_e=v_conv_b_e, ln_g_e=v_ln_g_e,
                ln_b_e=v_ln_b_e, w_pool_e=v_w_pool_e, pool_scale_e=v_pool_scale_e, w_out_e=v_w_out_e, mix_norm_o=v_mix_norm_o,
                w_in_o=v_w_in_o, conv_w_o=v_conv_w_o, w_out_o=v_w_out_o, ffn_norm=v_ffn_norm, w_gate=v_w_gate, w_up=v_w_up,
                w_down=v_w_down, final_norm=v_final_norm)
    names = list(weights)

    big = ("w_in_o", "w_out_o", "w_gate", "w_up", "w_down", "w_in_e", "w_out_e")
    delta, new_m, new_v = {}, {}, {}

    def update(nm):
        shape = weights[nm].shape
        rows = 1
        for dim in shape[:-1]:
            rows *= dim
        as2d = lambda a: a.reshape(rows, shape[-1])
        dl, mn, vn = seq(_adamw, "adamw_" + nm, as2d(weights[nm]), as2d(grad[nm]), as2d(mom1[nm]), as2d(mom2[nm]))
        delta[nm], new_m[nm], new_v[nm] = dl.reshape(shape), mn.reshape(shape), vn.reshape(shape)

    for nm in big[:2]:
        update(nm)
    share_end("layer0")
    for nm in big[2:5]:
        grad[nm] = shards[nm]
        update(nm)
    reduce_finish("mix0")
    sum_into("w_in_e", 0)
    sum_into("w_out_e", 1)
    for layer, m in enumerate(pool_mats):
        sum_into("w_pool_e", m, layer=layer, n_layers=n_pool)
    g_w_in_e, g_w_out_e, g_w_pool = _share_pieces("share_pieces_first", mats,
                                                  [shards["w_in_e"], shards["w_out_e"], shards["w_pool_e"]],
                                                  [(0,), (1,), pool_mats])
    grad["w_in_e"], grad["w_out_e"], grad["w_pool_e"] = g_w_in_e[None], g_w_out_e[None], g_w_pool[None]
    for nm in big[5:]:
        update(nm)

    packed_grads, small_stack = _small_wait(packed_grads, small_stack, small_sems, chain[0][0])
    me_dev = 4 * xi + 2 * yi + ci
    small_stack = jnp.where(lax.broadcasted_iota(jnp.int32, (N_DEV, 1, 1), 0) == me_dev, packed_grads[None], small_stack)
    small_sum = _unpack(_sum_devices(small_stack), [s.shape for s in small_partials])
    (g_mix_norm_e, g_conv_w_e_f, g_conv_b, g_ln_g, g_ln_b, g_pool_scale, g_mix_norm_o_f, g_conv_w_o_f,
     g_ffn_norm, g_final_norm) = small_sum

    def my_shard(full, axis):
        size = full.shape[axis] // N_CHIPS
        return lax.dynamic_slice_in_dim(full, me_k * size, size, axis)

    grad.update({
        "mix_norm_e": g_mix_norm_e, "conv_w_e": my_shard(g_conv_w_e_f, 1)[None], "conv_b_e": g_conv_b,
        "ln_g_e": g_ln_g, "ln_b_e": g_ln_b, "pool_scale_e": g_pool_scale,
        "mix_norm_o": my_shard(g_mix_norm_o_f, 1), "conv_w_o": my_shard(g_conv_w_o_f, 1)[None],
        "ffn_norm": g_ffn_norm, "final_norm": g_final_norm.reshape(final_norm.shape),
    })
    small = [nm for nm in names if nm not in big]
    shapes = [weights[nm].shape for nm in small]
    dl, mn, vn = _adamw("adamw_small", _pack([weights[nm] for nm in small]), _pack([grad[nm] for nm in small]),
                        _pack([mom1[nm] for nm in small]), _pack([mom2[nm] for nm in small]))
    for nm, a, b, c_ in zip(small, _unpack(dl, shapes), _unpack(mn, shapes), _unpack(vn, shapes)):
        delta[nm], new_m[nm], new_v[nm] = a, b, c_

    grad_x = dx.reshape(bsz, seq_len, d)
    return (loss, grad_x, *[grad[nm] for nm in names], *[delta[nm] for nm in names],
            *[new_m[nm] for nm in names], *[new_v[nm] for nm in names])
```

```python
import jax
import jax.numpy as jnp
from jax import lax
from jax.experimental import pallas as pl
from jax.experimental.pallas import tpu as pltpu

F32 = jnp.float32
BF16 = jnp.bfloat16
MESH_ID = pl.DeviceIdType.MESH
AXES = ("x", "y", "c")
N_CHIPS = 4
N_DEV = 8

EPS = 1e-6
POOL_WINDOWS = (2, 4, 8, 16)
ADAM_LR, ADAM_B1, ADAM_B2, ADAM_EPS, ADAM_WD, ADAM_STEP = 0.001, 0.9, 0.999, 1e-08, 0.01, 10

LANES = 128
CONV_HALO = 32
POOL_HALO = 16
SHORT_HALO = 16
V7X_VMEM_LIMIT = 56 * 1024 * 1024


def _cparams(*sem):
    return pltpu.CompilerParams(dimension_semantics=sem if sem else None, vmem_limit_bytes=V7X_VMEM_LIMIT)


def _pick(dim, prefs):
    for p in prefs:
        if p <= dim and dim % p == 0:
            return p
    return dim


def _sigmoid(x):
    return jax.nn.sigmoid(x)


_ANY = pl.BlockSpec(memory_space=pl.ANY)


def _behind(after, body, n_in):
    if not after:
        return body
    skip = len(after)

    def body_behind(*refs):
        return body(*refs[:n_in], *refs[n_in + skip:])

    return body_behind


_DOT_DIMS = {
    "nn": (((1,), (0,)), ((), ())),
    "nt": (((1,), (1,)), ((), ())),
    "tn": (((0,), (0,)), ((), ())),
}


def _mm(name, pairs, mode, out_dtypes, epilogue, extras=(), acc_of=None, tm=512, tn=512, tk=2048, row_chunk=0, after=()):
    a0, b0 = pairs[0]
    if mode == "nn":
        (m, k), n = a0.shape, b0.shape[1]
    elif mode == "nt":
        (m, k), n = a0.shape, b0.shape[0]
    else:
        (k, m), n = a0.shape, b0.shape[1]
    tm = _pick(m, (tm, 512, 256, 128, 64, 32, 16, 8))
    tn = _pick(n, (tn, 512, 256, 128))
    tk = _pick(k, (tk, 2048, 1024, 512, 256, 128))
    nk = k // tk
    n_pairs = len(pairs)
    acc_of = tuple(acc_of) if acc_of is not None else (0,) * n_pairs
    n_acc = max(acc_of) + 1
    n_ex, n_out = len(extras), len(out_dtypes)
    dims = _DOT_DIMS[mode]

    def body(*refs):
        a_refs = refs[:n_pairs]
        b_refs = refs[n_pairs:2 * n_pairs]
        e_refs = refs[2 * n_pairs:2 * n_pairs + n_ex]
        first_out = 2 * n_pairs + n_ex + len(after)
        o_refs = refs[first_out:first_out + n_out]
        acc_refs = refs[first_out + n_out:]

        def partial_sums(rows=None):
            sums = [None] * n_acc
            for p in range(n_pairs):
                a = a_refs[p][...] if rows is None else (a_refs[p][:, rows] if mode == "tn" else a_refs[p][rows, :])
                d = lax.dot_general(a, b_refs[p][...], dims, preferred_element_type=F32)
                sums[acc_of[p]] = d if sums[acc_of[p]] is None else sums[acc_of[p]] + d
            return sums

        if nk == 1 and row_chunk:
            for r0 in range(0, tm, row_chunk):
                rows = pl.ds(r0, row_chunk)
                epilogue(partial_sums(rows), [e.at[rows, :] for e in e_refs], [o.at[rows, :] for o in o_refs])
            return
        if nk == 1:
            epilogue(partial_sums(), e_refs, o_refs)
            return
        kk = pl.program_id(2)

        @pl.when(kk == 0)
        def _():
            for acc in acc_refs:
                acc[...] = jnp.zeros_like(acc)

        for acc, s in zip(acc_refs, partial_sums()):
            acc[...] += s

        @pl.when(kk == nk - 1)
        def _():
            epilogue([acc[...] for acc in acc_refs], e_refs, o_refs)

    if mode == "nn":
        a_spec = pl.BlockSpec((tm, tk), lambda i, j, kk: (i, kk))
        b_spec = pl.BlockSpec((tk, tn), lambda i, j, kk: (kk, j))
    elif mode == "nt":
        a_spec = pl.BlockSpec((tm, tk), lambda i, j, kk: (i, kk))
        b_spec = pl.BlockSpec((tn, tk), lambda i, j, kk: (j, kk))
    else:
        a_spec = pl.BlockSpec((tk, tm), lambda i, j, kk: (kk, i))
        b_spec = pl.BlockSpec((tk, tn), lambda i, j, kk: (kk, j))
    o_spec = pl.BlockSpec((tm, tn), lambda i, j, kk: (i, j))
    outs = pl.pallas_call(
        body,
        name=name,
        grid=(m // tm, n // tn, nk),
        in_specs=[a_spec] * n_pairs + [b_spec] * n_pairs + [o_spec] * n_ex
        + [pl.BlockSpec(memory_space=pl.ANY)] * len(after),
        out_specs=[o_spec] * n_out,
        out_shape=[jax.ShapeDtypeStruct((m, n), dt) for dt in out_dtypes],
        scratch_shapes=[pltpu.VMEM((tm, tn), F32) for _ in range(n_acc)] if nk > 1 else [],
        compiler_params=_cparams("parallel", "parallel", "arbitrary"),
    )(*[p[0] for p in pairs], *[p[1] for p in pairs], *extras, *after)
    return outs


def _mm_rows(name, pairs, mode, out_dtypes, epilogue, extras=(), vecs=(), n_sums=0, tm=512, tk=2048, row_chunk=128,
             chunk_dots=True, after=()):
    a0, b0 = pairs[0]
    (m, k), n = a0.shape, (b0.shape[1] if mode == "nn" else b0.shape[0])
    tm = _pick(m, (tm, 512, 256, 128, 64, 32, 16, 8))
    tk = _pick(k, (tk, 2048, 1024, 512, 256, 128))
    row_chunk = min(row_chunk, tm)
    nk = k // tk
    n_pairs, n_ex, n_vec, n_out = len(pairs), len(extras), len(vecs), len(out_dtypes)
    dims = _DOT_DIMS[mode]

    def body(*refs):
        a_refs = refs[:n_pairs]
        b_refs = refs[n_pairs:2 * n_pairs]
        e_refs = refs[2 * n_pairs:2 * n_pairs + n_ex]
        v_refs = refs[2 * n_pairs + n_ex:2 * n_pairs + n_ex + n_vec]
        first_out = 2 * n_pairs + n_ex + n_vec + len(after)
        o_refs = refs[first_out:first_out + n_out]
        s_refs = refs[first_out + n_out:first_out + n_out + n_sums]
        acc_refs = refs[first_out + n_out + n_sums:]
        i, kk = pl.program_id(0), pl.program_id(1)

        @pl.when((i == 0) & (kk == 0))
        def _():
            for s in s_refs:
                s[...] = jnp.zeros_like(s)

        def dots(rows):
            total = None
            for p in range(n_pairs):
                d = lax.dot_general(a_refs[p][rows, :], b_refs[p][...], dims, preferred_element_type=F32)
                total = d if total is None else total + d
            return total

        def finish(acc_of_rows):
            for r0 in range(0, tm, row_chunk):
                rows = pl.ds(r0, row_chunk)
                epilogue(acc_of_rows(rows), [e.at[rows, :] for e in e_refs], v_refs, [o.at[rows, :] for o in o_refs], s_refs)

        if nk == 1 and chunk_dots:
            finish(dots)
            return
        acc = acc_refs[0]
        if nk == 1:
            acc[...] = dots(slice(None))
            finish(lambda rows: acc[rows, :])
            return

        @pl.when(kk == 0)
        def _():
            acc[...] = jnp.zeros_like(acc)

        acc[...] += dots(slice(None))

        @pl.when(kk == nk - 1)
        def _():
            finish(lambda rows: acc[rows, :])

    a_spec = pl.BlockSpec((tm, tk), lambda i, kk: (i, kk))
    b_mode = dict(pipeline_mode=pl.Buffered(1)) if nk == 1 else {}
    b_spec = (pl.BlockSpec((tk, n), lambda i, kk: (kk, 0), **b_mode) if mode == "nn"
              else pl.BlockSpec((n, tk), lambda i, kk: (0, kk), **b_mode))
    row_spec = pl.BlockSpec((tm, n), lambda i, kk: (i, 0))
    vec_spec = pl.BlockSpec((1, n), lambda i, kk: (0, 0))
    return pl.pallas_call(
        body, name=name, grid=(m // tm, nk),
        in_specs=[a_spec] * n_pairs + [b_spec] * n_pairs + [row_spec] * n_ex + [vec_spec] * n_vec + [_ANY] * len(after),
        out_specs=[row_spec] * n_out + [vec_spec] * n_sums,
        out_shape=[jax.ShapeDtypeStruct((m, n), dt) for dt in out_dtypes] + [jax.ShapeDtypeStruct((1, n), F32)] * n_sums,
        scratch_shapes=[pltpu.VMEM((tm, n), F32)] if (nk > 1 or not chunk_dots) else [],
        compiler_params=_cparams("arbitrary", "arbitrary"),
    )(*[p[0] for p in pairs], *[p[1] for p in pairs], *extras, *vecs, *after)


def _ep_rows_residual_norm(acc, ex, vecs, outs, sums):
    h = ex[0][...] + acc
    outs[0][...] = h
    r = lax.rsqrt(jnp.mean(h * h, axis=-1, keepdims=True) + EPS)
    outs[1][...] = (h * r * vecs[0][...]).astype(BF16)


def _ep_rows_norm_bwd(dn, ex, vecs, outs, sums):
    x = ex[0][...]
    r = lax.rsqrt(jnp.mean(x * x, axis=-1, keepdims=True) + EPS)
    xhat = x * r
    sums[0][...] += jnp.sum(dn * xhat, axis=0, keepdims=True)
    dxh = dn * vecs[0][...]
    dh = ex[1][...] + r * (dxh - xhat * jnp.mean(dxh * xhat, axis=-1, keepdims=True))
    outs[0][...] = dh
    if len(outs) > 1:
        outs[1][...] = dh.astype(BF16)


def _ep_store(accs, ex, outs):
    outs[0][...] = accs[0].astype(outs[0].dtype)


def _ep_residual(accs, ex, outs):
    outs[0][...] = ex[0][...] + accs[0]


def _ep_swiglu(accs, ex, outs):
    g, u = accs
    s = _sigmoid(g)
    gs = g * s
    outs[0][...] = (u * (s * (1.0 + g * (1.0 - s)))).astype(BF16)
    outs[1][...] = gs.astype(BF16)
    outs[2][...] = (gs * u).astype(BF16)


def _ep_swiglu_bwd(accs, ex, outs):
    d = accs[0]
    outs[0][...] = (d * ex[0][...].astype(F32)).astype(BF16)
    outs[1][...] = (d * ex[1][...].astype(F32)).astype(BF16)


def _rms_fwd(name, h, g, after=()):
    t, d = h.shape
    tr = _pick(t, (256, 128, 64, 32, 16, 8))

    def body(h_ref, g_ref, *rest):
        o_ref = rest[-1]
        x = h_ref[...]
        r = lax.rsqrt(jnp.mean(x * x, axis=-1, keepdims=True) + EPS)
        o_ref[...] = (x * r * g_ref[...]).astype(BF16)

    return pl.pallas_call(
        body, name=name, grid=(t // tr,),
        in_specs=[pl.BlockSpec((tr, d), lambda i: (i, 0)), pl.BlockSpec((1, d), lambda i: (0, 0))]
        + [pl.BlockSpec(memory_space=pl.ANY)] * len(after),
        out_specs=pl.BlockSpec((tr, d), lambda i: (i, 0)),
        out_shape=jax.ShapeDtypeStruct((t, d), BF16),
        compiler_params=_cparams("parallel"),
    )(h, g, *after)


def _rms_bwd(name, dn, h, g, dres, after=()):
    t, d = h.shape
    tr = _pick(t, (256, 128, 64, 32, 16, 8))

    def body(dn_ref, h_ref, g_ref, dres_ref, dh_ref, dhb_ref, dg_ref):
        x = h_ref[...]
        r = lax.rsqrt(jnp.mean(x * x, axis=-1, keepdims=True) + EPS)
        xhat = x * r
        dnv = dn_ref[...].astype(F32)

        @pl.when(pl.program_id(0) == 0)
        def _():
            dg_ref[...] = jnp.zeros_like(dg_ref)

        dg_ref[...] += jnp.sum(dnv * xhat, axis=0, keepdims=True)
        dxh = dnv * g_ref[...]
        dh = dres_ref[...] + r * (dxh - xhat * jnp.mean(dxh * xhat, axis=-1, keepdims=True))
        dh_ref[...] = dh
        dhb_ref[...] = dh.astype(BF16)

    row = pl.BlockSpec((tr, d), lambda i: (i, 0))
    vec = pl.BlockSpec((1, d), lambda i: (0, 0))
    return pl.pallas_call(
        _behind(after, body, 4), name=name, grid=(t // tr,),
        in_specs=[row, row, vec, row] + [_ANY] * len(after),
        out_specs=[row, row, vec],
        out_shape=[jax.ShapeDtypeStruct((t, d), F32), jax.ShapeDtypeStruct((t, d), BF16),
                   jax.ShapeDtypeStruct((1, d), F32)],
        compiler_params=_cparams("arbitrary"),
    )(dn, h, g, dres, *after)


def _loss_head(h, g, target, after=()):
    t, d = h.shape
    tr = _pick(t, (256, 128, 64, 32, 16, 8))

    def body(h_ref, g_ref, t_ref, loss_ref, dh_ref, dhb_ref, dg_ref):
        x = h_ref[...]
        gv = g_ref[...]
        r = lax.rsqrt(jnp.mean(x * x, axis=-1, keepdims=True) + EPS)
        xhat = x * r
        err = xhat * gv - t_ref[...]

        @pl.when(pl.program_id(0) == 0)
        def _():
            dg_ref[...] = jnp.zeros_like(dg_ref)
            loss_ref[...] = jnp.zeros_like(loss_ref)

        loss_ref[...] += jnp.full(loss_ref.shape, 0.5 / d, F32) * jnp.sum(err * err)
        dy = err * (1.0 / d)
        dg_ref[...] += jnp.sum(dy * xhat, axis=0, keepdims=True)
        dxh = dy * gv
        dh = r * (dxh - xhat * jnp.mean(dxh * xhat, axis=-1, keepdims=True))
        dh_ref[...] = dh
        dhb_ref[...] = dh.astype(BF16)

    row = pl.BlockSpec((tr, d), lambda i: (i, 0))
    vec = pl.BlockSpec((1, d), lambda i: (0, 0))
    return pl.pallas_call(
        _behind(after, body, 3), name="loss_head", grid=(t // tr,),
        in_specs=[row, vec, row] + [_ANY] * len(after),
        out_specs=[pl.BlockSpec((1, LANES), lambda i: (0, 0)), row, row, vec],
        out_shape=[jax.ShapeDtypeStruct((1, LANES), F32), jax.ShapeDtypeStruct((t, d), F32),
                   jax.ShapeDtypeStruct((t, d), BF16), jax.ShapeDtypeStruct((1, d), F32)],
        compiler_params=_cparams("arbitrary"),
    )(h, g, target, *after)


def _cur(ts, width, col):
    return pl.BlockSpec((ts, width), lambda i: (i, col))


def _prev_halo(ts, halo, width, col):
    per = ts // halo
    return pl.BlockSpec((halo, width), lambda i: (jnp.maximum(i * per - 1, 0), col))


def _next_halo(ts, halo, width, col, n_rows):
    per = ts // halo
    last = n_rows // halo - 1
    return pl.BlockSpec((halo, width), lambda i: (jnp.minimum((i + 1) * per, last), col))


def _full(shape):
    nd = len(shape)
    return pl.BlockSpec(shape, lambda i: (0,) * nd)


def _shift_down(x, n):
    return x if n == 0 else pltpu.roll(x, n, 0)


def _shift_up(x, n):
    return x if n == 0 else pltpu.roll(x, x.shape[0] - n, 0)


CONV_ROWS = 32


def _conv_block_shape(channels, ts):
    return min(CONV_ROWS, ts), min(LANES, channels)


SUBLANES = 8


def _fill_shifted(rot_ref, ext):
    rot_ref[0] = ext
    for r in range(1, SUBLANES):
        rot_ref[r] = _shift_up(ext, r)


def _window(rot_ref, first, rows, c0, cw):
    r = first % SUBLANES
    return rot_ref[r, first - r:first - r + rows, c0:c0 + cw]


def _causal_taps(rot_ref, w_ref, halo, taps, r0, c0, rows, cw):
    acc = jnp.zeros((rows, cw), F32)
    for k in range(taps):
        acc = acc + w_ref[k:k + 1, c0:c0 + cw] * _window(rot_ref, halo + r0 - (taps - 1 - k), rows, c0, cw)
    return acc


def _pool_counts(i, ns, ts, w):
    pos = (i % ns) * ts + lax.broadcasted_iota(jnp.int32, (ts, 1), 0)
    return jnp.minimum(pos + 1, w).astype(F32)


def _pooled(cur, prev_tail, w, cnt):
    s = jnp.concatenate([prev_tail, cur], axis=0)
    d = 1
    while d < w:
        s = s + _shift_down(s, d)
        d *= 2
    return s[POOL_HALO:, :] / cnt - cur


def _mixer_e_fwd(u, conv_w, conv_b, ln_g, ln_b, w_pool, scale, seq, ts, after=()):
    t = u.shape[0]
    dc = conv_b.shape[1]
    ng, pg = w_pool.shape[0], w_pool.shape[1]
    taps = conv_w.shape[0]
    ns = seq // ts

    def body(val_ref, gate_ref, b_ref, pval_ref, pgate_ref, pb_ref, cw_ref, cb_ref, g_ref, be_ref, wp_ref, sc_ref,
             a2_ref, cat_ref, rot_ref):
        i = pl.program_id(0)
        keep_prev = jnp.where(i % ns == 0, 0.0, 1.0)
        a1 = val_ref[...] * _sigmoid(gate_ref[...])
        pa1 = pval_ref[...] * _sigmoid(pgate_ref[...]) * keep_prev
        _fill_shifted(rot_ref, jnp.concatenate([pa1, a1], axis=0))
        rows, cw = _conv_block_shape(dc, ts)
        for c0 in range(0, dc, cw):
            for r0 in range(0, ts, rows):
                acc = _causal_taps(rot_ref, cw_ref, CONV_HALO, taps, r0, c0, rows, cw)
                a2_ref[r0:r0 + rows, c0:c0 + cw] = acc + cb_ref[:, c0:c0 + cw]
        a2 = a2_ref[...]
        mu = jnp.mean(a2, axis=-1, keepdims=True)
        xc = a2 - mu
        rstd = lax.rsqrt(jnp.mean(xc * xc, axis=-1, keepdims=True) + EPS)
        a3 = xc * rstd * g_ref[...] + be_ref[...]
        cat_ref[:, 0:dc] = (a3 * _sigmoid(a3)).astype(BF16)
        for g in range(ng):
            lo, hi = g * pg, (g + 1) * pg
            w = POOL_WINDOWS[g]
            p = _pooled(b_ref[:, lo:hi], pb_ref[:, lo:hi] * keep_prev, w, _pool_counts(i, ns, ts, w))
            q = jnp.dot(p.astype(BF16), wp_ref[g].astype(BF16), preferred_element_type=F32)
            cat_ref[:, dc + lo:dc + hi] = (q * sc_ref[:, lo:hi]).astype(BF16)

    return pl.pallas_call(
        _behind(after, body, 12), name="mixer_e_fwd", grid=(t // ts,),
        in_specs=[_cur(ts, dc, 0), _cur(ts, dc, 1), _cur(ts, dc, 2),
                  _prev_halo(ts, CONV_HALO, dc, 0), _prev_halo(ts, CONV_HALO, dc, 1), _prev_halo(ts, POOL_HALO, dc, 2),
                  _full(conv_w.shape), _full(conv_b.shape), _full(ln_g.shape), _full(ln_b.shape),
                  _full(w_pool.shape), _full(scale.shape)] + [_ANY] * len(after),
        out_specs=[_cur(ts, dc, 0), _cur(ts, 2 * dc, 0)],
        out_shape=[jax.ShapeDtypeStruct((t, dc), F32), jax.ShapeDtypeStruct((t, 2 * dc), BF16)],
        scratch_shapes=[pltpu.VMEM((SUBLANES, CONV_HALO + ts, dc), F32)],
        compiler_params=_cparams("parallel"),
    )(u, u, u, u, u, u, conv_w, conv_b, ln_g, ln_b, w_pool, scale, *after)


def _mixer_e_bwd_norm(dcat, a2, ln_g, ln_b, ts, after=()):
    t, dc = a2.shape

    def body(d_ref, a2_ref, g_ref, be_ref, da2_ref, dg_ref, db_ref, dcb_ref):
        x = a2_ref[...]
        gv = g_ref[...]
        mu = jnp.mean(x, axis=-1, keepdims=True)
        xc = x - mu
        rstd = lax.rsqrt(jnp.mean(xc * xc, axis=-1, keepdims=True) + EPS)
        xhat = xc * rstd
        a3 = xhat * gv + be_ref[...]
        sg = _sigmoid(a3)
        da3 = d_ref[...] * (sg * (1.0 + a3 * (1.0 - sg)))
        dxh = da3 * gv
        da2 = rstd * (dxh - jnp.mean(dxh, axis=-1, keepdims=True)
                      - xhat * jnp.mean(dxh * xhat, axis=-1, keepdims=True))
        da2_ref[...] = da2

        @pl.when(pl.program_id(0) == 0)
        def _():
            dg_ref[...] = jnp.zeros_like(dg_ref)
            db_ref[...] = jnp.zeros_like(db_ref)
            dcb_ref[...] = jnp.zeros_like(dcb_ref)

        dg_ref[...] += jnp.sum(da3 * xhat, axis=0, keepdims=True)
        db_ref[...] += jnp.sum(da3, axis=0, keepdims=True)
        dcb_ref[...] += jnp.sum(da2, axis=0, keepdims=True)

    vec = _full((1, dc))
    return pl.pallas_call(
        _behind(after, body, 4), name="mixer_e_bwd_norm", grid=(t // ts,),
        in_specs=[_cur(ts, dc, 0), _cur(ts, dc, 0), vec, vec] + [_ANY] * len(after),
        out_specs=[_cur(ts, dc, 0), vec, vec, vec],
        out_shape=[jax.ShapeDtypeStruct((t, dc), F32)] + [jax.ShapeDtypeStruct((1, dc), F32)] * 3,
        compiler_params=_cparams("arbitrary"),
    )(dcat, a2, ln_g, ln_b, *after)


def _mixer_e_bwd_mix(da2, dcat, u, conv_w, w_pool, scale, seq, ts, after=()):
    t, dc = da2.shape
    ng, pg = w_pool.shape[0], w_pool.shape[1]
    taps = conv_w.shape[0]
    ns = seq // ts

    def body(da2_ref, nda2_ref, dp_ref, ndp_ref, val_ref, gate_ref, b_ref, pval_ref, pgate_ref, pb_ref,
             cw_ref, wp_ref, sc_ref, du_ref, dcw_ref, dwp_ref, dsc_ref, rota_ref, rotd_ref):
        i = pl.program_id(0)
        keep_prev = jnp.where(i % ns == 0, 0.0, 1.0)
        keep_next = jnp.where(i % ns == ns - 1, 0.0, 1.0)

        @pl.when(i == 0)
        def _():
            dcw_ref[...] = jnp.zeros_like(dcw_ref)
            dwp_ref[...] = jnp.zeros_like(dwp_ref)
            dsc_ref[...] = jnp.zeros_like(dsc_ref)

        val = val_ref[...]
        sg = _sigmoid(gate_ref[...])
        a1 = val * sg
        pa1 = pval_ref[...] * _sigmoid(pgate_ref[...]) * keep_prev
        _fill_shifted(rota_ref, jnp.concatenate([pa1, a1], axis=0))
        _fill_shifted(rotd_ref, jnp.concatenate([da2_ref[...], nda2_ref[...] * keep_next], axis=0))
        rows, cw = _conv_block_shape(dc, ts)
        for c0 in range(0, dc, cw):
            lanes = slice(c0, c0 + cw)
            dw = [jnp.zeros((SUBLANES, cw), F32)] * taps
            for r0 in range(0, ts, rows):
                blk = slice(r0, r0 + rows)
                d_blk = da2_ref[blk, lanes]
                da1 = jnp.zeros((rows, cw), F32)
                for k in range(taps):
                    sh = taps - 1 - k
                    prod = d_blk * _window(rota_ref, CONV_HALO + r0 - sh, rows, c0, cw)
                    for f in range(0, rows, SUBLANES):
                        dw[k] = dw[k] + prod[f:f + SUBLANES, :]
                    da1 = da1 + cw_ref[k:k + 1, lanes] * _window(rotd_ref, r0 + sh, rows, c0, cw)
                sg_b = sg[blk, lanes]
                du_ref[blk, lanes] = (da1 * sg_b).astype(BF16)
                du_ref[blk, dc + c0:dc + c0 + cw] = (da1 * a1[blk, lanes] * (1.0 - sg_b)).astype(BF16)
            for k in range(taps):
                dcw_ref[k:k + 1, lanes] += jnp.sum(dw[k], axis=0, keepdims=True)

        for g in range(ng):
            lo, hi = g * pg, (g + 1) * pg
            w = POOL_WINDOWS[g]
            cnt = _pool_counts(i, ns, ts, w)
            wpb = wp_ref[g].astype(BF16)
            sc = sc_ref[:, lo:hi]
            p = _pooled(b_ref[:, lo:hi], pb_ref[:, lo:hi] * keep_prev, w, cnt)
            pb16 = p.astype(BF16)
            q = jnp.dot(pb16, wpb, preferred_element_type=F32)
            dout = dp_ref[:, lo:hi]
            dsc_ref[:, lo:hi] += jnp.sum(dout * q, axis=0, keepdims=True)
            dq = (dout * sc).astype(BF16)
            dwp_ref[g] += lax.dot_general(pb16, dq, _DOT_DIMS["tn"], preferred_element_type=F32)
            dpool = lax.dot_general(dq, wpb, _DOT_DIMS["nt"], preferred_element_type=F32)
            ndq = (ndp_ref[:, lo:hi] * sc * keep_next).astype(BF16)
            ndpool = lax.dot_general(ndq, wpb, _DOT_DIMS["nt"], preferred_element_type=F32)
            s = jnp.concatenate([dpool / cnt, ndpool * (1.0 / w)], axis=0)
            d = 1
            while d < w:
                s = s + _shift_up(s, d)
                d *= 2
            du_ref[:, 2 * dc + lo:2 * dc + hi] = (s[:ts, :] - dpool).astype(BF16)

    return pl.pallas_call(
        _behind(after, body, 13), name="mixer_e_bwd_mix", grid=(t // ts,),
        in_specs=[_cur(ts, dc, 0), _next_halo(ts, CONV_HALO, dc, 0, t),
                  _cur(ts, dc, 1), _next_halo(ts, POOL_HALO, dc, 1, t),
                  _cur(ts, dc, 0), _cur(ts, dc, 1), _cur(ts, dc, 2),
                  _prev_halo(ts, CONV_HALO, dc, 0), _prev_halo(ts, CONV_HALO, dc, 1), _prev_halo(ts, POOL_HALO, dc, 2),
                  _full(conv_w.shape), _full(w_pool.shape), _full(scale.shape)] + [_ANY] * len(after),
        out_specs=[_cur(ts, 3 * dc, 0), _full(conv_w.shape), _full(w_pool.shape), _full(scale.shape)],
        out_shape=[jax.ShapeDtypeStruct((t, 3 * dc), BF16), jax.ShapeDtypeStruct(conv_w.shape, F32),
                   jax.ShapeDtypeStruct(w_pool.shape, F32), jax.ShapeDtypeStruct(scale.shape, F32)],
        scratch_shapes=[pltpu.VMEM((SUBLANES, ts + CONV_HALO, dc), F32)] * 2,
        compiler_params=_cparams("arbitrary"),
    )(da2, da2, dcat, dcat, u, u, u, u, u, u, conv_w, w_pool, scale, *after)


def _mixer_o_fwd(u, conv_w, seq, ts, after=()):
    t = u.shape[0]
    d = conv_w.shape[1]
    taps = conv_w.shape[0]
    ns = seq // ts

    def body(gb_ref, gc_ref, v_ref, pgc_ref, pv_ref, cw_ref, y_ref):
        keep_prev = jnp.where(pl.program_id(0) % ns == 0, 0.0, 1.0)
        f32 = lambda ref: ref[...].astype(F32)
        ext = jnp.concatenate([f32(pgc_ref) * f32(pv_ref) * keep_prev, f32(gc_ref) * f32(v_ref)], axis=0)
        cc = jnp.zeros_like(ext)
        for k in range(taps):
            cc = cc + cw_ref[k:k + 1, :] * _shift_down(ext, taps - 1 - k)
        y_ref[...] = (f32(gb_ref) * cc[SHORT_HALO:, :]).astype(BF16)

    return pl.pallas_call(
        _behind(after, body, 6), name="mixer_o_fwd", grid=(t // ts,),
        in_specs=[_cur(ts, d, 0), _cur(ts, d, 1), _cur(ts, d, 2),
                  _prev_halo(ts, SHORT_HALO, d, 1), _prev_halo(ts, SHORT_HALO, d, 2), _full(conv_w.shape)]
        + [_ANY] * len(after),
        out_specs=_cur(ts, d, 0),
        out_shape=jax.ShapeDtypeStruct((t, d), BF16),
        compiler_params=_cparams("parallel"),
    )(u, u, u, u, u, conv_w, *after)


def _mixer_o_bwd(dy, u, conv_w, seq, ts, after=()):
    t = u.shape[0]
    d = conv_w.shape[1]
    taps = conv_w.shape[0]
    ns = seq // ts

    def body(dy_ref, ndy_ref, gb_ref, gc_ref, v_ref, pgc_ref, pv_ref, ngb_ref, cw_ref, du_ref, dcw_ref):
        i = pl.program_id(0)
        keep_prev = jnp.where(i % ns == 0, 0.0, 1.0)
        keep_next = jnp.where(i % ns == ns - 1, 0.0, 1.0)

        @pl.when(i == 0)
        def _():
            dcw_ref[...] = jnp.zeros_like(dcw_ref)

        f32 = lambda ref: ref[...].astype(F32)
        gb, gc, v, dyv = f32(gb_ref), f32(gc_ref), f32(v_ref), f32(dy_ref)
        ext = jnp.concatenate([f32(pgc_ref) * f32(pv_ref) * keep_prev, gc * v], axis=0)
        dcc = dyv * gb
        ext_d = jnp.concatenate([dcc, f32(ndy_ref) * f32(ngb_ref) * keep_next], axis=0)
        cc = jnp.zeros_like(ext)
        dcv = jnp.zeros_like(ext_d)
        for k in range(taps):
            sh = taps - 1 - k
            shifted = _shift_down(ext, sh)
            cc = cc + cw_ref[k:k + 1, :] * shifted
            dcw_ref[k:k + 1, :] += jnp.sum(dcc * shifted[SHORT_HALO:, :], axis=0, keepdims=True)
            dcv = dcv + cw_ref[k:k + 1, :] * _shift_up(ext_d, sh)
        dcv = dcv[:ts, :]
        du_ref[:, 0:d] = (dyv * cc[SHORT_HALO:, :]).astype(BF16)
        du_ref[:, d:2 * d] = (dcv * v).astype(BF16)
        du_ref[:, 2 * d:3 * d] = (dcv * gc).astype(BF16)

    return pl.pallas_call(
        _behind(after, body, 9), name="mixer_o_bwd", grid=(t // ts,),
        in_specs=[_cur(ts, d, 0), _next_halo(ts, SHORT_HALO, d, 0, t),
                  _cur(ts, d, 0), _cur(ts, d, 1), _cur(ts, d, 2),
                  _prev_halo(ts, SHORT_HALO, d, 1), _prev_halo(ts, SHORT_HALO, d, 2),
                  _next_halo(ts, SHORT_HALO, d, 0, t), _full(conv_w.shape)] + [_ANY] * len(after),
        out_specs=[_cur(ts, 3 * d, 0), _full(conv_w.shape)],
        out_shape=[jax.ShapeDtypeStruct((t, 3 * d), BF16), jax.ShapeDtypeStruct(conv_w.shape, F32)],
        compiler_params=_cparams("arbitrary"),
    )(dy, dy, u, u, u, u, u, u, conv_w, *after)


def _cast_into_full(name, mat, w, layer, chip, after=()):
    tr = _pick(mat.sr, (512, 256, 128, 64, 32, 16))
    per = mat.sr // tr

    def body(chip_ref, w_ref, *rest):
        o_ref = rest[-1]
        o_ref[...] = w_ref[...].astype(BF16)

    if mat.kind == "col":
        o_spec = pl.BlockSpec((tr, mat.sc), lambda i, chip_ref: (i, chip_ref[0]))
    else:
        o_spec = pl.BlockSpec((tr, mat.sc), lambda i, chip_ref: (chip_ref[0] * per + i, 0))
    return pl.pallas_call(
        body, name=name,
        grid_spec=pltpu.PrefetchScalarGridSpec(
            num_scalar_prefetch=1, grid=(per,),
            in_specs=[pl.BlockSpec((None, tr, mat.sc), lambda i, chip_ref: (layer, i, 0))] + [_ANY] * len(after),
            out_specs=o_spec),
        out_shape=jax.ShapeDtypeStruct(mat.full_shape, BF16),
        compiler_params=_cparams("parallel"),
    )(chip, w, *after)


def _adamw(name, w, g, m, v, after=()):
    r, c = w.shape
    tr = _pick(r, (256, 128, 64, 32, 16, 8)) if c > 1024 else _pick(r, (512, 256, 128, 64, 32, 16, 8))
    bc1 = 1.0 - ADAM_B1 ** ADAM_STEP
    bc2 = 1.0 - ADAM_B2 ** ADAM_STEP

    def body(w_ref, g_ref, m_ref, v_ref, d_ref, mo_ref, vo_ref):
        gv = g_ref[...]
        mn = ADAM_B1 * m_ref[...] + (1.0 - ADAM_B1) * gv
        vn = ADAM_B2 * v_ref[...] + (1.0 - ADAM_B2) * (gv * gv)
        mo_ref[...] = mn
        vo_ref[...] = vn
        d_ref[...] = -ADAM_LR * ((mn / bc1) / (jnp.sqrt(vn / bc2) + ADAM_EPS) + ADAM_WD * w_ref[...])

    spec = pl.BlockSpec((tr, c), lambda i: (i, 0))
    return pl.pallas_call(_behind(after, body, 4), name=name, grid=(r // tr,),
                          in_specs=[spec] * 4 + [_ANY] * len(after), out_specs=[spec] * 3,
                          out_shape=[jax.ShapeDtypeStruct((r, c), F32)] * 3,
                          compiler_params=_cparams("parallel"))(w, g, m, v, *after)


def _aligned(offset, multiple):
    return offset if isinstance(offset, int) else pl.multiple_of(offset, multiple)


class _Mat:
    def __init__(self, kind, shard_shape):
        self.kind = kind
        self.sr, self.sc = shard_shape
        self.full_shape = (self.sr, self.sc * N_CHIPS) if kind == "col" else (self.sr * N_CHIPS, self.sc)
        self.pr, self.pc = self.sr // 2, self.sc

    def piece(self, ref, k, h):
        if self.kind == "col":
            return ref.at[pl.ds(_aligned(h * self.pr, 16), self.pr), pl.ds(_aligned(k * self.sc, LANES), self.sc)]
        return ref.at[pl.ds(_aligned(k * self.sr + h * self.pr, 16), self.pr), :]

    def shard(self, ref, k):
        if self.kind == "col":
            return ref.at[:, pl.ds(_aligned(k * self.sc, LANES), self.sc)]
        return ref.at[pl.ds(_aligned(k * self.sr, 16), self.sr), :]

    def half(self, ref, h):
        return ref.at[pl.ds(_aligned(h * self.pr, 16), self.pr), :]


def _place():
    x, y, c = lax.axis_index("x"), lax.axis_index("y"), lax.axis_index("c")
    others = [(1 - x, y), (x, 1 - y), (1 - x, 1 - y)]
    return x, y, c, others


_HBM = pl.BlockSpec(memory_space=pltpu.HBM)
_SEM = pl.BlockSpec(memory_space=pltpu.SEMAPHORE)
_TOKEN = jax.ShapeDtypeStruct((8, LANES), F32)
_TOKEN_SPEC = pl.BlockSpec(memory_space=pltpu.VMEM)


def _split_params():
    return pltpu.CompilerParams(has_side_effects=pltpu.SideEffectType.DATAFLOW_SIDE_EFFECTING)


def _in_hbm(a):
    return pltpu.with_memory_space_constraint(a, pltpu.HBM)


def _copy_to(src, dst, send_sem, recv_sem, to):
    return pltpu.make_async_remote_copy(src_ref=src, dst_ref=dst, send_sem=send_sem, recv_sem=recv_sem,
                                        device_id=to, device_id_type=MESH_ID)


def _place_small(packed, chip):
    rows, cols = packed.shape

    def body(chip_ref, p_ref, o_ref):
        o_ref[...] = p_ref[...]

    return pl.pallas_call(
        body, name="place_small",
        grid_spec=pltpu.PrefetchScalarGridSpec(
            num_scalar_prefetch=1, grid=(1,),
            in_specs=[pl.BlockSpec((rows, cols), lambda i, chip_ref: (0, 0))],
            out_specs=pl.BlockSpec((None, rows, cols), lambda i, chip_ref: (chip_ref[0], 0, 0))),
        out_shape=jax.ShapeDtypeStruct((N_CHIPS, rows, cols), F32),
        compiler_params=_cparams("arbitrary"),
    )(chip, packed)


def _gather_start(name, gmats, gfulls, small_all=None):
    n = len(gmats)
    n_in = n + (1 if small_all is not None else 0)

    def body(*refs):
        full_refs = refs[:n]
        outs = refs[n_in:]
        send_sem, recv_sem, token = outs[n_in], outs[n_in + 1], outs[n_in + 2]
        x, y, c, others = _place()
        me_k = 2 * x + y
        if small_all is not None:
            mine = refs[n].at[me_k]
            for ox, oy in others:
                _copy_to(mine, mine, send_sem, recv_sem, (ox, oy, c)).start()
        for m in range(n):
            mine = gmats[m].piece(full_refs[m], me_k, c)
            for ox, oy in others:
                _copy_to(mine, mine, send_sem, recv_sem, (ox, oy, c)).start()
        token[...] = jnp.zeros_like(token)

    operands = [_in_hbm(f) for f in gfulls] + ([_in_hbm(small_all)] if small_all is not None else [])
    outs = pl.pallas_call(
        body, name=name,
        in_specs=[_HBM] * n_in,
        out_specs=[_HBM] * n_in + [_SEM, _SEM, _TOKEN_SPEC],
        out_shape=[pltpu.HBM(a.shape, a.dtype) for a in operands] + [pltpu.SemaphoreType.DMA(())] * 2 + [_TOKEN],
        input_output_aliases={m: m for m in range(n_in)},
        compiler_params=_split_params(),
    )(*operands)
    return list(outs[:n]), (outs[n] if small_all is not None else None), (outs[n_in], outs[n_in + 1]), outs[n_in + 2]


def _gather_pass(name, gmats, gfulls, small_all, sems, after):
    k = len(gmats)
    n_buf = k + (1 if small_all is not None else 0)

    def body(*refs):
        bufs = refs[:n_buf]
        send_sem, recv_sem = refs[n_buf], refs[n_buf + 1]
        outs = refs[n_buf + 3:]
        fsend, frecv, token = outs[n_buf], outs[n_buf + 1], outs[n_buf + 2]
        x, y, c, others = _place()
        me_k = 2 * x + y
        sibling = (x, y, 1 - c)
        for m in range(k):
            for ox, oy in others:
                got = gmats[m].piece(bufs[m], 2 * ox + oy, c)
                _copy_to(got, got, send_sem, recv_sem, sibling).wait_recv()
        if small_all is not None:
            for ox, oy in others:
                got = bufs[k].at[2 * ox + oy]
                _copy_to(got, got, send_sem, recv_sem, sibling).wait_recv()
        for m in range(k):
            mine = gmats[m].piece(bufs[m], me_k, c)
            for _ in others:
                _copy_to(mine, mine, send_sem, recv_sem, sibling).wait_send()
        if small_all is not None:
            for _ in others:
                _copy_to(bufs[k].at[me_k], bufs[k].at[me_k], send_sem, recv_sem, sibling).wait_send()
        for m in range(k):
            for ox, oy in others:
                got = gmats[m].piece(bufs[m], 2 * ox + oy, c)
                _copy_to(got, got, fsend, frecv, sibling).start()
        token[...] = jnp.zeros_like(token)

    operands = [_in_hbm(f) for f in gfulls] + ([_in_hbm(small_all)] if small_all is not None else [])
    outs = pl.pallas_call(
        body, name=name,
        in_specs=[_HBM] * n_buf + [_SEM, _SEM, _ANY],
        out_specs=[_HBM] * n_buf + [_SEM, _SEM, _TOKEN_SPEC],
        out_shape=[pltpu.HBM(a.shape, a.dtype) for a in operands] + [pltpu.SemaphoreType.DMA(())] * 2 + [_TOKEN],
        input_output_aliases={i: i for i in range(n_buf)},
        compiler_params=_split_params(),
    )(*operands, sems[0], sems[1], after)
    return list(outs[:n_buf]), (outs[n_buf], outs[n_buf + 1]), outs[n_buf + 2]


def _gather_done(name, gmats, gfulls, sems, after):
    k = len(gmats)

    def body(*refs):
        bufs = refs[:k]
        send_sem, recv_sem = refs[k], refs[k + 1]
        x, y, c, others = _place()
        sibling = (x, y, 1 - c)
        for m in range(k):
            for ox, oy in others:
                got = gmats[m].piece(bufs[m], 2 * ox + oy, 1 - c)
                _copy_to(got, got, send_sem, recv_sem, sibling).wait_recv()
        for m in range(k):
            for ox, oy in others:
                sent = gmats[m].piece(bufs[m], 2 * ox + oy, c)
                _copy_to(sent, sent, send_sem, recv_sem, sibling).wait_send()

    outs = pl.pallas_call(
        body, name=name,
        in_specs=[_HBM] * k + [_SEM, _SEM, _ANY], out_specs=[_HBM] * k,
        out_shape=[pltpu.HBM(a.shape, a.dtype) for a in gfulls],
        input_output_aliases={i: i for i in range(k)},
        compiler_params=_split_params(),
    )(*[_in_hbm(f) for f in gfulls], sems[0], sems[1], after)
    return list(outs)


_FLIPS = [(fx, fy, fc) for fx in (0, 1) for fy in (0, 1) for fc in (0, 1) if (fx, fy, fc) != (0, 0, 0)]


def _small_start(packed, after):
    def body(small_ref, after_ref, small_thru, land_ref, send_sem, recv_sem, token):
        x, y, c, _ = _place()
        me = 4 * x + 2 * y + c
        for fx, fy, fc in _FLIPS:
            _copy_to(small_ref, land_ref.at[me], send_sem, recv_sem, (x ^ fx, y ^ fy, c ^ fc)).start()
        token[...] = jnp.zeros_like(token)

    outs = pl.pallas_call(
        body, name="small_grads_start",
        in_specs=[_HBM, _ANY], out_specs=[_HBM, _HBM, _SEM, _SEM, _TOKEN_SPEC],
        out_shape=[pltpu.HBM(packed.shape, F32), pltpu.HBM((N_DEV,) + packed.shape, F32)]
        + [pltpu.SemaphoreType.DMA(())] * 2 + [_TOKEN],
        input_output_aliases={0: 0},
        compiler_params=_split_params(),
    )(_in_hbm(packed), after)
    return outs[0], outs[1], (outs[2], outs[3]), outs[4]


def _small_wait(packed, landed, sems, after):
    def body(small_ref, land_ref, send_sem, recv_sem, after_ref, small_thru, land_thru):
        x, y, c, _ = _place()
        for fx, fy, fc in _FLIPS:
            got = land_ref.at[4 * (x ^ fx) + 2 * (y ^ fy) + (c ^ fc)]
            _copy_to(got, got, send_sem, recv_sem, (x, y, 1 - c)).wait_recv()
        for _ in _FLIPS:
            _copy_to(small_ref, small_ref, send_sem, recv_sem, (x, y, 1 - c)).wait_send()

    outs = pl.pallas_call(
        body, name="small_grads_wait",
        in_specs=[_HBM, _HBM, _SEM, _SEM, _ANY], out_specs=[_HBM, _HBM],
        out_shape=[pltpu.HBM(packed.shape, F32), pltpu.HBM(landed.shape, F32)],
        input_output_aliases={0: 0, 1: 1},
        compiler_params=_split_params(),
    )(_in_hbm(packed), _in_hbm(landed), sems[0], sems[1], after)
    return outs[0], outs[1]


def _exchange_start(name, mats, grads):
    n = len(mats)

    def body(*refs):
        g_refs = refs[:n]
        outs = refs[n:]
        land_refs = outs[n:2 * n]
        send_sem, recv_sem, token = outs[2 * n], outs[2 * n + 1], outs[2 * n + 2]
        x, y, c, _ = _place()
        for m in range(n):
            for k in range(N_CHIPS):
                _copy_to(mats[m].piece(g_refs[m], k, 1 - c), land_refs[m].at[k], send_sem, recv_sem, (x, y, 1 - c)).start()
        token[...] = jnp.zeros_like(token)

    outs = pl.pallas_call(
        body, name=name,
        in_specs=[_HBM] * n,
        out_specs=[_HBM] * (2 * n) + [_SEM, _SEM, _TOKEN_SPEC],
        out_shape=[pltpu.HBM(mt.full_shape, BF16) for mt in mats]
        + [pltpu.HBM((N_CHIPS, mt.pr, mt.pc), BF16) for mt in mats] + [pltpu.SemaphoreType.DMA(())] * 2 + [_TOKEN],
        input_output_aliases={m: m for m in range(n)},
        compiler_params=_split_params(),
    )(*[_in_hbm(g) for g in grads])
    return list(outs[:n]), list(outs[n:2 * n]), (outs[2 * n], outs[2 * n + 1]), outs[2 * n + 2]


def _exchange_wait(name, mats, grads, landed, sems, after):
    n = len(mats)

    def body(*refs):
        g_refs, land_refs = refs[:n], refs[n:2 * n]
        send_sem, recv_sem = refs[2 * n], refs[2 * n + 1]
        x, y, c, _ = _place()
        for m in range(n):
            for k in range(N_CHIPS):
                got = land_refs[m].at[k]
                _copy_to(got, got, send_sem, recv_sem, (x, y, 1 - c)).wait_recv()
        for m in range(n):
            for k in range(N_CHIPS):
                sent = mats[m].piece(g_refs[m], k, 1 - c)
                _copy_to(sent, sent, send_sem, recv_sem, (x, y, 1 - c)).wait_send()

    outs = pl.pallas_call(
        body, name=name,
        in_specs=[_HBM] * (2 * n) + [_SEM, _SEM, _ANY], out_specs=[_HBM] * (2 * n),
        out_shape=[pltpu.HBM(a.shape, a.dtype) for a in list(grads) + list(landed)],
        input_output_aliases={i: i for i in range(2 * n)},
        compiler_params=_split_params(),
    )(*[_in_hbm(a) for a in list(grads) + list(landed)], sems[0], sems[1], after)
    return list(outs[:n]), list(outs[n:])


def _add_halves(name, mat, grad, landed, core):
    tr = _pick(mat.pr, (1024, 704, 512, 352, 256, 128, 64, 32, 16))
    per = mat.pr // tr

    def body(core_ref, g_ref, l_ref, o_ref):
        o_ref[...] = (g_ref[...].astype(F32) + l_ref[...].astype(F32)).astype(BF16)

    if mat.kind == "col":
        g_spec = pl.BlockSpec((tr, mat.pc), lambda k, r, core_ref: (core_ref[0] * per + r, k))
    else:
        g_spec = pl.BlockSpec((tr, mat.pc), lambda k, r, core_ref: ((2 * k + core_ref[0]) * per + r, 0))
    p_spec = pl.BlockSpec((None, tr, mat.pc), lambda k, r, core_ref: (k, r, 0))
    return pl.pallas_call(
        body, name=name,
        grid_spec=pltpu.PrefetchScalarGridSpec(num_scalar_prefetch=1, grid=(N_CHIPS, per),
                                               in_specs=[g_spec, p_spec], out_specs=p_spec),
        out_shape=jax.ShapeDtypeStruct((N_CHIPS, mat.pr, mat.pc), BF16),
        compiler_params=_cparams("parallel", "parallel"),
    )(core, grad, landed)


def _scatter_start(name, mats, partials):
    n = len(mats)

    def body(*refs):
        p_refs = refs[:n]
        outs = refs[n:]
        land_refs = outs[n:2 * n]
        send_sem, recv_sem, token = outs[2 * n], outs[2 * n + 1], outs[2 * n + 2]
        x, y, c, others = _place()
        me_k = 2 * x + y
        for m in range(n):
            for ox, oy in others:
                _copy_to(p_refs[m].at[2 * ox + oy], land_refs[m].at[me_k], send_sem, recv_sem, (ox, oy, c)).start()
        token[...] = jnp.zeros_like(token)

    piece_shapes = [pltpu.HBM((N_CHIPS, mt.pr, mt.pc), BF16) for mt in mats]
    outs = pl.pallas_call(
        body, name=name,
        in_specs=[_HBM] * n,
        out_specs=[_HBM] * (2 * n) + [_SEM, _SEM, _TOKEN_SPEC],
        out_shape=piece_shapes + piece_shapes + [pltpu.SemaphoreType.DMA(())] * 2 + [_TOKEN],
        input_output_aliases={m: m for m in range(n)},
        compiler_params=_split_params(),
    )(*[_in_hbm(p) for p in partials])
    return list(outs[:n]), list(outs[n:2 * n]), (outs[2 * n], outs[2 * n + 1]), outs[2 * n + 2]


def _scatter_wait(name, mats, partials, landed, sems, after):
    n = len(mats)

    def body(*refs):
        p_refs, land_refs = refs[:n], refs[n:2 * n]
        send_sem, recv_sem = refs[2 * n], refs[2 * n + 1]
        x, y, c, others = _place()
        for m in range(n):
            for ox, oy in others:
                got = land_refs[m].at[2 * ox + oy]
                _copy_to(got, got, send_sem, recv_sem, (ox, oy, c)).wait_recv()
        for m in range(n):
            for ox, oy in others:
                sent = p_refs[m].at[2 * ox + oy]
                _copy_to(sent, sent, send_sem, recv_sem, (ox, oy, c)).wait_send()

    outs = pl.pallas_call(
        body, name=name,
        in_specs=[_HBM] * (2 * n) + [_SEM, _SEM, _ANY], out_specs=[_HBM] * (2 * n),
        out_shape=[pltpu.HBM(a.shape, a.dtype) for a in list(partials) + list(landed)],
        input_output_aliases={i: i for i in range(2 * n)},
        compiler_params=_split_params(),
    )(*[_in_hbm(a) for a in list(partials) + list(landed)], sems[0], sems[1], after)
    return list(outs[:n]), list(outs[n:])


def _sum_chips(name, mat, partial, landed, slots, layer=None, stack=None, n_layers=1):
    tr = _pick(mat.pr, (1024, 704, 512, 352, 256, 128, 64, 32, 16))
    per = mat.pr // tr

    def body(slots_ref, own_ref, a_ref, b_ref, c_ref, *rest):
        o_ref = rest[-1]
        o_ref[...] = ((own_ref[...].astype(F32) + a_ref[...].astype(F32)) + b_ref[...].astype(F32)) + c_ref[...].astype(F32)

    def slot_spec(which):
        return pl.BlockSpec((None, tr, mat.pc), lambda r, slots_ref: (slots_ref[which], r, 0))

    in_specs = [slot_spec(0), slot_spec(1), slot_spec(2), slot_spec(3)]
    operands = [slots, partial, landed, landed, landed]
    aliases = {}
    if layer is None:
        o_spec = pl.BlockSpec((tr, mat.pc), lambda r, slots_ref: (slots_ref[4] * per + r, 0))
        out_shape = jax.ShapeDtypeStruct((mat.sr, mat.sc), F32)
    else:
        o_spec = pl.BlockSpec((None, tr, mat.pc), lambda r, slots_ref: (layer, slots_ref[4] * per + r, 0))
        out_shape = jax.ShapeDtypeStruct((n_layers, mat.sr, mat.sc), F32)
        if stack is not None:
            in_specs.append(_ANY)
            operands.append(stack)
            aliases = {len(operands) - 1: 0}
    return pl.pallas_call(
        body, name=name,
        grid_spec=pltpu.PrefetchScalarGridSpec(num_scalar_prefetch=1, grid=(per,), in_specs=in_specs, out_specs=o_spec),
        out_shape=out_shape, input_output_aliases=aliases,
        compiler_params=_cparams("parallel"),
    )(*operands)


def _share_pieces(name, mats, shards, groups):
    n = len(mats)
    n_out = len(groups)

    def body(*refs):
        out_refs = refs[n_out:2 * n_out]
        send_sems, recv_sems = refs[2 * n_out:]
        x, y, c, _ = _place()
        sibling = (x, y, 1 - c)
        sent, waits = [], []
        for o, members in enumerate(groups):
            for l, m in enumerate(members):
                dst = out_refs[o].at[l] if len(members) > 1 else out_refs[o]
                mine = mats[m].half(dst, c)
                sent.append(pltpu.make_async_remote_copy(src_ref=mine, dst_ref=mine, send_sem=send_sems.at[m],
                                                         recv_sem=recv_sems.at[m], device_id=sibling, device_id_type=MESH_ID))
                theirs = mats[m].half(dst, 1 - c)
                waits.append(pltpu.make_async_remote_copy(src_ref=theirs, dst_ref=theirs, send_sem=send_sems.at[m],
                                                          recv_sem=recv_sems.at[m], device_id=sibling, device_id_type=MESH_ID))
        for cp in sent:
            cp.start()
        for cp in waits:
            cp.wait_recv()
        for cp in sent:
            cp.wait_send()

    return pl.pallas_call(
        body, name=name,
        in_specs=[_ANY] * n_out, out_specs=[_ANY] * n_out,
        out_shape=[jax.ShapeDtypeStruct(s.shape, F32) for s in shards],
        input_output_aliases={o: o for o in range(n_out)},
        scratch_shapes=[pltpu.SemaphoreType.DMA((n,)), pltpu.SemaphoreType.DMA((n,))],
    )(*shards)


def _share_start(name, mats, shards, items):
    n_out = len(shards)

    def body(*refs):
        out_refs = refs[n_out:2 * n_out]
        send_sem, recv_sem, token = refs[2 * n_out], refs[2 * n_out + 1], refs[2 * n_out + 2]
        x, y, c, _ = _place()
        for o, members in enumerate(items):
            for layer, m in members:
                dst = out_refs[o] if layer is None else out_refs[o].at[layer]
                mine = mats[m].half(dst, c)
                _copy_to(mine, mine, send_sem, recv_sem, (x, y, 1 - c)).start()
        token[...] = jnp.zeros_like(token)

    outs = pl.pallas_call(
        body, name=name,
        in_specs=[_HBM] * n_out, out_specs=[_HBM] * n_out + [_SEM, _SEM, _TOKEN_SPEC],
        out_shape=[pltpu.HBM(s.shape, F32) for s in shards] + [pltpu.SemaphoreType.DMA(())] * 2 + [_TOKEN],
        input_output_aliases={o: o for o in range(n_out)},
        compiler_params=_split_params(),
    )(*[_in_hbm(s) for s in shards])
    return list(outs[:n_out]), (outs[n_out], outs[n_out + 1]), outs[n_out + 2]


def _share_wait(name, mats, shards, items, sems, after):
    n_out = len(shards)

    def body(*refs):
        bufs = refs[:n_out]
        send_sem, recv_sem = refs[n_out], refs[n_out + 1]
        x, y, c, _ = _place()
        for o, members in enumerate(items):
            for layer, m in members:
                dst = bufs[o] if layer is None else bufs[o].at[layer]
                theirs = mats[m].half(dst, 1 - c)
                _copy_to(theirs, theirs, send_sem, recv_sem, (x, y, 1 - c)).wait_recv()
        for o, members in enumerate(items):
            for layer, m in members:
                dst = bufs[o] if layer is None else bufs[o].at[layer]
                mine = mats[m].half(dst, c)
                _copy_to(mine, mine, send_sem, recv_sem, (x, y, 1 - c)).wait_send()

    outs = pl.pallas_call(
        body, name=name,
        in_specs=[_HBM] * n_out + [_SEM, _SEM, _ANY], out_specs=[_HBM] * n_out,
        out_shape=[pltpu.HBM(s.shape, F32) for s in shards],
        input_output_aliases={o: o for o in range(n_out)},
        compiler_params=_split_params(),
    )(*[_in_hbm(s) for s in shards], sems[0], sems[1], after)
    return list(outs)


def _sum_devices(stacked):
    nd, r, c = stacked.shape

    def body(s_ref, o_ref):
        s = s_ref[0]
        for k in range(1, nd):
            s = s + s_ref[k]
        o_ref[...] = s

    return pl.pallas_call(
        body, name="sum_small_grads", grid=(1,),
        in_specs=[pl.BlockSpec((nd, r, c), lambda i: (0, 0, 0))],
        out_specs=pl.BlockSpec((r, c), lambda i: (0, 0)),
        out_shape=jax.ShapeDtypeStruct((r, c), F32),
        compiler_params=_cparams("arbitrary"),
    )(stacked)


def _pack(arrs):
    flat = jnp.concatenate([a.reshape(-1) for a in arrs])
    rows = -(-flat.shape[0] // (8 * LANES)) * 8
    return jnp.pad(flat, (0, rows * LANES - flat.shape[0])).reshape(rows, LANES)


def _unpack(packed, shapes):
    flat = packed.reshape(-1)
    out, at = [], 0
    for s in shapes:
        size = 1
        for dim in s:
            size *= dim
        out.append(flat[at:at + size].reshape(s))
        at += size
    return out


def kernel(x, mix_norm_e, w_in_e, conv_w_e, conv_b_e, ln_g_e, ln_b_e, w_pool_e, pool_scale_e, w_out_e, mix_norm_o, w_in_o, conv_w_o, w_out_o, ffn_norm, w_gate, w_up, w_down, final_norm, loss_target, m_mix_norm_e, m_w_in_e, m_conv_w_e, m_conv_b_e, m_ln_g_e, m_ln_b_e, m_w_pool_e, m_pool_scale_e, m_w_out_e, m_mix_norm_o, m_w_in_o, m_conv_w_o, m_w_out_o, m_ffn_norm, m_w_gate, m_w_up, m_w_down, m_final_norm, v_mix_norm_e, v_w_in_e, v_conv_w_e, v_conv_b_e, v_ln_g_e, v_ln_b_e, v_w_pool_e, v_pool_scale_e, v_w_out_e, v_mix_norm_o, v_w_in_o, v_conv_w_o, v_w_out_o, v_ffn_norm, v_w_gate, v_w_up, v_w_down, v_final_norm):
    bsz, seq_len, d = x.shape
    t = bsz * seq_len
    depth = ffn_norm.shape[0]
    assert depth == 2 and conv_b_e.shape[1] == pool_scale_e.shape[1]
    ts = _pick(seq_len, (256, 128, 64, 32))
    me_k = 2 * lax.axis_index("x") + lax.axis_index("y")
    core = lax.axis_index("c").astype(jnp.int32).reshape(1)

    mat_src = [("col", w_in_e, 0), ("row", w_out_e, 0), ("col", w_gate, 0), ("col", w_up, 0), ("row", w_down, 0),
               ("col", w_in_o, 0), ("row", w_out_o, 0), ("col", w_gate, 1), ("col", w_up, 1), ("row", w_down, 1)]
    mats = [_Mat(kind, w.shape[1:]) for kind, w, _ in mat_src]
    n_pool = w_pool_e.shape[1]
    pool_mats = tuple(range(len(mats), len(mats) + n_pool))
    mats = mats + [_Mat("row", w_pool_e.shape[2:])] * n_pool
    chip = me_k.astype(jnp.int32).reshape(1)
    small_shards = [conv_w_e[0], w_pool_e[0], mix_norm_o, conv_w_o[0]]
    packed_small = _pack(small_shards)

    chain = [()]

    def seq(fn, *args, **kw):
        out = fn(*args, after=chain[0], **kw)
        chain[0] = (out[0] if isinstance(out, (list, tuple)) else out,)
        return out

    def mm(*args, **kw):
        return seq(_mm, *args, **kw)

    gather_groups = [(0,), (1,), (2, 3), (4,), (5, 6), (7, 8), (9,)]
    fulls, gather_sems, small_all = [None] * len(mats), [], None
    for g, ms in enumerate(gather_groups):
        own16 = [seq(_cast_into_full, "cast_w%d" % m, mats[m], mat_src[m][1], mat_src[m][2], chip) for m in ms]
        sent, landing, sems, token = _gather_start("gather_start%d" % g, [mats[m] for m in ms], own16,
                                                   _place_small(packed_small, chip) if g == 0 else None)
        chain[0] = (token,)
        for m, f in zip(ms, sent):
            fulls[m] = f
        gather_sems.append(sems)
        if g == 0:
            small_all = landing

    passed = {}

    def gather_pass(g):
        ms = gather_groups[g]
        bufs, pass_sems, token = _gather_pass("gather_pass%d" % g, [mats[m] for m in ms], [fulls[m] for m in ms],
                                              small_all if g == 0 else None, gather_sems[g], chain[0][0])
        chain[0] = (token,)
        passed[g] = (bufs, pass_sems)

    def gather_done(g):
        ms = gather_groups[g]
        bufs, pass_sems = passed[g]
        done = _gather_done("gather_done%d" % g, [mats[m] for m in ms], bufs[:len(ms)], pass_sems, chain[0][0])
        chain[0] = (done[0],)
        return done + bufs[len(ms):]

    h0 = x.reshape(t, d)
    target = loss_target.reshape(t, d)
    gather_pass(0)
    n1 = seq(_rms_fwd, "mix0_norm", h0, mix_norm_e)
    W_in_e, small_all = gather_done(0)
    per_chip = [_unpack(small_all[k], [s.shape for s in small_shards]) for k in range(N_CHIPS)]
    conv_w_e_f = jnp.concatenate([p[0] for p in per_chip], axis=1)
    w_pool_f = jnp.concatenate([p[1] for p in per_chip], axis=1)
    mix_norm_o_f = jnp.concatenate([p[2] for p in per_chip], axis=1)
    conv_w_o_f = jnp.concatenate([p[3] for p in per_chip], axis=1)
    W_gate, W_up, W_down = [None, None], [None, None], [None, None]

    (u_e,) = mm("mix0_in", [(n1, W_in_e)], "nn", [F32], _ep_store, tm=1024, tn=1024, tk=2048)
    gather_pass(1)
    a2, cat = seq(_mixer_e_fwd, u_e, conv_w_e_f, conv_b_e, ln_g_e, ln_b_e, w_pool_f, pool_scale_e, seq_len, ts)
    (W_out_e,) = gather_done(1)
    (h1,) = mm("mix0_out", [(cat, W_out_e)], "nn", [F32], _ep_residual, extras=(h0,), tm=1024, tn=1024, tk=2048)
    gather_pass(2)
    n2 = seq(_rms_fwd, "ffn0_norm", h1, ffn_norm[0:1])
    W_gate[0], W_up[0] = gather_done(2)
    gt0, up0, act0 = mm("ffn0_gate_up", [(n2, W_gate[0]), (n2, W_up[0])], "nn", [BF16] * 3, _ep_swiglu,
                        acc_of=(0, 1), tm=1024, tn=512, tk=2048)
    gather_pass(3)
    (W_down[0],) = gather_done(3)
    gather_pass(4)
    (h2,) = mm("ffn0_down", [(act0, W_down[0])], "nn", [F32], _ep_residual, extras=(h1,), tm=512, tn=1024, tk=5632)
    n3 = seq(_rms_fwd, "mix1_norm", h2, mix_norm_o_f)
    W_in_o, W_out_o = gather_done(4)
    (u_o,) = mm("mix1_in", [(n3, W_in_o)], "nn", [BF16], _ep_store, tm=1024, tn=1024, tk=2048)
    gather_pass(5)
    y_o = seq(_mixer_o_fwd, u_o, conv_w_o_f, seq_len, ts)
    h3, n4 = seq(_mm_rows, "mix1_out", [(y_o, W_out_o)], "nn", [F32, BF16], _ep_rows_residual_norm, extras=(h2,),
                 vecs=(ffn_norm[1:2],), tm=512, tk=2048)
    gather_pass(6)
    W_gate[1], W_up[1] = gather_done(5)
    gt1, up1, act1 = mm("ffn1_gate_up", [(n4, W_gate[1]), (n4, W_up[1])], "nn", [BF16] * 3, _ep_swiglu,
                        acc_of=(0, 1), tm=1024, tn=512, tk=2048)
    (W_down[1],) = gather_done(6)
    (h4,) = mm("ffn1_down", [(act1, W_down[1])], "nn", [F32], _ep_residual, extras=(h3,), tm=512, tn=1024, tk=5632)
    loss_part, dh4, dh4b, d_final_norm = seq(_loss_head, h4, final_norm.reshape(1, d), target)
    loss = lax.psum(loss_part[0, 0], AXES)

    in_flight = {}
    partials, scattered = [None] * len(mats), [None] * len(mats)

    def reduce_begin(tag, ms, grads):
        gm = [mats[m] for m in ms]
        grads, landed, sems, token = _exchange_start("exchange_start_" + tag, gm, grads)
        chain[0] = (token,)
        in_flight[tag] = (ms, gm, grads, landed, sems)

    def reduce_advance(tag):
        ms, gm, grads, landed, sems = in_flight[tag]
        grads, landed = _exchange_wait("exchange_wait_" + tag, gm, grads, landed, sems, chain[0][0])
        parts = [_add_halves("add_halves%d" % m, mats[m], g, l, core) for m, g, l in zip(ms, grads, landed)]
        parts, lands, sems, token = _scatter_start("scatter_start_" + tag, gm, parts)
        chain[0] = (token,)
        in_flight[tag] = (ms, gm, parts, lands, sems)

    def reduce_finish(tag):
        ms, gm, parts, lands, sems = in_flight[tag]
        parts, lands = _scatter_wait("scatter_wait_" + tag, gm, parts, lands, sems, chain[0][0])
        chain[0] = (lands[0],)
        for m, p, l in zip(ms, parts, lands):
            partials[m], scattered[m] = p, l

    xi, yi, ci = lax.axis_index("x"), lax.axis_index("y"), lax.axis_index("c")
    slots = jnp.stack([me_k, 2 * (1 - xi) + yi, 2 * xi + (1 - yi), 2 * (1 - xi) + (1 - yi), ci]).astype(jnp.int32)
    shards = {}

    def sum_into(name, m, layer=None, n_layers=1):
        if layer is None:
            shards[name] = _sum_chips("sum_chips%d" % m, mats[m], partials[m], scattered[m], slots)
        else:
            shards[name] = _sum_chips("sum_chips%d" % m, mats[m], partials[m], scattered[m], slots, layer=layer,
                                      stack=shards.get(name), n_layers=n_layers)

    sharing = {}

    def share_begin(tag, names, items):
        arrays, sems, token = _share_start("share_start_" + tag, mats, [shards[nm] for nm in names], items)
        chain[0] = (token,)
        sharing[tag] = (names, items, arrays, sems)

    def share_end(tag):
        names, items, arrays, sems = sharing[tag]
        arrays = _share_wait("share_wait_" + tag, mats, arrays, items, sems, chain[0][0])
        chain[0] = (arrays[0],)
        for nm, a in zip(names, arrays):
            shards[nm] = a

    def ffn_bwd(l, dhb, n, gt, up, act, mid=None):
        dgt, dup = mm("ffn%d_dact" % l, [(dhb, W_down[l])], "nt", [BF16, BF16], _ep_swiglu_bwd, extras=(gt, up),
                      tm=1024, tn=512, tk=2048, row_chunk=256)
        (dW_down,) = mm("ffn%d_dw_down" % l, [(act, dhb)], "tn", [BF16], _ep_store, tm=512, tn=1024, tk=4096)
        if mid is not None:
            mid()
        (dn,) = mm("ffn%d_dn" % l, [(dgt, W_gate[l]), (dup, W_up[l])], "nt", [BF16], _ep_store,
                   tm=512, tn=2048, tk=1408)
        (dW_gate,) = mm("ffn%d_dw_gate" % l, [(n, dgt)], "tn", [BF16], _ep_store, tm=1024, tn=512, tk=4096)
        (dW_up,) = mm("ffn%d_dw_up" % l, [(n, dup)], "tn", [BF16], _ep_store, tm=1024, tn=512, tk=4096)
        return dn, dW_gate, dW_up, dW_down

    dn4, dW_gate1, dW_up1, dW_down1 = ffn_bwd(1, dh4b, n4, gt1, up1, act1)
    reduce_begin("ffn1", (7, 8, 9), [dW_gate1, dW_up1, dW_down1])
    dh3, dh3b, d_ffn_norm1 = seq(_rms_bwd, "ffn1_norm_bwd", dn4, h3, ffn_norm[1:2], dh4)
    (dy_o,) = mm("mix1_dy", [(dh3b, W_out_o)], "nt", [BF16], _ep_store, tm=1024, tn=1024, tk=2048)
    reduce_advance("ffn1")
    (dW_out_o,) = mm("mix1_dw_out", [(y_o, dh3b)], "tn", [BF16], _ep_store, tm=1024, tn=1024, tk=4096)
    du_o, d_conv_w_o = seq(_mixer_o_bwd, dy_o, u_o, conv_w_o_f, seq_len, ts)
    (dW_in_o,) = mm("mix1_dw_in", [(n3, du_o)], "tn", [BF16], _ep_store, tm=1024, tn=1024, tk=4096)
    reduce_begin("mix1", (5, 6), [dW_in_o, dW_out_o])
    dh2, dh2b, d_mix_norm_o = seq(_mm_rows, "mix1_dn", [(du_o, W_in_o)], "nt", [F32, BF16], _ep_rows_norm_bwd,
                                  extras=(h2, dh3), vecs=(mix_norm_o_f,), n_sums=1, tm=256, tk=6144, chunk_dots=False)
    reduce_advance("mix1")

    def finish_layer1():
        reduce_finish("ffn1")
        reduce_finish("mix1")
        sum_into("w_in_o", 5)
        sum_into("w_out_o", 6)
        for nm, m in (("w_gate", 7), ("w_up", 8), ("w_down", 9)):
            sum_into(nm, m, layer=1, n_layers=2)
        share_begin("layer1", ["w_in_o", "w_out_o", "w_gate", "w_up", "w_down"],
                    [[(None, 5)], [(None, 6)], [(1, 7)], [(1, 8)], [(1, 9)]])

    dn2, dW_gate0, dW_up0, dW_down0 = ffn_bwd(0, dh2b, n2, gt0, up0, act0, mid=finish_layer1)
    reduce_begin("ffn0", (2, 3, 4), [dW_gate0, dW_up0, dW_down0])
    dh1, dh1b, d_ffn_norm0 = seq(_rms_bwd, "ffn0_norm_bwd", dn2, h1, ffn_norm[0:1], dh2)
    (dcat,) = mm("mix0_dcat", [(dh1b, W_out_e)], "nt", [F32], _ep_store, tm=1024, tn=1024, tk=2048)
    reduce_advance("ffn0")
    (dW_out_e,) = mm("mix0_dw_out", [(cat, dh1b)], "tn", [BF16], _ep_store, tm=1024, tn=1024, tk=4096)
    da2, d_ln_g, d_ln_b, d_conv_b = seq(_mixer_e_bwd_norm, dcat, a2, ln_g_e, ln_b_e, ts)
    du_e, d_conv_w_e, d_w_pool, d_pool_scale = seq(_mixer_e_bwd_mix, da2, dcat, u_e, conv_w_e_f, w_pool_f, pool_scale_e,
                                                   seq_len, ts)
    (dW_in_e,) = mm("mix0_dw_in", [(n1, du_e)], "tn", [BF16], _ep_store, tm=1024, tn=1024, tk=4096)
    reduce_begin("mix0", (0, 1) + pool_mats, [dW_in_e, dW_out_e] + [d_w_pool[g].astype(BF16) for g in range(n_pool)])
    dx, d_mix_norm_e = seq(_mm_rows, "mix0_dn", [(du_e, W_in_e)], "nt", [F32], _ep_rows_norm_bwd,
                           extras=(h0, dh1), vecs=(mix_norm_e,), n_sums=1, tm=256, tk=3072, chunk_dots=False)
    reduce_advance("mix0")

    d_ffn_norm = jnp.concatenate([d_ffn_norm0, d_ffn_norm1], axis=0)
    small_partials = [d_mix_norm_e, d_conv_w_e, d_conv_b, d_ln_g, d_ln_b, d_pool_scale, d_mix_norm_o, d_conv_w_o,
                      d_ffn_norm, d_final_norm]
    packed_grads, small_stack, small_sems, token = _small_start(_pack(small_partials), chain[0][0])
    chain[0] = (token,)
    share_end("layer1")
    reduce_finish("ffn0")
    for nm, m in (("w_gate", 2), ("w_up", 3), ("w_down", 4)):
        sum_into(nm, m, layer=0, n_layers=2)
    share_begin("layer0", ["w_gate", "w_up", "w_down"], [[(0, 2)], [(0, 3)], [(0, 4)]])
    grad = {"w_in_o": shards["w_in_o"][None], "w_out_o": shards["w_out_o"][None]}
    weights = dict(mix_norm_e=mix_norm_e, w_in_e=w_in_e, conv_w_e=conv_w_e, conv_b_e=conv_b_e, ln_g_e=ln_g_e, ln_b_e=ln_b_e,
                   w_pool_e=w_pool_e, pool_scale_e=pool_scale_e, w_out_e=w_out_e, mix_norm_o=mix_norm_o, w_in_o=w_in_o,
                   conv_w_o=conv_w_o, w_out_o=w_out_o, ffn_norm=ffn_norm, w_gate=w_gate, w_up=w_up, w_down=w_down,
                   final_norm=final_norm)
    mom1 = dict(mix_norm_e=m_mix_norm_e, w_in_e=m_w_in_e, conv_w_e=m_conv_w_e, conv_b_e=m_conv_b_e, ln_g_e=m_ln_g_e,
                ln_b_e=m_ln_b_e, w_pool_e=m_w_pool_e, pool_scale_e=m_pool_scale_e, w_out_e=m_w_out_e, mix_norm_o=m_mix_norm_o,
                w_in_o=m_w_in_o, conv_w_o=m_conv_w_o, w_out_o=m_w_out_o, ffn_norm=m_ffn_norm, w_gate=m_w_gate, w_up=m_w_up,
                w_down=m_w_down, final_norm=m_final_norm)
    mom2 = dict(mix_norm_e=v_mix_norm_e, w_in_e=v_w_in_e, conv_w_e=v_conv_w_e, conv_b_e=v_conv_b_e, ln_g_e=v_ln_g_e,
                ln_b_e=v_ln_b_e, w_pool_e=v_w_pool_e, pool_scale_e=v_pool_scale_e, w_out_e=v_w_out_e, mix_norm_o=v_mix_norm_o,
                w_in_o=v_w_in_o, conv_w_o=v_conv_w_o, w_out_o=v_w_out_o, ffn_norm=v_ffn_norm, w_gate=v_w_gate, w_up=v_w_up,
                w_down=v_w_down, final_norm=v_final_norm)
    names = list(weights)

    big = ("w_in_o", "w_out_o", "w_gate", "w_up", "w_down", "w_in_e", "w_out_e")
    delta, new_m, new_v = {}, {}, {}

    def update(nm):
        shape = weights[nm].shape
        rows = 1
        for dim in shape[:-1]:
            rows *= dim
        as2d = lambda a: a.reshape(rows, shape[-1])
        dl, mn, vn = seq(_adamw, "adamw_" + nm, as2d(weights[nm]), as2d(grad[nm]), as2d(mom1[nm]), as2d(mom2[nm]))
        delta[nm], new_m[nm], new_v[nm] = dl.reshape(shape), mn.reshape(shape), vn.reshape(shape)

    for nm in big[:2]:
        update(nm)
    share_end("layer0")
    for nm in big[2:5]:
        grad[nm] = shards[nm]
        update(nm)
    reduce_finish("mix0")
    sum_into("w_in_e", 0)
    sum_into("w_out_e", 1)
    for layer, m in enumerate(pool_mats):
        sum_into("w_pool_e", m, layer=layer, n_layers=n_pool)
    g_w_in_e, g_w_out_e, g_w_pool = _share_pieces("share_pieces_first", mats,
                                                  [shards["w_in_e"], shards["w_out_e"], shards["w_pool_e"]],
                                                  [(0,), (1,), pool_mats])
    grad["w_in_e"], grad["w_out_e"], grad["w_pool_e"] = g_w_in_e[None], g_w_out_e[None], g_w_pool[None]
    for nm in big[5:]:
        update(nm)

    packed_grads, small_stack = _small_wait(packed_grads, small_stack, small_sems, chain[0][0])
    me_dev = 4 * xi + 2 * yi + ci
    small_stack = jnp.where(lax.broadcasted_iota(jnp.int32, (N_DEV, 1, 1), 0) == me_dev, packed_grads[None], small_stack)
    small_sum = _unpack(_sum_devices(small_stack), [s.shape for s in small_partials])
    (g_mix_norm_e, g_conv_w_e_f, g_conv_b, g_ln_g, g_ln_b, g_pool_scale, g_mix_norm_o_f, g_conv_w_o_f,
     g_ffn_norm, g_final_norm) = small_sum

    def my_shard(full, axis):
        size = full.shape[axis] // N_CHIPS
        return lax.dynamic_slice_in_dim(full, me_k * size, size, axis)

    grad.update({
        "mix_norm_e": g_mix_norm_e, "conv_w_e": my_shard(g_conv_w_e_f, 1)[None], "conv_b_e": g_conv_b,
        "ln_g_e": g_ln_g, "ln_b_e": g_ln_b, "pool_scale_e": g_pool_scale,
        "mix_norm_o": my_shard(g_mix_norm_o_f, 1), "conv_w_o": my_shard(g_conv_w_o_f, 1)[None],
        "ffn_norm": g_ffn_norm, "final_norm": g_final_norm.reshape(final_norm.shape),
    })
    small = [nm for nm in names if nm not in big]
    shapes = [weights[nm].shape for nm in small]
    dl, mn, vn = _adamw("adamw_small", _pack([weights[nm] for nm in small]), _pack([grad[nm] for nm in small]),
                        _pack([mom1[nm] for nm in small]), _pack([mom2[nm] for nm in small]))
    for nm, a, b, c_ in zip(small, _unpack(dl, shapes), _unpack(mn, shapes), _unpack(vn, shapes)):
        delta[nm], new_m[nm], new_v[nm] = a, b, c_

    grad_x = dx.reshape(bsz, seq_len, d)
    return (loss, grad_x, *[grad[nm] for nm in names], *[delta[nm] for nm in names],
            *[new_m[nm] for nm in names], *[new_v[nm] for nm in names])
```

```python
import jax
import jax.numpy as jnp
from jax import lax
from jax.experimental import pallas as pl
from jax.experimental.pallas import tpu as pltpu

F32 = jnp.float32
BF16 = jnp.bfloat16
MESH_ID = pl.DeviceIdType.MESH
AXES = ("x", "y", "c")
N_CHIPS = 4
N_DEV = 8

EPS = 1e-6
POOL_WINDOWS = (2, 4, 8, 16)
ADAM_LR, ADAM_B1, ADAM_B2, ADAM_EPS, ADAM_WD, ADAM_STEP = 0.001, 0.9, 0.999, 1e-08, 0.01, 10

LANES = 128
CONV_HALO = 32
POOL_HALO = 16
SHORT_HALO = 16
V7X_VMEM_LIMIT = 56 * 1024 * 1024


def _cparams(*sem):
    return pltpu.CompilerParams(dimension_semantics=sem if sem else None, vmem_limit_bytes=V7X_VMEM_LIMIT)


def _pick(dim, prefs):
    for p in prefs:
        if p <= dim and dim % p == 0:
            return p
    return dim


def _sigmoid(x):
    return jax.nn.sigmoid(x)


_ANY = pl.BlockSpec(memory_space=pl.ANY)


def _behind(after, body, n_in):
    if not after:
        return body
    skip = len(after)

    def body_behind(*refs):
        return body(*refs[:n_in], *refs[n_in + skip:])

    return body_behind


_DOT_DIMS = {
    "nn": (((1,), (0,)), ((), ())),
    "nt": (((1,), (1,)), ((), ())),
    "tn": (((0,), (0,)), ((), ())),
}


def _mm(name, pairs, mode, out_dtypes, epilogue, extras=(), acc_of=None, tm=512, tn=512, tk=2048, row_chunk=0, after=()):
    a0, b0 = pairs[0]
    if mode == "nn":
        (m, k), n = a0.shape, b0.shape[1]
    elif mode == "nt":
        (m, k), n = a0.shape, b0.shape[0]
    else:
        (k, m), n = a0.shape, b0.shape[1]
    tm = _pick(m, (tm, 512, 256, 128, 64, 32, 16, 8))
    tn = _pick(n, (tn, 512, 256, 128))
    tk = _pick(k, (tk, 2048, 1024, 512, 256, 128))
    nk = k // tk
    n_pairs = len(pairs)
    acc_of = tuple(acc_of) if acc_of is not None else (0,) * n_pairs
    n_acc = max(acc_of) + 1
    n_ex, n_out = len(extras), len(out_dtypes)
    dims = _DOT_DIMS[mode]

    def body(*refs):
        a_refs = refs[:n_pairs]
        b_refs = refs[n_pairs:2 * n_pairs]
        e_refs = refs[2 * n_pairs:2 * n_pairs + n_ex]
        first_out = 2 * n_pairs + n_ex + len(after)
        o_refs = refs[first_out:first_out + n_out]
        acc_refs = refs[first_out + n_out:]

        def partial_sums(rows=None):
            sums = [None] * n_acc
            for p in range(n_pairs):
                a = a_refs[p][...] if rows is None else (a_refs[p][:, rows] if mode == "tn" else a_refs[p][rows, :])
                d = lax.dot_general(a, b_refs[p][...], dims, preferred_element_type=F32)
                sums[acc_of[p]] = d if sums[acc_of[p]] is None else sums[acc_of[p]] + d
            return sums

        if nk == 1 and row_chunk:
            for r0 in range(0, tm, row_chunk):
                rows = pl.ds(r0, row_chunk)
                epilogue(partial_sums(rows), [e.at[rows, :] for e in e_refs], [o.at[rows, :] for o in o_refs])
            return
        if nk == 1:
            epilogue(partial_sums(), e_refs, o_refs)
            return
        kk = pl.program_id(2)

        @pl.when(kk == 0)
        def _():
            for acc in acc_refs:
                acc[...] = jnp.zeros_like(acc)

        for acc, s in zip(acc_refs, partial_sums()):
            acc[...] += s

        @pl.when(kk == nk - 1)
        def _():
            epilogue([acc[...] for acc in acc_refs], e_refs, o_refs)

    if mode == "nn":
        a_spec = pl.BlockSpec((tm, tk), lambda i, j, kk: (i, kk))
        b_spec = pl.BlockSpec((tk, tn), lambda i, j, kk: (kk, j))
    elif mode == "nt":
        a_spec = pl.BlockSpec((tm, tk), lambda i, j, kk: (i, kk))
        b_spec = pl.BlockSpec((tn, tk), lambda i, j, kk: (j, kk))
    else:
        a_spec = pl.BlockSpec((tk, tm), lambda i, j, kk: (kk, i))
        b_spec = pl.BlockSpec((tk, tn), lambda i, j, kk: (kk, j))
    o_spec = pl.BlockSpec((tm, tn), lambda i, j, kk: (i, j))
    outs = pl.pallas_call(
        body,
        name=name,
        grid=(m // tm, n // tn, nk),
        in_specs=[a_spec] * n_pairs + [b_spec] * n_pairs + [o_spec] * n_ex
        + [pl.BlockSpec(memory_space=pl.ANY)] * len(after),
        out_specs=[o_spec] * n_out,
        out_shape=[jax.ShapeDtypeStruct((m, n), dt) for dt in out_dtypes],
        scratch_shapes=[pltpu.VMEM((tm, tn), F32) for _ in range(n_acc)] if nk > 1 else [],
        compiler_params=_cparams("parallel", "parallel", "arbitrary"),
    )(*[p[0] for p in pairs], *[p[1] for p in pairs], *extras, *after)
    return outs


def _mm_rows(name, pairs, mode, out_dtypes, epilogue, extras=(), vecs=(), n_sums=0, tm=512, tk=2048, row_chunk=128,
             chunk_dots=True, after=()):
    a0, b0 = pairs[0]
    (m, k), n = a0.shape, (b0.shape[1] if mode == "nn" else b0.shape[0])
    tm = _pick(m, (tm, 512, 256, 128, 64, 32, 16, 8))
    tk = _pick(k, (tk, 2048, 1024, 512, 256, 128))
    row_chunk = min(row_chunk, tm)
    nk = k // tk
    n_pairs, n_ex, n_vec, n_out = len(pairs), len(extras), len(vecs), len(out_dtypes)
    dims = _DOT_DIMS[mode]

    def body(*refs):
        a_refs = refs[:n_pairs]
        b_refs = refs[n_pairs:2 * n_pairs]
        e_refs = refs[2 * n_pairs:2 * n_pairs + n_ex]
        v_refs = refs[2 * n_pairs + n_ex:2 * n_pairs + n_ex + n_vec]
        first_out = 2 * n_pairs + n_ex + n_vec + len(after)
        o_refs = refs[first_out:first_out + n_out]
        s_refs = refs[first_out + n_out:first_out + n_out + n_sums]
        acc_refs = refs[first_out + n_out + n_sums:]
        i, kk = pl.program_id(0), pl.program_id(1)

        @pl.when((i == 0) & (kk == 0))
        def _():
            for s in s_refs:
                s[...] = jnp.zeros_like(s)

        def dots(rows):
            total = None
            for p in range(n_pairs):
                d = lax.dot_general(a_refs[p][rows, :], b_refs[p][...], dims, preferred_element_type=F32)
                total = d if total is None else total + d
            return total

        def finish(acc_of_rows):
            for r0 in range(0, tm, row_chunk):
                rows = pl.ds(r0, row_chunk)
                epilogue(acc_of_rows(rows), [e.at[rows, :] for e in e_refs], v_refs, [o.at[rows, :] for o in o_refs], s_refs)

        if nk == 1 and chunk_dots:
            finish(dots)
            return
        acc = acc_refs[0]
        if nk == 1:
            acc[...] = dots(slice(None))
            finish(lambda rows: acc[rows, :])
            return

        @pl.when(kk == 0)
        def _():
            acc[...] = jnp.zeros_like(acc)

        acc[...] += dots(slice(None))

        @pl.when(kk == nk - 1)
        def _():
            finish(lambda rows: acc[rows, :])

    a_spec = pl.BlockSpec((tm, tk), lambda i, kk: (i, kk))
    b_mode = dict(pipeline_mode=pl.Buffered(1)) if nk == 1 else {}
    b_spec = (pl.BlockSpec((tk, n), lambda i, kk: (kk, 0), **b_mode) if mode == "nn"
              else pl.BlockSpec((n, tk), lambda i, kk: (0, kk), **b_mode))
    row_spec = pl.BlockSpec((tm, n), lambda i, kk: (i, 0))
    vec_spec = pl.BlockSpec((1, n), lambda i, kk: (0, 0))
    return pl.pallas_call(
        body, name=name, grid=(m // tm, nk),
        in_specs=[a_spec] * n_pairs + [b_spec] * n_pairs + [row_spec] * n_ex + [vec_spec] * n_vec + [_ANY] * len(after),
        out_specs=[row_spec] * n_out + [vec_spec] * n_sums,
        out_shape=[jax.ShapeDtypeStruct((m, n), dt) for dt in out_dtypes] + [jax.ShapeDtypeStruct((1, n), F32)] * n_sums,
        scratch_shapes=[pltpu.VMEM((tm, n), F32)] if (nk > 1 or not chunk_dots) else [],
        compiler_params=_cparams("arbitrary", "arbitrary"),
    )(*[p[0] for p in pairs], *[p[1] for p in pairs], *extras, *vecs, *after)


def _ep_rows_residual_norm(acc, ex, vecs, outs, sums):
    h = ex[0][...] + acc
    outs[0][...] = h
    r = lax.rsqrt(jnp.mean(h * h, axis=-1, keepdims=True) + EPS)
    outs[1][...] = (h * r * vecs[0][...]).astype(BF16)


def _ep_rows_loss_head(acc, ex, vecs, outs, sums):
    x = ex[0][...] + acc
    gain = vecs[0][...]
    inv_d = 1.0 / x.shape[-1]
    r = lax.rsqrt(jnp.mean(x * x, axis=-1, keepdims=True) + EPS)
    xhat = x * r
    err = xhat * gain - ex[1][...]
    sums[0][...] += (0.5 * inv_d) * jnp.sum(err * err, axis=0, keepdims=True)
    dy = err * inv_d
    sums[1][...] += jnp.sum(dy * xhat, axis=0, keepdims=True)
    dxh = dy * gain
    dh = r * (dxh - xhat * jnp.mean(dxh * xhat, axis=-1, keepdims=True))
    outs[0][...] = dh
    outs[1][...] = dh.astype(BF16)


def _ep_rows_norm_bwd(dn, ex, vecs, outs, sums):
    x = ex[0][...]
    r = lax.rsqrt(jnp.mean(x * x, axis=-1, keepdims=True) + EPS)
    xhat = x * r
    sums[0][...] += jnp.sum(dn * xhat, axis=0, keepdims=True)
    dxh = dn * vecs[0][...]
    dh = ex[1][...] + r * (dxh - xhat * jnp.mean(dxh * xhat, axis=-1, keepdims=True))
    outs[0][...] = dh
    if len(outs) > 1:
        outs[1][...] = dh.astype(BF16)


def _ep_store(accs, ex, outs):
    outs[0][...] = accs[0].astype(outs[0].dtype)


def _ep_residual(accs, ex, outs):
    outs[0][...] = ex[0][...] + accs[0]


def _ep_swiglu(accs, ex, outs):
    g, u = accs
    s = _sigmoid(g)
    gs = g * s
    outs[0][...] = (u * (s * (1.0 + g * (1.0 - s)))).astype(BF16)
    outs[1][...] = gs.astype(BF16)
    outs[2][...] = (gs * u).astype(BF16)


def _ep_swiglu_bwd(accs, ex, outs):
    d = accs[0]
    outs[0][...] = (d * ex[0][...].astype(F32)).astype(BF16)
    outs[1][...] = (d * ex[1][...].astype(F32)).astype(BF16)


def _rms_fwd(name, h, g, after=()):
    t, d = h.shape
    tr = _pick(t, (256, 128, 64, 32, 16, 8))

    def body(h_ref, g_ref, *rest):
        o_ref = rest[-1]
        x = h_ref[...]
        r = lax.rsqrt(jnp.mean(x * x, axis=-1, keepdims=True) + EPS)
        o_ref[...] = (x * r * g_ref[...]).astype(BF16)

    return pl.pallas_call(
        body, name=name, grid=(t // tr,),
        in_specs=[pl.BlockSpec((tr, d), lambda i: (i, 0)), pl.BlockSpec((1, d), lambda i: (0, 0))]
        + [pl.BlockSpec(memory_space=pl.ANY)] * len(after),
        out_specs=pl.BlockSpec((tr, d), lambda i: (i, 0)),
        out_shape=jax.ShapeDtypeStruct((t, d), BF16),
        compiler_params=_cparams("parallel"),
    )(h, g, *after)


def _rms_bwd(name, dn, h, g, dres, after=()):
    t, d = h.shape
    tr = _pick(t, (256, 128, 64, 32, 16, 8))

    def body(dn_ref, h_ref, g_ref, dres_ref, dh_ref, dhb_ref, dg_ref):
        x = h_ref[...]
        r = lax.rsqrt(jnp.mean(x * x, axis=-1, keepdims=True) + EPS)
        xhat = x * r
        dnv = dn_ref[...].astype(F32)

        @pl.when(pl.program_id(0) == 0)
        def _():
            dg_ref[...] = jnp.zeros_like(dg_ref)

        dg_ref[...] += jnp.sum(dnv * xhat, axis=0, keepdims=True)
        dxh = dnv * g_ref[...]
        dh = dres_ref[...] + r * (dxh - xhat * jnp.mean(dxh * xhat, axis=-1, keepdims=True))
        dh_ref[...] = dh
        dhb_ref[...] = dh.astype(BF16)

    row = pl.BlockSpec((tr, d), lambda i: (i, 0))
    vec = pl.BlockSpec((1, d), lambda i: (0, 0))
    return pl.pallas_call(
        _behind(after, body, 4), name=name, grid=(t // tr,),
        in_specs=[row, row, vec, row] + [_ANY] * len(after),
        out_specs=[row, row, vec],
        out_shape=[jax.ShapeDtypeStruct((t, d), F32), jax.ShapeDtypeStruct((t, d), BF16),
                   jax.ShapeDtypeStruct((1, d), F32)],
        compiler_params=_cparams("arbitrary"),
    )(dn, h, g, dres, *after)


def _loss_head(h, g, target, after=()):
    t, d = h.shape
    tr = _pick(t, (256, 128, 64, 32, 16, 8))

    def body(h_ref, g_ref, t_ref, loss_ref, dh_ref, dhb_ref, dg_ref):
        x = h_ref[...]
        gv = g_ref[...]
        r = lax.rsqrt(jnp.mean(x * x, axis=-1, keepdims=True) + EPS)
        xhat = x * r
        err = xhat * gv - t_ref[...]

        @pl.when(pl.program_id(0) == 0)
        def _():
            dg_ref[...] = jnp.zeros_like(dg_ref)
            loss_ref[...] = jnp.zeros_like(loss_ref)

        loss_ref[...] += jnp.full(loss_ref.shape, 0.5 / d, F32) * jnp.sum(err * err)
        dy = err * (1.0 / d)
        dg_ref[...] += jnp.sum(dy * xhat, axis=0, keepdims=True)
        dxh = dy * gv
        dh = r * (dxh - xhat * jnp.mean(dxh * xhat, axis=-1, keepdims=True))
        dh_ref[...] = dh
        dhb_ref[...] = dh.astype(BF16)

    row = pl.BlockSpec((tr, d), lambda i: (i, 0))
    vec = pl.BlockSpec((1, d), lambda i: (0, 0))
    return pl.pallas_call(
        _behind(after, body, 3), name="loss_head", grid=(t // tr,),
        in_specs=[row, vec, row] + [_ANY] * len(after),
        out_specs=[pl.BlockSpec((1, LANES), lambda i: (0, 0)), row, row, vec],
        out_shape=[jax.ShapeDtypeStruct((1, LANES), F32), jax.ShapeDtypeStruct((t, d), F32),
                   jax.ShapeDtypeStruct((t, d), BF16), jax.ShapeDtypeStruct((1, d), F32)],
        compiler_params=_cparams("arbitrary"),
    )(h, g, target, *after)


def _cur(ts, width, col):
    return pl.BlockSpec((ts, width), lambda i: (i, col))


def _prev_halo(ts, halo, width, col):
    per = ts // halo
    return pl.BlockSpec((halo, width), lambda i: (jnp.maximum(i * per - 1, 0), col))


def _next_halo(ts, halo, width, col, n_rows):
    per = ts // halo
    last = n_rows // halo - 1
    return pl.BlockSpec((halo, width), lambda i: (jnp.minimum((i + 1) * per, last), col))


def _full(shape):
    nd = len(shape)
    return pl.BlockSpec(shape, lambda i: (0,) * nd)


def _shift_down(x, n):
    return x if n == 0 else pltpu.roll(x, n, 0)


def _shift_up(x, n):
    return x if n == 0 else pltpu.roll(x, x.shape[0] - n, 0)


CONV_ROWS = 32


def _conv_block_shape(channels, ts):
    return min(CONV_ROWS, ts), min(LANES, channels)


SUBLANES = 8


def _fill_shifted(rot_ref, ext):
    rot_ref[0] = ext
    for r in range(1, SUBLANES):
        rot_ref[r] = _shift_up(ext, r)


def _window(rot_ref, first, rows, c0, cw):
    r = first % SUBLANES
    return rot_ref[r, first - r:first - r + rows, c0:c0 + cw]


def _causal_taps(rot_ref, w_ref, halo, taps, r0, c0, rows, cw):
    acc = jnp.zeros((rows, cw), F32)
    for k in range(taps):
        acc = acc + w_ref[k:k + 1, c0:c0 + cw] * _window(rot_ref, halo + r0 - (taps - 1 - k), rows, c0, cw)
    return acc


def _pool_counts(i, ns, ts, w):
    pos = (i % ns) * ts + lax.broadcasted_iota(jnp.int32, (ts, 1), 0)
    return jnp.minimum(pos + 1, w).astype(F32)


def _pooled(cur, prev_tail, w, cnt):
    s = jnp.concatenate([prev_tail, cur], axis=0)
    d = 1
    while d < w:
        s = s + _shift_down(s, d)
        d *= 2
    return s[POOL_HALO:, :] / cnt - cur


def _mixer_e_fwd(u, conv_w, conv_b, ln_g, ln_b, w_pool, scale, seq, ts, after=()):
    t = u.shape[0]
    dc = conv_b.shape[1]
    ng, pg = w_pool.shape[0], w_pool.shape[1]
    taps = conv_w.shape[0]
    ns = seq // ts

    def body(val_ref, gate_ref, b_ref, pval_ref, pgate_ref, pb_ref, cw_ref, cb_ref, g_ref, be_ref, wp_ref, sc_ref,
             a2_ref, cat_ref, rot_ref):
        i = pl.program_id(0)
        keep_prev = jnp.where(i % ns == 0, 0.0, 1.0)
        a1 = val_ref[...] * _sigmoid(gate_ref[...])
        pa1 = pval_ref[...] * _sigmoid(pgate_ref[...]) * keep_prev
        _fill_shifted(rot_ref, jnp.concatenate([pa1, a1], axis=0))
        rows, cw = _conv_block_shape(dc, ts)
        for c0 in range(0, dc, cw):
            for r0 in range(0, ts, rows):
                acc = _causal_taps(rot_ref, cw_ref, CONV_HALO, taps, r0, c0, rows, cw)
                a2_ref[r0:r0 + rows, c0:c0 + cw] = acc + cb_ref[:, c0:c0 + cw]
        a2 = a2_ref[...]
        mu = jnp.mean(a2, axis=-1, keepdims=True)
        xc = a2 - mu
        rstd = lax.rsqrt(jnp.mean(xc * xc, axis=-1, keepdims=True) + EPS)
        a3 = xc * rstd * g_ref[...] + be_ref[...]
        cat_ref[:, 0:dc] = (a3 * _sigmoid(a3)).astype(BF16)
        for g in range(ng):
            lo, hi = g * pg, (g + 1) * pg
            w = POOL_WINDOWS[g]
            p = _pooled(b_ref[:, lo:hi], pb_ref[:, lo:hi] * keep_prev, w, _pool_counts(i, ns, ts, w))
            q = jnp.dot(p.astype(BF16), wp_ref[g].astype(BF16), preferred_element_type=F32)
            cat_ref[:, dc + lo:dc + hi] = (q * sc_ref[:, lo:hi]).astype(BF16)

    return pl.pallas_call(
        _behind(after, body, 12), name="mixer_e_fwd", grid=(t // ts,),
        in_specs=[_cur(ts, dc, 0), _cur(ts, dc, 1), _cur(ts, dc, 2),
                  _prev_halo(ts, CONV_HALO, dc, 0), _prev_halo(ts, CONV_HALO, dc, 1), _prev_halo(ts, POOL_HALO, dc, 2),
                  _full(conv_w.shape), _full(conv_b.shape), _full(ln_g.shape), _full(ln_b.shape),
                  _full(w_pool.shape), _full(scale.shape)] + [_ANY] * len(after),
        out_specs=[_cur(ts, dc, 0), _cur(ts, 2 * dc, 0)],
        out_shape=[jax.ShapeDtypeStruct((t, dc), F32), jax.ShapeDtypeStruct((t, 2 * dc), BF16)],
        scratch_shapes=[pltpu.VMEM((SUBLANES, CONV_HALO + ts, dc), F32)],
        compiler_params=_cparams("parallel"),
    )(u, u, u, u, u, u, conv_w, conv_b, ln_g, ln_b, w_pool, scale, *after)


def _mixer_e_bwd_norm(dcat, a2, ln_g, ln_b, ts, after=()):
    t, dc = a2.shape

    def body(d_ref, a2_ref, g_ref, be_ref, da2_ref, dg_ref, db_ref, dcb_ref):
        x = a2_ref[...]
        gv = g_ref[...]
        mu = jnp.mean(x, axis=-1, keepdims=True)
        xc = x - mu
        rstd = lax.rsqrt(jnp.mean(xc * xc, axis=-1, keepdims=True) + EPS)
        xhat = xc * rstd
        a3 = xhat * gv + be_ref[...]
        sg = _sigmoid(a3)
        da3 = d_ref[...] * (sg * (1.0 + a3 * (1.0 - sg)))
        dxh = da3 * gv
        da2 = rstd * (dxh - jnp.mean(dxh, axis=-1, keepdims=True)
                      - xhat * jnp.mean(dxh * xhat, axis=-1, keepdims=True))
        da2_ref[...] = da2

        @pl.when(pl.program_id(0) == 0)
        def _():
            dg_ref[...] = jnp.zeros_like(dg_ref)
            db_ref[...] = jnp.zeros_like(db_ref)
            dcb_ref[...] = jnp.zeros_like(dcb_ref)

        dg_ref[...] += jnp.sum(da3 * xhat, axis=0, keepdims=True)
        db_ref[...] += jnp.sum(da3, axis=0, keepdims=True)
        dcb_ref[...] += jnp.sum(da2, axis=0, keepdims=True)

    vec = _full((1, dc))
    return pl.pallas_call(
        _behind(after, body, 4), name="mixer_e_bwd_norm", grid=(t // ts,),
        in_specs=[_cur(ts, dc, 0), _cur(ts, dc, 0), vec, vec] + [_ANY] * len(after),
        out_specs=[_cur(ts, dc, 0), vec, vec, vec],
        out_shape=[jax.ShapeDtypeStruct((t, dc), F32)] + [jax.ShapeDtypeStruct((1, dc), F32)] * 3,
        compiler_params=_cparams("arbitrary"),
    )(dcat, a2, ln_g, ln_b, *after)


def _mixer_e_bwd_mix(da2, dcat, u, conv_w, w_pool, scale, seq, ts, after=()):
    t, dc = da2.shape
    ng, pg = w_pool.shape[0], w_pool.shape[1]
    taps = conv_w.shape[0]
    ns = seq // ts

    def body(da2_ref, nda2_ref, dp_ref, ndp_ref, val_ref, gate_ref, b_ref, pval_ref, pgate_ref, pb_ref,
             cw_ref, wp_ref, sc_ref, du_ref, dcw_ref, dwp_ref, dsc_ref, rota_ref, rotd_ref):
        i = pl.program_id(0)
        keep_prev = jnp.where(i % ns == 0, 0.0, 1.0)
        keep_next = jnp.where(i % ns == ns - 1, 0.0, 1.0)

        @pl.when(i == 0)
        def _():
            dcw_ref[...] = jnp.zeros_like(dcw_ref)
            dwp_ref[...] = jnp.zeros_like(dwp_ref)
            dsc_ref[...] = jnp.zeros_like(dsc_ref)

        val = val_ref[...]
        sg = _sigmoid(gate_ref[...])
        a1 = val * sg
        pa1 = pval_ref[...] * _sigmoid(pgate_ref[...]) * keep_prev
        _fill_shifted(rota_ref, jnp.concatenate([pa1, a1], axis=0))
        _fill_shifted(rotd_ref, jnp.concatenate([da2_ref[...], nda2_ref[...] * keep_next], axis=0))
        rows, cw = _conv_block_shape(dc, ts)
        for c0 in range(0, dc, cw):
            lanes = slice(c0, c0 + cw)
            dw = [jnp.zeros((SUBLANES, cw), F32)] * taps
            for r0 in range(0, ts, rows):
                blk = slice(r0, r0 + rows)
                d_blk = da2_ref[blk, lanes]
                da1 = jnp.zeros((rows, cw), F32)
                for k in range(taps):
                    sh = taps - 1 - k
                    prod = d_blk * _window(rota_ref, CONV_HALO + r0 - sh, rows, c0, cw)
                    for f in range(0, rows, SUBLANES):
                        dw[k] = dw[k] + prod[f:f + SUBLANES, :]
                    da1 = da1 + cw_ref[k:k + 1, lanes] * _window(rotd_ref, r0 + sh, rows, c0, cw)
                sg_b = sg[blk, lanes]
                du_ref[blk, lanes] = (da1 * sg_b).astype(BF16)
                du_ref[blk, dc + c0:dc + c0 + cw] = (da1 * a1[blk, lanes] * (1.0 - sg_b)).astype(BF16)
            for k in range(taps):
                dcw_ref[k:k + 1, lanes] += jnp.sum(dw[k], axis=0, keepdims=True)

        for g in range(ng):
            lo, hi = g * pg, (g + 1) * pg
            w = POOL_WINDOWS[g]
            cnt = _pool_counts(i, ns, ts, w)
            wpb = wp_ref[g].astype(BF16)
            sc = sc_ref[:, lo:hi]
            p = _pooled(b_ref[:, lo:hi], pb_ref[:, lo:hi] * keep_prev, w, cnt)
            pb16 = p.astype(BF16)
            q = jnp.dot(pb16, wpb, preferred_element_type=F32)
            dout = dp_ref[:, lo:hi]
            dsc_ref[:, lo:hi] += jnp.sum(dout * q, axis=0, keepdims=True)
            dq = (dout * sc).astype(BF16)
            dwp_ref[g] += lax.dot_general(pb16, dq, _DOT_DIMS["tn"], preferred_element_type=F32)
            dpool = lax.dot_general(dq, wpb, _DOT_DIMS["nt"], preferred_element_type=F32)
            ndq = (ndp_ref[:, lo:hi] * sc * keep_next).astype(BF16)
            ndpool = lax.dot_general(ndq, wpb, _DOT_DIMS["nt"], preferred_element_type=F32)
            s = jnp.concatenate([dpool / cnt, ndpool * (1.0 / w)], axis=0)
            d = 1
            while d < w:
                s = s + _shift_up(s, d)
                d *= 2
            du_ref[:, 2 * dc + lo:2 * dc + hi] = (s[:ts, :] - dpool).astype(BF16)

    return pl.pallas_call(
        _behind(after, body, 13), name="mixer_e_bwd_mix", grid=(t // ts,),
        in_specs=[_cur(ts, dc, 0), _next_halo(ts, CONV_HALO, dc, 0, t),
                  _cur(ts, dc, 1), _next_halo(ts, POOL_HALO, dc, 1, t),
                  _cur(ts, dc, 0), _cur(ts, dc, 1), _cur(ts, dc, 2),
                  _prev_halo(ts, CONV_HALO, dc, 0), _prev_halo(ts, CONV_HALO, dc, 1), _prev_halo(ts, POOL_HALO, dc, 2),
                  _full(conv_w.shape), _full(w_pool.shape), _full(scale.shape)] + [_ANY] * len(after),
        out_specs=[_cur(ts, 3 * dc, 0), _full(conv_w.shape), _full(w_pool.shape), _full(scale.shape)],
        out_shape=[jax.ShapeDtypeStruct((t, 3 * dc), BF16), jax.ShapeDtypeStruct(conv_w.shape, F32),
                   jax.ShapeDtypeStruct(w_pool.shape, F32), jax.ShapeDtypeStruct(scale.shape, F32)],
        scratch_shapes=[pltpu.VMEM((SUBLANES, ts + CONV_HALO, dc), F32)] * 2,
        compiler_params=_cparams("arbitrary"),
    )(da2, da2, dcat, dcat, u, u, u, u, u, u, conv_w, w_pool, scale, *after)


def _mixer_o_fwd(u, conv_w, seq, ts, after=()):
    t = u.shape[0]
    d = conv_w.shape[1]
    taps = conv_w.shape[0]
    ns = seq // ts

    def body(gb_ref, gc_ref, v_ref, pgc_ref, pv_ref, cw_ref, y_ref):
        keep_prev = jnp.where(pl.program_id(0) % ns == 0, 0.0, 1.0)
        f32 = lambda ref: ref[...].astype(F32)
        ext = jnp.concatenate([f32(pgc_ref) * f32(pv_ref) * keep_prev, f32(gc_ref) * f32(v_ref)], axis=0)
        cc = jnp.zeros_like(ext)
        for k in range(taps):
            cc = cc + cw_ref[k:k + 1, :] * _shift_down(ext, taps - 1 - k)
        y_ref[...] = (f32(gb_ref) * cc[SHORT_HALO:, :]).astype(BF16)

    return pl.pallas_call(
        _behind(after, body, 6), name="mixer_o_fwd", grid=(t // ts,),
        in_specs=[_cur(ts, d, 0), _cur(ts, d, 1), _cur(ts, d, 2),
                  _prev_halo(ts, SHORT_HALO, d, 1), _prev_halo(ts, SHORT_HALO, d, 2), _full(conv_w.shape)]
        + [_ANY] * len(after),
        out_specs=_cur(ts, d, 0),
        out_shape=jax.ShapeDtypeStruct((t, d), BF16),
        compiler_params=_cparams("parallel"),
    )(u, u, u, u, u, conv_w, *after)


def _mixer_o_bwd(dy, u, conv_w, seq, ts, after=()):
    t = u.shape[0]
    d = conv_w.shape[1]
    taps = conv_w.shape[0]
    ns = seq // ts

    def body(dy_ref, ndy_ref, gb_ref, gc_ref, v_ref, pgc_ref, pv_ref, ngb_ref, cw_ref, du_ref, dcw_ref):
        i = pl.program_id(0)
        keep_prev = jnp.where(i % ns == 0, 0.0, 1.0)
        keep_next = jnp.where(i % ns == ns - 1, 0.0, 1.0)

        @pl.when(i == 0)
        def _():
            dcw_ref[...] = jnp.zeros_like(dcw_ref)

        f32 = lambda ref: ref[...].astype(F32)
        gb, gc, v, dyv = f32(gb_ref), f32(gc_ref), f32(v_ref), f32(dy_ref)
        ext = jnp.concatenate([f32(pgc_ref) * f32(pv_ref) * keep_prev, gc * v], axis=0)
        dcc = dyv * gb
        ext_d = jnp.concatenate([dcc, f32(ndy_ref) * f32(ngb_ref) * keep_next], axis=0)
        cc = jnp.zeros_like(ext)
        dcv = jnp.zeros_like(ext_d)
        for k in range(taps):
            sh = taps - 1 - k
            shifted = _shift_down(ext, sh)
            cc = cc + cw_ref[k:k + 1, :] * shifted
            dcw_ref[k:k + 1, :] += jnp.sum(dcc * shifted[SHORT_HALO:, :], axis=0, keepdims=True)
            dcv = dcv + cw_ref[k:k + 1, :] * _shift_up(ext_d, sh)
        dcv = dcv[:ts, :]
        du_ref[:, 0:d] = (dyv * cc[SHORT_HALO:, :]).astype(BF16)
        du_ref[:, d:2 * d] = (dcv * v).astype(BF16)
        du_ref[:, 2 * d:3 * d] = (dcv * gc).astype(BF16)

    return pl.pallas_call(
        _behind(after, body, 9), name="mixer_o_bwd", grid=(t // ts,),
        in_specs=[_cur(ts, d, 0), _next_halo(ts, SHORT_HALO, d, 0, t),
                  _cur(ts, d, 0), _cur(ts, d, 1), _cur(ts, d, 2),
                  _prev_halo(ts, SHORT_HALO, d, 1), _prev_halo(ts, SHORT_HALO, d, 2),
                  _next_halo(ts, SHORT_HALO, d, 0, t), _full(conv_w.shape)] + [_ANY] * len(after),
        out_specs=[_cur(ts, 3 * d, 0), _full(conv_w.shape)],
        out_shape=[jax.ShapeDtypeStruct((t, 3 * d), BF16), jax.ShapeDtypeStruct(conv_w.shape, F32)],
        compiler_params=_cparams("arbitrary"),
    )(dy, dy, u, u, u, u, u, u, conv_w, *after)


def _cast_into_full(name, mat, w, layer, chip, after=()):
    tr = _pick(mat.sr, (512, 256, 128, 64, 32, 16))
    per = mat.sr // tr

    def body(chip_ref, w_ref, *rest):
        o_ref = rest[-1]
        o_ref[...] = w_ref[...].astype(BF16)

    if mat.kind == "col":
        o_spec = pl.BlockSpec((tr, mat.sc), lambda i, chip_ref: (i, chip_ref[0]))
    else:
        o_spec = pl.BlockSpec((tr, mat.sc), lambda i, chip_ref: (chip_ref[0] * per + i, 0))
    return pl.pallas_call(
        body, name=name,
        grid_spec=pltpu.PrefetchScalarGridSpec(
            num_scalar_prefetch=1, grid=(per,),
            in_specs=[pl.BlockSpec((None, tr, mat.sc), lambda i, chip_ref: (layer, i, 0))] + [_ANY] * len(after),
            out_specs=o_spec),
        out_shape=jax.ShapeDtypeStruct(mat.full_shape, BF16),
        compiler_params=_cparams("parallel"),
    )(chip, w, *after)


def _adamw(name, w, g, m, v, after=()):
    r, c = w.shape
    tr = _pick(r, (256, 128, 64, 32, 16, 8)) if c > 1024 else _pick(r, (512, 256, 128, 64, 32, 16, 8))
    bc1 = 1.0 - ADAM_B1 ** ADAM_STEP
    bc2 = 1.0 - ADAM_B2 ** ADAM_STEP

    def body(w_ref, g_ref, m_ref, v_ref, d_ref, mo_ref, vo_ref):
        gv = g_ref[...]
        mn = ADAM_B1 * m_ref[...] + (1.0 - ADAM_B1) * gv
        vn = ADAM_B2 * v_ref[...] + (1.0 - ADAM_B2) * (gv * gv)
        mo_ref[...] = mn
        vo_ref[...] = vn
        d_ref[...] = -ADAM_LR * ((mn / bc1) / (jnp.sqrt(vn / bc2) + ADAM_EPS) + ADAM_WD * w_ref[...])

    spec = pl.BlockSpec((tr, c), lambda i: (i, 0))
    return pl.pallas_call(_behind(after, body, 4), name=name, grid=(r // tr,),
                          in_specs=[spec] * 4 + [_ANY] * len(after), out_specs=[spec] * 3,
                          out_shape=[jax.ShapeDtypeStruct((r, c), F32)] * 3,
                          compiler_params=_cparams("parallel"))(w, g, m, v, *after)


def _aligned(offset, multiple):
    return offset if isinstance(offset, int) else pl.multiple_of(offset, multiple)


class _Mat:
    def __init__(self, kind, shard_shape):
        self.kind = kind
        self.sr, self.sc = shard_shape
        self.full_shape = (self.sr, self.sc * N_CHIPS) if kind == "col" else (self.sr * N_CHIPS, self.sc)
        self.pr, self.pc = self.sr // 2, self.sc

    def piece(self, ref, k, h):
        if self.kind == "col":
            return ref.at[pl.ds(_aligned(h * self.pr, 16), self.pr), pl.ds(_aligned(k * self.sc, LANES), self.sc)]
        return ref.at[pl.ds(_aligned(k * self.sr + h * self.pr, 16), self.pr), :]

    def shard(self, ref, k):
        if self.kind == "col":
            return ref.at[:, pl.ds(_aligned(k * self.sc, LANES), self.sc)]
        return ref.at[pl.ds(_aligned(k * self.sr, 16), self.sr), :]

    def half(self, ref, h):
        return ref.at[pl.ds(_aligned(h * self.pr, 16), self.pr), :]


def _place():
    x, y, c = lax.axis_index("x"), lax.axis_index("y"), lax.axis_index("c")
    others = [(1 - x, y), (x, 1 - y), (1 - x, 1 - y)]
    return x, y, c, others


_HBM = pl.BlockSpec(memory_space=pltpu.HBM)
_SEM = pl.BlockSpec(memory_space=pltpu.SEMAPHORE)
_TOKEN = jax.ShapeDtypeStruct((8, LANES), F32)
_TOKEN_SPEC = pl.BlockSpec(memory_space=pltpu.VMEM)


def _split_params():
    return pltpu.CompilerParams(has_side_effects=pltpu.SideEffectType.DATAFLOW_SIDE_EFFECTING)


def _in_hbm(a):
    return pltpu.with_memory_space_constraint(a, pltpu.HBM)


def _copy_to(src, dst, send_sem, recv_sem, to):
    return pltpu.make_async_remote_copy(src_ref=src, dst_ref=dst, send_sem=send_sem, recv_sem=recv_sem,
                                        device_id=to, device_id_type=MESH_ID)


def _place_small(packed, chip):
    rows, cols = packed.shape

    def body(chip_ref, p_ref, o_ref):
        o_ref[...] = p_ref[...]

    return pl.pallas_call(
        body, name="place_small",
        grid_spec=pltpu.PrefetchScalarGridSpec(
            num_scalar_prefetch=1, grid=(1,),
            in_specs=[pl.BlockSpec((rows, cols), lambda i, chip_ref: (0, 0))],
            out_specs=pl.BlockSpec((None, rows, cols), lambda i, chip_ref: (chip_ref[0], 0, 0))),
        out_shape=jax.ShapeDtypeStruct((N_CHIPS, rows, cols), F32),
        compiler_params=_cparams("arbitrary"),
    )(chip, packed)


def _gather_start(name, gmats, gfulls, small_all=None):
    n = len(gmats)
    n_in = n + (1 if small_all is not None else 0)

    def body(*refs):
        full_refs = refs[:n]
        outs = refs[n_in:]
        send_sem, recv_sem, token = outs[n_in], outs[n_in + 1], outs[n_in + 2]
        x, y, c, others = _place()
        me_k = 2 * x + y
        if small_all is not None:
            mine = refs[n].at[me_k]
            for ox, oy in others:
                _copy_to(mine, mine, send_sem, recv_sem, (ox, oy, c)).start()
        for m in range(n):
            mine = gmats[m].piece(full_refs[m], me_k, c)
            for ox, oy in others:
                _copy_to(mine, mine, send_sem, recv_sem, (ox, oy, c)).start()
        token[...] = jnp.zeros_like(token)

    operands = [_in_hbm(f) for f in gfulls] + ([_in_hbm(small_all)] if small_all is not None else [])
    outs = pl.pallas_call(
        body, name=name,
        in_specs=[_HBM] * n_in,
        out_specs=[_HBM] * n_in + [_SEM, _SEM, _TOKEN_SPEC],
        out_shape=[pltpu.HBM(a.shape, a.dtype) for a in operands] + [pltpu.SemaphoreType.DMA(())] * 2 + [_TOKEN],
        input_output_aliases={m: m for m in range(n_in)},
        compiler_params=_split_params(),
    )(*operands)
    return list(outs[:n]), (outs[n] if small_all is not None else None), (outs[n_in], outs[n_in + 1]), outs[n_in + 2]


def _gather_pass(name, gmats, gfulls, small_all, sems, after):
    k = len(gmats)
    n_buf = k + (1 if small_all is not None else 0)

    def body(*refs):
        bufs = refs[:n_buf]
        send_sem, recv_sem = refs[n_buf], refs[n_buf + 1]
        outs = refs[n_buf + 3:]
        fsend, frecv, token = outs[n_buf], outs[n_buf + 1], outs[n_buf + 2]
        x, y, c, others = _place()
        me_k = 2 * x + y
        sibling = (x, y, 1 - c)
        for m in range(k):
            for ox, oy in others:
                got = gmats[m].piece(bufs[m], 2 * ox + oy, c)
                _copy_to(got, got, send_sem, recv_sem, sibling).wait_recv()
        if small_all is not None:
            for ox, oy in others:
                got = bufs[k].at[2 * ox + oy]
                _copy_to(got, got, send_sem, recv_sem, sibling).wait_recv()
        for m in range(k):
            mine = gmats[m].piece(bufs[m], me_k, c)
            for _ in others:
                _copy_to(mine, mine, send_sem, recv_sem, sibling).wait_send()
        if small_all is not None:
            for _ in others:
                _copy_to(bufs[k].at[me_k], bufs[k].at[me_k], send_sem, recv_sem, sibling).wait_send()
        for m in range(k):
            for ox, oy in others:
                got = gmats[m].piece(bufs[m], 2 * ox + oy, c)
                _copy_to(got, got, fsend, frecv, sibling).start()
        token[...] = jnp.zeros_like(token)

    operands = [_in_hbm(f) for f in gfulls] + ([_in_hbm(small_all)] if small_all is not None else [])
    outs = pl.pallas_call(
        body, name=name,
        in_specs=[_HBM] * n_buf + [_SEM, _SEM, _ANY],
        out_specs=[_HBM] * n_buf + [_SEM, _SEM, _TOKEN_SPEC],
        out_shape=[pltpu.HBM(a.shape, a.dtype) for a in operands] + [pltpu.SemaphoreType.DMA(())] * 2 + [_TOKEN],
        input_output_aliases={i: i for i in range(n_buf)},
        compiler_params=_split_params(),
    )(*operands, sems[0], sems[1], after)
    return list(outs[:n_buf]), (outs[n_buf], outs[n_buf + 1]), outs[n_buf + 2]


def _gather_done(name, gmats, gfulls, sems, after):
    k = len(gmats)

    def body(*refs):
        bufs = refs[:k]
        send_sem, recv_sem = refs[k], refs[k + 1]
        x, y, c, others = _place()
        sibling = (x, y, 1 - c)
        for m in range(k):
            for ox, oy in others:
                got = gmats[m].piece(bufs[m], 2 * ox + oy, 1 - c)
                _copy_to(got, got, send_sem, recv_sem, sibling).wait_recv()
        for m in range(k):
            for ox, oy in others:
                sent = gmats[m].piece(bufs[m], 2 * ox + oy, c)
                _copy_to(sent, sent, send_sem, recv_sem, sibling).wait_send()

    outs = pl.pallas_call(
        body, name=name,
        in_specs=[_HBM] * k + [_SEM, _SEM, _ANY], out_specs=[_HBM] * k,
        out_shape=[pltpu.HBM(a.shape, a.dtype) for a in gfulls],
        input_output_aliases={i: i for i in range(k)},
        compiler_params=_split_params(),
    )(*[_in_hbm(f) for f in gfulls], sems[0], sems[1], after)
    return list(outs)


_FLIPS = [(fx, fy, fc) for fx in (0, 1) for fy in (0, 1) for fc in (0, 1) if (fx, fy, fc) != (0, 0, 0)]


def _small_start(packed, after):
    def body(small_ref, after_ref, small_thru, land_ref, send_sem, recv_sem, token):
        x, y, c, _ = _place()
        me = 4 * x + 2 * y + c
        for fx, fy, fc in _FLIPS:
            _copy_to(small_ref, land_ref.at[me], send_sem, recv_sem, (x ^ fx, y ^ fy, c ^ fc)).start()
        token[...] = jnp.zeros_like(token)

    outs = pl.pallas_call(
        body, name="small_grads_start",
        in_specs=[_HBM, _ANY], out_specs=[_HBM, _HBM, _SEM, _SEM, _TOKEN_SPEC],
        out_shape=[pltpu.HBM(packed.shape, F32), pltpu.HBM((N_DEV,) + packed.shape, F32)]
        + [pltpu.SemaphoreType.DMA(())] * 2 + [_TOKEN],
        input_output_aliases={0: 0},
        compiler_params=_split_params(),
    )(_in_hbm(packed), after)
    return outs[0], outs[1], (outs[2], outs[3]), outs[4]


def _small_wait(packed, landed, sems, after):
    def body(small_ref, land_ref, send_sem, recv_sem, after_ref, small_thru, land_thru):
        x, y, c, _ = _place()
        for fx, fy, fc in _FLIPS:
            got = land_ref.at[4 * (x ^ fx) + 2 * (y ^ fy) + (c ^ fc)]
            _copy_to(got, got, send_sem, recv_sem, (x, y, 1 - c)).wait_recv()
        for _ in _FLIPS:
            _copy_to(small_ref, small_ref, send_sem, recv_sem, (x, y, 1 - c)).wait_send()

    outs = pl.pallas_call(
        body, name="small_grads_wait",
        in_specs=[_HBM, _HBM, _SEM, _SEM, _ANY], out_specs=[_HBM, _HBM],
        out_shape=[pltpu.HBM(packed.shape, F32), pltpu.HBM(landed.shape, F32)],
        input_output_aliases={0: 0, 1: 1},
        compiler_params=_split_params(),
    )(_in_hbm(packed), _in_hbm(landed), sems[0], sems[1], after)
    return outs[0], outs[1]


def _exchange_start(name, mats, grads):
    n = len(mats)

    def body(*refs):
        g_refs = refs[:n]
        outs = refs[n:]
        land_refs = outs[n:2 * n]
        send_sem, recv_sem, token = outs[2 * n], outs[2 * n + 1], outs[2 * n + 2]
        x, y, c, _ = _place()
        for m in range(n):
            for k in range(N_CHIPS):
                _copy_to(mats[m].piece(g_refs[m], k, 1 - c), land_refs[m].at[k], send_sem, recv_sem, (x, y, 1 - c)).start()
        token[...] = jnp.zeros_like(token)

    outs = pl.pallas_call(
        body, name=name,
        in_specs=[_HBM] * n,
        out_specs=[_HBM] * (2 * n) + [_SEM, _SEM, _TOKEN_SPEC],
        out_shape=[pltpu.HBM(mt.full_shape, BF16) for mt in mats]
        + [pltpu.HBM((N_CHIPS, mt.pr, mt.pc), BF16) for mt in mats] + [pltpu.SemaphoreType.DMA(())] * 2 + [_TOKEN],
        input_output_aliases={m: m for m in range(n)},
        compiler_params=_split_params(),
    )(*[_in_hbm(g) for g in grads])
    return list(outs[:n]), list(outs[n:2 * n]), (outs[2 * n], outs[2 * n + 1]), outs[2 * n + 2]


def _exchange_wait(name, mats, grads, landed, sems, after):
    n = len(mats)

    def body(*refs):
        g_refs, land_refs = refs[:n], refs[n:2 * n]
        send_sem, recv_sem = refs[2 * n], refs[2 * n + 1]
        x, y, c, _ = _place()
        for m in range(n):
            for k in range(N_CHIPS):
                got = land_refs[m].at[k]
                _copy_to(got, got, send_sem, recv_sem, (x, y, 1 - c)).wait_recv()
        for m in range(n):
            for k in range(N_CHIPS):
                sent = mats[m].piece(g_refs[m], k, 1 - c)
                _copy_to(sent, sent, send_sem, recv_sem, (x, y, 1 - c)).wait_send()

    outs = pl.pallas_call(
        body, name=name,
        in_specs=[_HBM] * (2 * n) + [_SEM, _SEM, _ANY], out_specs=[_HBM] * (2 * n),
        out_shape=[pltpu.HBM(a.shape, a.dtype) for a in list(grads) + list(landed)],
        input_output_aliases={i: i for i in range(2 * n)},
        compiler_params=_split_params(),
    )(*[_in_hbm(a) for a in list(grads) + list(landed)], sems[0], sems[1], after)
    return list(outs[:n]), list(outs[n:])


def _add_halves(name, mat, grad, landed, core):
    tr = _pick(mat.pr, (1024, 704, 512, 352, 256, 128, 64, 32, 16))
    per = mat.pr // tr

    def body(core_ref, g_ref, l_ref, o_ref):
        o_ref[...] = (g_ref[...].astype(F32) + l_ref[...].astype(F32)).astype(BF16)

    if mat.kind == "col":
        g_spec = pl.BlockSpec((tr, mat.pc), lambda k, r, core_ref: (core_ref[0] * per + r, k))
    else:
        g_spec = pl.BlockSpec((tr, mat.pc), lambda k, r, core_ref: ((2 * k + core_ref[0]) * per + r, 0))
    p_spec = pl.BlockSpec((None, tr, mat.pc), lambda k, r, core_ref: (k, r, 0))
    return pl.pallas_call(
        body, name=name,
        grid_spec=pltpu.PrefetchScalarGridSpec(num_scalar_prefetch=1, grid=(N_CHIPS, per),
                                               in_specs=[g_spec, p_spec], out_specs=p_spec),
        out_shape=jax.ShapeDtypeStruct((N_CHIPS, mat.pr, mat.pc), BF16),
        compiler_params=_cparams("parallel", "parallel"),
    )(core, grad, landed)


def _scatter_start(name, mats, partials):
    n = len(mats)

    def body(*refs):
        p_refs = refs[:n]
        outs = refs[n:]
        land_refs = outs[n:2 * n]
        send_sem, recv_sem, token = outs[2 * n], outs[2 * n + 1], outs[2 * n + 2]
        x, y, c, others = _place()
        me_k = 2 * x + y
        for m in range(n):
            for ox, oy in others:
                _copy_to(p_refs[m].at[2 * ox + oy], land_refs[m].at[me_k], send_sem, recv_sem, (ox, oy, c)).start()
        token[...] = jnp.zeros_like(token)

    piece_shapes = [pltpu.HBM((N_CHIPS, mt.pr, mt.pc), BF16) for mt in mats]
    outs = pl.pallas_call(
        body, name=name,
        in_specs=[_HBM] * n,
        out_specs=[_HBM] * (2 * n) + [_SEM, _SEM, _TOKEN_SPEC],
        out_shape=piece_shapes + piece_shapes + [pltpu.SemaphoreType.DMA(())] * 2 + [_TOKEN],
        input_output_aliases={m: m for m in range(n)},
        compiler_params=_split_params(),
    )(*[_in_hbm(p) for p in partials])
    return list(outs[:n]), list(outs[n:2 * n]), (outs[2 * n], outs[2 * n + 1]), outs[2 * n + 2]


def _scatter_wait(name, mats, partials, landed, sems, after):
    n = len(mats)

    def body(*refs):
        p_refs, land_refs = refs[:n], refs[n:2 * n]
        send_sem, recv_sem = refs[2 * n], refs[2 * n + 1]
        x, y, c, others = _place()
        for m in range(n):
            for ox, oy in others:
                got = land_refs[m].at[2 * ox + oy]
                _copy_to(got, got, send_sem, recv_sem, (ox, oy, c)).wait_recv()
        for m in range(n):
            for ox, oy in others:
                sent = p_refs[m].at[2 * ox + oy]
                _copy_to(sent, sent, send_sem, recv_sem, (ox, oy, c)).wait_send()

    outs = pl.pallas_call(
        body, name=name,
        in_specs=[_HBM] * (2 * n) + [_SEM, _SEM, _ANY], out_specs=[_HBM] * (2 * n),
        out_shape=[pltpu.HBM(a.shape, a.dtype) for a in list(partials) + list(landed)],
        input_output_aliases={i: i for i in range(2 * n)},
        compiler_params=_split_params(),
    )(*[_in_hbm(a) for a in list(partials) + list(landed)], sems[0], sems[1], after)
    return list(outs[:n]), list(outs[n:])


def _sum_chips(name, mat, partial, landed, slots, layer=None, stack=None, n_layers=1):
    tr = _pick(mat.pr, (1024, 704, 512, 352, 256, 128, 64, 32, 16))
    per = mat.pr // tr

    def body(slots_ref, own_ref, a_ref, b_ref, c_ref, *rest):
        o_ref = rest[-1]
        o_ref[...] = ((own_ref[...].astype(F32) + a_ref[...].astype(F32)) + b_ref[...].astype(F32)) + c_ref[...].astype(F32)

    def slot_spec(which):
        return pl.BlockSpec((None, tr, mat.pc), lambda r, slots_ref: (slots_ref[which], r, 0))

    in_specs = [slot_spec(0), slot_spec(1), slot_spec(2), slot_spec(3)]
    operands = [slots, partial, landed, landed, landed]
    aliases = {}
    if layer is None:
        o_spec = pl.BlockSpec((tr, mat.pc), lambda r, slots_ref: (slots_ref[4] * per + r, 0))
        out_shape = jax.ShapeDtypeStruct((mat.sr, mat.sc), F32)
    else:
        o_spec = pl.BlockSpec((None, tr, mat.pc), lambda r, slots_ref: (layer, slots_ref[4] * per + r, 0))
        out_shape = jax.ShapeDtypeStruct((n_layers, mat.sr, mat.sc), F32)
        if stack is not None:
            in_specs.append(_ANY)
            operands.append(stack)
            aliases = {len(operands) - 1: 0}
    return pl.pallas_call(
        body, name=name,
        grid_spec=pltpu.PrefetchScalarGridSpec(num_scalar_prefetch=1, grid=(per,), in_specs=in_specs, out_specs=o_spec),
        out_shape=out_shape, input_output_aliases=aliases,
        compiler_params=_cparams("parallel"),
    )(*operands)


def _share_pieces(name, mats, shards, groups):
    n = len(mats)
    n_out = len(groups)

    def body(*refs):
        out_refs = refs[n_out:2 * n_out]
        send_sems, recv_sems = refs[2 * n_out:]
        x, y, c, _ = _place()
        sibling = (x, y, 1 - c)
        sent, waits = [], []
        for o, members in enumerate(groups):
            for l, m in enumerate(members):
                dst = out_refs[o].at[l] if len(members) > 1 else out_refs[o]
                mine = mats[m].half(dst, c)
                sent.append(pltpu.make_async_remote_copy(src_ref=mine, dst_ref=mine, send_sem=send_sems.at[m],
                                                         recv_sem=recv_sems.at[m], device_id=sibling, device_id_type=MESH_ID))
                theirs = mats[m].half(dst, 1 - c)
                waits.append(pltpu.make_async_remote_copy(src_ref=theirs, dst_ref=theirs, send_sem=send_sems.at[m],
                                                          recv_sem=recv_sems.at[m], device_id=sibling, device_id_type=MESH_ID))
        for cp in sent:
            cp.start()
        for cp in waits:
            cp.wait_recv()
        for cp in sent:
            cp.wait_send()

    return pl.pallas_call(
        body, name=name,
        in_specs=[_ANY] * n_out, out_specs=[_ANY] * n_out,
        out_shape=[jax.ShapeDtypeStruct(s.shape, F32) for s in shards],
        input_output_aliases={o: o for o in range(n_out)},
        scratch_shapes=[pltpu.SemaphoreType.DMA((n,)), pltpu.SemaphoreType.DMA((n,))],
    )(*shards)


def _share_start(name, mats, shards, items):
    n_out = len(shards)

    def body(*refs):
        out_refs = refs[n_out:2 * n_out]
        send_sem, recv_sem, token = refs[2 * n_out], refs[2 * n_out + 1], refs[2 * n_out + 2]
        x, y, c, _ = _place()
        for o, members in enumerate(items):
            for layer, m in members:
                dst = out_refs[o] if layer is None else out_refs[o].at[layer]
                mine = mats[m].half(dst, c)
                _copy_to(mine, mine, send_sem, recv_sem, (x, y, 1 - c)).start()
        token[...] = jnp.zeros_like(token)

    outs = pl.pallas_call(
        body, name=name,
        in_specs=[_HBM] * n_out, out_specs=[_HBM] * n_out + [_SEM, _SEM, _TOKEN_SPEC],
        out_shape=[pltpu.HBM(s.shape, F32) for s in shards] + [pltpu.SemaphoreType.DMA(())] * 2 + [_TOKEN],
        input_output_aliases={o: o for o in range(n_out)},
        compiler_params=_split_params(),
    )(*[_in_hbm(s) for s in shards])
    return list(outs[:n_out]), (outs[n_out], outs[n_out + 1]), outs[n_out + 2]


def _share_wait(name, mats, shards, items, sems, after):
    n_out = len(shards)

    def body(*refs):
        bufs = refs[:n_out]
        send_sem, recv_sem = refs[n_out], refs[n_out + 1]
        x, y, c, _ = _place()
        for o, members in enumerate(items):
            for layer, m in members:
                dst = bufs[o] if layer is None else bufs[o].at[layer]
                theirs = mats[m].half(dst, 1 - c)
                _copy_to(theirs, theirs, send_sem, recv_sem, (x, y, 1 - c)).wait_recv()
        for o, members in enumerate(items):
            for layer, m in members:
                dst = bufs[o] if layer is None else bufs[o].at[layer]
                mine = mats[m].half(dst, c)
                _copy_to(mine, mine, send_sem, recv_sem, (x, y, 1 - c)).wait_send()

    outs = pl.pallas_call(
        body, name=name,
        in_specs=[_HBM] * n_out + [_SEM, _SEM, _ANY], out_specs=[_HBM] * n_out,
        out_shape=[pltpu.HBM(s.shape, F32) for s in shards],
        input_output_aliases={o: o for o in range(n_out)},
        compiler_params=_split_params(),
    )(*[_in_hbm(s) for s in shards], sems[0], sems[1], after)
    return list(outs)


def _sum_devices(stacked):
    nd, r, c = stacked.shape

    def body(s_ref, o_ref):
        s = s_ref[0]
        for k in range(1, nd):
            s = s + s_ref[k]
        o_ref[...] = s

    return pl.pallas_call(
        body, name="sum_small_grads", grid=(1,),
        in_specs=[pl.BlockSpec((nd, r, c), lambda i: (0, 0, 0))],
        out_specs=pl.BlockSpec((r, c), lambda i: (0, 0)),
        out_shape=jax.ShapeDtypeStruct((r, c), F32),
        compiler_params=_cparams("arbitrary"),
    )(stacked)


def _pack(arrs):
    flat = jnp.concatenate([a.reshape(-1) for a in arrs])
    rows = -(-flat.shape[0] // (8 * LANES)) * 8
    return jnp.pad(flat, (0, rows * LANES - flat.shape[0])).reshape(rows, LANES)


def _unpack(packed, shapes):
    flat = packed.reshape(-1)
    out, at = [], 0
    for s in shapes:
        size = 1
        for dim in s:
            size *= dim
        out.append(flat[at:at + size].reshape(s))
        at += size
    return out


def kernel(x, mix_norm_e, w_in_e, conv_w_e, conv_b_e, ln_g_e, ln_b_e, w_pool_e, pool_scale_e, w_out_e, mix_norm_o, w_in_o, conv_w_o, w_out_o, ffn_norm, w_gate, w_up, w_down, final_norm, loss_target, m_mix_norm_e, m_w_in_e, m_conv_w_e, m_conv_b_e, m_ln_g_e, m_ln_b_e, m_w_pool_e, m_pool_scale_e, m_w_out_e, m_mix_norm_o, m_w_in_o, m_conv_w_o, m_w_out_o, m_ffn_norm, m_w_gate, m_w_up, m_w_down, m_final_norm, v_mix_norm_e, v_w_in_e, v_conv_w_e, v_conv_b_e, v_ln_g_e, v_ln_b_e, v_w_pool_e, v_pool_scale_e, v_w_out_e, v_mix_norm_o, v_w_in_o, v_conv_w_o, v_w_out_o, v_ffn_norm, v_w_gate, v_w_up, v_w_down, v_final_norm):
    bsz, seq_len, d = x.shape
    t = bsz * seq_len
    depth = ffn_norm.shape[0]
    assert depth == 2 and conv_b_e.shape[1] == pool_scale_e.shape[1]
    ts = _pick(seq_len, (256, 128, 64, 32))
    me_k = 2 * lax.axis_index("x") + lax.axis_index("y")
    core = lax.axis_index("c").astype(jnp.int32).reshape(1)

    mat_src = [("col", w_in_e, 0), ("row", w_out_e, 0), ("col", w_gate, 0), ("col", w_up, 0), ("row", w_down, 0),
               ("col", w_in_o, 0), ("row", w_out_o, 0), ("col", w_gate, 1), ("col", w_up, 1), ("row", w_down, 1)]
    mats = [_Mat(kind, w.shape[1:]) for kind, w, _ in mat_src]
    n_pool = w_pool_e.shape[1]
    pool_mats = tuple(range(len(mats), len(mats) + n_pool))
    mats = mats + [_Mat("row", w_pool_e.shape[2:])] * n_pool
    chip = me_k.astype(jnp.int32).reshape(1)
    small_shards = [conv_w_e[0], w_pool_e[0], mix_norm_o, conv_w_o[0]]
    packed_small = _pack(small_shards)

    chain = [()]

    def seq(fn, *args, **kw):
        out = fn(*args, after=chain[0], **kw)
        chain[0] = (out[0] if isinstance(out, (list, tuple)) else out,)
        return out

    def mm(*args, **kw):
        return seq(_mm, *args, **kw)

    gather_groups = [(0,), (1,), (2, 3), (4,), (5, 6), (7, 8), (9,)]
    fulls, gather_sems, small_all = [None] * len(mats), [], None
    for g, ms in enumerate(gather_groups):
        own16 = [seq(_cast_into_full, "cast_w%d" % m, mats[m], mat_src[m][1], mat_src[m][2], chip) for m in ms]
        sent, landing, sems, token = _gather_start("gather_start%d" % g, [mats[m] for m in ms], own16,
                                                   _place_small(packed_small, chip) if g == 0 else None)
        chain[0] = (token,)
        for m, f in zip(ms, sent):
            fulls[m] = f
        gather_sems.append(sems)
        if g == 0:
            small_all = landing

    passed = {}

    def gather_pass(g):
        ms = gather_groups[g]
        bufs, pass_sems, token = _gather_pass("gather_pass%d" % g, [mats[m] for m in ms], [fulls[m] for m in ms],
                                              small_all if g == 0 else None, gather_sems[g], chain[0][0])
        chain[0] = (token,)
        passed[g] = (bufs, pass_sems)

    def gather_done(g):
        ms = gather_groups[g]
        bufs, pass_sems = passed[g]
        done = _gather_done("gather_done%d" % g, [mats[m] for m in ms], bufs[:len(ms)], pass_sems, chain[0][0])
        chain[0] = (done[0],)
        return done + bufs[len(ms):]

    h0 = x.reshape(t, d)
    target = loss_target.reshape(t, d)
    gather_pass(0)
    n1 = seq(_rms_fwd, "mix0_norm", h0, mix_norm_e)
    W_in_e, small_all = gather_done(0)
    per_chip = [_unpack(small_all[k], [s.shape for s in small_shards]) for k in range(N_CHIPS)]
    conv_w_e_f = jnp.concatenate([p[0] for p in per_chip], axis=1)
    w_pool_f = jnp.concatenate([p[1] for p in per_chip], axis=1)
    mix_norm_o_f = jnp.concatenate([p[2] for p in per_chip], axis=1)
    conv_w_o_f = jnp.concatenate([p[3] for p in per_chip], axis=1)
    W_gate, W_up, W_down = [None, None], [None, None], [None, None]

    (u_e,) = mm("mix0_in", [(n1, W_in_e)], "nn", [F32], _ep_store, tm=1024, tn=1024, tk=2048)
    gather_pass(1)
    a2, cat = seq(_mixer_e_fwd, u_e, conv_w_e_f, conv_b_e, ln_g_e, ln_b_e, w_pool_f, pool_scale_e, seq_len, ts)
    (W_out_e,) = gather_done(1)
    (h1,) = mm("mix0_out", [(cat, W_out_e)], "nn", [F32], _ep_residual, extras=(h0,), tm=1024, tn=1024, tk=2048)
    gather_pass(2)
    n2 = seq(_rms_fwd, "ffn0_norm", h1, ffn_norm[0:1])
    W_gate[0], W_up[0] = gather_done(2)
    gt0, up0, act0 = mm("ffn0_gate_up", [(n2, W_gate[0]), (n2, W_up[0])], "nn", [BF16] * 3, _ep_swiglu,
                        acc_of=(0, 1), tm=1024, tn=512, tk=2048)
    gather_pass(3)
    (W_down[0],) = gather_done(3)
    gather_pass(4)
    h2, n3 = seq(_mm_rows, "ffn0_down", [(act0, W_down[0])], "nn", [F32, BF16], _ep_rows_residual_norm, extras=(h1,),
                 vecs=(mix_norm_o_f,), tm=256, tk=act0.shape[1], chunk_dots=False)
    W_in_o, W_out_o = gather_done(4)
    (u_o,) = mm("mix1_in", [(n3, W_in_o)], "nn", [BF16], _ep_store, tm=1024, tn=1024, tk=2048)
    gather_pass(5)
    y_o = seq(_mixer_o_fwd, u_o, conv_w_o_f, seq_len, ts)
    h3, n4 = seq(_mm_rows, "mix1_out", [(y_o, W_out_o)], "nn", [F32, BF16], _ep_rows_residual_norm, extras=(h2,),
                 vecs=(ffn_norm[1:2],), tm=512, tk=2048)
    gather_pass(6)
    W_gate[1], W_up[1] = gather_done(5)
    gt1, up1, act1 = mm("ffn1_gate_up", [(n4, W_gate[1]), (n4, W_up[1])], "nn", [BF16] * 3, _ep_swiglu,
                        acc_of=(0, 1), tm=1024, tn=512, tk=2048)
    (W_down[1],) = gather_done(6)
    dh4, dh4b, loss_cols, d_final_norm = seq(_mm_rows, "ffn1_down", [(act1, W_down[1])], "nn", [F32, BF16],
                                             _ep_rows_loss_head, extras=(h3, target), vecs=(final_norm.reshape(1, d),),
                                             n_sums=2, tm=256, tk=act1.shape[1], chunk_dots=False)
    loss = lax.psum(jnp.sum(loss_cols), AXES)

    in_flight = {}
    partials, scattered = [None] * len(mats), [None] * len(mats)

    def reduce_begin(tag, ms, grads):
        gm = [mats[m] for m in ms]
        grads, landed, sems, token = _exchange_start("exchange_start_" + tag, gm, grads)
        chain[0] = (token,)
        in_flight[tag] = (ms, gm, grads, landed, sems)

    def reduce_advance(tag):
        ms, gm, grads, landed, sems = in_flight[tag]
        grads, landed = _exchange_wait("exchange_wait_" + tag, gm, grads, landed, sems, chain[0][0])
        parts = [_add_halves("add_halves%d" % m, mats[m], g, l, core) for m, g, l in zip(ms, grads, landed)]
        parts, lands, sems, token = _scatter_start("scatter_start_" + tag, gm, parts)
        chain[0] = (token,)
        in_flight[tag] = (ms, gm, parts, lands, sems)

    def reduce_finish(tag):
        ms, gm, parts, lands, sems = in_flight[tag]
        parts, lands = _scatter_wait("scatter_wait_" + tag, gm, parts, lands, sems, chain[0][0])
        chain[0] = (lands[0],)
        for m, p, l in zip(ms, parts, lands):
            partials[m], scattered[m] = p, l

    xi, yi, ci = lax.axis_index("x"), lax.axis_index("y"), lax.axis_index("c")
    slots = jnp.stack([me_k, 2 * (1 - xi) + yi, 2 * xi + (1 - yi), 2 * (1 - xi) + (1 - yi), ci]).astype(jnp.int32)
    shards = {}

    def sum_into(name, m, layer=None, n_layers=1):
        if layer is None:
            shards[name] = _sum_chips("sum_chips%d" % m, mats[m], partials[m], scattered[m], slots)
        else:
            shards[name] = _sum_chips("sum_chips%d" % m, mats[m], partials[m], scattered[m], slots, layer=layer,
                                      stack=shards.get(name), n_layers=n_layers)

    sharing = {}

    def share_begin(tag, names, items):
        arrays, sems, token = _share_start("share_start_" + tag, mats, [shards[nm] for nm in names], items)
        chain[0] = (token,)
        sharing[tag] = (names, items, arrays, sems)

    def share_end(tag):
        names, items, arrays, sems = sharing[tag]
        arrays = _share_wait("share_wait_" + tag, mats, arrays, items, sems, chain[0][0])
        chain[0] = (arrays[0],)
        for nm, a in zip(names, arrays):
            shards[nm] = a

    def ffn_bwd(l, dhb, n, gt, up, act, mid=None):
        dgt, dup = mm("ffn%d_dact" % l, [(dhb, W_down[l])], "nt", [BF16, BF16], _ep_swiglu_bwd, extras=(gt, up),
                      tm=1024, tn=512, tk=2048, row_chunk=512)
        (dW_down,) = mm("ffn%d_dw_down" % l, [(act, dhb)], "tn", [BF16], _ep_store, tm=512, tn=1024, tk=4096)
        if mid is not None:
            mid()
        (dn,) = mm("ffn%d_dn" % l, [(dgt, W_gate[l]), (dup, W_up[l])], "nt", [BF16], _ep_store,
                   tm=512, tn=2048, tk=1408)
        (dW_gate,) = mm("ffn%d_dw_gate" % l, [(n, dgt)], "tn", [BF16], _ep_store, tm=1024, tn=512, tk=4096)
        (dW_up,) = mm("ffn%d_dw_up" % l, [(n, dup)], "tn", [BF16], _ep_store, tm=1024, tn=512, tk=4096)
        return dn, dW_gate, dW_up, dW_down

    dn4, dW_gate1, dW_up1, dW_down1 = ffn_bwd(1, dh4b, n4, gt1, up1, act1)
    reduce_begin("ffn1", (7, 8, 9), [dW_gate1, dW_up1, dW_down1])
    dh3, dh3b, d_ffn_norm1 = seq(_rms_bwd, "ffn1_norm_bwd", dn4, h3, ffn_norm[1:2], dh4)
    (dy_o,) = mm("mix1_dy", [(dh3b, W_out_o)], "nt", [BF16], _ep_store, tm=1024, tn=1024, tk=2048)
    reduce_advance("ffn1")
    (dW_out_o,) = mm("mix1_dw_out", [(y_o, dh3b)], "tn", [BF16], _ep_store, tm=1024, tn=1024, tk=4096)
    du_o, d_conv_w_o = seq(_mixer_o_bwd, dy_o, u_o, conv_w_o_f, seq_len, ts)
    (dW_in_o,) = mm("mix1_dw_in", [(n3, du_o)], "tn", [BF16], _ep_store, tm=1024, tn=1024, tk=4096)
    reduce_begin("mix1", (5, 6), [dW_in_o, dW_out_o])
    dh2, dh2b, d_mix_norm_o = seq(_mm_rows, "mix1_dn", [(du_o, W_in_o)], "nt", [F32, BF16], _ep_rows_norm_bwd,
                                  extras=(h2, dh3), vecs=(mix_norm_o_f,), n_sums=1, tm=256, tk=6144, chunk_dots=False)
    reduce_advance("mix1")

    def finish_layer1():
        reduce_finish("ffn1")
        reduce_finish("mix1")
        sum_into("w_in_o", 5)
        sum_into("w_out_o", 6)
        for nm, m in (("w_gate", 7), ("w_up", 8), ("w_down", 9)):
            sum_into(nm, m, layer=1, n_layers=2)
        share_begin("layer1", ["w_in_o", "w_out_o", "w_gate", "w_up", "w_down"],
                    [[(None, 5)], [(None, 6)], [(1, 7)], [(1, 8)], [(1, 9)]])

    dn2, dW_gate0, dW_up0, dW_down0 = ffn_bwd(0, dh2b, n2, gt0, up0, act0, mid=finish_layer1)
    reduce_begin("ffn0", (2, 3, 4), [dW_gate0, dW_up0, dW_down0])
    dh1, dh1b, d_ffn_norm0 = seq(_rms_bwd, "ffn0_norm_bwd", dn2, h1, ffn_norm[0:1], dh2)
    (dcat,) = mm("mix0_dcat", [(dh1b, W_out_e)], "nt", [F32], _ep_store, tm=1024, tn=1024, tk=2048)
    reduce_advance("ffn0")
    (dW_out_e,) = mm("mix0_dw_out", [(cat, dh1b)], "tn", [BF16], _ep_store, tm=1024, tn=1024, tk=4096)
    da2, d_ln_g, d_ln_b, d_conv_b = seq(_mixer_e_bwd_norm, dcat, a2, ln_g_e, ln_b_e, ts)
    du_e, d_conv_w_e, d_w_pool, d_pool_scale = seq(_mixer_e_bwd_mix, da2, dcat, u_e, conv_w_e_f, w_pool_f, pool_scale_e,
                                                   seq_len, ts)
    (dW_in_e,) = mm("mix0_dw_in", [(n1, du_e)], "tn", [BF16], _ep_store, tm=1024, tn=1024, tk=4096)
    reduce_begin("mix0", (0, 1) + pool_mats, [dW_in_e, dW_out_e] + [d_w_pool[g].astype(BF16) for g in range(n_pool)])
    dx, d_mix_norm_e = seq(_mm_rows, "mix0_dn", [(du_e, W_in_e)], "nt", [F32], _ep_rows_norm_bwd,
                           extras=(h0, dh1), vecs=(mix_norm_e,), n_sums=1, tm=256, tk=3072, chunk_dots=False)
    reduce_advance("mix0")

    d_ffn_norm = jnp.concatenate([d_ffn_norm0, d_ffn_norm1], axis=0)
    small_partials = [d_mix_norm_e, d_conv_w_e, d_conv_b, d_ln_g, d_ln_b, d_pool_scale, d_mix_norm_o, d_conv_w_o,
                      d_ffn_norm, d_final_norm]
    packed_grads, small_stack, small_sems, token = _small_start(_pack(small_partials), chain[0][0])
    chain[0] = (token,)
    share_end("layer1")
    reduce_finish("ffn0")
    for nm, m in (("w_gate", 2), ("w_up", 3), ("w_down", 4)):
        sum_into(nm, m, layer=0, n_layers=2)
    share_begin("layer0", ["w_gate", "w_up", "w_down"], [[(0, 2)], [(0, 3)], [(0, 4)]])
    grad = {"w_in_o": shards["w_in_o"][None], "w_out_o": shards["w_out_o"][None]}
    weights = dict(mix_norm_e=mix_norm_e, w_in_e=w_in_e, conv_w_e=conv_w_e, conv_b_e=conv_b_e, ln_g_e=ln_g_e, ln_b_e=ln_b_e,
                   w_pool_e=w_pool_e, pool_scale_e=pool_scale_e, w_out_e=w_out_e, mix_norm_o=mix_norm_o, w_in_o=w_in_o,
                   conv_w_o=conv_w_o, w_out_o=w_out_o, ffn_norm=ffn_norm, w_gate=w_gate, w_up=w_up, w_down=w_down,
                   final_norm=final_norm)
    mom1 = dict(mix_norm_e=m_mix_norm_e, w_in_e=m_w_in_e, conv_w_e=m_conv_w_e, conv_b_e=m_conv_b_e, ln_g_e=m_ln_g_e,
                ln_b_e=m_ln_b_e, w_pool_e=m_w_pool_e, pool_scale_e=m_pool_scale_e, w_out_e=m_w_out_e, mix_norm_o=m_mix_norm_o,
                w_in_o=m_w_in_o, conv_w_o=m_conv_w_o, w_out_o=m_w_out_o, ffn_norm=m_ffn_norm, w_gate=m_w_gate, w_up=m_w_up,
                w_down=m_w_down, final_norm=m_final_norm)
    mom2 = dict(mix_norm_e=v_mix_norm_e, w_in_e=v_w_in_e, conv_w_e=v_conv_w_e, conv_b_e=v_conv_b_e, ln_g_e=v_ln_g_e,
                ln_b_e=v_ln_b_e, w_pool_e=v_w_pool_e, pool_scale_e=v_pool_scale_e, w_out_e=v_w_out_e, mix_norm_o=v_mix_norm_o,
                w_in_o=v_w_in_o, conv_w_o=v_conv_w_o, w_out_o=v_w_out_o, ffn_norm=v_ffn_norm, w_gate=v_w_gate, w_up=v_w_up,
                w_down=v_w_down, final_norm=v_final_norm)
    names = list(weights)

    big = ("w_in_o", "w_out_o", "w_gate", "w_up", "w_down", "w_in_e", "w_out_e")
    delta, new_m, new_v = {}, {}, {}

    def update(nm):
        shape = weights[nm].shape
        rows = 1
        for dim in shape[:-1]:
            rows *= dim
        as2d = lambda a: a.reshape(rows, shape[-1])
        dl, mn, vn = seq(_adamw, "adamw_" + nm, as2d(weights[nm]), as2d(grad[nm]), as2d(mom1[nm]), as2d(mom2[nm]))
        delta[nm], new_m[nm], new_v[nm] = dl.reshape(shape), mn.reshape(shape), vn.reshape(shape)

    for nm in big[:2]:
        update(nm)
    share_end("layer0")
    for nm in big[2:5]:
        grad[nm] = shards[nm]
        update(nm)
    reduce_finish("mix0")
    sum_into("w_in_e", 0)
    sum_into("w_out_e", 1)
    for layer, m in enumerate(pool_mats):
        sum_into("w_pool_e", m, layer=layer, n_layers=n_pool)
    g_w_in_e, g_w_out_e, g_w_pool = _share_pieces("share_pieces_first", mats,
                                                  [shards["w_in_e"], shards["w_out_e"], shards["w_pool_e"]],
                                                  [(0,), (1,), pool_mats])
    grad["w_in_e"], grad["w_out_e"], grad["w_pool_e"] = g_w_in_e[None], g_w_out_e[None], g_w_pool[None]
    for nm in big[5:]:
        update(nm)

    packed_grads, small_stack = _small_wait(packed_grads, small_stack, small_sems, chain[0][0])
    me_dev = 4 * xi + 2 * yi + ci
    small_stack = jnp.where(lax.broadcasted_iota(jnp.int32, (N_DEV, 1, 1), 0) == me_dev, packed_grads[None], small_stack)
    small_sum = _unpack(_sum_devices(small_stack), [s.shape for s in small_partials])
    (g_mix_norm_e, g_conv_w_e_f, g_conv_b, g_ln_g, g_ln_b, g_pool_scale, g_mix_norm_o_f, g_conv_w_o_f,
     g_ffn_norm, g_final_norm) = small_sum

    def my_shard(full, axis):
        size = full.shape[axis] // N_CHIPS
        return lax.dynamic_slice_in_dim(full, me_k * size, size, axis)

    grad.update({
        "mix_norm_e": g_mix_norm_e, "conv_w_e": my_shard(g_conv_w_e_f, 1)[None], "conv_b_e": g_conv_b,
        "ln_g_e": g_ln_g, "ln_b_e": g_ln_b, "pool_scale_e": g_pool_scale,
        "mix_norm_o": my_shard(g_mix_norm_o_f, 1), "conv_w_o": my_shard(g_conv_w_o_f, 1)[None],
        "ffn_norm": g_ffn_norm, "final_norm": g_final_norm.reshape(final_norm.shape),
    })
    small = [nm for nm in names if nm not in big]
    shapes = [weights[nm].shape for nm in small]
    dl, mn, vn = _adamw("adamw_small", _pack([weights[nm] for nm in small]), _pack([grad[nm] for nm in small]),
                        _pack([mom1[nm] for nm in small]), _pack([mom2[nm] for nm in small]))
    for nm, a, b, c_ in zip(small, _unpack(dl, shapes), _unpack(mn, shapes), _unpack(vn, shapes)):
        delta[nm], new_m[nm], new_v[nm] = a, b, c_

    grad_x = dx.reshape(bsz, seq_len, d)
    return (loss, grad_x, *[grad[nm] for nm in names], *[delta[nm] for nm in names],
            *[new_m[nm] for nm in names], *[new_v[nm] for nm in names])
```

```python
import jax
import jax.numpy as jnp
from jax import lax
from jax.experimental import pallas as pl
from jax.experimental.pallas import tpu as pltpu

F32 = jnp.float32
BF16 = jnp.bfloat16
MESH_ID = pl.DeviceIdType.MESH
AXES = ("x", "y", "c")
N_CHIPS = 4
N_DEV = 8

EPS = 1e-6
POOL_WINDOWS = (2, 4, 8, 16)
ADAM_LR, ADAM_B1, ADAM_B2, ADAM_EPS, ADAM_WD, ADAM_STEP = 0.001, 0.9, 0.999, 1e-08, 0.01, 10

LANES = 128
CONV_HALO = 32
POOL_HALO = 16
SHORT_HALO = 16
V7X_VMEM_LIMIT = 56 * 1024 * 1024


def _cparams(*sem):
    return pltpu.CompilerParams(dimension_semantics=sem if sem else None, vmem_limit_bytes=V7X_VMEM_LIMIT)


def _pick(dim, prefs):
    for p in prefs:
        if p <= dim and dim % p == 0:
            return p
    return dim


def _sigmoid(x):
    return jax.nn.sigmoid(x)


_ANY = pl.BlockSpec(memory_space=pl.ANY)


def _behind(after, body, n_in):
    if not after:
        return body
    skip = len(after)

    def body_behind(*refs):
        return body(*refs[:n_in], *refs[n_in + skip:])

    return body_behind


_DOT_DIMS = {
    "nn": (((1,), (0,)), ((), ())),
    "nt": (((1,), (1,)), ((), ())),
    "tn": (((0,), (0,)), ((), ())),
}


def _mm(name, pairs, mode, out_dtypes, epilogue, extras=(), acc_of=None, tm=512, tn=512, tk=2048, row_chunk=0, after=()):
    a0, b0 = pairs[0]
    if mode == "nn":
        (m, k), n = a0.shape, b0.shape[1]
    elif mode == "nt":
        (m, k), n = a0.shape, b0.shape[0]
    else:
        (k, m), n = a0.shape, b0.shape[1]
    tm = _pick(m, (tm, 512, 256, 128, 64, 32, 16, 8))
    tn = _pick(n, (tn, 512, 256, 128))
    tk = _pick(k, (tk, 2048, 1024, 512, 256, 128))
    nk = k // tk
    n_pairs = len(pairs)
    acc_of = tuple(acc_of) if acc_of is not None else (0,) * n_pairs
    n_acc = max(acc_of) + 1
    n_ex, n_out = len(extras), len(out_dtypes)
    dims = _DOT_DIMS[mode]

    def body(*refs):
        a_refs = refs[:n_pairs]
        b_refs = refs[n_pairs:2 * n_pairs]
        e_refs = refs[2 * n_pairs:2 * n_pairs + n_ex]
        first_out = 2 * n_pairs + n_ex + len(after)
        o_refs = refs[first_out:first_out + n_out]
        acc_refs = refs[first_out + n_out:]

        def partial_sums(rows=None):
            sums = [None] * n_acc
            for p in range(n_pairs):
                a = a_refs[p][...] if rows is None else (a_refs[p][:, rows] if mode == "tn" else a_refs[p][rows, :])
                d = lax.dot_general(a, b_refs[p][...], dims, preferred_element_type=F32)
                sums[acc_of[p]] = d if sums[acc_of[p]] is None else sums[acc_of[p]] + d
            return sums

        if nk == 1 and row_chunk:
            for r0 in range(0, tm, row_chunk):
                rows = pl.ds(r0, row_chunk)
                epilogue(partial_sums(rows), [e.at[rows, :] for e in e_refs], [o.at[rows, :] for o in o_refs])
            return
        if nk == 1:
            epilogue(partial_sums(), e_refs, o_refs)
            return
        kk = pl.program_id(2)

        @pl.when(kk == 0)
        def _():
            for acc in acc_refs:
                acc[...] = jnp.zeros_like(acc)

        for acc, s in zip(acc_refs, partial_sums()):
            acc[...] += s

        @pl.when(kk == nk - 1)
        def _():
            epilogue([acc[...] for acc in acc_refs], e_refs, o_refs)

    if mode == "nn":
        a_spec = pl.BlockSpec((tm, tk), lambda i, j, kk: (i, kk))
        b_spec = pl.BlockSpec((tk, tn), lambda i, j, kk: (kk, j))
    elif mode == "nt":
        a_spec = pl.BlockSpec((tm, tk), lambda i, j, kk: (i, kk))
        b_spec = pl.BlockSpec((tn, tk), lambda i, j, kk: (j, kk))
    else:
        a_spec = pl.BlockSpec((tk, tm), lambda i, j, kk: (kk, i))
        b_spec = pl.BlockSpec((tk, tn), lambda i, j, kk: (kk, j))
    o_spec = pl.BlockSpec((tm, tn), lambda i, j, kk: (i, j))
    outs = pl.pallas_call(
        body,
        name=name,
        grid=(m // tm, n // tn, nk),
        in_specs=[a_spec] * n_pairs + [b_spec] * n_pairs + [o_spec] * n_ex
        + [pl.BlockSpec(memory_space=pl.ANY)] * len(after),
        out_specs=[o_spec] * n_out,
        out_shape=[jax.ShapeDtypeStruct((m, n), dt) for dt in out_dtypes],
        scratch_shapes=[pltpu.VMEM((tm, tn), F32) for _ in range(n_acc)] if nk > 1 else [],
        compiler_params=_cparams("parallel", "parallel", "arbitrary"),
    )(*[p[0] for p in pairs], *[p[1] for p in pairs], *extras, *after)
    return outs


def _mm_rows(name, pairs, mode, out_dtypes, epilogue, extras=(), vecs=(), n_sums=0, tm=512, tk=2048, row_chunk=128,
             chunk_dots=True, after=()):
    a0, b0 = pairs[0]
    (m, k), n = a0.shape, (b0.shape[1] if mode == "nn" else b0.shape[0])
    tm = _pick(m, (tm, 512, 256, 128, 64, 32, 16, 8))
    tk = _pick(k, (tk, 2048, 1024, 512, 256, 128))
    row_chunk = min(row_chunk, tm)
    nk = k // tk
    n_pairs, n_ex, n_vec, n_out = len(pairs), len(extras), len(vecs), len(out_dtypes)
    dims = _DOT_DIMS[mode]

    def body(*refs):
        a_refs = refs[:n_pairs]
        b_refs = refs[n_pairs:2 * n_pairs]
        e_refs = refs[2 * n_pairs:2 * n_pairs + n_ex]
        v_refs = refs[2 * n_pairs + n_ex:2 * n_pairs + n_ex + n_vec]
        first_out = 2 * n_pairs + n_ex + n_vec + len(after)
        o_refs = refs[first_out:first_out + n_out]
        s_refs = refs[first_out + n_out:first_out + n_out + n_sums]
        acc_refs = refs[first_out + n_out + n_sums:]
        i, kk = pl.program_id(0), pl.program_id(1)

        @pl.when((i == 0) & (kk == 0))
        def _():
            for s in s_refs:
                s[...] = jnp.zeros_like(s)

        def dots(rows):
            total = None
            for p in range(n_pairs):
                d = lax.dot_general(a_refs[p][rows, :], b_refs[p][...], dims, preferred_element_type=F32)
                total = d if total is None else total + d
            return total

        def finish(acc_of_rows):
            for r0 in range(0, tm, row_chunk):
                rows = pl.ds(r0, row_chunk)
                epilogue(acc_of_rows(rows), [e.at[rows, :] for e in e_refs], v_refs, [o.at[rows, :] for o in o_refs], s_refs)

        if nk == 1 and chunk_dots:
            finish(dots)
            return
        acc = acc_refs[0]
        if nk == 1:
            acc[...] = dots(slice(None))
            finish(lambda rows: acc[rows, :])
            return

        @pl.when(kk == 0)
        def _():
            acc[...] = jnp.zeros_like(acc)

        acc[...] += dots(slice(None))

        @pl.when(kk == nk - 1)
        def _():
            finish(lambda rows: acc[rows, :])

    a_spec = pl.BlockSpec((tm, tk), lambda i, kk: (i, kk))
    b_mode = dict(pipeline_mode=pl.Buffered(1)) if nk == 1 else {}
    b_spec = (pl.BlockSpec((tk, n), lambda i, kk: (kk, 0), **b_mode) if mode == "nn"
              else pl.BlockSpec((n, tk), lambda i, kk: (0, kk), **b_mode))
    row_spec = pl.BlockSpec((tm, n), lambda i, kk: (i, 0))
    vec_spec = pl.BlockSpec((1, n), lambda i, kk: (0, 0))
    return pl.pallas_call(
        body, name=name, grid=(m // tm, nk),
        in_specs=[a_spec] * n_pairs + [b_spec] * n_pairs + [row_spec] * n_ex + [vec_spec] * n_vec + [_ANY] * len(after),
        out_specs=[row_spec] * n_out + [vec_spec] * n_sums,
        out_shape=[jax.ShapeDtypeStruct((m, n), dt) for dt in out_dtypes] + [jax.ShapeDtypeStruct((1, n), F32)] * n_sums,
        scratch_shapes=[pltpu.VMEM((tm, n), F32)] if (nk > 1 or not chunk_dots) else [],
        compiler_params=_cparams("arbitrary", "arbitrary"),
    )(*[p[0] for p in pairs], *[p[1] for p in pairs], *extras, *vecs, *after)


def _ep_rows_residual_norm(acc, ex, vecs, outs, sums):
    h = ex[0][...] + acc
    outs[0][...] = h
    r = lax.rsqrt(jnp.mean(h * h, axis=-1, keepdims=True) + EPS)
    outs[1][...] = (h * r * vecs[0][...]).astype(BF16)


def _ep_rows_loss_head(acc, ex, vecs, outs, sums):
    x = ex[0][...] + acc
    gain = vecs[0][...]
    inv_d = 1.0 / x.shape[-1]
    r = lax.rsqrt(jnp.mean(x * x, axis=-1, keepdims=True) + EPS)
    xhat = x * r
    err = xhat * gain - ex[1][...]
    sums[0][...] += (0.5 * inv_d) * jnp.sum(err * err, axis=0, keepdims=True)
    dy = err * inv_d
    sums[1][...] += jnp.sum(dy * xhat, axis=0, keepdims=True)
    dxh = dy * gain
    dh = r * (dxh - xhat * jnp.mean(dxh * xhat, axis=-1, keepdims=True))
    outs[0][...] = dh
    outs[1][...] = dh.astype(BF16)


def _ep_rows_norm_bwd(dn, ex, vecs, outs, sums):
    x = ex[0][...]
    r = lax.rsqrt(jnp.mean(x * x, axis=-1, keepdims=True) + EPS)
    xhat = x * r
    sums[0][...] += jnp.sum(dn * xhat, axis=0, keepdims=True)
    dxh = dn * vecs[0][...]
    dh = ex[1][...] + r * (dxh - xhat * jnp.mean(dxh * xhat, axis=-1, keepdims=True))
    outs[0][...] = dh
    if len(outs) > 1:
        outs[1][...] = dh.astype(BF16)


def _ep_store(accs, ex, outs):
    outs[0][...] = accs[0].astype(outs[0].dtype)


def _ep_residual(accs, ex, outs):
    outs[0][...] = ex[0][...] + accs[0]


def _ep_swiglu(accs, ex, outs):
    g, u = accs
    s = _sigmoid(g)
    gs = g * s
    outs[0][...] = (u * (s * (1.0 + g * (1.0 - s)))).astype(BF16)
    outs[1][...] = gs.astype(BF16)
    outs[2][...] = (gs * u).astype(BF16)


def _ep_swiglu_bwd(accs, ex, outs):
    d = accs[0]
    outs[0][...] = (d * ex[0][...].astype(F32)).astype(BF16)
    outs[1][...] = (d * ex[1][...].astype(F32)).astype(BF16)


def _rms_fwd(name, h, g, after=()):
    t, d = h.shape
    tr = _pick(t, (256, 128, 64, 32, 16, 8))

    def body(h_ref, g_ref, *rest):
        o_ref = rest[-1]
        x = h_ref[...]
        r = lax.rsqrt(jnp.mean(x * x, axis=-1, keepdims=True) + EPS)
        o_ref[...] = (x * r * g_ref[...]).astype(BF16)

    return pl.pallas_call(
        body, name=name, grid=(t // tr,),
        in_specs=[pl.BlockSpec((tr, d), lambda i: (i, 0)), pl.BlockSpec((1, d), lambda i: (0, 0))]
        + [pl.BlockSpec(memory_space=pl.ANY)] * len(after),
        out_specs=pl.BlockSpec((tr, d), lambda i: (i, 0)),
        out_shape=jax.ShapeDtypeStruct((t, d), BF16),
        compiler_params=_cparams("parallel"),
    )(h, g, *after)


def _rms_bwd(name, dn, h, g, dres, after=()):
    t, d = h.shape
    tr = _pick(t, (256, 128, 64, 32, 16, 8))

    def body(dn_ref, h_ref, g_ref, dres_ref, dh_ref, dhb_ref, dg_ref):
        @pl.when(pl.program_id(0) == 0)
        def _():
            dg_ref[...] = jnp.zeros_like(dg_ref)

        chunk = min(64, tr)
        for r0 in range(0, tr, chunk):
            rows = pl.ds(r0, chunk)
            x = h_ref[rows, :]
            r = lax.rsqrt(jnp.mean(x * x, axis=-1, keepdims=True) + EPS)
            xhat = x * r
            dnv = dn_ref[rows, :].astype(F32)
            dg_ref[...] += jnp.sum(dnv * xhat, axis=0, keepdims=True)
            dxh = dnv * g_ref[...]
            dh = dres_ref[rows, :] + r * (dxh - xhat * jnp.mean(dxh * xhat, axis=-1, keepdims=True))
            dh_ref[rows, :] = dh
            dhb_ref[rows, :] = dh.astype(BF16)

    row = pl.BlockSpec((tr, d), lambda i: (i, 0))
    vec = pl.BlockSpec((1, d), lambda i: (0, 0))
    return pl.pallas_call(
        _behind(after, body, 4), name=name, grid=(t // tr,),
        in_specs=[row, row, vec, row] + [_ANY] * len(after),
        out_specs=[row, row, vec],
        out_shape=[jax.ShapeDtypeStruct((t, d), F32), jax.ShapeDtypeStruct((t, d), BF16),
                   jax.ShapeDtypeStruct((1, d), F32)],
        compiler_params=_cparams("arbitrary"),
    )(dn, h, g, dres, *after)


def _loss_head(h, g, target, after=()):
    t, d = h.shape
    tr = _pick(t, (256, 128, 64, 32, 16, 8))

    def body(h_ref, g_ref, t_ref, loss_ref, dh_ref, dhb_ref, dg_ref):
        x = h_ref[...]
        gv = g_ref[...]
        r = lax.rsqrt(jnp.mean(x * x, axis=-1, keepdims=True) + EPS)
        xhat = x * r
        err = xhat * gv - t_ref[...]

        @pl.when(pl.program_id(0) == 0)
        def _():
            dg_ref[...] = jnp.zeros_like(dg_ref)
            loss_ref[...] = jnp.zeros_like(loss_ref)

        loss_ref[...] += jnp.full(loss_ref.shape, 0.5 / d, F32) * jnp.sum(err * err)
        dy = err * (1.0 / d)
        dg_ref[...] += jnp.sum(dy * xhat, axis=0, keepdims=True)
        dxh = dy * gv
        dh = r * (dxh - xhat * jnp.mean(dxh * xhat, axis=-1, keepdims=True))
        dh_ref[...] = dh
        dhb_ref[...] = dh.astype(BF16)

    row = pl.BlockSpec((tr, d), lambda i: (i, 0))
    vec = pl.BlockSpec((1, d), lambda i: (0, 0))
    return pl.pallas_call(
        _behind(after, body, 3), name="loss_head", grid=(t // tr,),
        in_specs=[row, vec, row] + [_ANY] * len(after),
        out_specs=[pl.BlockSpec((1, LANES), lambda i: (0, 0)), row, row, vec],
        out_shape=[jax.ShapeDtypeStruct((1, LANES), F32), jax.ShapeDtypeStruct((t, d), F32),
                   jax.ShapeDtypeStruct((t, d), BF16), jax.ShapeDtypeStruct((1, d), F32)],
        compiler_params=_cparams("arbitrary"),
    )(h, g, target, *after)


def _cur(ts, width, col):
    return pl.BlockSpec((ts, width), lambda i: (i, col))


def _prev_halo(ts, halo, width, col):
    per = ts // halo
    return pl.BlockSpec((halo, width), lambda i: (jnp.maximum(i * per - 1, 0), col))


def _next_halo(ts, halo, width, col, n_rows):
    per = ts // halo
    last = n_rows // halo - 1
    return pl.BlockSpec((halo, width), lambda i: (jnp.minimum((i + 1) * per, last), col))


def _full(shape):
    nd = len(shape)
    return pl.BlockSpec(shape, lambda i: (0,) * nd)


def _shift_down(x, n):
    return x if n == 0 else pltpu.roll(x, n, 0)


def _shift_up(x, n):
    return x if n == 0 else pltpu.roll(x, x.shape[0] - n, 0)


CONV_ROWS = 32


def _conv_block_shape(channels, ts):
    return min(CONV_ROWS, ts), min(LANES, channels)


SUBLANES = 8


def _fill_shifted(rot_ref, ext):
    rot_ref[0] = ext
    for r in range(1, SUBLANES):
        rot_ref[r] = _shift_up(ext, r)


def _window(rot_ref, first, rows, c0, cw):
    r = first % SUBLANES
    return rot_ref[r, first - r:first - r + rows, c0:c0 + cw]


def _causal_taps(rot_ref, w_ref, halo, taps, r0, c0, rows, cw):
    acc = jnp.zeros((rows, cw), F32)
    for k in range(taps):
        acc = acc + w_ref[k:k + 1, c0:c0 + cw] * _window(rot_ref, halo + r0 - (taps - 1 - k), rows, c0, cw)
    return acc


def _pool_counts(i, ns, ts, w):
    pos = (i % ns) * ts + lax.broadcasted_iota(jnp.int32, (ts, 1), 0)
    return jnp.minimum(pos + 1, w).astype(F32)


def _pooled(cur, prev_tail, w, cnt):
    s = jnp.concatenate([prev_tail, cur], axis=0)
    d = 1
    while d < w:
        s = s + _shift_down(s, d)
        d *= 2
    return s[POOL_HALO:, :] / cnt - cur


def _mixer_e_fwd(u, conv_w, conv_b, ln_g, ln_b, w_pool, scale, seq, ts, after=()):
    t = u.shape[0]
    dc = conv_b.shape[1]
    ng, pg = w_pool.shape[0], w_pool.shape[1]
    taps = conv_w.shape[0]
    ns = seq // ts

    def body(val_ref, gate_ref, b_ref, pval_ref, pgate_ref, pb_ref, cw_ref, cb_ref, g_ref, be_ref, wp_ref, sc_ref,
             a2_ref, cat_ref, rot_ref):
        i = pl.program_id(0)
        keep_prev = jnp.where(i % ns == 0, 0.0, 1.0)
        a1 = val_ref[...] * _sigmoid(gate_ref[...])
        pa1 = pval_ref[...] * _sigmoid(pgate_ref[...]) * keep_prev
        _fill_shifted(rot_ref, jnp.concatenate([pa1, a1], axis=0))
        rows, cw = _conv_block_shape(dc, ts)
        for c0 in range(0, dc, cw):
            for r0 in range(0, ts, rows):
                acc = _causal_taps(rot_ref, cw_ref, CONV_HALO, taps, r0, c0, rows, cw)
                a2_ref[r0:r0 + rows, c0:c0 + cw] = acc + cb_ref[:, c0:c0 + cw]
        a2 = a2_ref[...]
        mu = jnp.mean(a2, axis=-1, keepdims=True)
        xc = a2 - mu
        rstd = lax.rsqrt(jnp.mean(xc * xc, axis=-1, keepdims=True) + EPS)
        a3 = xc * rstd * g_ref[...] + be_ref[...]
        cat_ref[:, 0:dc] = (a3 * _sigmoid(a3)).astype(BF16)
        for g in range(ng):
            lo, hi = g * pg, (g + 1) * pg
            w = POOL_WINDOWS[g]
            p = _pooled(b_ref[:, lo:hi], pb_ref[:, lo:hi] * keep_prev, w, _pool_counts(i, ns, ts, w))
            q = jnp.dot(p.astype(BF16), wp_ref[g].astype(BF16), preferred_element_type=F32)
            cat_ref[:, dc + lo:dc + hi] = (q * sc_ref[:, lo:hi]).astype(BF16)

    return pl.pallas_call(
        _behind(after, body, 12), name="mixer_e_fwd", grid=(t // ts,),
        in_specs=[_cur(ts, dc, 0), _cur(ts, dc, 1), _cur(ts, dc, 2),
                  _prev_halo(ts, CONV_HALO, dc, 0), _prev_halo(ts, CONV_HALO, dc, 1), _prev_halo(ts, POOL_HALO, dc, 2),
                  _full(conv_w.shape), _full(conv_b.shape), _full(ln_g.shape), _full(ln_b.shape),
                  _full(w_pool.shape), _full(scale.shape)] + [_ANY] * len(after),
        out_specs=[_cur(ts, dc, 0), _cur(ts, 2 * dc, 0)],
        out_shape=[jax.ShapeDtypeStruct((t, dc), F32), jax.ShapeDtypeStruct((t, 2 * dc), BF16)],
        scratch_shapes=[pltpu.VMEM((SUBLANES, CONV_HALO + ts, dc), F32)],
        compiler_params=_cparams("parallel"),
    )(u, u, u, u, u, u, conv_w, conv_b, ln_g, ln_b, w_pool, scale, *after)


def _mixer_e_bwd_norm(dcat, a2, ln_g, ln_b, ts, after=()):
    t, dc = a2.shape

    def body(d_ref, a2_ref, g_ref, be_ref, da2_ref, dg_ref, db_ref, dcb_ref):
        x = a2_ref[...]
        gv = g_ref[...]
        mu = jnp.mean(x, axis=-1, keepdims=True)
        xc = x - mu
        rstd = lax.rsqrt(jnp.mean(xc * xc, axis=-1, keepdims=True) + EPS)
        xhat = xc * rstd
        a3 = xhat * gv + be_ref[...]
        sg = _sigmoid(a3)
        da3 = d_ref[...] * (sg * (1.0 + a3 * (1.0 - sg)))
        dxh = da3 * gv
        da2 = rstd * (dxh - jnp.mean(dxh, axis=-1, keepdims=True)
                      - xhat * jnp.mean(dxh * xhat, axis=-1, keepdims=True))
        da2_ref[...] = da2

        @pl.when(pl.program_id(0) == 0)
        def _():
            dg_ref[...] = jnp.zeros_like(dg_ref)
            db_ref[...] = jnp.zeros_like(db_ref)
            dcb_ref[...] = jnp.zeros_like(dcb_ref)

        dg_ref[...] += jnp.sum(da3 * xhat, axis=0, keepdims=True)
        db_ref[...] += jnp.sum(da3, axis=0, keepdims=True)
        dcb_ref[...] += jnp.sum(da2, axis=0, keepdims=True)

    vec = _full((1, dc))
    return pl.pallas_call(
        _behind(after, body, 4), name="mixer_e_bwd_norm", grid=(t // ts,),
        in_specs=[_cur(ts, dc, 0), _cur(ts, dc, 0), vec, vec] + [_ANY] * len(after),
        out_specs=[_cur(ts, dc, 0), vec, vec, vec],
        out_shape=[jax.ShapeDtypeStruct((t, dc), F32)] + [jax.ShapeDtypeStruct((1, dc), F32)] * 3,
        compiler_params=_cparams("arbitrary"),
    )(dcat, a2, ln_g, ln_b, *after)


def _mixer_e_bwd_mix(da2, dcat, u, conv_w, w_pool, scale, seq, ts, after=()):
    t, dc = da2.shape
    ng, pg = w_pool.shape[0], w_pool.shape[1]
    taps = conv_w.shape[0]
    ns = seq // ts

    def body(da2_ref, nda2_ref, dp_ref, ndp_ref, val_ref, gate_ref, b_ref, pval_ref, pgate_ref, pb_ref,
             cw_ref, wp_ref, sc_ref, du_ref, dcw_ref, dwp_ref, dsc_ref, rota_ref, rotd_ref):
        i = pl.program_id(0)
        keep_prev = jnp.where(i % ns == 0, 0.0, 1.0)
        keep_next = jnp.where(i % ns == ns - 1, 0.0, 1.0)

        @pl.when(i == 0)
        def _():
            dcw_ref[...] = jnp.zeros_like(dcw_ref)
            dwp_ref[...] = jnp.zeros_like(dwp_ref)
            dsc_ref[...] = jnp.zeros_like(dsc_ref)

        val = val_ref[...]
        sg = _sigmoid(gate_ref[...])
        a1 = val * sg
        pa1 = pval_ref[...] * _sigmoid(pgate_ref[...]) * keep_prev
        _fill_shifted(rota_ref, jnp.concatenate([pa1, a1], axis=0))
        _fill_shifted(rotd_ref, jnp.concatenate([da2_ref[...], nda2_ref[...] * keep_next], axis=0))
        rows, cw = _conv_block_shape(dc, ts)
        for c0 in range(0, dc, cw):
            lanes = slice(c0, c0 + cw)
            dw = [jnp.zeros((SUBLANES, cw), F32)] * taps
            for r0 in range(0, ts, rows):
                blk = slice(r0, r0 + rows)
                d_blk = da2_ref[blk, lanes]
                da1 = jnp.zeros((rows, cw), F32)
                for k in range(taps):
                    sh = taps - 1 - k
                    prod = d_blk * _window(rota_ref, CONV_HALO + r0 - sh, rows, c0, cw)
                    for f in range(0, rows, SUBLANES):
                        dw[k] = dw[k] + prod[f:f + SUBLANES, :]
                    da1 = da1 + cw_ref[k:k + 1, lanes] * _window(rotd_ref, r0 + sh, rows, c0, cw)
                sg_b = sg[blk, lanes]
                du_ref[blk, lanes] = (da1 * sg_b).astype(BF16)
                du_ref[blk, dc + c0:dc + c0 + cw] = (da1 * a1[blk, lanes] * (1.0 - sg_b)).astype(BF16)
            for k in range(taps):
                dcw_ref[k:k + 1, lanes] += jnp.sum(dw[k], axis=0, keepdims=True)

        for g in range(ng):
            lo, hi = g * pg, (g + 1) * pg
            w = POOL_WINDOWS[g]
            cnt = _pool_counts(i, ns, ts, w)
            wpb = wp_ref[g].astype(BF16)
            sc = sc_ref[:, lo:hi]
            p = _pooled(b_ref[:, lo:hi], pb_ref[:, lo:hi] * keep_prev, w, cnt)
            pb16 = p.astype(BF16)
            q = jnp.dot(pb16, wpb, preferred_element_type=F32)
            dout = dp_ref[:, lo:hi]
            dsc_ref[:, lo:hi] += jnp.sum(dout * q, axis=0, keepdims=True)
            dq = (dout * sc).astype(BF16)
            dwp_ref[g] += lax.dot_general(pb16, dq, _DOT_DIMS["tn"], preferred_element_type=F32)
            dpool = lax.dot_general(dq, wpb, _DOT_DIMS["nt"], preferred_element_type=F32)
            ndq = (ndp_ref[:, lo:hi] * sc * keep_next).astype(BF16)
            ndpool = lax.dot_general(ndq, wpb, _DOT_DIMS["nt"], preferred_element_type=F32)
            s = jnp.concatenate([dpool / cnt, ndpool * (1.0 / w)], axis=0)
            d = 1
            while d < w:
                s = s + _shift_up(s, d)
                d *= 2
            du_ref[:, 2 * dc + lo:2 * dc + hi] = (s[:ts, :] - dpool).astype(BF16)

    return pl.pallas_call(
        _behind(after, body, 13), name="mixer_e_bwd_mix", grid=(t // ts,),
        in_specs=[_cur(ts, dc, 0), _next_halo(ts, CONV_HALO, dc, 0, t),
                  _cur(ts, dc, 1), _next_halo(ts, POOL_HALO, dc, 1, t),
                  _cur(ts, dc, 0), _cur(ts, dc, 1), _cur(ts, dc, 2),
                  _prev_halo(ts, CONV_HALO, dc, 0), _prev_halo(ts, CONV_HALO, dc, 1), _prev_halo(ts, POOL_HALO, dc, 2),
                  _full(conv_w.shape), _full(w_pool.shape), _full(scale.shape)] + [_ANY] * len(after),
        out_specs=[_cur(ts, 3 * dc, 0), _full(conv_w.shape), _full(w_pool.shape), _full(scale.shape)],
        out_shape=[jax.ShapeDtypeStruct((t, 3 * dc), BF16), jax.ShapeDtypeStruct(conv_w.shape, F32),
                   jax.ShapeDtypeStruct(w_pool.shape, F32), jax.ShapeDtypeStruct(scale.shape, F32)],
        scratch_shapes=[pltpu.VMEM((SUBLANES, ts + CONV_HALO, dc), F32)] * 2,
        compiler_params=_cparams("arbitrary"),
    )(da2, da2, dcat, dcat, u, u, u, u, u, u, conv_w, w_pool, scale, *after)


def _mixer_o_fwd(u, conv_w, seq, ts, after=()):
    t = u.shape[0]
    d = conv_w.shape[1]
    taps = conv_w.shape[0]
    ns = seq // ts

    def body(gb_ref, gc_ref, v_ref, pgc_ref, pv_ref, cw_ref, y_ref):
        keep_prev = jnp.where(pl.program_id(0) % ns == 0, 0.0, 1.0)
        f32 = lambda ref: ref[...].astype(F32)
        ext = jnp.concatenate([f32(pgc_ref) * f32(pv_ref) * keep_prev, f32(gc_ref) * f32(v_ref)], axis=0)
        cc = jnp.zeros_like(ext)
        for k in range(taps):
            cc = cc + cw_ref[k:k + 1, :] * _shift_down(ext, taps - 1 - k)
        y_ref[...] = (f32(gb_ref) * cc[SHORT_HALO:, :]).astype(BF16)

    return pl.pallas_call(
        _behind(after, body, 6), name="mixer_o_fwd", grid=(t // ts,),
        in_specs=[_cur(ts, d, 0), _cur(ts, d, 1), _cur(ts, d, 2),
                  _prev_halo(ts, SHORT_HALO, d, 1), _prev_halo(ts, SHORT_HALO, d, 2), _full(conv_w.shape)]
        + [_ANY] * len(after),
        out_specs=_cur(ts, d, 0),
        out_shape=jax.ShapeDtypeStruct((t, d), BF16),
        compiler_params=_cparams("parallel"),
    )(u, u, u, u, u, conv_w, *after)


def _mixer_o_bwd(dy, u, conv_w, seq, ts, after=()):
    t = u.shape[0]
    d = conv_w.shape[1]
    taps = conv_w.shape[0]
    ns = seq // ts

    def body(dy_ref, ndy_ref, gb_ref, gc_ref, v_ref, pgc_ref, pv_ref, ngb_ref, cw_ref, du_ref, dcw_ref):
        i = pl.program_id(0)
        keep_prev = jnp.where(i % ns == 0, 0.0, 1.0)
        keep_next = jnp.where(i % ns == ns - 1, 0.0, 1.0)

        @pl.when(i == 0)
        def _():
            dcw_ref[...] = jnp.zeros_like(dcw_ref)

        f32 = lambda ref: ref[...].astype(F32)
        gb, gc, v, dyv = f32(gb_ref), f32(gc_ref), f32(v_ref), f32(dy_ref)
        ext = jnp.concatenate([f32(pgc_ref) * f32(pv_ref) * keep_prev, gc * v], axis=0)
        dcc = dyv * gb
        ext_d = jnp.concatenate([dcc, f32(ndy_ref) * f32(ngb_ref) * keep_next], axis=0)
        cc = jnp.zeros_like(ext)
        dcv = jnp.zeros_like(ext_d)
        for k in range(taps):
            sh = taps - 1 - k
            shifted = _shift_down(ext, sh)
            cc = cc + cw_ref[k:k + 1, :] * shifted
            dcw_ref[k:k + 1, :] += jnp.sum(dcc * shifted[SHORT_HALO:, :], axis=0, keepdims=True)
            dcv = dcv + cw_ref[k:k + 1, :] * _shift_up(ext_d, sh)
        dcv = dcv[:ts, :]
        du_ref[:, 0:d] = (dyv * cc[SHORT_HALO:, :]).astype(BF16)
        du_ref[:, d:2 * d] = (dcv * v).astype(BF16)
        du_ref[:, 2 * d:3 * d] = (dcv * gc).astype(BF16)

    return pl.pallas_call(
        _behind(after, body, 9), name="mixer_o_bwd", grid=(t // ts,),
        in_specs=[_cur(ts, d, 0), _next_halo(ts, SHORT_HALO, d, 0, t),
                  _cur(ts, d, 0), _cur(ts, d, 1), _cur(ts, d, 2),
                  _prev_halo(ts, SHORT_HALO, d, 1), _prev_halo(ts, SHORT_HALO, d, 2),
                  _next_halo(ts, SHORT_HALO, d, 0, t), _full(conv_w.shape)] + [_ANY] * len(after),
        out_specs=[_cur(ts, 3 * d, 0), _full(conv_w.shape)],
        out_shape=[jax.ShapeDtypeStruct((t, 3 * d), BF16), jax.ShapeDtypeStruct(conv_w.shape, F32)],
        compiler_params=_cparams("arbitrary"),
    )(dy, dy, u, u, u, u, u, u, conv_w, *after)


def _cast_into_full(name, mat, w, layer, chip, after=()):
    tr = _pick(mat.sr, (512, 256, 128, 64, 32, 16))
    per = mat.sr // tr

    def body(chip_ref, w_ref, *rest):
        o_ref = rest[-1]
        o_ref[...] = w_ref[...].astype(BF16)

    if mat.kind == "col":
        o_spec = pl.BlockSpec((tr, mat.sc), lambda i, chip_ref: (i, chip_ref[0]))
    else:
        o_spec = pl.BlockSpec((tr, mat.sc), lambda i, chip_ref: (chip_ref[0] * per + i, 0))
    return pl.pallas_call(
        body, name=name,
        grid_spec=pltpu.PrefetchScalarGridSpec(
            num_scalar_prefetch=1, grid=(per,),
            in_specs=[pl.BlockSpec((None, tr, mat.sc), lambda i, chip_ref: (layer, i, 0))] + [_ANY] * len(after),
            out_specs=o_spec),
        out_shape=jax.ShapeDtypeStruct(mat.full_shape, BF16),
        compiler_params=_cparams("parallel"),
    )(chip, w, *after)


def _adamw(name, w, g, m, v, copy_grad=False, after=()):
    r, c = w.shape
    tr = _pick(r, (256, 128, 64, 32, 16, 8)) if c > 1024 else _pick(r, (512, 256, 128, 64, 32, 16, 8))
    bc1 = 1.0 - ADAM_B1 ** ADAM_STEP
    bc2 = 1.0 - ADAM_B2 ** ADAM_STEP

    def body(w_ref, g_ref, m_ref, v_ref, d_ref, mo_ref, vo_ref, *rest):
        gv = g_ref[...]
        mn = ADAM_B1 * m_ref[...] + (1.0 - ADAM_B1) * gv
        vn = ADAM_B2 * v_ref[...] + (1.0 - ADAM_B2) * (gv * gv)
        mo_ref[...] = mn
        vo_ref[...] = vn
        d_ref[...] = -ADAM_LR * ((mn * (1.0 / bc1)) / (jnp.sqrt(vn * (1.0 / bc2)) + ADAM_EPS) + ADAM_WD * w_ref[...])

        if copy_grad:
            rest[0][...] = gv

    spec = pl.BlockSpec((tr, c), lambda i: (i, 0))
    n_out = 4 if copy_grad else 3
    return pl.pallas_call(_behind(after, body, 4), name=name, grid=(r // tr,),
                          in_specs=[spec] * 4 + [_ANY] * len(after), out_specs=[spec] * n_out,
                          out_shape=[jax.ShapeDtypeStruct((r, c), F32)] * n_out,
                          compiler_params=_cparams("parallel"))(w, g, m, v, *after)


def _aligned(offset, multiple):
    return offset if isinstance(offset, int) else pl.multiple_of(offset, multiple)


class _Mat:
    def __init__(self, kind, shard_shape):
        self.kind = kind
        self.sr, self.sc = shard_shape
        self.full_shape = (self.sr, self.sc * N_CHIPS) if kind == "col" else (self.sr * N_CHIPS, self.sc)
        self.pr, self.pc = self.sr // 2, self.sc

    def piece(self, ref, k, h):
        if self.kind == "col":
            return ref.at[pl.ds(_aligned(h * self.pr, 16), self.pr), pl.ds(_aligned(k * self.sc, LANES), self.sc)]
        return ref.at[pl.ds(_aligned(k * self.sr + h * self.pr, 16), self.pr), :]

    def shard(self, ref, k):
        if self.kind == "col":
            return ref.at[:, pl.ds(_aligned(k * self.sc, LANES), self.sc)]
        return ref.at[pl.ds(_aligned(k * self.sr, 16), self.sr), :]

    def half(self, ref, h):
        return ref.at[pl.ds(_aligned(h * self.pr, 16), self.pr), :]


def _place():
    x, y, c = lax.axis_index("x"), lax.axis_index("y"), lax.axis_index("c")
    others = [(1 - x, y), (x, 1 - y), (1 - x, 1 - y)]
    return x, y, c, others


_HBM = pl.BlockSpec(memory_space=pltpu.HBM)
_SEM = pl.BlockSpec(memory_space=pltpu.SEMAPHORE)
_TOKEN = jax.ShapeDtypeStruct((8, LANES), F32)
_TOKEN_SPEC = pl.BlockSpec(memory_space=pltpu.VMEM)


def _split_params():
    return pltpu.CompilerParams(has_side_effects=pltpu.SideEffectType.DATAFLOW_SIDE_EFFECTING)


def _in_hbm(a):
    return pltpu.with_memory_space_constraint(a, pltpu.HBM)


def _copy_to(src, dst, send_sem, recv_sem, to):
    return pltpu.make_async_remote_copy(src_ref=src, dst_ref=dst, send_sem=send_sem, recv_sem=recv_sem,
                                        device_id=to, device_id_type=MESH_ID)


def _place_small(packed, chip):
    rows, cols = packed.shape

    def body(chip_ref, p_ref, o_ref):
        o_ref[...] = p_ref[...]

    return pl.pallas_call(
        body, name="place_small",
        grid_spec=pltpu.PrefetchScalarGridSpec(
            num_scalar_prefetch=1, grid=(1,),
            in_specs=[pl.BlockSpec((rows, cols), lambda i, chip_ref: (0, 0))],
            out_specs=pl.BlockSpec((None, rows, cols), lambda i, chip_ref: (chip_ref[0], 0, 0))),
        out_shape=jax.ShapeDtypeStruct((N_CHIPS, rows, cols), F32),
        compiler_params=_cparams("arbitrary"),
    )(chip, packed)


def _gather_start(name, gmats, gfulls, small_all=None):
    n = len(gmats)
    n_in = n + (1 if small_all is not None else 0)

    def body(*refs):
        full_refs = refs[:n]
        outs = refs[n_in:]
        send_sem, recv_sem, token = outs[n_in], outs[n_in + 1], outs[n_in + 2]
        x, y, c, others = _place()
        me_k = 2 * x + y
        if small_all is not None:
            mine = refs[n].at[me_k]
            for ox, oy in others:
                _copy_to(mine, mine, send_sem, recv_sem, (ox, oy, c)).start()
        for m in range(n):
            mine = gmats[m].piece(full_refs[m], me_k, c)
            for ox, oy in others:
                _copy_to(mine, mine, send_sem, recv_sem, (ox, oy, c)).start()
        token[...] = jnp.zeros_like(token)

    operands = [_in_hbm(f) for f in gfulls] + ([_in_hbm(small_all)] if small_all is not None else [])
    outs = pl.pallas_call(
        body, name=name,
        in_specs=[_HBM] * n_in,
        out_specs=[_HBM] * n_in + [_SEM, _SEM, _TOKEN_SPEC],
        out_shape=[pltpu.HBM(a.shape, a.dtype) for a in operands] + [pltpu.SemaphoreType.DMA(())] * 2 + [_TOKEN],
        input_output_aliases={m: m for m in range(n_in)},
        compiler_params=_split_params(),
    )(*operands)
    return list(outs[:n]), (outs[n] if small_all is not None else None), (outs[n_in], outs[n_in + 1]), outs[n_in + 2]


def _gather_pass(name, gmats, gfulls, small_all, sems, after):
    k = len(gmats)
    n_buf = k + (1 if small_all is not None else 0)

    def body(*refs):
        bufs = refs[:n_buf]
        send_sem, recv_sem = refs[n_buf], refs[n_buf + 1]
        outs = refs[n_buf + 3:]
        fsend, frecv, token = outs[n_buf], outs[n_buf + 1], outs[n_buf + 2]
        x, y, c, others = _place()
        me_k = 2 * x + y
        sibling = (x, y, 1 - c)
        for m in range(k):
            for ox, oy in others:
                got = gmats[m].piece(bufs[m], 2 * ox + oy, c)
                _copy_to(got, got, send_sem, recv_sem, sibling).wait_recv()
        if small_all is not None:
            for ox, oy in others:
                got = bufs[k].at[2 * ox + oy]
                _copy_to(got, got, send_sem, recv_sem, sibling).wait_recv()
        for m in range(k):
            mine = gmats[m].piece(bufs[m], me_k, c)
            for _ in others:
                _copy_to(mine, mine, send_sem, recv_sem, sibling).wait_send()
        if small_all is not None:
            for _ in others:
                _copy_to(bufs[k].at[me_k], bufs[k].at[me_k], send_sem, recv_sem, sibling).wait_send()
        for m in range(k):
            for ox, oy in others:
                got = gmats[m].piece(bufs[m], 2 * ox + oy, c)
                _copy_to(got, got, fsend, frecv, sibling).start()
        token[...] = jnp.zeros_like(token)

    operands = [_in_hbm(f) for f in gfulls] + ([_in_hbm(small_all)] if small_all is not None else [])
    outs = pl.pallas_call(
        body, name=name,
        in_specs=[_HBM] * n_buf + [_SEM, _SEM, _ANY],
        out_specs=[_HBM] * n_buf + [_SEM, _SEM, _TOKEN_SPEC],
        out_shape=[pltpu.HBM(a.shape, a.dtype) for a in operands] + [pltpu.SemaphoreType.DMA(())] * 2 + [_TOKEN],
        input_output_aliases={i: i for i in range(n_buf)},
        compiler_params=_split_params(),
    )(*operands, sems[0], sems[1], after)
    return list(outs[:n_buf]), (outs[n_buf], outs[n_buf + 1]), outs[n_buf + 2]


def _gather_done(name, gmats, gfulls, sems, after):
    k = len(gmats)

    def body(*refs):
        bufs = refs[:k]
        send_sem, recv_sem = refs[k], refs[k + 1]
        x, y, c, others = _place()
        sibling = (x, y, 1 - c)
        for m in range(k):
            for ox, oy in others:
                got = gmats[m].piece(bufs[m], 2 * ox + oy, 1 - c)
                _copy_to(got, got, send_sem, recv_sem, sibling).wait_recv()
        for m in range(k):
            for ox, oy in others:
                sent = gmats[m].piece(bufs[m], 2 * ox + oy, c)
                _copy_to(sent, sent, send_sem, recv_sem, sibling).wait_send()

    outs = pl.pallas_call(
        body, name=name,
        in_specs=[_HBM] * k + [_SEM, _SEM, _ANY], out_specs=[_HBM] * k,
        out_shape=[pltpu.HBM(a.shape, a.dtype) for a in gfulls],
        input_output_aliases={i: i for i in range(k)},
        compiler_params=_split_params(),
    )(*[_in_hbm(f) for f in gfulls], sems[0], sems[1], after)
    return list(outs)


_FLIPS = [(fx, fy, fc) for fx in (0, 1) for fy in (0, 1) for fc in (0, 1) if (fx, fy, fc) != (0, 0, 0)]


def _small_start(packed, after):
    def body(small_ref, after_ref, small_thru, land_ref, send_sem, recv_sem, token):
        x, y, c, _ = _place()
        me = 4 * x + 2 * y + c
        for fx, fy, fc in _FLIPS:
            _copy_to(small_ref, land_ref.at[me], send_sem, recv_sem, (x ^ fx, y ^ fy, c ^ fc)).start()
        token[...] = jnp.zeros_like(token)

    outs = pl.pallas_call(
        body, name="small_grads_start",
        in_specs=[_HBM, _ANY], out_specs=[_HBM, _HBM, _SEM, _SEM, _TOKEN_SPEC],
        out_shape=[pltpu.HBM(packed.shape, F32), pltpu.HBM((N_DEV,) + packed.shape, F32)]
        + [pltpu.SemaphoreType.DMA(())] * 2 + [_TOKEN],
        input_output_aliases={0: 0},
        compiler_params=_split_params(),
    )(_in_hbm(packed), after)
    return outs[0], outs[1], (outs[2], outs[3]), outs[4]


def _small_wait(packed, landed, sems, after):
    def body(small_ref, land_ref, send_sem, recv_sem, after_ref, small_thru, land_thru):
        x, y, c, _ = _place()
        for fx, fy, fc in _FLIPS:
            got = land_ref.at[4 * (x ^ fx) + 2 * (y ^ fy) + (c ^ fc)]
            _copy_to(got, got, send_sem, recv_sem, (x, y, 1 - c)).wait_recv()
        for _ in _FLIPS:
            _copy_to(small_ref, small_ref, send_sem, recv_sem, (x, y, 1 - c)).wait_send()

    outs = pl.pallas_call(
        body, name="small_grads_wait",
        in_specs=[_HBM, _HBM, _SEM, _SEM, _ANY], out_specs=[_HBM, _HBM],
        out_shape=[pltpu.HBM(packed.shape, F32), pltpu.HBM(landed.shape, F32)],
        input_output_aliases={0: 0, 1: 1},
        compiler_params=_split_params(),
    )(_in_hbm(packed), _in_hbm(landed), sems[0], sems[1], after)
    return outs[0], outs[1]


def _exchange_start(name, mats, grads):
    n = len(mats)

    def body(*refs):
        g_refs = refs[:n]
        outs = refs[n:]
        land_refs = outs[n:2 * n]
        send_sem, recv_sem, token = outs[2 * n], outs[2 * n + 1], outs[2 * n + 2]
        x, y, c, _ = _place()
        for m in range(n):
            for k in range(N_CHIPS):
                _copy_to(mats[m].piece(g_refs[m], k, 1 - c), land_refs[m].at[k], send_sem, recv_sem, (x, y, 1 - c)).start()
        token[...] = jnp.zeros_like(token)

    outs = pl.pallas_call(
        body, name=name,
        in_specs=[_HBM] * n,
        out_specs=[_HBM] * (2 * n) + [_SEM, _SEM, _TOKEN_SPEC],
        out_shape=[pltpu.HBM(mt.full_shape, BF16) for mt in mats]
        + [pltpu.HBM((N_CHIPS, mt.pr, mt.pc), BF16) for mt in mats] + [pltpu.SemaphoreType.DMA(())] * 2 + [_TOKEN],
        input_output_aliases={m: m for m in range(n)},
        compiler_params=_split_params(),
    )(*[_in_hbm(g) for g in grads])
    return list(outs[:n]), list(outs[n:2 * n]), (outs[2 * n], outs[2 * n + 1]), outs[2 * n + 2]


def _exchange_wait(name, mats, grads, landed, sems, after):
    n = len(mats)

    def body(*refs):
        g_refs, land_refs = refs[:n], refs[n:2 * n]
        send_sem, recv_sem = refs[2 * n], refs[2 * n + 1]
        x, y, c, _ = _place()
        for m in range(n):
            for k in range(N_CHIPS):
                got = land_refs[m].at[k]
                _copy_to(got, got, send_sem, recv_sem, (x, y, 1 - c)).wait_recv()
        for m in range(n):
            for k in range(N_CHIPS):
                sent = mats[m].piece(g_refs[m], k, 1 - c)
                _copy_to(sent, sent, send_sem, recv_sem, (x, y, 1 - c)).wait_send()

    outs = pl.pallas_call(
        body, name=name,
        in_specs=[_HBM] * (2 * n) + [_SEM, _SEM, _ANY], out_specs=[_HBM] * (2 * n),
        out_shape=[pltpu.HBM(a.shape, a.dtype) for a in list(grads) + list(landed)],
        input_output_aliases={i: i for i in range(2 * n)},
        compiler_params=_split_params(),
    )(*[_in_hbm(a) for a in list(grads) + list(landed)], sems[0], sems[1], after)
    return list(outs[:n]), list(outs[n:])


def _add_halves(name, mat, grad, landed, core):
    tr = _pick(mat.pr, (1024, 704, 512, 352, 256, 128, 64, 32, 16))
    per = mat.pr // tr

    def body(core_ref, g_ref, l_ref, o_ref):
        o_ref[...] = (g_ref[...].astype(F32) + l_ref[...].astype(F32)).astype(BF16)

    if mat.kind == "col":
        g_spec = pl.BlockSpec((tr, mat.pc), lambda k, r, core_ref: (core_ref[0] * per + r, k))
    else:
        g_spec = pl.BlockSpec((tr, mat.pc), lambda k, r, core_ref: ((2 * k + core_ref[0]) * per + r, 0))
    p_spec = pl.BlockSpec((None, tr, mat.pc), lambda k, r, core_ref: (k, r, 0))
    return pl.pallas_call(
        body, name=name,
        grid_spec=pltpu.PrefetchScalarGridSpec(num_scalar_prefetch=1, grid=(N_CHIPS, per),
                                               in_specs=[g_spec, p_spec], out_specs=p_spec),
        out_shape=jax.ShapeDtypeStruct((N_CHIPS, mat.pr, mat.pc), BF16),
        compiler_params=_cparams("parallel", "parallel"),
    )(core, grad, landed)


def _scatter_start(name, mats, partials):
    n = len(mats)

    def body(*refs):
        p_refs = refs[:n]
        outs = refs[n:]
        land_refs = outs[n:2 * n]
        send_sem, recv_sem, token = outs[2 * n], outs[2 * n + 1], outs[2 * n + 2]
        x, y, c, others = _place()
        me_k = 2 * x + y
        for m in range(n):
            for ox, oy in others:
                _copy_to(p_refs[m].at[2 * ox + oy], land_refs[m].at[me_k], send_sem, recv_sem, (ox, oy, c)).start()
        token[...] = jnp.zeros_like(token)

    piece_shapes = [pltpu.HBM((N_CHIPS, mt.pr, mt.pc), BF16) for mt in mats]
    outs = pl.pallas_call(
        body, name=name,
        in_specs=[_HBM] * n,
        out_specs=[_HBM] * (2 * n) + [_SEM, _SEM, _TOKEN_SPEC],
        out_shape=piece_shapes + piece_shapes + [pltpu.SemaphoreType.DMA(())] * 2 + [_TOKEN],
        input_output_aliases={m: m for m in range(n)},
        compiler_params=_split_params(),
    )(*[_in_hbm(p) for p in partials])
    return list(outs[:n]), list(outs[n:2 * n]), (outs[2 * n], outs[2 * n + 1]), outs[2 * n + 2]


def _scatter_wait(name, mats, partials, landed, sems, after):
    n = len(mats)

    def body(*refs):
        p_refs, land_refs = refs[:n], refs[n:2 * n]
        send_sem, recv_sem = refs[2 * n], refs[2 * n + 1]
        x, y, c, others = _place()
        for m in range(n):
            for ox, oy in others:
                got = land_refs[m].at[2 * ox + oy]
                _copy_to(got, got, send_sem, recv_sem, (ox, oy, c)).wait_recv()
        for m in range(n):
            for ox, oy in others:
                sent = p_refs[m].at[2 * ox + oy]
                _copy_to(sent, sent, send_sem, recv_sem, (ox, oy, c)).wait_send()

    outs = pl.pallas_call(
        body, name=name,
        in_specs=[_HBM] * (2 * n) + [_SEM, _SEM, _ANY], out_specs=[_HBM] * (2 * n),
        out_shape=[pltpu.HBM(a.shape, a.dtype) for a in list(partials) + list(landed)],
        input_output_aliases={i: i for i in range(2 * n)},
        compiler_params=_split_params(),
    )(*[_in_hbm(a) for a in list(partials) + list(landed)], sems[0], sems[1], after)
    return list(outs[:n]), list(outs[n:])


def _sum_chips(name, mat, partial, landed, slots, layer=None, stack=None, n_layers=1):
    tr = _pick(mat.pr, (1024, 704, 512, 352, 256, 128, 64, 32, 16))
    per = mat.pr // tr

    def body(slots_ref, own_ref, a_ref, b_ref, c_ref, *rest):
        o_ref = rest[-1]
        o_ref[...] = ((own_ref[...].astype(F32) + a_ref[...].astype(F32)) + b_ref[...].astype(F32)) + c_ref[...].astype(F32)

    def slot_spec(which):
        return pl.BlockSpec((None, tr, mat.pc), lambda r, slots_ref: (slots_ref[which], r, 0))

    in_specs = [slot_spec(0), slot_spec(1), slot_spec(2), slot_spec(3)]
    operands = [slots, partial, landed, landed, landed]
    aliases = {}
    if layer is None:
        o_spec = pl.BlockSpec((tr, mat.pc), lambda r, slots_ref: (slots_ref[4] * per + r, 0))
        out_shape = jax.ShapeDtypeStruct((mat.sr, mat.sc), F32)
    else:
        o_spec = pl.BlockSpec((None, tr, mat.pc), lambda r, slots_ref: (layer, slots_ref[4] * per + r, 0))
        out_shape = jax.ShapeDtypeStruct((n_layers, mat.sr, mat.sc), F32)
        if stack is not None:
            in_specs.append(_ANY)
            operands.append(stack)
            aliases = {len(operands) - 1: 0}
    return pl.pallas_call(
        body, name=name,
        grid_spec=pltpu.PrefetchScalarGridSpec(num_scalar_prefetch=1, grid=(per,), in_specs=in_specs, out_specs=o_spec),
        out_shape=out_shape, input_output_aliases=aliases,
        compiler_params=_cparams("parallel"),
    )(*operands)


def _share_pieces(name, mats, shards, groups):
    n = len(mats)
    n_out = len(groups)

    def body(*refs):
        out_refs = refs[n_out:2 * n_out]
        send_sems, recv_sems = refs[2 * n_out:]
        x, y, c, _ = _place()
        sibling = (x, y, 1 - c)
        sent, waits = [], []
        for o, members in enumerate(groups):
            for l, m in enumerate(members):
                dst = out_refs[o].at[l] if len(members) > 1 else out_refs[o]
                mine = mats[m].half(dst, c)
                sent.append(pltpu.make_async_remote_copy(src_ref=mine, dst_ref=mine, send_sem=send_sems.at[m],
                                                         recv_sem=recv_sems.at[m], device_id=sibling, device_id_type=MESH_ID))
                theirs = mats[m].half(dst, 1 - c)
                waits.append(pltpu.make_async_remote_copy(src_ref=theirs, dst_ref=theirs, send_sem=send_sems.at[m],
                                                          recv_sem=recv_sems.at[m], device_id=sibling, device_id_type=MESH_ID))
        for cp in sent:
            cp.start()
        for cp in waits:
            cp.wait_recv()
        for cp in sent:
            cp.wait_send()

    return pl.pallas_call(
        body, name=name,
        in_specs=[_ANY] * n_out, out_specs=[_ANY] * n_out,
        out_shape=[jax.ShapeDtypeStruct(s.shape, F32) for s in shards],
        input_output_aliases={o: o for o in range(n_out)},
        scratch_shapes=[pltpu.SemaphoreType.DMA((n,)), pltpu.SemaphoreType.DMA((n,))],
    )(*shards)


def _share_start(name, mats, shards, items):
    n_out = len(shards)

    def body(*refs):
        out_refs = refs[n_out:2 * n_out]
        send_sem, recv_sem, token = refs[2 * n_out], refs[2 * n_out + 1], refs[2 * n_out + 2]
        x, y, c, _ = _place()
        for o, members in enumerate(items):
            for layer, m in members:
                dst = out_refs[o] if layer is None else out_refs[o].at[layer]
                mine = mats[m].half(dst, c)
                _copy_to(mine, mine, send_sem, recv_sem, (x, y, 1 - c)).start()
        token[...] = jnp.zeros_like(token)

    outs = pl.pallas_call(
        body, name=name,
        in_specs=[_HBM] * n_out, out_specs=[_HBM] * n_out + [_SEM, _SEM, _TOKEN_SPEC],
        out_shape=[pltpu.HBM(s.shape, F32) for s in shards] + [pltpu.SemaphoreType.DMA(())] * 2 + [_TOKEN],
        input_output_aliases={o: o for o in range(n_out)},
        compiler_params=_split_params(),
    )(*[_in_hbm(s) for s in shards])
    return list(outs[:n_out]), (outs[n_out], outs[n_out + 1]), outs[n_out + 2]


def _share_wait(name, mats, shards, items, sems, after):
    n_out = len(shards)

    def body(*refs):
        bufs = refs[:n_out]
        send_sem, recv_sem = refs[n_out], refs[n_out + 1]
        x, y, c, _ = _place()
        for o, members in enumerate(items):
            for layer, m in members:
                dst = bufs[o] if layer is None else bufs[o].at[layer]
                theirs = mats[m].half(dst, 1 - c)
                _copy_to(theirs, theirs, send_sem, recv_sem, (x, y, 1 - c)).wait_recv()
        for o, members in enumerate(items):
            for layer, m in members:
                dst = bufs[o] if layer is None else bufs[o].at[layer]
                mine = mats[m].half(dst, c)
                _copy_to(mine, mine, send_sem, recv_sem, (x, y, 1 - c)).wait_send()

    outs = pl.pallas_call(
        body, name=name,
        in_specs=[_HBM] * n_out + [_SEM, _SEM, _ANY], out_specs=[_HBM] * n_out,
        out_shape=[pltpu.HBM(s.shape, F32) for s in shards],
        input_output_aliases={o: o for o in range(n_out)},
        compiler_params=_split_params(),
    )(*[_in_hbm(s) for s in shards], sems[0], sems[1], after)
    return list(outs)


def _sum_devices(stacked):
    nd, r, c = stacked.shape

    def body(s_ref, o_ref):
        s = s_ref[0]
        for k in range(1, nd):
            s = s + s_ref[k]
        o_ref[...] = s

    return pl.pallas_call(
        body, name="sum_small_grads", grid=(1,),
        in_specs=[pl.BlockSpec((nd, r, c), lambda i: (0, 0, 0))],
        out_specs=pl.BlockSpec((r, c), lambda i: (0, 0)),
        out_shape=jax.ShapeDtypeStruct((r, c), F32),
        compiler_params=_cparams("arbitrary"),
    )(stacked)


def _pack(arrs):
    flat = jnp.concatenate([a.reshape(-1) for a in arrs])
    rows = -(-flat.shape[0] // (8 * LANES)) * 8
    return jnp.pad(flat, (0, rows * LANES - flat.shape[0])).reshape(rows, LANES)


def _unpack(packed, shapes):
    flat = packed.reshape(-1)
    out, at = [], 0
    for s in shapes:
        size = 1
        for dim in s:
            size *= dim
        out.append(flat[at:at + size].reshape(s))
        at += size
    return out


def kernel(x, mix_norm_e, w_in_e, conv_w_e, conv_b_e, ln_g_e, ln_b_e, w_pool_e, pool_scale_e, w_out_e, mix_norm_o, w_in_o, conv_w_o, w_out_o, ffn_norm, w_gate, w_up, w_down, final_norm, loss_target, m_mix_norm_e, m_w_in_e, m_conv_w_e, m_conv_b_e, m_ln_g_e, m_ln_b_e, m_w_pool_e, m_pool_scale_e, m_w_out_e, m_mix_norm_o, m_w_in_o, m_conv_w_o, m_w_out_o, m_ffn_norm, m_w_gate, m_w_up, m_w_down, m_final_norm, v_mix_norm_e, v_w_in_e, v_conv_w_e, v_conv_b_e, v_ln_g_e, v_ln_b_e, v_w_pool_e, v_pool_scale_e, v_w_out_e, v_mix_norm_o, v_w_in_o, v_conv_w_o, v_w_out_o, v_ffn_norm, v_w_gate, v_w_up, v_w_down, v_final_norm):
    bsz, seq_len, d = x.shape
    t = bsz * seq_len
    depth = ffn_norm.shape[0]
    assert depth == 2 and conv_b_e.shape[1] == pool_scale_e.shape[1]
    ts = _pick(seq_len, (256, 128, 64, 32))
    me_k = 2 * lax.axis_index("x") + lax.axis_index("y")
    core = lax.axis_index("c").astype(jnp.int32).reshape(1)

    mat_src = [("col", w_in_e, 0), ("row", w_out_e, 0), ("col", w_gate, 0), ("col", w_up, 0), ("row", w_down, 0),
               ("col", w_in_o, 0), ("row", w_out_o, 0), ("col", w_gate, 1), ("col", w_up, 1), ("row", w_down, 1)]
    mats = [_Mat(kind, w.shape[1:]) for kind, w, _ in mat_src]
    n_pool = w_pool_e.shape[1]
    pool_mats = tuple(range(len(mats), len(mats) + n_pool))
    mats = mats + [_Mat("row", w_pool_e.shape[2:])] * n_pool
    chip = me_k.astype(jnp.int32).reshape(1)
    small_shards = [conv_w_e[0], w_pool_e[0], mix_norm_o, conv_w_o[0]]
    packed_small = _pack(small_shards)

    chain = [()]

    def seq(fn, *args, **kw):
        out = fn(*args, after=chain[0], **kw)
        chain[0] = (out[0] if isinstance(out, (list, tuple)) else out,)
        return out

    def mm(*args, **kw):
        return seq(_mm, *args, **kw)

    gather_groups = [(0,), (1,), (2, 3), (4,), (5, 6), (7, 8), (9,)]
    fulls, gather_sems, small_all = [None] * len(mats), [], None
    for g, ms in enumerate(gather_groups):
        own16 = [seq(_cast_into_full, "cast_w%d" % m, mats[m], mat_src[m][1], mat_src[m][2], chip) for m in ms]
        sent, landing, sems, token = _gather_start("gather_start%d" % g, [mats[m] for m in ms], own16,
                                                   _place_small(packed_small, chip) if g == 0 else None)
        chain[0] = (token,)
        for m, f in zip(ms, sent):
            fulls[m] = f
        gather_sems.append(sems)
        if g == 0:
            small_all = landing

    passed = {}

    def gather_pass(g):
        ms = gather_groups[g]
        bufs, pass_sems, token = _gather_pass("gather_pass%d" % g, [mats[m] for m in ms], [fulls[m] for m in ms],
                                              small_all if g == 0 else None, gather_sems[g], chain[0][0])
        chain[0] = (token,)
        passed[g] = (bufs, pass_sems)

    def gather_done(g):
        ms = gather_groups[g]
        bufs, pass_sems = passed[g]
        done = _gather_done("gather_done%d" % g, [mats[m] for m in ms], bufs[:len(ms)], pass_sems, chain[0][0])
        chain[0] = (done[0],)
        return done + bufs[len(ms):]

    h0 = x.reshape(t, d)
    target = loss_target.reshape(t, d)
    gather_pass(0)
    n1 = seq(_rms_fwd, "mix0_norm", h0, mix_norm_e)
    W_in_e, small_all = gather_done(0)
    per_chip = [_unpack(small_all[k], [s.shape for s in small_shards]) for k in range(N_CHIPS)]
    conv_w_e_f = jnp.concatenate([p[0] for p in per_chip], axis=1)
    w_pool_f = jnp.concatenate([p[1] for p in per_chip], axis=1)
    mix_norm_o_f = jnp.concatenate([p[2] for p in per_chip], axis=1)
    conv_w_o_f = jnp.concatenate([p[3] for p in per_chip], axis=1)
    W_gate, W_up, W_down = [None, None], [None, None], [None, None]

    (u_e,) = mm("mix0_in", [(n1, W_in_e)], "nn", [F32], _ep_store, tm=1024, tn=1024, tk=2048)
    gather_pass(1)
    a2, cat = seq(_mixer_e_fwd, u_e, conv_w_e_f, conv_b_e, ln_g_e, ln_b_e, w_pool_f, pool_scale_e, seq_len, ts)
    (W_out_e,) = gather_done(1)
    (h1,) = mm("mix0_out", [(cat, W_out_e)], "nn", [F32], _ep_residual, extras=(h0,), tm=1024, tn=1024, tk=2048)
    gather_pass(2)
    n2 = seq(_rms_fwd, "ffn0_norm", h1, ffn_norm[0:1])
    W_gate[0], W_up[0] = gather_done(2)
    gt0, up0, act0 = mm("ffn0_gate_up", [(n2, W_gate[0]), (n2, W_up[0])], "nn", [BF16] * 3, _ep_swiglu,
                        acc_of=(0, 1), tm=1024, tn=512, tk=2048)
    gather_pass(3)
    (W_down[0],) = gather_done(3)
    gather_pass(4)
    h2, n3 = seq(_mm_rows, "ffn0_down", [(act0, W_down[0])], "nn", [F32, BF16], _ep_rows_residual_norm, extras=(h1,),
                 vecs=(mix_norm_o_f,), tm=256, tk=act0.shape[1], chunk_dots=False)
    W_in_o, W_out_o = gather_done(4)
    (u_o,) = mm("mix1_in", [(n3, W_in_o)], "nn", [BF16], _ep_store, tm=1024, tn=1024, tk=2048)
    gather_pass(5)
    y_o = seq(_mixer_o_fwd, u_o, conv_w_o_f, seq_len, ts)
    h3, n4 = seq(_mm_rows, "mix1_out", [(y_o, W_out_o)], "nn", [F32, BF16], _ep_rows_residual_norm, extras=(h2,),
                 vecs=(ffn_norm[1:2],), tm=512, tk=2048)
    gather_pass(6)
    W_gate[1], W_up[1] = gather_done(5)
    gt1, up1, act1 = mm("ffn1_gate_up", [(n4, W_gate[1]), (n4, W_up[1])], "nn", [BF16] * 3, _ep_swiglu,
                        acc_of=(0, 1), tm=1024, tn=512, tk=2048)
    (W_down[1],) = gather_done(6)
    dh4, dh4b, loss_cols, d_final_norm = seq(_mm_rows, "ffn1_down", [(act1, W_down[1])], "nn", [F32, BF16],
                                             _ep_rows_loss_head, extras=(h3, target), vecs=(final_norm.reshape(1, d),),
                                             n_sums=2, tm=256, tk=act1.shape[1], chunk_dots=False)
    loss = lax.psum(jnp.sum(loss_cols), AXES)

    in_flight = {}
    partials, scattered = [None] * len(mats), [None] * len(mats)

    def reduce_begin(tag, ms, grads):
        gm = [mats[m] for m in ms]
        grads, landed, sems, token = _exchange_start("exchange_start_" + tag, gm, grads)
        chain[0] = (token,)
        in_flight[tag] = (ms, gm, grads, landed, sems)

    def reduce_advance(tag):
        ms, gm, grads, landed, sems = in_flight[tag]
        grads, landed = _exchange_wait("exchange_wait_" + tag, gm, grads, landed, sems, chain[0][0])
        parts = [_add_halves("add_halves%d" % m, mats[m], g, l, core) for m, g, l in zip(ms, grads, landed)]
        parts, lands, sems, token = _scatter_start("scatter_start_" + tag, gm, parts)
        chain[0] = (token,)
        in_flight[tag] = (ms, gm, parts, lands, sems)

    def reduce_finish(tag):
        ms, gm, parts, lands, sems = in_flight[tag]
        parts, lands = _scatter_wait("scatter_wait_" + tag, gm, parts, lands, sems, chain[0][0])
        chain[0] = (lands[0],)
        for m, p, l in zip(ms, parts, lands):
            partials[m], scattered[m] = p, l

    xi, yi, ci = lax.axis_index("x"), lax.axis_index("y"), lax.axis_index("c")
    slots = jnp.stack([me_k, 2 * (1 - xi) + yi, 2 * xi + (1 - yi), 2 * (1 - xi) + (1 - yi), ci]).astype(jnp.int32)
    shards = {}

    def sum_into(name, m, layer=None, n_layers=1):
        if layer is None:
            shards[name] = _sum_chips("sum_chips%d" % m, mats[m], partials[m], scattered[m], slots)
        else:
            shards[name] = _sum_chips("sum_chips%d" % m, mats[m], partials[m], scattered[m], slots, layer=layer,
                                      stack=shards.get(name), n_layers=n_layers)

    sharing = {}

    def share_begin(tag, names, items):
        arrays, sems, token = _share_start("share_start_" + tag, mats, [shards[nm] for nm in names], items)
        chain[0] = (token,)
        sharing[tag] = (names, items, arrays, sems)

    def share_end(tag):
        names, items, arrays, sems = sharing[tag]
        arrays = _share_wait("share_wait_" + tag, mats, arrays, items, sems, chain[0][0])
        chain[0] = (arrays[0],)
        for nm, a in zip(names, arrays):
            shards[nm] = a

    def ffn_bwd(l, dhb, n, gt, up, act, mid=None):
        dgt, dup = mm("ffn%d_dact" % l, [(dhb, W_down[l])], "nt", [BF16, BF16], _ep_swiglu_bwd, extras=(gt, up),
                      tm=1024, tn=512, tk=2048, row_chunk=512)
        (dW_down,) = mm("ffn%d_dw_down" % l, [(act, dhb)], "tn", [BF16], _ep_store, tm=512, tn=1024, tk=4096)
        if mid is not None:
            mid()
        (dn,) = mm("ffn%d_dn" % l, [(dgt, W_gate[l]), (dup, W_up[l])], "nt", [BF16], _ep_store,
                   tm=512, tn=2048, tk=1408)
        (dW_gate,) = mm("ffn%d_dw_gate" % l, [(n, dgt)], "tn", [BF16], _ep_store, tm=1024, tn=512, tk=4096)
        (dW_up,) = mm("ffn%d_dw_up" % l, [(n, dup)], "tn", [BF16], _ep_store, tm=1024, tn=512, tk=4096)
        return dn, dW_gate, dW_up, dW_down

    dn4, dW_gate1, dW_up1, dW_down1 = ffn_bwd(1, dh4b, n4, gt1, up1, act1)
    reduce_begin("ffn1", (7, 8, 9), [dW_gate1, dW_up1, dW_down1])
    dh3, dh3b, d_ffn_norm1 = seq(_rms_bwd, "ffn1_norm_bwd", dn4, h3, ffn_norm[1:2], dh4)
    (dy_o,) = mm("mix1_dy", [(dh3b, W_out_o)], "nt", [BF16], _ep_store, tm=1024, tn=1024, tk=2048)
    reduce_advance("ffn1")
    (dW_out_o,) = mm("mix1_dw_out", [(y_o, dh3b)], "tn", [BF16], _ep_store, tm=1024, tn=1024, tk=4096)
    du_o, d_conv_w_o = seq(_mixer_o_bwd, dy_o, u_o, conv_w_o_f, seq_len, ts)
    (dW_in_o,) = mm("mix1_dw_in", [(n3, du_o)], "tn", [BF16], _ep_store, tm=1024, tn=1024, tk=4096)
    reduce_begin("mix1", (5, 6), [dW_in_o, dW_out_o])
    dh2, dh2b, d_mix_norm_o = seq(_mm_rows, "mix1_dn", [(du_o, W_in_o)], "nt", [F32, BF16], _ep_rows_norm_bwd,
                                  extras=(h2, dh3), vecs=(mix_norm_o_f,), n_sums=1, tm=256, tk=6144, chunk_dots=False)
    reduce_advance("mix1")

    def finish_layer1():
        reduce_finish("ffn1")
        reduce_finish("mix1")
        sum_into("w_in_o", 5)
        sum_into("w_out_o", 6)
        for nm, m in (("w_gate", 7), ("w_up", 8), ("w_down", 9)):
            sum_into(nm, m, layer=1, n_layers=2)
        share_begin("layer1", ["w_in_o", "w_out_o", "w_gate", "w_up", "w_down"],
                    [[(None, 5)], [(None, 6)], [(1, 7)], [(1, 8)], [(1, 9)]])

    dn2, dW_gate0, dW_up0, dW_down0 = ffn_bwd(0, dh2b, n2, gt0, up0, act0, mid=finish_layer1)
    reduce_begin("ffn0", (2, 3, 4), [dW_gate0, dW_up0, dW_down0])
    dh1, dh1b, d_ffn_norm0 = seq(_rms_bwd, "ffn0_norm_bwd", dn2, h1, ffn_norm[0:1], dh2)
    (dcat,) = mm("mix0_dcat", [(dh1b, W_out_e)], "nt", [F32], _ep_store, tm=1024, tn=1024, tk=2048)
    reduce_advance("ffn0")
    (dW_out_e,) = mm("mix0_dw_out", [(cat, dh1b)], "tn", [BF16], _ep_store, tm=1024, tn=1024, tk=4096)
    da2, d_ln_g, d_ln_b, d_conv_b = seq(_mixer_e_bwd_norm, dcat, a2, ln_g_e, ln_b_e, ts)
    du_e, d_conv_w_e, d_w_pool, d_pool_scale = seq(_mixer_e_bwd_mix, da2, dcat, u_e, conv_w_e_f, w_pool_f, pool_scale_e,
                                                   seq_len, ts)
    (dW_in_e,) = mm("mix0_dw_in", [(n1, du_e)], "tn", [BF16], _ep_store, tm=1024, tn=1024, tk=4096)
    reduce_begin("mix0", (0, 1) + pool_mats, [dW_in_e, dW_out_e] + [d_w_pool[g].astype(BF16) for g in range(n_pool)])
    dx, d_mix_norm_e = seq(_mm_rows, "mix0_dn", [(du_e, W_in_e)], "nt", [F32], _ep_rows_norm_bwd,
                           extras=(h0, dh1), vecs=(mix_norm_e,), n_sums=1, tm=256, tk=3072, chunk_dots=False)
    reduce_advance("mix0")

    d_ffn_norm = jnp.concatenate([d_ffn_norm0, d_ffn_norm1], axis=0)
    small_partials = [d_mix_norm_e, d_conv_w_e, d_conv_b, d_ln_g, d_ln_b, d_pool_scale, d_mix_norm_o, d_conv_w_o,
                      d_ffn_norm, d_final_norm]
    packed_grads, small_stack, small_sems, token = _small_start(_pack(small_partials), chain[0][0])
    chain[0] = (token,)
    share_end("layer1")
    reduce_finish("ffn0")
    for nm, m in (("w_gate", 2), ("w_up", 3), ("w_down", 4)):
        sum_into(nm, m, layer=0, n_layers=2)
    share_begin("layer0", ["w_gate", "w_up", "w_down"], [[(0, 2)], [(0, 3)], [(0, 4)]])
    grad = {"w_in_o": shards["w_in_o"][None], "w_out_o": shards["w_out_o"][None]}
    weights = dict(mix_norm_e=mix_norm_e, w_in_e=w_in_e, conv_w_e=conv_w_e, conv_b_e=conv_b_e, ln_g_e=ln_g_e, ln_b_e=ln_b_e,
                   w_pool_e=w_pool_e, pool_scale_e=pool_scale_e, w_out_e=w_out_e, mix_norm_o=mix_norm_o, w_in_o=w_in_o,
                   conv_w_o=conv_w_o, w_out_o=w_out_o, ffn_norm=ffn_norm, w_gate=w_gate, w_up=w_up, w_down=w_down,
                   final_norm=final_norm)
    mom1 = dict(mix_norm_e=m_mix_norm_e, w_in_e=m_w_in_e, conv_w_e=m_conv_w_e, conv_b_e=m_conv_b_e, ln_g_e=m_ln_g_e,
                ln_b_e=m_ln_b_e, w_pool_e=m_w_pool_e, pool_scale_e=m_pool_scale_e, w_out_e=m_w_out_e, mix_norm_o=m_mix_norm_o,
                w_in_o=m_w_in_o, conv_w_o=m_conv_w_o, w_out_o=m_w_out_o, ffn_norm=m_ffn_norm, w_gate=m_w_gate, w_up=m_w_up,
                w_down=m_w_down, final_norm=m_final_norm)
    mom2 = dict(mix_norm_e=v_mix_norm_e, w_in_e=v_w_in_e, conv_w_e=v_conv_w_e, conv_b_e=v_conv_b_e, ln_g_e=v_ln_g_e,
                ln_b_e=v_ln_b_e, w_pool_e=v_w_pool_e, pool_scale_e=v_pool_scale_e, w_out_e=v_w_out_e, mix_norm_o=v_mix_norm_o,
                w_in_o=v_w_in_o, conv_w_o=v_conv_w_o, w_out_o=v_w_out_o, ffn_norm=v_ffn_norm, w_gate=v_w_gate, w_up=v_w_up,
                w_down=v_w_down, final_norm=v_final_norm)
    names = list(weights)

    big = ("w_in_o", "w_out_o", "w_gate", "w_up", "w_down", "w_in_e", "w_out_e")
    delta, new_m, new_v = {}, {}, {}

    def update(nm):
        shape = weights[nm].shape
        rows = 1
        for dim in shape[:-1]:
            rows *= dim
        as2d = lambda a: a.reshape(rows, shape[-1])
        dl, mn, vn, gc = seq(_adamw, "adamw_" + nm, as2d(weights[nm]), as2d(grad[nm]), as2d(mom1[nm]), as2d(mom2[nm]),
                             copy_grad=True)
        delta[nm], new_m[nm], new_v[nm], grad[nm] = dl.reshape(shape), mn.reshape(shape), vn.reshape(shape), gc.reshape(shape)

    for nm in big[:2]:
        update(nm)
    share_end("layer0")
    for nm in big[2:5]:
        grad[nm] = shards[nm]
        update(nm)
    reduce_finish("mix0")
    sum_into("w_in_e", 0)
    sum_into("w_out_e", 1)
    for layer, m in enumerate(pool_mats):
        sum_into("w_pool_e", m, layer=layer, n_layers=n_pool)
    g_w_in_e, g_w_out_e, g_w_pool = _share_pieces("share_pieces_first", mats,
                                                  [shards["w_in_e"], shards["w_out_e"], shards["w_pool_e"]],
                                                  [(0,), (1,), pool_mats])
    grad["w_in_e"], grad["w_out_e"], grad["w_pool_e"] = g_w_in_e[None], g_w_out_e[None], g_w_pool[None]
    for nm in big[5:]:
        update(nm)

    packed_grads, small_stack = _small_wait(packed_grads, small_stack, small_sems, chain[0][0])
    me_dev = 4 * xi + 2 * yi + ci
    small_stack = jnp.where(lax.broadcasted_iota(jnp.int32, (N_DEV, 1, 1), 0) == me_dev, packed_grads[None], small_stack)
    small_sum = _unpack(_sum_devices(small_stack), [s.shape for s in small_partials])
    (g_mix_norm_e, g_conv_w_e_f, g_conv_b, g_ln_g, g_ln_b, g_pool_scale, g_mix_norm_o_f, g_conv_w_o_f,
     g_ffn_norm, g_final_norm) = small_sum

    def my_shard(full, axis):
        size = full.shape[axis] // N_CHIPS
        return lax.dynamic_slice_in_dim(full, me_k * size, size, axis)

    grad.update({
        "mix_norm_e": g_mix_norm_e, "conv_w_e": my_shard(g_conv_w_e_f, 1)[None], "conv_b_e": g_conv_b,
        "ln_g_e": g_ln_g, "ln_b_e": g_ln_b, "pool_scale_e": g_pool_scale,
        "mix_norm_o": my_shard(g_mix_norm_o_f, 1), "conv_w_o": my_shard(g_conv_w_o_f, 1)[None],
        "ffn_norm": g_ffn_norm, "final_norm": g_final_norm.reshape(final_norm.shape),
    })
    small = [nm for nm in names if nm not in big]
    shapes = [weights[nm].shape for nm in small]
    dl, mn, vn = _adamw("adamw_small", _pack([weights[nm] for nm in small]), _pack([grad[nm] for nm in small]),
                        _pack([mom1[nm] for nm in small]), _pack([mom2[nm] for nm in small]))
    for nm, a, b, c_ in zip(small, _unpack(dl, shapes), _unpack(mn, shapes), _unpack(vn, shapes)):
        delta[nm], new_m[nm], new_v[nm] = a, b, c_

    grad_x = dx.reshape(bsz, seq_len, d)
    return (loss, grad_x, *[grad[nm] for nm in names], *[delta[nm] for nm in names],
            *[new_m[nm] for nm in names], *[new_v[nm] for nm in names])
```

```python
import jax
import jax.numpy as jnp
from jax import lax
from jax.experimental import pallas as pl
from jax.experimental.pallas import tpu as pltpu

F32 = jnp.float32
BF16 = jnp.bfloat16
MESH_ID = pl.DeviceIdType.MESH
AXES = ("x", "y", "c")
N_CHIPS = 4
N_DEV = 8

EPS = 1e-6
POOL_WINDOWS = (2, 4, 8, 16)
ADAM_LR, ADAM_B1, ADAM_B2, ADAM_EPS, ADAM_WD, ADAM_STEP = 0.001, 0.9, 0.999, 1e-08, 0.01, 10

LANES = 128
CONV_HALO = 32
POOL_HALO = 16
SHORT_HALO = 16
V7X_VMEM_LIMIT = 56 * 1024 * 1024


def _cparams(*sem):
    return pltpu.CompilerParams(dimension_semantics=sem if sem else None, vmem_limit_bytes=V7X_VMEM_LIMIT)


def _pick(dim, prefs):
    for p in prefs:
        if p <= dim and dim % p == 0:
            return p
    return dim


def _sigmoid(x):
    return jax.nn.sigmoid(x)


_ANY = pl.BlockSpec(memory_space=pl.ANY)


def _behind(after, body, n_in):
    if not after:
        return body
    skip = len(after)

    def body_behind(*refs):
        return body(*refs[:n_in], *refs[n_in + skip:])

    return body_behind


_DOT_DIMS = {
    "nn": (((1,), (0,)), ((), ())),
    "nt": (((1,), (1,)), ((), ())),
    "tn": (((0,), (0,)), ((), ())),
}


def _mm(name, pairs, mode, out_dtypes, epilogue, extras=(), acc_of=None, tm=512, tn=512, tk=2048, row_chunk=0, after=()):
    a0, b0 = pairs[0]
    if mode == "nn":
        (m, k), n = a0.shape, b0.shape[1]
    elif mode == "nt":
        (m, k), n = a0.shape, b0.shape[0]
    else:
        (k, m), n = a0.shape, b0.shape[1]
    tm = _pick(m, (tm, 512, 256, 128, 64, 32, 16, 8))
    tn = _pick(n, (tn, 512, 256, 128))
    tk = _pick(k, (tk, 2048, 1024, 512, 256, 128))
    nk = k // tk
    n_pairs = len(pairs)
    acc_of = tuple(acc_of) if acc_of is not None else (0,) * n_pairs
    n_acc = max(acc_of) + 1
    n_ex, n_out = len(extras), len(out_dtypes)
    dims = _DOT_DIMS[mode]

    def body(*refs):
        a_refs = refs[:n_pairs]
        b_refs = refs[n_pairs:2 * n_pairs]
        e_refs = refs[2 * n_pairs:2 * n_pairs + n_ex]
        first_out = 2 * n_pairs + n_ex + len(after)
        o_refs = refs[first_out:first_out + n_out]
        acc_refs = refs[first_out + n_out:]

        def partial_sums(rows=None):
            sums = [None] * n_acc
            for p in range(n_pairs):
                a = a_refs[p][...] if rows is None else (a_refs[p][:, rows] if mode == "tn" else a_refs[p][rows, :])
                d = lax.dot_general(a, b_refs[p][...], dims, preferred_element_type=F32)
                sums[acc_of[p]] = d if sums[acc_of[p]] is None else sums[acc_of[p]] + d
            return sums

        if nk == 1 and row_chunk:
            for r0 in range(0, tm, row_chunk):
                rows = pl.ds(r0, row_chunk)
                epilogue(partial_sums(rows), [e.at[rows, :] for e in e_refs], [o.at[rows, :] for o in o_refs])
            return
        if nk == 1:
            epilogue(partial_sums(), e_refs, o_refs)
            return
        kk = pl.program_id(2)

        @pl.when(kk == 0)
        def _():
            for acc in acc_refs:
                acc[...] = jnp.zeros_like(acc)

        for acc, s in zip(acc_refs, partial_sums()):
            acc[...] += s

        @pl.when(kk == nk - 1)
        def _():
            epilogue([acc[...] for acc in acc_refs], e_refs, o_refs)

    if mode == "nn":
        a_spec = pl.BlockSpec((tm, tk), lambda i, j, kk: (i, kk))
        b_spec = pl.BlockSpec((tk, tn), lambda i, j, kk: (kk, j))
    elif mode == "nt":
        a_spec = pl.BlockSpec((tm, tk), lambda i, j, kk: (i, kk))
        b_spec = pl.BlockSpec((tn, tk), lambda i, j, kk: (j, kk))
    else:
        a_spec = pl.BlockSpec((tk, tm), lambda i, j, kk: (kk, i))
        b_spec = pl.BlockSpec((tk, tn), lambda i, j, kk: (kk, j))
    o_spec = pl.BlockSpec((tm, tn), lambda i, j, kk: (i, j))
    outs = pl.pallas_call(
        body,
        name=name,
        grid=(m // tm, n // tn, nk),
        in_specs=[a_spec] * n_pairs + [b_spec] * n_pairs + [o_spec] * n_ex
        + [pl.BlockSpec(memory_space=pl.ANY)] * len(after),
        out_specs=[o_spec] * n_out,
        out_shape=[jax.ShapeDtypeStruct((m, n), dt) for dt in out_dtypes],
        scratch_shapes=[pltpu.VMEM((tm, tn), F32) for _ in range(n_acc)] if nk > 1 else [],
        compiler_params=_cparams("parallel", "parallel", "arbitrary"),
    )(*[p[0] for p in pairs], *[p[1] for p in pairs], *extras, *after)
    return outs


def _mm_rows(name, pairs, mode, out_dtypes, epilogue, extras=(), vecs=(), n_sums=0, tm=512, tk=2048, row_chunk=128,
             chunk_dots=True, after=()):
    a0, b0 = pairs[0]
    (m, k), n = a0.shape, (b0.shape[1] if mode == "nn" else b0.shape[0])
    tm = _pick(m, (tm, 512, 256, 128, 64, 32, 16, 8))
    tk = _pick(k, (tk, 2048, 1024, 512, 256, 128))
    row_chunk = min(row_chunk, tm)
    nk = k // tk
    n_pairs, n_ex, n_vec, n_out = len(pairs), len(extras), len(vecs), len(out_dtypes)
    dims = _DOT_DIMS[mode]

    def body(*refs):
        a_refs = refs[:n_pairs]
        b_refs = refs[n_pairs:2 * n_pairs]
        e_refs = refs[2 * n_pairs:2 * n_pairs + n_ex]
        v_refs = refs[2 * n_pairs + n_ex:2 * n_pairs + n_ex + n_vec]
        first_out = 2 * n_pairs + n_ex + n_vec + len(after)
        o_refs = refs[first_out:first_out + n_out]
        s_refs = refs[first_out + n_out:first_out + n_out + n_sums]
        acc_refs = refs[first_out + n_out + n_sums:]
        i, kk = pl.program_id(0), pl.program_id(1)

        @pl.when((i == 0) & (kk == 0))
        def _():
            for s in s_refs:
                s[...] = jnp.zeros_like(s)

        def dots(rows):
            total = None
            for p in range(n_pairs):
                d = lax.dot_general(a_refs[p][rows, :], b_refs[p][...], dims, preferred_element_type=F32)
                total = d if total is None else total + d
            return total

        def finish(acc_of_rows):
            for r0 in range(0, tm, row_chunk):
                rows = pl.ds(r0, row_chunk)
                epilogue(acc_of_rows(rows), [e.at[rows, :] for e in e_refs], v_refs, [o.at[rows, :] for o in o_refs], s_refs)

        if nk == 1 and chunk_dots:
            finish(dots)
            return
        acc = acc_refs[0]
        if nk == 1:
            acc[...] = dots(slice(None))
            finish(lambda rows: acc[rows, :])
            return

        @pl.when(kk == 0)
        def _():
            acc[...] = jnp.zeros_like(acc)

        acc[...] += dots(slice(None))

        @pl.when(kk == nk - 1)
        def _():
            finish(lambda rows: acc[rows, :])

    a_spec = pl.BlockSpec((tm, tk), lambda i, kk: (i, kk))
    b_mode = dict(pipeline_mode=pl.Buffered(1)) if nk == 1 else {}
    b_spec = (pl.BlockSpec((tk, n), lambda i, kk: (kk, 0), **b_mode) if mode == "nn"
              else pl.BlockSpec((n, tk), lambda i, kk: (0, kk), **b_mode))
    row_spec = pl.BlockSpec((tm, n), lambda i, kk: (i, 0))
    vec_spec = pl.BlockSpec((1, n), lambda i, kk: (0, 0))
    return pl.pallas_call(
        body, name=name, grid=(m // tm, nk),
        in_specs=[a_spec] * n_pairs + [b_spec] * n_pairs + [row_spec] * n_ex + [vec_spec] * n_vec + [_ANY] * len(after),
        out_specs=[row_spec] * n_out + [vec_spec] * n_sums,
        out_shape=[jax.ShapeDtypeStruct((m, n), dt) for dt in out_dtypes] + [jax.ShapeDtypeStruct((1, n), F32)] * n_sums,
        scratch_shapes=[pltpu.VMEM((tm, n), F32)] if (nk > 1 or not chunk_dots) else [],
        compiler_params=_cparams("arbitrary", "arbitrary"),
    )(*[p[0] for p in pairs], *[p[1] for p in pairs], *extras, *vecs, *after)


def _ep_rows_residual_norm(acc, ex, vecs, outs, sums):
    h = ex[0][...] + acc
    outs[0][...] = h
    r = lax.rsqrt(jnp.mean(h * h, axis=-1, keepdims=True) + EPS)
    outs[1][...] = (h * r * vecs[0][...]).astype(BF16)


def _ep_rows_loss_head(acc, ex, vecs, outs, sums):
    x = ex[0][...] + acc
    gain = vecs[0][...]
    inv_d = 1.0 / x.shape[-1]
    r = lax.rsqrt(jnp.mean(x * x, axis=-1, keepdims=True) + EPS)
    xhat = x * r
    err = xhat * gain - ex[1][...]
    sums[0][...] += (0.5 * inv_d) * jnp.sum(err * err, axis=0, keepdims=True)
    dy = err * inv_d
    sums[1][...] += jnp.sum(dy * xhat, axis=0, keepdims=True)
    dxh = dy * gain
    dh = r * (dxh - xhat * jnp.mean(dxh * xhat, axis=-1, keepdims=True))
    outs[0][...] = dh
    outs[1][...] = dh.astype(BF16)


def _ep_rows_norm_bwd(dn, ex, vecs, outs, sums):
    x = ex[0][...]
    r = lax.rsqrt(jnp.mean(x * x, axis=-1, keepdims=True) + EPS)
    xhat = x * r
    sums[0][...] += jnp.sum(dn * xhat, axis=0, keepdims=True)
    dxh = dn * vecs[0][...]
    dh = ex[1][...] + r * (dxh - xhat * jnp.mean(dxh * xhat, axis=-1, keepdims=True))
    outs[0][...] = dh
    if len(outs) > 1:
        outs[1][...] = dh.astype(BF16)


def _ep_store(accs, ex, outs):
    outs[0][...] = accs[0].astype(outs[0].dtype)


def _ep_residual(accs, ex, outs):
    outs[0][...] = ex[0][...] + accs[0]


def _ep_swiglu(accs, ex, outs):
    g, u = accs
    s = _sigmoid(g)
    gs = g * s
    outs[0][...] = (u * (s * (1.0 + g * (1.0 - s)))).astype(BF16)
    outs[1][...] = gs.astype(BF16)
    outs[2][...] = (gs * u).astype(BF16)


def _ep_swiglu_bwd(accs, ex, outs):
    d = accs[0]
    outs[0][...] = (d * ex[0][...].astype(F32)).astype(BF16)
    outs[1][...] = (d * ex[1][...].astype(F32)).astype(BF16)


def _rms_fwd(name, h, g, after=()):
    t, d = h.shape
    tr = _pick(t, (256, 128, 64, 32, 16, 8))

    def body(h_ref, g_ref, *rest):
        o_ref = rest[-1]
        x = h_ref[...]
        r = lax.rsqrt(jnp.mean(x * x, axis=-1, keepdims=True) + EPS)
        o_ref[...] = (x * r * g_ref[...]).astype(BF16)

    return pl.pallas_call(
        body, name=name, grid=(t // tr,),
        in_specs=[pl.BlockSpec((tr, d), lambda i: (i, 0)), pl.BlockSpec((1, d), lambda i: (0, 0))]
        + [pl.BlockSpec(memory_space=pl.ANY)] * len(after),
        out_specs=pl.BlockSpec((tr, d), lambda i: (i, 0)),
        out_shape=jax.ShapeDtypeStruct((t, d), BF16),
        compiler_params=_cparams("parallel"),
    )(h, g, *after)


def _rms_bwd(name, dn, h, g, dres, after=()):
    t, d = h.shape
    tr = _pick(t, (256, 128, 64, 32, 16, 8))

    def body(dn_ref, h_ref, g_ref, dres_ref, dh_ref, dhb_ref, dg_ref):
        @pl.when(pl.program_id(0) == 0)
        def _():
            dg_ref[...] = jnp.zeros_like(dg_ref)

        chunk = min(64, tr)
        for r0 in range(0, tr, chunk):
            rows = pl.ds(r0, chunk)
            x = h_ref[rows, :]
            r = lax.rsqrt(jnp.mean(x * x, axis=-1, keepdims=True) + EPS)
            xhat = x * r
            dnv = dn_ref[rows, :].astype(F32)
            dg_ref[...] += jnp.sum(dnv * xhat, axis=0, keepdims=True)
            dxh = dnv * g_ref[...]
            dh = dres_ref[rows, :] + r * (dxh - xhat * jnp.mean(dxh * xhat, axis=-1, keepdims=True))
            dh_ref[rows, :] = dh
            dhb_ref[rows, :] = dh.astype(BF16)

    row = pl.BlockSpec((tr, d), lambda i: (i, 0))
    vec = pl.BlockSpec((1, d), lambda i: (0, 0))
    return pl.pallas_call(
        _behind(after, body, 4), name=name, grid=(t // tr,),
        in_specs=[row, row, vec, row] + [_ANY] * len(after),
        out_specs=[row, row, vec],
        out_shape=[jax.ShapeDtypeStruct((t, d), F32), jax.ShapeDtypeStruct((t, d), BF16),
                   jax.ShapeDtypeStruct((1, d), F32)],
        compiler_params=_cparams("arbitrary"),
    )(dn, h, g, dres, *after)


def _cur(ts, width, col):
    return pl.BlockSpec((ts, width), lambda i: (i, col))


def _prev_halo(ts, halo, width, col):
    per = ts // halo
    return pl.BlockSpec((halo, width), lambda i: (jnp.maximum(i * per - 1, 0), col))


def _next_halo(ts, halo, width, col, n_rows):
    per = ts // halo
    last = n_rows // halo - 1
    return pl.BlockSpec((halo, width), lambda i: (jnp.minimum((i + 1) * per, last), col))


def _full(shape):
    nd = len(shape)
    return pl.BlockSpec(shape, lambda i: (0,) * nd)


def _shift_down(x, n):
    return x if n == 0 else pltpu.roll(x, n, 0)


def _shift_up(x, n):
    return x if n == 0 else pltpu.roll(x, x.shape[0] - n, 0)


CONV_ROWS = 32


def _conv_block_shape(channels, ts):
    return min(CONV_ROWS, ts), min(LANES, channels)


SUBLANES = 8


def _fill_shifted(rot_ref, ext):
    rot_ref[0] = ext
    for r in range(1, SUBLANES):
        rot_ref[r] = _shift_up(ext, r)


def _window(rot_ref, first, rows, c0, cw):
    r = first % SUBLANES
    return rot_ref[r, first - r:first - r + rows, c0:c0 + cw]


def _causal_taps(rot_ref, w_ref, halo, taps, r0, c0, rows, cw):
    acc = jnp.zeros((rows, cw), F32)
    for k in range(taps):
        acc = acc + w_ref[k:k + 1, c0:c0 + cw] * _window(rot_ref, halo + r0 - (taps - 1 - k), rows, c0, cw)
    return acc


def _pool_counts(i, ns, ts, w):
    pos = (i % ns) * ts + lax.broadcasted_iota(jnp.int32, (ts, 1), 0)
    return jnp.minimum(pos + 1, w).astype(F32)


def _pooled(cur, prev_tail, w, cnt):
    s = jnp.concatenate([prev_tail, cur], axis=0)
    d = 1
    while d < w:
        s = s + _shift_down(s, d)
        d *= 2
    return s[POOL_HALO:, :] / cnt - cur


def _mixer_e_fwd(u, conv_w, conv_b, ln_g, ln_b, w_pool, scale, seq, ts, after=()):
    t = u.shape[0]
    dc = conv_b.shape[1]
    ng, pg = w_pool.shape[0], w_pool.shape[1]
    taps = conv_w.shape[0]
    ns = seq // ts

    def body(val_ref, gate_ref, b_ref, pval_ref, pgate_ref, pb_ref, cw_ref, cb_ref, g_ref, be_ref, wp_ref, sc_ref,
             a2_ref, cat_ref, rot_ref):
        i = pl.program_id(0)
        keep_prev = jnp.where(i % ns == 0, 0.0, 1.0)
        a1 = val_ref[...] * _sigmoid(gate_ref[...])
        pa1 = pval_ref[...] * _sigmoid(pgate_ref[...]) * keep_prev
        _fill_shifted(rot_ref, jnp.concatenate([pa1, a1], axis=0))
        rows, cw = _conv_block_shape(dc, ts)
        for c0 in range(0, dc, cw):
            for r0 in range(0, ts, rows):
                acc = _causal_taps(rot_ref, cw_ref, CONV_HALO, taps, r0, c0, rows, cw)
                a2_ref[r0:r0 + rows, c0:c0 + cw] = acc + cb_ref[:, c0:c0 + cw]
        a2 = a2_ref[...]
        mu = jnp.mean(a2, axis=-1, keepdims=True)
        xc = a2 - mu
        rstd = lax.rsqrt(jnp.mean(xc * xc, axis=-1, keepdims=True) + EPS)
        a3 = xc * rstd * g_ref[...] + be_ref[...]
        cat_ref[:, 0:dc] = (a3 * _sigmoid(a3)).astype(BF16)
        for g in range(ng):
            lo, hi = g * pg, (g + 1) * pg
            w = POOL_WINDOWS[g]
            p = _pooled(b_ref[:, lo:hi], pb_ref[:, lo:hi] * keep_prev, w, _pool_counts(i, ns, ts, w))
            q = jnp.dot(p.astype(BF16), wp_ref[g].astype(BF16), preferred_element_type=F32)
            cat_ref[:, dc + lo:dc + hi] = (q * sc_ref[:, lo:hi]).astype(BF16)

    return pl.pallas_call(
        _behind(after, body, 12), name="mixer_e_fwd", grid=(t // ts,),
        in_specs=[_cur(ts, dc, 0), _cur(ts, dc, 1), _cur(ts, dc, 2),
                  _prev_halo(ts, CONV_HALO, dc, 0), _prev_halo(ts, CONV_HALO, dc, 1), _prev_halo(ts, POOL_HALO, dc, 2),
                  _full(conv_w.shape), _full(conv_b.shape), _full(ln_g.shape), _full(ln_b.shape),
                  _full(w_pool.shape), _full(scale.shape)] + [_ANY] * len(after),
        out_specs=[_cur(ts, dc, 0), _cur(ts, 2 * dc, 0)],
        out_shape=[jax.ShapeDtypeStruct((t, dc), F32), jax.ShapeDtypeStruct((t, 2 * dc), BF16)],
        scratch_shapes=[pltpu.VMEM((SUBLANES, CONV_HALO + ts, dc), F32)],
        compiler_params=_cparams("parallel"),
    )(u, u, u, u, u, u, conv_w, conv_b, ln_g, ln_b, w_pool, scale, *after)


def _mixer_e_bwd_norm(dcat, a2, ln_g, ln_b, ts, after=()):
    t, dc = a2.shape

    def body(d_ref, a2_ref, g_ref, be_ref, da2_ref, dg_ref, db_ref, dcb_ref):
        x = a2_ref[...]
        gv = g_ref[...]
        mu = jnp.mean(x, axis=-1, keepdims=True)
        xc = x - mu
        rstd = lax.rsqrt(jnp.mean(xc * xc, axis=-1, keepdims=True) + EPS)
        xhat = xc * rstd
        a3 = xhat * gv + be_ref[...]
        sg = _sigmoid(a3)
        da3 = d_ref[...] * (sg * (1.0 + a3 * (1.0 - sg)))
        dxh = da3 * gv
        da2 = rstd * (dxh - jnp.mean(dxh, axis=-1, keepdims=True)
                      - xhat * jnp.mean(dxh * xhat, axis=-1, keepdims=True))
        da2_ref[...] = da2

        @pl.when(pl.program_id(0) == 0)
        def _():
            dg_ref[...] = jnp.zeros_like(dg_ref)
            db_ref[...] = jnp.zeros_like(db_ref)
            dcb_ref[...] = jnp.zeros_like(dcb_ref)

        dg_ref[...] += jnp.sum(da3 * xhat, axis=0, keepdims=True)
        db_ref[...] += jnp.sum(da3, axis=0, keepdims=True)
        dcb_ref[...] += jnp.sum(da2, axis=0, keepdims=True)

    vec = _full((1, dc))
    return pl.pallas_call(
        _behind(after, body, 4), name="mixer_e_bwd_norm", grid=(t // ts,),
        in_specs=[_cur(ts, dc, 0), _cur(ts, dc, 0), vec, vec] + [_ANY] * len(after),
        out_specs=[_cur(ts, dc, 0), vec, vec, vec],
        out_shape=[jax.ShapeDtypeStruct((t, dc), F32)] + [jax.ShapeDtypeStruct((1, dc), F32)] * 3,
        compiler_params=_cparams("arbitrary"),
    )(dcat, a2, ln_g, ln_b, *after)


def _mixer_e_bwd_mix(da2, dcat, u, conv_w, w_pool, scale, seq, ts, after=()):
    t, dc = da2.shape
    ng, pg = w_pool.shape[0], w_pool.shape[1]
    taps = conv_w.shape[0]
    ns = seq // ts

    def body(da2_ref, nda2_ref, dp_ref, ndp_ref, val_ref, gate_ref, b_ref, pval_ref, pgate_ref, pb_ref,
             cw_ref, wp_ref, sc_ref, du_ref, dcw_ref, dwp_ref, dsc_ref, rota_ref, rotd_ref):
        i = pl.program_id(0)
        keep_prev = jnp.where(i % ns == 0, 0.0, 1.0)
        keep_next = jnp.where(i % ns == ns - 1, 0.0, 1.0)

        @pl.when(i == 0)
        def _():
            dcw_ref[...] = jnp.zeros_like(dcw_ref)
            dwp_ref[...] = jnp.zeros_like(dwp_ref)
            dsc_ref[...] = jnp.zeros_like(dsc_ref)

        val = val_ref[...]
        sg = _sigmoid(gate_ref[...])
        a1 = val * sg
        pa1 = pval_ref[...] * _sigmoid(pgate_ref[...]) * keep_prev
        _fill_shifted(rota_ref, jnp.concatenate([pa1, a1], axis=0))
        _fill_shifted(rotd_ref, jnp.concatenate([da2_ref[...], nda2_ref[...] * keep_next], axis=0))
        rows, cw = _conv_block_shape(dc, ts)
        for c0 in range(0, dc, cw):
            lanes = slice(c0, c0 + cw)
            dw = [jnp.zeros((SUBLANES, cw), F32)] * taps
            for r0 in range(0, ts, rows):
                blk = slice(r0, r0 + rows)
                d_blk = da2_ref[blk, lanes]
                da1 = jnp.zeros((rows, cw), F32)
                for k in range(taps):
                    sh = taps - 1 - k
                    prod = d_blk * _window(rota_ref, CONV_HALO + r0 - sh, rows, c0, cw)
                    for f in range(0, rows, SUBLANES):
                        dw[k] = dw[k] + prod[f:f + SUBLANES, :]
                    da1 = da1 + cw_ref[k:k + 1, lanes] * _window(rotd_ref, r0 + sh, rows, c0, cw)
                sg_b = sg[blk, lanes]
                du_ref[blk, lanes] = (da1 * sg_b).astype(BF16)
                du_ref[blk, dc + c0:dc + c0 + cw] = (da1 * a1[blk, lanes] * (1.0 - sg_b)).astype(BF16)
            for k in range(taps):
                dcw_ref[k:k + 1, lanes] += jnp.sum(dw[k], axis=0, keepdims=True)

        for g in range(ng):
            lo, hi = g * pg, (g + 1) * pg
            w = POOL_WINDOWS[g]
            cnt = _pool_counts(i, ns, ts, w)
            wpb = wp_ref[g].astype(BF16)
            sc = sc_ref[:, lo:hi]
            p = _pooled(b_ref[:, lo:hi], pb_ref[:, lo:hi] * keep_prev, w, cnt)
            pb16 = p.astype(BF16)
            q = jnp.dot(pb16, wpb, preferred_element_type=F32)
            dout = dp_ref[:, lo:hi]
            dsc_ref[:, lo:hi] += jnp.sum(dout * q, axis=0, keepdims=True)
            dq = (dout * sc).astype(BF16)
            dwp_ref[g] += lax.dot_general(pb16, dq, _DOT_DIMS["tn"], preferred_element_type=F32)
            dpool = lax.dot_general(dq, wpb, _DOT_DIMS["nt"], preferred_element_type=F32)
            ndq = (ndp_ref[:, lo:hi] * sc * keep_next).astype(BF16)
            ndpool = lax.dot_general(ndq, wpb, _DOT_DIMS["nt"], preferred_element_type=F32)
            s = jnp.concatenate([dpool / cnt, ndpool * (1.0 / w)], axis=0)
            d = 1
            while d < w:
                s = s + _shift_up(s, d)
                d *= 2
            du_ref[:, 2 * dc + lo:2 * dc + hi] = (s[:ts, :] - dpool).astype(BF16)

    return pl.pallas_call(
        _behind(after, body, 13), name="mixer_e_bwd_mix", grid=(t // ts,),
        in_specs=[_cur(ts, dc, 0), _next_halo(ts, CONV_HALO, dc, 0, t),
                  _cur(ts, dc, 1), _next_halo(ts, POOL_HALO, dc, 1, t),
                  _cur(ts, dc, 0), _cur(ts, dc, 1), _cur(ts, dc, 2),
                  _prev_halo(ts, CONV_HALO, dc, 0), _prev_halo(ts, CONV_HALO, dc, 1), _prev_halo(ts, POOL_HALO, dc, 2),
                  _full(conv_w.shape), _full(w_pool.shape), _full(scale.shape)] + [_ANY] * len(after),
        out_specs=[_cur(ts, 3 * dc, 0), _full(conv_w.shape), _full(w_pool.shape), _full(scale.shape)],
        out_shape=[jax.ShapeDtypeStruct((t, 3 * dc), BF16), jax.ShapeDtypeStruct(conv_w.shape, F32),
                   jax.ShapeDtypeStruct(w_pool.shape, F32), jax.ShapeDtypeStruct(scale.shape, F32)],
        scratch_shapes=[pltpu.VMEM((SUBLANES, ts + CONV_HALO, dc), F32)] * 2,
        compiler_params=_cparams("arbitrary"),
    )(da2, da2, dcat, dcat, u, u, u, u, u, u, conv_w, w_pool, scale, *after)


def _mixer_o_fwd(u, conv_w, seq, ts, after=()):
    t = u.shape[0]
    d = conv_w.shape[1]
    taps = conv_w.shape[0]
    ns = seq // ts

    def body(gb_ref, gc_ref, v_ref, pgc_ref, pv_ref, cw_ref, y_ref):
        keep_prev = jnp.where(pl.program_id(0) % ns == 0, 0.0, 1.0)
        f32 = lambda ref: ref[...].astype(F32)
        ext = jnp.concatenate([f32(pgc_ref) * f32(pv_ref) * keep_prev, f32(gc_ref) * f32(v_ref)], axis=0)
        cc = jnp.zeros_like(ext)
        for k in range(taps):
            cc = cc + cw_ref[k:k + 1, :] * _shift_down(ext, taps - 1 - k)
        y_ref[...] = (f32(gb_ref) * cc[SHORT_HALO:, :]).astype(BF16)

    return pl.pallas_call(
        _behind(after, body, 6), name="mixer_o_fwd", grid=(t // ts,),
        in_specs=[_cur(ts, d, 0), _cur(ts, d, 1), _cur(ts, d, 2),
                  _prev_halo(ts, SHORT_HALO, d, 1), _prev_halo(ts, SHORT_HALO, d, 2), _full(conv_w.shape)]
        + [_ANY] * len(after),
        out_specs=_cur(ts, d, 0),
        out_shape=jax.ShapeDtypeStruct((t, d), BF16),
        compiler_params=_cparams("parallel"),
    )(u, u, u, u, u, conv_w, *after)


def _mixer_o_bwd(dy, u, conv_w, seq, ts, after=()):
    t = u.shape[0]
    d = conv_w.shape[1]
    taps = conv_w.shape[0]
    ns = seq // ts

    def body(dy_ref, ndy_ref, gb_ref, gc_ref, v_ref, pgc_ref, pv_ref, ngb_ref, cw_ref, du_ref, dcw_ref):
        i = pl.program_id(0)
        keep_prev = jnp.where(i % ns == 0, 0.0, 1.0)
        keep_next = jnp.where(i % ns == ns - 1, 0.0, 1.0)

        @pl.when(i == 0)
        def _():
            dcw_ref[...] = jnp.zeros_like(dcw_ref)

        f32 = lambda ref: ref[...].astype(F32)
        gb, gc, v, dyv = f32(gb_ref), f32(gc_ref), f32(v_ref), f32(dy_ref)
        ext = jnp.concatenate([f32(pgc_ref) * f32(pv_ref) * keep_prev, gc * v], axis=0)
        dcc = dyv * gb
        ext_d = jnp.concatenate([dcc, f32(ndy_ref) * f32(ngb_ref) * keep_next], axis=0)
        cc = jnp.zeros_like(ext)
        dcv = jnp.zeros_like(ext_d)
        for k in range(taps):
            sh = taps - 1 - k
            shifted = _shift_down(ext, sh)
            cc = cc + cw_ref[k:k + 1, :] * shifted
            dcw_ref[k:k + 1, :] += jnp.sum(dcc * shifted[SHORT_HALO:, :], axis=0, keepdims=True)
            dcv = dcv + cw_ref[k:k + 1, :] * _shift_up(ext_d, sh)
        dcv = dcv[:ts, :]
        du_ref[:, 0:d] = (dyv * cc[SHORT_HALO:, :]).astype(BF16)
        du_ref[:, d:2 * d] = (dcv * v).astype(BF16)
        du_ref[:, 2 * d:3 * d] = (dcv * gc).astype(BF16)

    return pl.pallas_call(
        _behind(after, body, 9), name="mixer_o_bwd", grid=(t // ts,),
        in_specs=[_cur(ts, d, 0), _next_halo(ts, SHORT_HALO, d, 0, t),
                  _cur(ts, d, 0), _cur(ts, d, 1), _cur(ts, d, 2),
                  _prev_halo(ts, SHORT_HALO, d, 1), _prev_halo(ts, SHORT_HALO, d, 2),
                  _next_halo(ts, SHORT_HALO, d, 0, t), _full(conv_w.shape)] + [_ANY] * len(after),
        out_specs=[_cur(ts, 3 * d, 0), _full(conv_w.shape)],
        out_shape=[jax.ShapeDtypeStruct((t, 3 * d), BF16), jax.ShapeDtypeStruct(conv_w.shape, F32)],
        compiler_params=_cparams("arbitrary"),
    )(dy, dy, u, u, u, u, u, u, conv_w, *after)


def _cast_into_full(name, mat, w, layer, chip, after=()):
    tr = _pick(mat.sr, (512, 256, 128, 64, 32, 16))
    per = mat.sr // tr

    def body(chip_ref, w_ref, *rest):
        o_ref = rest[-1]
        o_ref[...] = w_ref[...].astype(BF16)

    if mat.kind == "col":
        o_spec = pl.BlockSpec((tr, mat.sc), lambda i, chip_ref: (i, chip_ref[0]))
    else:
        o_spec = pl.BlockSpec((tr, mat.sc), lambda i, chip_ref: (chip_ref[0] * per + i, 0))
    return pl.pallas_call(
        body, name=name,
        grid_spec=pltpu.PrefetchScalarGridSpec(
            num_scalar_prefetch=1, grid=(per,),
            in_specs=[pl.BlockSpec((None, tr, mat.sc), lambda i, chip_ref: (layer, i, 0))] + [_ANY] * len(after),
            out_specs=o_spec),
        out_shape=jax.ShapeDtypeStruct(mat.full_shape, BF16),
        compiler_params=_cparams("parallel"),
    )(chip, w, *after)


def _adamw(name, w, g, m, v, copy_grad=False, after=()):
    r, c = w.shape
    tr = _pick(r, (256, 128, 64, 32, 16, 8)) if c > 1024 else _pick(r, (512, 256, 128, 64, 32, 16, 8))
    bc1 = 1.0 - ADAM_B1 ** ADAM_STEP
    bc2 = 1.0 - ADAM_B2 ** ADAM_STEP

    def body(w_ref, g_ref, m_ref, v_ref, d_ref, mo_ref, vo_ref, *rest):
        gv = g_ref[...]
        mn = ADAM_B1 * m_ref[...] + (1.0 - ADAM_B1) * gv
        vn = ADAM_B2 * v_ref[...] + (1.0 - ADAM_B2) * (gv * gv)
        mo_ref[...] = mn
        vo_ref[...] = vn
        d_ref[...] = -ADAM_LR * ((mn * (1.0 / bc1)) / (jnp.sqrt(vn * (1.0 / bc2)) + ADAM_EPS) + ADAM_WD * w_ref[...])

        if copy_grad:
            rest[0][...] = gv

    spec = pl.BlockSpec((tr, c), lambda i: (i, 0))
    n_out = 4 if copy_grad else 3
    return pl.pallas_call(_behind(after, body, 4), name=name, grid=(r // tr,),
                          in_specs=[spec] * 4 + [_ANY] * len(after), out_specs=[spec] * n_out,
                          out_shape=[jax.ShapeDtypeStruct((r, c), F32)] * n_out,
                          compiler_params=_cparams("parallel"))(w, g, m, v, *after)


def _aligned(offset, multiple):
    return offset if isinstance(offset, int) else pl.multiple_of(offset, multiple)


class _Mat:
    def __init__(self, kind, shard_shape):
        self.kind = kind
        self.sr, self.sc = shard_shape
        self.full_shape = (self.sr, self.sc * N_CHIPS) if kind == "col" else (self.sr * N_CHIPS, self.sc)
        self.pr, self.pc = self.sr // 2, self.sc

    def piece(self, ref, k, h):
        if self.kind == "col":
            return ref.at[pl.ds(_aligned(h * self.pr, 16), self.pr), pl.ds(_aligned(k * self.sc, LANES), self.sc)]
        return ref.at[pl.ds(_aligned(k * self.sr + h * self.pr, 16), self.pr), :]

    def shard(self, ref, k):
        if self.kind == "col":
            return ref.at[:, pl.ds(_aligned(k * self.sc, LANES), self.sc)]
        return ref.at[pl.ds(_aligned(k * self.sr, 16), self.sr), :]

    def half(self, ref, h):
        return ref.at[pl.ds(_aligned(h * self.pr, 16), self.pr), :]


def _place():
    x, y, c = lax.axis_index("x"), lax.axis_index("y"), lax.axis_index("c")
    others = [(1 - x, y), (x, 1 - y), (1 - x, 1 - y)]
    return x, y, c, others


_HBM = pl.BlockSpec(memory_space=pltpu.HBM)
_SEM = pl.BlockSpec(memory_space=pltpu.SEMAPHORE)
_TOKEN = jax.ShapeDtypeStruct((8, LANES), F32)
_TOKEN_SPEC = pl.BlockSpec(memory_space=pltpu.VMEM)


def _split_params():
    return pltpu.CompilerParams(has_side_effects=pltpu.SideEffectType.DATAFLOW_SIDE_EFFECTING)


def _in_hbm(a):
    return pltpu.with_memory_space_constraint(a, pltpu.HBM)


def _copy_to(src, dst, send_sem, recv_sem, to):
    return pltpu.make_async_remote_copy(src_ref=src, dst_ref=dst, send_sem=send_sem, recv_sem=recv_sem,
                                        device_id=to, device_id_type=MESH_ID)


def _place_small(packed, chip):
    rows, cols = packed.shape

    def body(chip_ref, p_ref, o_ref):
        o_ref[...] = p_ref[...]

    return pl.pallas_call(
        body, name="place_small",
        grid_spec=pltpu.PrefetchScalarGridSpec(
            num_scalar_prefetch=1, grid=(1,),
            in_specs=[pl.BlockSpec((rows, cols), lambda i, chip_ref: (0, 0))],
            out_specs=pl.BlockSpec((None, rows, cols), lambda i, chip_ref: (chip_ref[0], 0, 0))),
        out_shape=jax.ShapeDtypeStruct((N_CHIPS, rows, cols), F32),
        compiler_params=_cparams("arbitrary"),
    )(chip, packed)


def _gather_start(name, gmats, gfulls, small_all=None):
    n = len(gmats)
    n_in = n + (1 if small_all is not None else 0)

    def body(*refs):
        full_refs = refs[:n]
        outs = refs[n_in:]
        send_sem, recv_sem, token = outs[n_in], outs[n_in + 1], outs[n_in + 2]
        x, y, c, others = _place()
        me_k = 2 * x + y
        if small_all is not None:
            mine = refs[n].at[me_k]
            for ox, oy in others:
                _copy_to(mine, mine, send_sem, recv_sem, (ox, oy, c)).start()
        for m in range(n):
            mine = gmats[m].piece(full_refs[m], me_k, c)
            for ox, oy in others:
                _copy_to(mine, mine, send_sem, recv_sem, (ox, oy, c)).start()
        token[...] = jnp.zeros_like(token)

    operands = [_in_hbm(f) for f in gfulls] + ([_in_hbm(small_all)] if small_all is not None else [])
    outs = pl.pallas_call(
        body, name=name,
        in_specs=[_HBM] * n_in,
        out_specs=[_HBM] * n_in + [_SEM, _SEM, _TOKEN_SPEC],
        out_shape=[pltpu.HBM(a.shape, a.dtype) for a in operands] + [pltpu.SemaphoreType.DMA(())] * 2 + [_TOKEN],
        input_output_aliases={m: m for m in range(n_in)},
        compiler_params=_split_params(),
    )(*operands)
    return list(outs[:n]), (outs[n] if small_all is not None else None), (outs[n_in], outs[n_in + 1]), outs[n_in + 2]


def _gather_pass(name, gmats, gfulls, small_all, sems, after):
    k = len(gmats)
    n_buf = k + (1 if small_all is not None else 0)

    def body(*refs):
        bufs = refs[:n_buf]
        send_sem, recv_sem = refs[n_buf], refs[n_buf + 1]
        outs = refs[n_buf + 3:]
        fsend, frecv, token = outs[n_buf], outs[n_buf + 1], outs[n_buf + 2]
        x, y, c, others = _place()
        me_k = 2 * x + y
        sibling = (x, y, 1 - c)
        for m in range(k):
            for ox, oy in others:
                got = gmats[m].piece(bufs[m], 2 * ox + oy, c)
                _copy_to(got, got, send_sem, recv_sem, sibling).wait_recv()
        if small_all is not None:
            for ox, oy in others:
                got = bufs[k].at[2 * ox + oy]
                _copy_to(got, got, send_sem, recv_sem, sibling).wait_recv()
        for m in range(k):
            mine = gmats[m].piece(bufs[m], me_k, c)
            for _ in others:
                _copy_to(mine, mine, send_sem, recv_sem, sibling).wait_send()
        if small_all is not None:
            for _ in others:
                _copy_to(bufs[k].at[me_k], bufs[k].at[me_k], send_sem, recv_sem, sibling).wait_send()
        for m in range(k):
            for ox, oy in others:
                got = gmats[m].piece(bufs[m], 2 * ox + oy, c)
                _copy_to(got, got, fsend, frecv, sibling).start()
        token[...] = jnp.zeros_like(token)

    operands = [_in_hbm(f) for f in gfulls] + ([_in_hbm(small_all)] if small_all is not None else [])
    outs = pl.pallas_call(
        body, name=name,
        in_specs=[_HBM] * n_buf + [_SEM, _SEM, _ANY],
        out_specs=[_HBM] * n_buf + [_SEM, _SEM, _TOKEN_SPEC],
        out_shape=[pltpu.HBM(a.shape, a.dtype) for a in operands] + [pltpu.SemaphoreType.DMA(())] * 2 + [_TOKEN],
        input_output_aliases={i: i for i in range(n_buf)},
        compiler_params=_split_params(),
    )(*operands, sems[0], sems[1], after)
    return list(outs[:n_buf]), (outs[n_buf], outs[n_buf + 1]), outs[n_buf + 2]


def _gather_done(name, gmats, gfulls, sems, after):
    k = len(gmats)

    def body(*refs):
        bufs = refs[:k]
        send_sem, recv_sem = refs[k], refs[k + 1]
        x, y, c, others = _place()
        sibling = (x, y, 1 - c)
        for m in range(k):
            for ox, oy in others:
                got = gmats[m].piece(bufs[m], 2 * ox + oy, 1 - c)
                _copy_to(got, got, send_sem, recv_sem, sibling).wait_recv()
        for m in range(k):
            for ox, oy in others:
                sent = gmats[m].piece(bufs[m], 2 * ox + oy, c)
                _copy_to(sent, sent, send_sem, recv_sem, sibling).wait_send()

    outs = pl.pallas_call(
        body, name=name,
        in_specs=[_HBM] * k + [_SEM, _SEM, _ANY], out_specs=[_HBM] * k,
        out_shape=[pltpu.HBM(a.shape, a.dtype) for a in gfulls],
        input_output_aliases={i: i for i in range(k)},
        compiler_params=_split_params(),
    )(*[_in_hbm(f) for f in gfulls], sems[0], sems[1], after)
    return list(outs)


_FLIPS = [(fx, fy, fc) for fx in (0, 1) for fy in (0, 1) for fc in (0, 1) if (fx, fy, fc) != (0, 0, 0)]


def _small_start(packed, after):
    def body(small_ref, after_ref, small_thru, land_ref, send_sem, recv_sem, token):
        x, y, c, _ = _place()
        me = 4 * x + 2 * y + c
        for fx, fy, fc in _FLIPS:
            _copy_to(small_ref, land_ref.at[me], send_sem, recv_sem, (x ^ fx, y ^ fy, c ^ fc)).start()
        token[...] = jnp.zeros_like(token)

    outs = pl.pallas_call(
        body, name="small_grads_start",
        in_specs=[_HBM, _ANY], out_specs=[_HBM, _HBM, _SEM, _SEM, _TOKEN_SPEC],
        out_shape=[pltpu.HBM(packed.shape, F32), pltpu.HBM((N_DEV,) + packed.shape, F32)]
        + [pltpu.SemaphoreType.DMA(())] * 2 + [_TOKEN],
        input_output_aliases={0: 0},
        compiler_params=_split_params(),
    )(_in_hbm(packed), after)
    return outs[0], outs[1], (outs[2], outs[3]), outs[4]


def _small_wait(packed, landed, sems, after):
    def body(small_ref, land_ref, send_sem, recv_sem, after_ref, small_thru, land_thru):
        x, y, c, _ = _place()
        for fx, fy, fc in _FLIPS:
            got = land_ref.at[4 * (x ^ fx) + 2 * (y ^ fy) + (c ^ fc)]
            _copy_to(got, got, send_sem, recv_sem, (x, y, 1 - c)).wait_recv()
        for _ in _FLIPS:
            _copy_to(small_ref, small_ref, send_sem, recv_sem, (x, y, 1 - c)).wait_send()

    outs = pl.pallas_call(
        body, name="small_grads_wait",
        in_specs=[_HBM, _HBM, _SEM, _SEM, _ANY], out_specs=[_HBM, _HBM],
        out_shape=[pltpu.HBM(packed.shape, F32), pltpu.HBM(landed.shape, F32)],
        input_output_aliases={0: 0, 1: 1},
        compiler_params=_split_params(),
    )(_in_hbm(packed), _in_hbm(landed), sems[0], sems[1], after)
    return outs[0], outs[1]


def _exchange_start(name, mats, grads):
    n = len(mats)

    def body(*refs):
        g_refs = refs[:n]
        outs = refs[n:]
        land_refs = outs[n:2 * n]
        send_sem, recv_sem, token = outs[2 * n], outs[2 * n + 1], outs[2 * n + 2]
        x, y, c, _ = _place()
        for m in range(n):
            for k in range(N_CHIPS):
                _copy_to(mats[m].piece(g_refs[m], k, 1 - c), land_refs[m].at[k], send_sem, recv_sem, (x, y, 1 - c)).start()
        token[...] = jnp.zeros_like(token)

    outs = pl.pallas_call(
        body, name=name,
        in_specs=[_HBM] * n,
        out_specs=[_HBM] * (2 * n) + [_SEM, _SEM, _TOKEN_SPEC],
        out_shape=[pltpu.HBM(mt.full_shape, BF16) for mt in mats]
        + [pltpu.HBM((N_CHIPS, mt.pr, mt.pc), BF16) for mt in mats] + [pltpu.SemaphoreType.DMA(())] * 2 + [_TOKEN],
        input_output_aliases={m: m for m in range(n)},
        compiler_params=_split_params(),
    )(*[_in_hbm(g) for g in grads])
    return list(outs[:n]), list(outs[n:2 * n]), (outs[2 * n], outs[2 * n + 1]), outs[2 * n + 2]


def _exchange_wait(name, mats, grads, landed, sems, after):
    n = len(mats)

    def body(*refs):
        g_refs, land_refs = refs[:n], refs[n:2 * n]
        send_sem, recv_sem = refs[2 * n], refs[2 * n + 1]
        x, y, c, _ = _place()
        for m in range(n):
            for k in range(N_CHIPS):
                got = land_refs[m].at[k]
                _copy_to(got, got, send_sem, recv_sem, (x, y, 1 - c)).wait_recv()
        for m in range(n):
            for k in range(N_CHIPS):
                sent = mats[m].piece(g_refs[m], k, 1 - c)
                _copy_to(sent, sent, send_sem, recv_sem, (x, y, 1 - c)).wait_send()

    outs = pl.pallas_call(
        body, name=name,
        in_specs=[_HBM] * (2 * n) + [_SEM, _SEM, _ANY], out_specs=[_HBM] * (2 * n),
        out_shape=[pltpu.HBM(a.shape, a.dtype) for a in list(grads) + list(landed)],
        input_output_aliases={i: i for i in range(2 * n)},
        compiler_params=_split_params(),
    )(*[_in_hbm(a) for a in list(grads) + list(landed)], sems[0], sems[1], after)
    return list(outs[:n]), list(outs[n:])


def _add_halves(name, mat, grad, landed, core):
    tr = _pick(mat.pr, (1024, 704, 512, 352, 256, 128, 64, 32, 16))
    per = mat.pr // tr

    def body(core_ref, g_ref, l_ref, o_ref):
        o_ref[...] = (g_ref[...].astype(F32) + l_ref[...].astype(F32)).astype(BF16)

    if mat.kind == "col":
        g_spec = pl.BlockSpec((tr, mat.pc), lambda k, r, core_ref: (core_ref[0] * per + r, k))
    else:
        g_spec = pl.BlockSpec((tr, mat.pc), lambda k, r, core_ref: ((2 * k + core_ref[0]) * per + r, 0))
    p_spec = pl.BlockSpec((None, tr, mat.pc), lambda k, r, core_ref: (k, r, 0))
    return pl.pallas_call(
        body, name=name,
        grid_spec=pltpu.PrefetchScalarGridSpec(num_scalar_prefetch=1, grid=(N_CHIPS, per),
                                               in_specs=[g_spec, p_spec], out_specs=p_spec),
        out_shape=jax.ShapeDtypeStruct((N_CHIPS, mat.pr, mat.pc), BF16),
        compiler_params=_cparams("parallel", "parallel"),
    )(core, grad, landed)


def _scatter_start(name, mats, partials):
    n = len(mats)

    def body(*refs):
        p_refs = refs[:n]
        outs = refs[n:]
        land_refs = outs[n:2 * n]
        send_sem, recv_sem, token = outs[2 * n], outs[2 * n + 1], outs[2 * n + 2]
        x, y, c, others = _place()
        me_k = 2 * x + y
        for m in range(n):
            for ox, oy in others:
                _copy_to(p_refs[m].at[2 * ox + oy], land_refs[m].at[me_k], send_sem, recv_sem, (ox, oy, c)).start()
        token[...] = jnp.zeros_like(token)

    piece_shapes = [pltpu.HBM((N_CHIPS, mt.pr, mt.pc), BF16) for mt in mats]
    outs = pl.pallas_call(
        body, name=name,
        in_specs=[_HBM] * n,
        out_specs=[_HBM] * (2 * n) + [_SEM, _SEM, _TOKEN_SPEC],
        out_shape=piece_shapes + piece_shapes + [pltpu.SemaphoreType.DMA(())] * 2 + [_TOKEN],
        input_output_aliases={m: m for m in range(n)},
        compiler_params=_split_params(),
    )(*[_in_hbm(p) for p in partials])
    return list(outs[:n]), list(outs[n:2 * n]), (outs[2 * n], outs[2 * n + 1]), outs[2 * n + 2]


def _scatter_wait(name, mats, partials, landed, sems, after):
    n = len(mats)

    def body(*refs):
        p_refs, land_refs = refs[:n], refs[n:2 * n]
        send_sem, recv_sem = refs[2 * n], refs[2 * n + 1]
        x, y, c, others = _place()
        for m in range(n):
            for ox, oy in others:
                got = land_refs[m].at[2 * ox + oy]
                _copy_to(got, got, send_sem, recv_sem, (ox, oy, c)).wait_recv()
        for m in range(n):
            for ox, oy in others:
                sent = p_refs[m].at[2 * ox + oy]
                _copy_to(sent, sent, send_sem, recv_sem, (ox, oy, c)).wait_send()

    outs = pl.pallas_call(
        body, name=name,
        in_specs=[_HBM] * (2 * n) + [_SEM, _SEM, _ANY], out_specs=[_HBM] * (2 * n),
        out_shape=[pltpu.HBM(a.shape, a.dtype) for a in list(partials) + list(landed)],
        input_output_aliases={i: i for i in range(2 * n)},
        compiler_params=_split_params(),
    )(*[_in_hbm(a) for a in list(partials) + list(landed)], sems[0], sems[1], after)
    return list(outs[:n]), list(outs[n:])


def _sum_chips(name, mat, partial, landed, slots, layer=None, stack=None, n_layers=1):
    tr = _pick(mat.pr, (1024, 704, 512, 352, 256, 128, 64, 32, 16))
    per = mat.pr // tr

    def body(slots_ref, own_ref, a_ref, b_ref, c_ref, *rest):
        o_ref = rest[-1]
        o_ref[...] = ((own_ref[...].astype(F32) + a_ref[...].astype(F32)) + b_ref[...].astype(F32)) + c_ref[...].astype(F32)

    def slot_spec(which):
        return pl.BlockSpec((None, tr, mat.pc), lambda r, slots_ref: (slots_ref[which], r, 0))

    in_specs = [slot_spec(0), slot_spec(1), slot_spec(2), slot_spec(3)]
    operands = [slots, partial, landed, landed, landed]
    aliases = {}
    if layer is None:
        o_spec = pl.BlockSpec((tr, mat.pc), lambda r, slots_ref: (slots_ref[4] * per + r, 0))
        out_shape = jax.ShapeDtypeStruct((mat.sr, mat.sc), F32)
    else:
        o_spec = pl.BlockSpec((None, tr, mat.pc), lambda r, slots_ref: (layer, slots_ref[4] * per + r, 0))
        out_shape = jax.ShapeDtypeStruct((n_layers, mat.sr, mat.sc), F32)
        if stack is not None:
            in_specs.append(_ANY)
            operands.append(stack)
            aliases = {len(operands) - 1: 0}
    return pl.pallas_call(
        body, name=name,
        grid_spec=pltpu.PrefetchScalarGridSpec(num_scalar_prefetch=1, grid=(per,), in_specs=in_specs, out_specs=o_spec),
        out_shape=out_shape, input_output_aliases=aliases,
        compiler_params=_cparams("parallel"),
    )(*operands)


def _share_pieces(name, mats, shards, groups):
    n = len(mats)
    n_out = len(groups)

    def body(*refs):
        out_refs = refs[n_out:2 * n_out]
        send_sems, recv_sems = refs[2 * n_out:]
        x, y, c, _ = _place()
        sibling = (x, y, 1 - c)
        sent, waits = [], []
        for o, members in enumerate(groups):
            for l, m in enumerate(members):
                dst = out_refs[o].at[l] if len(members) > 1 else out_refs[o]
                mine = mats[m].half(dst, c)
                sent.append(pltpu.make_async_remote_copy(src_ref=mine, dst_ref=mine, send_sem=send_sems.at[m],
                                                         recv_sem=recv_sems.at[m], device_id=sibling, device_id_type=MESH_ID))
                theirs = mats[m].half(dst, 1 - c)
                waits.append(pltpu.make_async_remote_copy(src_ref=theirs, dst_ref=theirs, send_sem=send_sems.at[m],
                                                          recv_sem=recv_sems.at[m], device_id=sibling, device_id_type=MESH_ID))
        for cp in sent:
            cp.start()
        for cp in waits:
            cp.wait_recv()
        for cp in sent:
            cp.wait_send()

    return pl.pallas_call(
        body, name=name,
        in_specs=[_ANY] * n_out, out_specs=[_ANY] * n_out,
        out_shape=[jax.ShapeDtypeStruct(s.shape, F32) for s in shards],
        input_output_aliases={o: o for o in range(n_out)},
        scratch_shapes=[pltpu.SemaphoreType.DMA((n,)), pltpu.SemaphoreType.DMA((n,))],
    )(*shards)


def _share_start(name, mats, shards, items):
    n_out = len(shards)

    def body(*refs):
        out_refs = refs[n_out:2 * n_out]
        send_sem, recv_sem, token = refs[2 * n_out], refs[2 * n_out + 1], refs[2 * n_out + 2]
        x, y, c, _ = _place()
        for o, members in enumerate(items):
            for layer, m in members:
                dst = out_refs[o] if layer is None else out_refs[o].at[layer]
                mine = mats[m].half(dst, c)
                _copy_to(mine, mine, send_sem, recv_sem, (x, y, 1 - c)).start()
        token[...] = jnp.zeros_like(token)

    outs = pl.pallas_call(
        body, name=name,
        in_specs=[_HBM] * n_out, out_specs=[_HBM] * n_out + [_SEM, _SEM, _TOKEN_SPEC],
        out_shape=[pltpu.HBM(s.shape, F32) for s in shards] + [pltpu.SemaphoreType.DMA(())] * 2 + [_TOKEN],
        input_output_aliases={o: o for o in range(n_out)},
        compiler_params=_split_params(),
    )(*[_in_hbm(s) for s in shards])
    return list(outs[:n_out]), (outs[n_out], outs[n_out + 1]), outs[n_out + 2]


def _share_wait(name, mats, shards, items, sems, after):
    n_out = len(shards)

    def body(*refs):
        bufs = refs[:n_out]
        send_sem, recv_sem = refs[n_out], refs[n_out + 1]
        x, y, c, _ = _place()
        for o, members in enumerate(items):
            for layer, m in members:
                dst = bufs[o] if layer is None else bufs[o].at[layer]
                theirs = mats[m].half(dst, 1 - c)
                _copy_to(theirs, theirs, send_sem, recv_sem, (x, y, 1 - c)).wait_recv()
        for o, members in enumerate(items):
            for layer, m in members:
                dst = bufs[o] if layer is None else bufs[o].at[layer]
                mine = mats[m].half(dst, c)
                _copy_to(mine, mine, send_sem, recv_sem, (x, y, 1 - c)).wait_send()

    outs = pl.pallas_call(
        body, name=name,
        in_specs=[_HBM] * n_out + [_SEM, _SEM, _ANY], out_specs=[_HBM] * n_out,
        out_shape=[pltpu.HBM(s.shape, F32) for s in shards],
        input_output_aliases={o: o for o in range(n_out)},
        compiler_params=_split_params(),
    )(*[_in_hbm(s) for s in shards], sems[0], sems[1], after)
    return list(outs)


def _sum_devices(stacked):
    nd, r, c = stacked.shape

    def body(s_ref, o_ref):
        s = s_ref[0]
        for k in range(1, nd):
            s = s + s_ref[k]
        o_ref[...] = s

    return pl.pallas_call(
        body, name="sum_small_grads", grid=(1,),
        in_specs=[pl.BlockSpec((nd, r, c), lambda i: (0, 0, 0))],
        out_specs=pl.BlockSpec((r, c), lambda i: (0, 0)),
        out_shape=jax.ShapeDtypeStruct((r, c), F32),
        compiler_params=_cparams("arbitrary"),
    )(stacked)


def _pack(arrs):
    flat = jnp.concatenate([a.reshape(-1) for a in arrs])
    rows = -(-flat.shape[0] // (8 * LANES)) * 8
    return jnp.pad(flat, (0, rows * LANES - flat.shape[0])).reshape(rows, LANES)


def _unpack(packed, shapes):
    flat = packed.reshape(-1)
    out, at = [], 0
    for s in shapes:
        size = 1
        for dim in s:
            size *= dim
        out.append(flat[at:at + size].reshape(s))
        at += size
    return out


def kernel(x, mix_norm_e, w_in_e, conv_w_e, conv_b_e, ln_g_e, ln_b_e, w_pool_e, pool_scale_e, w_out_e, mix_norm_o, w_in_o, conv_w_o, w_out_o, ffn_norm, w_gate, w_up, w_down, final_norm, loss_target, m_mix_norm_e, m_w_in_e, m_conv_w_e, m_conv_b_e, m_ln_g_e, m_ln_b_e, m_w_pool_e, m_pool_scale_e, m_w_out_e, m_mix_norm_o, m_w_in_o, m_conv_w_o, m_w_out_o, m_ffn_norm, m_w_gate, m_w_up, m_w_down, m_final_norm, v_mix_norm_e, v_w_in_e, v_conv_w_e, v_conv_b_e, v_ln_g_e, v_ln_b_e, v_w_pool_e, v_pool_scale_e, v_w_out_e, v_mix_norm_o, v_w_in_o, v_conv_w_o, v_w_out_o, v_ffn_norm, v_w_gate, v_w_up, v_w_down, v_final_norm):
    bsz, seq_len, d = x.shape
    t = bsz * seq_len
    depth = ffn_norm.shape[0]
    assert depth == 2 and conv_b_e.shape[1] == pool_scale_e.shape[1]
    ts = _pick(seq_len, (256, 128, 64, 32))
    me_k = 2 * lax.axis_index("x") + lax.axis_index("y")
    core = lax.axis_index("c").astype(jnp.int32).reshape(1)

    mat_src = [("col", w_in_e, 0), ("row", w_out_e, 0), ("col", w_gate, 0), ("col", w_up, 0), ("row", w_down, 0),
               ("col", w_in_o, 0), ("row", w_out_o, 0), ("col", w_gate, 1), ("col", w_up, 1), ("row", w_down, 1)]
    mats = [_Mat(kind, w.shape[1:]) for kind, w, _ in mat_src]
    n_pool = w_pool_e.shape[1]
    pool_mats = tuple(range(len(mats), len(mats) + n_pool))
    mats = mats + [_Mat("row", w_pool_e.shape[2:])] * n_pool
    chip = me_k.astype(jnp.int32).reshape(1)
    small_shards = [conv_w_e[0], w_pool_e[0], mix_norm_o, conv_w_o[0]]
    packed_small = _pack(small_shards)

    chain = [()]

    def seq(fn, *args, **kw):
        out = fn(*args, after=chain[0], **kw)
        chain[0] = (out[0] if isinstance(out, (list, tuple)) else out,)
        return out

    def mm(*args, **kw):
        return seq(_mm, *args, **kw)

    gather_groups = [(0,), (1,), (2, 3), (4,), (5, 6), (7, 8), (9,)]
    fulls, gather_sems, small_all = [None] * len(mats), [], None
    for g, ms in enumerate(gather_groups):
        own16 = [seq(_cast_into_full, "cast_w%d" % m, mats[m], mat_src[m][1], mat_src[m][2], chip) for m in ms]
        sent, landing, sems, token = _gather_start("gather_start%d" % g, [mats[m] for m in ms], own16,
                                                   _place_small(packed_small, chip) if g == 0 else None)
        chain[0] = (token,)
        for m, f in zip(ms, sent):
            fulls[m] = f
        gather_sems.append(sems)
        if g == 0:
            small_all = landing

    passed = {}

    def gather_pass(g):
        ms = gather_groups[g]
        bufs, pass_sems, token = _gather_pass("gather_pass%d" % g, [mats[m] for m in ms], [fulls[m] for m in ms],
                                              small_all if g == 0 else None, gather_sems[g], chain[0][0])
        chain[0] = (token,)
        passed[g] = (bufs, pass_sems)

    def gather_done(g):
        ms = gather_groups[g]
        bufs, pass_sems = passed[g]
        done = _gather_done("gather_done%d" % g, [mats[m] for m in ms], bufs[:len(ms)], pass_sems, chain[0][0])
        chain[0] = (done[0],)
        return done + bufs[len(ms):]

    h0 = x.reshape(t, d)
    target = loss_target.reshape(t, d)
    gather_pass(0)
    n1 = seq(_rms_fwd, "mix0_norm", h0, mix_norm_e)
    W_in_e, small_all = gather_done(0)
    per_chip = [_unpack(small_all[k], [s.shape for s in small_shards]) for k in range(N_CHIPS)]
    conv_w_e_f = jnp.concatenate([p[0] for p in per_chip], axis=1)
    w_pool_f = jnp.concatenate([p[1] for p in per_chip], axis=1)
    mix_norm_o_f = jnp.concatenate([p[2] for p in per_chip], axis=1)
    conv_w_o_f = jnp.concatenate([p[3] for p in per_chip], axis=1)
    W_gate, W_up, W_down = [None, None], [None, None], [None, None]

    (u_e,) = mm("mix0_in", [(n1, W_in_e)], "nn", [F32], _ep_store, tm=1024, tn=1024, tk=2048)
    gather_pass(1)
    a2, cat = seq(_mixer_e_fwd, u_e, conv_w_e_f, conv_b_e, ln_g_e, ln_b_e, w_pool_f, pool_scale_e, seq_len, ts)
    (W_out_e,) = gather_done(1)
    (h1,) = mm("mix0_out", [(cat, W_out_e)], "nn", [F32], _ep_residual, extras=(h0,), tm=1024, tn=1024, tk=2048)
    gather_pass(2)
    n2 = seq(_rms_fwd, "ffn0_norm", h1, ffn_norm[0:1])
    W_gate[0], W_up[0] = gather_done(2)
    gt0, up0, act0 = mm("ffn0_gate_up", [(n2, W_gate[0]), (n2, W_up[0])], "nn", [BF16] * 3, _ep_swiglu,
                        acc_of=(0, 1), tm=1024, tn=512, tk=2048)
    gather_pass(3)
    (W_down[0],) = gather_done(3)
    gather_pass(4)
    h2, n3 = seq(_mm_rows, "ffn0_down", [(act0, W_down[0])], "nn", [F32, BF16], _ep_rows_residual_norm, extras=(h1,),
                 vecs=(mix_norm_o_f,), tm=256, tk=act0.shape[1], chunk_dots=False)
    W_in_o, W_out_o = gather_done(4)
    (u_o,) = mm("mix1_in", [(n3, W_in_o)], "nn", [BF16], _ep_store, tm=1024, tn=1024, tk=2048)
    gather_pass(5)
    y_o = seq(_mixer_o_fwd, u_o, conv_w_o_f, seq_len, ts)
    h3, n4 = seq(_mm_rows, "mix1_out", [(y_o, W_out_o)], "nn", [F32, BF16], _ep_rows_residual_norm, extras=(h2,),
                 vecs=(ffn_norm[1:2],), tm=512, tk=2048)
    gather_pass(6)
    W_gate[1], W_up[1] = gather_done(5)
    gt1, up1, act1 = mm("ffn1_gate_up", [(n4, W_gate[1]), (n4, W_up[1])], "nn", [BF16] * 3, _ep_swiglu,
                        acc_of=(0, 1), tm=1024, tn=512, tk=2048)
    (W_down[1],) = gather_done(6)
    dh4, dh4b, loss_cols, d_final_norm = seq(_mm_rows, "ffn1_down", [(act1, W_down[1])], "nn", [F32, BF16],
                                             _ep_rows_loss_head, extras=(h3, target), vecs=(final_norm.reshape(1, d),),
                                             n_sums=2, tm=256, tk=act1.shape[1], chunk_dots=False)
    loss = lax.psum(jnp.sum(loss_cols), AXES)

    in_flight = {}
    partials, scattered = [None] * len(mats), [None] * len(mats)

    def reduce_begin(tag, ms, grads):
        gm = [mats[m] for m in ms]
        grads, landed, sems, token = _exchange_start("exchange_start_" + tag, gm, grads)
        chain[0] = (token,)
        in_flight[tag] = (ms, gm, grads, landed, sems)

    def reduce_advance(tag):
        ms, gm, grads, landed, sems = in_flight[tag]
        grads, landed = _exchange_wait("exchange_wait_" + tag, gm, grads, landed, sems, chain[0][0])
        parts = [_add_halves("add_halves%d" % m, mats[m], g, l, core) for m, g, l in zip(ms, grads, landed)]
        parts, lands, sems, token = _scatter_start("scatter_start_" + tag, gm, parts)
        chain[0] = (token,)
        in_flight[tag] = (ms, gm, parts, lands, sems)

    def reduce_finish(tag):
        ms, gm, parts, lands, sems = in_flight[tag]
        parts, lands = _scatter_wait("scatter_wait_" + tag, gm, parts, lands, sems, chain[0][0])
        chain[0] = (lands[0],)
        for m, p, l in zip(ms, parts, lands):
            partials[m], scattered[m] = p, l

    xi, yi, ci = lax.axis_index("x"), lax.axis_index("y"), lax.axis_index("c")
    slots = jnp.stack([me_k, 2 * (1 - xi) + yi, 2 * xi + (1 - yi), 2 * (1 - xi) + (1 - yi), ci]).astype(jnp.int32)
    shards = {}

    def sum_into(name, m, layer=None, n_layers=1):
        if layer is None:
            shards[name] = _sum_chips("sum_chips%d" % m, mats[m], partials[m], scattered[m], slots)
        else:
            shards[name] = _sum_chips("sum_chips%d" % m, mats[m], partials[m], scattered[m], slots, layer=layer,
                                      stack=shards.get(name), n_layers=n_layers)

    sharing = {}

    def share_begin(tag, names, items):
        arrays, sems, token = _share_start("share_start_" + tag, mats, [shards[nm] for nm in names], items)
        chain[0] = (token,)
        sharing[tag] = (names, items, arrays, sems)

    def share_end(tag):
        names, items, arrays, sems = sharing[tag]
        arrays = _share_wait("share_wait_" + tag, mats, arrays, items, sems, chain[0][0])
        chain[0] = (arrays[0],)
        for nm, a in zip(names, arrays):
            shards[nm] = a

    def ffn_bwd(l, dhb, n, gt, up, act, mid=None):
        dgt, dup = mm("ffn%d_dact" % l, [(dhb, W_down[l])], "nt", [BF16, BF16], _ep_swiglu_bwd, extras=(gt, up),
                      tm=2048, tn=512, tk=2048, row_chunk=512)
        (dW_down,) = mm("ffn%d_dw_down" % l, [(act, dhb)], "tn", [BF16], _ep_store, tm=512, tn=1024, tk=4096)
        if mid is not None:
            mid()
        (dn,) = mm("ffn%d_dn" % l, [(dgt, W_gate[l]), (dup, W_up[l])], "nt", [BF16], _ep_store,
                   tm=512, tn=2048, tk=1408)
        (dW_gate,) = mm("ffn%d_dw_gate" % l, [(n, dgt)], "tn", [BF16], _ep_store, tm=1024, tn=512, tk=4096)
        (dW_up,) = mm("ffn%d_dw_up" % l, [(n, dup)], "tn", [BF16], _ep_store, tm=1024, tn=512, tk=4096)
        return dn, dW_gate, dW_up, dW_down

    dn4, dW_gate1, dW_up1, dW_down1 = ffn_bwd(1, dh4b, n4, gt1, up1, act1)
    reduce_begin("ffn1", (7, 8, 9), [dW_gate1, dW_up1, dW_down1])
    dh3, dh3b, d_ffn_norm1 = seq(_rms_bwd, "ffn1_norm_bwd", dn4, h3, ffn_norm[1:2], dh4)
    (dy_o,) = mm("mix1_dy", [(dh3b, W_out_o)], "nt", [BF16], _ep_store, tm=1024, tn=1024, tk=2048)
    reduce_advance("ffn1")
    (dW_out_o,) = mm("mix1_dw_out", [(y_o, dh3b)], "tn", [BF16], _ep_store, tm=1024, tn=1024, tk=4096)
    du_o, d_conv_w_o = seq(_mixer_o_bwd, dy_o, u_o, conv_w_o_f, seq_len, ts)
    (dW_in_o,) = mm("mix1_dw_in", [(n3, du_o)], "tn", [BF16], _ep_store, tm=1024, tn=1024, tk=4096)
    reduce_begin("mix1", (5, 6), [dW_in_o, dW_out_o])
    dh2, dh2b, d_mix_norm_o = seq(_mm_rows, "mix1_dn", [(du_o, W_in_o)], "nt", [F32, BF16], _ep_rows_norm_bwd,
                                  extras=(h2, dh3), vecs=(mix_norm_o_f,), n_sums=1, tm=256, tk=6144, chunk_dots=False)
    reduce_advance("mix1")

    def finish_layer1():
        reduce_finish("ffn1")
        reduce_finish("mix1")
        sum_into("w_in_o", 5)
        sum_into("w_out_o", 6)
        for nm, m in (("w_gate", 7), ("w_up", 8), ("w_down", 9)):
            sum_into(nm, m, layer=1, n_layers=2)
        share_begin("layer1", ["w_in_o", "w_out_o", "w_gate", "w_up", "w_down"],
                    [[(None, 5)], [(None, 6)], [(1, 7)], [(1, 8)], [(1, 9)]])

    dn2, dW_gate0, dW_up0, dW_down0 = ffn_bwd(0, dh2b, n2, gt0, up0, act0, mid=finish_layer1)
    reduce_begin("ffn0", (2, 3, 4), [dW_gate0, dW_up0, dW_down0])
    dh1, dh1b, d_ffn_norm0 = seq(_rms_bwd, "ffn0_norm_bwd", dn2, h1, ffn_norm[0:1], dh2)
    (dcat,) = mm("mix0_dcat", [(dh1b, W_out_e)], "nt", [F32], _ep_store, tm=1024, tn=1024, tk=2048)
    reduce_advance("ffn0")
    (dW_out_e,) = mm("mix0_dw_out", [(cat, dh1b)], "tn", [BF16], _ep_store, tm=1024, tn=1024, tk=4096)
    da2, d_ln_g, d_ln_b, d_conv_b = seq(_mixer_e_bwd_norm, dcat, a2, ln_g_e, ln_b_e, ts)
    du_e, d_conv_w_e, d_w_pool, d_pool_scale = seq(_mixer_e_bwd_mix, da2, dcat, u_e, conv_w_e_f, w_pool_f, pool_scale_e,
                                                   seq_len, ts)
    (dW_in_e,) = mm("mix0_dw_in", [(n1, du_e)], "tn", [BF16], _ep_store, tm=1024, tn=1024, tk=4096)
    reduce_begin("mix0", (0, 1) + pool_mats, [dW_in_e, dW_out_e] + [d_w_pool[g].astype(BF16) for g in range(n_pool)])
    dx, d_mix_norm_e = seq(_mm_rows, "mix0_dn", [(du_e, W_in_e)], "nt", [F32], _ep_rows_norm_bwd,
                           extras=(h0, dh1), vecs=(mix_norm_e,), n_sums=1, tm=256, tk=3072, chunk_dots=False)
    reduce_advance("mix0")

    d_ffn_norm = jnp.concatenate([d_ffn_norm0, d_ffn_norm1], axis=0)
    small_partials = [d_mix_norm_e, d_conv_w_e, d_conv_b, d_ln_g, d_ln_b, d_pool_scale, d_mix_norm_o, d_conv_w_o,
                      d_ffn_norm, d_final_norm]
    packed_grads, small_stack, small_sems, token = _small_start(_pack(small_partials), chain[0][0])
    chain[0] = (token,)
    share_end("layer1")
    reduce_finish("ffn0")
    for nm, m in (("w_gate", 2), ("w_up", 3), ("w_down", 4)):
        sum_into(nm, m, layer=0, n_layers=2)
    share_begin("layer0", ["w_gate", "w_up", "w_down"], [[(0, 2)], [(0, 3)], [(0, 4)]])
    grad = {"w_in_o": shards["w_in_o"][None], "w_out_o": shards["w_out_o"][None]}
    weights = dict(mix_norm_e=mix_norm_e, w_in_e=w_in_e, conv_w_e=conv_w_e, conv_b_e=conv_b_e, ln_g_e=ln_g_e, ln_b_e=ln_b_e,
                   w_pool_e=w_pool_e, pool_scale_e=pool_scale_e, w_out_e=w_out_e, mix_norm_o=mix_norm_o, w_in_o=w_in_o,
                   conv_w_o=conv_w_o, w_out_o=w_out_o, ffn_norm=ffn_norm, w_gate=w_gate, w_up=w_up, w_down=w_down,
                   final_norm=final_norm)
    mom1 = dict(mix_norm_e=m_mix_norm_e, w_in_e=m_w_in_e, conv_w_e=m_conv_w_e, conv_b_e=m_conv_b_e, ln_g_e=m_ln_g_e,
                ln_b_e=m_ln_b_e, w_pool_e=m_w_pool_e, pool_scale_e=m_pool_scale_e, w_out_e=m_w_out_e, mix_norm_o=m_mix_norm_o,
                w_in_o=m_w_in_o, conv_w_o=m_conv_w_o, w_out_o=m_w_out_o, ffn_norm=m_ffn_norm, w_gate=m_w_gate, w_up=m_w_up,
                w_down=m_w_down, final_norm=m_final_norm)
    mom2 = dict(mix_norm_e=v_mix_norm_e, w_in_e=v_w_in_e, conv_w_e=v_conv_w_e, conv_b_e=v_conv_b_e, ln_g_e=v_ln_g_e,
                ln_b_e=v_ln_b_e, w_pool_e=v_w_pool_e, pool_scale_e=v_pool_scale_e, w_out_e=v_w_out_e, mix_norm_o=v_mix_norm_o,
                w_in_o=v_w_in_o, conv_w_o=v_conv_w_o, w_out_o=v_w_out_o, ffn_norm=v_ffn_norm, w_gate=v_w_gate, w_up=v_w_up,
                w_down=v_w_down, final_norm=v_final_norm)
    names = list(weights)

    big = ("w_in_o", "w_out_o", "w_gate", "w_up", "w_down", "w_in_e", "w_out_e")
    delta, new_m, new_v = {}, {}, {}

    def update(nm):
        shape = weights[nm].shape
        rows = 1
        for dim in shape[:-1]:
            rows *= dim
        as2d = lambda a: a.reshape(rows, shape[-1])
        dl, mn, vn, gc = seq(_adamw, "adamw_" + nm, as2d(weights[nm]), as2d(grad[nm]), as2d(mom1[nm]), as2d(mom2[nm]),
                             copy_grad=True)
        delta[nm], new_m[nm], new_v[nm], grad[nm] = dl.reshape(shape), mn.reshape(shape), vn.reshape(shape), gc.reshape(shape)

    for nm in big[:2]:
        update(nm)
    share_end("layer0")
    for nm in big[2:5]:
        grad[nm] = shards[nm]
        update(nm)
    reduce_finish("mix0")
    sum_into("w_in_e", 0)
    sum_into("w_out_e", 1)
    for layer, m in enumerate(pool_mats):
        sum_into("w_pool_e", m, layer=layer, n_layers=n_pool)
    g_w_in_e, g_w_out_e, g_w_pool = _share_pieces("share_pieces_first", mats,
                                                  [shards["w_in_e"], shards["w_out_e"], shards["w_pool_e"]],
                                                  [(0,), (1,), pool_mats])
    grad["w_in_e"], grad["w_out_e"], grad["w_pool_e"] = g_w_in_e[None], g_w_out_e[None], g_w_pool[None]
    for nm in big[5:]:
        update(nm)

    packed_grads, small_stack = _small_wait(packed_grads, small_stack, small_sems, chain[0][0])
    me_dev = 4 * xi + 2 * yi + ci
    small_stack = jnp.where(lax.broadcasted_iota(jnp.int32, (N_DEV, 1, 1), 0) == me_dev, packed_grads[None], small_stack)
    small_sum = _unpack(_sum_devices(small_stack), [s.shape for s in small_partials])
    (g_mix_norm_e, g_conv_w_e_f, g_conv_b, g_ln_g, g_ln_b, g_pool_scale, g_mix_norm_o_f, g_conv_w_o_f,
     g_ffn_norm, g_final_norm) = small_sum

    def my_shard(full, axis):
        size = full.shape[axis] // N_CHIPS
        return lax.dynamic_slice_in_dim(full, me_k * size, size, axis)

    grad.update({
        "mix_norm_e": g_mix_norm_e, "conv_w_e": my_shard(g_conv_w_e_f, 1)[None], "conv_b_e": g_conv_b,
        "ln_g_e": g_ln_g, "ln_b_e": g_ln_b, "pool_scale_e": g_pool_scale,
        "mix_norm_o": my_shard(g_mix_norm_o_f, 1), "conv_w_o": my_shard(g_conv_w_o_f, 1)[None],
        "ffn_norm": g_ffn_norm, "final_norm": g_final_norm.reshape(final_norm.shape),
    })
    small = [nm for nm in names if nm not in big]
    shapes = [weights[nm].shape for nm in small]
    dl, mn, vn = _adamw("adamw_small", _pack([weights[nm] for nm in small]), _pack([grad[nm] for nm in small]),
                        _pack([mom1[nm] for nm in small]), _pack([mom2[nm] for nm in small]))
    for nm, a, b, c_ in zip(small, _unpack(dl, shapes), _unpack(mn, shapes), _unpack(vn, shapes)):
        delta[nm], new_m[nm], new_v[nm] = a, b, c_

    grad_x = dx.reshape(bsz, seq_len, d)
    return (loss, grad_x, *[grad[nm] for nm in names], *[delta[nm] for nm in names],
            *[new_m[nm] for nm in names], *[new_v[nm] for nm in names])
```

```python
import jax
import jax.numpy as jnp
from jax import lax
from jax.experimental import pallas as pl
from jax.experimental.pallas import tpu as pltpu

F32 = jnp.float32
BF16 = jnp.bfloat16
MESH_ID = pl.DeviceIdType.MESH
AXES = ("x", "y", "c")
N_CHIPS = 4
N_DEV = 8

EPS = 1e-6
POOL_WINDOWS = (2, 4, 8, 16)
ADAM_LR, ADAM_B1, ADAM_B2, ADAM_EPS, ADAM_WD, ADAM_STEP = 0.001, 0.9, 0.999, 1e-08, 0.01, 10

LANES = 128
CONV_HALO = 32
POOL_HALO = 16
SHORT_HALO = 16
V7X_VMEM_LIMIT = 56 * 1024 * 1024


def _cparams(*sem):
    return pltpu.CompilerParams(dimension_semantics=sem if sem else None, vmem_limit_bytes=V7X_VMEM_LIMIT)


def _pick(dim, prefs):
    for p in prefs:
        if p <= dim and dim % p == 0:
            return p
    return dim


def _sigmoid(x):
    return jax.nn.sigmoid(x)


_ANY = pl.BlockSpec(memory_space=pl.ANY)


def _behind(after, body, n_in):
    if not after:
        return body
    skip = len(after)

    def body_behind(*refs):
        return body(*refs[:n_in], *refs[n_in + skip:])

    return body_behind


_DOT_DIMS = {
    "nn": (((1,), (0,)), ((), ())),
    "nt": (((1,), (1,)), ((), ())),
    "tn": (((0,), (0,)), ((), ())),
}


def _mm(name, pairs, mode, out_dtypes, epilogue, extras=(), acc_of=None, tm=512, tn=512, tk=2048, row_chunk=0, after=()):
    a0, b0 = pairs[0]
    if mode == "nn":
        (m, k), n = a0.shape, b0.shape[1]
    elif mode == "nt":
        (m, k), n = a0.shape, b0.shape[0]
    else:
        (k, m), n = a0.shape, b0.shape[1]
    tm = _pick(m, (tm, 512, 256, 128, 64, 32, 16, 8))
    tn = _pick(n, (tn, 512, 256, 128))
    tk = _pick(k, (tk, 2048, 1024, 512, 256, 128))
    nk = k // tk
    n_pairs = len(pairs)
    acc_of = tuple(acc_of) if acc_of is not None else (0,) * n_pairs
    n_acc = max(acc_of) + 1
    n_ex, n_out = len(extras), len(out_dtypes)
    dims = _DOT_DIMS[mode]

    def body(*refs):
        a_refs = refs[:n_pairs]
        b_refs = refs[n_pairs:2 * n_pairs]
        e_refs = refs[2 * n_pairs:2 * n_pairs + n_ex]
        first_out = 2 * n_pairs + n_ex + len(after)
        o_refs = refs[first_out:first_out + n_out]
        acc_refs = refs[first_out + n_out:]

        def partial_sums(rows=None):
            sums = [None] * n_acc
            for p in range(n_pairs):
                a = a_refs[p][...] if rows is None else (a_refs[p][:, rows] if mode == "tn" else a_refs[p][rows, :])
                d = lax.dot_general(a, b_refs[p][...], dims, preferred_element_type=F32)
                sums[acc_of[p]] = d if sums[acc_of[p]] is None else sums[acc_of[p]] + d
            return sums

        if nk == 1 and row_chunk:
            for r0 in range(0, tm, row_chunk):
                rows = pl.ds(r0, row_chunk)
                epilogue(partial_sums(rows), [e.at[rows, :] for e in e_refs], [o.at[rows, :] for o in o_refs])
            return
        if nk == 1:
            epilogue(partial_sums(), e_refs, o_refs)
            return
        kk = pl.program_id(2)

        @pl.when(kk == 0)
        def _():
            for acc in acc_refs:
                acc[...] = jnp.zeros_like(acc)

        for acc, s in zip(acc_refs, partial_sums()):
            acc[...] += s

        @pl.when(kk == nk - 1)
        def _():
            epilogue([acc[...] for acc in acc_refs], e_refs, o_refs)

    if mode == "nn":
        a_spec = pl.BlockSpec((tm, tk), lambda i, j, kk: (i, kk))
        b_spec = pl.BlockSpec((tk, tn), lambda i, j, kk: (kk, j))
    elif mode == "nt":
        a_spec = pl.BlockSpec((tm, tk), lambda i, j, kk: (i, kk))
        b_spec = pl.BlockSpec((tn, tk), lambda i, j, kk: (j, kk))
    else:
        a_spec = pl.BlockSpec((tk, tm), lambda i, j, kk: (kk, i))
        b_spec = pl.BlockSpec((tk, tn), lambda i, j, kk: (kk, j))
    o_spec = pl.BlockSpec((tm, tn), lambda i, j, kk: (i, j))
    outs = pl.pallas_call(
        body,
        name=name,
        grid=(m // tm, n // tn, nk),
        in_specs=[a_spec] * n_pairs + [b_spec] * n_pairs + [o_spec] * n_ex
        + [pl.BlockSpec(memory_space=pl.ANY)] * len(after),
        out_specs=[o_spec] * n_out,
        out_shape=[jax.ShapeDtypeStruct((m, n), dt) for dt in out_dtypes],
        scratch_shapes=[pltpu.VMEM((tm, tn), F32) for _ in range(n_acc)] if nk > 1 else [],
        compiler_params=_cparams("parallel", "parallel", "arbitrary"),
    )(*[p[0] for p in pairs], *[p[1] for p in pairs], *extras, *after)
    return outs


def _mm_rows(name, pairs, mode, out_dtypes, epilogue, extras=(), vecs=(), n_sums=0, tm=512, tk=2048, row_chunk=128,
             chunk_dots=True, after=()):
    a0, b0 = pairs[0]
    (m, k), n = a0.shape, (b0.shape[1] if mode == "nn" else b0.shape[0])
    tm = _pick(m, (tm, 512, 256, 128, 64, 32, 16, 8))
    tk = _pick(k, (tk, 2048, 1024, 512, 256, 128))
    row_chunk = min(row_chunk, tm)
    nk = k // tk
    n_pairs, n_ex, n_vec, n_out = len(pairs), len(extras), len(vecs), len(out_dtypes)
    dims = _DOT_DIMS[mode]

    def body(*refs):
        a_refs = refs[:n_pairs]
        b_refs = refs[n_pairs:2 * n_pairs]
        e_refs = refs[2 * n_pairs:2 * n_pairs + n_ex]
        v_refs = refs[2 * n_pairs + n_ex:2 * n_pairs + n_ex + n_vec]
        first_out = 2 * n_pairs + n_ex + n_vec + len(after)
        o_refs = refs[first_out:first_out + n_out]
        s_refs = refs[first_out + n_out:first_out + n_out + n_sums]
        acc_refs = refs[first_out + n_out + n_sums:]
        i, kk = pl.program_id(0), pl.program_id(1)

        @pl.when((i == 0) & (kk == 0))
        def _():
            for s in s_refs:
                s[...] = jnp.zeros_like(s)

        def dots(rows):
            total = None
            for p in range(n_pairs):
                d = lax.dot_general(a_refs[p][rows, :], b_refs[p][...], dims, preferred_element_type=F32)
                total = d if total is None else total + d
            return total

        def finish(acc_of_rows):
            for r0 in range(0, tm, row_chunk):
                rows = pl.ds(r0, row_chunk)
                epilogue(acc_of_rows(rows), [e.at[rows, :] for e in e_refs], v_refs, [o.at[rows, :] for o in o_refs], s_refs)

        if nk == 1 and chunk_dots:
            finish(dots)
            return
        acc = acc_refs[0]
        if nk == 1:
            acc[...] = dots(slice(None))
            finish(lambda rows: acc[rows, :])
            return

        @pl.when(kk == 0)
        def _():
            acc[...] = jnp.zeros_like(acc)

        acc[...] += dots(slice(None))

        @pl.when(kk == nk - 1)
        def _():
            finish(lambda rows: acc[rows, :])

    a_spec = pl.BlockSpec((tm, tk), lambda i, kk: (i, kk))
    b_mode = dict(pipeline_mode=pl.Buffered(1)) if nk == 1 else {}
    b_spec = (pl.BlockSpec((tk, n), lambda i, kk: (kk, 0), **b_mode) if mode == "nn"
              else pl.BlockSpec((n, tk), lambda i, kk: (0, kk), **b_mode))
    row_spec = pl.BlockSpec((tm, n), lambda i, kk: (i, 0))
    vec_spec = pl.BlockSpec((1, n), lambda i, kk: (0, 0))
    return pl.pallas_call(
        body, name=name, grid=(m // tm, nk),
        in_specs=[a_spec] * n_pairs + [b_spec] * n_pairs + [row_spec] * n_ex + [vec_spec] * n_vec + [_ANY] * len(after),
        out_specs=[row_spec] * n_out + [vec_spec] * n_sums,
        out_shape=[jax.ShapeDtypeStruct((m, n), dt) for dt in out_dtypes] + [jax.ShapeDtypeStruct((1, n), F32)] * n_sums,
        scratch_shapes=[pltpu.VMEM((tm, n), F32)] if (nk > 1 or not chunk_dots) else [],
        compiler_params=_cparams("arbitrary", "arbitrary"),
    )(*[p[0] for p in pairs], *[p[1] for p in pairs], *extras, *vecs, *after)


def _ep_rows_residual_norm(acc, ex, vecs, outs, sums):
    h = ex[0][...] + acc
    outs[0][...] = h
    r = lax.rsqrt(jnp.mean(h * h, axis=-1, keepdims=True) + EPS)
    outs[1][...] = (h * r * vecs[0][...]).astype(BF16)


def _ep_rows_loss_head(acc, ex, vecs, outs, sums):
    x = ex[0][...] + acc
    gain = vecs[0][...]
    inv_d = 1.0 / x.shape[-1]
    r = lax.rsqrt(jnp.mean(x * x, axis=-1, keepdims=True) + EPS)
    xhat = x * r
    err = xhat * gain - ex[1][...]
    sums[0][...] += (0.5 * inv_d) * jnp.sum(err * err, axis=0, keepdims=True)
    dy = err * inv_d
    sums[1][...] += jnp.sum(dy * xhat, axis=0, keepdims=True)
    dxh = dy * gain
    dh = r * (dxh - xhat * jnp.mean(dxh * xhat, axis=-1, keepdims=True))
    outs[0][...] = dh
    outs[1][...] = dh.astype(BF16)


def _ep_rows_norm_bwd(dn, ex, vecs, outs, sums):
    x = ex[0][...]
    r = lax.rsqrt(jnp.mean(x * x, axis=-1, keepdims=True) + EPS)
    xhat = x * r
    sums[0][...] += jnp.sum(dn * xhat, axis=0, keepdims=True)
    dxh = dn * vecs[0][...]
    dh = ex[1][...] + r * (dxh - xhat * jnp.mean(dxh * xhat, axis=-1, keepdims=True))
    outs[0][...] = dh
    if len(outs) > 1:
        outs[1][...] = dh.astype(BF16)


def _ep_store(accs, ex, outs):
    outs[0][...] = accs[0].astype(outs[0].dtype)


def _ep_residual(accs, ex, outs):
    outs[0][...] = ex[0][...] + accs[0]


def _ep_swiglu(accs, ex, outs):
    g, u = accs
    s = _sigmoid(g)
    gs = g * s
    outs[0][...] = (u * (s * (1.0 + g * (1.0 - s)))).astype(BF16)
    outs[1][...] = gs.astype(BF16)
    outs[2][...] = (gs * u).astype(BF16)


def _ep_swiglu_bwd(accs, ex, outs):
    d = accs[0]
    outs[0][...] = (d * ex[0][...].astype(F32)).astype(BF16)
    outs[1][...] = (d * ex[1][...].astype(F32)).astype(BF16)


def _rms_fwd(name, h, g, after=()):
    t, d = h.shape
    tr = _pick(t, (256, 128, 64, 32, 16, 8))

    def body(h_ref, g_ref, *rest):
        o_ref = rest[-1]
        x = h_ref[...]
        r = lax.rsqrt(jnp.mean(x * x, axis=-1, keepdims=True) + EPS)
        o_ref[...] = (x * r * g_ref[...]).astype(BF16)

    return pl.pallas_call(
        body, name=name, grid=(t // tr,),
        in_specs=[pl.BlockSpec((tr, d), lambda i: (i, 0)), pl.BlockSpec((1, d), lambda i: (0, 0))]
        + [pl.BlockSpec(memory_space=pl.ANY)] * len(after),
        out_specs=pl.BlockSpec((tr, d), lambda i: (i, 0)),
        out_shape=jax.ShapeDtypeStruct((t, d), BF16),
        compiler_params=_cparams("parallel"),
    )(h, g, *after)


def _rms_bwd(name, dn, h, g, dres, after=()):
    t, d = h.shape
    tr = _pick(t, (256, 128, 64, 32, 16, 8))

    def body(dn_ref, h_ref, g_ref, dres_ref, dh_ref, dhb_ref, dg_ref):
        @pl.when(pl.program_id(0) == 0)
        def _():
            dg_ref[...] = jnp.zeros_like(dg_ref)

        chunk = min(64, tr)
        for r0 in range(0, tr, chunk):
            rows = pl.ds(r0, chunk)
            x = h_ref[rows, :]
            r = lax.rsqrt(jnp.mean(x * x, axis=-1, keepdims=True) + EPS)
            xhat = x * r
            dnv = dn_ref[rows, :].astype(F32)
            dg_ref[...] += jnp.sum(dnv * xhat, axis=0, keepdims=True)
            dxh = dnv * g_ref[...]
            dh = dres_ref[rows, :] + r * (dxh - xhat * jnp.mean(dxh * xhat, axis=-1, keepdims=True))
            dh_ref[rows, :] = dh
            dhb_ref[rows, :] = dh.astype(BF16)

    row = pl.BlockSpec((tr, d), lambda i: (i, 0))
    vec = pl.BlockSpec((1, d), lambda i: (0, 0))
    return pl.pallas_call(
        _behind(after, body, 4), name=name, grid=(t // tr,),
        in_specs=[row, row, vec, row] + [_ANY] * len(after),
        out_specs=[row, row, vec],
        out_shape=[jax.ShapeDtypeStruct((t, d), F32), jax.ShapeDtypeStruct((t, d), BF16),
                   jax.ShapeDtypeStruct((1, d), F32)],
        compiler_params=_cparams("arbitrary"),
    )(dn, h, g, dres, *after)


def _cur(ts, width, col):
    return pl.BlockSpec((ts, width), lambda i: (i, col))


def _prev_halo(ts, halo, width, col):
    per = ts // halo
    return pl.BlockSpec((halo, width), lambda i: (jnp.maximum(i * per - 1, 0), col))


def _next_halo(ts, halo, width, col, n_rows):
    per = ts // halo
    last = n_rows // halo - 1
    return pl.BlockSpec((halo, width), lambda i: (jnp.minimum((i + 1) * per, last), col))


def _full(shape):
    nd = len(shape)
    return pl.BlockSpec(shape, lambda i: (0,) * nd)


def _shift_down(x, n):
    return x if n == 0 else pltpu.roll(x, n, 0)


def _shift_up(x, n):
    return x if n == 0 else pltpu.roll(x, x.shape[0] - n, 0)


CONV_ROWS = 32


def _conv_block_shape(channels, ts):
    return min(CONV_ROWS, ts), min(LANES, channels)


SUBLANES = 8


def _fill_shifted(rot_ref, ext):
    rot_ref[0] = ext
    for r in range(1, SUBLANES):
        rot_ref[r] = _shift_up(ext, r)


def _window(rot_ref, first, rows, c0, cw):
    r = first % SUBLANES
    return rot_ref[r, first - r:first - r + rows, c0:c0 + cw]


def _causal_taps(rot_ref, w_ref, halo, taps, r0, c0, rows, cw):
    acc = jnp.zeros((rows, cw), F32)
    for k in range(taps):
        acc = acc + w_ref[k:k + 1, c0:c0 + cw] * _window(rot_ref, halo + r0 - (taps - 1 - k), rows, c0, cw)
    return acc


def _pool_counts(i, ns, ts, w):
    pos = (i % ns) * ts + lax.broadcasted_iota(jnp.int32, (ts, 1), 0)
    return jnp.minimum(pos + 1, w).astype(F32)


def _pooled(cur, prev_tail, w, cnt):
    s = jnp.concatenate([prev_tail, cur], axis=0)
    d = 1
    while d < w:
        s = s + _shift_down(s, d)
        d *= 2
    return s[POOL_HALO:, :] / cnt - cur


def _mixer_e_fwd(u, conv_w, conv_b, ln_g, ln_b, w_pool, scale, seq, ts, after=()):
    t = u.shape[0]
    dc = conv_b.shape[1]
    ng, pg = w_pool.shape[0], w_pool.shape[1]
    taps = conv_w.shape[0]
    ns = seq // ts

    def body(val_ref, gate_ref, b_ref, pval_ref, pgate_ref, pb_ref, cw_ref, cb_ref, g_ref, be_ref, wp_ref, sc_ref,
             a2_ref, cat_ref, rot_ref):
        i = pl.program_id(0)
        keep_prev = jnp.where(i % ns == 0, 0.0, 1.0)
        a1 = val_ref[...] * _sigmoid(gate_ref[...])
        pa1 = pval_ref[...] * _sigmoid(pgate_ref[...]) * keep_prev
        _fill_shifted(rot_ref, jnp.concatenate([pa1, a1], axis=0))
        rows, cw = _conv_block_shape(dc, ts)
        for c0 in range(0, dc, cw):
            for r0 in range(0, ts, rows):
                acc = _causal_taps(rot_ref, cw_ref, CONV_HALO, taps, r0, c0, rows, cw)
                a2_ref[r0:r0 + rows, c0:c0 + cw] = acc + cb_ref[:, c0:c0 + cw]
        a2 = a2_ref[...]
        mu = jnp.mean(a2, axis=-1, keepdims=True)
        xc = a2 - mu
        rstd = lax.rsqrt(jnp.mean(xc * xc, axis=-1, keepdims=True) + EPS)
        a3 = xc * rstd * g_ref[...] + be_ref[...]
        cat_ref[:, 0:dc] = (a3 * _sigmoid(a3)).astype(BF16)
        for g in range(ng):
            lo, hi = g * pg, (g + 1) * pg
            w = POOL_WINDOWS[g]
            p = _pooled(b_ref[:, lo:hi], pb_ref[:, lo:hi] * keep_prev, w, _pool_counts(i, ns, ts, w))
            q = jnp.dot(p.astype(BF16), wp_ref[g].astype(BF16), preferred_element_type=F32)
            cat_ref[:, dc + lo:dc + hi] = (q * sc_ref[:, lo:hi]).astype(BF16)

    return pl.pallas_call(
        _behind(after, body, 12), name="mixer_e_fwd", grid=(t // ts,),
        in_specs=[_cur(ts, dc, 0), _cur(ts, dc, 1), _cur(ts, dc, 2),
                  _prev_halo(ts, CONV_HALO, dc, 0), _prev_halo(ts, CONV_HALO, dc, 1), _prev_halo(ts, POOL_HALO, dc, 2),
                  _full(conv_w.shape), _full(conv_b.shape), _full(ln_g.shape), _full(ln_b.shape),
                  _full(w_pool.shape), _full(scale.shape)] + [_ANY] * len(after),
        out_specs=[_cur(ts, dc, 0), _cur(ts, 2 * dc, 0)],
        out_shape=[jax.ShapeDtypeStruct((t, dc), F32), jax.ShapeDtypeStruct((t, 2 * dc), BF16)],
        scratch_shapes=[pltpu.VMEM((SUBLANES, CONV_HALO + ts, dc), F32)],
        compiler_params=_cparams("parallel"),
    )(u, u, u, u, u, u, conv_w, conv_b, ln_g, ln_b, w_pool, scale, *after)


def _mixer_e_bwd_norm(dcat, a2, ln_g, ln_b, ts, after=()):
    t, dc = a2.shape

    def body(d_ref, a2_ref, g_ref, be_ref, da2_ref, dg_ref, db_ref, dcb_ref):
        x = a2_ref[...]
        gv = g_ref[...]
        mu = jnp.mean(x, axis=-1, keepdims=True)
        xc = x - mu
        rstd = lax.rsqrt(jnp.mean(xc * xc, axis=-1, keepdims=True) + EPS)
        xhat = xc * rstd
        a3 = xhat * gv + be_ref[...]
        sg = _sigmoid(a3)
        da3 = d_ref[...] * (sg * (1.0 + a3 * (1.0 - sg)))
        dxh = da3 * gv
        da2 = rstd * (dxh - jnp.mean(dxh, axis=-1, keepdims=True)
                      - xhat * jnp.mean(dxh * xhat, axis=-1, keepdims=True))
        da2_ref[...] = da2

        @pl.when(pl.program_id(0) == 0)
        def _():
            dg_ref[...] = jnp.zeros_like(dg_ref)
            db_ref[...] = jnp.zeros_like(db_ref)
            dcb_ref[...] = jnp.zeros_like(dcb_ref)

        dg_ref[...] += jnp.sum(da3 * xhat, axis=0, keepdims=True)
        db_ref[...] += jnp.sum(da3, axis=0, keepdims=True)
        dcb_ref[...] += jnp.sum(da2, axis=0, keepdims=True)

    vec = _full((1, dc))
    return pl.pallas_call(
        _behind(after, body, 4), name="mixer_e_bwd_norm", grid=(t // ts,),
        in_specs=[_cur(ts, dc, 0), _cur(ts, dc, 0), vec, vec] + [_ANY] * len(after),
        out_specs=[_cur(ts, dc, 0), vec, vec, vec],
        out_shape=[jax.ShapeDtypeStruct((t, dc), F32)] + [jax.ShapeDtypeStruct((1, dc), F32)] * 3,
        compiler_params=_cparams("arbitrary"),
    )(dcat, a2, ln_g, ln_b, *after)


def _mixer_e_bwd_mix(da2, dcat, u, conv_w, w_pool, scale, seq, ts, after=()):
    t, dc = da2.shape
    ng, pg = w_pool.shape[0], w_pool.shape[1]
    taps = conv_w.shape[0]
    ns = seq // ts

    def body(da2_ref, nda2_ref, dp_ref, ndp_ref, val_ref, gate_ref, b_ref, pval_ref, pgate_ref, pb_ref,
             cw_ref, wp_ref, sc_ref, du_ref, dcw_ref, dwp_ref, dsc_ref, rota_ref, rotd_ref):
        i = pl.program_id(0)
        keep_prev = jnp.where(i % ns == 0, 0.0, 1.0)
        keep_next = jnp.where(i % ns == ns - 1, 0.0, 1.0)

        @pl.when(i == 0)
        def _():
            dcw_ref[...] = jnp.zeros_like(dcw_ref)
            dwp_ref[...] = jnp.zeros_like(dwp_ref)
            dsc_ref[...] = jnp.zeros_like(dsc_ref)

        val = val_ref[...]
        sg = _sigmoid(gate_ref[...])
        a1 = val * sg
        pa1 = pval_ref[...] * _sigmoid(pgate_ref[...]) * keep_prev
        _fill_shifted(rota_ref, jnp.concatenate([pa1, a1], axis=0))
        _fill_shifted(rotd_ref, jnp.concatenate([da2_ref[...], nda2_ref[...] * keep_next], axis=0))
        rows, cw = _conv_block_shape(dc, ts)
        for c0 in range(0, dc, cw):
            lanes = slice(c0, c0 + cw)
            dw = [jnp.zeros((SUBLANES, cw), F32)] * taps
            for r0 in range(0, ts, rows):
                blk = slice(r0, r0 + rows)
                d_blk = da2_ref[blk, lanes]
                da1 = jnp.zeros((rows, cw), F32)
                for k in range(taps):
                    sh = taps - 1 - k
                    prod = d_blk * _window(rota_ref, CONV_HALO + r0 - sh, rows, c0, cw)
                    for f in range(0, rows, SUBLANES):
                        dw[k] = dw[k] + prod[f:f + SUBLANES, :]
                    da1 = da1 + cw_ref[k:k + 1, lanes] * _window(rotd_ref, r0 + sh, rows, c0, cw)
                sg_b = sg[blk, lanes]
                du_ref[blk, lanes] = (da1 * sg_b).astype(BF16)
                du_ref[blk, dc + c0:dc + c0 + cw] = (da1 * a1[blk, lanes] * (1.0 - sg_b)).astype(BF16)
            for k in range(taps):
                dcw_ref[k:k + 1, lanes] += jnp.sum(dw[k], axis=0, keepdims=True)

        for g in range(ng):
            lo, hi = g * pg, (g + 1) * pg
            w = POOL_WINDOWS[g]
            cnt = _pool_counts(i, ns, ts, w)
            wpb = wp_ref[g].astype(BF16)
            sc = sc_ref[:, lo:hi]
            p = _pooled(b_ref[:, lo:hi], pb_ref[:, lo:hi] * keep_prev, w, cnt)
            pb16 = p.astype(BF16)
            q = jnp.dot(pb16, wpb, preferred_element_type=F32)
            dout = dp_ref[:, lo:hi]
            dsc_ref[:, lo:hi] += jnp.sum(dout * q, axis=0, keepdims=True)
            dq = (dout * sc).astype(BF16)
            dwp_ref[g] += lax.dot_general(pb16, dq, _DOT_DIMS["tn"], preferred_element_type=F32)
            dpool = lax.dot_general(dq, wpb, _DOT_DIMS["nt"], preferred_element_type=F32)
            ndq = (ndp_ref[:, lo:hi] * sc * keep_next).astype(BF16)
            ndpool = lax.dot_general(ndq, wpb, _DOT_DIMS["nt"], preferred_element_type=F32)
            s = jnp.concatenate([dpool / cnt, ndpool * (1.0 / w)], axis=0)
            d = 1
            while d < w:
                s = s + _shift_up(s, d)
                d *= 2
            du_ref[:, 2 * dc + lo:2 * dc + hi] = (s[:ts, :] - dpool).astype(BF16)

    return pl.pallas_call(
        _behind(after, body, 13), name="mixer_e_bwd_mix", grid=(t // ts,),
        in_specs=[_cur(ts, dc, 0), _next_halo(ts, CONV_HALO, dc, 0, t),
                  _cur(ts, dc, 1), _next_halo(ts, POOL_HALO, dc, 1, t),
                  _cur(ts, dc, 0), _cur(ts, dc, 1), _cur(ts, dc, 2),
                  _prev_halo(ts, CONV_HALO, dc, 0), _prev_halo(ts, CONV_HALO, dc, 1), _prev_halo(ts, POOL_HALO, dc, 2),
                  _full(conv_w.shape), _full(w_pool.shape), _full(scale.shape)] + [_ANY] * len(after),
        out_specs=[_cur(ts, 3 * dc, 0), _full(conv_w.shape), _full(w_pool.shape), _full(scale.shape)],
        out_shape=[jax.ShapeDtypeStruct((t, 3 * dc), BF16), jax.ShapeDtypeStruct(conv_w.shape, F32),
                   jax.ShapeDtypeStruct(w_pool.shape, F32), jax.ShapeDtypeStruct(scale.shape, F32)],
        scratch_shapes=[pltpu.VMEM((SUBLANES, ts + CONV_HALO, dc), F32)] * 2,
        compiler_params=_cparams("arbitrary"),
    )(da2, da2, dcat, dcat, u, u, u, u, u, u, conv_w, w_pool, scale, *after)


def _mixer_o_fwd(u, conv_w, seq, ts, after=()):
    t = u.shape[0]
    d = conv_w.shape[1]
    taps = conv_w.shape[0]
    ns = seq // ts

    def body(gb_ref, gc_ref, v_ref, pgc_ref, pv_ref, cw_ref, y_ref):
        keep_prev = jnp.where(pl.program_id(0) % ns == 0, 0.0, 1.0)
        f32 = lambda ref: ref[...].astype(F32)
        ext = jnp.concatenate([f32(pgc_ref) * f32(pv_ref) * keep_prev, f32(gc_ref) * f32(v_ref)], axis=0)
        cc = jnp.zeros_like(ext)
        for k in range(taps):
            cc = cc + cw_ref[k:k + 1, :] * _shift_down(ext, taps - 1 - k)
        y_ref[...] = (f32(gb_ref) * cc[SHORT_HALO:, :]).astype(BF16)

    return pl.pallas_call(
        _behind(after, body, 6), name="mixer_o_fwd", grid=(t // ts,),
        in_specs=[_cur(ts, d, 0), _cur(ts, d, 1), _cur(ts, d, 2),
                  _prev_halo(ts, SHORT_HALO, d, 1), _prev_halo(ts, SHORT_HALO, d, 2), _full(conv_w.shape)]
        + [_ANY] * len(after),
        out_specs=_cur(ts, d, 0),
        out_shape=jax.ShapeDtypeStruct((t, d), BF16),
        compiler_params=_cparams("parallel"),
    )(u, u, u, u, u, conv_w, *after)


def _mixer_o_bwd(dy, u, conv_w, seq, ts, after=()):
    t = u.shape[0]
    d = conv_w.shape[1]
    taps = conv_w.shape[0]
    ns = seq // ts

    def body(dy_ref, ndy_ref, gb_ref, gc_ref, v_ref, pgc_ref, pv_ref, ngb_ref, cw_ref, du_ref, dcw_ref):
        i = pl.program_id(0)
        keep_prev = jnp.where(i % ns == 0, 0.0, 1.0)
        keep_next = jnp.where(i % ns == ns - 1, 0.0, 1.0)

        @pl.when(i == 0)
        def _():
            dcw_ref[...] = jnp.zeros_like(dcw_ref)

        f32 = lambda ref: ref[...].astype(F32)
        gb, gc, v, dyv = f32(gb_ref), f32(gc_ref), f32(v_ref), f32(dy_ref)
        ext = jnp.concatenate([f32(pgc_ref) * f32(pv_ref) * keep_prev, gc * v], axis=0)
        dcc = dyv * gb
        ext_d = jnp.concatenate([dcc, f32(ndy_ref) * f32(ngb_ref) * keep_next], axis=0)
        cc = jnp.zeros_like(ext)
        dcv = jnp.zeros_like(ext_d)
        for k in range(taps):
            sh = taps - 1 - k
            shifted = _shift_down(ext, sh)
            cc = cc + cw_ref[k:k + 1, :] * shifted
            dcw_ref[k:k + 1, :] += jnp.sum(dcc * shifted[SHORT_HALO:, :], axis=0, keepdims=True)
            dcv = dcv + cw_ref[k:k + 1, :] * _shift_up(ext_d, sh)
        dcv = dcv[:ts, :]
        du_ref[:, 0:d] = (dyv * cc[SHORT_HALO:, :]).astype(BF16)
        du_ref[:, d:2 * d] = (dcv * v).astype(BF16)
        du_ref[:, 2 * d:3 * d] = (dcv * gc).astype(BF16)

    return pl.pallas_call(
        _behind(after, body, 9), name="mixer_o_bwd", grid=(t // ts,),
        in_specs=[_cur(ts, d, 0), _next_halo(ts, SHORT_HALO, d, 0, t),
                  _cur(ts, d, 0), _cur(ts, d, 1), _cur(ts, d, 2),
                  _prev_halo(ts, SHORT_HALO, d, 1), _prev_halo(ts, SHORT_HALO, d, 2),
                  _next_halo(ts, SHORT_HALO, d, 0, t), _full(conv_w.shape)] + [_ANY] * len(after),
        out_specs=[_cur(ts, 3 * d, 0), _full(conv_w.shape)],
        out_shape=[jax.ShapeDtypeStruct((t, 3 * d), BF16), jax.ShapeDtypeStruct(conv_w.shape, F32)],
        compiler_params=_cparams("arbitrary"),
    )(dy, dy, u, u, u, u, u, u, conv_w, *after)


def _cast_into_full(name, mat, w, layer, chip, after=()):
    tr = _pick(mat.sr, (512, 256, 128, 64, 32, 16))
    per = mat.sr // tr

    def body(chip_ref, w_ref, *rest):
        o_ref = rest[-1]
        o_ref[...] = w_ref[...].astype(BF16)

    if mat.kind == "col":
        o_spec = pl.BlockSpec((tr, mat.sc), lambda i, chip_ref: (i, chip_ref[0]))
    else:
        o_spec = pl.BlockSpec((tr, mat.sc), lambda i, chip_ref: (chip_ref[0] * per + i, 0))
    return pl.pallas_call(
        body, name=name,
        grid_spec=pltpu.PrefetchScalarGridSpec(
            num_scalar_prefetch=1, grid=(per,),
            in_specs=[pl.BlockSpec((None, tr, mat.sc), lambda i, chip_ref: (layer, i, 0))] + [_ANY] * len(after),
            out_specs=o_spec),
        out_shape=jax.ShapeDtypeStruct(mat.full_shape, BF16),
        compiler_params=_cparams("parallel"),
    )(chip, w, *after)


def _adamw(name, w, g, m, v, copy_grad=False, after=()):
    r, c = w.shape
    tr = _pick(r, (256, 128, 64, 32, 16, 8)) if c > 1024 else _pick(r, (512, 256, 128, 64, 32, 16, 8))
    bc1 = 1.0 - ADAM_B1 ** ADAM_STEP
    bc2 = 1.0 - ADAM_B2 ** ADAM_STEP

    def body(w_ref, g_ref, m_ref, v_ref, d_ref, mo_ref, vo_ref, *rest):
        gv = g_ref[...]
        mn = ADAM_B1 * m_ref[...] + (1.0 - ADAM_B1) * gv
        vn = ADAM_B2 * v_ref[...] + (1.0 - ADAM_B2) * (gv * gv)
        mo_ref[...] = mn
        vo_ref[...] = vn
        d_ref[...] = -ADAM_LR * ((mn * (1.0 / bc1)) / (jnp.sqrt(vn * (1.0 / bc2)) + ADAM_EPS) + ADAM_WD * w_ref[...])

        if copy_grad:
            rest[0][...] = gv

    spec = pl.BlockSpec((tr, c), lambda i: (i, 0))
    n_out = 4 if copy_grad else 3
    return pl.pallas_call(_behind(after, body, 4), name=name, grid=(r // tr,),
                          in_specs=[spec] * 4 + [_ANY] * len(after), out_specs=[spec] * n_out,
                          out_shape=[jax.ShapeDtypeStruct((r, c), F32)] * n_out,
                          compiler_params=_cparams("parallel"))(w, g, m, v, *after)


def _aligned(offset, multiple):
    return offset if isinstance(offset, int) else pl.multiple_of(offset, multiple)


class _Mat:
    def __init__(self, kind, shard_shape):
        self.kind = kind
        self.sr, self.sc = shard_shape
        self.full_shape = (self.sr, self.sc * N_CHIPS) if kind == "col" else (self.sr * N_CHIPS, self.sc)
        self.pr, self.pc = self.sr // 2, self.sc

    def piece(self, ref, k, h):
        if self.kind == "col":
            return ref.at[pl.ds(_aligned(h * self.pr, 16), self.pr), pl.ds(_aligned(k * self.sc, LANES), self.sc)]
        return ref.at[pl.ds(_aligned(k * self.sr + h * self.pr, 16), self.pr), :]

    def shard(self, ref, k):
        if self.kind == "col":
            return ref.at[:, pl.ds(_aligned(k * self.sc, LANES), self.sc)]
        return ref.at[pl.ds(_aligned(k * self.sr, 16), self.sr), :]

    def half(self, ref, h):
        return ref.at[pl.ds(_aligned(h * self.pr, 16), self.pr), :]


def _place():
    x, y, c = lax.axis_index("x"), lax.axis_index("y"), lax.axis_index("c")
    others = [(1 - x, y), (x, 1 - y), (1 - x, 1 - y)]
    return x, y, c, others


_HBM = pl.BlockSpec(memory_space=pltpu.HBM)
_SEM = pl.BlockSpec(memory_space=pltpu.SEMAPHORE)
_TOKEN = jax.ShapeDtypeStruct((8, LANES), F32)
_TOKEN_SPEC = pl.BlockSpec(memory_space=pltpu.VMEM)


def _split_params():
    return pltpu.CompilerParams(has_side_effects=pltpu.SideEffectType.DATAFLOW_SIDE_EFFECTING)


def _in_hbm(a):
    return pltpu.with_memory_space_constraint(a, pltpu.HBM)


def _copy_to(src, dst, send_sem, recv_sem, to):
    return pltpu.make_async_remote_copy(src_ref=src, dst_ref=dst, send_sem=send_sem, recv_sem=recv_sem,
                                        device_id=to, device_id_type=MESH_ID)


def _place_small(packed, chip):
    rows, cols = packed.shape

    def body(chip_ref, p_ref, o_ref):
        o_ref[...] = p_ref[...]

    return pl.pallas_call(
        body, name="place_small",
        grid_spec=pltpu.PrefetchScalarGridSpec(
            num_scalar_prefetch=1, grid=(1,),
            in_specs=[pl.BlockSpec((rows, cols), lambda i, chip_ref: (0, 0))],
            out_specs=pl.BlockSpec((None, rows, cols), lambda i, chip_ref: (chip_ref[0], 0, 0))),
        out_shape=jax.ShapeDtypeStruct((N_CHIPS, rows, cols), F32),
        compiler_params=_cparams("arbitrary"),
    )(chip, packed)


def _gather_start(name, gmats, gfulls, small_all=None):
    n = len(gmats)
    n_in = n + (1 if small_all is not None else 0)

    def body(*refs):
        full_refs = refs[:n]
        outs = refs[n_in:]
        send_sem, recv_sem, token = outs[n_in], outs[n_in + 1], outs[n_in + 2]
        x, y, c, others = _place()
        me_k = 2 * x + y
        if small_all is not None:
            mine = refs[n].at[me_k]
            for ox, oy in others:
                _copy_to(mine, mine, send_sem, recv_sem, (ox, oy, c)).start()
        for m in range(n):
            mine = gmats[m].piece(full_refs[m], me_k, c)
            for ox, oy in others:
                _copy_to(mine, mine, send_sem, recv_sem, (ox, oy, c)).start()
        token[...] = jnp.zeros_like(token)

    operands = [_in_hbm(f) for f in gfulls] + ([_in_hbm(small_all)] if small_all is not None else [])
    outs = pl.pallas_call(
        body, name=name,
        in_specs=[_HBM] * n_in,
        out_specs=[_HBM] * n_in + [_SEM, _SEM, _TOKEN_SPEC],
        out_shape=[pltpu.HBM(a.shape, a.dtype) for a in operands] + [pltpu.SemaphoreType.DMA(())] * 2 + [_TOKEN],
        input_output_aliases={m: m for m in range(n_in)},
        compiler_params=_split_params(),
    )(*operands)
    return list(outs[:n]), (outs[n] if small_all is not None else None), (outs[n_in], outs[n_in + 1]), outs[n_in + 2]


def _gather_pass(name, gmats, gfulls, small_all, sems, after):
    k = len(gmats)
    n_buf = k + (1 if small_all is not None else 0)

    def body(*refs):
        bufs = refs[:n_buf]
        send_sem, recv_sem = refs[n_buf], refs[n_buf + 1]
        outs = refs[n_buf + 3:]
        fsend, frecv, token = outs[n_buf], outs[n_buf + 1], outs[n_buf + 2]
        x, y, c, others = _place()
        me_k = 2 * x + y
        sibling = (x, y, 1 - c)
        for m in range(k):
            for ox, oy in others:
                got = gmats[m].piece(bufs[m], 2 * ox + oy, c)
                _copy_to(got, got, send_sem, recv_sem, sibling).wait_recv()
        if small_all is not None:
            for ox, oy in others:
                got = bufs[k].at[2 * ox + oy]
                _copy_to(got, got, send_sem, recv_sem, sibling).wait_recv()
        for m in range(k):
            mine = gmats[m].piece(bufs[m], me_k, c)
            for _ in others:
                _copy_to(mine, mine, send_sem, recv_sem, sibling).wait_send()
        if small_all is not None:
            for _ in others:
                _copy_to(bufs[k].at[me_k], bufs[k].at[me_k], send_sem, recv_sem, sibling).wait_send()
        for m in range(k):
            for ox, oy in others:
                got = gmats[m].piece(bufs[m], 2 * ox + oy, c)
                _copy_to(got, got, fsend, frecv, sibling).start()
        token[...] = jnp.zeros_like(token)

    operands = [_in_hbm(f) for f in gfulls] + ([_in_hbm(small_all)] if small_all is not None else [])
    outs = pl.pallas_call(
        body, name=name,
        in_specs=[_HBM] * n_buf + [_SEM, _SEM, _ANY],
        out_specs=[_HBM] * n_buf + [_SEM, _SEM, _TOKEN_SPEC],
        out_shape=[pltpu.HBM(a.shape, a.dtype) for a in operands] + [pltpu.SemaphoreType.DMA(())] * 2 + [_TOKEN],
        input_output_aliases={i: i for i in range(n_buf)},
        compiler_params=_split_params(),
    )(*operands, sems[0], sems[1], after)
    return list(outs[:n_buf]), (outs[n_buf], outs[n_buf + 1]), outs[n_buf + 2]


def _gather_done(name, gmats, gfulls, sems, after):
    k = len(gmats)

    def body(*refs):
        bufs = refs[:k]
        send_sem, recv_sem = refs[k], refs[k + 1]
        x, y, c, others = _place()
        sibling = (x, y, 1 - c)
        for m in range(k):
            for ox, oy in others:
                got = gmats[m].piece(bufs[m], 2 * ox + oy, 1 - c)
                _copy_to(got, got, send_sem, recv_sem, sibling).wait_recv()
        for m in range(k):
            for ox, oy in others:
                sent = gmats[m].piece(bufs[m], 2 * ox + oy, c)
                _copy_to(sent, sent, send_sem, recv_sem, sibling).wait_send()

    outs = pl.pallas_call(
        body, name=name,
        in_specs=[_HBM] * k + [_SEM, _SEM, _ANY], out_specs=[_HBM] * k,
        out_shape=[pltpu.HBM(a.shape, a.dtype) for a in gfulls],
        input_output_aliases={i: i for i in range(k)},
        compiler_params=_split_params(),
    )(*[_in_hbm(f) for f in gfulls], sems[0], sems[1], after)
    return list(outs)


_FLIPS = [(fx, fy, fc) for fx in (0, 1) for fy in (0, 1) for fc in (0, 1) if (fx, fy, fc) != (0, 0, 0)]


def _small_start(packed, after):
    def body(small_ref, after_ref, small_thru, land_ref, send_sem, recv_sem, token):
        x, y, c, _ = _place()
        me = 4 * x + 2 * y + c
        for fx, fy, fc in _FLIPS:
            _copy_to(small_ref, land_ref.at[me], send_sem, recv_sem, (x ^ fx, y ^ fy, c ^ fc)).start()
        token[...] = jnp.zeros_like(token)

    outs = pl.pallas_call(
        body, name="small_grads_start",
        in_specs=[_HBM, _ANY], out_specs=[_HBM, _HBM, _SEM, _SEM, _TOKEN_SPEC],
        out_shape=[pltpu.HBM(packed.shape, F32), pltpu.HBM((N_DEV,) + packed.shape, F32)]
        + [pltpu.SemaphoreType.DMA(())] * 2 + [_TOKEN],
        input_output_aliases={0: 0},
        compiler_params=_split_params(),
    )(_in_hbm(packed), after)
    return outs[0], outs[1], (outs[2], outs[3]), outs[4]


def _small_wait(packed, landed, sems, after):
    def body(small_ref, land_ref, send_sem, recv_sem, after_ref, small_thru, land_thru):
        x, y, c, _ = _place()
        for fx, fy, fc in _FLIPS:
            got = land_ref.at[4 * (x ^ fx) + 2 * (y ^ fy) + (c ^ fc)]
            _copy_to(got, got, send_sem, recv_sem, (x, y, 1 - c)).wait_recv()
        for _ in _FLIPS:
            _copy_to(small_ref, small_ref, send_sem, recv_sem, (x, y, 1 - c)).wait_send()

    outs = pl.pallas_call(
        body, name="small_grads_wait",
        in_specs=[_HBM, _HBM, _SEM, _SEM, _ANY], out_specs=[_HBM, _HBM],
        out_shape=[pltpu.HBM(packed.shape, F32), pltpu.HBM(landed.shape, F32)],
        input_output_aliases={0: 0, 1: 1},
        compiler_params=_split_params(),
    )(_in_hbm(packed), _in_hbm(landed), sems[0], sems[1], after)
    return outs[0], outs[1]


def _exchange_start(name, mats, grads):
    n = len(mats)

    def body(*refs):
        g_refs = refs[:n]
        outs = refs[n:]
        land_refs = outs[n:2 * n]
        send_sem, recv_sem, token = outs[2 * n], outs[2 * n + 1], outs[2 * n + 2]
        x, y, c, _ = _place()
        for m in range(n):
            for k in range(N_CHIPS):
                _copy_to(mats[m].piece(g_refs[m], k, 1 - c), land_refs[m].at[k], send_sem, recv_sem, (x, y, 1 - c)).start()
        token[...] = jnp.zeros_like(token)

    outs = pl.pallas_call(
        body, name=name,
        in_specs=[_HBM] * n,
        out_specs=[_HBM] * (2 * n) + [_SEM, _SEM, _TOKEN_SPEC],
        out_shape=[pltpu.HBM(mt.full_shape, BF16) for mt in mats]
        + [pltpu.HBM((N_CHIPS, mt.pr, mt.pc), BF16) for mt in mats] + [pltpu.SemaphoreType.DMA(())] * 2 + [_TOKEN],
        input_output_aliases={m: m for m in range(n)},
        compiler_params=_split_params(),
    )(*[_in_hbm(g) for g in grads])
    return list(outs[:n]), list(outs[n:2 * n]), (outs[2 * n], outs[2 * n + 1]), outs[2 * n + 2]


def _exchange_wait(name, mats, grads, landed, sems, after):
    n = len(mats)

    def body(*refs):
        g_refs, land_refs = refs[:n], refs[n:2 * n]
        send_sem, recv_sem = refs[2 * n], refs[2 * n + 1]
        x, y, c, _ = _place()
        for m in range(n):
            for k in range(N_CHIPS):
                got = land_refs[m].at[k]
                _copy_to(got, got, send_sem, recv_sem, (x, y, 1 - c)).wait_recv()
        for m in range(n):
            for k in range(N_CHIPS):
                sent = mats[m].piece(g_refs[m], k, 1 - c)
                _copy_to(sent, sent, send_sem, recv_sem, (x, y, 1 - c)).wait_send()

    outs = pl.pallas_call(
        body, name=name,
        in_specs=[_HBM] * (2 * n) + [_SEM, _SEM, _ANY], out_specs=[_HBM] * (2 * n),
        out_shape=[pltpu.HBM(a.shape, a.dtype) for a in list(grads) + list(landed)],
        input_output_aliases={i: i for i in range(2 * n)},
        compiler_params=_split_params(),
    )(*[_in_hbm(a) for a in list(grads) + list(landed)], sems[0], sems[1], after)
    return list(outs[:n]), list(outs[n:])


def _add_halves(name, mat, grad, landed, core):
    tr = _pick(mat.pr, (1024, 704, 512, 352, 256, 128, 64, 32, 16))
    per = mat.pr // tr

    def body(core_ref, g_ref, l_ref, o_ref):
        o_ref[...] = (g_ref[...].astype(F32) + l_ref[...].astype(F32)).astype(BF16)

    if mat.kind == "col":
        g_spec = pl.BlockSpec((tr, mat.pc), lambda k, r, core_ref: (core_ref[0] * per + r, k))
    else:
        g_spec = pl.BlockSpec((tr, mat.pc), lambda k, r, core_ref: ((2 * k + core_ref[0]) * per + r, 0))
    p_spec = pl.BlockSpec((None, tr, mat.pc), lambda k, r, core_ref: (k, r, 0))
    return pl.pallas_call(
        body, name=name,
        grid_spec=pltpu.PrefetchScalarGridSpec(num_scalar_prefetch=1, grid=(N_CHIPS, per),
                                               in_specs=[g_spec, p_spec], out_specs=p_spec),
        out_shape=jax.ShapeDtypeStruct((N_CHIPS, mat.pr, mat.pc), BF16),
        compiler_params=_cparams("parallel", "parallel"),
    )(core, grad, landed)


def _scatter_start(name, mats, partials):
    n = len(mats)

    def body(*refs):
        p_refs = refs[:n]
        outs = refs[n:]
        land_refs = outs[n:2 * n]
        send_sem, recv_sem, token = outs[2 * n], outs[2 * n + 1], outs[2 * n + 2]
        x, y, c, others = _place()
        me_k = 2 * x + y
        for m in range(n):
            for ox, oy in others:
                _copy_to(p_refs[m].at[2 * ox + oy], land_refs[m].at[me_k], send_sem, recv_sem, (ox, oy, c)).start()
        token[...] = jnp.zeros_like(token)

    piece_shapes = [pltpu.HBM((N_CHIPS, mt.pr, mt.pc), BF16) for mt in mats]
    outs = pl.pallas_call(
        body, name=name,
        in_specs=[_HBM] * n,
        out_specs=[_HBM] * (2 * n) + [_SEM, _SEM, _TOKEN_SPEC],
        out_shape=piece_shapes + piece_shapes + [pltpu.SemaphoreType.DMA(())] * 2 + [_TOKEN],
        input_output_aliases={m: m for m in range(n)},
        compiler_params=_split_params(),
    )(*[_in_hbm(p) for p in partials])
    return list(outs[:n]), list(outs[n:2 * n]), (outs[2 * n], outs[2 * n + 1]), outs[2 * n + 2]


def _scatter_wait(name, mats, partials, landed, sems, after):
    n = len(mats)

    def body(*refs):
        p_refs, land_refs = refs[:n], refs[n:2 * n]
        send_sem, recv_sem = refs[2 * n], refs[2 * n + 1]
        x, y, c, others = _place()
        for m in range(n):
            for ox, oy in others:
                got = land_refs[m].at[2 * ox + oy]
                _copy_to(got, got, send_sem, recv_sem, (ox, oy, c)).wait_recv()
        for m in range(n):
            for ox, oy in others:
                sent = p_refs[m].at[2 * ox + oy]
                _copy_to(sent, sent, send_sem, recv_sem, (ox, oy, c)).wait_send()

    outs = pl.pallas_call(
        body, name=name,
        in_specs=[_HBM] * (2 * n) + [_SEM, _SEM, _ANY], out_specs=[_HBM] * (2 * n),
        out_shape=[pltpu.HBM(a.shape, a.dtype) for a in list(partials) + list(landed)],
        input_output_aliases={i: i for i in range(2 * n)},
        compiler_params=_split_params(),
    )(*[_in_hbm(a) for a in list(partials) + list(landed)], sems[0], sems[1], after)
    return list(outs[:n]), list(outs[n:])


def _sum_chips(name, mat, partial, landed, slots, layer=None, stack=None, n_layers=1):
    tr = _pick(mat.pr, (1024, 704, 512, 352, 256, 128, 64, 32, 16))
    per = mat.pr // tr

    def body(slots_ref, own_ref, a_ref, b_ref, c_ref, *rest):
        o_ref = rest[-1]
        o_ref[...] = ((own_ref[...].astype(F32) + a_ref[...].astype(F32)) + b_ref[...].astype(F32)) + c_ref[...].astype(F32)

    def slot_spec(which):
        return pl.BlockSpec((None, tr, mat.pc), lambda r, slots_ref: (slots_ref[which], r, 0))

    in_specs = [slot_spec(0), slot_spec(1), slot_spec(2), slot_spec(3)]
    operands = [slots, partial, landed, landed, landed]
    aliases = {}
    if layer is None:
        o_spec = pl.BlockSpec((tr, mat.pc), lambda r, slots_ref: (slots_ref[4] * per + r, 0))
        out_shape = jax.ShapeDtypeStruct((mat.sr, mat.sc), F32)
    else:
        o_spec = pl.BlockSpec((None, tr, mat.pc), lambda r, slots_ref: (layer, slots_ref[4] * per + r, 0))
        out_shape = jax.ShapeDtypeStruct((n_layers, mat.sr, mat.sc), F32)
        if stack is not None:
            in_specs.append(_ANY)
            operands.append(stack)
            aliases = {len(operands) - 1: 0}
    return pl.pallas_call(
        body, name=name,
        grid_spec=pltpu.PrefetchScalarGridSpec(num_scalar_prefetch=1, grid=(per,), in_specs=in_specs, out_specs=o_spec),
        out_shape=out_shape, input_output_aliases=aliases,
        compiler_params=_cparams("parallel"),
    )(*operands)


def _share_pieces(name, mats, shards, groups):
    n = len(mats)
    n_out = len(groups)

    def body(*refs):
        out_refs = refs[n_out:2 * n_out]
        send_sems, recv_sems = refs[2 * n_out:]
        x, y, c, _ = _place()
        sibling = (x, y, 1 - c)
        sent, waits = [], []
        for o, members in enumerate(groups):
            for l, m in enumerate(members):
                dst = out_refs[o].at[l] if len(members) > 1 else out_refs[o]
                mine = mats[m].half(dst, c)
                sent.append(pltpu.make_async_remote_copy(src_ref=mine, dst_ref=mine, send_sem=send_sems.at[m],
                                                         recv_sem=recv_sems.at[m], device_id=sibling, device_id_type=MESH_ID))
                theirs = mats[m].half(dst, 1 - c)
                waits.append(pltpu.make_async_remote_copy(src_ref=theirs, dst_ref=theirs, send_sem=send_sems.at[m],
                                                          recv_sem=recv_sems.at[m], device_id=sibling, device_id_type=MESH_ID))
        for cp in sent:
            cp.start()
        for cp in waits:
            cp.wait_recv()
        for cp in sent:
            cp.wait_send()

    return pl.pallas_call(
        body, name=name,
        in_specs=[_ANY] * n_out, out_specs=[_ANY] * n_out,
        out_shape=[jax.ShapeDtypeStruct(s.shape, F32) for s in shards],
        input_output_aliases={o: o for o in range(n_out)},
        scratch_shapes=[pltpu.SemaphoreType.DMA((n,)), pltpu.SemaphoreType.DMA((n,))],
    )(*shards)


def _share_start(name, mats, shards, items):
    n_out = len(shards)

    def body(*refs):
        out_refs = refs[n_out:2 * n_out]
        send_sem, recv_sem, token = refs[2 * n_out], refs[2 * n_out + 1], refs[2 * n_out + 2]
        x, y, c, _ = _place()
        for o, members in enumerate(items):
            for layer, m in members:
                dst = out_refs[o] if layer is None else out_refs[o].at[layer]
                mine = mats[m].half(dst, c)
                _copy_to(mine, mine, send_sem, recv_sem, (x, y, 1 - c)).start()
        token[...] = jnp.zeros_like(token)

    outs = pl.pallas_call(
        body, name=name,
        in_specs=[_HBM] * n_out, out_specs=[_HBM] * n_out + [_SEM, _SEM, _TOKEN_SPEC],
        out_shape=[pltpu.HBM(s.shape, F32) for s in shards] + [pltpu.SemaphoreType.DMA(())] * 2 + [_TOKEN],
        input_output_aliases={o: o for o in range(n_out)},
        compiler_params=_split_params(),
    )(*[_in_hbm(s) for s in shards])
    return list(outs[:n_out]), (outs[n_out], outs[n_out + 1]), outs[n_out + 2]


def _share_wait(name, mats, shards, items, sems, after):
    n_out = len(shards)

    def body(*refs):
        bufs = refs[:n_out]
        send_sem, recv_sem = refs[n_out], refs[n_out + 1]
        x, y, c, _ = _place()
        for o, members in enumerate(items):
            for layer, m in members:
                dst = bufs[o] if layer is None else bufs[o].at[layer]
                theirs = mats[m].half(dst, 1 - c)
                _copy_to(theirs, theirs, send_sem, recv_sem, (x, y, 1 - c)).wait_recv()
        for o, members in enumerate(items):
            for layer, m in members:
                dst = bufs[o] if layer is None else bufs[o].at[layer]
                mine = mats[m].half(dst, c)
                _copy_to(mine, mine, send_sem, recv_sem, (x, y, 1 - c)).wait_send()

    outs = pl.pallas_call(
        body, name=name,
        in_specs=[_HBM] * n_out + [_SEM, _SEM, _ANY], out_specs=[_HBM] * n_out,
        out_shape=[pltpu.HBM(s.shape, F32) for s in shards],
        input_output_aliases={o: o for o in range(n_out)},
        compiler_params=_split_params(),
    )(*[_in_hbm(s) for s in shards], sems[0], sems[1], after)
    return list(outs)


def _sum_devices(stacked):
    nd, r, c = stacked.shape

    def body(s_ref, o_ref):
        s = s_ref[0]
        for k in range(1, nd):
            s = s + s_ref[k]
        o_ref[...] = s

    return pl.pallas_call(
        body, name="sum_small_grads", grid=(1,),
        in_specs=[pl.BlockSpec((nd, r, c), lambda i: (0, 0, 0))],
        out_specs=pl.BlockSpec((r, c), lambda i: (0, 0)),
        out_shape=jax.ShapeDtypeStruct((r, c), F32),
        compiler_params=_cparams("arbitrary"),
    )(stacked)


def _pack(arrs):
    flat = jnp.concatenate([a.reshape(-1) for a in arrs])
    rows = -(-flat.shape[0] // (8 * LANES)) * 8
    return jnp.pad(flat, (0, rows * LANES - flat.shape[0])).reshape(rows, LANES)


def _unpack(packed, shapes):
    flat = packed.reshape(-1)
    out, at = [], 0
    for s in shapes:
        size = 1
        for dim in s:
            size *= dim
        out.append(flat[at:at + size].reshape(s))
        at += size
    return out


def kernel(x, mix_norm_e, w_in_e, conv_w_e, conv_b_e, ln_g_e, ln_b_e, w_pool_e, pool_scale_e, w_out_e, mix_norm_o, w_in_o, conv_w_o, w_out_o, ffn_norm, w_gate, w_up, w_down, final_norm, loss_target, m_mix_norm_e, m_w_in_e, m_conv_w_e, m_conv_b_e, m_ln_g_e, m_ln_b_e, m_w_pool_e, m_pool_scale_e, m_w_out_e, m_mix_norm_o, m_w_in_o, m_conv_w_o, m_w_out_o, m_ffn_norm, m_w_gate, m_w_up, m_w_down, m_final_norm, v_mix_norm_e, v_w_in_e, v_conv_w_e, v_conv_b_e, v_ln_g_e, v_ln_b_e, v_w_pool_e, v_pool_scale_e, v_w_out_e, v_mix_norm_o, v_w_in_o, v_conv_w_o, v_w_out_o, v_ffn_norm, v_w_gate, v_w_up, v_w_down, v_final_norm):
    bsz, seq_len, d = x.shape
    t = bsz * seq_len
    depth = ffn_norm.shape[0]
    assert depth == 2 and conv_b_e.shape[1] == pool_scale_e.shape[1]
    ts = _pick(seq_len, (256, 128, 64, 32))
    me_k = 2 * lax.axis_index("x") + lax.axis_index("y")
    core = lax.axis_index("c").astype(jnp.int32).reshape(1)

    mat_src = [("col", w_in_e, 0), ("row", w_out_e, 0), ("col", w_gate, 0), ("col", w_up, 0), ("row", w_down, 0),
               ("col", w_in_o, 0), ("row", w_out_o, 0), ("col", w_gate, 1), ("col", w_up, 1), ("row", w_down, 1)]
    mats = [_Mat(kind, w.shape[1:]) for kind, w, _ in mat_src]
    n_pool = w_pool_e.shape[1]
    pool_mats = tuple(range(len(mats), len(mats) + n_pool))
    mats = mats + [_Mat("row", w_pool_e.shape[2:])] * n_pool
    chip = me_k.astype(jnp.int32).reshape(1)
    small_shards = [conv_w_e[0], w_pool_e[0], mix_norm_o, conv_w_o[0]]
    packed_small = _pack(small_shards)

    chain = [()]

    def seq(fn, *args, **kw):
        out = fn(*args, after=chain[0], **kw)
        chain[0] = (out[0] if isinstance(out, (list, tuple)) else out,)
        return out

    def mm(*args, **kw):
        return seq(_mm, *args, **kw)

    gather_groups = [(0,), (1,), (2, 3), (4,), (5, 6), (7, 8), (9,)]
    fulls, gather_sems, small_all = [None] * len(mats), [], None
    for g, ms in enumerate(gather_groups):
        own16 = [seq(_cast_into_full, "cast_w%d" % m, mats[m], mat_src[m][1], mat_src[m][2], chip) for m in ms]
        sent, landing, sems, token = _gather_start("gather_start%d" % g, [mats[m] for m in ms], own16,
                                                   _place_small(packed_small, chip) if g == 0 else None)
        chain[0] = (token,)
        for m, f in zip(ms, sent):
            fulls[m] = f
        gather_sems.append(sems)
        if g == 0:
            small_all = landing

    passed = {}

    def gather_pass(g):
        ms = gather_groups[g]
        bufs, pass_sems, token = _gather_pass("gather_pass%d" % g, [mats[m] for m in ms], [fulls[m] for m in ms],
                                              small_all if g == 0 else None, gather_sems[g], chain[0][0])
        chain[0] = (token,)
        passed[g] = (bufs, pass_sems)

    def gather_done(g):
        ms = gather_groups[g]
        bufs, pass_sems = passed[g]
        done = _gather_done("gather_done%d" % g, [mats[m] for m in ms], bufs[:len(ms)], pass_sems, chain[0][0])
        chain[0] = (done[0],)
        return done + bufs[len(ms):]

    h0 = x.reshape(t, d)
    target = loss_target.reshape(t, d)
    gather_pass(0)
    n1 = seq(_rms_fwd, "mix0_norm", h0, mix_norm_e)
    W_in_e, small_all = gather_done(0)
    per_chip = [_unpack(small_all[k], [s.shape for s in small_shards]) for k in range(N_CHIPS)]
    conv_w_e_f = jnp.concatenate([p[0] for p in per_chip], axis=1)
    w_pool_f = jnp.concatenate([p[1] for p in per_chip], axis=1)
    mix_norm_o_f = jnp.concatenate([p[2] for p in per_chip], axis=1)
    conv_w_o_f = jnp.concatenate([p[3] for p in per_chip], axis=1)
    W_gate, W_up, W_down = [None, None], [None, None], [None, None]

    (u_e,) = mm("mix0_in", [(n1, W_in_e)], "nn", [F32], _ep_store, tm=1024, tn=1024, tk=2048)
    gather_pass(1)
    a2, cat = seq(_mixer_e_fwd, u_e, conv_w_e_f, conv_b_e, ln_g_e, ln_b_e, w_pool_f, pool_scale_e, seq_len, ts)
    (W_out_e,) = gather_done(1)
    (h1,) = mm("mix0_out", [(cat, W_out_e)], "nn", [F32], _ep_residual, extras=(h0,), tm=1024, tn=1024, tk=2048)
    gather_pass(2)
    n2 = seq(_rms_fwd, "ffn0_norm", h1, ffn_norm[0:1])
    W_gate[0], W_up[0] = gather_done(2)
    gt0, up0, act0 = mm("ffn0_gate_up", [(n2, W_gate[0]), (n2, W_up[0])], "nn", [BF16] * 3, _ep_swiglu,
                        acc_of=(0, 1), tm=1024, tn=512, tk=2048)
    gather_pass(3)
    (W_down[0],) = gather_done(3)
    gather_pass(4)
    h2, n3 = seq(_mm_rows, "ffn0_down", [(act0, W_down[0])], "nn", [F32, BF16], _ep_rows_residual_norm, extras=(h1,),
                 vecs=(mix_norm_o_f,), tm=256, tk=act0.shape[1], chunk_dots=False)
    W_in_o, W_out_o = gather_done(4)
    (u_o,) = mm("mix1_in", [(n3, W_in_o)], "nn", [BF16], _ep_store, tm=1024, tn=1024, tk=2048)
    gather_pass(5)
    y_o = seq(_mixer_o_fwd, u_o, conv_w_o_f, seq_len, ts)
    h3, n4 = seq(_mm_rows, "mix1_out", [(y_o, W_out_o)], "nn", [F32, BF16], _ep_rows_residual_norm, extras=(h2,),
                 vecs=(ffn_norm[1:2],), tm=512, tk=2048)
    gather_pass(6)
    W_gate[1], W_up[1] = gather_done(5)
    gt1, up1, act1 = mm("ffn1_gate_up", [(n4, W_gate[1]), (n4, W_up[1])], "nn", [BF16] * 3, _ep_swiglu,
                        acc_of=(0, 1), tm=1024, tn=512, tk=2048)
    (W_down[1],) = gather_done(6)
    dh4, dh4b, loss_cols, d_final_norm = seq(_mm_rows, "ffn1_down", [(act1, W_down[1])], "nn", [F32, BF16],
                                             _ep_rows_loss_head, extras=(h3, target), vecs=(final_norm.reshape(1, d),),
                                             n_sums=2, tm=256, tk=act1.shape[1], chunk_dots=False)
    loss = lax.psum(jnp.sum(loss_cols), AXES)

    in_flight = {}
    partials, scattered = [None] * len(mats), [None] * len(mats)

    def reduce_begin(tag, ms, grads):
        gm = [mats[m] for m in ms]
        grads, landed, sems, token = _exchange_start("exchange_start_" + tag, gm, grads)
        chain[0] = (token,)
        in_flight[tag] = (ms, gm, grads, landed, sems)

    def reduce_advance(tag):
        ms, gm, grads, landed, sems = in_flight[tag]
        grads, landed = _exchange_wait("exchange_wait_" + tag, gm, grads, landed, sems, chain[0][0])
        parts = [_add_halves("add_halves%d" % m, mats[m], g, l, core) for m, g, l in zip(ms, grads, landed)]
        parts, lands, sems, token = _scatter_start("scatter_start_" + tag, gm, parts)
        chain[0] = (token,)
        in_flight[tag] = (ms, gm, parts, lands, sems)

    def reduce_finish(tag):
        ms, gm, parts, lands, sems = in_flight[tag]
        parts, lands = _scatter_wait("scatter_wait_" + tag, gm, parts, lands, sems, chain[0][0])
        chain[0] = (lands[0],)
        for m, p, l in zip(ms, parts, lands):
            partials[m], scattered[m] = p, l

    xi, yi, ci = lax.axis_index("x"), lax.axis_index("y"), lax.axis_index("c")
    slots = jnp.stack([me_k, 2 * (1 - xi) + yi, 2 * xi + (1 - yi), 2 * (1 - xi) + (1 - yi), ci]).astype(jnp.int32)
    shards = {}

    def sum_into(name, m, layer=None, n_layers=1):
        if layer is None:
            shards[name] = _sum_chips("sum_chips%d" % m, mats[m], partials[m], scattered[m], slots)
        else:
            shards[name] = _sum_chips("sum_chips%d" % m, mats[m], partials[m], scattered[m], slots, layer=layer,
                                      stack=shards.get(name), n_layers=n_layers)

    sharing = {}

    def share_begin(tag, names, items):
        arrays, sems, token = _share_start("share_start_" + tag, mats, [shards[nm] for nm in names], items)
        chain[0] = (token,)
        sharing[tag] = (names, items, arrays, sems)

    def share_end(tag):
        names, items, arrays, sems = sharing[tag]
        arrays = _share_wait("share_wait_" + tag, mats, arrays, items, sems, chain[0][0])
        chain[0] = (arrays[0],)
        for nm, a in zip(names, arrays):
            shards[nm] = a

    def ffn_bwd(l, dhb, n, gt, up, act, mid=None):
        dgt, dup = mm("ffn%d_dact" % l, [(dhb, W_down[l])], "nt", [BF16, BF16], _ep_swiglu_bwd, extras=(gt, up),
                      tm=2048, tn=512, tk=2048, row_chunk=512)
        (dW_down,) = mm("ffn%d_dw_down" % l, [(act, dhb)], "tn", [BF16], _ep_store, tm=512, tn=1024, tk=4096)
        if mid is not None:
            mid()
        (dn,) = mm("ffn%d_dn" % l, [(dgt, W_gate[l]), (dup, W_up[l])], "nt", [BF16], _ep_store,
                   tm=512, tn=2048, tk=1408)
        (dW_gate,) = mm("ffn%d_dw_gate" % l, [(n, dgt)], "tn", [BF16], _ep_store, tm=2048, tn=512, tk=4096)
        (dW_up,) = mm("ffn%d_dw_up" % l, [(n, dup)], "tn", [BF16], _ep_store, tm=2048, tn=512, tk=4096)
        return dn, dW_gate, dW_up, dW_down

    dn4, dW_gate1, dW_up1, dW_down1 = ffn_bwd(1, dh4b, n4, gt1, up1, act1)
    reduce_begin("ffn1", (7, 8, 9), [dW_gate1, dW_up1, dW_down1])
    dh3, dh3b, d_ffn_norm1 = seq(_rms_bwd, "ffn1_norm_bwd", dn4, h3, ffn_norm[1:2], dh4)
    (dy_o,) = mm("mix1_dy", [(dh3b, W_out_o)], "nt", [BF16], _ep_store, tm=1024, tn=1024, tk=2048)
    reduce_advance("ffn1")
    (dW_out_o,) = mm("mix1_dw_out", [(y_o, dh3b)], "tn", [BF16], _ep_store, tm=1024, tn=1024, tk=4096)
    du_o, d_conv_w_o = seq(_mixer_o_bwd, dy_o, u_o, conv_w_o_f, seq_len, ts)
    (dW_in_o,) = mm("mix1_dw_in", [(n3, du_o)], "tn", [BF16], _ep_store, tm=1024, tn=1024, tk=4096)
    reduce_begin("mix1", (5, 6), [dW_in_o, dW_out_o])
    dh2, dh2b, d_mix_norm_o = seq(_mm_rows, "mix1_dn", [(du_o, W_in_o)], "nt", [F32, BF16], _ep_rows_norm_bwd,
                                  extras=(h2, dh3), vecs=(mix_norm_o_f,), n_sums=1, tm=256, tk=6144, chunk_dots=False)
    reduce_advance("mix1")

    def finish_layer1():
        reduce_finish("ffn1")
        reduce_finish("mix1")
        sum_into("w_in_o", 5)
        sum_into("w_out_o", 6)
        for nm, m in (("w_gate", 7), ("w_up", 8), ("w_down", 9)):
            sum_into(nm, m, layer=1, n_layers=2)
        share_begin("layer1", ["w_in_o", "w_out_o", "w_gate", "w_up", "w_down"],
                    [[(None, 5)], [(None, 6)], [(1, 7)], [(1, 8)], [(1, 9)]])

    dn2, dW_gate0, dW_up0, dW_down0 = ffn_bwd(0, dh2b, n2, gt0, up0, act0, mid=finish_layer1)
    reduce_begin("ffn0", (2, 3, 4), [dW_gate0, dW_up0, dW_down0])
    dh1, dh1b, d_ffn_norm0 = seq(_rms_bwd, "ffn0_norm_bwd", dn2, h1, ffn_norm[0:1], dh2)
    (dcat,) = mm("mix0_dcat", [(dh1b, W_out_e)], "nt", [F32], _ep_store, tm=1024, tn=1024, tk=2048)
    reduce_advance("ffn0")
    (dW_out_e,) = mm("mix0_dw_out", [(cat, dh1b)], "tn", [BF16], _ep_store, tm=1024, tn=1024, tk=4096)
    da2, d_ln_g, d_ln_b, d_conv_b = seq(_mixer_e_bwd_norm, dcat, a2, ln_g_e, ln_b_e, ts)
    du_e, d_conv_w_e, d_w_pool, d_pool_scale = seq(_mixer_e_bwd_mix, da2, dcat, u_e, conv_w_e_f, w_pool_f, pool_scale_e,
                                                   seq_len, ts)
    (dW_in_e,) = mm("mix0_dw_in", [(n1, du_e)], "tn", [BF16], _ep_store, tm=1024, tn=1024, tk=4096)
    reduce_begin("mix0", (0, 1) + pool_mats, [dW_in_e, dW_out_e] + [d_w_pool[g].astype(BF16) for g in range(n_pool)])
    dx, d_mix_norm_e = seq(_mm_rows, "mix0_dn", [(du_e, W_in_e)], "nt", [F32], _ep_rows_norm_bwd,
                           extras=(h0, dh1), vecs=(mix_norm_e,), n_sums=1, tm=256, tk=3072, chunk_dots=False)
    reduce_advance("mix0")

    d_ffn_norm = jnp.concatenate([d_ffn_norm0, d_ffn_norm1], axis=0)
    small_partials = [d_mix_norm_e, d_conv_w_e, d_conv_b, d_ln_g, d_ln_b, d_pool_scale, d_mix_norm_o, d_conv_w_o,
                      d_ffn_norm, d_final_norm]
    packed_grads, small_stack, small_sems, token = _small_start(_pack(small_partials), chain[0][0])
    chain[0] = (token,)
    share_end("layer1")
    reduce_finish("ffn0")
    for nm, m in (("w_gate", 2), ("w_up", 3), ("w_down", 4)):
        sum_into(nm, m, layer=0, n_layers=2)
    share_begin("layer0", ["w_gate", "w_up", "w_down"], [[(0, 2)], [(0, 3)], [(0, 4)]])
    grad = {"w_in_o": shards["w_in_o"][None], "w_out_o": shards["w_out_o"][None]}
    weights = dict(mix_norm_e=mix_norm_e, w_in_e=w_in_e, conv_w_e=conv_w_e, conv_b_e=conv_b_e, ln_g_e=ln_g_e, ln_b_e=ln_b_e,
                   w_pool_e=w_pool_e, pool_scale_e=pool_scale_e, w_out_e=w_out_e, mix_norm_o=mix_norm_o, w_in_o=w_in_o,
                   conv_w_o=conv_w_o, w_out_o=w_out_o, ffn_norm=ffn_norm, w_gate=w_gate, w_up=w_up, w_down=w_down,
                   final_norm=final_norm)
    mom1 = dict(mix_norm_e=m_mix_norm_e, w_in_e=m_w_in_e, conv_w_e=m_conv_w_e, conv_b_e=m_conv_b_e, ln_g_e=m_ln_g_e,
                ln_b_e=m_ln_b_e, w_pool_e=m_w_pool_e, pool_scale_e=m_pool_scale_e, w_out_e=m_w_out_e, mix_norm_o=m_mix_norm_o,
                w_in_o=m_w_in_o, conv_w_o=m_conv_w_o, w_out_o=m_w_out_o, ffn_norm=m_ffn_norm, w_gate=m_w_gate, w_up=m_w_up,
                w_down=m_w_down, final_norm=m_final_norm)
    mom2 = dict(mix_norm_e=v_mix_norm_e, w_in_e=v_w_in_e, conv_w_e=v_conv_w_e, conv_b_e=v_conv_b_e, ln_g_e=v_ln_g_e,
                ln_b_e=v_ln_b_e, w_pool_e=v_w_pool_e, pool_scale_e=v_pool_scale_e, w_out_e=v_w_out_e, mix_norm_o=v_mix_norm_o,
                w_in_o=v_w_in_o, conv_w_o=v_conv_w_o, w_out_o=v_w_out_o, ffn_norm=v_ffn_norm, w_gate=v_w_gate, w_up=v_w_up,
                w_down=v_w_down, final_norm=v_final_norm)
    names = list(weights)

    big = ("w_in_o", "w_out_o", "w_gate", "w_up", "w_down", "w_in_e", "w_out_e")
    delta, new_m, new_v = {}, {}, {}

    def update(nm):
        shape = weights[nm].shape
        rows = 1
        for dim in shape[:-1]:
            rows *= dim
        as2d = lambda a: a.reshape(rows, shape[-1])
        dl, mn, vn, gc = seq(_adamw, "adamw_" + nm, as2d(weights[nm]), as2d(grad[nm]), as2d(mom1[nm]), as2d(mom2[nm]),
                             copy_grad=True)
        delta[nm], new_m[nm], new_v[nm], grad[nm] = dl.reshape(shape), mn.reshape(shape), vn.reshape(shape), gc.reshape(shape)

    for nm in big[:2]:
        update(nm)
    share_end("layer0")
    for nm in big[2:5]:
        grad[nm] = shards[nm]
        update(nm)
    reduce_finish("mix0")
    sum_into("w_in_e", 0)
    sum_into("w_out_e", 1)
    for layer, m in enumerate(pool_mats):
        sum_into("w_pool_e", m, layer=layer, n_layers=n_pool)
    g_w_in_e, g_w_out_e, g_w_pool = _share_pieces("share_pieces_first", mats,
                                                  [shards["w_in_e"], shards["w_out_e"], shards["w_pool_e"]],
                                                  [(0,), (1,), pool_mats])
    grad["w_in_e"], grad["w_out_e"], grad["w_pool_e"] = g_w_in_e[None], g_w_out_e[None], g_w_pool[None]
    for nm in big[5:]:
        update(nm)

    packed_grads, small_stack = _small_wait(packed_grads, small_stack, small_sems, chain[0][0])
    me_dev = 4 * xi + 2 * yi + ci
    small_stack = jnp.where(lax.broadcasted_iota(jnp.int32, (N_DEV, 1, 1), 0) == me_dev, packed_grads[None], small_stack)
    small_sum = _unpack(_sum_devices(small_stack), [s.shape for s in small_partials])
    (g_mix_norm_e, g_conv_w_e_f, g_conv_b, g_ln_g, g_ln_b, g_pool_scale, g_mix_norm_o_f, g_conv_w_o_f,
     g_ffn_norm, g_final_norm) = small_sum

    def my_shard(full, axis):
        size = full.shape[axis] // N_CHIPS
        return lax.dynamic_slice_in_dim(full, me_k * size, size, axis)

    grad.update({
        "mix_norm_e": g_mix_norm_e, "conv_w_e": my_shard(g_conv_w_e_f, 1)[None], "conv_b_e": g_conv_b,
        "ln_g_e": g_ln_g, "ln_b_e": g_ln_b, "pool_scale_e": g_pool_scale,
        "mix_norm_o": my_shard(g_mix_norm_o_f, 1), "conv_w_o": my_shard(g_conv_w_o_f, 1)[None],
        "ffn_norm": g_ffn_norm, "final_norm": g_final_norm.reshape(final_norm.shape),
    })
    small = [nm for nm in names if nm not in big]
    shapes = [weights[nm].shape for nm in small]
    dl, mn, vn = _adamw("adamw_small", _pack([weights[nm] for nm in small]), _pack([grad[nm] for nm in small]),
                        _pack([mom1[nm] for nm in small]), _pack([mom2[nm] for nm in small]))
    for nm, a, b, c_ in zip(small, _unpack(dl, shapes), _unpack(mn, shapes), _unpack(vn, shapes)):
        delta[nm], new_m[nm], new_v[nm] = a, b, c_

    grad_x = dx.reshape(bsz, seq_len, d)
    return (loss, grad_x, *[grad[nm] for nm in names], *[delta[nm] for nm in names],
            *[new_m[nm] for nm in names], *[new_v[nm] for nm in names])
```
